```python
import math
import jax, jax.numpy as jnp
from jax import lax
import numpy as np

D_MODEL = 2048
BATCH = 2
SEQ = 4096
DEPTH = 1
DEC_BATCH = 8
DEC_SEQ = 64
PAST_LEN = 4096

CHUNK = 64
Q_BLOCK = 128
CONV_CHANNELS = D_MODEL
CONV_WIDTH = 31
N_HEADS = 16
Q_LORA_RANK = 512
KV_LORA_RANK = 512
NOPE_DIM = 128
ROPE_DIM = 64
QK_DIM = NOPE_DIM + ROPE_DIM
V_DIM = 128
ROPE_THETA = 10000.0
N_EXPERTS = 32
TOP_K = 4
D_FF = D_MODEL
SWIGLU_ALPHA = 1.702
SWIGLU_LIMIT = 7.0
MOE_BLOCK = 128
PLE_DIM = 256
EPS = 1e-6
NEG_INF = -1e30
IN_WIDTH = 2 * CONV_CHANNELS + Q_LORA_RANK + KV_LORA_RANK + ROPE_DIM + 2 * D_MODEL

kernel_name = 'streaming_conformer_mla_moe_step'


def rms_norm(x, g):
    xf = x.astype(jnp.float32)
    y = xf * lax.rsqrt(jnp.mean(xf * xf, axis=-1, keepdims=True) + EPS)
    return (y * g.astype(jnp.float32)).astype(x.dtype)


def layer_norm(x, g, b):
    xf = x.astype(jnp.float32)
    xc = xf - jnp.mean(xf, axis=-1, keepdims=True)
    var = jnp.mean(xc * xc, axis=-1, keepdims=True)
    return (xc * lax.rsqrt(var + EPS) * g.astype(jnp.float32) + b.astype(jnp.float32)).astype(x.dtype)


def rope(x, pos):
    half = ROPE_DIM // 2
    inv_freq = ROPE_THETA ** (-jnp.arange(half, dtype=jnp.float32) / half)
    ang = pos.astype(jnp.float32)[:, None] * inv_freq[None, :]
    cos = jnp.cos(ang)[None, :, None, :]
    sin = jnp.sin(ang)[None, :, None, :]
    xf = x.astype(jnp.float32)
    x1, x2 = xf[..., :half], xf[..., half:]
    return jnp.concatenate([x1 * cos - x2 * sin, x2 * cos + x1 * sin], axis=-1).astype(x.dtype)


def chunk_causal_attention(q, k, v, q_pos, k_pos):
    B, Tq = q.shape[0], q.shape[1]
    blk = Q_BLOCK if Tq % Q_BLOCK == 0 else Tq
    nb = Tq // blk
    qb = q.reshape(B, nb, blk, N_HEADS, QK_DIM).transpose(1, 0, 2, 3, 4)
    pb = q_pos.reshape(nb, blk)
    k_chunk = k_pos // CHUNK
    scale = 1.0 / math.sqrt(QK_DIM)

    def one_block(args):
        qi, pi = args
        s = jnp.einsum('bqhd,bkhd->bhqk', qi, k, preferred_element_type=jnp.float32) * scale
        mask = k_chunk[None, :] <= (pi // CHUNK)[:, None]
        s = jnp.where(mask[None, None], s, NEG_INF)
        w = jax.nn.softmax(s, axis=-1)
        return jnp.einsum('bhqk,bkhd->bqhd', w.astype(v.dtype), v)

    o = lax.map(one_block, (qb, pb))
    return o.transpose(1, 0, 2, 3, 4).reshape(B, Tq, N_HEADS, V_DIM)


def moe_ffn(h, w_router, b_router, w_gu, b_gu, w_dn, b_dn):
    lead = h.shape[:-1]
    xt = h.reshape(-1, D_MODEL)
    n_tok = xt.shape[0]
    logits = (xt @ w_router).astype(jnp.float32) + b_router.astype(jnp.float32)
    top_val, top_idx = lax.top_k(logits, TOP_K)
    gate = jax.nn.softmax(top_val, axis=-1)
    n_asg = n_tok * TOP_K
    flat_e = top_idx.reshape(-1).astype(jnp.int32)
    flat_t = jnp.arange(n_asg, dtype=jnp.int32) // TOP_K
    order = jnp.argsort(flat_e)
    se = flat_e[order]
    counts = jnp.bincount(flat_e, length=N_EXPERTS).astype(jnp.int32)
    pcounts = (counts + MOE_BLOCK - 1) // MOE_BLOCK * MOE_BLOCK
    pend = jnp.cumsum(pcounts)
    pstart = pend - pcounts
    gstart = jnp.cumsum(counts) - counts
    dest = pstart[se] + jnp.arange(n_asg, dtype=jnp.int32) - gstart[se]
    n_blocks = -(-(n_asg + N_EXPERTS * (MOE_BLOCK - 1)) // MOE_BLOCK)
    n_rows = n_blocks * MOE_BLOCK
    row_tok = jnp.zeros((n_rows,), jnp.int32).at[dest].set(flat_t[order])
    row_gate = jnp.zeros((n_rows,), jnp.float32).at[dest].set(gate.reshape(-1)[order])
    blk_exp = jnp.minimum(jnp.searchsorted(pend, jnp.arange(n_blocks, dtype=jnp.int32) * MOE_BLOCK,
                                           side='right'), N_EXPERTS - 1)
    xs = xt[row_tok].reshape(n_blocks, MOE_BLOCK, D_MODEL)

    def expert_block(args):
        xb, e = args
        gu = xb @ w_gu[e] + b_gu[e]
        g = jnp.minimum(gu[..., :D_FF], SWIGLU_LIMIT)
        u = jnp.clip(gu[..., D_FF:], -SWIGLU_LIMIT, SWIGLU_LIMIT)
        a = (u + 1.0) * (g * jax.nn.sigmoid(SWIGLU_ALPHA * g))
        return a @ w_dn[e] + b_dn[e]

    ys = lax.map(expert_block, (xs, blk_exp)).reshape(n_rows, D_MODEL)
    out = jnp.zeros(xt.shape, jnp.float32).at[row_tok].add(ys.astype(jnp.float32) * row_gate[:, None])
    return out.astype(h.dtype).reshape(*lead, D_MODEL)


def hybrid_layer(x, p, kv_past, kr_past, conv_past,
                 g_mix, w_in, b_gate, w_dw, b_dw, g_cn, b_cn, w_conv_out,
                 g_qa, g_kva, w_qb, w_kb, w_vb, g_qn, g_kn, w_o, w_out,
                 g_ffn, w_router, b_router, w_gu, b_gu, w_dn, b_dn,
                 g_ple, w_ple_gate, w_ple):
    B, T = x.shape[0], x.shape[1]
    past = kv_past.shape[1]
    h = rms_norm(x, g_mix)
    z = h @ w_in
    o1 = 2 * CONV_CHANNELS
    o2 = o1 + Q_LORA_RANK
    o3 = o2 + KV_LORA_RANK
    o4 = o3 + ROPE_DIM
    u = z[..., :o1]
    q_lat = z[..., o1:o2]
    kv_lat = z[..., o2:o3]
    kr_new = z[..., o3:o4]
    z_gate = z[..., o4:] + b_gate

    glu = u[..., :CONV_CHANNELS] * jax.nn.sigmoid(u[..., CONV_CHANNELS:])
    padded = jnp.concatenate([conv_past, glu], axis=1)
    conv_new = padded[:, -(CONV_WIDTH - 1):]
    c = lax.conv_general_dilated(padded, w_dw[:, None, :], window_strides=(1,), padding='VALID',
                                 dimension_numbers=('NWC', 'WIO', 'NWC'),
                                 feature_group_count=CONV_CHANNELS) + b_dw
    c = layer_norm(c, g_cn, b_cn)
    conv_out = (c * jax.nn.sigmoid(c)) @ w_conv_out

    q_pos = past + jnp.arange(T, dtype=jnp.int32)
    k_pos = jnp.arange(past + T, dtype=jnp.int32)
    q = jnp.einsum('btr,rhd->bthd', rms_norm(q_lat, g_qa), w_qb)
    q = rms_norm(q, g_qn)
    q = jnp.concatenate([q[..., :NOPE_DIM], rope(q[..., NOPE_DIM:], q_pos)], axis=-1)
    kv_new = rms_norm(kv_lat, g_kva)
    kv_all = jnp.concatenate([kv_past, kv_new], axis=1)
    kr_all = jnp.concatenate([kr_past, kr_new], axis=1)
    k_nope = jnp.einsum('bsr,rhd->bshd', kv_all, w_kb)
    k_r = jnp.broadcast_to(kr_all[:, :, None, :], k_nope.shape[:3] + (ROPE_DIM,))
    k = rms_norm(jnp.concatenate([k_nope, k_r], axis=-1), g_kn)
    k = jnp.concatenate([k[..., :NOPE_DIM], rope(k[..., NOPE_DIM:], k_pos)], axis=-1)
    v = jnp.einsum('bsr,rhd->bshd', kv_all, w_vb)
    attn = chunk_causal_attention(q, k, v, q_pos, k_pos)
    mla_out = attn.reshape(B, T, N_HEADS * V_DIM) @ w_o

    mix = jax.nn.sigmoid(z_gate[..., :D_MODEL]) * conv_out + jax.nn.sigmoid(z_gate[..., D_MODEL:]) * mla_out
    x = x + mix @ w_out

    x = x + moe_ffn(rms_norm(x, g_ffn), w_router, b_router, w_gu, b_gu, w_dn, b_dn)

    x = x + jax.nn.sigmoid(rms_norm(x, g_ple) @ w_ple_gate) * (p @ w_ple)
    return x, kv_new, kr_new, conv_new


def setup_inputs(seed: int = 0) -> dict:
    key = jax.random.key(seed)
    ks = jax.random.split(key, 40)
    f32 = jnp.float32

    def nrm(k, shape, scale):
        return jax.random.normal(k, shape, f32) * scale

    def gain(k, shape):
        return 1.0 + 0.01 * jax.random.normal(k, shape, f32)

    L = DEPTH
    return {
        'x_prompt': nrm(ks[0], (BATCH, SEQ, D_MODEL), 1.0),
        'x_sample': nrm(ks[1], (DEC_BATCH, DEC_SEQ, D_MODEL), 1.0),
        'cache_kv_latent': nrm(ks[2], (L, DEC_BATCH, PAST_LEN, KV_LORA_RANK), 1.0),
        'cache_k_rope': nrm(ks[3], (L, DEC_BATCH, PAST_LEN, ROPE_DIM), 1.0),
        'state_conv': nrm(ks[4], (L, DEC_BATCH, CONV_WIDTH - 1, CONV_CHANNELS), 0.5),
        'p_prompt': nrm(ks[5], (L, BATCH, SEQ, PLE_DIM), 1.0),
        'p_sample': nrm(ks[6], (L, DEC_BATCH, DEC_SEQ, PLE_DIM), 1.0),
        'g_mix': gain(ks[7], (L, D_MODEL)),
        'w_in': nrm(ks[8], (L, D_MODEL, IN_WIDTH), D_MODEL ** -0.5),
        'b_gate': nrm(ks[9], (L, 2 * D_MODEL), 0.02),
        'w_dw': nrm(ks[10], (L, CONV_WIDTH, CONV_CHANNELS), CONV_WIDTH ** -0.5),
        'b_dw': nrm(ks[11], (L, CONV_CHANNELS), 0.02),
        'g_cn': gain(ks[12], (L, CONV_CHANNELS)),
        'b_cn': nrm(ks[13], (L, CONV_CHANNELS), 0.02),
        'w_conv_out': nrm(ks[14], (L, CONV_CHANNELS, D_MODEL), CONV_CHANNELS ** -0.5),
        'g_qa': gain(ks[15], (L, Q_LORA_RANK)),
        'g_kva': gain(ks[16], (L, KV_LORA_RANK)),
        'w_qb': nrm(ks[17], (L, Q_LORA_RANK, N_HEADS, QK_DIM), Q_LORA_RANK ** -0.5),
        'w_kb': nrm(ks[18], (L, KV_LORA_RANK, N_HEADS, NOPE_DIM), KV_LORA_RANK ** -0.5),
        'w_vb': nrm(ks[19], (L, KV_LORA_RANK, N_HEADS, V_DIM), KV_LORA_RANK ** -0.5),
        'g_qn': gain(ks[20], (L, QK_DIM)),
        'g_kn': gain(ks[21], (L, QK_DIM)),
        'w_o': nrm(ks[22], (L, N_HEADS * V_DIM, D_MODEL), (N_HEADS * V_DIM) ** -0.5),
        'w_out': nrm(ks[23], (L, D_MODEL, D_MODEL), D_MODEL ** -0.5),
        'g_ffn': gain(ks[24], (L, D_MODEL)),
        'w_router': nrm(ks[25], (L, D_MODEL, N_EXPERTS), D_MODEL ** -0.5),
        'b_router': nrm(ks[26], (L, N_EXPERTS), 0.01),
        'w_gu': nrm(ks[27], (L, N_EXPERTS, D_MODEL, 2 * D_FF), D_MODEL ** -0.5),
        'b_gu': nrm(ks[28], (L, N_EXPERTS, 2 * D_FF), 0.02),
        'w_dn': nrm(ks[29], (L, N_EXPERTS, D_FF, D_MODEL), D_FF ** -0.5),
        'b_dn': nrm(ks[30], (L, N_EXPERTS, D_MODEL), 0.02),
        'g_ple': gain(ks[31], (L, D_MODEL)),
        'w_ple_gate': nrm(ks[32], (L, D_MODEL, D_MODEL), D_MODEL ** -0.5),
        'w_ple': nrm(ks[33], (L, PLE_DIM, D_MODEL), PLE_DIM ** -0.5),
    }


def reference(x_prompt, x_sample, cache_kv_latent, cache_k_rope, state_conv, p_prompt, p_sample,
              g_mix, w_in, b_gate, w_dw, b_dw, g_cn, b_cn, w_conv_out,
              g_qa, g_kva, w_qb, w_kb, w_vb, g_qn, g_kn, w_o, w_out,
              g_ffn, w_router, b_router, w_gu, b_gu, w_dn, b_dn,
              g_ple, w_ple_gate, w_ple):
    xp, xs = x_prompt, x_sample
    nb = xp.shape[0]
    kvp_l, krp_l, cvp_l, kvs_l, krs_l, cvs_l = [], [], [], [], [], []
    for i in range(DEPTH):
        lw = (g_mix[i], w_in[i], b_gate[i], w_dw[i], b_dw[i], g_cn[i], b_cn[i], w_conv_out[i],
              g_qa[i], g_kva[i], w_qb[i], w_kb[i], w_vb[i], g_qn[i], g_kn[i], w_o[i], w_out[i],
              g_ffn[i], w_router[i], b_router[i], w_gu[i], b_gu[i], w_dn[i], b_dn[i],
              g_ple[i], w_ple_gate[i], w_ple[i])
        empty_kv = jnp.zeros((nb, 0, KV_LORA_RANK), xp.dtype)
        empty_kr = jnp.zeros((nb, 0, ROPE_DIM), xp.dtype)
        zero_conv = jnp.zeros((nb, CONV_WIDTH - 1, CONV_CHANNELS), xp.dtype)
        xp, kvp, krp, cvp = hybrid_layer(xp, p_prompt[i], empty_kv, empty_kr, zero_conv, *lw)
        xs, kvs, krs, cvs = hybrid_layer(xs, p_sample[i], cache_kv_latent[i], cache_k_rope[i],
                                         state_conv[i], *lw)
        kvp_l.append(kvp)
        krp_l.append(krp)
        cvp_l.append(cvp)
        kvs_l.append(kvs)
        krs_l.append(krs)
        cvs_l.append(cvs)
    return (xp, xs, jnp.stack(kvp_l), jnp.stack(krp_l), jnp.stack(cvp_l),
            jnp.stack(kvs_l), jnp.stack(krs_l), jnp.stack(cvs_l))
```

```python
import functools
import math

import jax
import jax.numpy as jnp
from jax import lax
from jax.experimental import pallas as pl
from jax.experimental.pallas import tpu as pltpu

F32 = jnp.float32
BF16 = jnp.bfloat16
I32 = jnp.int32

D_MODEL = 2048
BATCH = 2
SEQ = 4096
DEC_BATCH = 8
DEC_SEQ = 64
PAST_LEN = 4096
CHUNK = 64
CONV_CHANNELS = D_MODEL
CONV_WIDTH = 31
N_HEADS = 16
Q_LORA_RANK = 512
KV_LORA_RANK = 512
NOPE_DIM = 128
ROPE_DIM = 64
QK_DIM = NOPE_DIM + ROPE_DIM
V_DIM = 128
ROPE_THETA = 10000.0
N_EXPERTS = 32
TOP_K = 4
D_FF = D_MODEL
SWIGLU_ALPHA = 1.702
SWIGLU_LIMIT = 7.0
PLE_DIM = 256
EPS = 1e-6
NEG_INF = -1e30

N_P = BATCH * SEQ
N_S = DEC_BATCH * DEC_SEQ
N_TOK = N_P + N_S
TK_S = PAST_LEN + DEC_SEQ
O_U = 2 * CONV_CHANNELS
O_Q = O_U + Q_LORA_RANK
O_KV = O_Q + KV_LORA_RANK
O_KR = O_KV + ROPE_DIM
MID_W = 1152
HEAD_PAD = 256

TM = 512
CONV_T = 64
HALO = 32
MOE_BLK = 256
MOE_MAX_BLKS = (N_TOK * TOP_K) // MOE_BLK + N_EXPERTS
MOE_ROWS = MOE_MAX_BLKS * MOE_BLK
VMEM_LIMIT = 48 * 1024 * 1024


def _cparams(n_axes):
    return pltpu.CompilerParams(dimension_semantics=("arbitrary",) * n_axes,
                                vmem_limit_bytes=VMEM_LIMIT)


def _sigmoid(x):
    return 1.0 / (1.0 + jnp.exp(-x))


def _dot(a, b):
    return jnp.dot(a, b, preferred_element_type=F32)


def _in_mid_kernel(x_ref, g_ref, w_ref, gqa_ref, gkva_ref, h_ref, q_ref, kv_ref, kr_ref):
    x = x_ref[...]
    h = x * lax.rsqrt(jnp.mean(x * x, axis=-1, keepdims=True) + EPS) * g_ref[...]
    hb = h.astype(BF16)
    h_ref[...] = hb
    z = _dot(hb, w_ref[...])
    ql = z[:, :Q_LORA_RANK]
    kvl = z[:, Q_LORA_RANK:Q_LORA_RANK + KV_LORA_RANK]
    qn = ql * lax.rsqrt(jnp.mean(ql * ql, axis=-1, keepdims=True) + EPS) * gqa_ref[...]
    q_ref[...] = qn.astype(BF16)
    kv_ref[...] = kvl * lax.rsqrt(jnp.mean(kvl * kvl, axis=-1, keepdims=True) + EPS) * gkva_ref[...]
    kr_ref[...] = z[:, Q_LORA_RANK + KV_LORA_RANK:]


def _in_mid(x, g_mix, w_mid, g_qa, g_kva):
    n = x.shape[0]
    return pl.pallas_call(
        _in_mid_kernel,
        grid=(n // TM,),
        in_specs=[
            pl.BlockSpec((TM, D_MODEL), lambda i: (i, 0)),
            pl.BlockSpec((1, D_MODEL), lambda i: (0, 0)),
            pl.BlockSpec((D_MODEL, MID_W), lambda i: (0, 0)),
            pl.BlockSpec((1, Q_LORA_RANK), lambda i: (0, 0)),
            pl.BlockSpec((1, KV_LORA_RANK), lambda i: (0, 0)),
        ],
        out_specs=[
            pl.BlockSpec((TM, D_MODEL), lambda i: (i, 0)),
            pl.BlockSpec((TM, Q_LORA_RANK), lambda i: (i, 0)),
            pl.BlockSpec((TM, KV_LORA_RANK), lambda i: (i, 0)),
            pl.BlockSpec((TM, 128), lambda i: (i, 0)),
        ],
        out_shape=[
            jax.ShapeDtypeStruct((n, D_MODEL), BF16),
            jax.ShapeDtypeStruct((n, Q_LORA_RANK), BF16),
            jax.ShapeDtypeStruct((n, KV_LORA_RANK), F32),
            jax.ShapeDtypeStruct((n, 128), F32),
        ],
        compiler_params=_cparams(1),
        name="in_mid",
    )(x, g_mix, w_mid, g_qa, g_kva)


def _glu_kernel(h_ref, w1_ref, w2_ref, o_ref):
    h = h_ref[...]
    o_ref[...] = _dot(h, w1_ref[...]) * _sigmoid(_dot(h, w2_ref[...]))


def _in_glu(h, w_in_b):
    n = h.shape[0]
    tn = 512
    nj = CONV_CHANNELS // tn
    return pl.pallas_call(
        _glu_kernel,
        grid=(n // TM, nj),
        in_specs=[
            pl.BlockSpec((TM, D_MODEL), lambda i, j: (i, 0)),
            pl.BlockSpec((D_MODEL, tn), lambda i, j: (0, j)),
            pl.BlockSpec((D_MODEL, tn), lambda i, j: (0, j + nj)),
        ],
        out_specs=pl.BlockSpec((TM, tn), lambda i, j: (i, j)),
        out_shape=jax.ShapeDtypeStruct((n, CONV_CHANNELS), F32),
        compiler_params=_cparams(2),
        name="in_glu",
    )(h, w_in_b, w_in_b)


_CONV_TILES_PER_SEQ = SEQ // CONV_T
_CONV_PROMPT_TILES = N_P // CONV_T
_CONV_LANES = 512


def _conv_kernel(cur_ref, prev_ref, hist_ref, w_ref, bdw_ref, g_ref, b_ref, o_ref, win_ref, conv_ref):
    i = pl.program_id(0)
    first = jnp.logical_or(i >= _CONV_PROMPT_TILES, i % _CONV_TILES_PER_SEQ == 0)

    @pl.when(first)
    def _():
        win_ref[0:HALO, :] = hist_ref[0]

    @pl.when(jnp.logical_not(first))
    def _():
        win_ref[0:HALO, :] = prev_ref[...]

    win_ref[HALO:HALO + CONV_T, :] = cur_ref[...]
    base = HALO - (CONV_WIDTH - 1)
    for c in range(0, CONV_CHANNELS, _CONV_LANES):
        acc = jnp.zeros((CONV_T, _CONV_LANES), F32)
        for k in range(CONV_WIDTH):
            acc = acc + w_ref[k:k + 1, c:c + _CONV_LANES] * win_ref[base + k:base + k + CONV_T, c:c + _CONV_LANES]
        conv_ref[:, c:c + _CONV_LANES] = acc + bdw_ref[:, c:c + _CONV_LANES]
    y = conv_ref[...]
    yc = y - jnp.mean(y, axis=-1, keepdims=True)
    var = jnp.mean(yc * yc, axis=-1, keepdims=True)
    z = yc * lax.rsqrt(var + EPS) * g_ref[...] + b_ref[...]
    o_ref[...] = (z * _sigmoid(z)).astype(BF16)


def _conv_module(glu, hist, w_dw, b_dw, g_cn, b_cn):
    n = glu.shape[0]
    n_tiles = n // CONV_T
    halo_per_tile = CONV_T // HALO

    def seq_of(i):
        return jnp.where(i < _CONV_PROMPT_TILES, i // _CONV_TILES_PER_SEQ, i - _CONV_PROMPT_TILES + BATCH)

    return pl.pallas_call(
        _conv_kernel,
        grid=(n_tiles,),
        in_specs=[
            pl.BlockSpec((CONV_T, CONV_CHANNELS), lambda i: (i, 0)),
            pl.BlockSpec((HALO, CONV_CHANNELS), lambda i: (jnp.maximum(i * halo_per_tile - 1, 0), 0)),
            pl.BlockSpec((1, HALO, CONV_CHANNELS), lambda i: (seq_of(i), 0, 0)),
            pl.BlockSpec((CONV_WIDTH, CONV_CHANNELS), lambda i: (0, 0)),
            pl.BlockSpec((1, CONV_CHANNELS), lambda i: (0, 0)),
            pl.BlockSpec((1, CONV_CHANNELS), lambda i: (0, 0)),
            pl.BlockSpec((1, CONV_CHANNELS), lambda i: (0, 0)),
        ],
        out_specs=pl.BlockSpec((CONV_T, CONV_CHANNELS), lambda i: (i, 0)),
        out_shape=jax.ShapeDtypeStruct((n, CONV_CHANNELS), BF16),
        scratch_shapes=[pltpu.VMEM((HALO + CONV_T, CONV_CHANNELS), F32),
                        pltpu.VMEM((CONV_T, CONV_CHANNELS), F32)],
        compiler_params=_cparams(1),
        name="conv_module",
    )(glu, glu, hist, w_dw, b_dw, g_cn, b_cn)


def _rope_pair(u, c, s):
    return u * c + pltpu.roll(u, 64, 1) * s


def _q_heads_kernel(ql_ref, w_ref, g_ref, c_ref, s_ref, o_ref):
    qf = _dot(ql_ref[...], w_ref[0])
    ssq = jnp.sum(qf * qf, axis=-1, keepdims=True)
    qn = qf * lax.rsqrt(ssq * (1.0 / QK_DIM) + EPS) * g_ref[...]
    scale = 1.0 / math.sqrt(QK_DIM)
    o_ref[0, :, :NOPE_DIM] = (qn[:, :NOPE_DIM] * scale).astype(BF16)
    o_ref[0, :, NOPE_DIM:] = (_rope_pair(qn[:, NOPE_DIM:], c_ref[...], s_ref[...]) * scale).astype(BF16)


def _q_heads(q_lat, w_q, g_q, cos_t, sin_t):
    n = q_lat.shape[0]
    return pl.pallas_call(
        _q_heads_kernel,
        grid=(n // TM, N_HEADS),
        in_specs=[
            pl.BlockSpec((TM, Q_LORA_RANK), lambda i, h: (i, 0)),
            pl.BlockSpec((1, Q_LORA_RANK, HEAD_PAD), lambda i, h: (h, 0, 0)),
            pl.BlockSpec((1, HEAD_PAD), lambda i, h: (0, 0)),
            pl.BlockSpec((TM, 128), lambda i, h: (i, 0)),
            pl.BlockSpec((TM, 128), lambda i, h: (i, 0)),
        ],
        out_specs=pl.BlockSpec((1, TM, HEAD_PAD), lambda i, h: (h, i, 0)),
        out_shape=jax.ShapeDtypeStruct((N_HEADS, n, HEAD_PAD), BF16),
        compiler_params=_cparams(2),
        name="q_heads",
    )(q_lat, w_q, g_q, cos_t, sin_t)


def _kv_heads_kernel(kv_ref, kr_ref, w_ref, gn_ref, gr_ref, c_ref, s_ref, k_ref, v_ref, krot_ref, ssqr_ref):
    @pl.when(pl.program_id(1) == 0)
    def _():
        u = kr_ref[...]
        ssqr_ref[...] = jnp.sum(u * u, axis=-1, keepdims=True)
        krot_ref[...] = _rope_pair(u * gr_ref[...], c_ref[...], s_ref[...])

    z = _dot(kv_ref[...], w_ref[0])
    kn = z[:, :NOPE_DIM]
    ssq = jnp.sum(kn * kn, axis=-1, keepdims=True) + ssqr_ref[...]
    scale = lax.rsqrt(ssq * (1.0 / QK_DIM) + EPS)
    k_ref[0, :, :NOPE_DIM] = (kn * scale * gn_ref[...]).astype(BF16)
    k_ref[0, :, NOPE_DIM:] = (krot_ref[...] * scale).astype(BF16)
    v_ref[0] = z[:, NOPE_DIM:].astype(BF16)


def _kv_heads(kv_lat, kr_pad, w_kv, g_kn_nope, g_kn_rope, cos_t, sin_t):
    n = kv_lat.shape[0]
    return pl.pallas_call(
        _kv_heads_kernel,
        grid=(n // TM, N_HEADS),
        in_specs=[
            pl.BlockSpec((TM, KV_LORA_RANK), lambda i, h: (i, 0)),
            pl.BlockSpec((TM, 128), lambda i, h: (i, 0)),
            pl.BlockSpec((1, KV_LORA_RANK, NOPE_DIM + V_DIM), lambda i, h: (h, 0, 0)),
            pl.BlockSpec((1, NOPE_DIM), lambda i, h: (0, 0)),
            pl.BlockSpec((1, 128), lambda i, h: (0, 0)),
            pl.BlockSpec((TM, 128), lambda i, h: (i, 0)),
            pl.BlockSpec((TM, 128), lambda i, h: (i, 0)),
        ],
        out_specs=[
            pl.BlockSpec((1, TM, HEAD_PAD), lambda i, h: (h, i, 0)),
            pl.BlockSpec((1, TM, V_DIM), lambda i, h: (h, i, 0)),
        ],
        out_shape=[
            jax.ShapeDtypeStruct((N_HEADS, n, HEAD_PAD), BF16),
            jax.ShapeDtypeStruct((N_HEADS, n, V_DIM), BF16),
        ],
        scratch_shapes=[pltpu.VMEM((TM, 128), F32), pltpu.VMEM((TM, 1), F32)],
        compiler_params=_cparams(2),
        name="kv_heads",
    )(kv_lat, kr_pad, w_kv, g_kn_nope, g_kn_rope, cos_t, sin_t)


_TQ = 512
_TKB = 512


def _flash_prompt_kernel(q_ref, k_ref, v_ref, o_ref, m_ref, l_ref, acc_ref):
    qi = pl.program_id(2)
    q = q_ref[0]
    m_ref[...] = jnp.full((_TQ, 1), NEG_INF, F32)
    l_ref[...] = jnp.zeros((_TQ, 1), F32)
    acc_ref[...] = jnp.zeros((_TQ, V_DIM), F32)

    def step(ki, masked):
        start = pl.multiple_of(ki * _TKB, _TKB)
        k = k_ref[0, pl.ds(start, _TKB), :]
        v = v_ref[0, pl.ds(start, _TKB), :]
        s = lax.dot_general(q, k, (((1,), (1,)), ((), ())), preferred_element_type=F32)
        if masked:
            rc = lax.broadcasted_iota(I32, (_TQ, _TKB), 0) // CHUNK
            cc = lax.broadcasted_iota(I32, (_TQ, _TKB), 1) // CHUNK
            s = jnp.where(cc <= rc, s, NEG_INF)
        m_prev = m_ref[...]
        m_new = jnp.maximum(m_prev, jnp.max(s, axis=-1, keepdims=True))
        p = jnp.exp(s - m_new)
        alpha = jnp.exp(m_prev - m_new)
        l_ref[...] = alpha * l_ref[...] + jnp.sum(p, axis=-1, keepdims=True)
        acc_ref[...] = alpha * acc_ref[...] + _dot(p.astype(BF16), v)
        m_ref[...] = m_new

    def body(ki, carry):
        step(ki, False)
        return carry

    lax.fori_loop(0, qi, body, 0)
    step(qi, True)
    o_ref[...] = (acc_ref[...] / l_ref[...]).astype(BF16)


def _flash_prompt(q, k, v):
    nq = SEQ // _TQ
    return pl.pallas_call(
        _flash_prompt_kernel,
        grid=(BATCH, N_HEADS, nq),
        in_specs=[
            pl.BlockSpec((1, _TQ, HEAD_PAD), lambda b, h, i: (h, b * nq + i, 0)),
            pl.BlockSpec((1, SEQ, HEAD_PAD), lambda b, h, i: (h, b, 0)),
            pl.BlockSpec((1, SEQ, V_DIM), lambda b, h, i: (h, b, 0)),
        ],
        out_specs=pl.BlockSpec((_TQ, V_DIM), lambda b, h, i: (b * nq + i, h)),
        out_shape=jax.ShapeDtypeStruct((N_TOK, N_HEADS * V_DIM), BF16),
        scratch_shapes=[pltpu.VMEM((_TQ, 1), F32), pltpu.VMEM((_TQ, 1), F32),
                        pltpu.VMEM((_TQ, V_DIM), F32)],
        compiler_params=_cparams(3),
        name="flash_prompt",
    )(q, k, v)


def _flash_sample_kernel(prev_ref, q_ref, k_ref, v_ref, o_ref):
    del prev_ref
    s = lax.dot_general(q_ref[0], k_ref[0], (((1,), (1,)), ((), ())), preferred_element_type=F32)
    m = jnp.max(s, axis=-1, keepdims=True)
    p = jnp.exp(s - m)
    l = jnp.sum(p, axis=-1, keepdims=True)
    o_ref[...] = (_dot(p.astype(BF16), v_ref[0]) / l).astype(BF16)


def _flash_sample(attn, q, k, v):
    assert (PAST_LEN + DEC_SEQ - 1) // CHUNK <= PAST_LEN // CHUNK
    q_blk0 = N_P // DEC_SEQ
    return pl.pallas_call(
        _flash_sample_kernel,
        grid=(DEC_BATCH, N_HEADS),
        in_specs=[
            pl.BlockSpec(memory_space=pl.ANY),
            pl.BlockSpec((1, DEC_SEQ, HEAD_PAD), lambda b, h: (h, q_blk0 + b, 0)),
            pl.BlockSpec((1, TK_S, HEAD_PAD), lambda b, h: (h, b, 0)),
            pl.BlockSpec((1, TK_S, V_DIM), lambda b, h: (h, b, 0)),
        ],
        out_specs=pl.BlockSpec((DEC_SEQ, V_DIM), lambda b, h: (q_blk0 + b, h)),
        out_shape=jax.ShapeDtypeStruct((N_TOK, N_HEADS * V_DIM), BF16),
        input_output_aliases={0: 0},
        compiler_params=_cparams(2),
        name="flash_sample",
    )(attn, q, k, v)


def _merge_kernel(h_ref, c_ref, a_ref, wga_ref, wgb_ref, bga_ref, bgb_ref, wc_ref, wo_ref, o_ref):
    h = h_ref[...]
    ga = _sigmoid(_dot(h, wga_ref[...]) + bga_ref[...])
    gb = _sigmoid(_dot(h, wgb_ref[...]) + bgb_ref[...])
    mix = ga * _dot(c_ref[...], wc_ref[...]) + gb * _dot(a_ref[...], wo_ref[...])
    o_ref[...] = mix.astype(BF16)


def _merge(h, c_act, attn, w_gate, b_gate, w_conv_out, w_o):
    n = h.shape[0]
    tn = 512
    nj = D_MODEL // tn
    row = lambda i, j: (i, 0)
    return pl.pallas_call(
        _merge_kernel,
        grid=(n // TM, nj),
        in_specs=[
            pl.BlockSpec((TM, D_MODEL), row),
            pl.BlockSpec((TM, CONV_CHANNELS), row),
            pl.BlockSpec((TM, N_HEADS * V_DIM), row),
            pl.BlockSpec((D_MODEL, tn), lambda i, j: (0, j)),
            pl.BlockSpec((D_MODEL, tn), lambda i, j: (0, j + nj)),
            pl.BlockSpec((1, tn), lambda i, j: (0, j)),
            pl.BlockSpec((1, tn), lambda i, j: (0, j + nj)),
            pl.BlockSpec((CONV_CHANNELS, tn), lambda i, j: (0, j)),
            pl.BlockSpec((N_HEADS * V_DIM, tn), lambda i, j: (0, j)),
        ],
        out_specs=pl.BlockSpec((TM, tn), lambda i, j: (i, j)),
        out_shape=jax.ShapeDtypeStruct((n, D_MODEL), BF16),
        compiler_params=_cparams(2),
        name="merge",
    )(h, c_act, attn, w_gate, w_gate, b_gate, b_gate, w_conv_out, w_o)


def _split_bf16(x):
    hi = x.astype(BF16)
    lo = (x - hi.astype(F32)).astype(BF16)
    return hi, lo


def _out_router_kernel(mix_ref, x_ref, w_ref, g_ref, wrh_ref, wrl_ref, br_ref,
                       x1_ref, hm_ref, idx_ref, gate_ref):
    x1 = x_ref[...] + _dot(mix_ref[...], w_ref[...])
    x1_ref[...] = x1
    hn = x1 * lax.rsqrt(jnp.mean(x1 * x1, axis=-1, keepdims=True) + EPS) * g_ref[...]
    hm_ref[...] = hn.astype(BF16)
    hh, hl = _split_bf16(hn)
    logits = _dot(hh, wrh_ref[...]) + (_dot(hh, wrl_ref[...]) + _dot(hl, wrh_ref[...])) + br_ref[...]
    lane = lax.broadcasted_iota(I32, logits.shape, 1).astype(F32)
    vals = []
    idx_out = jnp.zeros(logits.shape, F32)
    for k in range(TOP_K):
        m = jnp.max(logits, axis=-1, keepdims=True)
        sel = jnp.min(jnp.where(logits == m, lane, 1e9), axis=-1, keepdims=True)
        vals.append(m)
        idx_out = jnp.where(lane == float(k), sel, idx_out)
        logits = jnp.where(lane == sel, -jnp.inf, logits)
    exps = [jnp.exp(v - vals[0]) for v in vals]
    denom = exps[0] + exps[1] + exps[2] + exps[3]
    gate_out = jnp.zeros(idx_out.shape, F32)
    for k in range(TOP_K):
        gate_out = jnp.where(lane == float(k), exps[k] / denom, gate_out)
    idx_ref[...] = idx_out.astype(I32)
    gate_ref[...] = gate_out


def _out_router(mix, x, w_out, g_ffn, wr_hi, wr_lo, b_r):
    n = x.shape[0]
    tm = 256
    const = lambda i: (0, 0)
    row = lambda i: (i, 0)
    return pl.pallas_call(
        _out_router_kernel,
        grid=(n // tm,),
        in_specs=[
            pl.BlockSpec((tm, D_MODEL), row),
            pl.BlockSpec((tm, D_MODEL), row),
            pl.BlockSpec((D_MODEL, D_MODEL), const),
            pl.BlockSpec((1, D_MODEL), const),
            pl.BlockSpec((D_MODEL, 128), const),
            pl.BlockSpec((D_MODEL, 128), const),
            pl.BlockSpec((1, 128), const),
        ],
        out_specs=[
            pl.BlockSpec((tm, D_MODEL), row),
            pl.BlockSpec((tm, D_MODEL), row),
            pl.BlockSpec((tm, 128), row),
            pl.BlockSpec((tm, 128), row),
        ],
        out_shape=[
            jax.ShapeDtypeStruct((n, D_MODEL), F32),
            jax.ShapeDtypeStruct((n, D_MODEL), BF16),
            jax.ShapeDtypeStruct((n, 128), I32),
            jax.ShapeDtypeStruct((n, 128), F32),
        ],
        compiler_params=_cparams(1),
        name="out_router",
    )(mix, x, w_out, g_ffn, wr_hi, wr_lo, b_r)


def _moe_up_kernel(se_ref, sn_ref, sb_ref, sv_ref, sf_ref,
                   x_ref, wg_ref, wu_ref, bg_ref, bu_ref, o_ref, wgb_ref, wub_ref):
    t = pl.program_id(0)

    @pl.when(sf_ref[t] == 1)
    def _():
        wgb_ref[...] = wg_ref[0].astype(BF16)
        wub_ref[...] = wu_ref[0].astype(BF16)

    @pl.when(sv_ref[t] == 1)
    def _():
        x = x_ref[...]
        g = _dot(x, wgb_ref[...]) + bg_ref[0]
        u = _dot(x, wub_ref[...]) + bu_ref[0]
        g = jnp.minimum(g, SWIGLU_LIMIT)
        u = jnp.clip(u, -SWIGLU_LIMIT, SWIGLU_LIMIT)
        o_ref[...] = ((u + 1.0) * (g * _sigmoid(SWIGLU_ALPHA * g))).astype(BF16)


_UP_TN = 512
_UP_TILES = D_FF // _UP_TN
_DN_TN = 1024
_DN_TILES = D_MODEL // _DN_TN


def _moe_up(plan, xs, w_gu, b_gu):
    steps = plan[0].shape[0]
    return pl.pallas_call(
        _moe_up_kernel,
        grid_spec=pltpu.PrefetchScalarGridSpec(
            num_scalar_prefetch=5,
            grid=(steps,),
            in_specs=[
                pl.BlockSpec((MOE_BLK, D_MODEL), lambda t, se, sn, sb, sv, sf: (sb[t], 0)),
                pl.BlockSpec((1, D_MODEL, _UP_TN), lambda t, se, sn, sb, sv, sf: (se[t], 0, sn[t])),
                pl.BlockSpec((1, D_MODEL, _UP_TN), lambda t, se, sn, sb, sv, sf: (se[t], 0, sn[t] + _UP_TILES)),
                pl.BlockSpec((1, 1, _UP_TN), lambda t, se, sn, sb, sv, sf: (se[t], 0, sn[t])),
                pl.BlockSpec((1, 1, _UP_TN), lambda t, se, sn, sb, sv, sf: (se[t], 0, sn[t] + _UP_TILES)),
            ],
            out_specs=pl.BlockSpec((MOE_BLK, _UP_TN), lambda t, se, sn, sb, sv, sf: (sb[t], sn[t])),
            scratch_shapes=[pltpu.VMEM((D_MODEL, _UP_TN), BF16), pltpu.VMEM((D_MODEL, _UP_TN), BF16)],
        ),
        out_shape=jax.ShapeDtypeStruct((MOE_ROWS, D_FF), BF16),
        compiler_params=_cparams(1),
        name="moe_up",
    )(*plan, xs, w_gu, w_gu, b_gu, b_gu)


def _moe_down_kernel(se_ref, sn_ref, sb_ref, sv_ref, sf_ref, a_ref, w_ref, b_ref, o_ref, wb_ref):
    t = pl.program_id(0)

    @pl.when(sf_ref[t] == 1)
    def _():
        wb_ref[...] = w_ref[0].astype(BF16)

    @pl.when(sv_ref[t] == 1)
    def _():
        o_ref[...] = _dot(a_ref[...], wb_ref[...]) + b_ref[0]


def _moe_down(plan, act, w_dn, b_dn):
    steps = plan[0].shape[0]
    return pl.pallas_call(
        _moe_down_kernel,
        grid_spec=pltpu.PrefetchScalarGridSpec(
            num_scalar_prefetch=5,
            grid=(steps,),
            in_specs=[
                pl.BlockSpec((MOE_BLK, D_FF), lambda t, se, sn, sb, sv, sf: (sb[t], 0)),
                pl.BlockSpec((1, D_FF, _DN_TN), lambda t, se, sn, sb, sv, sf: (se[t], 0, sn[t])),
                pl.BlockSpec((1, 1, _DN_TN), lambda t, se, sn, sb, sv, sf: (se[t], 0, sn[t])),
            ],
            out_specs=pl.BlockSpec((MOE_BLK, _DN_TN), lambda t, se, sn, sb, sv, sf: (sb[t], sn[t])),
            scratch_shapes=[pltpu.VMEM((D_FF, _DN_TN), BF16)],
        ),
        out_shape=jax.ShapeDtypeStruct((MOE_ROWS, D_MODEL), F32),
        compiler_params=_cparams(1),
        name="moe_down",
    )(*plan, act, w_dn, b_dn)


def _moe_dispatch(top_idx):
    n_asg = N_TOK * TOP_K
    flat_e = top_idx.reshape(-1)
    onehot = (flat_e[:, None] == jnp.arange(N_EXPERTS, dtype=I32)[None, :]).astype(I32)
    csum = jnp.cumsum(onehot, axis=0)
    counts = csum[-1]
    rank = jnp.take_along_axis(csum, flat_e[:, None], axis=1)[:, 0] - 1
    nblk = (counts + MOE_BLK - 1) // MOE_BLK
    blk_start = jnp.cumsum(nblk) - nblk
    dest = blk_start[flat_e] * MOE_BLK + rank
    row_tok = jnp.zeros((MOE_ROWS,), I32).at[dest].set(jnp.arange(n_asg, dtype=I32) // TOP_K)
    return dest, row_tok, nblk, blk_start


def _moe_steps(nblk, blk_start, n_tiles):
    t_max = n_tiles * MOE_MAX_BLKS
    per_e = nblk * n_tiles
    s_end = jnp.cumsum(per_e)
    s_start = s_end - per_e
    total = s_end[-1]
    t = jnp.arange(t_max, dtype=I32)
    tc = jnp.minimum(t, total - 1)
    e = jnp.minimum(jnp.searchsorted(s_end, tc, side="right").astype(I32), N_EXPERTS - 1)
    local = tc - s_start[e]
    nb = jnp.maximum(nblk[e], 1)
    n = local // nb
    r = local % nb
    blk = blk_start[e] + r
    valid = t < total
    first = jnp.logical_and(valid, r == 0)
    return e, n, blk, valid.astype(I32), first.astype(I32)


_FIN_TM = 256
_FIN_TN = 512


def _final_kernel(x1_ref, y0_ref, y1_ref, y2_ref, y3_ref, gate_ref, g_ref, wg_ref, p_ref, wp_ref,
                  o_ref, x2_ref, hp_ref):
    j = pl.program_id(1)

    @pl.when(j == 0)
    def _():
        gate = gate_ref[...]
        moe = (y0_ref[0] * gate[:, 0:1] + y1_ref[0] * gate[:, 1:2]
               + y2_ref[0] * gate[:, 2:3] + y3_ref[0] * gate[:, 3:4])
        x2 = x1_ref[...] + moe
        x2_ref[...] = x2
        hp = x2 * lax.rsqrt(jnp.mean(x2 * x2, axis=-1, keepdims=True) + EPS) * g_ref[...]
        hp_ref[...] = hp.astype(BF16)

    col = pl.multiple_of(j * _FIN_TN, _FIN_TN)
    emb = _dot(p_ref[...].astype(BF16), wp_ref[...])
    o_ref[...] = x2_ref[:, pl.ds(col, _FIN_TN)] + _sigmoid(_dot(hp_ref[...], wg_ref[...])) * emb


def _final(x1, y4, gate, g_ple, w_ple_gate, p, w_ple):
    n = x1.shape[0]
    row = lambda i, j: (i, 0)
    yspec = lambda k: pl.BlockSpec((1, _FIN_TM, D_MODEL), lambda i, j: (k, i, 0))
    return pl.pallas_call(
        _final_kernel,
        grid=(n // _FIN_TM, D_MODEL // _FIN_TN),
        in_specs=[
            pl.BlockSpec((_FIN_TM, D_MODEL), row),
            yspec(0), yspec(1), yspec(2), yspec(3),
            pl.BlockSpec((_FIN_TM, 128), row),
            pl.BlockSpec((1, D_MODEL), lambda i, j: (0, 0)),
            pl.BlockSpec((D_MODEL, _FIN_TN), lambda i, j: (0, j)),
            pl.BlockSpec((_FIN_TM, PLE_DIM), row),
            pl.BlockSpec((PLE_DIM, _FIN_TN), lambda i, j: (0, j)),
        ],
        out_specs=pl.BlockSpec((_FIN_TM, _FIN_TN), lambda i, j: (i, j)),
        out_shape=jax.ShapeDtypeStruct((n, D_MODEL), F32),
        scratch_shapes=[pltpu.VMEM((_FIN_TM, D_MODEL), F32), pltpu.VMEM((_FIN_TM, D_MODEL), BF16)],
        compiler_params=_cparams(2),
        name="final",
    )(x1, y4, y4, y4, y4, gate, g_ple, w_ple_gate, p, w_ple)


def _rope_layout(x):
    half = ROPE_DIM // 2
    z = jnp.zeros(x.shape[:-1] + (half,), x.dtype)
    return jnp.concatenate([x[..., :half], z, x[..., half:], z], axis=-1)


def _rope_tables(pos):
    half = ROPE_DIM // 2
    inv_freq = ROPE_THETA ** (-jnp.arange(half, dtype=F32) / half)
    ang = pos.astype(F32)[:, None] * inv_freq[None, :]
    cos, sin = jnp.cos(ang), jnp.sin(ang)
    z = jnp.zeros_like(cos)
    return (jnp.concatenate([cos, z, cos, z], axis=-1), jnp.concatenate([-sin, z, sin, z], axis=-1))


def _layer(x, p, cache_kv, cache_kr, state_conv,
           g_mix, w_in, b_gate, w_dw, b_dw, g_cn, b_cn, w_conv_out,
           g_qa, g_kva, w_qb, w_kb, w_vb, g_qn, g_kn, w_o, w_out,
           g_ffn, w_router, b_router, w_gu, b_gu, w_dn, b_dn,
           g_ple, w_ple_gate, w_ple):
    row = lambda v: v.reshape(1, -1)
    w_in_b = w_in.astype(BF16)
    w_mid = jnp.pad(w_in_b[:, O_U:O_KR], ((0, 0), (0, MID_W - (O_KR - O_U))))
    w_gate = w_in_b[:, O_KR:]

    h, q_lat, kv_new, kr_pad = _in_mid(x, row(g_mix), w_mid, row(g_qa), row(g_kva))
    kr_new = kr_pad[:, :ROPE_DIM]
    glu = _in_glu(h, w_in_b)

    hist = jnp.concatenate([jnp.zeros((BATCH, HALO, CONV_CHANNELS), F32),
                            jnp.pad(state_conv, ((0, 0), (HALO - (CONV_WIDTH - 1), 0), (0, 0)))], axis=0)
    c_act = _conv_module(glu, hist, w_dw, row(b_dw), row(g_cn), row(b_cn))

    pos_p = jnp.arange(SEQ, dtype=I32)
    pos_s = PAST_LEN + jnp.arange(DEC_SEQ, dtype=I32)
    pos_q = jnp.concatenate([jnp.tile(pos_p, BATCH), jnp.tile(pos_s, DEC_BATCH)])
    cos_q, sin_q = _rope_tables(pos_q)
    w_q = jnp.concatenate([w_qb[..., :NOPE_DIM], _rope_layout(w_qb[..., NOPE_DIM:])], axis=-1)
    w_q = w_q.transpose(1, 0, 2).astype(BF16)
    g_q = jnp.concatenate([g_qn[:NOPE_DIM], _rope_layout(g_qn[NOPE_DIM:])]).reshape(1, HEAD_PAD)
    q = _q_heads(q_lat, w_q, g_q, cos_q, sin_q)

    w_kv = jnp.concatenate([w_kb, w_vb], axis=-1).transpose(1, 0, 2).astype(BF16)
    g_kn_nope = g_kn[:NOPE_DIM].reshape(1, NOPE_DIM)
    g_kn_rope = _rope_layout(g_kn[NOPE_DIM:]).reshape(1, 128)
    kv_p = kv_new[:N_P].astype(BF16)
    kr_p = _rope_layout(kr_new[:N_P])
    cos_kp, sin_kp = cos_q[:N_P], sin_q[:N_P]
    k_p, v_p = _kv_heads(kv_p, kr_p, w_kv, g_kn_nope, g_kn_rope, cos_kp, sin_kp)
    kv_s = jnp.concatenate([cache_kv, kv_new[N_P:].reshape(DEC_BATCH, DEC_SEQ, KV_LORA_RANK)], axis=1)
    kv_s = kv_s.astype(BF16).reshape(DEC_BATCH * TK_S, KV_LORA_RANK)
    kr_s = jnp.concatenate([cache_kr, kr_new[N_P:].reshape(DEC_BATCH, DEC_SEQ, ROPE_DIM)], axis=1)
    kr_s = _rope_layout(kr_s).reshape(DEC_BATCH * TK_S, 128)
    cos_ks, sin_ks = _rope_tables(jnp.tile(jnp.arange(TK_S, dtype=I32), DEC_BATCH))
    k_s, v_s = _kv_heads(kv_s, kr_s, w_kv, g_kn_nope, g_kn_rope, cos_ks, sin_ks)

    attn = _flash_prompt(q, k_p, v_p)
    attn = _flash_sample(attn, q, k_s, v_s)

    mix = _merge(h, c_act, attn, w_gate, row(b_gate), w_conv_out.astype(BF16), w_o.astype(BF16))

    wr = jnp.pad(w_router, ((0, 0), (0, 128 - N_EXPERTS)))
    wr_hi, wr_lo = _split_bf16(wr)
    b_r = jnp.concatenate([b_router, jnp.full((128 - N_EXPERTS,), -jnp.inf, F32)]).reshape(1, 128)
    x1, hm, idx_pad, gate_pad = _out_router(mix, x, w_out.astype(BF16), row(g_ffn), wr_hi, wr_lo, b_r)

    top_idx = idx_pad[:, :TOP_K]
    dest, row_tok, nblk, blk_start = _moe_dispatch(top_idx)
    xs = jnp.take(hm, row_tok, axis=0)
    act = _moe_up(_moe_steps(nblk, blk_start, _UP_TILES), xs, w_gu, b_gu.reshape(N_EXPERTS, 1, 2 * D_FF))
    ys = _moe_down(_moe_steps(nblk, blk_start, _DN_TILES), act, w_dn, b_dn.reshape(N_EXPERTS, 1, D_MODEL))
    y4 = jnp.take(ys, dest.reshape(N_TOK, TOP_K).T, axis=0)

    out = _final(x1, y4, gate_pad, row(g_ple), w_ple_gate.astype(BF16), p, w_ple.astype(BF16))
    return out, kv_new, kr_new, glu


def kernel(x_prompt, x_sample, cache_kv_latent, cache_k_rope, state_conv, p_prompt, p_sample, g_mix, w_in, b_gate, w_dw, b_dw, g_cn, b_cn, w_conv_out, g_qa, g_kva, w_qb, w_kb, w_vb, g_qn, g_kn, w_o, w_out, g_ffn, w_router, b_router, w_gu, b_gu, w_dn, b_dn, g_ple, w_ple_gate, w_ple):
    assert g_mix.shape[0] == 1
    x = jnp.concatenate([x_prompt.reshape(N_P, D_MODEL), x_sample.reshape(N_S, D_MODEL)], axis=0)
    p = jnp.concatenate([p_prompt[0].reshape(N_P, PLE_DIM), p_sample[0].reshape(N_S, PLE_DIM)], axis=0)
    out, kv_new, kr_new, glu = _layer(
        x, p, cache_kv_latent[0], cache_k_rope[0], state_conv[0],
        g_mix[0], w_in[0], b_gate[0], w_dw[0], b_dw[0], g_cn[0], b_cn[0], w_conv_out[0],
        g_qa[0], g_kva[0], w_qb[0], w_kb[0], w_vb[0], g_qn[0], g_kn[0], w_o[0], w_out[0],
        g_ffn[0], w_router[0], b_router[0], w_gu[0], b_gu[0], w_dn[0], b_dn[0],
        g_ple[0], w_ple_gate[0], w_ple[0])
    tail = CONV_WIDTH - 1
    glu_p = glu[:N_P].reshape(BATCH, SEQ, CONV_CHANNELS)
    glu_s = glu[N_P:].reshape(DEC_BATCH, DEC_SEQ, CONV_CHANNELS)
    return (out[:N_P].reshape(BATCH, SEQ, D_MODEL),
            out[N_P:].reshape(DEC_BATCH, DEC_SEQ, D_MODEL),
            kv_new[:N_P].reshape(1, BATCH, SEQ, KV_LORA_RANK),
            kr_new[:N_P].reshape(1, BATCH, SEQ, ROPE_DIM),
            glu_p[:, SEQ - tail:][None],
            kv_new[N_P:].reshape(1, DEC_BATCH, DEC_SEQ, KV_LORA_RANK),
            kr_new[N_P:].reshape(1, DEC_BATCH, DEC_SEQ, ROPE_DIM),
            glu_s[:, DEC_SEQ - tail:][None])
```

```python
import math

import jax
import jax.numpy as jnp
from jax import lax
from jax.experimental import pallas as pl
from jax.experimental.pallas import tpu as pltpu

F32 = jnp.float32
BF16 = jnp.bfloat16
I32 = jnp.int32
U32 = jnp.uint32

D_MODEL = 2048
BATCH = 2
SEQ = 4096
DEC_BATCH = 8
DEC_SEQ = 64
PAST_LEN = 4096
CHUNK = 64
CONV_CHANNELS = D_MODEL
CONV_WIDTH = 31
N_HEADS = 16
Q_LORA_RANK = 512
KV_LORA_RANK = 512
NOPE_DIM = 128
ROPE_DIM = 64
QK_DIM = NOPE_DIM + ROPE_DIM
V_DIM = 128
ROPE_THETA = 10000.0
N_EXPERTS = 32
TOP_K = 4
D_FF = D_MODEL
SWIGLU_ALPHA = 1.702
SWIGLU_LIMIT = 7.0
PLE_DIM = 256
EPS = 1e-6
NEG_INF = -1e30

N_P = BATCH * SEQ
N_S = DEC_BATCH * DEC_SEQ
N_TOK = N_P + N_S
O_U = 2 * CONV_CHANNELS
O_Q = O_U + Q_LORA_RANK
O_KV = O_Q + KV_LORA_RANK
O_KR = O_KV + ROPE_DIM
MID_W = 1152
HEAD_PAD = 256

TM = 512
CONV_T = 64
HALO = 32
MOE_BLK = 256
MOE_MAX_BLKS = (N_TOK * TOP_K) // MOE_BLK + N_EXPERTS
MOE_ROWS = MOE_MAX_BLKS * MOE_BLK
VMEM_LIMIT = 48 * 1024 * 1024


def _cparams(n_axes):
    return pltpu.CompilerParams(dimension_semantics=("arbitrary",) * n_axes,
                                vmem_limit_bytes=VMEM_LIMIT)


def _sigmoid(x):
    return 1.0 / (1.0 + jnp.exp(-x))


def _dot(a, b):
    return jnp.dot(a, b, preferred_element_type=F32)


def _in_mid_kernel(x_ref, g_ref, w_ref, gqa_ref, gkva_ref, h_ref, q_ref, kv_ref, kr_ref):
    x = x_ref[...]
    h = x * lax.rsqrt(jnp.mean(x * x, axis=-1, keepdims=True) + EPS) * g_ref[...]
    hb = h.astype(BF16)
    h_ref[...] = hb
    z = _dot(hb, w_ref[...])
    ql = z[:, :Q_LORA_RANK]
    kvl = z[:, Q_LORA_RANK:Q_LORA_RANK + KV_LORA_RANK]
    qn = ql * lax.rsqrt(jnp.mean(ql * ql, axis=-1, keepdims=True) + EPS) * gqa_ref[...]
    q_ref[...] = qn.astype(BF16)
    kv_ref[...] = kvl * lax.rsqrt(jnp.mean(kvl * kvl, axis=-1, keepdims=True) + EPS) * gkva_ref[...]
    kr_ref[...] = z[:, Q_LORA_RANK + KV_LORA_RANK:]


def _in_mid(x, g_mix, w_mid, g_qa, g_kva):
    n = x.shape[0]
    return pl.pallas_call(
        _in_mid_kernel,
        grid=(n // TM,),
        in_specs=[
            pl.BlockSpec((TM, D_MODEL), lambda i: (i, 0)),
            pl.BlockSpec((1, D_MODEL), lambda i: (0, 0)),
            pl.BlockSpec((D_MODEL, MID_W), lambda i: (0, 0)),
            pl.BlockSpec((1, Q_LORA_RANK), lambda i: (0, 0)),
            pl.BlockSpec((1, KV_LORA_RANK), lambda i: (0, 0)),
        ],
        out_specs=[
            pl.BlockSpec((TM, D_MODEL), lambda i: (i, 0)),
            pl.BlockSpec((TM, Q_LORA_RANK), lambda i: (i, 0)),
            pl.BlockSpec((TM, KV_LORA_RANK), lambda i: (i, 0)),
            pl.BlockSpec((TM, 128), lambda i: (i, 0)),
        ],
        out_shape=[
            jax.ShapeDtypeStruct((n, D_MODEL), BF16),
            jax.ShapeDtypeStruct((n, Q_LORA_RANK), BF16),
            jax.ShapeDtypeStruct((n, KV_LORA_RANK), F32),
            jax.ShapeDtypeStruct((n, 128), F32),
        ],
        compiler_params=_cparams(1),
        name="in_mid",
    )(x, g_mix, w_mid, g_qa, g_kva)


def _glu_kernel(h_ref, w1_ref, w2_ref, o_ref):
    h = h_ref[...]
    o_ref[...] = _dot(h, w1_ref[...]) * _sigmoid(_dot(h, w2_ref[...]))


def _in_glu(h, w_in_b):
    n = h.shape[0]
    tn = 512
    nj = CONV_CHANNELS // tn
    return pl.pallas_call(
        _glu_kernel,
        grid=(n // TM, nj),
        in_specs=[
            pl.BlockSpec((TM, D_MODEL), lambda i, j: (i, 0)),
            pl.BlockSpec((D_MODEL, tn), lambda i, j: (0, j)),
            pl.BlockSpec((D_MODEL, tn), lambda i, j: (0, j + nj)),
        ],
        out_specs=pl.BlockSpec((TM, tn), lambda i, j: (i, j)),
        out_shape=jax.ShapeDtypeStruct((n, CONV_CHANNELS), F32),
        compiler_params=_cparams(2),
        name="in_glu",
    )(h, w_in_b, w_in_b)


_CONV_TILES_PER_SEQ = SEQ // CONV_T
_CONV_PROMPT_TILES = N_P // CONV_T
_CONV_LANES = 512


def _conv_kernel(cur_ref, prev_ref, hist_ref, w_ref, bdw_ref, g_ref, b_ref, o_ref, win_ref, conv_ref):
    i = pl.program_id(0)
    first = jnp.logical_or(i >= _CONV_PROMPT_TILES, i % _CONV_TILES_PER_SEQ == 0)

    @pl.when(first)
    def _():
        win_ref[0:HALO, :] = hist_ref[0]

    @pl.when(jnp.logical_not(first))
    def _():
        win_ref[0:HALO, :] = prev_ref[...]

    win_ref[HALO:HALO + CONV_T, :] = cur_ref[...]
    base = HALO - (CONV_WIDTH - 1)
    for c in range(0, CONV_CHANNELS, _CONV_LANES):
        acc = jnp.zeros((CONV_T, _CONV_LANES), F32)
        for k in range(CONV_WIDTH):
            acc = acc + w_ref[k:k + 1, c:c + _CONV_LANES] * win_ref[base + k:base + k + CONV_T, c:c + _CONV_LANES]
        conv_ref[:, c:c + _CONV_LANES] = acc + bdw_ref[:, c:c + _CONV_LANES]
    y = conv_ref[...]
    yc = y - jnp.mean(y, axis=-1, keepdims=True)
    var = jnp.mean(yc * yc, axis=-1, keepdims=True)
    z = yc * lax.rsqrt(var + EPS) * g_ref[...] + b_ref[...]
    o_ref[...] = (z * _sigmoid(z)).astype(BF16)


def _conv_module(glu, hist, w_dw, b_dw, g_cn, b_cn):
    n = glu.shape[0]
    n_tiles = n // CONV_T
    halo_per_tile = CONV_T // HALO

    def seq_of(i):
        return jnp.where(i < _CONV_PROMPT_TILES, i // _CONV_TILES_PER_SEQ, i - _CONV_PROMPT_TILES + BATCH)

    return pl.pallas_call(
        _conv_kernel,
        grid=(n_tiles,),
        in_specs=[
            pl.BlockSpec((CONV_T, CONV_CHANNELS), lambda i: (i, 0)),
            pl.BlockSpec((HALO, CONV_CHANNELS), lambda i: (jnp.maximum(i * halo_per_tile - 1, 0), 0)),
            pl.BlockSpec((1, HALO, CONV_CHANNELS), lambda i: (seq_of(i), 0, 0)),
            pl.BlockSpec((CONV_WIDTH, CONV_CHANNELS), lambda i: (0, 0)),
            pl.BlockSpec((1, CONV_CHANNELS), lambda i: (0, 0)),
            pl.BlockSpec((1, CONV_CHANNELS), lambda i: (0, 0)),
            pl.BlockSpec((1, CONV_CHANNELS), lambda i: (0, 0)),
        ],
        out_specs=pl.BlockSpec((CONV_T, CONV_CHANNELS), lambda i: (i, 0)),
        out_shape=jax.ShapeDtypeStruct((n, CONV_CHANNELS), BF16),
        scratch_shapes=[pltpu.VMEM((HALO + CONV_T, CONV_CHANNELS), F32),
                        pltpu.VMEM((CONV_T, CONV_CHANNELS), F32)],
        compiler_params=_cparams(1),
        name="conv_module",
    )(glu, glu, hist, w_dw, b_dw, g_cn, b_cn)


ATT_TM = 256
_TAB_PROMPT_TILES = N_P // ATT_TM
_TAB_SEQ_TILES = SEQ // ATT_TM
_TAB_ROWS = SEQ + ATT_TM


def _tab_idx_new(i):
    return jnp.where(i < _TAB_PROMPT_TILES, i % _TAB_SEQ_TILES, _TAB_SEQ_TILES)


def _tab_idx_cache(i):
    return i % _TAB_SEQ_TILES


def _rope_pair(u, c, s):
    return u * c + pltpu.roll(u, 64, 1) * s


_Q_SCALE = math.log2(math.e) / math.sqrt(QK_DIM)


def _q_heads_kernel(ql_ref, w_ref, g_ref, c_ref, s_ref, o_ref):
    ql = ql_ref[...]
    g = g_ref[...]
    c = c_ref[...]
    s = s_ref[...]
    for h in range(N_HEADS):
        qf = _dot(ql, w_ref[:, h * HEAD_PAD:(h + 1) * HEAD_PAD])
        ssq = jnp.sum(qf * qf, axis=-1, keepdims=True)
        qn = qf * (lax.rsqrt(ssq * (1.0 / QK_DIM) + EPS) * _Q_SCALE) * g
        o_ref[h, :, :NOPE_DIM] = qn[:, :NOPE_DIM].astype(BF16)
        o_ref[h, :, NOPE_DIM:] = _rope_pair(qn[:, NOPE_DIM:], c, s).astype(BF16)


def _q_heads(q_lat, w_q, g_q, cos_t, sin_t):
    n = q_lat.shape[0]
    return pl.pallas_call(
        _q_heads_kernel,
        grid=(n // ATT_TM,),
        in_specs=[
            pl.BlockSpec((ATT_TM, Q_LORA_RANK), lambda i: (i, 0)),
            pl.BlockSpec((Q_LORA_RANK, N_HEADS * HEAD_PAD), lambda i: (0, 0)),
            pl.BlockSpec((1, HEAD_PAD), lambda i: (0, 0)),
            pl.BlockSpec((ATT_TM, 128), lambda i: (_tab_idx_new(i), 0)),
            pl.BlockSpec((ATT_TM, 128), lambda i: (_tab_idx_new(i), 0)),
        ],
        out_specs=pl.BlockSpec((N_HEADS, ATT_TM, HEAD_PAD), lambda i: (0, i, 0)),
        out_shape=jax.ShapeDtypeStruct((N_HEADS, n, HEAD_PAD), BF16),
        compiler_params=_cparams(1),
        name="q_heads",
    )(q_lat, w_q, g_q, cos_t, sin_t)


def _kv_heads_kernel(kv_ref, kr_ref, w_ref, gn_ref, gr_ref, c_ref, s_ref, k_ref, v_ref):
    kv = kv_ref[...].astype(BF16)
    u = kr_ref[...]
    ssq_r = jnp.sum(u * u, axis=-1, keepdims=True)
    krot = _rope_pair(u * gr_ref[...], c_ref[...], s_ref[...])
    gn = gn_ref[...]
    for h in range(N_HEADS):
        z = _dot(kv, w_ref[:, h * HEAD_PAD:(h + 1) * HEAD_PAD])
        kn = z[:, :NOPE_DIM]
        ssq = jnp.sum(kn * kn, axis=-1, keepdims=True) + ssq_r
        scale = lax.rsqrt(ssq * (1.0 / QK_DIM) + EPS)
        k_ref[h, :, :NOPE_DIM] = (kn * scale * gn).astype(BF16)
        k_ref[h, :, NOPE_DIM:] = (krot * scale).astype(BF16)
        v_ref[h] = z[:, NOPE_DIM:].astype(BF16)


def _kv_heads(kv_lat, kr_pad, w_kv, g_kn_nope, g_kn_rope, cos_t, sin_t, tab_idx, name):
    n = kv_lat.shape[0]
    return pl.pallas_call(
        _kv_heads_kernel,
        grid=(n // ATT_TM,),
        in_specs=[
            pl.BlockSpec((ATT_TM, KV_LORA_RANK), lambda i: (i, 0)),
            pl.BlockSpec((ATT_TM, 128), lambda i: (i, 0)),
            pl.BlockSpec((KV_LORA_RANK, N_HEADS * HEAD_PAD), lambda i: (0, 0)),
            pl.BlockSpec((1, NOPE_DIM), lambda i: (0, 0)),
            pl.BlockSpec((1, 128), lambda i: (0, 0)),
            pl.BlockSpec((ATT_TM, 128), lambda i: (tab_idx(i), 0)),
            pl.BlockSpec((ATT_TM, 128), lambda i: (tab_idx(i), 0)),
        ],
        out_specs=[
            pl.BlockSpec((N_HEADS, ATT_TM, HEAD_PAD), lambda i: (0, i, 0)),
            pl.BlockSpec((N_HEADS, ATT_TM, V_DIM), lambda i: (0, i, 0)),
        ],
        out_shape=[
            jax.ShapeDtypeStruct((N_HEADS, n, HEAD_PAD), BF16),
            jax.ShapeDtypeStruct((N_HEADS, n, V_DIM), BF16),
        ],
        compiler_params=_cparams(1),
        name=name,
    )(kv_lat, kr_pad, w_kv, g_kn_nope, g_kn_rope, cos_t, sin_t)


_TQ = 512
_TKB = 512
_HB = 2


def _flash_prompt_kernel(q_ref, k_ref, v_ref, o_ref, m_ref, l_ref, acc_ref):
    qi = pl.program_id(2)
    m_ref[...] = jnp.full(m_ref.shape, NEG_INF, F32)
    l_ref[...] = jnp.zeros(l_ref.shape, F32)
    acc_ref[...] = jnp.zeros(acc_ref.shape, F32)
    nlb = _TKB // 128

    def step(ki, masked):
        start = pl.multiple_of(ki * _TKB, _TKB)
        for hh in range(_HB):
            k = k_ref[hh, pl.ds(start, _TKB), :]
            v = v_ref[hh, pl.ds(start, _TKB), :]
            s = lax.dot_general(q_ref[hh], k, (((1,), (1,)), ((), ())), preferred_element_type=F32)
            if masked:
                rc = lax.broadcasted_iota(I32, (_TQ, _TKB), 0) // CHUNK
                cc = lax.broadcasted_iota(I32, (_TQ, _TKB), 1) // CHUNK
                s = jnp.where(cc <= rc, s, NEG_INF)
            sb = [s[:, c * 128:(c + 1) * 128] for c in range(nlb)]
            bm = sb[0]
            for c in range(1, nlb):
                bm = jnp.maximum(bm, sb[c])
            m_prev = m_ref[hh]
            m_new = jnp.maximum(m_prev, jnp.max(bm, axis=-1, keepdims=True))
            alpha = jnp.exp2(m_prev - m_new)
            ps = [jnp.exp2(x - m_new) for x in sb]
            psum = ps[0]
            for c in range(1, nlb):
                psum = psum + ps[c]
            l_ref[hh] = alpha * l_ref[hh] + psum
            p = jnp.concatenate(ps, axis=1).astype(BF16)
            acc_ref[hh] = alpha * acc_ref[hh] + _dot(p, v)
            m_ref[hh] = m_new

    def body(ki, carry):
        step(ki, False)
        return carry

    lax.fori_loop(0, qi, body, 0)
    step(qi, True)
    for hh in range(_HB):
        l = jnp.sum(l_ref[hh], axis=-1, keepdims=True)
        o_ref[:, hh * V_DIM:(hh + 1) * V_DIM] = (acc_ref[hh] / l).astype(BF16)


def _flash_prompt(q, k, v):
    nq = SEQ // _TQ
    return pl.pallas_call(
        _flash_prompt_kernel,
        grid=(BATCH, N_HEADS // _HB, nq),
        in_specs=[
            pl.BlockSpec((_HB, _TQ, HEAD_PAD), lambda b, h, i: (h, b * nq + i, 0)),
            pl.BlockSpec((_HB, SEQ, HEAD_PAD), lambda b, h, i: (h, b, 0)),
            pl.BlockSpec((_HB, SEQ, V_DIM), lambda b, h, i: (h, b, 0)),
        ],
        out_specs=pl.BlockSpec((_TQ, _HB * V_DIM), lambda b, h, i: (b * nq + i, h)),
        out_shape=jax.ShapeDtypeStruct((N_TOK, N_HEADS * V_DIM), BF16),
        scratch_shapes=[pltpu.VMEM((_HB, _TQ, 128), F32), pltpu.VMEM((_HB, _TQ, 128), F32),
                        pltpu.VMEM((_HB, _TQ, V_DIM), F32)],
        compiler_params=_cparams(3),
        name="flash_prompt",
    )(q, k, v)


def _flash_sample_kernel(prev_ref, q_ref, kc_ref, vc_ref, kn_ref, vn_ref, o_ref):
    del prev_ref
    nt = (((1,), (1,)), ((), ()))
    for hh in range(_HB):
        q = q_ref[hh]
        s1 = lax.dot_general(q, kc_ref[hh], nt, preferred_element_type=F32)
        s2 = lax.dot_general(q, kn_ref[hh], nt, preferred_element_type=F32)
        m = jnp.maximum(jnp.max(s1, axis=-1, keepdims=True), jnp.max(s2, axis=-1, keepdims=True))
        p1 = jnp.exp2(s1 - m)
        p2 = jnp.exp2(s2 - m)
        l = jnp.sum(p1, axis=-1, keepdims=True) + jnp.sum(p2, axis=-1, keepdims=True)
        o = _dot(p1.astype(BF16), vc_ref[hh]) + _dot(p2.astype(BF16), vn_ref[hh])
        o_ref[:, hh * V_DIM:(hh + 1) * V_DIM] = (o / l).astype(BF16)


def _flash_sample(attn, q, k_cache, v_cache, k_new, v_new):
    assert (PAST_LEN + DEC_SEQ - 1) // CHUNK <= PAST_LEN // CHUNK
    blk0 = N_P // DEC_SEQ
    new = lambda b, h: (h, blk0 + b, 0)
    cache = lambda b, h: (h, b, 0)
    return pl.pallas_call(
        _flash_sample_kernel,
        grid=(DEC_BATCH, N_HEADS // _HB),
        in_specs=[
            pl.BlockSpec(memory_space=pl.ANY),
            pl.BlockSpec((_HB, DEC_SEQ, HEAD_PAD), new),
            pl.BlockSpec((_HB, PAST_LEN, HEAD_PAD), cache),
            pl.BlockSpec((_HB, PAST_LEN, V_DIM), cache),
            pl.BlockSpec((_HB, DEC_SEQ, HEAD_PAD), new),
            pl.BlockSpec((_HB, DEC_SEQ, V_DIM), new),
        ],
        out_specs=pl.BlockSpec((DEC_SEQ, _HB * V_DIM), lambda b, h: (blk0 + b, h)),
        out_shape=jax.ShapeDtypeStruct((N_TOK, N_HEADS * V_DIM), BF16),
        input_output_aliases={0: 0},
        compiler_params=_cparams(2),
        name="flash_sample",
    )(attn, q, k_cache, v_cache, k_new, v_new)


def _merge_kernel(h_ref, c_ref, a_ref, wga_ref, wgb_ref, bga_ref, bgb_ref, wc_ref, wo_ref, o_ref):
    h = h_ref[...]
    ga = _sigmoid(_dot(h, wga_ref[...]) + bga_ref[...])
    gb = _sigmoid(_dot(h, wgb_ref[...]) + bgb_ref[...])
    mix = ga * _dot(c_ref[...], wc_ref[...]) + gb * _dot(a_ref[...], wo_ref[...])
    o_ref[...] = mix.astype(BF16)


def _merge(h, c_act, attn, w_gate, b_gate, w_conv_out, w_o):
    n = h.shape[0]
    tn = 512
    nj = D_MODEL // tn
    row = lambda i, j: (i, 0)
    return pl.pallas_call(
        _merge_kernel,
        grid=(n // TM, nj),
        in_specs=[
            pl.BlockSpec((TM, D_MODEL), row),
            pl.BlockSpec((TM, CONV_CHANNELS), row),
            pl.BlockSpec((TM, N_HEADS * V_DIM), row),
            pl.BlockSpec((D_MODEL, tn), lambda i, j: (0, j)),
            pl.BlockSpec((D_MODEL, tn), lambda i, j: (0, j + nj)),
            pl.BlockSpec((1, tn), lambda i, j: (0, j)),
            pl.BlockSpec((1, tn), lambda i, j: (0, j + nj)),
            pl.BlockSpec((CONV_CHANNELS, tn), lambda i, j: (0, j)),
            pl.BlockSpec((N_HEADS * V_DIM, tn), lambda i, j: (0, j)),
        ],
        out_specs=pl.BlockSpec((TM, tn), lambda i, j: (i, j)),
        out_shape=jax.ShapeDtypeStruct((n, D_MODEL), BF16),
        compiler_params=_cparams(2),
        name="merge",
    )(h, c_act, attn, w_gate, w_gate, b_gate, b_gate, w_conv_out, w_o)


def _split_bf16(x):
    hi = x.astype(BF16)
    lo = (x - hi.astype(F32)).astype(BF16)
    return hi, lo


_HALF = D_MODEL // 2


def _pack_bf16_pair(a, b):
    ua = lax.bitcast_convert_type(a.astype(BF16).astype(F32), U32)
    ub = lax.bitcast_convert_type(b.astype(BF16).astype(F32), U32)
    return lax.bitcast_convert_type(ua | (ub >> 16), F32)


def _unpack_bf16_pair(w):
    w = lax.bitcast_convert_type(w, U32)
    a = lax.bitcast_convert_type(w & jnp.uint32(0xFFFF0000), F32).astype(BF16)
    b = lax.bitcast_convert_type(w << 16, F32).astype(BF16)
    return a, b


def _out_router_kernel(mix_ref, x_ref, w_ref, g_ref, wrh_ref, wrl_ref, br_ref,
                       x1_ref, hm_ref, idx_ref, gate_ref):
    x1 = x_ref[...] + _dot(mix_ref[...], w_ref[...])
    x1_ref[...] = x1
    hn = x1 * lax.rsqrt(jnp.mean(x1 * x1, axis=-1, keepdims=True) + EPS) * g_ref[...]
    hm_ref[...] = _pack_bf16_pair(hn[:, :_HALF], hn[:, _HALF:])
    hh, hl = _split_bf16(hn)
    logits = _dot(hh, wrh_ref[...]) + (_dot(hh, wrl_ref[...]) + _dot(hl, wrh_ref[...])) + br_ref[...]
    lane = lax.broadcasted_iota(I32, logits.shape, 1).astype(F32)
    vals = []
    idx_out = jnp.zeros(logits.shape, F32)
    for k in range(TOP_K):
        m = jnp.max(logits, axis=-1, keepdims=True)
        sel = jnp.min(jnp.where(logits == m, lane, 1e9), axis=-1, keepdims=True)
        vals.append(m)
        idx_out = jnp.where(lane == float(k), sel, idx_out)
        logits = jnp.where(lane == sel, -jnp.inf, logits)
    exps = [jnp.exp(v - vals[0]) for v in vals]
    denom = exps[0] + exps[1] + exps[2] + exps[3]
    gate_out = jnp.zeros(idx_out.shape, F32)
    for k in range(TOP_K):
        gate_out = jnp.where(lane == float(k), exps[k] / denom, gate_out)
    idx_ref[...] = idx_out.astype(I32)
    gate_ref[...] = gate_out


def _out_router(mix, x, w_out, g_ffn, wr_hi, wr_lo, b_r):
    n = x.shape[0]
    tm = 256
    const = lambda i: (0, 0)
    row = lambda i: (i, 0)
    return pl.pallas_call(
        _out_router_kernel,
        grid=(n // tm,),
        in_specs=[
            pl.BlockSpec((tm, D_MODEL), row),
            pl.BlockSpec((tm, D_MODEL), row),
            pl.BlockSpec((D_MODEL, D_MODEL), const),
            pl.BlockSpec((1, D_MODEL), const),
            pl.BlockSpec((D_MODEL, 128), const),
            pl.BlockSpec((D_MODEL, 128), const),
            pl.BlockSpec((1, 128), const),
        ],
        out_specs=[
            pl.BlockSpec((tm, D_MODEL), row),
            pl.BlockSpec((tm, _HALF), row),
            pl.BlockSpec((tm, 128), row),
            pl.BlockSpec((tm, 128), row),
        ],
        out_shape=[
            jax.ShapeDtypeStruct((n, D_MODEL), F32),
            jax.ShapeDtypeStruct((2 * n, _HALF), F32),
            jax.ShapeDtypeStruct((n, 128), I32),
            jax.ShapeDtypeStruct((n, 128), F32),
        ],
        compiler_params=_cparams(1),
        name="out_router",
    )(mix, x, w_out, g_ffn, wr_hi, wr_lo, b_r)


def _moe_up_kernel(se_ref, sn_ref, sb_ref, sv_ref, sf_ref,
                   x_ref, wg_ref, wu_ref, bg_ref, bu_ref, o_ref, wgb_ref, wub_ref):
    t = pl.program_id(0)

    @pl.when(sf_ref[t] == 1)
    def _():
        wgb_ref[...] = wg_ref[0].astype(BF16)
        wub_ref[...] = wu_ref[0].astype(BF16)

    @pl.when(sv_ref[t] == 1)
    def _():
        xa, xb = _unpack_bf16_pair(x_ref[...])
        g = _dot(xa, wgb_ref[:_HALF, :]) + _dot(xb, wgb_ref[_HALF:, :]) + bg_ref[0]
        u = _dot(xa, wub_ref[:_HALF, :]) + _dot(xb, wub_ref[_HALF:, :]) + bu_ref[0]
        g = jnp.minimum(g, SWIGLU_LIMIT)
        u = jnp.clip(u, -SWIGLU_LIMIT, SWIGLU_LIMIT)
        o_ref[...] = ((u + 1.0) * (g * _sigmoid(SWIGLU_ALPHA * g))).astype(BF16)


_UP_TN = 512
_UP_TILES = D_FF // _UP_TN
_DN_TN = 1024
_DN_TILES = D_MODEL // _DN_TN


def _moe_up(plan, xs, w_gu, b_gu):
    steps = plan[0].shape[0]
    return pl.pallas_call(
        _moe_up_kernel,
        grid_spec=pltpu.PrefetchScalarGridSpec(
            num_scalar_prefetch=5,
            grid=(steps,),
            in_specs=[
                pl.BlockSpec((MOE_BLK, _HALF), lambda t, se, sn, sb, sv, sf: (sb[t], 0)),
                pl.BlockSpec((1, D_MODEL, _UP_TN), lambda t, se, sn, sb, sv, sf: (se[t], 0, sn[t])),
                pl.BlockSpec((1, D_MODEL, _UP_TN), lambda t, se, sn, sb, sv, sf: (se[t], 0, sn[t] + _UP_TILES)),
                pl.BlockSpec((1, 1, _UP_TN), lambda t, se, sn, sb, sv, sf: (se[t], 0, sn[t])),
                pl.BlockSpec((1, 1, _UP_TN), lambda t, se, sn, sb, sv, sf: (se[t], 0, sn[t] + _UP_TILES)),
            ],
            out_specs=pl.BlockSpec((MOE_BLK, _UP_TN), lambda t, se, sn, sb, sv, sf: (sb[t], sn[t])),
            scratch_shapes=[pltpu.VMEM((D_MODEL, _UP_TN), BF16), pltpu.VMEM((D_MODEL, _UP_TN), BF16)],
        ),
        out_shape=jax.ShapeDtypeStruct((MOE_ROWS, D_FF), BF16),
        compiler_params=_cparams(1),
        name="moe_up",
    )(*plan, xs, w_gu, w_gu, b_gu, b_gu)


def _moe_down_kernel(se_ref, sn_ref, sb_ref, sv_ref, sf_ref, a_ref, w_ref, b_ref, o_ref, wb_ref):
    t = pl.program_id(0)

    @pl.when(sf_ref[t] == 1)
    def _():
        wb_ref[...] = w_ref[0].astype(BF16)

    @pl.when(sv_ref[t] == 1)
    def _():
        o_ref[...] = _dot(a_ref[...], wb_ref[...]) + b_ref[0]


def _moe_down(plan, act, w_dn, b_dn):
    steps = plan[0].shape[0]
    return pl.pallas_call(
        _moe_down_kernel,
        grid_spec=pltpu.PrefetchScalarGridSpec(
            num_scalar_prefetch=5,
            grid=(steps,),
            in_specs=[
                pl.BlockSpec((MOE_BLK, D_FF), lambda t, se, sn, sb, sv, sf: (sb[t], 0)),
                pl.BlockSpec((1, D_FF, _DN_TN), lambda t, se, sn, sb, sv, sf: (se[t], 0, sn[t])),
                pl.BlockSpec((1, 1, _DN_TN), lambda t, se, sn, sb, sv, sf: (se[t], 0, sn[t])),
            ],
            out_specs=pl.BlockSpec((MOE_BLK, _DN_TN), lambda t, se, sn, sb, sv, sf: (sb[t], sn[t])),
            scratch_shapes=[pltpu.VMEM((D_FF, _DN_TN), BF16)],
        ),
        out_shape=jax.ShapeDtypeStruct((MOE_ROWS, D_MODEL), F32),
        compiler_params=_cparams(1),
        name="moe_down",
    )(*plan, act, w_dn, b_dn)


def _moe_dispatch(top_idx):
    n_asg = N_TOK * TOP_K
    flat_e = top_idx.reshape(-1)
    onehot = (flat_e[:, None] == jnp.arange(N_EXPERTS, dtype=I32)[None, :]).astype(I32)
    csum = jnp.cumsum(onehot, axis=0)
    counts = csum[-1]
    rank = jnp.sum(csum * onehot, axis=1) - 1
    nblk = (counts + MOE_BLK - 1) // MOE_BLK
    blk_start = jnp.cumsum(nblk) - nblk
    dest = jnp.sum(onehot * blk_start[None, :], axis=1) * MOE_BLK + rank
    row_tok = jnp.zeros((MOE_ROWS,), I32).at[dest].set(jnp.arange(n_asg, dtype=I32) // TOP_K,
                                                        mode="promise_in_bounds", unique_indices=True)
    return dest, row_tok, nblk, blk_start


def _moe_steps(nblk, blk_start, n_tiles):
    t_max = n_tiles * MOE_MAX_BLKS
    per_e = nblk * n_tiles
    s_end = jnp.cumsum(per_e)
    total = s_end[-1]
    t = jnp.arange(t_max, dtype=I32)
    tc = jnp.minimum(t, total - 1)
    e = jnp.minimum(jnp.sum((s_end[None, :] <= tc[:, None]).astype(I32), axis=1), N_EXPERTS - 1)
    sel = (e[:, None] == jnp.arange(N_EXPERTS, dtype=I32)[None, :]).astype(I32)
    pick = lambda v: jnp.sum(sel * v[None, :], axis=1)
    local = tc - pick(s_end - per_e)
    nb = jnp.maximum(pick(nblk), 1)
    n = local // nb
    r = local % nb
    blk = pick(blk_start) + r
    valid = t < total
    first = jnp.logical_and(valid, r == 0)
    return e, n, blk, valid.astype(I32), first.astype(I32)


_FIN_TM = 256
_FIN_TN = 512


def _final_kernel(x1_ref, y0_ref, y1_ref, y2_ref, y3_ref, gate_ref, g_ref, wg_ref, p_ref, wp_ref,
                  o_ref, x2_ref, hp_ref):
    j = pl.program_id(1)

    @pl.when(j == 0)
    def _():
        gate = gate_ref[...]
        moe = (y0_ref[0] * gate[:, 0:1] + y1_ref[0] * gate[:, 1:2]
               + y2_ref[0] * gate[:, 2:3] + y3_ref[0] * gate[:, 3:4])
        x2 = x1_ref[...] + moe
        x2_ref[...] = x2
        hp = x2 * lax.rsqrt(jnp.mean(x2 * x2, axis=-1, keepdims=True) + EPS) * g_ref[...]
        hp_ref[...] = hp.astype(BF16)

    col = pl.multiple_of(j * _FIN_TN, _FIN_TN)
    emb = _dot(p_ref[...].astype(BF16), wp_ref[...])
    o_ref[...] = x2_ref[:, pl.ds(col, _FIN_TN)] + _sigmoid(_dot(hp_ref[...], wg_ref[...])) * emb


def _final(x1, y4, gate, g_ple, w_ple_gate, p, w_ple, row0, name):
    n = p.shape[0]
    t0 = row0 // _FIN_TM
    row = lambda i, j: (t0 + i, 0)
    yspec = lambda k: pl.BlockSpec((1, _FIN_TM, D_MODEL), lambda i, j: (k, t0 + i, 0))
    return pl.pallas_call(
        _final_kernel,
        grid=(n // _FIN_TM, D_MODEL // _FIN_TN),
        in_specs=[
            pl.BlockSpec((_FIN_TM, D_MODEL), row),
            yspec(0), yspec(1), yspec(2), yspec(3),
            pl.BlockSpec((_FIN_TM, 128), row),
            pl.BlockSpec((1, D_MODEL), lambda i, j: (0, 0)),
            pl.BlockSpec((D_MODEL, _FIN_TN), lambda i, j: (0, j)),
            pl.BlockSpec((_FIN_TM, PLE_DIM), lambda i, j: (i, 0)),
            pl.BlockSpec((PLE_DIM, _FIN_TN), lambda i, j: (0, j)),
        ],
        out_specs=pl.BlockSpec((_FIN_TM, _FIN_TN), lambda i, j: (i, j)),
        out_shape=jax.ShapeDtypeStruct((n, D_MODEL), F32),
        scratch_shapes=[pltpu.VMEM((_FIN_TM, D_MODEL), F32), pltpu.VMEM((_FIN_TM, D_MODEL), BF16)],
        compiler_params=_cparams(2),
        name=name,
    )(x1, y4, y4, y4, y4, gate, g_ple, w_ple_gate, p, w_ple)


def _rope_layout(x):
    half = ROPE_DIM // 2
    z = jnp.zeros(x.shape[:-1] + (half,), x.dtype)
    return jnp.concatenate([x[..., :half], z, x[..., half:], z], axis=-1)


def _rope_tables():
    half = ROPE_DIM // 2
    inv_freq = ROPE_THETA ** (-jnp.arange(half, dtype=F32) / half)
    pos = jnp.arange(PAST_LEN + DEC_SEQ, dtype=I32)
    ang = pos.astype(F32)[:, None] * inv_freq[None, :]
    cos, sin = jnp.cos(ang), jnp.sin(ang)
    z = jnp.zeros_like(cos)
    c = jnp.concatenate([cos, z, cos, z], axis=-1)
    s = jnp.concatenate([-sin, z, sin, z], axis=-1)
    rep = ATT_TM // DEC_SEQ
    return (jnp.concatenate([c[:SEQ], jnp.tile(c[PAST_LEN:], (rep, 1))], axis=0),
            jnp.concatenate([s[:SEQ], jnp.tile(s[PAST_LEN:], (rep, 1))], axis=0))


def _layer(x, p_prompt, p_sample, cache_kv, cache_kr, state_conv,
           g_mix, w_in, b_gate, w_dw, b_dw, g_cn, b_cn, w_conv_out,
           g_qa, g_kva, w_qb, w_kb, w_vb, g_qn, g_kn, w_o, w_out,
           g_ffn, w_router, b_router, w_gu, b_gu, w_dn, b_dn,
           g_ple, w_ple_gate, w_ple):
    assert SEQ == PAST_LEN
    row = lambda v: v.reshape(1, -1)
    w_in_b = w_in.astype(BF16)
    w_mid = jnp.concatenate([w_in_b[:, O_U:O_KV], _rope_layout(w_in_b[:, O_KV:O_KR])], axis=1)
    w_gate = w_in_b[:, O_KR:]

    h, q_lat, kv_new, kr_pad = _in_mid(x, row(g_mix), w_mid, row(g_qa), row(g_kva))
    half = ROPE_DIM // 2
    kr_new = jnp.concatenate([kr_pad[:, :half], kr_pad[:, 2 * half:3 * half]], axis=1)
    glu = _in_glu(h, w_in_b)

    hist = jnp.concatenate([jnp.zeros((BATCH, HALO, CONV_CHANNELS), F32),
                            jnp.pad(state_conv, ((0, 0), (HALO - (CONV_WIDTH - 1), 0), (0, 0)))], axis=0)
    c_act = _conv_module(glu, hist, w_dw, row(b_dw), row(g_cn), row(b_cn))

    cos_t, sin_t = _rope_tables()
    w_q = jnp.concatenate([w_qb[..., :NOPE_DIM], _rope_layout(w_qb[..., NOPE_DIM:])], axis=-1)
    w_q = w_q.reshape(Q_LORA_RANK, N_HEADS * HEAD_PAD).astype(BF16)
    g_q = jnp.concatenate([g_qn[:NOPE_DIM], _rope_layout(g_qn[NOPE_DIM:])]).reshape(1, HEAD_PAD)
    q = _q_heads(q_lat, w_q, g_q, cos_t, sin_t)

    w_kv = jnp.concatenate([w_kb, w_vb], axis=-1).reshape(KV_LORA_RANK, N_HEADS * HEAD_PAD).astype(BF16)
    g_kn_nope = g_kn[:NOPE_DIM].reshape(1, NOPE_DIM)
    g_kn_rope = _rope_layout(g_kn[NOPE_DIM:]).reshape(1, 128)
    k_new, v_new = _kv_heads(kv_new, kr_pad, w_kv, g_kn_nope, g_kn_rope, cos_t, sin_t,
                             _tab_idx_new, "kv_heads_new")
    k_cache, v_cache = _kv_heads(cache_kv.reshape(DEC_BATCH * PAST_LEN, KV_LORA_RANK),
                                 _rope_layout(cache_kr).reshape(DEC_BATCH * PAST_LEN, 128),
                                 w_kv, g_kn_nope, g_kn_rope, cos_t, sin_t, _tab_idx_cache, "kv_heads_cache")

    attn = _flash_prompt(q, k_new, v_new)
    attn = _flash_sample(attn, q, k_cache, v_cache, k_new, v_new)

    mix = _merge(h, c_act, attn, w_gate, row(b_gate), w_conv_out.astype(BF16), w_o.astype(BF16))

    wr = jnp.pad(w_router, ((0, 0), (0, 128 - N_EXPERTS)))
    wr_hi, wr_lo = _split_bf16(wr)
    b_r = jnp.concatenate([b_router, jnp.full((128 - N_EXPERTS,), -jnp.inf, F32)]).reshape(1, 128)
    x1, hm, idx_pad, gate_pad = _out_router(mix, x, w_out.astype(BF16), row(g_ffn), wr_hi, wr_lo, b_r)

    top_idx = idx_pad[:, :TOP_K]
    dest, row_tok, nblk, blk_start = _moe_dispatch(top_idx)
    xs = hm.at[row_tok].get(mode="promise_in_bounds")
    act = _moe_up(_moe_steps(nblk, blk_start, _UP_TILES), xs, w_gu, b_gu.reshape(N_EXPERTS, 1, 2 * D_FF))
    ys = _moe_down(_moe_steps(nblk, blk_start, _DN_TILES), act, w_dn, b_dn.reshape(N_EXPERTS, 1, D_MODEL))
    y4 = ys.at[dest.reshape(N_TOK, TOP_K).T].get(mode="promise_in_bounds")

    fin = (row(g_ple), w_ple_gate.astype(BF16))
    out_p = _final(x1, y4, gate_pad, *fin, p_prompt, w_ple.astype(BF16), 0, "final_prompt")
    out_s = _final(x1, y4, gate_pad, *fin, p_sample, w_ple.astype(BF16), N_P, "final_sample")
    return out_p, out_s, kv_new, kr_new, glu


def kernel(x_prompt, x_sample, cache_kv_latent, cache_k_rope, state_conv, p_prompt, p_sample, g_mix, w_in, b_gate, w_dw, b_dw, g_cn, b_cn, w_conv_out, g_qa, g_kva, w_qb, w_kb, w_vb, g_qn, g_kn, w_o, w_out, g_ffn, w_router, b_router, w_gu, b_gu, w_dn, b_dn, g_ple, w_ple_gate, w_ple):
    assert g_mix.shape[0] == 1
    x = jnp.concatenate([x_prompt.reshape(N_P, D_MODEL), x_sample.reshape(N_S, D_MODEL)], axis=0)
    out_p, out_s, kv_new, kr_new, glu = _layer(
        x, p_prompt[0].reshape(N_P, PLE_DIM), p_sample[0].reshape(N_S, PLE_DIM),
        cache_kv_latent[0], cache_k_rope[0], state_conv[0],
        g_mix[0], w_in[0], b_gate[0], w_dw[0], b_dw[0], g_cn[0], b_cn[0], w_conv_out[0],
        g_qa[0], g_kva[0], w_qb[0], w_kb[0], w_vb[0], g_qn[0], g_kn[0], w_o[0], w_out[0],
        g_ffn[0], w_router[0], b_router[0], w_gu[0], b_gu[0], w_dn[0], b_dn[0],
        g_ple[0], w_ple_gate[0], w_ple[0])
    tail = CONV_WIDTH - 1
    conv_p = jnp.stack([glu[(b + 1) * SEQ - tail:(b + 1) * SEQ] for b in range(BATCH)])
    conv_s = glu[N_P:].reshape(DEC_BATCH, DEC_SEQ, CONV_CHANNELS)[:, DEC_SEQ - tail:]
    return (out_p.reshape(BATCH, SEQ, D_MODEL),
            out_s.reshape(DEC_BATCH, DEC_SEQ, D_MODEL),
            kv_new[:N_P].reshape(1, BATCH, SEQ, KV_LORA_RANK),
            kr_new[:N_P].reshape(1, BATCH, SEQ, ROPE_DIM),
            conv_p[None],
            kv_new[N_P:].reshape(1, DEC_BATCH, DEC_SEQ, KV_LORA_RANK),
            kr_new[N_P:].reshape(1, DEC_BATCH, DEC_SEQ, ROPE_DIM),
            conv_s[None])
```

```python
import functools
import math

import jax
import jax.numpy as jnp
from jax import lax
from jax.experimental import pallas as pl
from jax.experimental.pallas import tpu as pltpu

F32 = jnp.float32
BF16 = jnp.bfloat16
I32 = jnp.int32
U32 = jnp.uint32

D_MODEL = 2048
BATCH = 2
SEQ = 4096
DEC_BATCH = 8
DEC_SEQ = 64
PAST_LEN = 4096
CHUNK = 64
CONV_CHANNELS = D_MODEL
CONV_WIDTH = 31
N_HEADS = 16
Q_LORA_RANK = 512
KV_LORA_RANK = 512
NOPE_DIM = 128
ROPE_DIM = 64
QK_DIM = NOPE_DIM + ROPE_DIM
V_DIM = 128
ROPE_THETA = 10000.0
N_EXPERTS = 32
TOP_K = 4
D_FF = D_MODEL
SWIGLU_ALPHA = 1.702
SWIGLU_LIMIT = 7.0
PLE_DIM = 256
EPS = 1e-6
NEG_INF = -1e30

N_P = BATCH * SEQ
N_S = DEC_BATCH * DEC_SEQ
N_TOK = N_P + N_S
O_U = 2 * CONV_CHANNELS
O_Q = O_U + Q_LORA_RANK
O_KV = O_Q + KV_LORA_RANK
O_KR = O_KV + ROPE_DIM
MID_W = 1152
HEAD_PAD = 256

TM = 512
CONV_T = 64
HALO = 32
MOE_BLK = 512
MOE_MAX_BLKS = (N_TOK * TOP_K) // MOE_BLK + N_EXPERTS
MOE_ROWS = MOE_MAX_BLKS * MOE_BLK
VMEM_LIMIT = 48 * 1024 * 1024


def _cparams(n_axes):
    return pltpu.CompilerParams(dimension_semantics=("arbitrary",) * n_axes,
                                vmem_limit_bytes=VMEM_LIMIT)


def _sigmoid(x):
    return 1.0 / (1.0 + jnp.exp(-x))


def _dot(a, b):
    return jnp.dot(a, b, preferred_element_type=F32)


def _in_mid_kernel(x_ref, g_ref, w_ref, gqa_ref, gkva_ref, h_ref, q_ref, kv_ref, kr_ref):
    x = x_ref[...]
    h = x * lax.rsqrt(jnp.mean(x * x, axis=-1, keepdims=True) + EPS) * g_ref[...]
    hb = h.astype(BF16)
    h_ref[...] = hb
    z = _dot(hb, w_ref[...])
    ql = z[:, :Q_LORA_RANK]
    kvl = z[:, Q_LORA_RANK:Q_LORA_RANK + KV_LORA_RANK]
    qn = ql * lax.rsqrt(jnp.mean(ql * ql, axis=-1, keepdims=True) + EPS) * gqa_ref[...]
    q_ref[...] = qn.astype(BF16)
    kv_ref[...] = kvl * lax.rsqrt(jnp.mean(kvl * kvl, axis=-1, keepdims=True) + EPS) * gkva_ref[...]
    kr_ref[...] = z[:, Q_LORA_RANK + KV_LORA_RANK:]


def _in_mid(x, g_mix, w_mid, g_qa, g_kva):
    n = x.shape[0]
    return pl.pallas_call(
        _in_mid_kernel,
        grid=(n // TM,),
        in_specs=[
            pl.BlockSpec((TM, D_MODEL), lambda i: (i, 0)),
            pl.BlockSpec((1, D_MODEL), lambda i: (0, 0)),
            pl.BlockSpec((D_MODEL, MID_W), lambda i: (0, 0)),
            pl.BlockSpec((1, Q_LORA_RANK), lambda i: (0, 0)),
            pl.BlockSpec((1, KV_LORA_RANK), lambda i: (0, 0)),
        ],
        out_specs=[
            pl.BlockSpec((TM, D_MODEL), lambda i: (i, 0)),
            pl.BlockSpec((TM, Q_LORA_RANK), lambda i: (i, 0)),
            pl.BlockSpec((TM, KV_LORA_RANK), lambda i: (i, 0)),
            pl.BlockSpec((TM, 128), lambda i: (i, 0)),
        ],
        out_shape=[
            jax.ShapeDtypeStruct((n, D_MODEL), BF16),
            jax.ShapeDtypeStruct((n, Q_LORA_RANK), BF16),
            jax.ShapeDtypeStruct((n, KV_LORA_RANK), F32),
            jax.ShapeDtypeStruct((n, 128), F32),
        ],
        compiler_params=_cparams(1),
        name="in_mid",
    )(x, g_mix, w_mid, g_qa, g_kva)


def _glu_kernel(h_ref, w1_ref, w2_ref, o_ref):
    h = h_ref[...]
    o_ref[...] = _dot(h, w1_ref[...]) * _sigmoid(_dot(h, w2_ref[...]))


def _in_glu(h, w_in_b):
    n = h.shape[0]
    tn = 512
    nj = CONV_CHANNELS // tn
    return pl.pallas_call(
        _glu_kernel,
        grid=(n // TM, nj),
        in_specs=[
            pl.BlockSpec((TM, D_MODEL), lambda i, j: (i, 0)),
            pl.BlockSpec((D_MODEL, tn), lambda i, j: (0, j)),
            pl.BlockSpec((D_MODEL, tn), lambda i, j: (0, j + nj)),
        ],
        out_specs=pl.BlockSpec((TM, tn), lambda i, j: (i, j)),
        out_shape=jax.ShapeDtypeStruct((n, CONV_CHANNELS), F32),
        compiler_params=_cparams(2),
        name="in_glu",
    )(h, w_in_b, w_in_b)


_CONV_TILES_PER_SEQ = SEQ // CONV_T
_CONV_PROMPT_TILES = N_P // CONV_T
_CONV_LANES = 512
SUBLANES = 8
_SHIFT_ROWS = (HALO // SUBLANES - 1) * SUBLANES + CONV_T


def _conv_kernel(cur_ref, prev_ref, hist_ref, w_ref, bdw_ref, g_ref, b_ref, o_ref, win_ref, conv_ref, shift_ref):
    i = pl.program_id(0)
    first = jnp.logical_or(i >= _CONV_PROMPT_TILES, i % _CONV_TILES_PER_SEQ == 0)

    @pl.when(first)
    def _():
        win_ref[0:HALO, :] = hist_ref[0]

    @pl.when(jnp.logical_not(first))
    def _():
        win_ref[0:HALO, :] = prev_ref[...]

    win_ref[HALO:HALO + CONV_T, :] = cur_ref[...]
    for r in range(1, SUBLANES):
        shift_ref[r - 1] = win_ref[r:r + _SHIFT_ROWS, :]
    base = HALO - (CONV_WIDTH - 1)
    for c in range(0, CONV_CHANNELS, _CONV_LANES):
        acc = jnp.zeros((CONV_T, _CONV_LANES), F32)
        for k in range(CONV_WIDTH):
            q, r = divmod(base + k, SUBLANES)
            lanes = slice(c, c + _CONV_LANES)
            rows = slice(q * SUBLANES, q * SUBLANES + CONV_T)
            src = win_ref[rows, lanes] if r == 0 else shift_ref[r - 1, rows, lanes]
            acc = acc + w_ref[k:k + 1, lanes] * src
        conv_ref[:, c:c + _CONV_LANES] = acc + bdw_ref[:, c:c + _CONV_LANES]
    y = conv_ref[...]
    yc = y - jnp.mean(y, axis=-1, keepdims=True)
    var = jnp.mean(yc * yc, axis=-1, keepdims=True)
    z = yc * lax.rsqrt(var + EPS) * g_ref[...] + b_ref[...]
    o_ref[...] = (z * _sigmoid(z)).astype(BF16)


def _conv_module(glu, hist, w_dw, b_dw, g_cn, b_cn):
    n = glu.shape[0]
    n_tiles = n // CONV_T
    halo_per_tile = CONV_T // HALO

    def seq_of(i):
        return jnp.where(i < _CONV_PROMPT_TILES, i // _CONV_TILES_PER_SEQ, i - _CONV_PROMPT_TILES + BATCH)

    return pl.pallas_call(
        _conv_kernel,
        grid=(n_tiles,),
        in_specs=[
            pl.BlockSpec((CONV_T, CONV_CHANNELS), lambda i: (i, 0)),
            pl.BlockSpec((HALO, CONV_CHANNELS), lambda i: (jnp.maximum(i * halo_per_tile - 1, 0), 0)),
            pl.BlockSpec((1, HALO, CONV_CHANNELS), lambda i: (seq_of(i), 0, 0)),
            pl.BlockSpec((CONV_WIDTH, CONV_CHANNELS), lambda i: (0, 0)),
            pl.BlockSpec((1, CONV_CHANNELS), lambda i: (0, 0)),
            pl.BlockSpec((1, CONV_CHANNELS), lambda i: (0, 0)),
            pl.BlockSpec((1, CONV_CHANNELS), lambda i: (0, 0)),
        ],
        out_specs=pl.BlockSpec((CONV_T, CONV_CHANNELS), lambda i: (i, 0)),
        out_shape=jax.ShapeDtypeStruct((n, CONV_CHANNELS), BF16),
        scratch_shapes=[pltpu.VMEM((HALO + CONV_T, CONV_CHANNELS), F32),
                        pltpu.VMEM((CONV_T, CONV_CHANNELS), F32),
                        pltpu.VMEM((SUBLANES - 1, _SHIFT_ROWS, CONV_CHANNELS), F32)],
        compiler_params=_cparams(1),
        name="conv_module",
    )(glu, glu, hist, w_dw, b_dw, g_cn, b_cn)


ATT_TM = 256
_TAB_PROMPT_TILES = N_P // ATT_TM
_TAB_SEQ_TILES = SEQ // ATT_TM
_TAB_ROWS = SEQ + ATT_TM


def _tab_idx_new(i):
    return jnp.where(i < _TAB_PROMPT_TILES, i % _TAB_SEQ_TILES, _TAB_SEQ_TILES)


def _tab_idx_cache(i):
    return i % _TAB_SEQ_TILES


def _rope_pair(u, c, s):
    return u * c + pltpu.roll(u, 64, 1) * s


_Q_SCALE = math.log2(math.e) / math.sqrt(QK_DIM)


def _q_heads_kernel(ql_ref, w_ref, g_ref, c_ref, s_ref, o_ref):
    ql = ql_ref[...]
    g = g_ref[...]
    c = c_ref[...]
    s = s_ref[...]
    for h in range(N_HEADS):
        qf = _dot(ql, w_ref[:, h * HEAD_PAD:(h + 1) * HEAD_PAD])
        ssq = jnp.sum(qf * qf, axis=-1, keepdims=True)
        qn = qf * (lax.rsqrt(ssq * (1.0 / QK_DIM) + EPS) * _Q_SCALE) * g
        o_ref[h, :, :NOPE_DIM] = qn[:, :NOPE_DIM].astype(BF16)
        o_ref[h, :, NOPE_DIM:] = _rope_pair(qn[:, NOPE_DIM:], c, s).astype(BF16)


def _q_heads(q_lat, w_q, g_q, cos_t, sin_t):
    n = q_lat.shape[0]
    return pl.pallas_call(
        _q_heads_kernel,
        grid=(n // ATT_TM,),
        in_specs=[
            pl.BlockSpec((ATT_TM, Q_LORA_RANK), lambda i: (i, 0)),
            pl.BlockSpec((Q_LORA_RANK, N_HEADS * HEAD_PAD), lambda i: (0, 0)),
            pl.BlockSpec((1, HEAD_PAD), lambda i: (0, 0)),
            pl.BlockSpec((ATT_TM, 128), lambda i: (_tab_idx_new(i), 0)),
            pl.BlockSpec((ATT_TM, 128), lambda i: (_tab_idx_new(i), 0)),
        ],
        out_specs=pl.BlockSpec((N_HEADS, ATT_TM, HEAD_PAD), lambda i: (0, i, 0)),
        out_shape=jax.ShapeDtypeStruct((N_HEADS, n, HEAD_PAD), BF16),
        compiler_params=_cparams(1),
        name="q_heads",
    )(q_lat, w_q, g_q, cos_t, sin_t)


def _kv_heads_kernel(kv_ref, kr_ref, w_ref, gn_ref, gr_ref, c_ref, s_ref, k_ref, v_ref):
    kv = kv_ref[...].astype(BF16)
    u = kr_ref[...]
    ssq_r = jnp.sum(u * u, axis=-1, keepdims=True)
    krot = _rope_pair(u * gr_ref[...], c_ref[...], s_ref[...])
    gn = gn_ref[...]
    for h in range(N_HEADS):
        z = _dot(kv, w_ref[:, h * HEAD_PAD:(h + 1) * HEAD_PAD])
        kn = z[:, :NOPE_DIM]
        ssq = jnp.sum(kn * kn, axis=-1, keepdims=True) + ssq_r
        scale = lax.rsqrt(ssq * (1.0 / QK_DIM) + EPS)
        k_ref[h, :, :NOPE_DIM] = (kn * scale * gn).astype(BF16)
        k_ref[h, :, NOPE_DIM:] = (krot * scale).astype(BF16)
        v_ref[h] = z[:, NOPE_DIM:].astype(BF16)


def _kv_heads(kv_lat, kr_pad, w_kv, g_kn_nope, g_kn_rope, cos_t, sin_t, tab_idx, name):
    n = kv_lat.shape[0]
    return pl.pallas_call(
        _kv_heads_kernel,
        grid=(n // ATT_TM,),
        in_specs=[
            pl.BlockSpec((ATT_TM, KV_LORA_RANK), lambda i: (i, 0)),
            pl.BlockSpec((ATT_TM, 128), lambda i: (i, 0)),
            pl.BlockSpec((KV_LORA_RANK, N_HEADS * HEAD_PAD), lambda i: (0, 0)),
            pl.BlockSpec((1, NOPE_DIM), lambda i: (0, 0)),
            pl.BlockSpec((1, 128), lambda i: (0, 0)),
            pl.BlockSpec((ATT_TM, 128), lambda i: (tab_idx(i), 0)),
            pl.BlockSpec((ATT_TM, 128), lambda i: (tab_idx(i), 0)),
        ],
        out_specs=[
            pl.BlockSpec((N_HEADS, ATT_TM, HEAD_PAD), lambda i: (0, i, 0)),
            pl.BlockSpec((N_HEADS, ATT_TM, V_DIM), lambda i: (0, i, 0)),
        ],
        out_shape=[
            jax.ShapeDtypeStruct((N_HEADS, n, HEAD_PAD), BF16),
            jax.ShapeDtypeStruct((N_HEADS, n, V_DIM), BF16),
        ],
        compiler_params=_cparams(1),
        name=name,
    )(kv_lat, kr_pad, w_kv, g_kn_nope, g_kn_rope, cos_t, sin_t)


_TQ = 512
_TKB = 512
_HB = 2


def _flash_prompt_kernel(q_ref, k_ref, v_ref, o_ref, m_ref, l_ref, acc_ref):
    qi = pl.program_id(2)
    m_ref[...] = jnp.full(m_ref.shape, NEG_INF, F32)
    l_ref[...] = jnp.zeros(l_ref.shape, F32)
    acc_ref[...] = jnp.zeros(acc_ref.shape, F32)
    nlb = _TKB // 128

    def step(ki, masked):
        start = pl.multiple_of(ki * _TKB, _TKB)
        for hh in range(_HB):
            k = k_ref[hh, pl.ds(start, _TKB), :]
            v = v_ref[hh, pl.ds(start, _TKB), :]
            s = lax.dot_general(q_ref[hh], k, (((1,), (1,)), ((), ())), preferred_element_type=F32)
            if masked:
                rc = lax.broadcasted_iota(I32, (_TQ, _TKB), 0) // CHUNK
                cc = lax.broadcasted_iota(I32, (_TQ, _TKB), 1) // CHUNK
                s = jnp.where(cc <= rc, s, NEG_INF)
            sb = [s[:, c * 128:(c + 1) * 128] for c in range(nlb)]
            bm = sb[0]
            for c in range(1, nlb):
                bm = jnp.maximum(bm, sb[c])
            m_prev = m_ref[hh]
            m_new = jnp.maximum(m_prev, jnp.max(bm, axis=-1, keepdims=True))
            alpha = jnp.exp2(m_prev - m_new)
            ps = [jnp.exp2(x - m_new) for x in sb]
            psum = ps[0]
            for c in range(1, nlb):
                psum = psum + ps[c]
            l_ref[hh] = alpha * l_ref[hh] + psum
            p = jnp.concatenate(ps, axis=1).astype(BF16)
            acc_ref[hh] = alpha * acc_ref[hh] + _dot(p, v)
            m_ref[hh] = m_new

    def body(ki, carry):
        step(ki, False)
        return carry

    lax.fori_loop(0, qi, body, 0)
    step(qi, True)
    for hh in range(_HB):
        l = jnp.sum(l_ref[hh], axis=-1, keepdims=True)
        o_ref[:, hh * V_DIM:(hh + 1) * V_DIM] = (acc_ref[hh] / l).astype(BF16)


def _flash_prompt(q, k, v):
    nq = SEQ // _TQ
    return pl.pallas_call(
        _flash_prompt_kernel,
        grid=(BATCH, N_HEADS // _HB, nq),
        in_specs=[
            pl.BlockSpec((_HB, _TQ, HEAD_PAD), lambda b, h, i: (h, b * nq + i, 0)),
            pl.BlockSpec((_HB, SEQ, HEAD_PAD), lambda b, h, i: (h, b, 0)),
            pl.BlockSpec((_HB, SEQ, V_DIM), lambda b, h, i: (h, b, 0)),
        ],
        out_specs=pl.BlockSpec((_TQ, _HB * V_DIM), lambda b, h, i: (b * nq + i, h)),
        out_shape=jax.ShapeDtypeStruct((N_TOK, N_HEADS * V_DIM), BF16),
        scratch_shapes=[pltpu.VMEM((_HB, _TQ, 128), F32), pltpu.VMEM((_HB, _TQ, 128), F32),
                        pltpu.VMEM((_HB, _TQ, V_DIM), F32)],
        compiler_params=_cparams(3),
        name="flash_prompt",
    )(q, k, v)


def _flash_sample_kernel(prev_ref, q_ref, kc_ref, vc_ref, kn_ref, vn_ref, o_ref):
    del prev_ref
    nt = (((1,), (1,)), ((), ()))
    for hh in range(_HB):
        q = q_ref[hh]
        s1 = lax.dot_general(q, kc_ref[hh], nt, preferred_element_type=F32)
        s2 = lax.dot_general(q, kn_ref[hh], nt, preferred_element_type=F32)
        m = jnp.maximum(jnp.max(s1, axis=-1, keepdims=True), jnp.max(s2, axis=-1, keepdims=True))
        p1 = jnp.exp2(s1 - m)
        p2 = jnp.exp2(s2 - m)
        l = jnp.sum(p1, axis=-1, keepdims=True) + jnp.sum(p2, axis=-1, keepdims=True)
        o = _dot(p1.astype(BF16), vc_ref[hh]) + _dot(p2.astype(BF16), vn_ref[hh])
        o_ref[:, hh * V_DIM:(hh + 1) * V_DIM] = (o / l).astype(BF16)


def _flash_sample(attn, q, k_cache, v_cache, k_new, v_new):
    assert (PAST_LEN + DEC_SEQ - 1) // CHUNK <= PAST_LEN // CHUNK
    blk0 = N_P // DEC_SEQ
    new = lambda b, h: (h, blk0 + b, 0)
    cache = lambda b, h: (h, b, 0)
    return pl.pallas_call(
        _flash_sample_kernel,
        grid=(DEC_BATCH, N_HEADS // _HB),
        in_specs=[
            pl.BlockSpec(memory_space=pl.ANY),
            pl.BlockSpec((_HB, DEC_SEQ, HEAD_PAD), new),
            pl.BlockSpec((_HB, PAST_LEN, HEAD_PAD), cache),
            pl.BlockSpec((_HB, PAST_LEN, V_DIM), cache),
            pl.BlockSpec((_HB, DEC_SEQ, HEAD_PAD), new),
            pl.BlockSpec((_HB, DEC_SEQ, V_DIM), new),
        ],
        out_specs=pl.BlockSpec((DEC_SEQ, _HB * V_DIM), lambda b, h: (blk0 + b, h)),
        out_shape=jax.ShapeDtypeStruct((N_TOK, N_HEADS * V_DIM), BF16),
        input_output_aliases={0: 0},
        compiler_params=_cparams(2),
        name="flash_sample",
    )(attn, q, k_cache, v_cache, k_new, v_new)


def _merge_kernel(h_ref, c_ref, a_ref, wga_ref, wgb_ref, bga_ref, bgb_ref, wc_ref, wo_ref, o_ref):
    h = h_ref[...]
    ga = _sigmoid(_dot(h, wga_ref[...]) + bga_ref[...])
    gb = _sigmoid(_dot(h, wgb_ref[...]) + bgb_ref[...])
    mix = ga * _dot(c_ref[...], wc_ref[...]) + gb * _dot(a_ref[...], wo_ref[...])
    o_ref[...] = mix.astype(BF16)


def _merge(h, c_act, attn, w_gate, b_gate, w_conv_out, w_o):
    n = h.shape[0]
    tn = 512
    nj = D_MODEL // tn
    row = lambda i, j: (i, 0)
    return pl.pallas_call(
        _merge_kernel,
        grid=(n // TM, nj),
        in_specs=[
            pl.BlockSpec((TM, D_MODEL), row),
            pl.BlockSpec((TM, CONV_CHANNELS), row),
            pl.BlockSpec((TM, N_HEADS * V_DIM), row),
            pl.BlockSpec((D_MODEL, tn), lambda i, j: (0, j)),
            pl.BlockSpec((D_MODEL, tn), lambda i, j: (0, j + nj)),
            pl.BlockSpec((1, tn), lambda i, j: (0, j)),
            pl.BlockSpec((1, tn), lambda i, j: (0, j + nj)),
            pl.BlockSpec((CONV_CHANNELS, tn), lambda i, j: (0, j)),
            pl.BlockSpec((N_HEADS * V_DIM, tn), lambda i, j: (0, j)),
        ],
        out_specs=pl.BlockSpec((TM, tn), lambda i, j: (i, j)),
        out_shape=jax.ShapeDtypeStruct((n, D_MODEL), BF16),
        compiler_params=_cparams(2),
        name="merge",
    )(h, c_act, attn, w_gate, w_gate, b_gate, b_gate, w_conv_out, w_o)


def _split_bf16(x):
    hi = x.astype(BF16)
    lo = (x - hi.astype(F32)).astype(BF16)
    return hi, lo


_HALF = D_MODEL // 2


def _pack_bf16_pair(a, b):
    ua = lax.bitcast_convert_type(a.astype(BF16).astype(F32), U32)
    ub = lax.bitcast_convert_type(b.astype(BF16).astype(F32), U32)
    return lax.bitcast_convert_type(ua | (ub >> 16), F32)


def _unpack_bf16_pair(w):
    w = lax.bitcast_convert_type(w, U32)
    a = lax.bitcast_convert_type(w & jnp.uint32(0xFFFF0000), F32).astype(BF16)
    b = lax.bitcast_convert_type(w << 16, F32).astype(BF16)
    return a, b


def _out_router_kernel(n_tiles, mix_ref, x_ref, w_ref, g_ref, wrh_ref, wrl_ref, br_ref,
                       x1_ref, hm_ref, idx_ref, gate_ref):
    i = pl.program_id(0)

    @pl.when(i < n_tiles)
    def _():
        _out_router_tile(mix_ref, x_ref, w_ref, g_ref, wrh_ref, wrl_ref, br_ref,
                         x1_ref, hm_ref, idx_ref, gate_ref)

    @pl.when(i >= n_tiles)
    def _():
        hm_ref[...] = jnp.zeros(hm_ref.shape, F32)


def _out_router_tile(mix_ref, x_ref, w_ref, g_ref, wrh_ref, wrl_ref, br_ref,
                     x1_ref, hm_ref, idx_ref, gate_ref):
    x1 = x_ref[...] + _dot(mix_ref[...], w_ref[...])
    x1_ref[...] = x1
    hn = x1 * lax.rsqrt(jnp.mean(x1 * x1, axis=-1, keepdims=True) + EPS) * g_ref[...]
    hm_ref[...] = _pack_bf16_pair(hn[:, :_HALF], hn[:, _HALF:])
    hh, hl = _split_bf16(hn)
    logits = _dot(hh, wrh_ref[...]) + (_dot(hh, wrl_ref[...]) + _dot(hl, wrh_ref[...])) + br_ref[...]
    lane = lax.broadcasted_iota(I32, logits.shape, 1).astype(F32)
    vals = []
    idx_out = jnp.zeros(logits.shape, F32)
    for k in range(TOP_K):
        m = jnp.max(logits, axis=-1, keepdims=True)
        sel = jnp.min(jnp.where(logits == m, lane, 1e9), axis=-1, keepdims=True)
        vals.append(m)
        idx_out = jnp.where(lane == float(k), sel, idx_out)
        logits = jnp.where(lane == sel, -jnp.inf, logits)
    exps = [jnp.exp(v - vals[0]) for v in vals]
    denom = exps[0] + exps[1] + exps[2] + exps[3]
    gate_out = jnp.zeros(idx_out.shape, F32)
    for k in range(TOP_K):
        gate_out = jnp.where(lane == float(k), exps[k] / denom, gate_out)
    idx_ref[...] = idx_out.astype(I32)
    gate_ref[...] = gate_out


def _out_router(mix, x, w_out, g_ffn, wr_hi, wr_lo, b_r):
    n = x.shape[0]
    tm = 256
    n_tiles = n // tm
    const = lambda i: (0, 0)
    row = lambda i: (jnp.minimum(i, n_tiles - 1), 0)
    return pl.pallas_call(
        functools.partial(_out_router_kernel, n_tiles),
        grid=(2 * n_tiles,),
        in_specs=[
            pl.BlockSpec((tm, D_MODEL), row),
            pl.BlockSpec((tm, D_MODEL), row),
            pl.BlockSpec((D_MODEL, D_MODEL), const),
            pl.BlockSpec((1, D_MODEL), const),
            pl.BlockSpec((D_MODEL, 128), const),
            pl.BlockSpec((D_MODEL, 128), const),
            pl.BlockSpec((1, 128), const),
        ],
        out_specs=[
            pl.BlockSpec((tm, D_MODEL), row),
            pl.BlockSpec((tm, _HALF), lambda i: (i, 0)),
            pl.BlockSpec((tm, 128), row),
            pl.BlockSpec((tm, 128), row),
        ],
        out_shape=[
            jax.ShapeDtypeStruct((n, D_MODEL), F32),
            jax.ShapeDtypeStruct((2 * n, _HALF), F32),
            jax.ShapeDtypeStruct((n, 128), I32),
            jax.ShapeDtypeStruct((n, 128), F32),
        ],
        compiler_params=_cparams(1),
        name="out_router",
    )(mix, x, w_out, g_ffn, wr_hi, wr_lo, b_r)


def _moe_up_kernel(se_ref, sw_ref, sn_ref, sb_ref, sv_ref, sf_ref,
                   prev_ref, x_ref, wg_ref, wu_ref, bg_ref, bu_ref, o_ref, wgb_ref, wub_ref):
    del prev_ref
    t = pl.program_id(0)

    @pl.when(sf_ref[t] == 1)
    def _():
        wgb_ref[...] = wg_ref[0].astype(BF16)
        wub_ref[...] = wu_ref[0].astype(BF16)

    @pl.when(sv_ref[t] == 1)
    def _():
        xa, xb = _unpack_bf16_pair(x_ref[...])
        g = _dot(xa, wgb_ref[:_HALF, :]) + _dot(xb, wgb_ref[_HALF:, :]) + bg_ref[0]
        u = _dot(xa, wub_ref[:_HALF, :]) + _dot(xb, wub_ref[_HALF:, :]) + bu_ref[0]
        g = jnp.minimum(g, SWIGLU_LIMIT)
        u = jnp.clip(u, -SWIGLU_LIMIT, SWIGLU_LIMIT)
        o_ref[...] = ((u + 1.0) * (g * _sigmoid(SWIGLU_ALPHA * g))).astype(BF16)

    @pl.when(sv_ref[t] == 0)
    def _():
        o_ref[...] = jnp.zeros(o_ref.shape, BF16)


_UP_TN = 512
_UP_TILES = D_FF // _UP_TN
_DN_TN = 1024
_DN_TILES = D_MODEL // _DN_TN
MOE_CHUNKS = 4
_CHUNK_BLKS = MOE_MAX_BLKS // MOE_CHUNKS


def _moe_up(plan, act_prev, xs, w_gu, b_gu, chunk):
    steps = plan[0].shape[0]
    blk0 = chunk * _CHUNK_BLKS
    wspec = lambda off: pl.BlockSpec((1, D_MODEL, _UP_TN),
                                     lambda t, se, sw, sn, sb, sv, sf: (se[t], 0, sw[t] + off))
    bspec = lambda off: pl.BlockSpec((1, 1, _UP_TN), lambda t, se, sw, sn, sb, sv, sf: (se[t], 0, sw[t] + off))
    aliases = {} if act_prev is None else {6: 0}
    prev = jnp.zeros((8, 128), BF16) if act_prev is None else act_prev
    return pl.pallas_call(
        _moe_up_kernel,
        grid_spec=pltpu.PrefetchScalarGridSpec(
            num_scalar_prefetch=6,
            grid=(steps,),
            in_specs=[
                pl.BlockSpec(memory_space=pl.ANY),
                pl.BlockSpec((MOE_BLK, _HALF), lambda t, se, sw, sn, sb, sv, sf: (sb[t], 0)),
                wspec(0), wspec(_UP_TILES), bspec(0), bspec(_UP_TILES),
            ],
            out_specs=pl.BlockSpec((MOE_BLK, _UP_TN), lambda t, se, sw, sn, sb, sv, sf: (blk0 + sb[t], sn[t])),
            scratch_shapes=[pltpu.VMEM((D_MODEL, _UP_TN), BF16), pltpu.VMEM((D_MODEL, _UP_TN), BF16)],
        ),
        out_shape=jax.ShapeDtypeStruct((MOE_ROWS, D_FF), BF16),
        input_output_aliases=aliases,
        compiler_params=_cparams(1),
        name=f"moe_up_{chunk}",
    )(*plan, prev, xs, w_gu, w_gu, b_gu, b_gu)


def _moe_down_kernel(se_ref, sw_ref, sn_ref, sb_ref, sv_ref, sf_ref, a_ref, w_ref, b_ref, o_ref, wb_ref):
    t = pl.program_id(0)

    @pl.when(sf_ref[t] == 1)
    def _():
        wb_ref[...] = w_ref[0].astype(BF16)

    @pl.when(sv_ref[t] == 1)
    def _():
        o_ref[...] = _dot(a_ref[...], wb_ref[...]) + b_ref[0]

    @pl.when(sv_ref[t] == 0)
    def _():
        o_ref[...] = jnp.zeros(o_ref.shape, F32)


def _moe_down(plan, act, w_dn, b_dn):
    steps = plan[0].shape[0]
    return pl.pallas_call(
        _moe_down_kernel,
        grid_spec=pltpu.PrefetchScalarGridSpec(
            num_scalar_prefetch=6,
            grid=(steps,),
            in_specs=[
                pl.BlockSpec((MOE_BLK, D_FF), lambda t, se, sw, sn, sb, sv, sf: (sb[t], 0)),
                pl.BlockSpec((1, D_FF, _DN_TN), lambda t, se, sw, sn, sb, sv, sf: (se[t], 0, sw[t])),
                pl.BlockSpec((1, 1, _DN_TN), lambda t, se, sw, sn, sb, sv, sf: (se[t], 0, sw[t])),
            ],
            out_specs=pl.BlockSpec((MOE_BLK, _DN_TN), lambda t, se, sw, sn, sb, sv, sf: (sb[t], sn[t])),
            scratch_shapes=[pltpu.VMEM((D_FF, _DN_TN), BF16)],
        ),
        out_shape=jax.ShapeDtypeStruct((MOE_ROWS, D_MODEL), F32),
        compiler_params=_cparams(1),
        name="moe_down",
    )(*plan, act, w_dn, b_dn)


def _moe_dispatch(top_idx):
    n_asg = N_TOK * TOP_K
    flat_e = top_idx.reshape(-1)
    onehot = (flat_e[:, None] == jnp.arange(N_EXPERTS, dtype=I32)[None, :]).astype(I32)
    csum = jnp.cumsum(onehot, axis=0)
    counts = csum[-1]
    rank = jnp.sum(csum * onehot, axis=1) - 1
    nblk = (counts + MOE_BLK - 1) // MOE_BLK
    blk_start = jnp.cumsum(nblk) - nblk
    dest = jnp.sum(onehot * blk_start[None, :], axis=1) * MOE_BLK + rank
    pad_src = jnp.arange(MOE_ROWS, dtype=I32) % N_TOK
    row_tok = pad_src.at[dest].set(jnp.arange(n_asg, dtype=I32) // TOP_K,
                                   mode="promise_in_bounds", unique_indices=True)
    return dest, row_tok, nblk, blk_start


def _moe_steps(nblk, blk_start, n_tiles, blk_lo, n_blks):
    t_max = n_tiles * n_blks
    lo = jnp.clip(blk_start, blk_lo, blk_lo + n_blks)
    hi = jnp.clip(blk_start + nblk, blk_lo, blk_lo + n_blks)
    nb_e = hi - lo
    per_e = nb_e * n_tiles
    s_end = jnp.cumsum(per_e)
    total = s_end[-1]
    t = jnp.arange(t_max, dtype=I32)
    tc = jnp.clip(t, 0, jnp.maximum(total - 1, 0))
    e = jnp.minimum(jnp.sum((s_end[None, :] <= tc[:, None]).astype(I32), axis=1), N_EXPERTS - 1)
    sel = (e[:, None] == jnp.arange(N_EXPERTS, dtype=I32)[None, :]).astype(I32)
    pick = lambda v: jnp.sum(sel * v[None, :], axis=1)
    local = tc - pick(s_end - per_e)
    nb = jnp.maximum(pick(nb_e), 1)
    w_tile = jnp.clip(local // nb, 0, n_tiles - 1)
    r = local % nb
    valid = t < total
    first = jnp.logical_and(valid, r == 0)
    fill = t - total
    blk = jnp.where(valid, pick(lo) - blk_lo + r, total // n_tiles + fill // n_tiles)
    o_tile = jnp.where(valid, w_tile, fill % n_tiles)
    blk = jnp.clip(blk, 0, n_blks - 1)
    return e, w_tile, o_tile, blk, valid.astype(I32), first.astype(I32)


_FIN_TM = 256
_FIN_TN = 512
FIN_CHUNKS = 4


def _final_kernel(prev_ref, x1_ref, y0_ref, y1_ref, y2_ref, y3_ref, gate_ref, g_ref, wg_ref, p_ref, wp_ref,
                  o_ref, x2_ref):
    del prev_ref
    gate = gate_ref[...]
    moe = (y0_ref[0] * gate[:, 0:1] + y1_ref[0] * gate[:, 1:2]
           + y2_ref[0] * gate[:, 2:3] + y3_ref[0] * gate[:, 3:4])
    x2 = x1_ref[...] + moe
    x2_ref[...] = x2
    hp = (x2 * lax.rsqrt(jnp.mean(x2 * x2, axis=-1, keepdims=True) + EPS) * g_ref[...]).astype(BF16)
    pb = p_ref[...].astype(BF16)
    for c in range(0, D_MODEL, _FIN_TN):
        cols = slice(c, c + _FIN_TN)
        emb = _dot(pb, wp_ref[:, cols])
        o_ref[:, cols] = x2_ref[:, cols] + _sigmoid(_dot(hp, wg_ref[:, cols])) * emb


def _final(out_prev, x1, y4, gate, g_ple, w_ple_gate, p, w_ple, tok0, out0, n, n_out, name):
    t0 = tok0 // _FIN_TM
    o0 = out0 // _FIN_TM
    pt0 = out0 // _FIN_TM
    const = lambda i: (0, 0)
    yspec = lambda k: pl.BlockSpec((1, _FIN_TM, D_MODEL), lambda i: (k, i, 0))
    once = pl.Buffered(1)
    aliases = {} if out_prev is None else {0: 0}
    prev = jnp.zeros((8, 128), F32) if out_prev is None else out_prev
    return pl.pallas_call(
        _final_kernel,
        grid=(n // _FIN_TM,),
        in_specs=[
            pl.BlockSpec(memory_space=pl.ANY),
            pl.BlockSpec((_FIN_TM, D_MODEL), lambda i: (t0 + i, 0)),
            yspec(0), yspec(1), yspec(2), yspec(3),
            pl.BlockSpec((_FIN_TM, 128), lambda i: (t0 + i, 0)),
            pl.BlockSpec((1, D_MODEL), const),
            pl.BlockSpec((D_MODEL, D_MODEL), const, pipeline_mode=once),
            pl.BlockSpec((_FIN_TM, PLE_DIM), lambda i: (pt0 + i, 0)),
            pl.BlockSpec((PLE_DIM, D_MODEL), const, pipeline_mode=once),
        ],
        out_specs=pl.BlockSpec((_FIN_TM, D_MODEL), lambda i: (o0 + i, 0)),
        out_shape=jax.ShapeDtypeStruct((n_out, D_MODEL), F32),
        scratch_shapes=[pltpu.VMEM((_FIN_TM, D_MODEL), F32)],
        input_output_aliases=aliases,
        compiler_params=_cparams(1),
        name=name,
    )(prev, x1, y4, y4, y4, y4, gate, g_ple, w_ple_gate, p, w_ple)


def _rope_layout(x):
    half = ROPE_DIM // 2
    z = jnp.zeros(x.shape[:-1] + (half,), x.dtype)
    return jnp.concatenate([x[..., :half], z, x[..., half:], z], axis=-1)


def _rope_tables():
    half = ROPE_DIM // 2
    inv_freq = ROPE_THETA ** (-jnp.arange(half, dtype=F32) / half)
    pos = jnp.arange(PAST_LEN + DEC_SEQ, dtype=I32)
    ang = pos.astype(F32)[:, None] * inv_freq[None, :]
    cos, sin = jnp.cos(ang), jnp.sin(ang)
    z = jnp.zeros_like(cos)
    c = jnp.concatenate([cos, z, cos, z], axis=-1)
    s = jnp.concatenate([-sin, z, sin, z], axis=-1)
    rep = ATT_TM // DEC_SEQ
    return (jnp.concatenate([c[:SEQ], jnp.tile(c[PAST_LEN:], (rep, 1))], axis=0),
            jnp.concatenate([s[:SEQ], jnp.tile(s[PAST_LEN:], (rep, 1))], axis=0))


def _layer(x, p_prompt, p_sample, cache_kv, cache_kr, state_conv,
           g_mix, w_in, b_gate, w_dw, b_dw, g_cn, b_cn, w_conv_out,
           g_qa, g_kva, w_qb, w_kb, w_vb, g_qn, g_kn, w_o, w_out,
           g_ffn, w_router, b_router, w_gu, b_gu, w_dn, b_dn,
           g_ple, w_ple_gate, w_ple):
    assert SEQ == PAST_LEN
    row = lambda v: v.reshape(1, -1)
    w_in_b = w_in.astype(BF16)
    w_mid = jnp.concatenate([w_in_b[:, O_U:O_KV], _rope_layout(w_in_b[:, O_KV:O_KR])], axis=1)
    w_gate = w_in_b[:, O_KR:]

    h, q_lat, kv_new, kr_pad = _in_mid(x, row(g_mix), w_mid, row(g_qa), row(g_kva))
    half = ROPE_DIM // 2
    kr_new = jnp.concatenate([kr_pad[:, :half], kr_pad[:, 2 * half:3 * half]], axis=1)
    glu = _in_glu(h, w_in_b)

    hist = jnp.concatenate([jnp.zeros((BATCH, HALO, CONV_CHANNELS), F32),
                            jnp.pad(state_conv, ((0, 0), (HALO - (CONV_WIDTH - 1), 0), (0, 0)))], axis=0)
    c_act = _conv_module(glu, hist, w_dw, row(b_dw), row(g_cn), row(b_cn))

    cos_t, sin_t = _rope_tables()
    w_q = jnp.concatenate([w_qb[..., :NOPE_DIM], _rope_layout(w_qb[..., NOPE_DIM:])], axis=-1)
    w_q = w_q.reshape(Q_LORA_RANK, N_HEADS * HEAD_PAD).astype(BF16)
    g_q = jnp.concatenate([g_qn[:NOPE_DIM], _rope_layout(g_qn[NOPE_DIM:])]).reshape(1, HEAD_PAD)
    q = _q_heads(q_lat, w_q, g_q, cos_t, sin_t)

    w_kv = jnp.concatenate([w_kb, w_vb], axis=-1).reshape(KV_LORA_RANK, N_HEADS * HEAD_PAD).astype(BF16)
    g_kn_nope = g_kn[:NOPE_DIM].reshape(1, NOPE_DIM)
    g_kn_rope = _rope_layout(g_kn[NOPE_DIM:]).reshape(1, 128)
    k_new, v_new = _kv_heads(kv_new, kr_pad, w_kv, g_kn_nope, g_kn_rope, cos_t, sin_t,
                             _tab_idx_new, "kv_heads_new")
    k_cache, v_cache = _kv_heads(cache_kv.reshape(DEC_BATCH * PAST_LEN, KV_LORA_RANK),
                                 _rope_layout(cache_kr).reshape(DEC_BATCH * PAST_LEN, 128),
                                 w_kv, g_kn_nope, g_kn_rope, cos_t, sin_t, _tab_idx_cache, "kv_heads_cache")

    attn = _flash_prompt(q, k_new, v_new)
    attn = _flash_sample(attn, q, k_cache, v_cache, k_new, v_new)

    mix = _merge(h, c_act, attn, w_gate, row(b_gate), w_conv_out.astype(BF16), w_o.astype(BF16))

    wr = jnp.pad(w_router, ((0, 0), (0, 128 - N_EXPERTS)))
    wr_hi, wr_lo = _split_bf16(wr)
    b_r = jnp.concatenate([b_router, jnp.full((128 - N_EXPERTS,), -jnp.inf, F32)]).reshape(1, 128)
    x1, hm, idx_pad, gate_pad = _out_router(mix, x, w_out.astype(BF16), row(g_ffn), wr_hi, wr_lo, b_r)

    top_idx = idx_pad[:, :TOP_K]
    dest, row_tok, nblk, blk_start = _moe_dispatch(top_idx)
    b_gu3 = b_gu.reshape(N_EXPERTS, 1, 2 * D_FF)
    chunk_rows = _CHUNK_BLKS * MOE_BLK
    act = None
    for c in range(MOE_CHUNKS):
        xs = hm.at[row_tok[c * chunk_rows:(c + 1) * chunk_rows]].get(mode="promise_in_bounds")
        plan = _moe_steps(nblk, blk_start, _UP_TILES, c * _CHUNK_BLKS, _CHUNK_BLKS)
        act = _moe_up(plan, act, xs, w_gu, b_gu3, c)
    ys = _moe_down(_moe_steps(nblk, blk_start, _DN_TILES, 0, MOE_MAX_BLKS), act, w_dn,
                   b_dn.reshape(N_EXPERTS, 1, D_MODEL))

    dest_t = dest.reshape(N_TOK, TOP_K).T
    fin = (row(g_ple), w_ple_gate.astype(BF16))
    w_ple_b = w_ple.astype(BF16)
    n_c = N_P // FIN_CHUNKS
    out_p = None
    for c in range(FIN_CHUNKS):
        y4 = ys.at[dest_t[:, c * n_c:(c + 1) * n_c]].get(mode="promise_in_bounds")
        out_p = _final(out_p, x1, y4, gate_pad, *fin, p_prompt, w_ple_b, c * n_c, c * n_c, n_c, N_P,
                       f"final_prompt_{c}")
    y4 = ys.at[dest_t[:, N_P:]].get(mode="promise_in_bounds")
    out_s = _final(None, x1, y4, gate_pad, *fin, p_sample, w_ple_b, N_P, 0, N_S, N_S, "final_sample")
    return out_p, out_s, kv_new, kr_new, glu


def kernel(x_prompt, x_sample, cache_kv_latent, cache_k_rope, state_conv, p_prompt, p_sample, g_mix, w_in, b_gate, w_dw, b_dw, g_cn, b_cn, w_conv_out, g_qa, g_kva, w_qb, w_kb, w_vb, g_qn, g_kn, w_o, w_out, g_ffn, w_router, b_router, w_gu, b_gu, w_dn, b_dn, g_ple, w_ple_gate, w_ple):
    assert g_mix.shape[0] == 1
    x = jnp.concatenate([x_prompt.reshape(N_P, D_MODEL), x_sample.reshape(N_S, D_MODEL)], axis=0)
    out_p, out_s, kv_new, kr_new, glu = _layer(
        x, p_prompt[0].reshape(N_P, PLE_DIM), p_sample[0].reshape(N_S, PLE_DIM),
        cache_kv_latent[0], cache_k_rope[0], state_conv[0],
        g_mix[0], w_in[0], b_gate[0], w_dw[0], b_dw[0], g_cn[0], b_cn[0], w_conv_out[0],
        g_qa[0], g_kva[0], w_qb[0], w_kb[0], w_vb[0], g_qn[0], g_kn[0], w_o[0], w_out[0],
        g_ffn[0], w_router[0], b_router[0], w_gu[0], b_gu[0], w_dn[0], b_dn[0],
        g_ple[0], w_ple_gate[0], w_ple[0])
    tail = CONV_WIDTH - 1
    conv_p = jnp.stack([glu[(b + 1) * SEQ - tail:(b + 1) * SEQ] for b in range(BATCH)])
    conv_s = glu[N_P:].reshape(DEC_BATCH, DEC_SEQ, CONV_CHANNELS)[:, DEC_SEQ - tail:]
    return (out_p.reshape(BATCH, SEQ, D_MODEL),
            out_s.reshape(DEC_BATCH, DEC_SEQ, D_MODEL),
            kv_new[:N_P].reshape(1, BATCH, SEQ, KV_LORA_RANK),
            kr_new[:N_P].reshape(1, BATCH, SEQ, ROPE_DIM),
            conv_p[None],
            kv_new[N_P:].reshape(1, DEC_BATCH, DEC_SEQ, KV_LORA_RANK),
            kr_new[N_P:].reshape(1, DEC_BATCH, DEC_SEQ, ROPE_DIM),
            conv_s[None])
```

```python
import functools
import math

import jax
import jax.numpy as jnp
from jax import lax
from jax.experimental import pallas as pl
from jax.experimental.pallas import tpu as pltpu

F32 = jnp.float32
BF16 = jnp.bfloat16
I32 = jnp.int32
U32 = jnp.uint32

D_MODEL = 2048
BATCH = 2
SEQ = 4096
DEC_BATCH = 8
DEC_SEQ = 64
PAST_LEN = 4096
CHUNK = 64
CONV_CHANNELS = D_MODEL
CONV_WIDTH = 31
N_HEADS = 16
Q_LORA_RANK = 512
KV_LORA_RANK = 512
NOPE_DIM = 128
ROPE_DIM = 64
QK_DIM = NOPE_DIM + ROPE_DIM
V_DIM = 128
ROPE_THETA = 10000.0
N_EXPERTS = 32
TOP_K = 4
D_FF = D_MODEL
SWIGLU_ALPHA = 1.702
SWIGLU_LIMIT = 7.0
PLE_DIM = 256
EPS = 1e-6
NEG_INF = -1e30

N_P = BATCH * SEQ
N_S = DEC_BATCH * DEC_SEQ
N_TOK = N_P + N_S
O_U = 2 * CONV_CHANNELS
O_Q = O_U + Q_LORA_RANK
O_KV = O_Q + KV_LORA_RANK
O_KR = O_KV + ROPE_DIM
MID_W = 1152
HEAD_PAD = 256

TM = 512
CONV_T = 64
HALO = 32
MOE_BLK = 512
MOE_MAX_BLKS = (N_TOK * TOP_K) // MOE_BLK + N_EXPERTS
MOE_ROWS = MOE_MAX_BLKS * MOE_BLK
VMEM_LIMIT = 48 * 1024 * 1024


def _cparams(n_axes):
    return pltpu.CompilerParams(dimension_semantics=("arbitrary",) * n_axes,
                                vmem_limit_bytes=VMEM_LIMIT)


def _sigmoid(x):
    return 1.0 / (1.0 + jnp.exp(-x))


def _dot(a, b):
    return jnp.dot(a, b, preferred_element_type=F32)


def _in_mid_kernel(x_ref, g_ref, w_ref, gqa_ref, gkva_ref, h_ref, q_ref, kv_ref, kr_ref):
    x = x_ref[...]
    h = x * lax.rsqrt(jnp.mean(x * x, axis=-1, keepdims=True) + EPS) * g_ref[...]
    hb = h.astype(BF16)
    h_ref[...] = hb
    z = _dot(hb, w_ref[...])
    ql = z[:, :Q_LORA_RANK]
    kvl = z[:, Q_LORA_RANK:Q_LORA_RANK + KV_LORA_RANK]
    qn = ql * lax.rsqrt(jnp.mean(ql * ql, axis=-1, keepdims=True) + EPS) * gqa_ref[...]
    q_ref[...] = qn.astype(BF16)
    kv_ref[...] = kvl * lax.rsqrt(jnp.mean(kvl * kvl, axis=-1, keepdims=True) + EPS) * gkva_ref[...]
    kr_ref[...] = z[:, Q_LORA_RANK + KV_LORA_RANK:]


def _in_mid(x, g_mix, w_mid, g_qa, g_kva):
    n = x.shape[0]
    return pl.pallas_call(
        _in_mid_kernel,
        grid=(n // TM,),
        in_specs=[
            pl.BlockSpec((TM, D_MODEL), lambda i: (i, 0)),
            pl.BlockSpec((1, D_MODEL), lambda i: (0, 0)),
            pl.BlockSpec((D_MODEL, MID_W), lambda i: (0, 0)),
            pl.BlockSpec((1, Q_LORA_RANK), lambda i: (0, 0)),
            pl.BlockSpec((1, KV_LORA_RANK), lambda i: (0, 0)),
        ],
        out_specs=[
            pl.BlockSpec((TM, D_MODEL), lambda i: (i, 0)),
            pl.BlockSpec((TM, Q_LORA_RANK), lambda i: (i, 0)),
            pl.BlockSpec((TM, KV_LORA_RANK), lambda i: (i, 0)),
            pl.BlockSpec((TM, 128), lambda i: (i, 0)),
        ],
        out_shape=[
            jax.ShapeDtypeStruct((n, D_MODEL), BF16),
            jax.ShapeDtypeStruct((n, Q_LORA_RANK), BF16),
            jax.ShapeDtypeStruct((n, KV_LORA_RANK), F32),
            jax.ShapeDtypeStruct((n, 128), F32),
        ],
        compiler_params=_cparams(1),
        name="in_mid",
    )(x, g_mix, w_mid, g_qa, g_kva)


def _glu_kernel(h_ref, w1_ref, w2_ref, o_ref):
    h = h_ref[...]
    o_ref[...] = _dot(h, w1_ref[...]) * _sigmoid(_dot(h, w2_ref[...]))


def _in_glu(h, w_in_b):
    n = h.shape[0]
    tn = 512
    nj = CONV_CHANNELS // tn
    return pl.pallas_call(
        _glu_kernel,
        grid=(n // TM, nj),
        in_specs=[
            pl.BlockSpec((TM, D_MODEL), lambda i, j: (i, 0)),
            pl.BlockSpec((D_MODEL, tn), lambda i, j: (0, j)),
            pl.BlockSpec((D_MODEL, tn), lambda i, j: (0, j + nj)),
        ],
        out_specs=pl.BlockSpec((TM, tn), lambda i, j: (i, j)),
        out_shape=jax.ShapeDtypeStruct((n, CONV_CHANNELS), F32),
        compiler_params=_cparams(2),
        name="in_glu",
    )(h, w_in_b, w_in_b)


_CONV_TILES_PER_SEQ = SEQ // CONV_T
_CONV_PROMPT_TILES = N_P // CONV_T
_CONV_LANES = 512
SUBLANES = 8
_SHIFT_ROWS = (HALO // SUBLANES - 1) * SUBLANES + CONV_T


def _conv_kernel(cur_ref, prev_ref, hist_ref, w_ref, bdw_ref, g_ref, b_ref, o_ref, win_ref, conv_ref, shift_ref):
    i = pl.program_id(0)
    first = jnp.logical_or(i >= _CONV_PROMPT_TILES, i % _CONV_TILES_PER_SEQ == 0)

    @pl.when(first)
    def _():
        win_ref[0:HALO, :] = hist_ref[0]

    @pl.when(jnp.logical_not(first))
    def _():
        win_ref[0:HALO, :] = prev_ref[...]

    win_ref[HALO:HALO + CONV_T, :] = cur_ref[...]
    for r in range(1, SUBLANES):
        shift_ref[r - 1] = win_ref[r:r + _SHIFT_ROWS, :]
    base = HALO - (CONV_WIDTH - 1)
    for c in range(0, CONV_CHANNELS, _CONV_LANES):
        acc = jnp.zeros((CONV_T, _CONV_LANES), F32)
        for k in range(CONV_WIDTH):
            q, r = divmod(base + k, SUBLANES)
            lanes = slice(c, c + _CONV_LANES)
            rows = slice(q * SUBLANES, q * SUBLANES + CONV_T)
            src = win_ref[rows, lanes] if r == 0 else shift_ref[r - 1, rows, lanes]
            acc = acc + w_ref[k:k + 1, lanes] * src
        conv_ref[:, c:c + _CONV_LANES] = acc + bdw_ref[:, c:c + _CONV_LANES]
    y = conv_ref[...]
    yc = y - jnp.mean(y, axis=-1, keepdims=True)
    var = jnp.mean(yc * yc, axis=-1, keepdims=True)
    z = yc * lax.rsqrt(var + EPS) * g_ref[...] + b_ref[...]
    o_ref[...] = (z * _sigmoid(z)).astype(BF16)


def _conv_module(glu, hist, w_dw, b_dw, g_cn, b_cn):
    n = glu.shape[0]
    n_tiles = n // CONV_T
    halo_per_tile = CONV_T // HALO

    def seq_of(i):
        return jnp.where(i < _CONV_PROMPT_TILES, i // _CONV_TILES_PER_SEQ, i - _CONV_PROMPT_TILES + BATCH)

    return pl.pallas_call(
        _conv_kernel,
        grid=(n_tiles,),
        in_specs=[
            pl.BlockSpec((CONV_T, CONV_CHANNELS), lambda i: (i, 0)),
            pl.BlockSpec((HALO, CONV_CHANNELS), lambda i: (jnp.maximum(i * halo_per_tile - 1, 0), 0)),
            pl.BlockSpec((1, HALO, CONV_CHANNELS), lambda i: (seq_of(i), 0, 0)),
            pl.BlockSpec((CONV_WIDTH, CONV_CHANNELS), lambda i: (0, 0)),
            pl.BlockSpec((1, CONV_CHANNELS), lambda i: (0, 0)),
            pl.BlockSpec((1, CONV_CHANNELS), lambda i: (0, 0)),
            pl.BlockSpec((1, CONV_CHANNELS), lambda i: (0, 0)),
        ],
        out_specs=pl.BlockSpec((CONV_T, CONV_CHANNELS), lambda i: (i, 0)),
        out_shape=jax.ShapeDtypeStruct((n, CONV_CHANNELS), BF16),
        scratch_shapes=[pltpu.VMEM((HALO + CONV_T, CONV_CHANNELS), F32),
                        pltpu.VMEM((CONV_T, CONV_CHANNELS), F32),
                        pltpu.VMEM((SUBLANES - 1, _SHIFT_ROWS, CONV_CHANNELS), F32)],
        compiler_params=_cparams(1),
        name="conv_module",
    )(glu, glu, hist, w_dw, b_dw, g_cn, b_cn)


ATT_TM = 256
_TAB_PROMPT_TILES = N_P // ATT_TM
_TAB_SEQ_TILES = SEQ // ATT_TM
_TAB_ROWS = SEQ + ATT_TM


def _tab_idx_new(i):
    return jnp.where(i < _TAB_PROMPT_TILES, i % _TAB_SEQ_TILES, _TAB_SEQ_TILES)


def _tab_idx_cache(i):
    return i % _TAB_SEQ_TILES


def _rope_pair(u, c, s):
    return u * c + pltpu.roll(u, 64, 1) * s


_Q_SCALE = math.log2(math.e) / math.sqrt(QK_DIM)


def _q_heads_kernel(ql_ref, w_ref, g_ref, c_ref, s_ref, o_ref):
    ql = ql_ref[...]
    g = g_ref[...]
    c = c_ref[...]
    s = s_ref[...]
    for h in range(N_HEADS):
        qf = _dot(ql, w_ref[:, h * HEAD_PAD:(h + 1) * HEAD_PAD])
        ssq = jnp.sum(qf * qf, axis=-1, keepdims=True)
        qn = qf * (lax.rsqrt(ssq * (1.0 / QK_DIM) + EPS) * _Q_SCALE) * g
        o_ref[h, :, :NOPE_DIM] = qn[:, :NOPE_DIM].astype(BF16)
        o_ref[h, :, NOPE_DIM:] = _rope_pair(qn[:, NOPE_DIM:], c, s).astype(BF16)


def _q_heads(q_lat, w_q, g_q, cos_t, sin_t):
    n = q_lat.shape[0]
    return pl.pallas_call(
        _q_heads_kernel,
        grid=(n // ATT_TM,),
        in_specs=[
            pl.BlockSpec((ATT_TM, Q_LORA_RANK), lambda i: (i, 0)),
            pl.BlockSpec((Q_LORA_RANK, N_HEADS * HEAD_PAD), lambda i: (0, 0)),
            pl.BlockSpec((1, HEAD_PAD), lambda i: (0, 0)),
            pl.BlockSpec((ATT_TM, 128), lambda i: (_tab_idx_new(i), 0)),
            pl.BlockSpec((ATT_TM, 128), lambda i: (_tab_idx_new(i), 0)),
        ],
        out_specs=pl.BlockSpec((N_HEADS, ATT_TM, HEAD_PAD), lambda i: (0, i, 0)),
        out_shape=jax.ShapeDtypeStruct((N_HEADS, n, HEAD_PAD), BF16),
        compiler_params=_cparams(1),
        name="q_heads",
    )(q_lat, w_q, g_q, cos_t, sin_t)


def _kv_heads_kernel(kv_ref, kr_ref, w_ref, gn_ref, gr_ref, c_ref, s_ref, k_ref, v_ref):
    kv = kv_ref[...].astype(BF16)
    u = kr_ref[...]
    ssq_r = jnp.sum(u * u, axis=-1, keepdims=True)
    krot = _rope_pair(u * gr_ref[...], c_ref[...], s_ref[...])
    gn = gn_ref[...]
    for h in range(N_HEADS):
        z = _dot(kv, w_ref[:, h * HEAD_PAD:(h + 1) * HEAD_PAD])
        kn = z[:, :NOPE_DIM]
        ssq = jnp.sum(kn * kn, axis=-1, keepdims=True) + ssq_r
        scale = lax.rsqrt(ssq * (1.0 / QK_DIM) + EPS)
        k_ref[h, :, :NOPE_DIM] = (kn * scale * gn).astype(BF16)
        k_ref[h, :, NOPE_DIM:] = (krot * scale).astype(BF16)
        v_ref[h] = z[:, NOPE_DIM:].astype(BF16)


def _kv_heads(kv_lat, kr_pad, w_kv, g_kn_nope, g_kn_rope, cos_t, sin_t, tab_idx, name):
    n = kv_lat.shape[0]
    return pl.pallas_call(
        _kv_heads_kernel,
        grid=(n // ATT_TM,),
        in_specs=[
            pl.BlockSpec((ATT_TM, KV_LORA_RANK), lambda i: (i, 0)),
            pl.BlockSpec((ATT_TM, 128), lambda i: (i, 0)),
            pl.BlockSpec((KV_LORA_RANK, N_HEADS * HEAD_PAD), lambda i: (0, 0)),
            pl.BlockSpec((1, NOPE_DIM), lambda i: (0, 0)),
            pl.BlockSpec((1, 128), lambda i: (0, 0)),
            pl.BlockSpec((ATT_TM, 128), lambda i: (tab_idx(i), 0)),
            pl.BlockSpec((ATT_TM, 128), lambda i: (tab_idx(i), 0)),
        ],
        out_specs=[
            pl.BlockSpec((N_HEADS, ATT_TM, HEAD_PAD), lambda i: (0, i, 0)),
            pl.BlockSpec((N_HEADS, ATT_TM, V_DIM), lambda i: (0, i, 0)),
        ],
        out_shape=[
            jax.ShapeDtypeStruct((N_HEADS, n, HEAD_PAD), BF16),
            jax.ShapeDtypeStruct((N_HEADS, n, V_DIM), BF16),
        ],
        compiler_params=_cparams(1),
        name=name,
    )(kv_lat, kr_pad, w_kv, g_kn_nope, g_kn_rope, cos_t, sin_t)


_TQ = 512
_TKB = 512
_HB = 2


def _flash_prompt_kernel(q_ref, k_ref, v_ref, o_ref, m_ref, l_ref, acc_ref):
    qi = pl.program_id(2)
    m_ref[...] = jnp.full(m_ref.shape, NEG_INF, F32)
    l_ref[...] = jnp.zeros(l_ref.shape, F32)
    acc_ref[...] = jnp.zeros(acc_ref.shape, F32)
    nlb = _TKB // 128

    def step(ki, masked):
        start = pl.multiple_of(ki * _TKB, _TKB)
        for hh in range(_HB):
            k = k_ref[hh, pl.ds(start, _TKB), :]
            v = v_ref[hh, pl.ds(start, _TKB), :]
            s = lax.dot_general(q_ref[hh], k, (((1,), (1,)), ((), ())), preferred_element_type=F32)
            if masked:
                rc = lax.broadcasted_iota(I32, (_TQ, _TKB), 0) // CHUNK
                cc = lax.broadcasted_iota(I32, (_TQ, _TKB), 1) // CHUNK
                s = jnp.where(cc <= rc, s, NEG_INF)
            sb = [s[:, c * 128:(c + 1) * 128] for c in range(nlb)]
            bm = sb[0]
            for c in range(1, nlb):
                bm = jnp.maximum(bm, sb[c])
            m_prev = m_ref[hh]
            m_new = jnp.maximum(m_prev, jnp.max(bm, axis=-1, keepdims=True))
            alpha = jnp.exp2(m_prev - m_new)
            ps = [jnp.exp2(x - m_new) for x in sb]
            psum = ps[0]
            for c in range(1, nlb):
                psum = psum + ps[c]
            l_ref[hh] = alpha * l_ref[hh] + psum
            p = jnp.concatenate(ps, axis=1).astype(BF16)
            acc_ref[hh] = alpha * acc_ref[hh] + _dot(p, v)
            m_ref[hh] = m_new

    def body(ki, carry):
        step(ki, False)
        return carry

    lax.fori_loop(0, qi, body, 0)
    step(qi, True)
    for hh in range(_HB):
        l = jnp.sum(l_ref[hh], axis=-1, keepdims=True)
        o_ref[:, hh * V_DIM:(hh + 1) * V_DIM] = (acc_ref[hh] / l).astype(BF16)


def _flash_prompt(q, k, v):
    nq = SEQ // _TQ
    return pl.pallas_call(
        _flash_prompt_kernel,
        grid=(BATCH, N_HEADS // _HB, nq),
        in_specs=[
            pl.BlockSpec((_HB, _TQ, HEAD_PAD), lambda b, h, i: (h, b * nq + i, 0)),
            pl.BlockSpec((_HB, SEQ, HEAD_PAD), lambda b, h, i: (h, b, 0)),
            pl.BlockSpec((_HB, SEQ, V_DIM), lambda b, h, i: (h, b, 0)),
        ],
        out_specs=pl.BlockSpec((_TQ, _HB * V_DIM), lambda b, h, i: (b * nq + i, h)),
        out_shape=jax.ShapeDtypeStruct((N_TOK, N_HEADS * V_DIM), BF16),
        scratch_shapes=[pltpu.VMEM((_HB, _TQ, 128), F32), pltpu.VMEM((_HB, _TQ, 128), F32),
                        pltpu.VMEM((_HB, _TQ, V_DIM), F32)],
        compiler_params=_cparams(3),
        name="flash_prompt",
    )(q, k, v)


def _flash_sample_kernel(prev_ref, q_ref, kc_ref, vc_ref, kn_ref, vn_ref, o_ref):
    del prev_ref
    nt = (((1,), (1,)), ((), ()))
    for hh in range(_HB):
        q = q_ref[hh]
        s1 = lax.dot_general(q, kc_ref[hh], nt, preferred_element_type=F32)
        s2 = lax.dot_general(q, kn_ref[hh], nt, preferred_element_type=F32)
        m = jnp.maximum(jnp.max(s1, axis=-1, keepdims=True), jnp.max(s2, axis=-1, keepdims=True))
        p1 = jnp.exp2(s1 - m)
        p2 = jnp.exp2(s2 - m)
        l = jnp.sum(p1, axis=-1, keepdims=True) + jnp.sum(p2, axis=-1, keepdims=True)
        o = _dot(p1.astype(BF16), vc_ref[hh]) + _dot(p2.astype(BF16), vn_ref[hh])
        o_ref[:, hh * V_DIM:(hh + 1) * V_DIM] = (o / l).astype(BF16)


def _flash_sample(attn, q, k_cache, v_cache, k_new, v_new):
    assert (PAST_LEN + DEC_SEQ - 1) // CHUNK <= PAST_LEN // CHUNK
    blk0 = N_P // DEC_SEQ
    new = lambda b, h: (h, blk0 + b, 0)
    cache = lambda b, h: (h, b, 0)
    return pl.pallas_call(
        _flash_sample_kernel,
        grid=(DEC_BATCH, N_HEADS // _HB),
        in_specs=[
            pl.BlockSpec(memory_space=pl.ANY),
            pl.BlockSpec((_HB, DEC_SEQ, HEAD_PAD), new),
            pl.BlockSpec((_HB, PAST_LEN, HEAD_PAD), cache),
            pl.BlockSpec((_HB, PAST_LEN, V_DIM), cache),
            pl.BlockSpec((_HB, DEC_SEQ, HEAD_PAD), new),
            pl.BlockSpec((_HB, DEC_SEQ, V_DIM), new),
        ],
        out_specs=pl.BlockSpec((DEC_SEQ, _HB * V_DIM), lambda b, h: (blk0 + b, h)),
        out_shape=jax.ShapeDtypeStruct((N_TOK, N_HEADS * V_DIM), BF16),
        input_output_aliases={0: 0},
        compiler_params=_cparams(2),
        name="flash_sample",
    )(attn, q, k_cache, v_cache, k_new, v_new)


def _merge_kernel(h_ref, c_ref, a_ref, wga_ref, wgb_ref, bga_ref, bgb_ref, wc_ref, wo_ref, o_ref):
    h = h_ref[...]
    ga = _sigmoid(_dot(h, wga_ref[...]) + bga_ref[...])
    gb = _sigmoid(_dot(h, wgb_ref[...]) + bgb_ref[...])
    mix = ga * _dot(c_ref[...], wc_ref[...]) + gb * _dot(a_ref[...], wo_ref[...])
    o_ref[...] = mix.astype(BF16)


def _merge(h, c_act, attn, w_gate, b_gate, w_conv_out, w_o):
    n = h.shape[0]
    tn = 512
    nj = D_MODEL // tn
    row = lambda i, j: (i, 0)
    return pl.pallas_call(
        _merge_kernel,
        grid=(n // TM, nj),
        in_specs=[
            pl.BlockSpec((TM, D_MODEL), row),
            pl.BlockSpec((TM, CONV_CHANNELS), row),
            pl.BlockSpec((TM, N_HEADS * V_DIM), row),
            pl.BlockSpec((D_MODEL, tn), lambda i, j: (0, j)),
            pl.BlockSpec((D_MODEL, tn), lambda i, j: (0, j + nj)),
            pl.BlockSpec((1, tn), lambda i, j: (0, j)),
            pl.BlockSpec((1, tn), lambda i, j: (0, j + nj)),
            pl.BlockSpec((CONV_CHANNELS, tn), lambda i, j: (0, j)),
            pl.BlockSpec((N_HEADS * V_DIM, tn), lambda i, j: (0, j)),
        ],
        out_specs=pl.BlockSpec((TM, tn), lambda i, j: (i, j)),
        out_shape=jax.ShapeDtypeStruct((n, D_MODEL), BF16),
        compiler_params=_cparams(2),
        name="merge",
    )(h, c_act, attn, w_gate, w_gate, b_gate, b_gate, w_conv_out, w_o)


def _split_bf16(x):
    hi = x.astype(BF16)
    lo = (x - hi.astype(F32)).astype(BF16)
    return hi, lo


_HALF = D_MODEL // 2


def _pack_bf16_pair(a, b):
    ua = lax.bitcast_convert_type(a.astype(BF16).astype(F32), U32)
    ub = lax.bitcast_convert_type(b.astype(BF16).astype(F32), U32)
    return lax.bitcast_convert_type(ua | (ub >> 16), F32)


def _unpack_bf16_pair(w):
    w = lax.bitcast_convert_type(w, U32)
    a = lax.bitcast_convert_type(w & jnp.uint32(0xFFFF0000), F32).astype(BF16)
    b = lax.bitcast_convert_type(w << 16, F32).astype(BF16)
    return a, b


def _out_router_kernel(n_tiles, mix_ref, x_ref, w_ref, g_ref, wrh_ref, wrl_ref, br_ref,
                       x1_ref, hm_ref, idx_ref, gate_ref):
    i = pl.program_id(0)

    @pl.when(i < n_tiles)
    def _():
        _out_router_tile(mix_ref, x_ref, w_ref, g_ref, wrh_ref, wrl_ref, br_ref,
                         x1_ref, hm_ref, idx_ref, gate_ref)

    @pl.when(i >= n_tiles)
    def _():
        hm_ref[...] = jnp.zeros(hm_ref.shape, F32)


def _out_router_tile(mix_ref, x_ref, w_ref, g_ref, wrh_ref, wrl_ref, br_ref,
                     x1_ref, hm_ref, idx_ref, gate_ref):
    x1 = x_ref[...] + _dot(mix_ref[...], w_ref[...])
    x1_ref[...] = x1
    hn = x1 * lax.rsqrt(jnp.mean(x1 * x1, axis=-1, keepdims=True) + EPS) * g_ref[...]
    hm_ref[...] = _pack_bf16_pair(hn[:, :_HALF], hn[:, _HALF:])
    hh, hl = _split_bf16(hn)
    logits = _dot(hh, wrh_ref[...]) + (_dot(hh, wrl_ref[...]) + _dot(hl, wrh_ref[...])) + br_ref[...]
    lane = lax.broadcasted_iota(I32, logits.shape, 1).astype(F32)
    vals = []
    idx_out = jnp.zeros(logits.shape, F32)
    for k in range(TOP_K):
        m = jnp.max(logits, axis=-1, keepdims=True)
        sel = jnp.min(jnp.where(logits == m, lane, 1e9), axis=-1, keepdims=True)
        vals.append(m)
        idx_out = jnp.where(lane == float(k), sel, idx_out)
        logits = jnp.where(lane == sel, -jnp.inf, logits)
    exps = [jnp.exp(v - vals[0]) for v in vals]
    denom = exps[0] + exps[1] + exps[2] + exps[3]
    gate_out = jnp.zeros(idx_out.shape, F32)
    for k in range(TOP_K):
        gate_out = jnp.where(lane == float(k), exps[k] / denom, gate_out)
    idx_ref[...] = idx_out.astype(I32)
    gate_ref[...] = gate_out


def _out_router(mix, x, w_out, g_ffn, wr_hi, wr_lo, b_r):
    n = x.shape[0]
    tm = 256
    n_tiles = n // tm
    const = lambda i: (0, 0)
    row = lambda i: (jnp.minimum(i, n_tiles - 1), 0)
    return pl.pallas_call(
        functools.partial(_out_router_kernel, n_tiles),
        grid=(2 * n_tiles,),
        in_specs=[
            pl.BlockSpec((tm, D_MODEL), row),
            pl.BlockSpec((tm, D_MODEL), row),
            pl.BlockSpec((D_MODEL, D_MODEL), const),
            pl.BlockSpec((1, D_MODEL), const),
            pl.BlockSpec((D_MODEL, 128), const),
            pl.BlockSpec((D_MODEL, 128), const),
            pl.BlockSpec((1, 128), const),
        ],
        out_specs=[
            pl.BlockSpec((tm, D_MODEL), row),
            pl.BlockSpec((tm, _HALF), lambda i: (i, 0)),
            pl.BlockSpec((tm, 128), row),
            pl.BlockSpec((tm, 128), row),
        ],
        out_shape=[
            jax.ShapeDtypeStruct((n, D_MODEL), F32),
            jax.ShapeDtypeStruct((2 * n, _HALF), F32),
            jax.ShapeDtypeStruct((n, 128), I32),
            jax.ShapeDtypeStruct((n, 128), F32),
        ],
        compiler_params=_cparams(1),
        name="out_router",
    )(mix, x, w_out, g_ffn, wr_hi, wr_lo, b_r)


_F_VALID, _F_FIRST, _F_NEXT, _F_GROUP0, _F_SLOT = 1, 2, 4, 8, 16


def _stream_weights(t, se_ref, sw_ref, ne_ref, nw_ref, fl_ref, copies, cast):
    flags = fl_ref[t]

    @pl.when((flags & _F_FIRST) != 0)
    def _():
        slot = (flags // _F_SLOT) & 1
        cur = copies(se_ref[t], sw_ref[t], slot)

        @pl.when((flags & _F_GROUP0) != 0)
        def _():
            for c in cur:
                c.start()

        for c in cur:
            c.wait()

        @pl.when((flags & _F_NEXT) != 0)
        def _():
            for c in copies(ne_ref[t], nw_ref[t], 1 - slot):
                c.start()

        cast(slot)


def _moe_up_kernel(se_ref, sw_ref, sn_ref, sb_ref, ne_ref, nw_ref, fl_ref,
                   prev_ref, x_ref, w_hbm, bg_ref, bu_ref, o_ref, wbuf_ref, wgb_ref, wub_ref, sem_ref):
    del prev_ref
    t = pl.program_id(0)

    def copies(e, w, slot):
        col = pl.multiple_of(w * _UP_TN, _UP_TN)
        return (pltpu.make_async_copy(w_hbm.at[e, :, pl.ds(col, _UP_TN)], wbuf_ref.at[slot, 0], sem_ref.at[slot, 0]),
                pltpu.make_async_copy(w_hbm.at[e, :, pl.ds(col + D_FF, _UP_TN)], wbuf_ref.at[slot, 1],
                                      sem_ref.at[slot, 1]))

    def cast(slot):
        wgb_ref[...] = wbuf_ref[slot, 0].astype(BF16)
        wub_ref[...] = wbuf_ref[slot, 1].astype(BF16)

    _stream_weights(t, se_ref, sw_ref, ne_ref, nw_ref, fl_ref, copies, cast)
    valid = (fl_ref[t] & _F_VALID) != 0

    @pl.when(valid)
    def _():
        xa, xb = _unpack_bf16_pair(x_ref[...])
        g = _dot(xa, wgb_ref[:_HALF, :]) + _dot(xb, wgb_ref[_HALF:, :]) + bg_ref[0]
        u = _dot(xa, wub_ref[:_HALF, :]) + _dot(xb, wub_ref[_HALF:, :]) + bu_ref[0]
        g = jnp.minimum(g, SWIGLU_LIMIT)
        u = jnp.clip(u, -SWIGLU_LIMIT, SWIGLU_LIMIT)
        o_ref[...] = ((u + 1.0) * (g * _sigmoid(SWIGLU_ALPHA * g))).astype(BF16)

    @pl.when(jnp.logical_not(valid))
    def _():
        o_ref[...] = jnp.zeros(o_ref.shape, BF16)


_UP_TN = 512
_UP_TILES = D_FF // _UP_TN
_DN_TN = 1024
_DN_TILES = D_MODEL // _DN_TN
MOE_CHUNKS = 4
_CHUNK_BLKS = MOE_MAX_BLKS // MOE_CHUNKS


def _moe_up(plan, act_prev, xs, w_gu, b_gu, chunk):
    steps = plan[0].shape[0]
    blk0 = chunk * _CHUNK_BLKS
    bspec = lambda off: pl.BlockSpec((1, 1, _UP_TN), lambda t, se, sw, sn, sb, ne, nw, fl: (se[t], 0, sw[t] + off))
    aliases = {} if act_prev is None else {len(plan): 0}
    prev = jnp.zeros((8, 128), BF16) if act_prev is None else act_prev
    return pl.pallas_call(
        _moe_up_kernel,
        grid_spec=pltpu.PrefetchScalarGridSpec(
            num_scalar_prefetch=len(plan),
            grid=(steps,),
            in_specs=[
                pl.BlockSpec(memory_space=pl.ANY),
                pl.BlockSpec((MOE_BLK, _HALF), lambda t, se, sw, sn, sb, ne, nw, fl: (sb[t], 0)),
                pl.BlockSpec(memory_space=pl.ANY),
                bspec(0), bspec(_UP_TILES),
            ],
            out_specs=pl.BlockSpec((MOE_BLK, _UP_TN),
                                   lambda t, se, sw, sn, sb, ne, nw, fl: (blk0 + sb[t], sn[t])),
            scratch_shapes=[pltpu.VMEM((2, 2, D_MODEL, _UP_TN), F32),
                            pltpu.VMEM((D_MODEL, _UP_TN), BF16), pltpu.VMEM((D_MODEL, _UP_TN), BF16),
                            pltpu.SemaphoreType.DMA((2, 2))],
        ),
        out_shape=jax.ShapeDtypeStruct((MOE_ROWS, D_FF), BF16),
        input_output_aliases=aliases,
        compiler_params=_cparams(1),
        name=f"moe_up_{chunk}",
    )(*plan, prev, xs, w_gu, b_gu, b_gu)


_DN_HALF = _DN_TN // 2


def _moe_down_kernel(se_ref, sw_ref, sn_ref, sb_ref, ne_ref, nw_ref, fl_ref,
                     a_ref, w_hbm, b_ref, o_ref, wbuf_ref, wb_ref, sem_ref):
    t = pl.program_id(0)

    def copies(e, w, slot):
        col = pl.multiple_of(w * _DN_TN, _DN_TN)
        return (pltpu.make_async_copy(w_hbm.at[e, :, pl.ds(col, _DN_TN)], wbuf_ref.at[slot], sem_ref.at[slot]),)

    def cast(slot):
        wb_ref[...] = wbuf_ref[slot].astype(BF16)

    _stream_weights(t, se_ref, sw_ref, ne_ref, nw_ref, fl_ref, copies, cast)
    valid = (fl_ref[t] & _F_VALID) != 0

    @pl.when(valid)
    def _():
        y = _dot(a_ref[...], wb_ref[...]) + b_ref[0]
        o_ref[...] = _pack_bf16_pair(y[:, :_DN_HALF], y[:, _DN_HALF:])

    @pl.when(jnp.logical_not(valid))
    def _():
        o_ref[...] = jnp.zeros(o_ref.shape, F32)


def _moe_down(plan, act, w_dn, b_dn):
    steps = plan[0].shape[0]
    return pl.pallas_call(
        _moe_down_kernel,
        grid_spec=pltpu.PrefetchScalarGridSpec(
            num_scalar_prefetch=len(plan),
            grid=(steps,),
            in_specs=[
                pl.BlockSpec((MOE_BLK, D_FF), lambda t, se, sw, sn, sb, ne, nw, fl: (sb[t], 0)),
                pl.BlockSpec(memory_space=pl.ANY),
                pl.BlockSpec((1, 1, _DN_TN), lambda t, se, sw, sn, sb, ne, nw, fl: (se[t], 0, sw[t])),
            ],
            out_specs=pl.BlockSpec((MOE_BLK, _DN_HALF), lambda t, se, sw, sn, sb, ne, nw, fl: (sb[t], sn[t])),
            scratch_shapes=[pltpu.VMEM((2, D_FF, _DN_TN), F32), pltpu.VMEM((D_FF, _DN_TN), BF16),
                            pltpu.SemaphoreType.DMA((2,))],
        ),
        out_shape=jax.ShapeDtypeStruct((MOE_ROWS, _HALF), F32),
        compiler_params=_cparams(1),
        name="moe_down",
    )(*plan, act, w_dn, b_dn)


def _moe_dispatch(top_idx):
    n_asg = N_TOK * TOP_K
    flat_e = top_idx.reshape(-1)
    onehot = (flat_e[:, None] == jnp.arange(N_EXPERTS, dtype=I32)[None, :]).astype(I32)
    csum = jnp.cumsum(onehot, axis=0)
    counts = csum[-1]
    rank = jnp.sum(csum * onehot, axis=1) - 1
    nblk = (counts + MOE_BLK - 1) // MOE_BLK
    blk_start = jnp.cumsum(nblk) - nblk
    dest = jnp.sum(onehot * blk_start[None, :], axis=1) * MOE_BLK + rank
    pad_src = jnp.arange(MOE_ROWS, dtype=I32) % N_TOK
    row_tok = pad_src.at[dest].set(jnp.arange(n_asg, dtype=I32) // TOP_K,
                                   mode="promise_in_bounds", unique_indices=True)
    return dest, row_tok, nblk, blk_start


def _moe_steps(nblk, blk_start, n_tiles, blk_lo, n_blks):
    t_max = n_tiles * n_blks
    lo = jnp.clip(blk_start, blk_lo, blk_lo + n_blks)
    hi = jnp.clip(blk_start + nblk, blk_lo, blk_lo + n_blks)
    nb_e = hi - lo
    per_e = nb_e * n_tiles
    s_end = jnp.cumsum(per_e)
    total = s_end[-1]
    t = jnp.arange(t_max, dtype=I32)
    tc = jnp.clip(t, 0, jnp.maximum(total - 1, 0))
    e = jnp.minimum(jnp.sum((s_end[None, :] <= tc[:, None]).astype(I32), axis=1), N_EXPERTS - 1)
    sel = (e[:, None] == jnp.arange(N_EXPERTS, dtype=I32)[None, :]).astype(I32)
    pick = lambda v: jnp.sum(sel * v[None, :], axis=1)
    local = tc - pick(s_end - per_e)
    nb = jnp.maximum(pick(nb_e), 1)
    w_tile = jnp.clip(local // nb, 0, n_tiles - 1)
    r = local % nb
    valid = t < total
    first = jnp.logical_and(valid, r == 0)
    fill = t - total
    blk = jnp.where(valid, pick(lo) - blk_lo + r, total // n_tiles + fill // n_tiles)
    o_tile = jnp.where(valid, w_tile, fill % n_tiles)
    blk = jnp.clip(blk, 0, n_blks - 1)
    ids = jnp.arange(N_EXPERTS, dtype=I32)
    owners = jnp.where(nb_e > 0, ids, N_EXPERTS)
    later = jnp.flip(lax.cummin(jnp.flip(owners)))
    next_owner = pick(jnp.concatenate([later[1:], jnp.full((1,), N_EXPERTS, I32)]))
    last_tile = w_tile == n_tiles - 1
    next_e = jnp.where(last_tile, next_owner, e)
    next_w = jnp.where(last_tile, 0, w_tile + 1)
    has_next = jnp.logical_and(first, next_e < N_EXPERTS)
    group = jnp.cumsum(first.astype(I32)) - 1
    flags = (valid * _F_VALID + first * _F_FIRST + has_next * _F_NEXT
             + jnp.logical_and(first, group == 0) * _F_GROUP0 + (group % 2) * _F_SLOT)
    return e, w_tile, o_tile, blk, jnp.minimum(next_e, N_EXPERTS - 1), next_w, flags.astype(I32)


_FIN_TM = 256
_FIN_TN = 512
FIN_CHUNKS = 4


def _unpack_expert_rows(words):
    u = lax.bitcast_convert_type(words, U32)
    hi = lax.bitcast_convert_type(u & jnp.uint32(0xFFFF0000), F32)
    lo = lax.bitcast_convert_type(u << 16, F32)
    parts = []
    for n in range(_DN_TILES):
        cols = slice(n * _DN_HALF, (n + 1) * _DN_HALF)
        parts += [hi[:, cols], lo[:, cols]]
    return jnp.concatenate(parts, axis=1)


def _final_kernel(prev_ref, x1_ref, y0_ref, y1_ref, y2_ref, y3_ref, gate_ref, g_ref, wg_ref, p_ref, wp_ref,
                  o_ref, x2_ref):
    del prev_ref
    gate = gate_ref[...]
    moe = (_unpack_expert_rows(y0_ref[0]) * gate[:, 0:1] + _unpack_expert_rows(y1_ref[0]) * gate[:, 1:2]
           + _unpack_expert_rows(y2_ref[0]) * gate[:, 2:3] + _unpack_expert_rows(y3_ref[0]) * gate[:, 3:4])
    x2 = x1_ref[...] + moe
    x2_ref[...] = x2
    hp = (x2 * lax.rsqrt(jnp.mean(x2 * x2, axis=-1, keepdims=True) + EPS) * g_ref[...]).astype(BF16)
    pb = p_ref[...].astype(BF16)
    for c in range(0, D_MODEL, _FIN_TN):
        cols = slice(c, c + _FIN_TN)
        emb = _dot(pb, wp_ref[:, cols])
        o_ref[:, cols] = x2_ref[:, cols] + _sigmoid(_dot(hp, wg_ref[:, cols])) * emb


def _final(out_prev, x1, y4, gate, g_ple, w_ple_gate, p, w_ple, tok0, out0, n, n_out, name):
    t0 = tok0 // _FIN_TM
    o0 = out0 // _FIN_TM
    pt0 = out0 // _FIN_TM
    const = lambda i: (0, 0)
    yspec = lambda k: pl.BlockSpec((1, _FIN_TM, _HALF), lambda i: (k, i, 0))
    once = pl.Buffered(1)
    aliases = {} if out_prev is None else {0: 0}
    prev = jnp.zeros((8, 128), F32) if out_prev is None else out_prev
    return pl.pallas_call(
        _final_kernel,
        grid=(n // _FIN_TM,),
        in_specs=[
            pl.BlockSpec(memory_space=pl.ANY),
            pl.BlockSpec((_FIN_TM, D_MODEL), lambda i: (t0 + i, 0)),
            yspec(0), yspec(1), yspec(2), yspec(3),
            pl.BlockSpec((_FIN_TM, 128), lambda i: (t0 + i, 0)),
            pl.BlockSpec((1, D_MODEL), const),
            pl.BlockSpec((D_MODEL, D_MODEL), const, pipeline_mode=once),
            pl.BlockSpec((_FIN_TM, PLE_DIM), lambda i: (pt0 + i, 0)),
            pl.BlockSpec((PLE_DIM, D_MODEL), const, pipeline_mode=once),
        ],
        out_specs=pl.BlockSpec((_FIN_TM, D_MODEL), lambda i: (o0 + i, 0)),
        out_shape=jax.ShapeDtypeStruct((n_out, D_MODEL), F32),
        scratch_shapes=[pltpu.VMEM((_FIN_TM, D_MODEL), F32)],
        input_output_aliases=aliases,
        compiler_params=_cparams(1),
        name=name,
    )(prev, x1, y4, y4, y4, y4, gate, g_ple, w_ple_gate, p, w_ple)


def _rope_layout(x):
    half = ROPE_DIM // 2
    z = jnp.zeros(x.shape[:-1] + (half,), x.dtype)
    return jnp.concatenate([x[..., :half], z, x[..., half:], z], axis=-1)


def _rope_tables():
    half = ROPE_DIM // 2
    inv_freq = ROPE_THETA ** (-jnp.arange(half, dtype=F32) / half)
    pos = jnp.arange(PAST_LEN + DEC_SEQ, dtype=I32)
    ang = pos.astype(F32)[:, None] * inv_freq[None, :]
    cos, sin = jnp.cos(ang), jnp.sin(ang)
    z = jnp.zeros_like(cos)
    c = jnp.concatenate([cos, z, cos, z], axis=-1)
    s = jnp.concatenate([-sin, z, sin, z], axis=-1)
    rep = ATT_TM // DEC_SEQ
    return (jnp.concatenate([c[:SEQ], jnp.tile(c[PAST_LEN:], (rep, 1))], axis=0),
            jnp.concatenate([s[:SEQ], jnp.tile(s[PAST_LEN:], (rep, 1))], axis=0))


def _layer(x, p_prompt, p_sample, cache_kv, cache_kr, state_conv,
           g_mix, w_in, b_gate, w_dw, b_dw, g_cn, b_cn, w_conv_out,
           g_qa, g_kva, w_qb, w_kb, w_vb, g_qn, g_kn, w_o, w_out,
           g_ffn, w_router, b_router, w_gu, b_gu, w_dn, b_dn,
           g_ple, w_ple_gate, w_ple):
    assert SEQ == PAST_LEN
    row = lambda v: v.reshape(1, -1)
    w_in_b = w_in.astype(BF16)
    w_mid = jnp.concatenate([w_in_b[:, O_U:O_KV], _rope_layout(w_in_b[:, O_KV:O_KR])], axis=1)
    w_gate = w_in_b[:, O_KR:]

    h, q_lat, kv_new, kr_pad = _in_mid(x, row(g_mix), w_mid, row(g_qa), row(g_kva))
    half = ROPE_DIM // 2
    kr_new = jnp.concatenate([kr_pad[:, :half], kr_pad[:, 2 * half:3 * half]], axis=1)
    glu = _in_glu(h, w_in_b)

    hist = jnp.concatenate([jnp.zeros((BATCH, HALO, CONV_CHANNELS), F32),
                            jnp.pad(state_conv, ((0, 0), (HALO - (CONV_WIDTH - 1), 0), (0, 0)))], axis=0)
    c_act = _conv_module(glu, hist, w_dw, row(b_dw), row(g_cn), row(b_cn))

    cos_t, sin_t = _rope_tables()
    w_q = jnp.concatenate([w_qb[..., :NOPE_DIM], _rope_layout(w_qb[..., NOPE_DIM:])], axis=-1)
    w_q = w_q.reshape(Q_LORA_RANK, N_HEADS * HEAD_PAD).astype(BF16)
    g_q = jnp.concatenate([g_qn[:NOPE_DIM], _rope_layout(g_qn[NOPE_DIM:])]).reshape(1, HEAD_PAD)
    q = _q_heads(q_lat, w_q, g_q, cos_t, sin_t)

    w_kv = jnp.concatenate([w_kb, w_vb], axis=-1).reshape(KV_LORA_RANK, N_HEADS * HEAD_PAD).astype(BF16)
    g_kn_nope = g_kn[:NOPE_DIM].reshape(1, NOPE_DIM)
    g_kn_rope = _rope_layout(g_kn[NOPE_DIM:]).reshape(1, 128)
    k_new, v_new = _kv_heads(kv_new, kr_pad, w_kv, g_kn_nope, g_kn_rope, cos_t, sin_t,
                             _tab_idx_new, "kv_heads_new")
    k_cache, v_cache = _kv_heads(cache_kv.reshape(DEC_BATCH * PAST_LEN, KV_LORA_RANK),
                                 _rope_layout(cache_kr).reshape(DEC_BATCH * PAST_LEN, 128),
                                 w_kv, g_kn_nope, g_kn_rope, cos_t, sin_t, _tab_idx_cache, "kv_heads_cache")

    attn = _flash_prompt(q, k_new, v_new)
    attn = _flash_sample(attn, q, k_cache, v_cache, k_new, v_new)

    mix = _merge(h, c_act, attn, w_gate, row(b_gate), w_conv_out.astype(BF16), w_o.astype(BF16))

    wr = jnp.pad(w_router, ((0, 0), (0, 128 - N_EXPERTS)))
    wr_hi, wr_lo = _split_bf16(wr)
    b_r = jnp.concatenate([b_router, jnp.full((128 - N_EXPERTS,), -jnp.inf, F32)]).reshape(1, 128)
    x1, hm, idx_pad, gate_pad = _out_router(mix, x, w_out.astype(BF16), row(g_ffn), wr_hi, wr_lo, b_r)

    top_idx = idx_pad[:, :TOP_K]
    dest, row_tok, nblk, blk_start = _moe_dispatch(top_idx)
    b_gu3 = b_gu.reshape(N_EXPERTS, 1, 2 * D_FF)
    chunk_rows = _CHUNK_BLKS * MOE_BLK
    act = None
    for c in range(MOE_CHUNKS):
        xs = hm.at[row_tok[c * chunk_rows:(c + 1) * chunk_rows]].get(mode="promise_in_bounds")
        plan = _moe_steps(nblk, blk_start, _UP_TILES, c * _CHUNK_BLKS, _CHUNK_BLKS)
        act = _moe_up(plan, act, xs, w_gu, b_gu3, c)
    ys = _moe_down(_moe_steps(nblk, blk_start, _DN_TILES, 0, MOE_MAX_BLKS), act, w_dn,
                   b_dn.reshape(N_EXPERTS, 1, D_MODEL))

    dest_t = dest.reshape(N_TOK, TOP_K).T
    fin = (row(g_ple), w_ple_gate.astype(BF16))
    w_ple_b = w_ple.astype(BF16)
    n_c = N_P // FIN_CHUNKS
    out_p = None
    for c in range(FIN_CHUNKS):
        y4 = ys.at[dest_t[:, c * n_c:(c + 1) * n_c]].get(mode="promise_in_bounds")
        out_p = _final(out_p, x1, y4, gate_pad, *fin, p_prompt, w_ple_b, c * n_c, c * n_c, n_c, N_P,
                       f"final_prompt_{c}")
    y4 = ys.at[dest_t[:, N_P:]].get(mode="promise_in_bounds")
    out_s = _final(None, x1, y4, gate_pad, *fin, p_sample, w_ple_b, N_P, 0, N_S, N_S, "final_sample")
    return out_p, out_s, kv_new, kr_new, glu


def kernel(x_prompt, x_sample, cache_kv_latent, cache_k_rope, state_conv, p_prompt, p_sample, g_mix, w_in, b_gate, w_dw, b_dw, g_cn, b_cn, w_conv_out, g_qa, g_kva, w_qb, w_kb, w_vb, g_qn, g_kn, w_o, w_out, g_ffn, w_router, b_router, w_gu, b_gu, w_dn, b_dn, g_ple, w_ple_gate, w_ple):
    assert g_mix.shape[0] == 1
    x = jnp.concatenate([x_prompt.reshape(N_P, D_MODEL), x_sample.reshape(N_S, D_MODEL)], axis=0)
    out_p, out_s, kv_new, kr_new, glu = _layer(
        x, p_prompt[0].reshape(N_P, PLE_DIM), p_sample[0].reshape(N_S, PLE_DIM),
        cache_kv_latent[0], cache_k_rope[0], state_conv[0],
        g_mix[0], w_in[0], b_gate[0], w_dw[0], b_dw[0], g_cn[0], b_cn[0], w_conv_out[0],
        g_qa[0], g_kva[0], w_qb[0], w_kb[0], w_vb[0], g_qn[0], g_kn[0], w_o[0], w_out[0],
        g_ffn[0], w_router[0], b_router[0], w_gu[0], b_gu[0], w_dn[0], b_dn[0],
        g_ple[0], w_ple_gate[0], w_ple[0])
    tail = CONV_WIDTH - 1
    conv_p = jnp.stack([glu[(b + 1) * SEQ - tail:(b + 1) * SEQ] for b in range(BATCH)])
    conv_s = glu[N_P:].reshape(DEC_BATCH, DEC_SEQ, CONV_CHANNELS)[:, DEC_SEQ - tail:]
    return (out_p.reshape(BATCH, SEQ, D_MODEL),
            out_s.reshape(DEC_BATCH, DEC_SEQ, D_MODEL),
            kv_new[:N_P].reshape(1, BATCH, SEQ, KV_LORA_RANK),
            kr_new[:N_P].reshape(1, BATCH, SEQ, ROPE_DIM),
            conv_p[None],
            kv_new[N_P:].reshape(1, DEC_BATCH, DEC_SEQ, KV_LORA_RANK),
            kr_new[N_P:].reshape(1, DEC_BATCH, DEC_SEQ, ROPE_DIM),
            conv_s[None])
```

```python
import functools
import math

import jax
import jax.numpy as jnp
from jax import lax
from jax.experimental import pallas as pl
from jax.experimental.pallas import tpu as pltpu

F32 = jnp.float32
BF16 = jnp.bfloat16
I32 = jnp.int32
U32 = jnp.uint32

D_MODEL = 2048
BATCH = 2
SEQ = 4096
DEC_BATCH = 8
DEC_SEQ = 64
PAST_LEN = 4096
CHUNK = 64
CONV_CHANNELS = D_MODEL
CONV_WIDTH = 31
N_HEADS = 16
Q_LORA_RANK = 512
KV_LORA_RANK = 512
NOPE_DIM = 128
ROPE_DIM = 64
QK_DIM = NOPE_DIM + ROPE_DIM
V_DIM = 128
ROPE_THETA = 10000.0
N_EXPERTS = 32
TOP_K = 4
D_FF = D_MODEL
SWIGLU_ALPHA = 1.702
SWIGLU_LIMIT = 7.0
PLE_DIM = 256
EPS = 1e-6
NEG_INF = -1e30

N_P = BATCH * SEQ
N_S = DEC_BATCH * DEC_SEQ
N_TOK = N_P + N_S
O_U = 2 * CONV_CHANNELS
O_Q = O_U + Q_LORA_RANK
O_KV = O_Q + KV_LORA_RANK
O_KR = O_KV + ROPE_DIM
MID_W = 1152
HEAD_PAD = 256

TM = 512
CONV_T = 64
HALO = 32
MOE_BLK = 512
MOE_MAX_BLKS = (N_TOK * TOP_K) // MOE_BLK + N_EXPERTS
MOE_ROWS = MOE_MAX_BLKS * MOE_BLK
VMEM_LIMIT = 48 * 1024 * 1024


def _cparams(n_axes):
    return pltpu.CompilerParams(dimension_semantics=("arbitrary",) * n_axes,
                                vmem_limit_bytes=VMEM_LIMIT)


def _sigmoid(x):
    return 1.0 / (1.0 + jnp.exp(-x))


def _dot(a, b):
    return jnp.dot(a, b, preferred_element_type=F32)


def _stacked_rows(i, n_prompt_tiles, xp_ref, xs_ref):
    return jnp.where(i < n_prompt_tiles, xp_ref[...], xs_ref[...])


def _in_mid_kernel(xp_ref, xs_ref, g_ref, w_ref, gqa_ref, gkva_ref, h_ref, q_ref, kv_ref, kr_ref):
    x = _stacked_rows(pl.program_id(0), N_P // TM, xp_ref, xs_ref)
    h = x * lax.rsqrt(jnp.mean(x * x, axis=-1, keepdims=True) + EPS) * g_ref[...]
    hb = h.astype(BF16)
    h_ref[...] = hb
    z = _dot(hb, w_ref[...])
    ql = z[:, :Q_LORA_RANK]
    kvl = z[:, Q_LORA_RANK:Q_LORA_RANK + KV_LORA_RANK]
    qn = ql * lax.rsqrt(jnp.mean(ql * ql, axis=-1, keepdims=True) + EPS) * gqa_ref[...]
    q_ref[...] = qn.astype(BF16)
    kv_ref[...] = kvl * lax.rsqrt(jnp.mean(kvl * kvl, axis=-1, keepdims=True) + EPS) * gkva_ref[...]
    kr_ref[...] = z[:, Q_LORA_RANK + KV_LORA_RANK:]


def _in_mid(xp, xs, g_mix, w_mid, g_qa, g_kva):
    n = N_TOK
    npt = N_P // TM
    return pl.pallas_call(
        _in_mid_kernel,
        grid=(n // TM,),
        in_specs=[
            pl.BlockSpec((TM, D_MODEL), lambda i: (jnp.minimum(i, npt - 1), 0)),
            pl.BlockSpec((TM, D_MODEL), lambda i: (jnp.maximum(i - npt, 0), 0)),
            pl.BlockSpec((1, D_MODEL), lambda i: (0, 0)),
            pl.BlockSpec((D_MODEL, MID_W), lambda i: (0, 0)),
            pl.BlockSpec((1, Q_LORA_RANK), lambda i: (0, 0)),
            pl.BlockSpec((1, KV_LORA_RANK), lambda i: (0, 0)),
        ],
        out_specs=[
            pl.BlockSpec((TM, D_MODEL), lambda i: (i, 0)),
            pl.BlockSpec((TM, Q_LORA_RANK), lambda i: (i, 0)),
            pl.BlockSpec((TM, KV_LORA_RANK), lambda i: (i, 0)),
            pl.BlockSpec((TM, 128), lambda i: (i, 0)),
        ],
        out_shape=[
            jax.ShapeDtypeStruct((n, D_MODEL), BF16),
            jax.ShapeDtypeStruct((n, Q_LORA_RANK), BF16),
            jax.ShapeDtypeStruct((n, KV_LORA_RANK), F32),
            jax.ShapeDtypeStruct((n, 128), F32),
        ],
        compiler_params=_cparams(1),
        name="in_mid",
    )(xp, xs, g_mix, w_mid, g_qa, g_kva)


def _glu_kernel(h_ref, w1_ref, w2_ref, o_ref):
    h = h_ref[...]
    o_ref[...] = _dot(h, w1_ref[...]) * _sigmoid(_dot(h, w2_ref[...]))


def _in_glu(h, w_in_b):
    n = h.shape[0]
    tn = 512
    nj = CONV_CHANNELS // tn
    return pl.pallas_call(
        _glu_kernel,
        grid=(n // TM, nj),
        in_specs=[
            pl.BlockSpec((TM, D_MODEL), lambda i, j: (i, 0)),
            pl.BlockSpec((D_MODEL, tn), lambda i, j: (0, j)),
            pl.BlockSpec((D_MODEL, tn), lambda i, j: (0, j + nj)),
        ],
        out_specs=pl.BlockSpec((TM, tn), lambda i, j: (i, j)),
        out_shape=jax.ShapeDtypeStruct((n, CONV_CHANNELS), F32),
        compiler_params=_cparams(2),
        name="in_glu",
    )(h, w_in_b, w_in_b)


_CONV_TILES_PER_SEQ = SEQ // CONV_T
_CONV_PROMPT_TILES = N_P // CONV_T
_CONV_LANES = 512
SUBLANES = 8
_SHIFT_ROWS = (HALO // SUBLANES - 1) * SUBLANES + CONV_T


def _conv_kernel(cur_ref, prev_ref, hist_ref, w_ref, bdw_ref, g_ref, b_ref, o_ref, win_ref, conv_ref, shift_ref):
    i = pl.program_id(0)
    first = jnp.logical_or(i >= _CONV_PROMPT_TILES, i % _CONV_TILES_PER_SEQ == 0)

    @pl.when(first)
    def _():
        win_ref[0:HALO, :] = hist_ref[0]

    @pl.when(jnp.logical_not(first))
    def _():
        win_ref[0:HALO, :] = prev_ref[...]

    win_ref[HALO:HALO + CONV_T, :] = cur_ref[...]
    for r in range(1, SUBLANES):
        shift_ref[r - 1] = win_ref[r:r + _SHIFT_ROWS, :]
    base = HALO - (CONV_WIDTH - 1)
    for c in range(0, CONV_CHANNELS, _CONV_LANES):
        acc = jnp.zeros((CONV_T, _CONV_LANES), F32)
        for k in range(CONV_WIDTH):
            q, r = divmod(base + k, SUBLANES)
            lanes = slice(c, c + _CONV_LANES)
            rows = slice(q * SUBLANES, q * SUBLANES + CONV_T)
            src = win_ref[rows, lanes] if r == 0 else shift_ref[r - 1, rows, lanes]
            acc = acc + w_ref[k:k + 1, lanes] * src
        conv_ref[:, c:c + _CONV_LANES] = acc + bdw_ref[:, c:c + _CONV_LANES]
    y = conv_ref[...]
    yc = y - jnp.mean(y, axis=-1, keepdims=True)
    var = jnp.mean(yc * yc, axis=-1, keepdims=True)
    z = yc * lax.rsqrt(var + EPS) * g_ref[...] + b_ref[...]
    o_ref[...] = (z * _sigmoid(z)).astype(BF16)


def _conv_module(glu, hist, w_dw, b_dw, g_cn, b_cn):
    n = glu.shape[0]
    n_tiles = n // CONV_T
    halo_per_tile = CONV_T // HALO

    def seq_of(i):
        return jnp.where(i < _CONV_PROMPT_TILES, i // _CONV_TILES_PER_SEQ, i - _CONV_PROMPT_TILES + BATCH)

    return pl.pallas_call(
        _conv_kernel,
        grid=(n_tiles,),
        in_specs=[
            pl.BlockSpec((CONV_T, CONV_CHANNELS), lambda i: (i, 0)),
            pl.BlockSpec((HALO, CONV_CHANNELS), lambda i: (jnp.maximum(i * halo_per_tile - 1, 0), 0)),
            pl.BlockSpec((1, HALO, CONV_CHANNELS), lambda i: (seq_of(i), 0, 0)),
            pl.BlockSpec((CONV_WIDTH, CONV_CHANNELS), lambda i: (0, 0)),
            pl.BlockSpec((1, CONV_CHANNELS), lambda i: (0, 0)),
            pl.BlockSpec((1, CONV_CHANNELS), lambda i: (0, 0)),
            pl.BlockSpec((1, CONV_CHANNELS), lambda i: (0, 0)),
        ],
        out_specs=pl.BlockSpec((CONV_T, CONV_CHANNELS), lambda i: (i, 0)),
        out_shape=jax.ShapeDtypeStruct((n, CONV_CHANNELS), BF16),
        scratch_shapes=[pltpu.VMEM((HALO + CONV_T, CONV_CHANNELS), F32),
                        pltpu.VMEM((CONV_T, CONV_CHANNELS), F32),
                        pltpu.VMEM((SUBLANES - 1, _SHIFT_ROWS, CONV_CHANNELS), F32)],
        compiler_params=_cparams(1),
        name="conv_module",
    )(glu, glu, hist, w_dw, b_dw, g_cn, b_cn)


ATT_TM = 256
_TAB_PROMPT_TILES = N_P // ATT_TM
_TAB_SEQ_TILES = SEQ // ATT_TM
_TAB_ROWS = SEQ + ATT_TM


def _tab_idx_new(i):
    return jnp.where(i < _TAB_PROMPT_TILES, i % _TAB_SEQ_TILES, _TAB_SEQ_TILES)


def _rope_pair(u, c, s):
    return u * c + pltpu.roll(u, 64, 1) * s


_Q_SCALE = math.log2(math.e) / math.sqrt(QK_DIM)


def _q_heads_kernel(ql_ref, w_ref, g_ref, c_ref, s_ref, o_ref):
    ql = ql_ref[...]
    g = g_ref[...]
    c = c_ref[...]
    s = s_ref[...]
    for h in range(N_HEADS):
        qf = _dot(ql, w_ref[:, h * HEAD_PAD:(h + 1) * HEAD_PAD])
        ssq = jnp.sum(qf * qf, axis=-1, keepdims=True)
        qn = qf * (lax.rsqrt(ssq * (1.0 / QK_DIM) + EPS) * _Q_SCALE) * g
        o_ref[h, :, :NOPE_DIM] = qn[:, :NOPE_DIM].astype(BF16)
        o_ref[h, :, NOPE_DIM:] = _rope_pair(qn[:, NOPE_DIM:], c, s).astype(BF16)


def _q_heads(q_lat, w_q, g_q, cos_t, sin_t):
    n = q_lat.shape[0]
    return pl.pallas_call(
        _q_heads_kernel,
        grid=(n // ATT_TM,),
        in_specs=[
            pl.BlockSpec((ATT_TM, Q_LORA_RANK), lambda i: (i, 0)),
            pl.BlockSpec((Q_LORA_RANK, N_HEADS * HEAD_PAD), lambda i: (0, 0)),
            pl.BlockSpec((1, HEAD_PAD), lambda i: (0, 0)),
            pl.BlockSpec((ATT_TM, 128), lambda i: (_tab_idx_new(i), 0)),
            pl.BlockSpec((ATT_TM, 128), lambda i: (_tab_idx_new(i), 0)),
        ],
        out_specs=pl.BlockSpec((N_HEADS, ATT_TM, HEAD_PAD), lambda i: (0, i, 0)),
        out_shape=jax.ShapeDtypeStruct((N_HEADS, n, HEAD_PAD), BF16),
        compiler_params=_cparams(1),
        name="q_heads",
    )(q_lat, w_q, g_q, cos_t, sin_t)


def _kv_heads_kernel(kv_ref, kr_ref, w_ref, gn_ref, gr_ref, c_ref, s_ref, k_ref, v_ref):
    kv = kv_ref[...].astype(BF16)
    u = kr_ref[...]
    ssq_r = jnp.sum(u * u, axis=-1, keepdims=True)
    krot = _rope_pair(u * gr_ref[...], c_ref[...], s_ref[...])
    gn = gn_ref[...]
    for h in range(N_HEADS):
        z = _dot(kv, w_ref[:, h * HEAD_PAD:(h + 1) * HEAD_PAD])
        kn = z[:, :NOPE_DIM]
        ssq = jnp.sum(kn * kn, axis=-1, keepdims=True) + ssq_r
        scale = lax.rsqrt(ssq * (1.0 / QK_DIM) + EPS)
        k_ref[h, :, :NOPE_DIM] = (kn * scale * gn).astype(BF16)
        k_ref[h, :, NOPE_DIM:] = (krot * scale).astype(BF16)
        v_ref[h] = z[:, NOPE_DIM:].astype(BF16)


def _kv_heads(kv_lat, kr_pad, w_kv, g_kn_nope, g_kn_rope, cos_t, sin_t, tab_idx, name):
    n = kv_lat.shape[0]
    return pl.pallas_call(
        _kv_heads_kernel,
        grid=(n // ATT_TM,),
        in_specs=[
            pl.BlockSpec((ATT_TM, KV_LORA_RANK), lambda i: (i, 0)),
            pl.BlockSpec((ATT_TM, 128), lambda i: (i, 0)),
            pl.BlockSpec((KV_LORA_RANK, N_HEADS * HEAD_PAD), lambda i: (0, 0)),
            pl.BlockSpec((1, NOPE_DIM), lambda i: (0, 0)),
            pl.BlockSpec((1, 128), lambda i: (0, 0)),
            pl.BlockSpec((ATT_TM, 128), lambda i: (tab_idx(i), 0)),
            pl.BlockSpec((ATT_TM, 128), lambda i: (tab_idx(i), 0)),
        ],
        out_specs=[
            pl.BlockSpec((N_HEADS, ATT_TM, HEAD_PAD), lambda i: (0, i, 0)),
            pl.BlockSpec((N_HEADS, ATT_TM, V_DIM), lambda i: (0, i, 0)),
        ],
        out_shape=[
            jax.ShapeDtypeStruct((N_HEADS, n, HEAD_PAD), BF16),
            jax.ShapeDtypeStruct((N_HEADS, n, V_DIM), BF16),
        ],
        compiler_params=_cparams(1),
        name=name,
    )(kv_lat, kr_pad, w_kv, g_kn_nope, g_kn_rope, cos_t, sin_t)


_TQ = 512
_TKB = 512
_HB = 2


def _flash_prompt_kernel(q_ref, k_ref, v_ref, o_ref, m_ref, l_ref, acc_ref):
    qi = pl.program_id(2)
    m_ref[...] = jnp.full(m_ref.shape, NEG_INF, F32)
    l_ref[...] = jnp.zeros(l_ref.shape, F32)
    acc_ref[...] = jnp.zeros(acc_ref.shape, F32)
    nlb = _TKB // 128

    def step(ki, masked):
        start = pl.multiple_of(ki * _TKB, _TKB)
        for hh in range(_HB):
            k = k_ref[hh, pl.ds(start, _TKB), :]
            v = v_ref[hh, pl.ds(start, _TKB), :]
            s = lax.dot_general(q_ref[hh], k, (((1,), (1,)), ((), ())), preferred_element_type=F32)
            if masked:
                rc = lax.broadcasted_iota(I32, (_TQ, _TKB), 0) // CHUNK
                cc = lax.broadcasted_iota(I32, (_TQ, _TKB), 1) // CHUNK
                s = jnp.where(cc <= rc, s, NEG_INF)
            sb = [s[:, c * 128:(c + 1) * 128] for c in range(nlb)]
            bm = sb[0]
            for c in range(1, nlb):
                bm = jnp.maximum(bm, sb[c])
            m_prev = m_ref[hh]
            m_new = jnp.maximum(m_prev, jnp.max(bm, axis=-1, keepdims=True))
            alpha = jnp.exp2(m_prev - m_new)
            ps = [jnp.exp2(x - m_new) for x in sb]
            psum = ps[0]
            for c in range(1, nlb):
                psum = psum + ps[c]
            l_ref[hh] = alpha * l_ref[hh] + psum
            p = jnp.concatenate(ps, axis=1).astype(BF16)
            acc_ref[hh] = alpha * acc_ref[hh] + _dot(p, v)
            m_ref[hh] = m_new

    def body(ki, carry):
        step(ki, False)
        return carry

    lax.fori_loop(0, qi, body, 0)
    step(qi, True)
    for hh in range(_HB):
        l = jnp.sum(l_ref[hh], axis=-1, keepdims=True)
        o_ref[:, hh * V_DIM:(hh + 1) * V_DIM] = (acc_ref[hh] / l).astype(BF16)


def _flash_prompt(q, k, v):
    nq = SEQ // _TQ
    return pl.pallas_call(
        _flash_prompt_kernel,
        grid=(BATCH, N_HEADS // _HB, nq),
        in_specs=[
            pl.BlockSpec((_HB, _TQ, HEAD_PAD), lambda b, h, i: (h, b * nq + i, 0)),
            pl.BlockSpec((_HB, SEQ, HEAD_PAD), lambda b, h, i: (h, b, 0)),
            pl.BlockSpec((_HB, SEQ, V_DIM), lambda b, h, i: (h, b, 0)),
        ],
        out_specs=pl.BlockSpec((_TQ, _HB * V_DIM), lambda b, h, i: (b * nq + i, h)),
        out_shape=jax.ShapeDtypeStruct((N_TOK, N_HEADS * V_DIM), BF16),
        scratch_shapes=[pltpu.VMEM((_HB, _TQ, 128), F32), pltpu.VMEM((_HB, _TQ, 128), F32),
                        pltpu.VMEM((_HB, _TQ, V_DIM), F32)],
        compiler_params=_cparams(3),
        name="flash_prompt",
    )(q, k, v)


_KC_ROWS = 512


def _flash_sample_kernel(prev_ref, q_ref, kv_ref, kr_ref, w_ref, gn_ref, gr_ref, c_ref, s_ref, kn_ref, vn_ref,
                         o_ref, kvb_ref, krot_ref, ssqr_ref, k_ref, v_ref):
    del prev_ref

    @pl.when(pl.program_id(1) == 0)
    def _():
        kvb_ref[...] = kv_ref[...].astype(BF16)
        u = kr_ref[...]
        ssqr_ref[...] = jnp.sum(u * u, axis=-1, keepdims=True)
        krot_ref[...] = _rope_pair(u * gr_ref[...], c_ref[...], s_ref[...])

    gn = gn_ref[...]
    nt = (((1,), (1,)), ((), ()))
    for hh in range(_HB):
        w = w_ref[:, hh * HEAD_PAD:(hh + 1) * HEAD_PAD]
        for r in range(0, PAST_LEN, _KC_ROWS):
            rows = slice(r, r + _KC_ROWS)
            z = _dot(kvb_ref[rows, :], w)
            kn = z[:, :NOPE_DIM]
            ssq = jnp.sum(kn * kn, axis=-1, keepdims=True) + ssqr_ref[rows, :]
            scale = lax.rsqrt(ssq * (1.0 / QK_DIM) + EPS)
            k_ref[rows, :NOPE_DIM] = (kn * scale * gn).astype(BF16)
            k_ref[rows, NOPE_DIM:] = (krot_ref[rows, :] * scale).astype(BF16)
            v_ref[rows, :] = z[:, NOPE_DIM:].astype(BF16)
        q = q_ref[hh]
        s1 = lax.dot_general(q, k_ref[...], nt, preferred_element_type=F32)
        s2 = lax.dot_general(q, kn_ref[hh], nt, preferred_element_type=F32)
        m = jnp.maximum(jnp.max(s1, axis=-1, keepdims=True), jnp.max(s2, axis=-1, keepdims=True))
        p1 = jnp.exp2(s1 - m)
        p2 = jnp.exp2(s2 - m)
        l = jnp.sum(p1, axis=-1, keepdims=True) + jnp.sum(p2, axis=-1, keepdims=True)
        o = _dot(p1.astype(BF16), v_ref[...]) + _dot(p2.astype(BF16), vn_ref[hh])
        o_ref[:, hh * V_DIM:(hh + 1) * V_DIM] = (o / l).astype(BF16)


def _flash_sample(attn, q, cache_kv, cache_kr_pad, w_kv, g_kn_nope, g_kn_rope, cos_t, sin_t, k_new, v_new):
    assert (PAST_LEN + DEC_SEQ - 1) // CHUNK <= PAST_LEN // CHUNK
    blk0 = N_P // DEC_SEQ
    new = lambda b, h: (h, blk0 + b, 0)
    const = lambda b, h: (0, 0)
    once = pl.Buffered(1)
    return pl.pallas_call(
        _flash_sample_kernel,
        grid=(DEC_BATCH, N_HEADS // _HB),
        in_specs=[
            pl.BlockSpec(memory_space=pl.ANY),
            pl.BlockSpec((_HB, DEC_SEQ, HEAD_PAD), new),
            pl.BlockSpec((PAST_LEN, KV_LORA_RANK), lambda b, h: (b, 0)),
            pl.BlockSpec((PAST_LEN, 128), lambda b, h: (b, 0)),
            pl.BlockSpec((KV_LORA_RANK, _HB * HEAD_PAD), lambda b, h: (0, h)),
            pl.BlockSpec((1, NOPE_DIM), const),
            pl.BlockSpec((1, 128), const),
            pl.BlockSpec((PAST_LEN, 128), const, pipeline_mode=once),
            pl.BlockSpec((PAST_LEN, 128), const, pipeline_mode=once),
            pl.BlockSpec((_HB, DEC_SEQ, HEAD_PAD), new),
            pl.BlockSpec((_HB, DEC_SEQ, V_DIM), new),
        ],
        out_specs=pl.BlockSpec((DEC_SEQ, _HB * V_DIM), lambda b, h: (blk0 + b, h)),
        out_shape=jax.ShapeDtypeStruct((N_TOK, N_HEADS * V_DIM), BF16),
        scratch_shapes=[pltpu.VMEM((PAST_LEN, KV_LORA_RANK), BF16),
                        pltpu.VMEM((PAST_LEN, 128), F32),
                        pltpu.VMEM((PAST_LEN, 1), F32),
                        pltpu.VMEM((PAST_LEN, HEAD_PAD), BF16),
                        pltpu.VMEM((PAST_LEN, V_DIM), BF16)],
        input_output_aliases={0: 0},
        compiler_params=_cparams(2),
        name="flash_sample",
    )(attn, q, cache_kv, cache_kr_pad, w_kv, g_kn_nope, g_kn_rope, cos_t, sin_t, k_new, v_new)


def _merge_kernel(h_ref, c_ref, a_ref, wga_ref, wgb_ref, bga_ref, bgb_ref, wc_ref, wo_ref, o_ref):
    h = h_ref[...]
    ga = _sigmoid(_dot(h, wga_ref[...]) + bga_ref[...])
    gb = _sigmoid(_dot(h, wgb_ref[...]) + bgb_ref[...])
    mix = ga * _dot(c_ref[...], wc_ref[...]) + gb * _dot(a_ref[...], wo_ref[...])
    o_ref[...] = mix.astype(BF16)


def _merge(h, c_act, attn, w_gate, b_gate, w_conv_out, w_o):
    n = h.shape[0]
    tn = 512
    nj = D_MODEL // tn
    row = lambda i, j: (i, 0)
    return pl.pallas_call(
        _merge_kernel,
        grid=(n // TM, nj),
        in_specs=[
            pl.BlockSpec((TM, D_MODEL), row),
            pl.BlockSpec((TM, CONV_CHANNELS), row),
            pl.BlockSpec((TM, N_HEADS * V_DIM), row),
            pl.BlockSpec((D_MODEL, tn), lambda i, j: (0, j)),
            pl.BlockSpec((D_MODEL, tn), lambda i, j: (0, j + nj)),
            pl.BlockSpec((1, tn), lambda i, j: (0, j)),
            pl.BlockSpec((1, tn), lambda i, j: (0, j + nj)),
            pl.BlockSpec((CONV_CHANNELS, tn), lambda i, j: (0, j)),
            pl.BlockSpec((N_HEADS * V_DIM, tn), lambda i, j: (0, j)),
        ],
        out_specs=pl.BlockSpec((TM, tn), lambda i, j: (i, j)),
        out_shape=jax.ShapeDtypeStruct((n, D_MODEL), BF16),
        compiler_params=_cparams(2),
        name="merge",
    )(h, c_act, attn, w_gate, w_gate, b_gate, b_gate, w_conv_out, w_o)


def _split_bf16(x):
    hi = x.astype(BF16)
    lo = (x - hi.astype(F32)).astype(BF16)
    return hi, lo


_HALF = D_MODEL // 2


def _pack_bf16_pair(a, b):
    ua = lax.bitcast_convert_type(a.astype(BF16).astype(F32), U32)
    ub = lax.bitcast_convert_type(b.astype(BF16).astype(F32), U32)
    return lax.bitcast_convert_type(ua | (ub >> 16), F32)


def _unpack_bf16_pair(w):
    w = lax.bitcast_convert_type(w, U32)
    a = lax.bitcast_convert_type(w & jnp.uint32(0xFFFF0000), F32).astype(BF16)
    b = lax.bitcast_convert_type(w << 16, F32).astype(BF16)
    return a, b


def _out_router_kernel(n_tiles, n_prompt_tiles, mix_ref, xp_ref, xs_ref, w_ref, g_ref, wrh_ref, wrl_ref, br_ref,
                       x1_ref, hm_ref, idx_ref, gate_ref):
    i = pl.program_id(0)

    @pl.when(i < n_tiles)
    def _():
        x = _stacked_rows(i, n_prompt_tiles, xp_ref, xs_ref)
        _out_router_tile(mix_ref, x, w_ref, g_ref, wrh_ref, wrl_ref, br_ref,
                         x1_ref, hm_ref, idx_ref, gate_ref)

    @pl.when(i >= n_tiles)
    def _():
        hm_ref[...] = jnp.zeros(hm_ref.shape, F32)


def _out_router_tile(mix_ref, x, w_ref, g_ref, wrh_ref, wrl_ref, br_ref,
                     x1_ref, hm_ref, idx_ref, gate_ref):
    x1 = x + _dot(mix_ref[...], w_ref[...])
    x1_ref[...] = x1
    hn = x1 * lax.rsqrt(jnp.mean(x1 * x1, axis=-1, keepdims=True) + EPS) * g_ref[...]
    hm_ref[...] = _pack_bf16_pair(hn[:, :_HALF], hn[:, _HALF:])
    hh, hl = _split_bf16(hn)
    logits = _dot(hh, wrh_ref[...]) + (_dot(hh, wrl_ref[...]) + _dot(hl, wrh_ref[...])) + br_ref[...]
    lane = lax.broadcasted_iota(I32, logits.shape, 1).astype(F32)
    vals = []
    idx_out = jnp.zeros(logits.shape, F32)
    for k in range(TOP_K):
        m = jnp.max(logits, axis=-1, keepdims=True)
        sel = jnp.min(jnp.where(logits == m, lane, 1e9), axis=-1, keepdims=True)
        vals.append(m)
        idx_out = jnp.where(lane == float(k), sel, idx_out)
        logits = jnp.where(lane == sel, -jnp.inf, logits)
    exps = [jnp.exp(v - vals[0]) for v in vals]
    denom = exps[0] + exps[1] + exps[2] + exps[3]
    gate_out = jnp.zeros(idx_out.shape, F32)
    for k in range(TOP_K):
        gate_out = jnp.where(lane == float(k), exps[k] / denom, gate_out)
    idx_ref[...] = idx_out.astype(I32)
    gate_ref[...] = gate_out


def _out_router(mix, xp, xs, w_out, g_ffn, wr_hi, wr_lo, b_r):
    n = N_TOK
    tm = 256
    n_tiles = n // tm
    npt = N_P // tm
    const = lambda i: (0, 0)
    row = lambda i: (jnp.minimum(i, n_tiles - 1), 0)
    return pl.pallas_call(
        functools.partial(_out_router_kernel, n_tiles, npt),
        grid=(2 * n_tiles,),
        in_specs=[
            pl.BlockSpec((tm, D_MODEL), row),
            pl.BlockSpec((tm, D_MODEL), lambda i: (jnp.minimum(i, npt - 1), 0)),
            pl.BlockSpec((tm, D_MODEL), lambda i: (jnp.clip(i - npt, 0, N_S // tm - 1), 0)),
            pl.BlockSpec((D_MODEL, D_MODEL), const),
            pl.BlockSpec((1, D_MODEL), const),
            pl.BlockSpec((D_MODEL, 128), const),
            pl.BlockSpec((D_MODEL, 128), const),
            pl.BlockSpec((1, 128), const),
        ],
        out_specs=[
            pl.BlockSpec((tm, D_MODEL), row),
            pl.BlockSpec((tm, _HALF), lambda i: (i, 0)),
            pl.BlockSpec((tm, 128), row),
            pl.BlockSpec((tm, 128), row),
        ],
        out_shape=[
            jax.ShapeDtypeStruct((n, D_MODEL), F32),
            jax.ShapeDtypeStruct((2 * n, _HALF), F32),
            jax.ShapeDtypeStruct((n, 128), I32),
            jax.ShapeDtypeStruct((n, 128), F32),
        ],
        compiler_params=_cparams(1),
        name="out_router",
    )(mix, xp, xs, w_out, g_ffn, wr_hi, wr_lo, b_r)


_F_VALID, _F_FIRST, _F_NEXT, _F_GROUP0, _F_SLOT = 1, 2, 4, 8, 16


def _stream_weights(t, se_ref, sw_ref, ne_ref, nw_ref, fl_ref, copies, cast):
    flags = fl_ref[t]

    @pl.when((flags & _F_FIRST) != 0)
    def _():
        slot = (flags // _F_SLOT) & 1
        cur = copies(se_ref[t], sw_ref[t], slot)

        @pl.when((flags & _F_GROUP0) != 0)
        def _():
            for c in cur:
                c.start()

        for c in cur:
            c.wait()

        @pl.when((flags & _F_NEXT) != 0)
        def _():
            for c in copies(ne_ref[t], nw_ref[t], 1 - slot):
                c.start()

        cast(slot)


def _moe_up_kernel(se_ref, sw_ref, sn_ref, sb_ref, si_ref, ne_ref, nw_ref, fl_ref,
                   prev_ref, x_ref, w_hbm, bg_ref, bu_ref, o_ref, wbuf_ref, wgb_ref, wub_ref, sem_ref):
    del prev_ref
    t = pl.program_id(0)

    def copies(e, w, slot):
        col = pl.multiple_of(w * _UP_TN, _UP_TN)
        return (pltpu.make_async_copy(w_hbm.at[e, :, pl.ds(col, _UP_TN)], wbuf_ref.at[slot, 0], sem_ref.at[slot, 0]),
                pltpu.make_async_copy(w_hbm.at[e, :, pl.ds(col + D_FF, _UP_TN)], wbuf_ref.at[slot, 1],
                                      sem_ref.at[slot, 1]))

    def cast(slot):
        wgb_ref[...] = wbuf_ref[slot, 0].astype(BF16)
        wub_ref[...] = wbuf_ref[slot, 1].astype(BF16)

    _stream_weights(t, se_ref, sw_ref, ne_ref, nw_ref, fl_ref, copies, cast)
    valid = (fl_ref[t] & _F_VALID) != 0

    @pl.when(valid)
    def _():
        xa, xb = _unpack_bf16_pair(x_ref[...])
        g = _dot(xa, wgb_ref[:_HALF, :]) + _dot(xb, wgb_ref[_HALF:, :]) + bg_ref[0]
        u = _dot(xa, wub_ref[:_HALF, :]) + _dot(xb, wub_ref[_HALF:, :]) + bu_ref[0]
        g = jnp.minimum(g, SWIGLU_LIMIT)
        u = jnp.clip(u, -SWIGLU_LIMIT, SWIGLU_LIMIT)
        o_ref[...] = ((u + 1.0) * (g * _sigmoid(SWIGLU_ALPHA * g))).astype(BF16)

    @pl.when(jnp.logical_not(valid))
    def _():
        o_ref[...] = jnp.zeros(o_ref.shape, BF16)


_UP_TN = 512
_UP_TILES = D_FF // _UP_TN
_DN_TN = 1024
_DN_TILES = D_MODEL // _DN_TN
MOE_CHUNKS = 4
_CHUNK_BLKS = MOE_MAX_BLKS // MOE_CHUNKS


def _moe_up(plan, act_prev, xs, w_gu, b_gu, chunk):
    steps = plan[0].shape[0]
    blk0 = chunk * _CHUNK_BLKS
    bspec = lambda off: pl.BlockSpec((1, 1, _UP_TN), lambda t, se, sw, sn, sb, si, ne, nw, fl: (se[t], 0, sw[t] + off))
    aliases = {} if act_prev is None else {len(plan): 0}
    prev = jnp.zeros((8, 128), BF16) if act_prev is None else act_prev
    return pl.pallas_call(
        _moe_up_kernel,
        grid_spec=pltpu.PrefetchScalarGridSpec(
            num_scalar_prefetch=len(plan),
            grid=(steps,),
            in_specs=[
                pl.BlockSpec(memory_space=pl.ANY),
                pl.BlockSpec((MOE_BLK, _HALF), lambda t, se, sw, sn, sb, si, ne, nw, fl: (si[t], 0)),
                pl.BlockSpec(memory_space=pl.ANY),
                bspec(0), bspec(_UP_TILES),
            ],
            out_specs=pl.BlockSpec((MOE_BLK, _UP_TN),
                                   lambda t, se, sw, sn, sb, si, ne, nw, fl: (blk0 + sb[t], sn[t])),
            scratch_shapes=[pltpu.VMEM((2, 2, D_MODEL, _UP_TN), F32),
                            pltpu.VMEM((D_MODEL, _UP_TN), BF16), pltpu.VMEM((D_MODEL, _UP_TN), BF16),
                            pltpu.SemaphoreType.DMA((2, 2))],
        ),
        out_shape=jax.ShapeDtypeStruct((MOE_ROWS, D_FF), BF16),
        input_output_aliases=aliases,
        compiler_params=_cparams(1),
        name=f"moe_up_{chunk}",
    )(*plan, prev, xs, w_gu, b_gu, b_gu)


_DN_HALF = _DN_TN // 2


def _moe_down_kernel(se_ref, sw_ref, sn_ref, sb_ref, si_ref, ne_ref, nw_ref, fl_ref,
                     a_ref, w_hbm, b_ref, o_ref, wbuf_ref, wb_ref, sem_ref):
    t = pl.program_id(0)

    def copies(e, w, slot):
        col = pl.multiple_of(w * _DN_TN, _DN_TN)
        return (pltpu.make_async_copy(w_hbm.at[e, :, pl.ds(col, _DN_TN)], wbuf_ref.at[slot], sem_ref.at[slot]),)

    def cast(slot):
        wb_ref[...] = wbuf_ref[slot].astype(BF16)

    _stream_weights(t, se_ref, sw_ref, ne_ref, nw_ref, fl_ref, copies, cast)
    valid = (fl_ref[t] & _F_VALID) != 0

    @pl.when(valid)
    def _():
        y = _dot(a_ref[...], wb_ref[...]) + b_ref[0]
        o_ref[...] = _pack_bf16_pair(y[:, :_DN_HALF], y[:, _DN_HALF:])

    @pl.when(jnp.logical_not(valid))
    def _():
        o_ref[...] = jnp.zeros(o_ref.shape, F32)


def _moe_down(plan, act, w_dn, b_dn):
    steps = plan[0].shape[0]
    return pl.pallas_call(
        _moe_down_kernel,
        grid_spec=pltpu.PrefetchScalarGridSpec(
            num_scalar_prefetch=len(plan),
            grid=(steps,),
            in_specs=[
                pl.BlockSpec((MOE_BLK, D_FF), lambda t, se, sw, sn, sb, si, ne, nw, fl: (si[t], 0)),
                pl.BlockSpec(memory_space=pl.ANY),
                pl.BlockSpec((1, 1, _DN_TN), lambda t, se, sw, sn, sb, si, ne, nw, fl: (se[t], 0, sw[t])),
            ],
            out_specs=pl.BlockSpec((MOE_BLK, _DN_HALF), lambda t, se, sw, sn, sb, si, ne, nw, fl: (sb[t], sn[t])),
            scratch_shapes=[pltpu.VMEM((2, D_FF, _DN_TN), F32), pltpu.VMEM((D_FF, _DN_TN), BF16),
                            pltpu.SemaphoreType.DMA((2,))],
        ),
        out_shape=jax.ShapeDtypeStruct((MOE_ROWS, _HALF), F32),
        compiler_params=_cparams(1),
        name="moe_down",
    )(*plan, act, w_dn, b_dn)


def _moe_dispatch(top_idx):
    n_asg = N_TOK * TOP_K
    flat_e = top_idx.reshape(-1)
    onehot = (flat_e[:, None] == jnp.arange(N_EXPERTS, dtype=I32)[None, :]).astype(I32)
    csum = jnp.cumsum(onehot, axis=0)
    counts = csum[-1]
    rank = jnp.sum(csum * onehot, axis=1) - 1
    nblk = (counts + MOE_BLK - 1) // MOE_BLK
    blk_start = jnp.cumsum(nblk) - nblk
    dest = jnp.sum(onehot * blk_start[None, :], axis=1) * MOE_BLK + rank
    pad_src = jnp.arange(MOE_ROWS, dtype=I32) % N_TOK
    row_tok = pad_src.at[dest].set(jnp.arange(n_asg, dtype=I32) // TOP_K,
                                   mode="promise_in_bounds", unique_indices=True)
    return dest, row_tok, nblk, blk_start


def _moe_steps(nblk, blk_start, n_tiles, blk_lo, n_blks):
    t_max = n_tiles * n_blks
    lo = jnp.clip(blk_start, blk_lo, blk_lo + n_blks)
    hi = jnp.clip(blk_start + nblk, blk_lo, blk_lo + n_blks)
    nb_e = hi - lo
    per_e = nb_e * n_tiles
    s_end = jnp.cumsum(per_e)
    total = s_end[-1]
    t = jnp.arange(t_max, dtype=I32)
    tc = jnp.clip(t, 0, jnp.maximum(total - 1, 0))
    e = jnp.minimum(jnp.sum((s_end[None, :] <= tc[:, None]).astype(I32), axis=1), N_EXPERTS - 1)
    sel = (e[:, None] == jnp.arange(N_EXPERTS, dtype=I32)[None, :]).astype(I32)
    pick = lambda v: jnp.sum(sel * v[None, :], axis=1)
    local = tc - pick(s_end - per_e)
    nb = jnp.maximum(pick(nb_e), 1)
    w_tile = jnp.clip(local // nb, 0, n_tiles - 1)
    r = local % nb
    valid = t < total
    first = jnp.logical_and(valid, r == 0)
    fill = t - total
    blk = jnp.where(valid, pick(lo) - blk_lo + r, total // n_tiles + fill // n_tiles)
    o_tile = jnp.where(valid, w_tile, fill % n_tiles)
    blk = jnp.clip(blk, 0, n_blks - 1)
    blk_in = jnp.where(valid, blk, jnp.maximum(total // n_tiles - 1, 0))
    ids = jnp.arange(N_EXPERTS, dtype=I32)
    owners = jnp.where(nb_e > 0, ids, N_EXPERTS)
    later = jnp.flip(lax.cummin(jnp.flip(owners)))
    next_owner = pick(jnp.concatenate([later[1:], jnp.full((1,), N_EXPERTS, I32)]))
    last_tile = w_tile == n_tiles - 1
    next_e = jnp.where(last_tile, next_owner, e)
    next_w = jnp.where(last_tile, 0, w_tile + 1)
    has_next = jnp.logical_and(first, next_e < N_EXPERTS)
    group = jnp.cumsum(first.astype(I32)) - 1
    flags = (valid * _F_VALID + first * _F_FIRST + has_next * _F_NEXT
             + jnp.logical_and(first, group == 0) * _F_GROUP0 + (group % 2) * _F_SLOT)
    return e, w_tile, o_tile, blk, blk_in, jnp.minimum(next_e, N_EXPERTS - 1), next_w, flags.astype(I32)


_FIN_TM = 256
_FIN_TN = 512
FIN_CHUNKS = 4


def _unpack_expert_rows(words):
    u = lax.bitcast_convert_type(words, U32)
    hi = lax.bitcast_convert_type(u & jnp.uint32(0xFFFF0000), F32)
    lo = lax.bitcast_convert_type(u << 16, F32)
    parts = []
    for n in range(_DN_TILES):
        cols = slice(n * _DN_HALF, (n + 1) * _DN_HALF)
        parts += [hi[:, cols], lo[:, cols]]
    return jnp.concatenate(parts, axis=1)


def _final_kernel(prev_ref, x1_ref, y0_ref, y1_ref, y2_ref, y3_ref, gate_ref, g_ref, wg_ref, p_ref, wp_ref,
                  o_ref, x2_ref):
    del prev_ref
    gate = gate_ref[...]
    moe = (_unpack_expert_rows(y0_ref[0]) * gate[:, 0:1] + _unpack_expert_rows(y1_ref[0]) * gate[:, 1:2]
           + _unpack_expert_rows(y2_ref[0]) * gate[:, 2:3] + _unpack_expert_rows(y3_ref[0]) * gate[:, 3:4])
    x2 = x1_ref[...] + moe
    x2_ref[...] = x2
    hp = (x2 * lax.rsqrt(jnp.mean(x2 * x2, axis=-1, keepdims=True) + EPS) * g_ref[...]).astype(BF16)
    pb = p_ref[...].astype(BF16)
    for c in range(0, D_MODEL, _FIN_TN):
        cols = slice(c, c + _FIN_TN)
        emb = _dot(pb, wp_ref[:, cols])
        o_ref[:, cols] = x2_ref[:, cols] + _sigmoid(_dot(hp, wg_ref[:, cols])) * emb


def _final(out_prev, x1, y4, gate, g_ple, w_ple_gate, p, w_ple, tok0, out0, n, n_out, name):
    t0 = tok0 // _FIN_TM
    o0 = out0 // _FIN_TM
    pt0 = out0 // _FIN_TM
    const = lambda i: (0, 0)
    yspec = lambda k: pl.BlockSpec((1, _FIN_TM, _HALF), lambda i: (k, i, 0))
    once = pl.Buffered(1)
    aliases = {} if out_prev is None else {0: 0}
    prev = jnp.zeros((8, 128), F32) if out_prev is None else out_prev
    return pl.pallas_call(
        _final_kernel,
        grid=(n // _FIN_TM,),
        in_specs=[
            pl.BlockSpec(memory_space=pl.ANY),
            pl.BlockSpec((_FIN_TM, D_MODEL), lambda i: (t0 + i, 0)),
            yspec(0), yspec(1), yspec(2), yspec(3),
            pl.BlockSpec((_FIN_TM, 128), lambda i: (t0 + i, 0)),
            pl.BlockSpec((1, D_MODEL), const),
            pl.BlockSpec((D_MODEL, D_MODEL), const, pipeline_mode=once),
            pl.BlockSpec((_FIN_TM, PLE_DIM), lambda i: (pt0 + i, 0)),
            pl.BlockSpec((PLE_DIM, D_MODEL), const, pipeline_mode=once),
        ],
        out_specs=pl.BlockSpec((_FIN_TM, D_MODEL), lambda i: (o0 + i, 0)),
        out_shape=jax.ShapeDtypeStruct((n_out, D_MODEL), F32),
        scratch_shapes=[pltpu.VMEM((_FIN_TM, D_MODEL), F32)],
        input_output_aliases=aliases,
        compiler_params=_cparams(1),
        name=name,
    )(prev, x1, y4, y4, y4, y4, gate, g_ple, w_ple_gate, p, w_ple)


def _rope_layout(x):
    half = ROPE_DIM // 2
    z = jnp.zeros(x.shape[:-1] + (half,), x.dtype)
    return jnp.concatenate([x[..., :half], z, x[..., half:], z], axis=-1)


def _rope_tables():
    half = ROPE_DIM // 2
    inv_freq = ROPE_THETA ** (-jnp.arange(half, dtype=F32) / half)
    pos = jnp.arange(PAST_LEN + DEC_SEQ, dtype=I32)
    ang = pos.astype(F32)[:, None] * inv_freq[None, :]
    cos, sin = jnp.cos(ang), jnp.sin(ang)
    z = jnp.zeros_like(cos)
    c = jnp.concatenate([cos, z, cos, z], axis=-1)
    s = jnp.concatenate([-sin, z, sin, z], axis=-1)
    rep = ATT_TM // DEC_SEQ
    return (jnp.concatenate([c[:SEQ], jnp.tile(c[PAST_LEN:], (rep, 1))], axis=0),
            jnp.concatenate([s[:SEQ], jnp.tile(s[PAST_LEN:], (rep, 1))], axis=0))


def _layer(xp, xs, p_prompt, p_sample, cache_kv, cache_kr, state_conv,
           g_mix, w_in, b_gate, w_dw, b_dw, g_cn, b_cn, w_conv_out,
           g_qa, g_kva, w_qb, w_kb, w_vb, g_qn, g_kn, w_o, w_out,
           g_ffn, w_router, b_router, w_gu, b_gu, w_dn, b_dn,
           g_ple, w_ple_gate, w_ple):
    assert SEQ == PAST_LEN
    row = lambda v: v.reshape(1, -1)
    w_in_b = w_in.astype(BF16)
    w_mid = jnp.concatenate([w_in_b[:, O_U:O_KV], _rope_layout(w_in_b[:, O_KV:O_KR])], axis=1)
    w_gate = w_in_b[:, O_KR:]

    h, q_lat, kv_new, kr_pad = _in_mid(xp, xs, row(g_mix), w_mid, row(g_qa), row(g_kva))
    half = ROPE_DIM // 2
    kr_new = jnp.concatenate([kr_pad[:, :half], kr_pad[:, 2 * half:3 * half]], axis=1)
    glu = _in_glu(h, w_in_b)

    hist = jnp.concatenate([jnp.zeros((BATCH, HALO, CONV_CHANNELS), F32),
                            jnp.pad(state_conv, ((0, 0), (HALO - (CONV_WIDTH - 1), 0), (0, 0)))], axis=0)
    c_act = _conv_module(glu, hist, w_dw, row(b_dw), row(g_cn), row(b_cn))

    cos_t, sin_t = _rope_tables()
    w_q = jnp.concatenate([w_qb[..., :NOPE_DIM], _rope_layout(w_qb[..., NOPE_DIM:])], axis=-1)
    w_q = w_q.reshape(Q_LORA_RANK, N_HEADS * HEAD_PAD).astype(BF16)
    g_q = jnp.concatenate([g_qn[:NOPE_DIM], _rope_layout(g_qn[NOPE_DIM:])]).reshape(1, HEAD_PAD)
    q = _q_heads(q_lat, w_q, g_q, cos_t, sin_t)

    w_kv = jnp.concatenate([w_kb, w_vb], axis=-1).reshape(KV_LORA_RANK, N_HEADS * HEAD_PAD).astype(BF16)
    g_kn_nope = g_kn[:NOPE_DIM].reshape(1, NOPE_DIM)
    g_kn_rope = _rope_layout(g_kn[NOPE_DIM:]).reshape(1, 128)
    k_new, v_new = _kv_heads(kv_new, kr_pad, w_kv, g_kn_nope, g_kn_rope, cos_t, sin_t,
                             _tab_idx_new, "kv_heads_new")
    attn = _flash_prompt(q, k_new, v_new)
    attn = _flash_sample(attn, q, cache_kv.reshape(DEC_BATCH * PAST_LEN, KV_LORA_RANK),
                         _rope_layout(cache_kr).reshape(DEC_BATCH * PAST_LEN, 128),
                         w_kv, g_kn_nope, g_kn_rope, cos_t, sin_t, k_new, v_new)

    mix = _merge(h, c_act, attn, w_gate, row(b_gate), w_conv_out.astype(BF16), w_o.astype(BF16))

    wr = jnp.pad(w_router, ((0, 0), (0, 128 - N_EXPERTS)))
    wr_hi, wr_lo = _split_bf16(wr)
    b_r = jnp.concatenate([b_router, jnp.full((128 - N_EXPERTS,), -jnp.inf, F32)]).reshape(1, 128)
    x1, hm, idx_pad, gate_pad = _out_router(mix, xp, xs, w_out.astype(BF16), row(g_ffn), wr_hi, wr_lo, b_r)

    top_idx = idx_pad[:, :TOP_K]
    dest, row_tok, nblk, blk_start = _moe_dispatch(top_idx)
    b_gu3 = b_gu.reshape(N_EXPERTS, 1, 2 * D_FF)
    chunk_rows = _CHUNK_BLKS * MOE_BLK
    act = None
    for c in range(MOE_CHUNKS):
        xs = hm.at[row_tok[c * chunk_rows:(c + 1) * chunk_rows]].get(mode="promise_in_bounds")
        plan = _moe_steps(nblk, blk_start, _UP_TILES, c * _CHUNK_BLKS, _CHUNK_BLKS)
        act = _moe_up(plan, act, xs, w_gu, b_gu3, c)
    ys = _moe_down(_moe_steps(nblk, blk_start, _DN_TILES, 0, MOE_MAX_BLKS), act, w_dn,
                   b_dn.reshape(N_EXPERTS, 1, D_MODEL))

    dest_t = dest.reshape(N_TOK, TOP_K).T
    fin = (row(g_ple), w_ple_gate.astype(BF16))
    w_ple_b = w_ple.astype(BF16)
    n_c = N_P // FIN_CHUNKS
    out_p = None
    for c in range(FIN_CHUNKS):
        y4 = ys.at[dest_t[:, c * n_c:(c + 1) * n_c]].get(mode="promise_in_bounds")
        out_p = _final(out_p, x1, y4, gate_pad, *fin, p_prompt, w_ple_b, c * n_c, c * n_c, n_c, N_P,
                       f"final_prompt_{c}")
    y4 = ys.at[dest_t[:, N_P:]].get(mode="promise_in_bounds")
    out_s = _final(None, x1, y4, gate_pad, *fin, p_sample, w_ple_b, N_P, 0, N_S, N_S, "final_sample")
    return out_p, out_s, kv_new, kr_new, glu


def kernel(x_prompt, x_sample, cache_kv_latent, cache_k_rope, state_conv, p_prompt, p_sample, g_mix, w_in, b_gate, w_dw, b_dw, g_cn, b_cn, w_conv_out, g_qa, g_kva, w_qb, w_kb, w_vb, g_qn, g_kn, w_o, w_out, g_ffn, w_router, b_router, w_gu, b_gu, w_dn, b_dn, g_ple, w_ple_gate, w_ple):
    assert g_mix.shape[0] == 1
    out_p, out_s, kv_new, kr_new, glu = _layer(
        x_prompt.reshape(N_P, D_MODEL), x_sample.reshape(N_S, D_MODEL),
        p_prompt[0].reshape(N_P, PLE_DIM), p_sample[0].reshape(N_S, PLE_DIM),
        cache_kv_latent[0], cache_k_rope[0], state_conv[0],
        g_mix[0], w_in[0], b_gate[0], w_dw[0], b_dw[0], g_cn[0], b_cn[0], w_conv_out[0],
        g_qa[0], g_kva[0], w_qb[0], w_kb[0], w_vb[0], g_qn[0], g_kn[0], w_o[0], w_out[0],
        g_ffn[0], w_router[0], b_router[0], w_gu[0], b_gu[0], w_dn[0], b_dn[0],
        g_ple[0], w_ple_gate[0], w_ple[0])
    tail = CONV_WIDTH - 1
    conv_p = jnp.stack([glu[(b + 1) * SEQ - tail:(b + 1) * SEQ] for b in range(BATCH)])
    conv_s = glu[N_P:].reshape(DEC_BATCH, DEC_SEQ, CONV_CHANNELS)[:, DEC_SEQ - tail:]
    return (out_p.reshape(BATCH, SEQ, D_MODEL),
            out_s.reshape(DEC_BATCH, DEC_SEQ, D_MODEL),
            kv_new[:N_P].reshape(1, BATCH, SEQ, KV_LORA_RANK),
            kr_new[:N_P].reshape(1, BATCH, SEQ, ROPE_DIM),
            conv_p[None],
            kv_new[N_P:].reshape(1, DEC_BATCH, DEC_SEQ, KV_LORA_RANK),
            kr_new[N_P:].reshape(1, DEC_BATCH, DEC_SEQ, ROPE_DIM),
            conv_s[None])
```

```python
import functools
import math

import jax
import jax.numpy as jnp
from jax import lax
from jax.experimental import pallas as pl
from jax.experimental.pallas import tpu as pltpu

F32 = jnp.float32
BF16 = jnp.bfloat16
I32 = jnp.int32
U32 = jnp.uint32

D_MODEL = 2048
BATCH = 2
SEQ = 4096
DEC_BATCH = 8
DEC_SEQ = 64
PAST_LEN = 4096
CHUNK = 64
CONV_CHANNELS = D_MODEL
CONV_WIDTH = 31
N_HEADS = 16
Q_LORA_RANK = 512
KV_LORA_RANK = 512
NOPE_DIM = 128
ROPE_DIM = 64
QK_DIM = NOPE_DIM + ROPE_DIM
V_DIM = 128
ROPE_THETA = 10000.0
N_EXPERTS = 32
TOP_K = 4
D_FF = D_MODEL
SWIGLU_ALPHA = 1.702
SWIGLU_LIMIT = 7.0
PLE_DIM = 256
EPS = 1e-6
NEG_INF = -1e30

N_P = BATCH * SEQ
N_S = DEC_BATCH * DEC_SEQ
N_TOK = N_P + N_S
O_U = 2 * CONV_CHANNELS
O_Q = O_U + Q_LORA_RANK
O_KV = O_Q + KV_LORA_RANK
O_KR = O_KV + ROPE_DIM
MID_W = 1152
HEAD_PAD = 256

TM = 512
CONV_T = 64
HALO = 32
MOE_BLK = 512
MOE_MAX_BLKS = (N_TOK * TOP_K) // MOE_BLK + N_EXPERTS
MOE_ROWS = MOE_MAX_BLKS * MOE_BLK
VMEM_LIMIT = 48 * 1024 * 1024


def _cparams(n_axes):
    return pltpu.CompilerParams(dimension_semantics=("arbitrary",) * n_axes,
                                vmem_limit_bytes=VMEM_LIMIT)


def _sigmoid(x):
    return 1.0 / (1.0 + jnp.exp(-x))


def _dot(a, b):
    return jnp.dot(a, b, preferred_element_type=F32)


def _stacked_rows(i, n_prompt_tiles, xp_ref, xs_ref):
    return jnp.where(i < n_prompt_tiles, xp_ref[...], xs_ref[...])


def _in_mid_kernel(xp_ref, xs_ref, g_ref, w_ref, gqa_ref, gkva_ref, h_ref, q_ref, kv_ref, kr_ref):
    x = _stacked_rows(pl.program_id(0), N_P // TM, xp_ref, xs_ref)
    h = x * lax.rsqrt(jnp.mean(x * x, axis=-1, keepdims=True) + EPS) * g_ref[...]
    hb = h.astype(BF16)
    h_ref[...] = hb
    z = _dot(hb, w_ref[...])
    ql = z[:, :Q_LORA_RANK]
    kvl = z[:, Q_LORA_RANK:Q_LORA_RANK + KV_LORA_RANK]
    qn = ql * lax.rsqrt(jnp.mean(ql * ql, axis=-1, keepdims=True) + EPS) * gqa_ref[...]
    q_ref[...] = qn.astype(BF16)
    kv_ref[...] = kvl * lax.rsqrt(jnp.mean(kvl * kvl, axis=-1, keepdims=True) + EPS) * gkva_ref[...]
    kr_ref[...] = z[:, Q_LORA_RANK + KV_LORA_RANK:]


def _in_mid(xp, xs, g_mix, w_mid, g_qa, g_kva):
    n = N_TOK
    npt = N_P // TM
    return pl.pallas_call(
        _in_mid_kernel,
        grid=(n // TM,),
        in_specs=[
            pl.BlockSpec((TM, D_MODEL), lambda i: (jnp.minimum(i, npt - 1), 0)),
            pl.BlockSpec((TM, D_MODEL), lambda i: (jnp.maximum(i - npt, 0), 0)),
            pl.BlockSpec((1, D_MODEL), lambda i: (0, 0)),
            pl.BlockSpec((D_MODEL, MID_W), lambda i: (0, 0)),
            pl.BlockSpec((1, Q_LORA_RANK), lambda i: (0, 0)),
            pl.BlockSpec((1, KV_LORA_RANK), lambda i: (0, 0)),
        ],
        out_specs=[
            pl.BlockSpec((TM, D_MODEL), lambda i: (i, 0)),
            pl.BlockSpec((TM, Q_LORA_RANK), lambda i: (i, 0)),
            pl.BlockSpec((TM, KV_LORA_RANK), lambda i: (i, 0)),
            pl.BlockSpec((TM, 128), lambda i: (i, 0)),
        ],
        out_shape=[
            jax.ShapeDtypeStruct((n, D_MODEL), BF16),
            jax.ShapeDtypeStruct((n, Q_LORA_RANK), BF16),
            jax.ShapeDtypeStruct((n, KV_LORA_RANK), F32),
            jax.ShapeDtypeStruct((n, 128), F32),
        ],
        compiler_params=_cparams(1),
        name="in_mid",
    )(xp, xs, g_mix, w_mid, g_qa, g_kva)


def _glu_kernel(h_ref, w1_ref, w2_ref, o_ref):
    h = h_ref[...]
    o_ref[...] = _dot(h, w1_ref[...]) * _sigmoid(_dot(h, w2_ref[...]))


def _in_glu(h, w_in_b):
    n = h.shape[0]
    tn = 512
    nj = CONV_CHANNELS // tn
    return pl.pallas_call(
        _glu_kernel,
        grid=(n // TM, nj),
        in_specs=[
            pl.BlockSpec((TM, D_MODEL), lambda i, j: (i, 0)),
            pl.BlockSpec((D_MODEL, tn), lambda i, j: (0, j)),
            pl.BlockSpec((D_MODEL, tn), lambda i, j: (0, j + nj)),
        ],
        out_specs=pl.BlockSpec((TM, tn), lambda i, j: (i, j)),
        out_shape=jax.ShapeDtypeStruct((n, CONV_CHANNELS), F32),
        compiler_params=_cparams(2),
        name="in_glu",
    )(h, w_in_b, w_in_b)


_CONV_TILES_PER_SEQ = SEQ // CONV_T
_CONV_PROMPT_TILES = N_P // CONV_T
_CONV_LANES = 512
SUBLANES = 8
_SHIFT_ROWS = (HALO // SUBLANES - 1) * SUBLANES + CONV_T


def _conv_kernel(cur_ref, prev_ref, hist_ref, w_ref, bdw_ref, g_ref, b_ref, o_ref, win_ref, conv_ref, shift_ref):
    i = pl.program_id(0)
    first = jnp.logical_or(i >= _CONV_PROMPT_TILES, i % _CONV_TILES_PER_SEQ == 0)

    @pl.when(first)
    def _():
        win_ref[0:HALO, :] = hist_ref[0]

    @pl.when(jnp.logical_not(first))
    def _():
        win_ref[0:HALO, :] = prev_ref[...]

    win_ref[HALO:HALO + CONV_T, :] = cur_ref[...]
    for r in range(1, SUBLANES):
        shift_ref[r - 1] = win_ref[r:r + _SHIFT_ROWS, :]
    base = HALO - (CONV_WIDTH - 1)
    for c in range(0, CONV_CHANNELS, _CONV_LANES):
        acc = jnp.zeros((CONV_T, _CONV_LANES), F32)
        for k in range(CONV_WIDTH):
            q, r = divmod(base + k, SUBLANES)
            lanes = slice(c, c + _CONV_LANES)
            rows = slice(q * SUBLANES, q * SUBLANES + CONV_T)
            src = win_ref[rows, lanes] if r == 0 else shift_ref[r - 1, rows, lanes]
            acc = acc + w_ref[k:k + 1, lanes] * src
        conv_ref[:, c:c + _CONV_LANES] = acc + bdw_ref[:, c:c + _CONV_LANES]
    y = conv_ref[...]
    yc = y - jnp.mean(y, axis=-1, keepdims=True)
    var = jnp.mean(yc * yc, axis=-1, keepdims=True)
    z = yc * lax.rsqrt(var + EPS) * g_ref[...] + b_ref[...]
    o_ref[...] = (z * _sigmoid(z)).astype(BF16)


def _conv_module(glu, hist, w_dw, b_dw, g_cn, b_cn):
    n = glu.shape[0]
    n_tiles = n // CONV_T
    halo_per_tile = CONV_T // HALO

    def seq_of(i):
        return jnp.where(i < _CONV_PROMPT_TILES, i // _CONV_TILES_PER_SEQ, i - _CONV_PROMPT_TILES + BATCH)

    return pl.pallas_call(
        _conv_kernel,
        grid=(n_tiles,),
        in_specs=[
            pl.BlockSpec((CONV_T, CONV_CHANNELS), lambda i: (i, 0)),
            pl.BlockSpec((HALO, CONV_CHANNELS), lambda i: (jnp.maximum(i * halo_per_tile - 1, 0), 0)),
            pl.BlockSpec((1, HALO, CONV_CHANNELS), lambda i: (seq_of(i), 0, 0)),
            pl.BlockSpec((CONV_WIDTH, CONV_CHANNELS), lambda i: (0, 0)),
            pl.BlockSpec((1, CONV_CHANNELS), lambda i: (0, 0)),
            pl.BlockSpec((1, CONV_CHANNELS), lambda i: (0, 0)),
            pl.BlockSpec((1, CONV_CHANNELS), lambda i: (0, 0)),
        ],
        out_specs=pl.BlockSpec((CONV_T, CONV_CHANNELS), lambda i: (i, 0)),
        out_shape=jax.ShapeDtypeStruct((n, CONV_CHANNELS), BF16),
        scratch_shapes=[pltpu.VMEM((HALO + CONV_T, CONV_CHANNELS), F32),
                        pltpu.VMEM((CONV_T, CONV_CHANNELS), F32),
                        pltpu.VMEM((SUBLANES - 1, _SHIFT_ROWS, CONV_CHANNELS), F32)],
        compiler_params=_cparams(1),
        name="conv_module",
    )(glu, glu, hist, w_dw, b_dw, g_cn, b_cn)


ATT_TM = 256
_TAB_PROMPT_TILES = N_P // ATT_TM
_TAB_SEQ_TILES = SEQ // ATT_TM
_TAB_ROWS = SEQ + ATT_TM


def _tab_idx_new(i):
    return jnp.where(i < _TAB_PROMPT_TILES, i % _TAB_SEQ_TILES, _TAB_SEQ_TILES)


def _rope_pair(u, c, s):
    return u * c + pltpu.roll(u, 64, 1) * s


_Q_SCALE = math.log2(math.e) / math.sqrt(QK_DIM)


def _q_heads_kernel(ql_ref, w_ref, g_ref, c_ref, s_ref, o_ref):
    ql = ql_ref[...]
    g = g_ref[...]
    c = c_ref[...]
    s = s_ref[...]
    for h in range(N_HEADS):
        qf = _dot(ql, w_ref[:, h * HEAD_PAD:(h + 1) * HEAD_PAD])
        ssq = jnp.sum(qf * qf, axis=-1, keepdims=True)
        qn = qf * (lax.rsqrt(ssq * (1.0 / QK_DIM) + EPS) * _Q_SCALE) * g
        o_ref[h, :, :NOPE_DIM] = qn[:, :NOPE_DIM].astype(BF16)
        o_ref[h, :, NOPE_DIM:] = _rope_pair(qn[:, NOPE_DIM:], c, s).astype(BF16)


def _q_heads(q_lat, w_q, g_q, cos_t, sin_t):
    n = q_lat.shape[0]
    return pl.pallas_call(
        _q_heads_kernel,
        grid=(n // ATT_TM,),
        in_specs=[
            pl.BlockSpec((ATT_TM, Q_LORA_RANK), lambda i: (i, 0)),
            pl.BlockSpec((Q_LORA_RANK, N_HEADS * HEAD_PAD), lambda i: (0, 0)),
            pl.BlockSpec((1, HEAD_PAD), lambda i: (0, 0)),
            pl.BlockSpec((ATT_TM, 128), lambda i: (_tab_idx_new(i), 0)),
            pl.BlockSpec((ATT_TM, 128), lambda i: (_tab_idx_new(i), 0)),
        ],
        out_specs=pl.BlockSpec((N_HEADS, ATT_TM, HEAD_PAD), lambda i: (0, i, 0)),
        out_shape=jax.ShapeDtypeStruct((N_HEADS, n, HEAD_PAD), BF16),
        compiler_params=_cparams(1),
        name="q_heads",
    )(q_lat, w_q, g_q, cos_t, sin_t)


def _kv_heads_kernel(kv_ref, kr_ref, w_ref, gn_ref, gr_ref, c_ref, s_ref, k_ref, v_ref):
    kv = kv_ref[...].astype(BF16)
    u = kr_ref[...]
    ssq_r = jnp.sum(u * u, axis=-1, keepdims=True)
    krot = _rope_pair(u * gr_ref[...], c_ref[...], s_ref[...])
    gn = gn_ref[...]
    for h in range(N_HEADS):
        z = _dot(kv, w_ref[:, h * HEAD_PAD:(h + 1) * HEAD_PAD])
        kn = z[:, :NOPE_DIM]
        ssq = jnp.sum(kn * kn, axis=-1, keepdims=True) + ssq_r
        scale = lax.rsqrt(ssq * (1.0 / QK_DIM) + EPS)
        k_ref[h, :, :NOPE_DIM] = (kn * scale * gn).astype(BF16)
        k_ref[h, :, NOPE_DIM:] = (krot * scale).astype(BF16)
        v_ref[h] = z[:, NOPE_DIM:].astype(BF16)


def _kv_heads(kv_lat, kr_pad, w_kv, g_kn_nope, g_kn_rope, cos_t, sin_t, tab_idx, name):
    n = kv_lat.shape[0]
    return pl.pallas_call(
        _kv_heads_kernel,
        grid=(n // ATT_TM,),
        in_specs=[
            pl.BlockSpec((ATT_TM, KV_LORA_RANK), lambda i: (i, 0)),
            pl.BlockSpec((ATT_TM, 128), lambda i: (i, 0)),
            pl.BlockSpec((KV_LORA_RANK, N_HEADS * HEAD_PAD), lambda i: (0, 0)),
            pl.BlockSpec((1, NOPE_DIM), lambda i: (0, 0)),
            pl.BlockSpec((1, 128), lambda i: (0, 0)),
            pl.BlockSpec((ATT_TM, 128), lambda i: (tab_idx(i), 0)),
            pl.BlockSpec((ATT_TM, 128), lambda i: (tab_idx(i), 0)),
        ],
        out_specs=[
            pl.BlockSpec((N_HEADS, ATT_TM, HEAD_PAD), lambda i: (0, i, 0)),
            pl.BlockSpec((N_HEADS, ATT_TM, V_DIM), lambda i: (0, i, 0)),
        ],
        out_shape=[
            jax.ShapeDtypeStruct((N_HEADS, n, HEAD_PAD), BF16),
            jax.ShapeDtypeStruct((N_HEADS, n, V_DIM), BF16),
        ],
        compiler_params=_cparams(1),
        name=name,
    )(kv_lat, kr_pad, w_kv, g_kn_nope, g_kn_rope, cos_t, sin_t)


_TQ = 512
_TKB = 512
_HB = 2
_HBP = 4


def _flash_prompt_kernel(q_ref, k_ref, v_ref, o_ref, m_ref, l_ref, acc_ref):
    qi = pl.program_id(2)
    m_ref[...] = jnp.full(m_ref.shape, NEG_INF, F32)
    l_ref[...] = jnp.zeros(l_ref.shape, F32)
    acc_ref[...] = jnp.zeros(acc_ref.shape, F32)
    nlb = _TKB // 128

    def step(ki, masked):
        start = pl.multiple_of(ki * _TKB, _TKB)
        for hh in range(_HBP):
            k = k_ref[hh, pl.ds(start, _TKB), :]
            v = v_ref[hh, pl.ds(start, _TKB), :]
            s = lax.dot_general(q_ref[hh], k, (((1,), (1,)), ((), ())), preferred_element_type=F32)
            if masked:
                rc = lax.broadcasted_iota(I32, (_TQ, _TKB), 0) // CHUNK
                cc = lax.broadcasted_iota(I32, (_TQ, _TKB), 1) // CHUNK
                s = jnp.where(cc <= rc, s, NEG_INF)
            sb = [s[:, c * 128:(c + 1) * 128] for c in range(nlb)]
            bm = sb[0]
            for c in range(1, nlb):
                bm = jnp.maximum(bm, sb[c])
            m_prev = m_ref[hh]
            m_new = jnp.maximum(m_prev, jnp.max(bm, axis=-1, keepdims=True))
            alpha = jnp.exp2(m_prev - m_new)
            ps = [jnp.exp2(x - m_new) for x in sb]
            psum = ps[0]
            for c in range(1, nlb):
                psum = psum + ps[c]
            l_ref[hh] = alpha * l_ref[hh] + psum
            p = jnp.concatenate(ps, axis=1).astype(BF16)
            acc_ref[hh] = alpha * acc_ref[hh] + _dot(p, v)
            m_ref[hh] = m_new

    def body(ki, carry):
        step(ki, False)
        return carry

    lax.fori_loop(0, qi, body, 0)
    step(qi, True)
    for hh in range(_HBP):
        l = jnp.sum(l_ref[hh], axis=-1, keepdims=True)
        o_ref[:, hh * V_DIM:(hh + 1) * V_DIM] = (acc_ref[hh] / l).astype(BF16)


def _flash_prompt(q, k, v):
    nq = SEQ // _TQ
    return pl.pallas_call(
        _flash_prompt_kernel,
        grid=(BATCH, N_HEADS // _HBP, nq),
        in_specs=[
            pl.BlockSpec((_HBP, _TQ, HEAD_PAD), lambda b, h, i: (h, b * nq + i, 0)),
            pl.BlockSpec((_HBP, SEQ, HEAD_PAD), lambda b, h, i: (h, b, 0)),
            pl.BlockSpec((_HBP, SEQ, V_DIM), lambda b, h, i: (h, b, 0)),
        ],
        out_specs=pl.BlockSpec((_TQ, _HBP * V_DIM), lambda b, h, i: (b * nq + i, h)),
        out_shape=jax.ShapeDtypeStruct((N_TOK, N_HEADS * V_DIM), BF16),
        scratch_shapes=[pltpu.VMEM((_HBP, _TQ, 128), F32), pltpu.VMEM((_HBP, _TQ, 128), F32),
                        pltpu.VMEM((_HBP, _TQ, V_DIM), F32)],
        compiler_params=_cparams(3),
        name="flash_prompt",
    )(q, k, v)


_KC_ROWS = 512


def _flash_sample_kernel(prev_ref, q_ref, kv_ref, kr_ref, w_ref, gn_ref, gr_ref, c_ref, s_ref, kn_ref, vn_ref,
                         o_ref, kvb_ref, krot_ref, ssqr_ref, k_ref, v_ref):
    del prev_ref

    @pl.when(pl.program_id(1) == 0)
    def _():
        kvb_ref[...] = kv_ref[...].astype(BF16)
        u = kr_ref[...]
        ssqr_ref[...] = jnp.sum(u * u, axis=-1, keepdims=True)
        krot_ref[...] = _rope_pair(u * gr_ref[...], c_ref[...], s_ref[...])

    gn = gn_ref[...]
    nt = (((1,), (1,)), ((), ()))
    for hh in range(_HB):
        w = w_ref[:, hh * HEAD_PAD:(hh + 1) * HEAD_PAD]
        for r in range(0, PAST_LEN, _KC_ROWS):
            rows = slice(r, r + _KC_ROWS)
            z = _dot(kvb_ref[rows, :], w)
            kn = z[:, :NOPE_DIM]
            ssq = jnp.sum(kn * kn, axis=-1, keepdims=True) + ssqr_ref[rows, :]
            scale = lax.rsqrt(ssq * (1.0 / QK_DIM) + EPS)
            k_ref[rows, :NOPE_DIM] = (kn * scale * gn).astype(BF16)
            k_ref[rows, NOPE_DIM:] = (krot_ref[rows, :] * scale).astype(BF16)
            v_ref[rows, :] = z[:, NOPE_DIM:].astype(BF16)
        q = q_ref[hh]
        s1 = lax.dot_general(q, k_ref[...], nt, preferred_element_type=F32)
        s2 = lax.dot_general(q, kn_ref[hh], nt, preferred_element_type=F32)
        m = jnp.maximum(jnp.max(s1, axis=-1, keepdims=True), jnp.max(s2, axis=-1, keepdims=True))
        p1 = jnp.exp2(s1 - m)
        p2 = jnp.exp2(s2 - m)
        l = jnp.sum(p1, axis=-1, keepdims=True) + jnp.sum(p2, axis=-1, keepdims=True)
        o = _dot(p1.astype(BF16), v_ref[...]) + _dot(p2.astype(BF16), vn_ref[hh])
        o_ref[:, hh * V_DIM:(hh + 1) * V_DIM] = (o / l).astype(BF16)


def _flash_sample(attn, q, cache_kv, cache_kr_pad, w_kv, g_kn_nope, g_kn_rope, cos_t, sin_t, k_new, v_new):
    assert (PAST_LEN + DEC_SEQ - 1) // CHUNK <= PAST_LEN // CHUNK
    blk0 = N_P // DEC_SEQ
    new = lambda b, h: (h, blk0 + b, 0)
    const = lambda b, h: (0, 0)
    once = pl.Buffered(1)
    return pl.pallas_call(
        _flash_sample_kernel,
        grid=(DEC_BATCH, N_HEADS // _HB),
        in_specs=[
            pl.BlockSpec(memory_space=pl.ANY),
            pl.BlockSpec((_HB, DEC_SEQ, HEAD_PAD), new),
            pl.BlockSpec((PAST_LEN, KV_LORA_RANK), lambda b, h: (b, 0)),
            pl.BlockSpec((PAST_LEN, 128), lambda b, h: (b, 0)),
            pl.BlockSpec((KV_LORA_RANK, _HB * HEAD_PAD), lambda b, h: (0, h)),
            pl.BlockSpec((1, NOPE_DIM), const),
            pl.BlockSpec((1, 128), const),
            pl.BlockSpec((PAST_LEN, 128), const, pipeline_mode=once),
            pl.BlockSpec((PAST_LEN, 128), const, pipeline_mode=once),
            pl.BlockSpec((_HB, DEC_SEQ, HEAD_PAD), new),
            pl.BlockSpec((_HB, DEC_SEQ, V_DIM), new),
        ],
        out_specs=pl.BlockSpec((DEC_SEQ, _HB * V_DIM), lambda b, h: (blk0 + b, h)),
        out_shape=jax.ShapeDtypeStruct((N_TOK, N_HEADS * V_DIM), BF16),
        scratch_shapes=[pltpu.VMEM((PAST_LEN, KV_LORA_RANK), BF16),
                        pltpu.VMEM((PAST_LEN, 128), F32),
                        pltpu.VMEM((PAST_LEN, 1), F32),
                        pltpu.VMEM((PAST_LEN, HEAD_PAD), BF16),
                        pltpu.VMEM((PAST_LEN, V_DIM), BF16)],
        input_output_aliases={0: 0},
        compiler_params=_cparams(2),
        name="flash_sample",
    )(attn, q, cache_kv, cache_kr_pad, w_kv, g_kn_nope, g_kn_rope, cos_t, sin_t, k_new, v_new)


def _merge_kernel(h_ref, c_ref, a_ref, wga_ref, wgb_ref, bga_ref, bgb_ref, wc_ref, wo_ref, o_ref):
    h = h_ref[...]
    ga = _sigmoid(_dot(h, wga_ref[...]) + bga_ref[...])
    gb = _sigmoid(_dot(h, wgb_ref[...]) + bgb_ref[...])
    mix = ga * _dot(c_ref[...], wc_ref[...]) + gb * _dot(a_ref[...], wo_ref[...])
    o_ref[...] = mix.astype(BF16)


def _merge(h, c_act, attn, w_gate, b_gate, w_conv_out, w_o):
    n = h.shape[0]
    tn = 512
    nj = D_MODEL // tn
    row = lambda i, j: (i, 0)
    return pl.pallas_call(
        _merge_kernel,
        grid=(n // TM, nj),
        in_specs=[
            pl.BlockSpec((TM, D_MODEL), row),
            pl.BlockSpec((TM, CONV_CHANNELS), row),
            pl.BlockSpec((TM, N_HEADS * V_DIM), row),
            pl.BlockSpec((D_MODEL, tn), lambda i, j: (0, j)),
            pl.BlockSpec((D_MODEL, tn), lambda i, j: (0, j + nj)),
            pl.BlockSpec((1, tn), lambda i, j: (0, j)),
            pl.BlockSpec((1, tn), lambda i, j: (0, j + nj)),
            pl.BlockSpec((CONV_CHANNELS, tn), lambda i, j: (0, j)),
            pl.BlockSpec((N_HEADS * V_DIM, tn), lambda i, j: (0, j)),
        ],
        out_specs=pl.BlockSpec((TM, tn), lambda i, j: (i, j)),
        out_shape=jax.ShapeDtypeStruct((n, D_MODEL), BF16),
        compiler_params=_cparams(2),
        name="merge",
    )(h, c_act, attn, w_gate, w_gate, b_gate, b_gate, w_conv_out, w_o)


def _split_bf16(x):
    hi = x.astype(BF16)
    lo = (x - hi.astype(F32)).astype(BF16)
    return hi, lo


_HALF = D_MODEL // 2


def _pack_bf16_pair(a, b):
    ua = lax.bitcast_convert_type(a.astype(BF16).astype(F32), U32)
    ub = lax.bitcast_convert_type(b.astype(BF16).astype(F32), U32)
    return lax.bitcast_convert_type(ua | (ub >> 16), F32)


def _unpack_bf16_pair(w):
    w = lax.bitcast_convert_type(w, U32)
    a = lax.bitcast_convert_type(w & jnp.uint32(0xFFFF0000), F32).astype(BF16)
    b = lax.bitcast_convert_type(w << 16, F32).astype(BF16)
    return a, b


def _out_router_kernel(n_tiles, n_prompt_tiles, mix_ref, xp_ref, xs_ref, w_ref, g_ref, wrh_ref, wrl_ref, br_ref,
                       x1_ref, hm_ref, idx_ref, gate_ref):
    i = pl.program_id(0)

    @pl.when(i < n_tiles)
    def _():
        x = _stacked_rows(i, n_prompt_tiles, xp_ref, xs_ref)
        _out_router_tile(mix_ref, x, w_ref, g_ref, wrh_ref, wrl_ref, br_ref,
                         x1_ref, hm_ref, idx_ref, gate_ref)

    @pl.when(i >= n_tiles)
    def _():
        hm_ref[...] = jnp.zeros(hm_ref.shape, F32)


def _out_router_tile(mix_ref, x, w_ref, g_ref, wrh_ref, wrl_ref, br_ref,
                     x1_ref, hm_ref, idx_ref, gate_ref):
    x1 = x + _dot(mix_ref[...], w_ref[...])
    x1_ref[...] = x1
    hn = x1 * lax.rsqrt(jnp.mean(x1 * x1, axis=-1, keepdims=True) + EPS) * g_ref[...]
    hm_ref[...] = _pack_bf16_pair(hn[:, :_HALF], hn[:, _HALF:])
    hh, hl = _split_bf16(hn)
    logits = _dot(hh, wrh_ref[...]) + (_dot(hh, wrl_ref[...]) + _dot(hl, wrh_ref[...])) + br_ref[...]
    lane = lax.broadcasted_iota(I32, logits.shape, 1).astype(F32)
    vals = []
    idx_out = jnp.zeros(logits.shape, F32)
    for k in range(TOP_K):
        m = jnp.max(logits, axis=-1, keepdims=True)
        sel = jnp.min(jnp.where(logits == m, lane, 1e9), axis=-1, keepdims=True)
        vals.append(m)
        idx_out = jnp.where(lane == float(k), sel, idx_out)
        logits = jnp.where(lane == sel, -jnp.inf, logits)
    exps = [jnp.exp(v - vals[0]) for v in vals]
    denom = exps[0] + exps[1] + exps[2] + exps[3]
    gate_out = jnp.zeros(idx_out.shape, F32)
    for k in range(TOP_K):
        gate_out = jnp.where(lane == float(k), exps[k] / denom, gate_out)
    idx_ref[...] = idx_out.astype(I32)
    gate_ref[...] = gate_out


def _out_router(mix, xp, xs, w_out, g_ffn, wr_hi, wr_lo, b_r):
    n = N_TOK
    tm = 256
    n_tiles = n // tm
    npt = N_P // tm
    const = lambda i: (0, 0)
    row = lambda i: (jnp.minimum(i, n_tiles - 1), 0)
    return pl.pallas_call(
        functools.partial(_out_router_kernel, n_tiles, npt),
        grid=(2 * n_tiles,),
        in_specs=[
            pl.BlockSpec((tm, D_MODEL), row),
            pl.BlockSpec((tm, D_MODEL), lambda i: (jnp.minimum(i, npt - 1), 0)),
            pl.BlockSpec((tm, D_MODEL), lambda i: (jnp.clip(i - npt, 0, N_S // tm - 1), 0)),
            pl.BlockSpec((D_MODEL, D_MODEL), const),
            pl.BlockSpec((1, D_MODEL), const),
            pl.BlockSpec((D_MODEL, 128), const),
            pl.BlockSpec((D_MODEL, 128), const),
            pl.BlockSpec((1, 128), const),
        ],
        out_specs=[
            pl.BlockSpec((tm, D_MODEL), row),
            pl.BlockSpec((tm, _HALF), lambda i: (i, 0)),
            pl.BlockSpec((tm, 128), row),
            pl.BlockSpec((tm, 128), row),
        ],
        out_shape=[
            jax.ShapeDtypeStruct((n, D_MODEL), F32),
            jax.ShapeDtypeStruct((2 * n, _HALF), F32),
            jax.ShapeDtypeStruct((n, 128), I32),
            jax.ShapeDtypeStruct((n, 128), F32),
        ],
        compiler_params=_cparams(1),
        name="out_router",
    )(mix, xp, xs, w_out, g_ffn, wr_hi, wr_lo, b_r)


_F_VALID, _F_FIRST, _F_NEXT, _F_GROUP0, _F_SLOT = 1, 2, 4, 8, 16


_P_E, _P_W, _P_N, _P_B, _P_BI, _P_NE, _P_NW, _P_FL = range(8)


def _stream_weights(t, plan_ref, copies, cast):
    flags = plan_ref[_P_FL, t]

    @pl.when((flags & _F_FIRST) != 0)
    def _():
        slot = (flags // _F_SLOT) & 1
        cur = copies(plan_ref[_P_E, t], plan_ref[_P_W, t], slot)

        @pl.when((flags & _F_GROUP0) != 0)
        def _():
            for c in cur:
                c.start()

        for c in cur:
            c.wait()

        @pl.when((flags & _F_NEXT) != 0)
        def _():
            for c in copies(plan_ref[_P_NE, t], plan_ref[_P_NW, t], 1 - slot):
                c.start()

        cast(slot)


def _moe_up_kernel(plan_ref, prev_ref, x_ref, w_hbm, bg_ref, bu_ref, o_ref, wbuf_ref, wgb_ref, wub_ref, sem_ref):
    del prev_ref
    t = pl.program_id(0)

    def copies(e, w, slot):
        col = pl.multiple_of(w * _UP_TN, _UP_TN)
        return (pltpu.make_async_copy(w_hbm.at[e, :, pl.ds(col, _UP_TN)], wbuf_ref.at[slot, 0], sem_ref.at[slot, 0]),
                pltpu.make_async_copy(w_hbm.at[e, :, pl.ds(col + D_FF, _UP_TN)], wbuf_ref.at[slot, 1],
                                      sem_ref.at[slot, 1]))

    def cast(slot):
        wgb_ref[...] = wbuf_ref[slot, 0].astype(BF16)
        wub_ref[...] = wbuf_ref[slot, 1].astype(BF16)

    _stream_weights(t, plan_ref, copies, cast)
    valid = (plan_ref[_P_FL, t] & _F_VALID) != 0

    @pl.when(valid)
    def _():
        xa, xb = _unpack_bf16_pair(x_ref[...])
        g = _dot(xa, wgb_ref[:_HALF, :]) + _dot(xb, wgb_ref[_HALF:, :]) + bg_ref[0]
        u = _dot(xa, wub_ref[:_HALF, :]) + _dot(xb, wub_ref[_HALF:, :]) + bu_ref[0]
        g = jnp.minimum(g, SWIGLU_LIMIT)
        u = jnp.clip(u, -SWIGLU_LIMIT, SWIGLU_LIMIT)
        o_ref[...] = ((u + 1.0) * (g * _sigmoid(SWIGLU_ALPHA * g))).astype(BF16)

    @pl.when(jnp.logical_not(valid))
    def _():
        o_ref[...] = jnp.zeros(o_ref.shape, BF16)


_UP_TN = 512
_UP_TILES = D_FF // _UP_TN
_DN_TN = 1024
_DN_TILES = D_MODEL // _DN_TN
MOE_CHUNKS = 4
_CHUNK_BLKS = MOE_MAX_BLKS // MOE_CHUNKS


def _moe_up(plan, act_prev, xs, w_gu, b_gu, chunk):
    steps = plan.shape[1]
    blk0 = chunk * _CHUNK_BLKS
    bspec = lambda off: pl.BlockSpec((1, 1, _UP_TN), lambda t, p: (p[_P_E, t], 0, p[_P_W, t] + off))
    aliases = {} if act_prev is None else {1: 0}
    prev = jnp.zeros((8, 128), BF16) if act_prev is None else act_prev
    return pl.pallas_call(
        _moe_up_kernel,
        grid_spec=pltpu.PrefetchScalarGridSpec(
            num_scalar_prefetch=1,
            grid=(steps,),
            in_specs=[
                pl.BlockSpec(memory_space=pl.ANY),
                pl.BlockSpec((MOE_BLK, _HALF), lambda t, p: (p[_P_BI, t], 0)),
                pl.BlockSpec(memory_space=pl.ANY),
                bspec(0), bspec(_UP_TILES),
            ],
            out_specs=pl.BlockSpec((MOE_BLK, _UP_TN),
                                   lambda t, p: (blk0 + p[_P_B, t], p[_P_N, t])),
            scratch_shapes=[pltpu.VMEM((2, 2, D_MODEL, _UP_TN), F32),
                            pltpu.VMEM((D_MODEL, _UP_TN), BF16), pltpu.VMEM((D_MODEL, _UP_TN), BF16),
                            pltpu.SemaphoreType.DMA((2, 2))],
        ),
        out_shape=jax.ShapeDtypeStruct((MOE_ROWS, D_FF), BF16),
        input_output_aliases=aliases,
        compiler_params=_cparams(1),
        name=f"moe_up_{chunk}",
    )(plan, prev, xs, w_gu, b_gu, b_gu)


_DN_HALF = _DN_TN // 2


def _moe_down_kernel(plan_ref, a_ref, w_hbm, b_ref, o_ref, wbuf_ref, wb_ref, sem_ref):
    t = pl.program_id(0)

    def copies(e, w, slot):
        col = pl.multiple_of(w * _DN_TN, _DN_TN)
        return (pltpu.make_async_copy(w_hbm.at[e, :, pl.ds(col, _DN_TN)], wbuf_ref.at[slot], sem_ref.at[slot]),)

    def cast(slot):
        wb_ref[...] = wbuf_ref[slot].astype(BF16)

    _stream_weights(t, plan_ref, copies, cast)
    valid = (plan_ref[_P_FL, t] & _F_VALID) != 0

    @pl.when(valid)
    def _():
        y = _dot(a_ref[...], wb_ref[...]) + b_ref[0]
        o_ref[...] = _pack_bf16_pair(y[:, :_DN_HALF], y[:, _DN_HALF:])

    @pl.when(jnp.logical_not(valid))
    def _():
        o_ref[...] = jnp.zeros(o_ref.shape, F32)


def _moe_down(plan, act, w_dn, b_dn):
    steps = plan.shape[1]
    return pl.pallas_call(
        _moe_down_kernel,
        grid_spec=pltpu.PrefetchScalarGridSpec(
            num_scalar_prefetch=1,
            grid=(steps,),
            in_specs=[
                pl.BlockSpec((MOE_BLK, D_FF), lambda t, p: (p[_P_BI, t], 0)),
                pl.BlockSpec(memory_space=pl.ANY),
                pl.BlockSpec((1, 1, _DN_TN), lambda t, p: (p[_P_E, t], 0, p[_P_W, t])),
            ],
            out_specs=pl.BlockSpec((MOE_BLK, _DN_HALF), lambda t, p: (p[_P_B, t], p[_P_N, t])),
            scratch_shapes=[pltpu.VMEM((2, D_FF, _DN_TN), F32), pltpu.VMEM((D_FF, _DN_TN), BF16),
                            pltpu.SemaphoreType.DMA((2,))],
        ),
        out_shape=jax.ShapeDtypeStruct((MOE_ROWS, _HALF), F32),
        compiler_params=_cparams(1),
        name="moe_down",
    )(plan, act, w_dn, b_dn)


def _moe_dispatch(top_idx):
    n_asg = N_TOK * TOP_K
    flat_e = top_idx.reshape(-1)
    onehot = (flat_e[:, None] == jnp.arange(N_EXPERTS, dtype=I32)[None, :]).astype(I32)
    csum = jnp.cumsum(onehot, axis=0)
    counts = csum[-1]
    rank = jnp.sum(csum * onehot, axis=1) - 1
    nblk = (counts + MOE_BLK - 1) // MOE_BLK
    blk_start = jnp.cumsum(nblk) - nblk
    dest = jnp.sum(onehot * blk_start[None, :], axis=1) * MOE_BLK + rank
    pad_src = jnp.arange(MOE_ROWS, dtype=I32) % N_TOK
    row_tok = pad_src.at[dest].set(jnp.arange(n_asg, dtype=I32) // TOP_K,
                                   mode="promise_in_bounds", unique_indices=True)
    return dest, row_tok, nblk, blk_start


def _moe_steps(nblk, blk_start, n_tiles, blk_lo, n_blks):
    t_max = n_tiles * n_blks
    lo = jnp.clip(blk_start, blk_lo, blk_lo + n_blks)
    hi = jnp.clip(blk_start + nblk, blk_lo, blk_lo + n_blks)
    nb_e = hi - lo
    per_e = nb_e * n_tiles
    s_end = jnp.cumsum(per_e)
    total = s_end[-1]
    t = jnp.arange(t_max, dtype=I32)
    tc = jnp.clip(t, 0, jnp.maximum(total - 1, 0))
    e = jnp.minimum(jnp.sum((s_end[None, :] <= tc[:, None]).astype(I32), axis=1), N_EXPERTS - 1)
    sel = (e[:, None] == jnp.arange(N_EXPERTS, dtype=I32)[None, :]).astype(I32)
    pick = lambda v: jnp.sum(sel * v[None, :], axis=1)
    local = tc - pick(s_end - per_e)
    nb = jnp.maximum(pick(nb_e), 1)
    w_tile = jnp.clip(local // nb, 0, n_tiles - 1)
    r = local % nb
    valid = t < total
    first = jnp.logical_and(valid, r == 0)
    fill = t - total
    blk = jnp.where(valid, pick(lo) - blk_lo + r, total // n_tiles + fill // n_tiles)
    o_tile = jnp.where(valid, w_tile, fill % n_tiles)
    blk = jnp.clip(blk, 0, n_blks - 1)
    blk_in = jnp.where(valid, blk, jnp.maximum(total // n_tiles - 1, 0))
    ids = jnp.arange(N_EXPERTS, dtype=I32)
    owners = jnp.where(nb_e > 0, ids, N_EXPERTS)
    later = jnp.flip(lax.cummin(jnp.flip(owners)))
    next_owner = pick(jnp.concatenate([later[1:], jnp.full((1,), N_EXPERTS, I32)]))
    last_tile = w_tile == n_tiles - 1
    next_e = jnp.where(last_tile, next_owner, e)
    next_w = jnp.where(last_tile, 0, w_tile + 1)
    has_next = jnp.logical_and(first, next_e < N_EXPERTS)
    group = jnp.cumsum(first.astype(I32)) - 1
    flags = (valid * _F_VALID + first * _F_FIRST + has_next * _F_NEXT
             + jnp.logical_and(first, group == 0) * _F_GROUP0 + (group % 2) * _F_SLOT)
    rows = {_P_E: e, _P_W: w_tile, _P_N: o_tile, _P_B: blk, _P_BI: blk_in,
            _P_NE: jnp.minimum(next_e, N_EXPERTS - 1), _P_NW: next_w, _P_FL: flags}
    return jnp.stack([rows[k].astype(I32) for k in range(len(rows))])


def _moe_plans(nblk, blk_start, n_tiles, n_chunks, n_blks):
    los = jnp.arange(n_chunks, dtype=I32) * n_blks
    return jax.vmap(lambda lo: _moe_steps(nblk, blk_start, n_tiles, lo, n_blks))(los)


_FIN_TM = 256
_FIN_TN = 512
FIN_CHUNKS = 4


def _unpack_expert_rows(words):
    u = lax.bitcast_convert_type(words, U32)
    hi = lax.bitcast_convert_type(u & jnp.uint32(0xFFFF0000), F32)
    lo = lax.bitcast_convert_type(u << 16, F32)
    parts = []
    for n in range(_DN_TILES):
        cols = slice(n * _DN_HALF, (n + 1) * _DN_HALF)
        parts += [hi[:, cols], lo[:, cols]]
    return jnp.concatenate(parts, axis=1)


def _final_kernel(prev_ref, x1_ref, y0_ref, y1_ref, y2_ref, y3_ref, gate_ref, g_ref, wg_ref, p_ref, wp_ref,
                  o_ref, x2_ref):
    del prev_ref
    gate = gate_ref[...]
    moe = (_unpack_expert_rows(y0_ref[0]) * gate[:, 0:1] + _unpack_expert_rows(y1_ref[0]) * gate[:, 1:2]
           + _unpack_expert_rows(y2_ref[0]) * gate[:, 2:3] + _unpack_expert_rows(y3_ref[0]) * gate[:, 3:4])
    x2 = x1_ref[...] + moe
    x2_ref[...] = x2
    hp = (x2 * lax.rsqrt(jnp.mean(x2 * x2, axis=-1, keepdims=True) + EPS) * g_ref[...]).astype(BF16)
    pb = p_ref[...].astype(BF16)
    for c in range(0, D_MODEL, _FIN_TN):
        cols = slice(c, c + _FIN_TN)
        emb = _dot(pb, wp_ref[:, cols])
        o_ref[:, cols] = x2_ref[:, cols] + _sigmoid(_dot(hp, wg_ref[:, cols])) * emb


def _final(out_prev, x1, y4, gate, g_ple, w_ple_gate, p, w_ple, tok0, out0, n, n_out, name):
    t0 = tok0 // _FIN_TM
    o0 = out0 // _FIN_TM
    pt0 = out0 // _FIN_TM
    const = lambda i: (0, 0)
    yspec = lambda k: pl.BlockSpec((1, _FIN_TM, _HALF), lambda i: (k, i, 0))
    once = pl.Buffered(1)
    aliases = {} if out_prev is None else {0: 0}
    prev = jnp.zeros((8, 128), F32) if out_prev is None else out_prev
    return pl.pallas_call(
        _final_kernel,
        grid=(n // _FIN_TM,),
        in_specs=[
            pl.BlockSpec(memory_space=pl.ANY),
            pl.BlockSpec((_FIN_TM, D_MODEL), lambda i: (t0 + i, 0)),
            yspec(0), yspec(1), yspec(2), yspec(3),
            pl.BlockSpec((_FIN_TM, 128), lambda i: (t0 + i, 0)),
            pl.BlockSpec((1, D_MODEL), const),
            pl.BlockSpec((D_MODEL, D_MODEL), const, pipeline_mode=once),
            pl.BlockSpec((_FIN_TM, PLE_DIM), lambda i: (pt0 + i, 0)),
            pl.BlockSpec((PLE_DIM, D_MODEL), const, pipeline_mode=once),
        ],
        out_specs=pl.BlockSpec((_FIN_TM, D_MODEL), lambda i: (o0 + i, 0)),
        out_shape=jax.ShapeDtypeStruct((n_out, D_MODEL), F32),
        scratch_shapes=[pltpu.VMEM((_FIN_TM, D_MODEL), F32)],
        input_output_aliases=aliases,
        compiler_params=_cparams(1),
        name=name,
    )(prev, x1, y4, y4, y4, y4, gate, g_ple, w_ple_gate, p, w_ple)


def _rope_layout(x):
    half = ROPE_DIM // 2
    z = jnp.zeros(x.shape[:-1] + (half,), x.dtype)
    return jnp.concatenate([x[..., :half], z, x[..., half:], z], axis=-1)


def _rope_tables():
    half = ROPE_DIM // 2
    inv_freq = ROPE_THETA ** (-jnp.arange(half, dtype=F32) / half)
    pos = jnp.arange(PAST_LEN + DEC_SEQ, dtype=I32)
    ang = pos.astype(F32)[:, None] * inv_freq[None, :]
    cos, sin = jnp.cos(ang), jnp.sin(ang)
    z = jnp.zeros_like(cos)
    c = jnp.concatenate([cos, z, cos, z], axis=-1)
    s = jnp.concatenate([-sin, z, sin, z], axis=-1)
    rep = ATT_TM // DEC_SEQ
    return (jnp.concatenate([c[:SEQ], jnp.tile(c[PAST_LEN:], (rep, 1))], axis=0),
            jnp.concatenate([s[:SEQ], jnp.tile(s[PAST_LEN:], (rep, 1))], axis=0))


def _layer(xp, xs, p_prompt, p_sample, cache_kv, cache_kr, state_conv,
           g_mix, w_in, b_gate, w_dw, b_dw, g_cn, b_cn, w_conv_out,
           g_qa, g_kva, w_qb, w_kb, w_vb, g_qn, g_kn, w_o, w_out,
           g_ffn, w_router, b_router, w_gu, b_gu, w_dn, b_dn,
           g_ple, w_ple_gate, w_ple):
    assert SEQ == PAST_LEN
    row = lambda v: v.reshape(1, -1)
    w_in_b = w_in.astype(BF16)
    w_mid = jnp.concatenate([w_in_b[:, O_U:O_KV], _rope_layout(w_in_b[:, O_KV:O_KR])], axis=1)
    w_gate = w_in_b[:, O_KR:]

    h, q_lat, kv_new, kr_pad = _in_mid(xp, xs, row(g_mix), w_mid, row(g_qa), row(g_kva))
    half = ROPE_DIM // 2
    kr_new = jnp.concatenate([kr_pad[:, :half], kr_pad[:, 2 * half:3 * half]], axis=1)
    glu = _in_glu(h, w_in_b)

    hist = jnp.concatenate([jnp.zeros((BATCH, HALO, CONV_CHANNELS), F32),
                            jnp.pad(state_conv, ((0, 0), (HALO - (CONV_WIDTH - 1), 0), (0, 0)))], axis=0)
    c_act = _conv_module(glu, hist, w_dw, row(b_dw), row(g_cn), row(b_cn))

    cos_t, sin_t = _rope_tables()
    w_q = jnp.concatenate([w_qb[..., :NOPE_DIM], _rope_layout(w_qb[..., NOPE_DIM:])], axis=-1)
    w_q = w_q.reshape(Q_LORA_RANK, N_HEADS * HEAD_PAD).astype(BF16)
    g_q = jnp.concatenate([g_qn[:NOPE_DIM], _rope_layout(g_qn[NOPE_DIM:])]).reshape(1, HEAD_PAD)
    q = _q_heads(q_lat, w_q, g_q, cos_t, sin_t)

    w_kv = jnp.concatenate([w_kb, w_vb], axis=-1).reshape(KV_LORA_RANK, N_HEADS * HEAD_PAD).astype(BF16)
    g_kn_nope = g_kn[:NOPE_DIM].reshape(1, NOPE_DIM)
    g_kn_rope = _rope_layout(g_kn[NOPE_DIM:]).reshape(1, 128)
    k_new, v_new = _kv_heads(kv_new, kr_pad, w_kv, g_kn_nope, g_kn_rope, cos_t, sin_t,
                             _tab_idx_new, "kv_heads_new")
    attn = _flash_prompt(q, k_new, v_new)
    attn = _flash_sample(attn, q, cache_kv.reshape(DEC_BATCH * PAST_LEN, KV_LORA_RANK),
                         _rope_layout(cache_kr).reshape(DEC_BATCH * PAST_LEN, 128),
                         w_kv, g_kn_nope, g_kn_rope, cos_t, sin_t, k_new, v_new)

    mix = _merge(h, c_act, attn, w_gate, row(b_gate), w_conv_out.astype(BF16), w_o.astype(BF16))

    wr = jnp.pad(w_router, ((0, 0), (0, 128 - N_EXPERTS)))
    wr_hi, wr_lo = _split_bf16(wr)
    b_r = jnp.concatenate([b_router, jnp.full((128 - N_EXPERTS,), -jnp.inf, F32)]).reshape(1, 128)
    x1, hm, idx_pad, gate_pad = _out_router(mix, xp, xs, w_out.astype(BF16), row(g_ffn), wr_hi, wr_lo, b_r)

    top_idx = idx_pad[:, :TOP_K]
    dest, row_tok, nblk, blk_start = _moe_dispatch(top_idx)
    b_gu3 = b_gu.reshape(N_EXPERTS, 1, 2 * D_FF)
    chunk_rows = _CHUNK_BLKS * MOE_BLK
    up_plans = _moe_plans(nblk, blk_start, _UP_TILES, MOE_CHUNKS, _CHUNK_BLKS)
    down_plan = _moe_plans(nblk, blk_start, _DN_TILES, 1, MOE_MAX_BLKS)[0]
    act = None
    for c in range(MOE_CHUNKS):
        xs = hm.at[row_tok[c * chunk_rows:(c + 1) * chunk_rows]].get(mode="promise_in_bounds")
        act = _moe_up(up_plans[c], act, xs, w_gu, b_gu3, c)
    ys = _moe_down(down_plan, act, w_dn, b_dn.reshape(N_EXPERTS, 1, D_MODEL))

    dest_t = dest.reshape(N_TOK, TOP_K).T
    fin = (row(g_ple), w_ple_gate.astype(BF16))
    w_ple_b = w_ple.astype(BF16)
    n_c = N_P // FIN_CHUNKS
    out_p = None
    for c in range(FIN_CHUNKS):
        y4 = ys.at[dest_t[:, c * n_c:(c + 1) * n_c]].get(mode="promise_in_bounds")
        out_p = _final(out_p, x1, y4, gate_pad, *fin, p_prompt, w_ple_b, c * n_c, c * n_c, n_c, N_P,
                       f"final_prompt_{c}")
    y4 = ys.at[dest_t[:, N_P:]].get(mode="promise_in_bounds")
    out_s = _final(None, x1, y4, gate_pad, *fin, p_sample, w_ple_b, N_P, 0, N_S, N_S, "final_sample")
    return out_p, out_s, kv_new, kr_new, glu


def kernel(x_prompt, x_sample, cache_kv_latent, cache_k_rope, state_conv, p_prompt, p_sample, g_mix, w_in, b_gate, w_dw, b_dw, g_cn, b_cn, w_conv_out, g_qa, g_kva, w_qb, w_kb, w_vb, g_qn, g_kn, w_o, w_out, g_ffn, w_router, b_router, w_gu, b_gu, w_dn, b_dn, g_ple, w_ple_gate, w_ple):
    assert g_mix.shape[0] == 1
    out_p, out_s, kv_new, kr_new, glu = _layer(
        x_prompt.reshape(N_P, D_MODEL), x_sample.reshape(N_S, D_MODEL),
        p_prompt[0].reshape(N_P, PLE_DIM), p_sample[0].reshape(N_S, PLE_DIM),
        cache_kv_latent[0], cache_k_rope[0], state_conv[0],
        g_mix[0], w_in[0], b_gate[0], w_dw[0], b_dw[0], g_cn[0], b_cn[0], w_conv_out[0],
        g_qa[0], g_kva[0], w_qb[0], w_kb[0], w_vb[0], g_qn[0], g_kn[0], w_o[0], w_out[0],
        g_ffn[0], w_router[0], b_router[0], w_gu[0], b_gu[0], w_dn[0], b_dn[0],
        g_ple[0], w_ple_gate[0], w_ple[0])
    tail = CONV_WIDTH - 1
    conv_p = jnp.stack([glu[(b + 1) * SEQ - tail:(b + 1) * SEQ] for b in range(BATCH)])
    conv_s = glu[N_P:].reshape(DEC_BATCH, DEC_SEQ, CONV_CHANNELS)[:, DEC_SEQ - tail:]
    return (out_p.reshape(BATCH, SEQ, D_MODEL),
            out_s.reshape(DEC_BATCH, DEC_SEQ, D_MODEL),
            kv_new[:N_P].reshape(1, BATCH, SEQ, KV_LORA_RANK),
            kr_new[:N_P].reshape(1, BATCH, SEQ, ROPE_DIM),
            conv_p[None],
            kv_new[N_P:].reshape(1, DEC_BATCH, DEC_SEQ, KV_LORA_RANK),
            kr_new[N_P:].reshape(1, DEC_BATCH, DEC_SEQ, ROPE_DIM),
            conv_s[None])
```

```python
import functools
import math

import jax
import jax.numpy as jnp
from jax import lax
from jax.experimental import pallas as pl
from jax.experimental.pallas import tpu as pltpu

F32 = jnp.float32
BF16 = jnp.bfloat16
I32 = jnp.int32
U32 = jnp.uint32

D_MODEL = 2048
BATCH = 2
SEQ = 4096
DEC_BATCH = 8
DEC_SEQ = 64
PAST_LEN = 4096
CHUNK = 64
CONV_CHANNELS = D_MODEL
CONV_WIDTH = 31
N_HEADS = 16
Q_LORA_RANK = 512
KV_LORA_RANK = 512
NOPE_DIM = 128
ROPE_DIM = 64
QK_DIM = NOPE_DIM + ROPE_DIM
V_DIM = 128
ROPE_THETA = 10000.0
N_EXPERTS = 32
TOP_K = 4
D_FF = D_MODEL
SWIGLU_ALPHA = 1.702
SWIGLU_LIMIT = 7.0
PLE_DIM = 256
EPS = 1e-6
NEG_INF = -1e30

N_P = BATCH * SEQ
N_S = DEC_BATCH * DEC_SEQ
N_TOK = N_P + N_S
O_U = 2 * CONV_CHANNELS
O_Q = O_U + Q_LORA_RANK
O_KV = O_Q + KV_LORA_RANK
O_KR = O_KV + ROPE_DIM
MID_W = 1152
HEAD_PAD = 256

TM = 512
CONV_T = 64
HALO = 32
MOE_BLK = 512
MOE_MAX_BLKS = (N_TOK * TOP_K) // MOE_BLK + N_EXPERTS
MOE_ROWS = MOE_MAX_BLKS * MOE_BLK
VMEM_LIMIT = 48 * 1024 * 1024


def _cparams(n_axes):
    return pltpu.CompilerParams(dimension_semantics=("arbitrary",) * n_axes,
                                vmem_limit_bytes=VMEM_LIMIT)


def _sigmoid(x):
    return 1.0 / (1.0 + jnp.exp(-x))


def _dot(a, b):
    return jnp.dot(a, b, preferred_element_type=F32)


def _stacked_rows(i, n_prompt_tiles, xp_ref, xs_ref):
    return jnp.where(i < n_prompt_tiles, xp_ref[...], xs_ref[...])


def _in_mid_kernel(xp_ref, xs_ref, g_ref, w_ref, gqa_ref, gkva_ref, h_ref, q_ref, kv_ref, kr_ref):
    x = _stacked_rows(pl.program_id(0), N_P // TM, xp_ref, xs_ref)
    h = x * lax.rsqrt(jnp.mean(x * x, axis=-1, keepdims=True) + EPS) * g_ref[...]
    hb = h.astype(BF16)
    h_ref[...] = hb
    z = _dot(hb, w_ref[...])
    ql = z[:, :Q_LORA_RANK]
    kvl = z[:, Q_LORA_RANK:Q_LORA_RANK + KV_LORA_RANK]
    qn = ql * lax.rsqrt(jnp.mean(ql * ql, axis=-1, keepdims=True) + EPS) * gqa_ref[...]
    q_ref[...] = qn.astype(BF16)
    kv_ref[...] = kvl * lax.rsqrt(jnp.mean(kvl * kvl, axis=-1, keepdims=True) + EPS) * gkva_ref[...]
    kr_ref[...] = z[:, Q_LORA_RANK + KV_LORA_RANK:]


def _in_mid(xp, xs, g_mix, w_mid, g_qa, g_kva):
    n = N_TOK
    npt = N_P // TM
    return pl.pallas_call(
        _in_mid_kernel,
        grid=(n // TM,),
        in_specs=[
            pl.BlockSpec((TM, D_MODEL), lambda i: (jnp.minimum(i, npt - 1), 0)),
            pl.BlockSpec((TM, D_MODEL), lambda i: (jnp.maximum(i - npt, 0), 0)),
            pl.BlockSpec((1, D_MODEL), lambda i: (0, 0)),
            pl.BlockSpec((D_MODEL, MID_W), lambda i: (0, 0)),
            pl.BlockSpec((1, Q_LORA_RANK), lambda i: (0, 0)),
            pl.BlockSpec((1, KV_LORA_RANK), lambda i: (0, 0)),
        ],
        out_specs=[
            pl.BlockSpec((TM, D_MODEL), lambda i: (i, 0)),
            pl.BlockSpec((TM, Q_LORA_RANK), lambda i: (i, 0)),
            pl.BlockSpec((TM, KV_LORA_RANK), lambda i: (i, 0)),
            pl.BlockSpec((TM, 128), lambda i: (i, 0)),
        ],
        out_shape=[
            jax.ShapeDtypeStruct((n, D_MODEL), BF16),
            jax.ShapeDtypeStruct((n, Q_LORA_RANK), BF16),
            jax.ShapeDtypeStruct((n, KV_LORA_RANK), F32),
            jax.ShapeDtypeStruct((n, 128), F32),
        ],
        compiler_params=_cparams(1),
        name="in_mid",
    )(xp, xs, g_mix, w_mid, g_qa, g_kva)


def _glu_kernel(h_ref, w1_ref, w2_ref, o_ref):
    h = h_ref[...]
    o_ref[...] = _dot(h, w1_ref[...]) * _sigmoid(_dot(h, w2_ref[...]))


def _in_glu(h, w_in_b):
    n = h.shape[0]
    tn = 512
    nj = CONV_CHANNELS // tn
    return pl.pallas_call(
        _glu_kernel,
        grid=(n // TM, nj),
        in_specs=[
            pl.BlockSpec((TM, D_MODEL), lambda i, j: (i, 0)),
            pl.BlockSpec((D_MODEL, tn), lambda i, j: (0, j)),
            pl.BlockSpec((D_MODEL, tn), lambda i, j: (0, j + nj)),
        ],
        out_specs=pl.BlockSpec((TM, tn), lambda i, j: (i, j)),
        out_shape=jax.ShapeDtypeStruct((n, CONV_CHANNELS), F32),
        compiler_params=_cparams(2),
        name="in_glu",
    )(h, w_in_b, w_in_b)


_CONV_TILES_PER_SEQ = SEQ // CONV_T
_CONV_PROMPT_TILES = N_P // CONV_T
_CONV_LANES = 512
SUBLANES = 8
_SHIFT_ROWS = (HALO // SUBLANES - 1) * SUBLANES + CONV_T


def _conv_kernel(cur_ref, prev_ref, hist_ref, w_ref, bdw_ref, g_ref, b_ref, o_ref, win_ref, conv_ref, shift_ref):
    i = pl.program_id(0)
    first = jnp.logical_or(i >= _CONV_PROMPT_TILES, i % _CONV_TILES_PER_SEQ == 0)

    @pl.when(first)
    def _():
        win_ref[0:HALO, :] = hist_ref[0]

    @pl.when(jnp.logical_not(first))
    def _():
        win_ref[0:HALO, :] = prev_ref[...]

    win_ref[HALO:HALO + CONV_T, :] = cur_ref[...]
    for r in range(1, SUBLANES):
        shift_ref[r - 1] = win_ref[r:r + _SHIFT_ROWS, :]
    base = HALO - (CONV_WIDTH - 1)
    for c in range(0, CONV_CHANNELS, _CONV_LANES):
        acc = jnp.zeros((CONV_T, _CONV_LANES), F32)
        for k in range(CONV_WIDTH):
            q, r = divmod(base + k, SUBLANES)
            lanes = slice(c, c + _CONV_LANES)
            rows = slice(q * SUBLANES, q * SUBLANES + CONV_T)
            src = win_ref[rows, lanes] if r == 0 else shift_ref[r - 1, rows, lanes]
            acc = acc + w_ref[k:k + 1, lanes] * src
        conv_ref[:, c:c + _CONV_LANES] = acc + bdw_ref[:, c:c + _CONV_LANES]
    y = conv_ref[...]
    yc = y - jnp.mean(y, axis=-1, keepdims=True)
    var = jnp.mean(yc * yc, axis=-1, keepdims=True)
    z = yc * lax.rsqrt(var + EPS) * g_ref[...] + b_ref[...]
    o_ref[...] = (z * _sigmoid(z)).astype(BF16)


def _conv_module(glu, hist, w_dw, b_dw, g_cn, b_cn):
    n = glu.shape[0]
    n_tiles = n // CONV_T
    halo_per_tile = CONV_T // HALO

    def seq_of(i):
        return jnp.where(i < _CONV_PROMPT_TILES, i // _CONV_TILES_PER_SEQ, i - _CONV_PROMPT_TILES + BATCH)

    return pl.pallas_call(
        _conv_kernel,
        grid=(n_tiles,),
        in_specs=[
            pl.BlockSpec((CONV_T, CONV_CHANNELS), lambda i: (i, 0)),
            pl.BlockSpec((HALO, CONV_CHANNELS), lambda i: (jnp.maximum(i * halo_per_tile - 1, 0), 0)),
            pl.BlockSpec((1, HALO, CONV_CHANNELS), lambda i: (seq_of(i), 0, 0)),
            pl.BlockSpec((CONV_WIDTH, CONV_CHANNELS), lambda i: (0, 0)),
            pl.BlockSpec((1, CONV_CHANNELS), lambda i: (0, 0)),
            pl.BlockSpec((1, CONV_CHANNELS), lambda i: (0, 0)),
            pl.BlockSpec((1, CONV_CHANNELS), lambda i: (0, 0)),
        ],
        out_specs=pl.BlockSpec((CONV_T, CONV_CHANNELS), lambda i: (i, 0)),
        out_shape=jax.ShapeDtypeStruct((n, CONV_CHANNELS), BF16),
        scratch_shapes=[pltpu.VMEM((HALO + CONV_T, CONV_CHANNELS), F32),
                        pltpu.VMEM((CONV_T, CONV_CHANNELS), F32),
                        pltpu.VMEM((SUBLANES - 1, _SHIFT_ROWS, CONV_CHANNELS), F32)],
        compiler_params=_cparams(1),
        name="conv_module",
    )(glu, glu, hist, w_dw, b_dw, g_cn, b_cn)


ATT_TM = 256
_TAB_PROMPT_TILES = N_P // ATT_TM
_TAB_SEQ_TILES = SEQ // ATT_TM
_TAB_ROWS = SEQ + ATT_TM


def _tab_idx_new(i):
    return jnp.where(i < _TAB_PROMPT_TILES, i % _TAB_SEQ_TILES, _TAB_SEQ_TILES)


def _rope_pair(u, c, s):
    return u * c + pltpu.roll(u, 64, 1) * s


_Q_SCALE = math.log2(math.e) / math.sqrt(QK_DIM)


def _q_heads_kernel(ql_ref, w_ref, g_ref, c_ref, s_ref, o_ref):
    ql = ql_ref[...]
    g = g_ref[...]
    c = c_ref[...]
    s = s_ref[...]
    for h in range(N_HEADS):
        qf = _dot(ql, w_ref[:, h * HEAD_PAD:(h + 1) * HEAD_PAD])
        ssq = jnp.sum(qf * qf, axis=-1, keepdims=True)
        qn = qf * (lax.rsqrt(ssq * (1.0 / QK_DIM) + EPS) * _Q_SCALE) * g
        o_ref[h, :, :NOPE_DIM] = qn[:, :NOPE_DIM].astype(BF16)
        o_ref[h, :, NOPE_DIM:] = _rope_pair(qn[:, NOPE_DIM:], c, s).astype(BF16)


def _q_heads(q_lat, w_q, g_q, cos_t, sin_t):
    n = q_lat.shape[0]
    return pl.pallas_call(
        _q_heads_kernel,
        grid=(n // ATT_TM,),
        in_specs=[
            pl.BlockSpec((ATT_TM, Q_LORA_RANK), lambda i: (i, 0)),
            pl.BlockSpec((Q_LORA_RANK, N_HEADS * HEAD_PAD), lambda i: (0, 0)),
            pl.BlockSpec((1, HEAD_PAD), lambda i: (0, 0)),
            pl.BlockSpec((ATT_TM, 128), lambda i: (_tab_idx_new(i), 0)),
            pl.BlockSpec((ATT_TM, 128), lambda i: (_tab_idx_new(i), 0)),
        ],
        out_specs=pl.BlockSpec((N_HEADS, ATT_TM, HEAD_PAD), lambda i: (0, i, 0)),
        out_shape=jax.ShapeDtypeStruct((N_HEADS, n, HEAD_PAD), BF16),
        compiler_params=_cparams(1),
        name="q_heads",
    )(q_lat, w_q, g_q, cos_t, sin_t)


def _kv_heads_kernel(kv_ref, kr_ref, w_ref, gr_ref, c_ref, s_ref, k_ref, v_ref):
    kv = kv_ref[...].astype(BF16)
    u = kr_ref[...]
    ssq_r = jnp.sum(u * u, axis=-1, keepdims=True)
    krot = _rope_pair(u * gr_ref[...], c_ref[...], s_ref[...])
    for h in range(N_HEADS):
        z = _dot(kv, w_ref[:, h * HEAD_PAD:(h + 1) * HEAD_PAD])
        kn = z[:, :NOPE_DIM]
        ssq = jnp.sum(kn * kn, axis=-1, keepdims=True) + ssq_r
        scale = lax.rsqrt(ssq * (1.0 / QK_DIM) + EPS)
        k_ref[h, :, :NOPE_DIM] = (kn * scale).astype(BF16)
        k_ref[h, :, NOPE_DIM:] = (krot * scale).astype(BF16)
        v_ref[h] = z[:, NOPE_DIM:].astype(BF16)


def _kv_heads(kv_lat, kr_pad, w_kv, g_kn_rope, cos_t, sin_t, tab_idx, name):
    n = kv_lat.shape[0]
    return pl.pallas_call(
        _kv_heads_kernel,
        grid=(n // ATT_TM,),
        in_specs=[
            pl.BlockSpec((ATT_TM, KV_LORA_RANK), lambda i: (i, 0)),
            pl.BlockSpec((ATT_TM, 128), lambda i: (i, 0)),
            pl.BlockSpec((KV_LORA_RANK, N_HEADS * HEAD_PAD), lambda i: (0, 0)),
            pl.BlockSpec((1, 128), lambda i: (0, 0)),
            pl.BlockSpec((ATT_TM, 128), lambda i: (tab_idx(i), 0)),
            pl.BlockSpec((ATT_TM, 128), lambda i: (tab_idx(i), 0)),
        ],
        out_specs=[
            pl.BlockSpec((N_HEADS, ATT_TM, HEAD_PAD), lambda i: (0, i, 0)),
            pl.BlockSpec((N_HEADS, ATT_TM, V_DIM), lambda i: (0, i, 0)),
        ],
        out_shape=[
            jax.ShapeDtypeStruct((N_HEADS, n, HEAD_PAD), BF16),
            jax.ShapeDtypeStruct((N_HEADS, n, V_DIM), BF16),
        ],
        compiler_params=_cparams(1),
        name=name,
    )(kv_lat, kr_pad, w_kv, g_kn_rope, cos_t, sin_t)


_TQ = 512
_TKB = 512
_HB = 2
_HBP = 4


def _flash_prompt_kernel(q_ref, k_ref, v_ref, o_ref, m_ref, l_ref, acc_ref):
    qi = pl.program_id(2)
    m_ref[...] = jnp.full(m_ref.shape, NEG_INF, F32)
    l_ref[...] = jnp.zeros(l_ref.shape, F32)
    acc_ref[...] = jnp.zeros(acc_ref.shape, F32)
    nlb = _TKB // 128

    def step(ki, masked):
        start = pl.multiple_of(ki * _TKB, _TKB)
        scores = [lax.dot_general(q_ref[hh], k_ref[hh, pl.ds(start, _TKB), :], (((1,), (1,)), ((), ())),
                                  preferred_element_type=F32) for hh in range(_HBP)]
        probs = []
        for hh in range(_HBP):
            s = scores[hh]
            if masked:
                rc = lax.broadcasted_iota(I32, (_TQ, _TKB), 0) // CHUNK
                cc = lax.broadcasted_iota(I32, (_TQ, _TKB), 1) // CHUNK
                s = jnp.where(cc <= rc, s, NEG_INF)
            sb = [s[:, c * 128:(c + 1) * 128] for c in range(nlb)]
            bm = sb[0]
            for c in range(1, nlb):
                bm = jnp.maximum(bm, sb[c])
            m_prev = m_ref[hh]
            m_new = jnp.maximum(m_prev, jnp.max(bm, axis=-1, keepdims=True))
            alpha = jnp.exp2(m_prev - m_new)
            ps = [jnp.exp2(x - m_new) for x in sb]
            psum = ps[0]
            for c in range(1, nlb):
                psum = psum + ps[c]
            l_ref[hh] = alpha * l_ref[hh] + psum
            m_ref[hh] = m_new
            probs.append((alpha, jnp.concatenate(ps, axis=1).astype(BF16)))
        for hh in range(_HBP):
            alpha, p = probs[hh]
            acc_ref[hh] = alpha * acc_ref[hh] + _dot(p, v_ref[hh, pl.ds(start, _TKB), :])

    def body(ki, carry):
        step(ki, False)
        return carry

    lax.fori_loop(0, qi, body, 0)
    step(qi, True)
    for hh in range(_HBP):
        l = jnp.sum(l_ref[hh], axis=-1, keepdims=True)
        o_ref[:, hh * V_DIM:(hh + 1) * V_DIM] = (acc_ref[hh] / l).astype(BF16)


def _flash_prompt(q, k, v):
    nq = SEQ // _TQ
    return pl.pallas_call(
        _flash_prompt_kernel,
        grid=(BATCH, N_HEADS // _HBP, nq),
        in_specs=[
            pl.BlockSpec((_HBP, _TQ, HEAD_PAD), lambda b, h, i: (h, b * nq + i, 0)),
            pl.BlockSpec((_HBP, SEQ, HEAD_PAD), lambda b, h, i: (h, b, 0)),
            pl.BlockSpec((_HBP, SEQ, V_DIM), lambda b, h, i: (h, b, 0)),
        ],
        out_specs=pl.BlockSpec((_TQ, _HBP * V_DIM), lambda b, h, i: (b * nq + i, h)),
        out_shape=jax.ShapeDtypeStruct((N_TOK, N_HEADS * V_DIM), BF16),
        scratch_shapes=[pltpu.VMEM((_HBP, _TQ, 128), F32), pltpu.VMEM((_HBP, _TQ, 128), F32),
                        pltpu.VMEM((_HBP, _TQ, V_DIM), F32)],
        compiler_params=_cparams(3),
        name="flash_prompt",
    )(q, k, v)


_KC_ROWS = 512


def _flash_sample_kernel(prev_ref, q_ref, kv_ref, kr_ref, w_ref, gr_ref, c_ref, s_ref, kn_ref, vn_ref,
                         o_ref, kvb_ref, krot_ref, ssqr_ref, k_ref, v_ref):
    del prev_ref

    @pl.when(pl.program_id(1) == 0)
    def _():
        kvb_ref[...] = kv_ref[...].astype(BF16)
        u = kr_ref[...]
        ssqr_ref[...] = jnp.broadcast_to(jnp.sum(u * u, axis=-1, keepdims=True), ssqr_ref.shape)
        krot_ref[...] = _rope_pair(u * gr_ref[...], c_ref[...], s_ref[...])

    nt = (((1,), (1,)), ((), ()))
    for hh in range(_HB):
        w = w_ref[:, hh * HEAD_PAD:(hh + 1) * HEAD_PAD]
        for r in range(0, PAST_LEN, _KC_ROWS):
            rows = slice(r, r + _KC_ROWS)
            z = _dot(kvb_ref[rows, :], w)
            kn = z[:, :NOPE_DIM]
            ssq = jnp.sum(kn * kn, axis=-1, keepdims=True) + ssqr_ref[rows, :]
            scale = lax.rsqrt(ssq * (1.0 / QK_DIM) + EPS)
            k_ref[rows, :NOPE_DIM] = (kn * scale).astype(BF16)
            k_ref[rows, NOPE_DIM:] = (krot_ref[rows, :] * scale).astype(BF16)
            v_ref[rows, :] = z[:, NOPE_DIM:].astype(BF16)
        q = q_ref[hh]
        s1 = lax.dot_general(q, k_ref[...], nt, preferred_element_type=F32)
        s2 = lax.dot_general(q, kn_ref[hh], nt, preferred_element_type=F32)
        m = jnp.maximum(jnp.max(s1, axis=-1, keepdims=True), jnp.max(s2, axis=-1, keepdims=True))
        p1 = jnp.exp2(s1 - m)
        p2 = jnp.exp2(s2 - m)
        l = jnp.sum(p1, axis=-1, keepdims=True) + jnp.sum(p2, axis=-1, keepdims=True)
        o = _dot(p1.astype(BF16), v_ref[...]) + _dot(p2.astype(BF16), vn_ref[hh])
        o_ref[:, hh * V_DIM:(hh + 1) * V_DIM] = (o / l).astype(BF16)


def _flash_sample(attn, q, cache_kv, cache_kr_pad, w_kv, g_kn_rope, cos_t, sin_t, k_new, v_new):
    assert (PAST_LEN + DEC_SEQ - 1) // CHUNK <= PAST_LEN // CHUNK
    blk0 = N_P // DEC_SEQ
    new = lambda b, h: (h, blk0 + b, 0)
    const = lambda b, h: (0, 0)
    once = pl.Buffered(1)
    return pl.pallas_call(
        _flash_sample_kernel,
        grid=(DEC_BATCH, N_HEADS // _HB),
        in_specs=[
            pl.BlockSpec(memory_space=pl.ANY),
            pl.BlockSpec((_HB, DEC_SEQ, HEAD_PAD), new),
            pl.BlockSpec((PAST_LEN, KV_LORA_RANK), lambda b, h: (b, 0)),
            pl.BlockSpec((PAST_LEN, 128), lambda b, h: (b, 0)),
            pl.BlockSpec((KV_LORA_RANK, _HB * HEAD_PAD), lambda b, h: (0, h)),
            pl.BlockSpec((1, 128), const),
            pl.BlockSpec((PAST_LEN, 128), const, pipeline_mode=once),
            pl.BlockSpec((PAST_LEN, 128), const, pipeline_mode=once),
            pl.BlockSpec((_HB, DEC_SEQ, HEAD_PAD), new),
            pl.BlockSpec((_HB, DEC_SEQ, V_DIM), new),
        ],
        out_specs=pl.BlockSpec((DEC_SEQ, _HB * V_DIM), lambda b, h: (blk0 + b, h)),
        out_shape=jax.ShapeDtypeStruct((N_TOK, N_HEADS * V_DIM), BF16),
        scratch_shapes=[pltpu.VMEM((PAST_LEN, KV_LORA_RANK), BF16),
                        pltpu.VMEM((PAST_LEN, 128), F32),
                        pltpu.VMEM((PAST_LEN, 128), F32),
                        pltpu.VMEM((PAST_LEN, HEAD_PAD), BF16),
                        pltpu.VMEM((PAST_LEN, V_DIM), BF16)],
        input_output_aliases={0: 0},
        compiler_params=_cparams(2),
        name="flash_sample",
    )(attn, q, cache_kv, cache_kr_pad, w_kv, g_kn_rope, cos_t, sin_t, k_new, v_new)


def _merge_kernel(h_ref, c_ref, a_ref, wga_ref, wgb_ref, bga_ref, bgb_ref, wc_ref, wo_ref, o_ref):
    h = h_ref[...]
    ga = _sigmoid(_dot(h, wga_ref[...]) + bga_ref[...])
    gb = _sigmoid(_dot(h, wgb_ref[...]) + bgb_ref[...])
    mix = ga * _dot(c_ref[...], wc_ref[...]) + gb * _dot(a_ref[...], wo_ref[...])
    o_ref[...] = mix.astype(BF16)


def _merge(h, c_act, attn, w_gate, b_gate, w_conv_out, w_o):
    n = h.shape[0]
    tn = 512
    nj = D_MODEL // tn
    row = lambda i, j: (i, 0)
    return pl.pallas_call(
        _merge_kernel,
        grid=(n // TM, nj),
        in_specs=[
            pl.BlockSpec((TM, D_MODEL), row),
            pl.BlockSpec((TM, CONV_CHANNELS), row),
            pl.BlockSpec((TM, N_HEADS * V_DIM), row),
            pl.BlockSpec((D_MODEL, tn), lambda i, j: (0, j)),
            pl.BlockSpec((D_MODEL, tn), lambda i, j: (0, j + nj)),
            pl.BlockSpec((1, tn), lambda i, j: (0, j)),
            pl.BlockSpec((1, tn), lambda i, j: (0, j + nj)),
            pl.BlockSpec((CONV_CHANNELS, tn), lambda i, j: (0, j)),
            pl.BlockSpec((N_HEADS * V_DIM, tn), lambda i, j: (0, j)),
        ],
        out_specs=pl.BlockSpec((TM, tn), lambda i, j: (i, j)),
        out_shape=jax.ShapeDtypeStruct((n, D_MODEL), BF16),
        compiler_params=_cparams(2),
        name="merge",
    )(h, c_act, attn, w_gate, w_gate, b_gate, b_gate, w_conv_out, w_o)


def _split_bf16(x):
    hi = x.astype(BF16)
    lo = (x - hi.astype(F32)).astype(BF16)
    return hi, lo


_HALF = D_MODEL // 2


def _pack_bf16_pair(a, b):
    ua = lax.bitcast_convert_type(a.astype(BF16).astype(F32), U32)
    ub = lax.bitcast_convert_type(b.astype(BF16).astype(F32), U32)
    return lax.bitcast_convert_type(ua | (ub >> 16), F32)


def _unpack_bf16_pair(w):
    w = lax.bitcast_convert_type(w, U32)
    a = lax.bitcast_convert_type(w & jnp.uint32(0xFFFF0000), F32).astype(BF16)
    b = lax.bitcast_convert_type(w << 16, F32).astype(BF16)
    return a, b


def _out_router_kernel(n_tiles, n_prompt_tiles, mix_ref, xp_ref, xs_ref, w_ref, g_ref, wrh_ref, wrl_ref, br_ref,
                       x1_ref, hm_ref, idx_ref, gate_ref):
    i = pl.program_id(0)

    @pl.when(i < n_tiles)
    def _():
        x = _stacked_rows(i, n_prompt_tiles, xp_ref, xs_ref)
        _out_router_tile(mix_ref, x, w_ref, g_ref, wrh_ref, wrl_ref, br_ref,
                         x1_ref, hm_ref, idx_ref, gate_ref)

    @pl.when(i >= n_tiles)
    def _():
        hm_ref[...] = jnp.zeros(hm_ref.shape, F32)


def _out_router_tile(mix_ref, x, w_ref, g_ref, wrh_ref, wrl_ref, br_ref,
                     x1_ref, hm_ref, idx_ref, gate_ref):
    x1 = x + _dot(mix_ref[...], w_ref[...])
    x1_ref[...] = x1
    hn = x1 * lax.rsqrt(jnp.mean(x1 * x1, axis=-1, keepdims=True) + EPS) * g_ref[...]
    hm_ref[...] = _pack_bf16_pair(hn[:, :_HALF], hn[:, _HALF:])
    hh, hl = _split_bf16(hn)
    logits = _dot(hh, wrh_ref[...]) + (_dot(hh, wrl_ref[...]) + _dot(hl, wrh_ref[...])) + br_ref[...]
    lane = lax.broadcasted_iota(I32, logits.shape, 1).astype(F32)
    vals = []
    idx_out = jnp.zeros(logits.shape, F32)
    for k in range(TOP_K):
        m = jnp.max(logits, axis=-1, keepdims=True)
        sel = jnp.min(jnp.where(logits == m, lane, 1e9), axis=-1, keepdims=True)
        vals.append(m)
        idx_out = jnp.where(lane == float(k), sel, idx_out)
        logits = jnp.where(lane == sel, -jnp.inf, logits)
    exps = [jnp.exp(v - vals[0]) for v in vals]
    denom = exps[0] + exps[1] + exps[2] + exps[3]
    gate_out = jnp.zeros(idx_out.shape, F32)
    for k in range(TOP_K):
        gate_out = jnp.where(lane == float(k), exps[k] / denom, gate_out)
    idx_ref[...] = idx_out.astype(I32)
    gate_ref[...] = gate_out


def _out_router(mix, xp, xs, w_out, g_ffn, wr_hi, wr_lo, b_r):
    n = N_TOK
    tm = 256
    n_tiles = n // tm
    npt = N_P // tm
    const = lambda i: (0, 0)
    row = lambda i: (jnp.minimum(i, n_tiles - 1), 0)
    return pl.pallas_call(
        functools.partial(_out_router_kernel, n_tiles, npt),
        grid=(2 * n_tiles,),
        in_specs=[
            pl.BlockSpec((tm, D_MODEL), row),
            pl.BlockSpec((tm, D_MODEL), lambda i: (jnp.minimum(i, npt - 1), 0)),
            pl.BlockSpec((tm, D_MODEL), lambda i: (jnp.clip(i - npt, 0, N_S // tm - 1), 0)),
            pl.BlockSpec((D_MODEL, D_MODEL), const),
            pl.BlockSpec((1, D_MODEL), const),
            pl.BlockSpec((D_MODEL, 128), const),
            pl.BlockSpec((D_MODEL, 128), const),
            pl.BlockSpec((1, 128), const),
        ],
        out_specs=[
            pl.BlockSpec((tm, D_MODEL), row),
            pl.BlockSpec((tm, _HALF), lambda i: (i, 0)),
            pl.BlockSpec((tm, 128), row),
            pl.BlockSpec((tm, 128), row),
        ],
        out_shape=[
            jax.ShapeDtypeStruct((n, D_MODEL), F32),
            jax.ShapeDtypeStruct((2 * n, _HALF), F32),
            jax.ShapeDtypeStruct((n, 128), I32),
            jax.ShapeDtypeStruct((n, 128), F32),
        ],
        compiler_params=_cparams(1),
        name="out_router",
    )(mix, xp, xs, w_out, g_ffn, wr_hi, wr_lo, b_r)


_F_VALID, _F_FIRST, _F_NEXT, _F_GROUP0, _F_SLOT = 1, 2, 4, 8, 16


_P_E, _P_W, _P_N, _P_B, _P_BI, _P_NE, _P_NW, _P_FL, _P_SUBS = range(9)
MOE_SUB = 128


def _stream_weights(t, plan_ref, copies, cast):
    flags = plan_ref[_P_FL, t]

    @pl.when((flags & _F_FIRST) != 0)
    def _():
        slot = (flags // _F_SLOT) & 1
        cur = copies(plan_ref[_P_E, t], plan_ref[_P_W, t], slot)

        @pl.when((flags & _F_GROUP0) != 0)
        def _():
            for c in cur:
                c.start()

        for c in cur:
            c.wait()

        @pl.when((flags & _F_NEXT) != 0)
        def _():
            for c in copies(plan_ref[_P_NE, t], plan_ref[_P_NW, t], 1 - slot):
                c.start()

        cast(slot)


def _for_used_rows(valid, subs, rows_body):
    for n_sub in range(1, MOE_BLK // MOE_SUB + 1):
        @pl.when(jnp.logical_and(valid, subs == n_sub))
        def _(m=n_sub * MOE_SUB):
            rows_body(m)


def _moe_up_kernel(plan_ref, prev_ref, x_ref, w_hbm, bg_ref, bu_ref, o_ref, wbuf_ref, wgb_ref, wub_ref, sem_ref):
    del prev_ref
    t = pl.program_id(0)

    def copies(e, w, slot):
        col = pl.multiple_of(w * _UP_TN, _UP_TN)
        return (pltpu.make_async_copy(w_hbm.at[e, :, pl.ds(col, _UP_TN)], wbuf_ref.at[slot, 0], sem_ref.at[slot, 0]),
                pltpu.make_async_copy(w_hbm.at[e, :, pl.ds(col + D_FF, _UP_TN)], wbuf_ref.at[slot, 1],
                                      sem_ref.at[slot, 1]))

    def cast(slot):
        wgb_ref[...] = wbuf_ref[slot, 0].astype(BF16)
        wub_ref[...] = wbuf_ref[slot, 1].astype(BF16)

    _stream_weights(t, plan_ref, copies, cast)
    valid = (plan_ref[_P_FL, t] & _F_VALID) != 0

    def rows_body(m):
        xa, xb = _unpack_bf16_pair(x_ref[:m, :])
        g = _dot(xa, wgb_ref[:_HALF, :]) + _dot(xb, wgb_ref[_HALF:, :]) + bg_ref[0]
        u = _dot(xa, wub_ref[:_HALF, :]) + _dot(xb, wub_ref[_HALF:, :]) + bu_ref[0]
        g = jnp.minimum(g, SWIGLU_LIMIT)
        u = jnp.clip(u, -SWIGLU_LIMIT, SWIGLU_LIMIT)
        o_ref[:m, :] = ((u + 1.0) * (g * _sigmoid(SWIGLU_ALPHA * g))).astype(BF16)
        if m < MOE_BLK:
            o_ref[m:, :] = jnp.zeros((MOE_BLK - m, o_ref.shape[1]), BF16)

    _for_used_rows(valid, plan_ref[_P_SUBS, t], rows_body)

    @pl.when(jnp.logical_not(valid))
    def _():
        o_ref[...] = jnp.zeros(o_ref.shape, BF16)


_UP_TN = 512
_UP_TILES = D_FF // _UP_TN
_DN_TN = 1024
_DN_TILES = D_MODEL // _DN_TN
MOE_CHUNKS = 4
_CHUNK_BLKS = MOE_MAX_BLKS // MOE_CHUNKS


def _moe_up(plan, act_prev, xs, w_gu, b_gu, chunk):
    steps = plan.shape[1]
    blk0 = chunk * _CHUNK_BLKS
    bspec = lambda off: pl.BlockSpec((1, 1, _UP_TN), lambda t, p: (p[_P_E, t], 0, p[_P_W, t] + off))
    aliases = {} if act_prev is None else {1: 0}
    prev = jnp.zeros((8, 128), BF16) if act_prev is None else act_prev
    return pl.pallas_call(
        _moe_up_kernel,
        grid_spec=pltpu.PrefetchScalarGridSpec(
            num_scalar_prefetch=1,
            grid=(steps,),
            in_specs=[
                pl.BlockSpec(memory_space=pl.ANY),
                pl.BlockSpec((MOE_BLK, _HALF), lambda t, p: (p[_P_BI, t], 0)),
                pl.BlockSpec(memory_space=pl.ANY),
                bspec(0), bspec(_UP_TILES),
            ],
            out_specs=pl.BlockSpec((MOE_BLK, _UP_TN),
                                   lambda t, p: (blk0 + p[_P_B, t], p[_P_N, t])),
            scratch_shapes=[pltpu.VMEM((2, 2, D_MODEL, _UP_TN), F32),
                            pltpu.VMEM((D_MODEL, _UP_TN), BF16), pltpu.VMEM((D_MODEL, _UP_TN), BF16),
                            pltpu.SemaphoreType.DMA((2, 2))],
        ),
        out_shape=jax.ShapeDtypeStruct((MOE_ROWS, D_FF), BF16),
        input_output_aliases=aliases,
        compiler_params=_cparams(1),
        name=f"moe_up_{chunk}",
    )(plan, prev, xs, w_gu, b_gu, b_gu)


_DN_HALF = _DN_TN // 2


def _moe_down_kernel(plan_ref, a_ref, w_hbm, b_ref, o_ref, wbuf_ref, wb_ref, sem_ref):
    t = pl.program_id(0)

    def copies(e, w, slot):
        col = pl.multiple_of(w * _DN_TN, _DN_TN)
        return (pltpu.make_async_copy(w_hbm.at[e, :, pl.ds(col, _DN_TN)], wbuf_ref.at[slot], sem_ref.at[slot]),)

    def cast(slot):
        wb_ref[...] = wbuf_ref[slot].astype(BF16)

    _stream_weights(t, plan_ref, copies, cast)
    valid = (plan_ref[_P_FL, t] & _F_VALID) != 0

    def rows_body(m):
        y = _dot(a_ref[:m, :], wb_ref[...]) + b_ref[0]
        o_ref[:m, :] = _pack_bf16_pair(y[:, :_DN_HALF], y[:, _DN_HALF:])
        if m < MOE_BLK:
            o_ref[m:, :] = jnp.zeros((MOE_BLK - m, o_ref.shape[1]), F32)

    _for_used_rows(valid, plan_ref[_P_SUBS, t], rows_body)

    @pl.when(jnp.logical_not(valid))
    def _():
        o_ref[...] = jnp.zeros(o_ref.shape, F32)


def _moe_down(plan, act, w_dn, b_dn):
    steps = plan.shape[1]
    return pl.pallas_call(
        _moe_down_kernel,
        grid_spec=pltpu.PrefetchScalarGridSpec(
            num_scalar_prefetch=1,
            grid=(steps,),
            in_specs=[
                pl.BlockSpec((MOE_BLK, D_FF), lambda t, p: (p[_P_BI, t], 0)),
                pl.BlockSpec(memory_space=pl.ANY),
                pl.BlockSpec((1, 1, _DN_TN), lambda t, p: (p[_P_E, t], 0, p[_P_W, t])),
            ],
            out_specs=pl.BlockSpec((MOE_BLK, _DN_HALF), lambda t, p: (p[_P_B, t], p[_P_N, t])),
            scratch_shapes=[pltpu.VMEM((2, D_FF, _DN_TN), F32), pltpu.VMEM((D_FF, _DN_TN), BF16),
                            pltpu.SemaphoreType.DMA((2,))],
        ),
        out_shape=jax.ShapeDtypeStruct((MOE_ROWS, _HALF), F32),
        compiler_params=_cparams(1),
        name="moe_down",
    )(plan, act, w_dn, b_dn)


def _moe_dispatch(top_idx):
    n_asg = N_TOK * TOP_K
    flat_e = top_idx.reshape(-1)
    onehot = (flat_e[:, None] == jnp.arange(N_EXPERTS, dtype=I32)[None, :]).astype(I32)
    csum = jnp.cumsum(onehot, axis=0)
    counts = csum[-1]
    rank = jnp.sum(csum * onehot, axis=1) - 1
    nblk = (counts + MOE_BLK - 1) // MOE_BLK
    blk_start = jnp.cumsum(nblk) - nblk
    dest = jnp.sum(onehot * blk_start[None, :], axis=1) * MOE_BLK + rank
    pad_src = jnp.arange(MOE_ROWS, dtype=I32) % N_TOK
    row_tok = pad_src.at[dest].set(jnp.arange(n_asg, dtype=I32) // TOP_K,
                                   mode="promise_in_bounds", unique_indices=True)
    return dest, row_tok, counts, nblk, blk_start


def _moe_steps(counts, nblk, blk_start, n_tiles, blk_lo, n_blks):
    t_max = n_tiles * n_blks
    lo = jnp.clip(blk_start, blk_lo, blk_lo + n_blks)
    hi = jnp.clip(blk_start + nblk, blk_lo, blk_lo + n_blks)
    nb_e = hi - lo
    per_e = nb_e * n_tiles
    s_end = jnp.cumsum(per_e)
    total = s_end[-1]
    t = jnp.arange(t_max, dtype=I32)
    tc = jnp.clip(t, 0, jnp.maximum(total - 1, 0))
    e = jnp.minimum(jnp.sum((s_end[None, :] <= tc[:, None]).astype(I32), axis=1), N_EXPERTS - 1)
    sel = (e[:, None] == jnp.arange(N_EXPERTS, dtype=I32)[None, :]).astype(I32)
    pick = lambda v: jnp.sum(sel * v[None, :], axis=1)
    local = tc - pick(s_end - per_e)
    nb = jnp.maximum(pick(nb_e), 1)
    w_tile = jnp.clip(local // nb, 0, n_tiles - 1)
    r = local % nb
    valid = t < total
    first = jnp.logical_and(valid, r == 0)
    fill = t - total
    blk = jnp.where(valid, pick(lo) - blk_lo + r, total // n_tiles + fill // n_tiles)
    rows_used = pick(counts) - (pick(lo) + r - pick(blk_start)) * MOE_BLK
    subs = jnp.clip((rows_used + MOE_SUB - 1) // MOE_SUB, 1, MOE_BLK // MOE_SUB)
    o_tile = jnp.where(valid, w_tile, fill % n_tiles)
    blk = jnp.clip(blk, 0, n_blks - 1)
    blk_in = jnp.where(valid, blk, jnp.maximum(total // n_tiles - 1, 0))
    ids = jnp.arange(N_EXPERTS, dtype=I32)
    owners = jnp.where(nb_e > 0, ids, N_EXPERTS)
    later = jnp.flip(lax.cummin(jnp.flip(owners)))
    next_owner = pick(jnp.concatenate([later[1:], jnp.full((1,), N_EXPERTS, I32)]))
    last_tile = w_tile == n_tiles - 1
    next_e = jnp.where(last_tile, next_owner, e)
    next_w = jnp.where(last_tile, 0, w_tile + 1)
    has_next = jnp.logical_and(first, next_e < N_EXPERTS)
    group = jnp.cumsum(first.astype(I32)) - 1
    flags = (valid * _F_VALID + first * _F_FIRST + has_next * _F_NEXT
             + jnp.logical_and(first, group == 0) * _F_GROUP0 + (group % 2) * _F_SLOT)
    rows = {_P_E: e, _P_W: w_tile, _P_N: o_tile, _P_B: blk, _P_BI: blk_in,
            _P_NE: jnp.minimum(next_e, N_EXPERTS - 1), _P_NW: next_w, _P_FL: flags, _P_SUBS: subs}
    return jnp.stack([rows[k].astype(I32) for k in range(len(rows))])


def _moe_plans(counts, nblk, blk_start, n_tiles, n_chunks, n_blks):
    los = jnp.arange(n_chunks, dtype=I32) * n_blks
    return jax.vmap(lambda lo: _moe_steps(counts, nblk, blk_start, n_tiles, lo, n_blks))(los)


_FIN_TM = 256
_FIN_TN = 512
FIN_CHUNKS = 4


def _unpack_expert_rows(words):
    u = lax.bitcast_convert_type(words, U32)
    hi = lax.bitcast_convert_type(u & jnp.uint32(0xFFFF0000), F32)
    lo = lax.bitcast_convert_type(u << 16, F32)
    parts = []
    for n in range(_DN_TILES):
        cols = slice(n * _DN_HALF, (n + 1) * _DN_HALF)
        parts += [hi[:, cols], lo[:, cols]]
    return jnp.concatenate(parts, axis=1)


def _final_kernel(prev_ref, x1_ref, y0_ref, y1_ref, y2_ref, y3_ref, gate_ref, g_ref, wg_ref, p_ref, wp_ref,
                  o_ref, x2_ref):
    del prev_ref
    gate = gate_ref[...]
    moe = (_unpack_expert_rows(y0_ref[0]) * gate[:, 0:1] + _unpack_expert_rows(y1_ref[0]) * gate[:, 1:2]
           + _unpack_expert_rows(y2_ref[0]) * gate[:, 2:3] + _unpack_expert_rows(y3_ref[0]) * gate[:, 3:4])
    x2 = x1_ref[...] + moe
    x2_ref[...] = x2
    hp = (x2 * lax.rsqrt(jnp.mean(x2 * x2, axis=-1, keepdims=True) + EPS) * g_ref[...]).astype(BF16)
    pb = p_ref[...].astype(BF16)
    for c in range(0, D_MODEL, _FIN_TN):
        cols = slice(c, c + _FIN_TN)
        emb = _dot(pb, wp_ref[:, cols])
        o_ref[:, cols] = x2_ref[:, cols] + _sigmoid(_dot(hp, wg_ref[:, cols])) * emb


def _final(out_prev, x1, y4, gate, g_ple, w_ple_gate, p, w_ple, tok0, out0, n, n_out, name):
    t0 = tok0 // _FIN_TM
    o0 = out0 // _FIN_TM
    pt0 = out0 // _FIN_TM
    const = lambda i: (0, 0)
    yspec = lambda k: pl.BlockSpec((1, _FIN_TM, _HALF), lambda i: (k, i, 0))
    once = pl.Buffered(1)
    aliases = {} if out_prev is None else {0: 0}
    prev = jnp.zeros((8, 128), F32) if out_prev is None else out_prev
    return pl.pallas_call(
        _final_kernel,
        grid=(n // _FIN_TM,),
        in_specs=[
            pl.BlockSpec(memory_space=pl.ANY),
            pl.BlockSpec((_FIN_TM, D_MODEL), lambda i: (t0 + i, 0)),
            yspec(0), yspec(1), yspec(2), yspec(3),
            pl.BlockSpec((_FIN_TM, 128), lambda i: (t0 + i, 0)),
            pl.BlockSpec((1, D_MODEL), const),
            pl.BlockSpec((D_MODEL, D_MODEL), const, pipeline_mode=once),
            pl.BlockSpec((_FIN_TM, PLE_DIM), lambda i: (pt0 + i, 0)),
            pl.BlockSpec((PLE_DIM, D_MODEL), const, pipeline_mode=once),
        ],
        out_specs=pl.BlockSpec((_FIN_TM, D_MODEL), lambda i: (o0 + i, 0)),
        out_shape=jax.ShapeDtypeStruct((n_out, D_MODEL), F32),
        scratch_shapes=[pltpu.VMEM((_FIN_TM, D_MODEL), F32)],
        input_output_aliases=aliases,
        compiler_params=_cparams(1),
        name=name,
    )(prev, x1, y4, y4, y4, y4, gate, g_ple, w_ple_gate, p, w_ple)


def _rope_layout(x):
    half = ROPE_DIM // 2
    z = jnp.zeros(x.shape[:-1] + (half,), x.dtype)
    return jnp.concatenate([x[..., :half], z, x[..., half:], z], axis=-1)


def _rope_tables():
    half = ROPE_DIM // 2
    inv_freq = ROPE_THETA ** (-jnp.arange(half, dtype=F32) / half)
    pos = jnp.arange(PAST_LEN + DEC_SEQ, dtype=I32)
    ang = pos.astype(F32)[:, None] * inv_freq[None, :]
    cos, sin = jnp.cos(ang), jnp.sin(ang)
    z = jnp.zeros_like(cos)
    c = jnp.concatenate([cos, z, cos, z], axis=-1)
    s = jnp.concatenate([-sin, z, sin, z], axis=-1)
    rep = ATT_TM // DEC_SEQ
    return (jnp.concatenate([c[:SEQ], jnp.tile(c[PAST_LEN:], (rep, 1))], axis=0),
            jnp.concatenate([s[:SEQ], jnp.tile(s[PAST_LEN:], (rep, 1))], axis=0))


def _layer(xp, xs, p_prompt, p_sample, cache_kv, cache_kr, state_conv,
           g_mix, w_in, b_gate, w_dw, b_dw, g_cn, b_cn, w_conv_out,
           g_qa, g_kva, w_qb, w_kb, w_vb, g_qn, g_kn, w_o, w_out,
           g_ffn, w_router, b_router, w_gu, b_gu, w_dn, b_dn,
           g_ple, w_ple_gate, w_ple):
    assert SEQ == PAST_LEN
    row = lambda v: v.reshape(1, -1)
    w_in_b = w_in.astype(BF16)
    w_mid = jnp.concatenate([w_in_b[:, O_U:O_KV], _rope_layout(w_in_b[:, O_KV:O_KR])], axis=1)
    w_gate = w_in_b[:, O_KR:]

    h, q_lat, kv_new, kr_pad = _in_mid(xp, xs, row(g_mix), w_mid, row(g_qa), row(g_kva))
    half = ROPE_DIM // 2
    kr_new = jnp.concatenate([kr_pad[:, :half], kr_pad[:, 2 * half:3 * half]], axis=1)
    glu = _in_glu(h, w_in_b)

    hist = jnp.concatenate([jnp.zeros((BATCH, HALO, CONV_CHANNELS), F32),
                            jnp.pad(state_conv, ((0, 0), (HALO - (CONV_WIDTH - 1), 0), (0, 0)))], axis=0)
    c_act = _conv_module(glu, hist, w_dw, row(b_dw), row(g_cn), row(b_cn))

    cos_t, sin_t = _rope_tables()
    w_q = jnp.concatenate([w_qb[..., :NOPE_DIM], _rope_layout(w_qb[..., NOPE_DIM:])], axis=-1)
    w_q = w_q.reshape(Q_LORA_RANK, N_HEADS * HEAD_PAD).astype(BF16)
    g_q = jnp.concatenate([g_qn[:NOPE_DIM] * g_kn[:NOPE_DIM], _rope_layout(g_qn[NOPE_DIM:])]).reshape(1, HEAD_PAD)
    q = _q_heads(q_lat, w_q, g_q, cos_t, sin_t)

    w_kv = jnp.concatenate([w_kb, w_vb], axis=-1).reshape(KV_LORA_RANK, N_HEADS * HEAD_PAD).astype(BF16)
    g_kn_rope = _rope_layout(g_kn[NOPE_DIM:]).reshape(1, 128)
    k_new, v_new = _kv_heads(kv_new, kr_pad, w_kv, g_kn_rope, cos_t, sin_t, _tab_idx_new, "kv_heads_new")
    attn = _flash_prompt(q, k_new, v_new)
    attn = _flash_sample(attn, q, cache_kv.reshape(DEC_BATCH * PAST_LEN, KV_LORA_RANK),
                         _rope_layout(cache_kr).reshape(DEC_BATCH * PAST_LEN, 128),
                         w_kv, g_kn_rope, cos_t, sin_t, k_new, v_new)

    mix = _merge(h, c_act, attn, w_gate, row(b_gate), w_conv_out.astype(BF16), w_o.astype(BF16))

    wr = jnp.pad(w_router, ((0, 0), (0, 128 - N_EXPERTS)))
    wr_hi, wr_lo = _split_bf16(wr)
    b_r = jnp.concatenate([b_router, jnp.full((128 - N_EXPERTS,), -jnp.inf, F32)]).reshape(1, 128)
    x1, hm, idx_pad, gate_pad = _out_router(mix, xp, xs, w_out.astype(BF16), row(g_ffn), wr_hi, wr_lo, b_r)

    top_idx = idx_pad[:, :TOP_K]
    dest, row_tok, counts, nblk, blk_start = _moe_dispatch(top_idx)
    b_gu3 = b_gu.reshape(N_EXPERTS, 1, 2 * D_FF)
    chunk_rows = _CHUNK_BLKS * MOE_BLK
    up_plans = _moe_plans(counts, nblk, blk_start, _UP_TILES, MOE_CHUNKS, _CHUNK_BLKS)
    down_plan = _moe_plans(counts, nblk, blk_start, _DN_TILES, 1, MOE_MAX_BLKS)[0]
    act = None
    for c in range(MOE_CHUNKS):
        xs = hm.at[row_tok[c * chunk_rows:(c + 1) * chunk_rows]].get(mode="promise_in_bounds")
        act = _moe_up(up_plans[c], act, xs, w_gu, b_gu3, c)
    ys = _moe_down(down_plan, act, w_dn, b_dn.reshape(N_EXPERTS, 1, D_MODEL))

    dest_t = dest.reshape(N_TOK, TOP_K).T
    fin = (row(g_ple), w_ple_gate.astype(BF16))
    w_ple_b = w_ple.astype(BF16)
    n_c = N_P // FIN_CHUNKS
    out_p = None
    for c in range(FIN_CHUNKS):
        y4 = ys.at[dest_t[:, c * n_c:(c + 1) * n_c]].get(mode="promise_in_bounds")
        out_p = _final(out_p, x1, y4, gate_pad, *fin, p_prompt, w_ple_b, c * n_c, c * n_c, n_c, N_P,
                       f"final_prompt_{c}")
    y4 = ys.at[dest_t[:, N_P:]].get(mode="promise_in_bounds")
    out_s = _final(None, x1, y4, gate_pad, *fin, p_sample, w_ple_b, N_P, 0, N_S, N_S, "final_sample")
    return out_p, out_s, kv_new, kr_new, glu


def kernel(x_prompt, x_sample, cache_kv_latent, cache_k_rope, state_conv, p_prompt, p_sample, g_mix, w_in, b_gate, w_dw, b_dw, g_cn, b_cn, w_conv_out, g_qa, g_kva, w_qb, w_kb, w_vb, g_qn, g_kn, w_o, w_out, g_ffn, w_router, b_router, w_gu, b_gu, w_dn, b_dn, g_ple, w_ple_gate, w_ple):
    assert g_mix.shape[0] == 1
    out_p, out_s, kv_new, kr_new, glu = _layer(
        x_prompt.reshape(N_P, D_MODEL), x_sample.reshape(N_S, D_MODEL),
        p_prompt[0].reshape(N_P, PLE_DIM), p_sample[0].reshape(N_S, PLE_DIM),
        cache_kv_latent[0], cache_k_rope[0], state_conv[0],
        g_mix[0], w_in[0], b_gate[0], w_dw[0], b_dw[0], g_cn[0], b_cn[0], w_conv_out[0],
        g_qa[0], g_kva[0], w_qb[0], w_kb[0], w_vb[0], g_qn[0], g_kn[0], w_o[0], w_out[0],
        g_ffn[0], w_router[0], b_router[0], w_gu[0], b_gu[0], w_dn[0], b_dn[0],
        g_ple[0], w_ple_gate[0], w_ple[0])
    tail = CONV_WIDTH - 1
    conv_p = jnp.stack([glu[(b + 1) * SEQ - tail:(b + 1) * SEQ] for b in range(BATCH)])
    conv_s = glu[N_P:].reshape(DEC_BATCH, DEC_SEQ, CONV_CHANNELS)[:, DEC_SEQ - tail:]
    return (out_p.reshape(BATCH, SEQ, D_MODEL),
            out_s.reshape(DEC_BATCH, DEC_SEQ, D_MODEL),
            kv_new[:N_P].reshape(1, BATCH, SEQ, KV_LORA_RANK),
            kr_new[:N_P].reshape(1, BATCH, SEQ, ROPE_DIM),
            conv_p[None],
            kv_new[N_P:].reshape(1, DEC_BATCH, DEC_SEQ, KV_LORA_RANK),
            kr_new[N_P:].reshape(1, DEC_BATCH, DEC_SEQ, ROPE_DIM),
            conv_s[None])
```

```python
import functools
import math

import jax
import jax.numpy as jnp
from jax import lax
from jax.experimental import pallas as pl
from jax.experimental.pallas import tpu as pltpu

F32 = jnp.float32
BF16 = jnp.bfloat16
I32 = jnp.int32
U32 = jnp.uint32

D_MODEL = 2048
BATCH = 2
SEQ = 4096
DEC_BATCH = 8
DEC_SEQ = 64
PAST_LEN = 4096
CHUNK = 64
CONV_CHANNELS = D_MODEL
CONV_WIDTH = 31
N_HEADS = 16
Q_LORA_RANK = 512
KV_LORA_RANK = 512
NOPE_DIM = 128
ROPE_DIM = 64
QK_DIM = NOPE_DIM + ROPE_DIM
V_DIM = 128
ROPE_THETA = 10000.0
N_EXPERTS = 32
TOP_K = 4
D_FF = D_MODEL
SWIGLU_ALPHA = 1.702
SWIGLU_LIMIT = 7.0
PLE_DIM = 256
EPS = 1e-6
NEG_INF = -1e30

N_P = BATCH * SEQ
N_S = DEC_BATCH * DEC_SEQ
N_TOK = N_P + N_S
O_U = 2 * CONV_CHANNELS
O_Q = O_U + Q_LORA_RANK
O_KV = O_Q + KV_LORA_RANK
O_KR = O_KV + ROPE_DIM
MID_W = 1152
HEAD_PAD = 256

TM = 512
CONV_T = 64
HALO = 32
MOE_BLK = 512
MOE_MAX_BLKS = (N_TOK * TOP_K) // MOE_BLK + N_EXPERTS
MOE_ROWS = MOE_MAX_BLKS * MOE_BLK
VMEM_LIMIT = 48 * 1024 * 1024


def _cparams(n_axes):
    return pltpu.CompilerParams(dimension_semantics=("arbitrary",) * n_axes,
                                vmem_limit_bytes=VMEM_LIMIT)


def _sigmoid(x):
    return 1.0 / (1.0 + jnp.exp(-x))


def _dot(a, b):
    return jnp.dot(a, b, preferred_element_type=F32)


def _stacked_rows(i, n_prompt_tiles, xp_ref, xs_ref):
    return jnp.where(i < n_prompt_tiles, xp_ref[...], xs_ref[...])


def _in_mid_kernel(xp_ref, xs_ref, g_ref, w_ref, gqa_ref, gkva_ref, h_ref, q_ref, kv_ref, kr_ref):
    x = _stacked_rows(pl.program_id(0), N_P // TM, xp_ref, xs_ref)
    h = x * lax.rsqrt(jnp.mean(x * x, axis=-1, keepdims=True) + EPS) * g_ref[...]
    hb = h.astype(BF16)
    h_ref[...] = hb
    z = _dot(hb, w_ref[...])
    ql = z[:, :Q_LORA_RANK]
    kvl = z[:, Q_LORA_RANK:Q_LORA_RANK + KV_LORA_RANK]
    qn = ql * lax.rsqrt(jnp.mean(ql * ql, axis=-1, keepdims=True) + EPS) * gqa_ref[...]
    q_ref[...] = qn.astype(BF16)
    kv_ref[...] = kvl * lax.rsqrt(jnp.mean(kvl * kvl, axis=-1, keepdims=True) + EPS) * gkva_ref[...]
    kr_ref[...] = z[:, Q_LORA_RANK + KV_LORA_RANK:]


def _in_mid(xp, xs, g_mix, w_mid, g_qa, g_kva):
    n = N_TOK
    npt = N_P // TM
    return pl.pallas_call(
        _in_mid_kernel,
        grid=(n // TM,),
        in_specs=[
            pl.BlockSpec((TM, D_MODEL), lambda i: (jnp.minimum(i, npt - 1), 0)),
            pl.BlockSpec((TM, D_MODEL), lambda i: (jnp.maximum(i - npt, 0), 0)),
            pl.BlockSpec((1, D_MODEL), lambda i: (0, 0)),
            pl.BlockSpec((D_MODEL, MID_W), lambda i: (0, 0)),
            pl.BlockSpec((1, Q_LORA_RANK), lambda i: (0, 0)),
            pl.BlockSpec((1, KV_LORA_RANK), lambda i: (0, 0)),
        ],
        out_specs=[
            pl.BlockSpec((TM, D_MODEL), lambda i: (i, 0)),
            pl.BlockSpec((TM, Q_LORA_RANK), lambda i: (i, 0)),
            pl.BlockSpec((TM, KV_LORA_RANK), lambda i: (i, 0)),
            pl.BlockSpec((TM, 128), lambda i: (i, 0)),
        ],
        out_shape=[
            jax.ShapeDtypeStruct((n, D_MODEL), BF16),
            jax.ShapeDtypeStruct((n, Q_LORA_RANK), BF16),
            jax.ShapeDtypeStruct((n, KV_LORA_RANK), F32),
            jax.ShapeDtypeStruct((n, 128), F32),
        ],
        compiler_params=_cparams(1),
        name="in_mid",
    )(xp, xs, g_mix, w_mid, g_qa, g_kva)


def _glu_kernel(h_ref, w1_ref, w2_ref, o_ref):
    h = h_ref[...]
    o_ref[...] = _dot(h, w1_ref[...]) * _sigmoid(_dot(h, w2_ref[...]))


def _in_glu(h, w_in_b):
    n = h.shape[0]
    tn = 1024
    nj = CONV_CHANNELS // tn
    return pl.pallas_call(
        _glu_kernel,
        grid=(n // TM, nj),
        in_specs=[
            pl.BlockSpec((TM, D_MODEL), lambda i, j: (i, 0)),
            pl.BlockSpec((D_MODEL, tn), lambda i, j: (0, j)),
            pl.BlockSpec((D_MODEL, tn), lambda i, j: (0, j + nj)),
        ],
        out_specs=pl.BlockSpec((TM, tn), lambda i, j: (i, j)),
        out_shape=jax.ShapeDtypeStruct((n, CONV_CHANNELS), F32),
        compiler_params=_cparams(2),
        name="in_glu",
    )(h, w_in_b, w_in_b)


_CONV_TILES_PER_SEQ = SEQ // CONV_T
_CONV_PROMPT_TILES = N_P // CONV_T
_CONV_LANES = 512
SUBLANES = 8
_SHIFT_ROWS = (HALO // SUBLANES - 1) * SUBLANES + CONV_T


def _conv_kernel(cur_ref, prev_ref, hist_ref, w_ref, bdw_ref, g_ref, b_ref, o_ref, win_ref, conv_ref, shift_ref):
    i = pl.program_id(0)
    first = jnp.logical_or(i >= _CONV_PROMPT_TILES, i % _CONV_TILES_PER_SEQ == 0)

    @pl.when(first)
    def _():
        win_ref[0:HALO, :] = hist_ref[0]

    @pl.when(jnp.logical_not(first))
    def _():
        win_ref[0:HALO, :] = prev_ref[...]

    win_ref[HALO:HALO + CONV_T, :] = cur_ref[...]
    for r in range(1, SUBLANES):
        shift_ref[r - 1] = win_ref[r:r + _SHIFT_ROWS, :]
    base = HALO - (CONV_WIDTH - 1)
    for c in range(0, CONV_CHANNELS, _CONV_LANES):
        acc = jnp.zeros((CONV_T, _CONV_LANES), F32)
        for k in range(CONV_WIDTH):
            q, r = divmod(base + k, SUBLANES)
            lanes = slice(c, c + _CONV_LANES)
            rows = slice(q * SUBLANES, q * SUBLANES + CONV_T)
            src = win_ref[rows, lanes] if r == 0 else shift_ref[r - 1, rows, lanes]
            acc = acc + w_ref[k:k + 1, lanes] * src
        conv_ref[:, c:c + _CONV_LANES] = acc + bdw_ref[:, c:c + _CONV_LANES]
    y = conv_ref[...]
    yc = y - jnp.mean(y, axis=-1, keepdims=True)
    var = jnp.mean(yc * yc, axis=-1, keepdims=True)
    z = yc * lax.rsqrt(var + EPS) * g_ref[...] + b_ref[...]
    o_ref[...] = (z * _sigmoid(z)).astype(BF16)


def _conv_module(glu, hist, w_dw, b_dw, g_cn, b_cn):
    n = glu.shape[0]
    n_tiles = n // CONV_T
    halo_per_tile = CONV_T // HALO

    def seq_of(i):
        return jnp.where(i < _CONV_PROMPT_TILES, i // _CONV_TILES_PER_SEQ, i - _CONV_PROMPT_TILES + BATCH)

    return pl.pallas_call(
        _conv_kernel,
        grid=(n_tiles,),
        in_specs=[
            pl.BlockSpec((CONV_T, CONV_CHANNELS), lambda i: (i, 0)),
            pl.BlockSpec((HALO, CONV_CHANNELS), lambda i: (jnp.maximum(i * halo_per_tile - 1, 0), 0)),
            pl.BlockSpec((1, HALO, CONV_CHANNELS), lambda i: (seq_of(i), 0, 0)),
            pl.BlockSpec((CONV_WIDTH, CONV_CHANNELS), lambda i: (0, 0)),
            pl.BlockSpec((1, CONV_CHANNELS), lambda i: (0, 0)),
            pl.BlockSpec((1, CONV_CHANNELS), lambda i: (0, 0)),
            pl.BlockSpec((1, CONV_CHANNELS), lambda i: (0, 0)),
        ],
        out_specs=pl.BlockSpec((CONV_T, CONV_CHANNELS), lambda i: (i, 0)),
        out_shape=jax.ShapeDtypeStruct((n, CONV_CHANNELS), BF16),
        scratch_shapes=[pltpu.VMEM((HALO + CONV_T, CONV_CHANNELS), F32),
                        pltpu.VMEM((CONV_T, CONV_CHANNELS), F32),
                        pltpu.VMEM((SUBLANES - 1, _SHIFT_ROWS, CONV_CHANNELS), F32)],
        compiler_params=_cparams(1),
        name="conv_module",
    )(glu, glu, hist, w_dw, b_dw, g_cn, b_cn)


ATT_TM = 512
_TAB_PROMPT_TILES = N_P // ATT_TM
_TAB_SEQ_TILES = SEQ // ATT_TM
_TAB_ROWS = SEQ + ATT_TM


def _tab_idx_new(i):
    return jnp.where(i < _TAB_PROMPT_TILES, i % _TAB_SEQ_TILES, _TAB_SEQ_TILES)


def _rope_pair(u, c, s):
    return u * c + pltpu.roll(u, 64, 1) * s


_Q_SCALE = math.log2(math.e) / math.sqrt(QK_DIM)


def _q_heads_kernel(ql_ref, w_ref, g_ref, c_ref, s_ref, o_ref):
    ql = ql_ref[...]
    g = g_ref[...]
    c = c_ref[...]
    s = s_ref[...]
    for h in range(N_HEADS):
        qf = _dot(ql, w_ref[:, h * HEAD_PAD:(h + 1) * HEAD_PAD])
        ssq = jnp.sum(qf * qf, axis=-1, keepdims=True)
        qn = qf * (lax.rsqrt(ssq * (1.0 / QK_DIM) + EPS) * _Q_SCALE) * g
        o_ref[h, :, :NOPE_DIM] = qn[:, :NOPE_DIM].astype(BF16)
        o_ref[h, :, NOPE_DIM:] = _rope_pair(qn[:, NOPE_DIM:], c, s).astype(BF16)


def _q_heads(q_lat, w_q, g_q, cos_t, sin_t):
    n = q_lat.shape[0]
    return pl.pallas_call(
        _q_heads_kernel,
        grid=(n // ATT_TM,),
        in_specs=[
            pl.BlockSpec((ATT_TM, Q_LORA_RANK), lambda i: (i, 0)),
            pl.BlockSpec((Q_LORA_RANK, N_HEADS * HEAD_PAD), lambda i: (0, 0)),
            pl.BlockSpec((1, HEAD_PAD), lambda i: (0, 0)),
            pl.BlockSpec((ATT_TM, 128), lambda i: (_tab_idx_new(i), 0)),
            pl.BlockSpec((ATT_TM, 128), lambda i: (_tab_idx_new(i), 0)),
        ],
        out_specs=pl.BlockSpec((N_HEADS, ATT_TM, HEAD_PAD), lambda i: (0, i, 0)),
        out_shape=jax.ShapeDtypeStruct((N_HEADS, n, HEAD_PAD), BF16),
        compiler_params=_cparams(1),
        name="q_heads",
    )(q_lat, w_q, g_q, cos_t, sin_t)


def _kv_heads_kernel(kv_ref, kr_ref, w_ref, gr_ref, c_ref, s_ref, k_ref, v_ref):
    kv = kv_ref[...].astype(BF16)
    u = kr_ref[...]
    ssq_r = jnp.sum(u * u, axis=-1, keepdims=True)
    krot = _rope_pair(u * gr_ref[...], c_ref[...], s_ref[...])
    for h in range(N_HEADS):
        z = _dot(kv, w_ref[:, h * HEAD_PAD:(h + 1) * HEAD_PAD])
        kn = z[:, :NOPE_DIM]
        ssq = jnp.sum(kn * kn, axis=-1, keepdims=True) + ssq_r
        scale = lax.rsqrt(ssq * (1.0 / QK_DIM) + EPS)
        k_ref[h, :, :NOPE_DIM] = (kn * scale).astype(BF16)
        k_ref[h, :, NOPE_DIM:] = (krot * scale).astype(BF16)
        v_ref[h] = z[:, NOPE_DIM:].astype(BF16)


def _kv_heads(kv_lat, kr_pad, w_kv, g_kn_rope, cos_t, sin_t, tab_idx, name):
    n = kv_lat.shape[0]
    return pl.pallas_call(
        _kv_heads_kernel,
        grid=(n // ATT_TM,),
        in_specs=[
            pl.BlockSpec((ATT_TM, KV_LORA_RANK), lambda i: (i, 0)),
            pl.BlockSpec((ATT_TM, 128), lambda i: (i, 0)),
            pl.BlockSpec((KV_LORA_RANK, N_HEADS * HEAD_PAD), lambda i: (0, 0)),
            pl.BlockSpec((1, 128), lambda i: (0, 0)),
            pl.BlockSpec((ATT_TM, 128), lambda i: (tab_idx(i), 0)),
            pl.BlockSpec((ATT_TM, 128), lambda i: (tab_idx(i), 0)),
        ],
        out_specs=[
            pl.BlockSpec((N_HEADS, ATT_TM, HEAD_PAD), lambda i: (0, i, 0)),
            pl.BlockSpec((N_HEADS, ATT_TM, V_DIM), lambda i: (0, i, 0)),
        ],
        out_shape=[
            jax.ShapeDtypeStruct((N_HEADS, n, HEAD_PAD), BF16),
            jax.ShapeDtypeStruct((N_HEADS, n, V_DIM), BF16),
        ],
        compiler_params=_cparams(1),
        name=name,
    )(kv_lat, kr_pad, w_kv, g_kn_rope, cos_t, sin_t)


_TQ = 512
_TKB = 512
_HB = 4
_HBP = 4


def _flash_prompt_kernel(q_ref, k_ref, v_ref, o_ref, m_ref, l_ref, acc_ref):
    qi = pl.program_id(2)
    m_ref[...] = jnp.full(m_ref.shape, NEG_INF, F32)
    l_ref[...] = jnp.zeros(l_ref.shape, F32)
    acc_ref[...] = jnp.zeros(acc_ref.shape, F32)
    nlb = _TKB // 128

    def step(ki, masked):
        start = pl.multiple_of(ki * _TKB, _TKB)
        scores = [lax.dot_general(q_ref[hh], k_ref[hh, pl.ds(start, _TKB), :], (((1,), (1,)), ((), ())),
                                  preferred_element_type=F32) for hh in range(_HBP)]
        probs = []
        for hh in range(_HBP):
            s = scores[hh]
            if masked:
                rc = lax.broadcasted_iota(I32, (_TQ, _TKB), 0) // CHUNK
                cc = lax.broadcasted_iota(I32, (_TQ, _TKB), 1) // CHUNK
                s = jnp.where(cc <= rc, s, NEG_INF)
            sb = [s[:, c * 128:(c + 1) * 128] for c in range(nlb)]
            bm = sb[0]
            for c in range(1, nlb):
                bm = jnp.maximum(bm, sb[c])
            m_prev = m_ref[hh]
            m_new = jnp.maximum(m_prev, jnp.max(bm, axis=-1, keepdims=True))
            alpha = jnp.exp2(m_prev - m_new)
            ps = [jnp.exp2(x - m_new) for x in sb]
            psum = ps[0]
            for c in range(1, nlb):
                psum = psum + ps[c]
            l_ref[hh] = alpha * l_ref[hh] + psum
            m_ref[hh] = m_new
            probs.append((alpha, jnp.concatenate(ps, axis=1).astype(BF16)))
        for hh in range(_HBP):
            alpha, p = probs[hh]
            acc_ref[hh] = alpha * acc_ref[hh] + _dot(p, v_ref[hh, pl.ds(start, _TKB), :])

    def body(ki, carry):
        step(ki, False)
        return carry

    lax.fori_loop(0, qi, body, 0)
    step(qi, True)
    for hh in range(_HBP):
        l = jnp.sum(l_ref[hh], axis=-1, keepdims=True)
        o_ref[:, hh * V_DIM:(hh + 1) * V_DIM] = (acc_ref[hh] / l).astype(BF16)


def _flash_prompt(q, k, v):
    nq = SEQ // _TQ
    return pl.pallas_call(
        _flash_prompt_kernel,
        grid=(BATCH, N_HEADS // _HBP, nq),
        in_specs=[
            pl.BlockSpec((_HBP, _TQ, HEAD_PAD), lambda b, h, i: (h, b * nq + i, 0)),
            pl.BlockSpec((_HBP, SEQ, HEAD_PAD), lambda b, h, i: (h, b, 0)),
            pl.BlockSpec((_HBP, SEQ, V_DIM), lambda b, h, i: (h, b, 0)),
        ],
        out_specs=pl.BlockSpec((_TQ, _HBP * V_DIM), lambda b, h, i: (b * nq + i, h)),
        out_shape=jax.ShapeDtypeStruct((N_TOK, N_HEADS * V_DIM), BF16),
        scratch_shapes=[pltpu.VMEM((_HBP, _TQ, 128), F32), pltpu.VMEM((_HBP, _TQ, 128), F32),
                        pltpu.VMEM((_HBP, _TQ, V_DIM), F32)],
        compiler_params=_cparams(3),
        name="flash_prompt",
    )(q, k, v)


_KC_ROWS = 512


def _flash_sample_kernel(prev_ref, q_ref, kv_ref, kr_ref, w_ref, gr_ref, c_ref, s_ref, kn_ref, vn_ref,
                         o_ref, kvb_ref, krot_ref, ssqr_ref, k_ref, v_ref):
    del prev_ref

    @pl.when(pl.program_id(1) == 0)
    def _():
        kvb_ref[...] = kv_ref[...].astype(BF16)
        u = kr_ref[...]
        ssqr_ref[...] = jnp.broadcast_to(jnp.sum(u * u, axis=-1, keepdims=True), ssqr_ref.shape)
        krot_ref[...] = _rope_pair(u * gr_ref[...], c_ref[...], s_ref[...])

    nt = (((1,), (1,)), ((), ()))
    for hh in range(_HB):
        w = w_ref[:, hh * HEAD_PAD:(hh + 1) * HEAD_PAD]
        for r in range(0, PAST_LEN, _KC_ROWS):
            rows = slice(r, r + _KC_ROWS)
            z = _dot(kvb_ref[rows, :], w)
            kn = z[:, :NOPE_DIM]
            ssq = jnp.sum(kn * kn, axis=-1, keepdims=True) + ssqr_ref[rows, :]
            scale = lax.rsqrt(ssq * (1.0 / QK_DIM) + EPS)
            k_ref[rows, :NOPE_DIM] = (kn * scale).astype(BF16)
            k_ref[rows, NOPE_DIM:] = (krot_ref[rows, :] * scale).astype(BF16)
            v_ref[rows, :] = z[:, NOPE_DIM:].astype(BF16)
        q = q_ref[hh]
        s1 = lax.dot_general(q, k_ref[...], nt, preferred_element_type=F32)
        s2 = lax.dot_general(q, kn_ref[hh], nt, preferred_element_type=F32)
        m = jnp.maximum(jnp.max(s1, axis=-1, keepdims=True), jnp.max(s2, axis=-1, keepdims=True))
        p1 = jnp.exp2(s1 - m)
        p2 = jnp.exp2(s2 - m)
        l = jnp.sum(p1, axis=-1, keepdims=True) + jnp.sum(p2, axis=-1, keepdims=True)
        o = _dot(p1.astype(BF16), v_ref[...]) + _dot(p2.astype(BF16), vn_ref[hh])
        o_ref[:, hh * V_DIM:(hh + 1) * V_DIM] = (o / l).astype(BF16)


def _flash_sample(attn, q, cache_kv, cache_kr_pad, w_kv, g_kn_rope, cos_t, sin_t, k_new, v_new):
    assert (PAST_LEN + DEC_SEQ - 1) // CHUNK <= PAST_LEN // CHUNK
    blk0 = N_P // DEC_SEQ
    new = lambda b, h: (h, blk0 + b, 0)
    const = lambda b, h: (0, 0)
    once = pl.Buffered(1)
    return pl.pallas_call(
        _flash_sample_kernel,
        grid=(DEC_BATCH, N_HEADS // _HB),
        in_specs=[
            pl.BlockSpec(memory_space=pl.ANY),
            pl.BlockSpec((_HB, DEC_SEQ, HEAD_PAD), new),
            pl.BlockSpec((PAST_LEN, KV_LORA_RANK), lambda b, h: (b, 0)),
            pl.BlockSpec((PAST_LEN, 128), lambda b, h: (b, 0)),
            pl.BlockSpec((KV_LORA_RANK, _HB * HEAD_PAD), lambda b, h: (0, h)),
            pl.BlockSpec((1, 128), const),
            pl.BlockSpec((PAST_LEN, 128), const, pipeline_mode=once),
            pl.BlockSpec((PAST_LEN, 128), const, pipeline_mode=once),
            pl.BlockSpec((_HB, DEC_SEQ, HEAD_PAD), new),
            pl.BlockSpec((_HB, DEC_SEQ, V_DIM), new),
        ],
        out_specs=pl.BlockSpec((DEC_SEQ, _HB * V_DIM), lambda b, h: (blk0 + b, h)),
        out_shape=jax.ShapeDtypeStruct((N_TOK, N_HEADS * V_DIM), BF16),
        scratch_shapes=[pltpu.VMEM((PAST_LEN, KV_LORA_RANK), BF16),
                        pltpu.VMEM((PAST_LEN, 128), F32),
                        pltpu.VMEM((PAST_LEN, 128), F32),
                        pltpu.VMEM((PAST_LEN, HEAD_PAD), BF16),
                        pltpu.VMEM((PAST_LEN, V_DIM), BF16)],
        input_output_aliases={0: 0},
        compiler_params=_cparams(2),
        name="flash_sample",
    )(attn, q, cache_kv, cache_kr_pad, w_kv, g_kn_rope, cos_t, sin_t, k_new, v_new)


def _merge_kernel(h_ref, c_ref, a_ref, wga_ref, wgb_ref, bga_ref, bgb_ref, wc_ref, wo_ref, o_ref):
    h = h_ref[...]
    ga = _sigmoid(_dot(h, wga_ref[...]) + bga_ref[...])
    gb = _sigmoid(_dot(h, wgb_ref[...]) + bgb_ref[...])
    mix = ga * _dot(c_ref[...], wc_ref[...]) + gb * _dot(a_ref[...], wo_ref[...])
    o_ref[...] = mix.astype(BF16)


def _merge(h, c_act, attn, w_gate, b_gate, w_conv_out, w_o):
    n = h.shape[0]
    tn = 512
    nj = D_MODEL // tn
    g0 = O_U // tn
    row = lambda i, j: (i, 0)
    return pl.pallas_call(
        _merge_kernel,
        grid=(n // TM, nj),
        in_specs=[
            pl.BlockSpec((TM, D_MODEL), row),
            pl.BlockSpec((TM, CONV_CHANNELS), row),
            pl.BlockSpec((TM, N_HEADS * V_DIM), row),
            pl.BlockSpec((D_MODEL, tn), lambda i, j: (0, g0 + j)),
            pl.BlockSpec((D_MODEL, tn), lambda i, j: (0, g0 + j + nj)),
            pl.BlockSpec((1, tn), lambda i, j: (0, j)),
            pl.BlockSpec((1, tn), lambda i, j: (0, j + nj)),
            pl.BlockSpec((CONV_CHANNELS, tn), lambda i, j: (0, j)),
            pl.BlockSpec((N_HEADS * V_DIM, tn), lambda i, j: (0, j)),
        ],
        out_specs=pl.BlockSpec((TM, tn), lambda i, j: (i, j)),
        out_shape=jax.ShapeDtypeStruct((n, D_MODEL), BF16),
        compiler_params=_cparams(2),
        name="merge",
    )(h, c_act, attn, w_gate, w_gate, b_gate, b_gate, w_conv_out, w_o)


def _split_bf16(x):
    hi = x.astype(BF16)
    lo = (x - hi.astype(F32)).astype(BF16)
    return hi, lo


_HALF = D_MODEL // 2


def _pack_bf16_pair(a, b):
    ua = lax.bitcast_convert_type(a.astype(BF16).astype(F32), U32)
    ub = lax.bitcast_convert_type(b.astype(BF16).astype(F32), U32)
    return lax.bitcast_convert_type(ua | (ub >> 16), F32)


def _unpack_bf16_pair(w):
    w = lax.bitcast_convert_type(w, U32)
    a = lax.bitcast_convert_type(w & jnp.uint32(0xFFFF0000), F32).astype(BF16)
    b = lax.bitcast_convert_type(w << 16, F32).astype(BF16)
    return a, b


def _out_router_kernel(n_tiles, n_prompt_tiles, mix_ref, xp_ref, xs_ref, w_ref, g_ref, wrh_ref, wrl_ref, br_ref,
                       x1_ref, hm_ref, idx_ref, gate_ref):
    i = pl.program_id(0)

    @pl.when(i < n_tiles)
    def _():
        x = _stacked_rows(i, n_prompt_tiles, xp_ref, xs_ref)
        _out_router_tile(mix_ref, x, w_ref, g_ref, wrh_ref, wrl_ref, br_ref,
                         x1_ref, hm_ref, idx_ref, gate_ref)

    @pl.when(i >= n_tiles)
    def _():
        hm_ref[...] = jnp.zeros(hm_ref.shape, F32)


def _out_router_tile(mix_ref, x, w_ref, g_ref, wrh_ref, wrl_ref, br_ref,
                     x1_ref, hm_ref, idx_ref, gate_ref):
    x1 = x + _dot(mix_ref[...], w_ref[...])
    x1_ref[...] = x1
    hn = x1 * lax.rsqrt(jnp.mean(x1 * x1, axis=-1, keepdims=True) + EPS) * g_ref[...]
    hm_ref[...] = _pack_bf16_pair(hn[:, :_HALF], hn[:, _HALF:])
    hh, hl = _split_bf16(hn)
    logits = _dot(hh, wrh_ref[...]) + (_dot(hh, wrl_ref[...]) + _dot(hl, wrh_ref[...])) + br_ref[...]
    lane = lax.broadcasted_iota(I32, logits.shape, 1).astype(F32)
    vals = []
    idx_out = jnp.zeros(logits.shape, F32)
    for k in range(TOP_K):
        m = jnp.max(logits, axis=-1, keepdims=True)
        sel = jnp.min(jnp.where(logits == m, lane, 1e9), axis=-1, keepdims=True)
        vals.append(m)
        idx_out = jnp.where(lane == float(k), sel, idx_out)
        logits = jnp.where(lane == sel, -jnp.inf, logits)
    exps = [jnp.exp(v - vals[0]) for v in vals]
    denom = exps[0] + exps[1] + exps[2] + exps[3]
    gate_out = jnp.zeros(idx_out.shape, F32)
    for k in range(TOP_K):
        gate_out = jnp.where(lane == float(k), exps[k] / denom, gate_out)
    idx_ref[...] = idx_out.astype(I32)
    gate_ref[...] = gate_out


def _out_router(mix, xp, xs, w_out, g_ffn, wr_hi, wr_lo, b_r):
    n = N_TOK
    tm = 256
    n_tiles = n // tm
    npt = N_P // tm
    const = lambda i: (0, 0)
    row = lambda i: (jnp.minimum(i, n_tiles - 1), 0)
    return pl.pallas_call(
        functools.partial(_out_router_kernel, n_tiles, npt),
        grid=(2 * n_tiles,),
        in_specs=[
            pl.BlockSpec((tm, D_MODEL), row),
            pl.BlockSpec((tm, D_MODEL), lambda i: (jnp.minimum(i, npt - 1), 0)),
            pl.BlockSpec((tm, D_MODEL), lambda i: (jnp.clip(i - npt, 0, N_S // tm - 1), 0)),
            pl.BlockSpec((D_MODEL, D_MODEL), const),
            pl.BlockSpec((1, D_MODEL), const),
            pl.BlockSpec((D_MODEL, 128), const),
            pl.BlockSpec((D_MODEL, 128), const),
            pl.BlockSpec((1, 128), const),
        ],
        out_specs=[
            pl.BlockSpec((tm, D_MODEL), row),
            pl.BlockSpec((tm, _HALF), lambda i: (i, 0)),
            pl.BlockSpec((tm, 128), row),
            pl.BlockSpec((tm, 128), row),
        ],
        out_shape=[
            jax.ShapeDtypeStruct((n, D_MODEL), F32),
            jax.ShapeDtypeStruct((2 * n, _HALF), F32),
            jax.ShapeDtypeStruct((n, 128), I32),
            jax.ShapeDtypeStruct((n, 128), F32),
        ],
        compiler_params=_cparams(1),
        name="out_router",
    )(mix, xp, xs, w_out, g_ffn, wr_hi, wr_lo, b_r)


_F_VALID, _F_FIRST, _F_NEXT, _F_GROUP0, _F_SLOT = 1, 2, 4, 8, 16


_P_E, _P_W, _P_N, _P_B, _P_BI, _P_NE, _P_NW, _P_FL, _P_SUBS = range(9)
MOE_SUB = 128


def _stream_weights(t, plan_ref, copies, cast):
    flags = plan_ref[_P_FL, t]

    @pl.when((flags & _F_FIRST) != 0)
    def _():
        slot = (flags // _F_SLOT) & 1
        cur = copies(plan_ref[_P_E, t], plan_ref[_P_W, t], slot)

        @pl.when((flags & _F_GROUP0) != 0)
        def _():
            for c in cur:
                c.start()

        for c in cur:
            c.wait()

        @pl.when((flags & _F_NEXT) != 0)
        def _():
            for c in copies(plan_ref[_P_NE, t], plan_ref[_P_NW, t], 1 - slot):
                c.start()

        cast(slot)


def _for_used_rows(valid, subs, rows_body):
    for n_sub in range(1, MOE_BLK // MOE_SUB + 1):
        @pl.when(jnp.logical_and(valid, subs == n_sub))
        def _(m=n_sub * MOE_SUB):
            rows_body(m)


def _moe_up_kernel(plan_ref, prev_ref, x_ref, w_hbm, bg_ref, bu_ref, o_ref, wbuf_ref, wgb_ref, wub_ref, sem_ref):
    del prev_ref
    t = pl.program_id(0)

    def copies(e, w, slot):
        col = pl.multiple_of(w * _UP_TN, _UP_TN)
        return (pltpu.make_async_copy(w_hbm.at[e, :, pl.ds(col, _UP_TN)], wbuf_ref.at[slot, 0], sem_ref.at[slot, 0]),
                pltpu.make_async_copy(w_hbm.at[e, :, pl.ds(col + D_FF, _UP_TN)], wbuf_ref.at[slot, 1],
                                      sem_ref.at[slot, 1]))

    def cast(slot):
        wgb_ref[...] = wbuf_ref[slot, 0].astype(BF16)
        wub_ref[...] = wbuf_ref[slot, 1].astype(BF16)

    _stream_weights(t, plan_ref, copies, cast)
    valid = (plan_ref[_P_FL, t] & _F_VALID) != 0

    def rows_body(m):
        xa, xb = _unpack_bf16_pair(x_ref[:m, :])
        g = _dot(xa, wgb_ref[:_HALF, :]) + _dot(xb, wgb_ref[_HALF:, :]) + bg_ref[0]
        u = _dot(xa, wub_ref[:_HALF, :]) + _dot(xb, wub_ref[_HALF:, :]) + bu_ref[0]
        g = jnp.minimum(g, SWIGLU_LIMIT)
        u = jnp.clip(u, -SWIGLU_LIMIT, SWIGLU_LIMIT)
        o_ref[:m, :] = ((u + 1.0) * (g * _sigmoid(SWIGLU_ALPHA * g))).astype(BF16)
        if m < MOE_BLK:
            o_ref[m:, :] = jnp.zeros((MOE_BLK - m, o_ref.shape[1]), BF16)

    _for_used_rows(valid, plan_ref[_P_SUBS, t], rows_body)

    @pl.when(jnp.logical_not(valid))
    def _():
        o_ref[...] = jnp.zeros(o_ref.shape, BF16)


_UP_TN = 512
_UP_TILES = D_FF // _UP_TN
_DN_TN = 1024
_DN_TILES = D_MODEL // _DN_TN
MOE_CHUNKS = 4
_CHUNK_BLKS = MOE_MAX_BLKS // MOE_CHUNKS


def _moe_up(plan, act_prev, xs, w_gu, b_gu, chunk):
    steps = plan.shape[1]
    blk0 = chunk * _CHUNK_BLKS
    bspec = lambda off: pl.BlockSpec((1, 1, _UP_TN), lambda t, p: (p[_P_E, t], 0, p[_P_W, t] + off))
    aliases = {} if act_prev is None else {1: 0}
    prev = jnp.zeros((8, 128), BF16) if act_prev is None else act_prev
    return pl.pallas_call(
        _moe_up_kernel,
        grid_spec=pltpu.PrefetchScalarGridSpec(
            num_scalar_prefetch=1,
            grid=(steps,),
            in_specs=[
                pl.BlockSpec(memory_space=pl.ANY),
                pl.BlockSpec((MOE_BLK, _HALF), lambda t, p: (p[_P_BI, t], 0)),
                pl.BlockSpec(memory_space=pl.ANY),
                bspec(0), bspec(_UP_TILES),
            ],
            out_specs=pl.BlockSpec((MOE_BLK, _UP_TN),
                                   lambda t, p: (blk0 + p[_P_B, t], p[_P_N, t])),
            scratch_shapes=[pltpu.VMEM((2, 2, D_MODEL, _UP_TN), F32),
                            pltpu.VMEM((D_MODEL, _UP_TN), BF16), pltpu.VMEM((D_MODEL, _UP_TN), BF16),
                            pltpu.SemaphoreType.DMA((2, 2))],
        ),
        out_shape=jax.ShapeDtypeStruct((MOE_ROWS, D_FF), BF16),
        input_output_aliases=aliases,
        compiler_params=_cparams(1),
        name=f"moe_up_{chunk}",
    )(plan, prev, xs, w_gu, b_gu, b_gu)


_DN_HALF = _DN_TN // 2


def _moe_down_kernel(plan_ref, a_ref, w_hbm, b_ref, o_ref, wbuf_ref, wb_ref, sem_ref):
    t = pl.program_id(0)

    def copies(e, w, slot):
        col = pl.multiple_of(w * _DN_TN, _DN_TN)
        return (pltpu.make_async_copy(w_hbm.at[e, :, pl.ds(col, _DN_TN)], wbuf_ref.at[slot], sem_ref.at[slot]),)

    def cast(slot):
        wb_ref[...] = wbuf_ref[slot].astype(BF16)

    _stream_weights(t, plan_ref, copies, cast)
    valid = (plan_ref[_P_FL, t] & _F_VALID) != 0

    def rows_body(m):
        y = _dot(a_ref[:m, :], wb_ref[...]) + b_ref[0]
        o_ref[:m, :] = _pack_bf16_pair(y[:, :_DN_HALF], y[:, _DN_HALF:])
        if m < MOE_BLK:
            o_ref[m:, :] = jnp.zeros((MOE_BLK - m, o_ref.shape[1]), F32)

    _for_used_rows(valid, plan_ref[_P_SUBS, t], rows_body)

    @pl.when(jnp.logical_not(valid))
    def _():
        o_ref[...] = jnp.zeros(o_ref.shape, F32)


def _moe_down(plan, act, w_dn, b_dn):
    steps = plan.shape[1]
    return pl.pallas_call(
        _moe_down_kernel,
        grid_spec=pltpu.PrefetchScalarGridSpec(
            num_scalar_prefetch=1,
            grid=(steps,),
            in_specs=[
                pl.BlockSpec((MOE_BLK, D_FF), lambda t, p: (p[_P_BI, t], 0)),
                pl.BlockSpec(memory_space=pl.ANY),
                pl.BlockSpec((1, 1, _DN_TN), lambda t, p: (p[_P_E, t], 0, p[_P_W, t])),
            ],
            out_specs=pl.BlockSpec((MOE_BLK, _DN_HALF), lambda t, p: (p[_P_B, t], p[_P_N, t])),
            scratch_shapes=[pltpu.VMEM((2, D_FF, _DN_TN), F32), pltpu.VMEM((D_FF, _DN_TN), BF16),
                            pltpu.SemaphoreType.DMA((2,))],
        ),
        out_shape=jax.ShapeDtypeStruct((MOE_ROWS, _HALF), F32),
        compiler_params=_cparams(1),
        name="moe_down",
    )(plan, act, w_dn, b_dn)


def _moe_dispatch(top_idx):
    n_asg = N_TOK * TOP_K
    flat_e = top_idx.reshape(-1)
    onehot = (flat_e[:, None] == jnp.arange(N_EXPERTS, dtype=I32)[None, :]).astype(I32)
    csum = jnp.cumsum(onehot, axis=0)
    counts = csum[-1]
    rank = jnp.sum(csum * onehot, axis=1) - 1
    nblk = (counts + MOE_BLK - 1) // MOE_BLK
    blk_start = jnp.cumsum(nblk) - nblk
    dest = jnp.sum(onehot * blk_start[None, :], axis=1) * MOE_BLK + rank
    pad_src = jnp.arange(MOE_ROWS, dtype=I32) % N_TOK
    row_tok = pad_src.at[dest].set(jnp.arange(n_asg, dtype=I32) // TOP_K,
                                   mode="promise_in_bounds", unique_indices=True)
    return dest, row_tok, counts, nblk, blk_start


def _moe_steps(counts, nblk, blk_start, n_tiles, blk_lo, n_blks):
    t_max = n_tiles * n_blks
    lo = jnp.clip(blk_start, blk_lo, blk_lo + n_blks)
    hi = jnp.clip(blk_start + nblk, blk_lo, blk_lo + n_blks)
    nb_e = hi - lo
    per_e = nb_e * n_tiles
    s_end = jnp.cumsum(per_e)
    total = s_end[-1]
    t = jnp.arange(t_max, dtype=I32)
    tc = jnp.clip(t, 0, jnp.maximum(total - 1, 0))
    e = jnp.minimum(jnp.sum((s_end[None, :] <= tc[:, None]).astype(I32), axis=1), N_EXPERTS - 1)
    sel = (e[:, None] == jnp.arange(N_EXPERTS, dtype=I32)[None, :]).astype(I32)
    pick = lambda v: jnp.sum(sel * v[None, :], axis=1)
    local = tc - pick(s_end - per_e)
    nb = jnp.maximum(pick(nb_e), 1)
    w_tile = jnp.clip(local // nb, 0, n_tiles - 1)
    r = local % nb
    valid = t < total
    first = jnp.logical_and(valid, r == 0)
    fill = t - total
    blk = jnp.where(valid, pick(lo) - blk_lo + r, total // n_tiles + fill // n_tiles)
    rows_used = pick(counts) - (pick(lo) + r - pick(blk_start)) * MOE_BLK
    subs = jnp.clip((rows_used + MOE_SUB - 1) // MOE_SUB, 1, MOE_BLK // MOE_SUB)
    o_tile = jnp.where(valid, w_tile, fill % n_tiles)
    blk = jnp.clip(blk, 0, n_blks - 1)
    blk_in = jnp.where(valid, blk, jnp.maximum(total // n_tiles - 1, 0))
    ids = jnp.arange(N_EXPERTS, dtype=I32)
    owners = jnp.where(nb_e > 0, ids, N_EXPERTS)
    later = jnp.flip(lax.cummin(jnp.flip(owners)))
    next_owner = pick(jnp.concatenate([later[1:], jnp.full((1,), N_EXPERTS, I32)]))
    last_tile = w_tile == n_tiles - 1
    next_e = jnp.where(last_tile, next_owner, e)
    next_w = jnp.where(last_tile, 0, w_tile + 1)
    has_next = jnp.logical_and(first, next_e < N_EXPERTS)
    group = jnp.cumsum(first.astype(I32)) - 1
    flags = (valid * _F_VALID + first * _F_FIRST + has_next * _F_NEXT
             + jnp.logical_and(first, group == 0) * _F_GROUP0 + (group % 2) * _F_SLOT)
    rows = {_P_E: e, _P_W: w_tile, _P_N: o_tile, _P_B: blk, _P_BI: blk_in,
            _P_NE: jnp.minimum(next_e, N_EXPERTS - 1), _P_NW: next_w, _P_FL: flags, _P_SUBS: subs}
    return jnp.stack([rows[k].astype(I32) for k in range(len(rows))])


def _moe_plans(counts, nblk, blk_start, n_tiles, n_chunks, n_blks):
    los = jnp.arange(n_chunks, dtype=I32) * n_blks
    return jax.vmap(lambda lo: _moe_steps(counts, nblk, blk_start, n_tiles, lo, n_blks))(los)


_FIN_TM = 256
_FIN_TN = 512
FIN_CHUNKS = 4


def _unpack_expert_rows(words):
    u = lax.bitcast_convert_type(words, U32)
    hi = lax.bitcast_convert_type(u & jnp.uint32(0xFFFF0000), F32)
    lo = lax.bitcast_convert_type(u << 16, F32)
    parts = []
    for n in range(_DN_TILES):
        cols = slice(n * _DN_HALF, (n + 1) * _DN_HALF)
        parts += [hi[:, cols], lo[:, cols]]
    return jnp.concatenate(parts, axis=1)


def _final_kernel(prev_ref, x1_ref, y0_ref, y1_ref, y2_ref, y3_ref, gate_ref, g_ref, wg_ref, p_ref, wp_ref,
                  o_ref, x2_ref):
    del prev_ref
    gate = gate_ref[...]
    moe = (_unpack_expert_rows(y0_ref[0]) * gate[:, 0:1] + _unpack_expert_rows(y1_ref[0]) * gate[:, 1:2]
           + _unpack_expert_rows(y2_ref[0]) * gate[:, 2:3] + _unpack_expert_rows(y3_ref[0]) * gate[:, 3:4])
    x2 = x1_ref[...] + moe
    x2_ref[...] = x2
    hp = (x2 * lax.rsqrt(jnp.mean(x2 * x2, axis=-1, keepdims=True) + EPS) * g_ref[...]).astype(BF16)
    pb = p_ref[...].astype(BF16)
    for c in range(0, D_MODEL, _FIN_TN):
        cols = slice(c, c + _FIN_TN)
        emb = _dot(pb, wp_ref[:, cols])
        o_ref[:, cols] = x2_ref[:, cols] + _sigmoid(_dot(hp, wg_ref[:, cols])) * emb


def _final(out_prev, x1, y4, gate, g_ple, w_ple_gate, p, w_ple, tok0, out0, n, n_out, name):
    t0 = tok0 // _FIN_TM
    o0 = out0 // _FIN_TM
    pt0 = out0 // _FIN_TM
    const = lambda i: (0, 0)
    yspec = lambda k: pl.BlockSpec((1, _FIN_TM, _HALF), lambda i: (k, i, 0))
    once = pl.Buffered(1)
    aliases = {} if out_prev is None else {0: 0}
    prev = jnp.zeros((8, 128), F32) if out_prev is None else out_prev
    return pl.pallas_call(
        _final_kernel,
        grid=(n // _FIN_TM,),
        in_specs=[
            pl.BlockSpec(memory_space=pl.ANY),
            pl.BlockSpec((_FIN_TM, D_MODEL), lambda i: (t0 + i, 0)),
            yspec(0), yspec(1), yspec(2), yspec(3),
            pl.BlockSpec((_FIN_TM, 128), lambda i: (t0 + i, 0)),
            pl.BlockSpec((1, D_MODEL), const),
            pl.BlockSpec((D_MODEL, D_MODEL), const, pipeline_mode=once),
            pl.BlockSpec((_FIN_TM, PLE_DIM), lambda i: (pt0 + i, 0)),
            pl.BlockSpec((PLE_DIM, D_MODEL), const, pipeline_mode=once),
        ],
        out_specs=pl.BlockSpec((_FIN_TM, D_MODEL), lambda i: (o0 + i, 0)),
        out_shape=jax.ShapeDtypeStruct((n_out, D_MODEL), F32),
        scratch_shapes=[pltpu.VMEM((_FIN_TM, D_MODEL), F32)],
        input_output_aliases=aliases,
        compiler_params=_cparams(1),
        name=name,
    )(prev, x1, y4, y4, y4, y4, gate, g_ple, w_ple_gate, p, w_ple)


def _rope_layout(x):
    half = ROPE_DIM // 2
    z = jnp.zeros(x.shape[:-1] + (half,), x.dtype)
    return jnp.concatenate([x[..., :half], z, x[..., half:], z], axis=-1)


def _rope_tables():
    half = ROPE_DIM // 2
    inv_freq = ROPE_THETA ** (-jnp.arange(half, dtype=F32) / half)
    pos = jnp.arange(PAST_LEN + DEC_SEQ, dtype=I32)
    ang = pos.astype(F32)[:, None] * inv_freq[None, :]
    cos, sin = jnp.cos(ang), jnp.sin(ang)
    z = jnp.zeros_like(cos)
    c = jnp.concatenate([cos, z, cos, z], axis=-1)
    s = jnp.concatenate([-sin, z, sin, z], axis=-1)
    rep = ATT_TM // DEC_SEQ
    return (jnp.concatenate([c[:SEQ], jnp.tile(c[PAST_LEN:], (rep, 1))], axis=0),
            jnp.concatenate([s[:SEQ], jnp.tile(s[PAST_LEN:], (rep, 1))], axis=0))


def _layer(xp, xs, p_prompt, p_sample, cache_kv, cache_kr, state_conv,
           g_mix, w_in, b_gate, w_dw, b_dw, g_cn, b_cn, w_conv_out,
           g_qa, g_kva, w_qb, w_kb, w_vb, g_qn, g_kn, w_o, w_out,
           g_ffn, w_router, b_router, w_gu, b_gu, w_dn, b_dn,
           g_ple, w_ple_gate, w_ple):
    assert SEQ == PAST_LEN
    row = lambda v: v.reshape(1, -1)
    w_in_b = jnp.concatenate([w_in[:, :O_U], w_in[:, O_KR:]], axis=1).astype(BF16)
    w_mid = jnp.concatenate([w_in[:, O_U:O_KV], _rope_layout(w_in[:, O_KV:O_KR])], axis=1).astype(BF16)

    h, q_lat, kv_new, kr_pad = _in_mid(xp, xs, row(g_mix), w_mid, row(g_qa), row(g_kva))
    half = ROPE_DIM // 2
    kr_new = jnp.concatenate([kr_pad[:, :half], kr_pad[:, 2 * half:3 * half]], axis=1)
    glu = _in_glu(h, w_in_b)

    hist = jnp.concatenate([jnp.zeros((BATCH, HALO, CONV_CHANNELS), F32),
                            jnp.pad(state_conv, ((0, 0), (HALO - (CONV_WIDTH - 1), 0), (0, 0)))], axis=0)
    c_act = _conv_module(glu, hist, w_dw, row(b_dw), row(g_cn), row(b_cn))

    cos_t, sin_t = _rope_tables()
    w_q = jnp.concatenate([w_qb[..., :NOPE_DIM], _rope_layout(w_qb[..., NOPE_DIM:])], axis=-1)
    w_q = w_q.reshape(Q_LORA_RANK, N_HEADS * HEAD_PAD).astype(BF16)
    g_q = jnp.concatenate([g_qn[:NOPE_DIM] * g_kn[:NOPE_DIM], _rope_layout(g_qn[NOPE_DIM:])]).reshape(1, HEAD_PAD)
    q = _q_heads(q_lat, w_q, g_q, cos_t, sin_t)

    w_kv = jnp.concatenate([w_kb, w_vb], axis=-1).reshape(KV_LORA_RANK, N_HEADS * HEAD_PAD).astype(BF16)
    g_kn_rope = _rope_layout(g_kn[NOPE_DIM:]).reshape(1, 128)
    k_new, v_new = _kv_heads(kv_new, kr_pad, w_kv, g_kn_rope, cos_t, sin_t, _tab_idx_new, "kv_heads_new")
    attn = _flash_prompt(q, k_new, v_new)
    attn = _flash_sample(attn, q, cache_kv.reshape(DEC_BATCH * PAST_LEN, KV_LORA_RANK),
                         _rope_layout(cache_kr).reshape(DEC_BATCH * PAST_LEN, 128),
                         w_kv, g_kn_rope, cos_t, sin_t, k_new, v_new)

    mix = _merge(h, c_act, attn, w_in_b, row(b_gate), w_conv_out.astype(BF16), w_o.astype(BF16))

    wr = jnp.pad(w_router, ((0, 0), (0, 128 - N_EXPERTS)))
    wr_hi, wr_lo = _split_bf16(wr)
    b_r = jnp.concatenate([b_router, jnp.full((128 - N_EXPERTS,), -jnp.inf, F32)]).reshape(1, 128)
    x1, hm, idx_pad, gate_pad = _out_router(mix, xp, xs, w_out.astype(BF16), row(g_ffn), wr_hi, wr_lo, b_r)

    top_idx = idx_pad[:, :TOP_K]
    dest, row_tok, counts, nblk, blk_start = _moe_dispatch(top_idx)
    b_gu3 = b_gu.reshape(N_EXPERTS, 1, 2 * D_FF)
    chunk_rows = _CHUNK_BLKS * MOE_BLK
    up_plans = _moe_plans(counts, nblk, blk_start, _UP_TILES, MOE_CHUNKS, _CHUNK_BLKS)
    down_plan = _moe_plans(counts, nblk, blk_start, _DN_TILES, 1, MOE_MAX_BLKS)[0]
    act = None
    for c in range(MOE_CHUNKS):
        xs = hm.at[row_tok[c * chunk_rows:(c + 1) * chunk_rows]].get(mode="promise_in_bounds")
        act = _moe_up(up_plans[c], act, xs, w_gu, b_gu3, c)
    ys = _moe_down(down_plan, act, w_dn, b_dn.reshape(N_EXPERTS, 1, D_MODEL))

    dest_t = dest.reshape(N_TOK, TOP_K).T
    fin = (row(g_ple), w_ple_gate.astype(BF16))
    w_ple_b = w_ple.astype(BF16)
    n_c = N_P // FIN_CHUNKS
    out_p = None
    for c in range(FIN_CHUNKS):
        y4 = ys.at[dest_t[:, c * n_c:(c + 1) * n_c]].get(mode="promise_in_bounds")
        out_p = _final(out_p, x1, y4, gate_pad, *fin, p_prompt, w_ple_b, c * n_c, c * n_c, n_c, N_P,
                       f"final_prompt_{c}")
    y4 = ys.at[dest_t[:, N_P:]].get(mode="promise_in_bounds")
    out_s = _final(None, x1, y4, gate_pad, *fin, p_sample, w_ple_b, N_P, 0, N_S, N_S, "final_sample")
    return out_p, out_s, kv_new, kr_new, glu


def kernel(x_prompt, x_sample, cache_kv_latent, cache_k_rope, state_conv, p_prompt, p_sample, g_mix, w_in, b_gate, w_dw, b_dw, g_cn, b_cn, w_conv_out, g_qa, g_kva, w_qb, w_kb, w_vb, g_qn, g_kn, w_o, w_out, g_ffn, w_router, b_router, w_gu, b_gu, w_dn, b_dn, g_ple, w_ple_gate, w_ple):
    assert g_mix.shape[0] == 1
    out_p, out_s, kv_new, kr_new, glu = _layer(
        x_prompt.reshape(N_P, D_MODEL), x_sample.reshape(N_S, D_MODEL),
        p_prompt[0].reshape(N_P, PLE_DIM), p_sample[0].reshape(N_S, PLE_DIM),
        cache_kv_latent[0], cache_k_rope[0], state_conv[0],
        g_mix[0], w_in[0], b_gate[0], w_dw[0], b_dw[0], g_cn[0], b_cn[0], w_conv_out[0],
        g_qa[0], g_kva[0], w_qb[0], w_kb[0], w_vb[0], g_qn[0], g_kn[0], w_o[0], w_out[0],
        g_ffn[0], w_router[0], b_router[0], w_gu[0], b_gu[0], w_dn[0], b_dn[0],
        g_ple[0], w_ple_gate[0], w_ple[0])
    tail = CONV_WIDTH - 1
    conv_p = jnp.stack([glu[(b + 1) * SEQ - tail:(b + 1) * SEQ] for b in range(BATCH)])
    conv_s = glu[N_P:].reshape(DEC_BATCH, DEC_SEQ, CONV_CHANNELS)[:, DEC_SEQ - tail:]
    return (out_p.reshape(BATCH, SEQ, D_MODEL),
            out_s.reshape(DEC_BATCH, DEC_SEQ, D_MODEL),
            kv_new[:N_P].reshape(1, BATCH, SEQ, KV_LORA_RANK),
            kr_new[:N_P].reshape(1, BATCH, SEQ, ROPE_DIM),
            conv_p[None],
            kv_new[N_P:].reshape(1, DEC_BATCH, DEC_SEQ, KV_LORA_RANK),
            kr_new[N_P:].reshape(1, DEC_BATCH, DEC_SEQ, ROPE_DIM),
            conv_s[None])
```

```python
import functools
import math

import jax
import jax.numpy as jnp
from jax import lax
from jax.experimental import pallas as pl
from jax.experimental.pallas import tpu as pltpu

F32 = jnp.float32
BF16 = jnp.bfloat16
I32 = jnp.int32
U32 = jnp.uint32

D_MODEL = 2048
BATCH = 2
SEQ = 4096
DEC_BATCH = 8
DEC_SEQ = 64
PAST_LEN = 4096
CHUNK = 64
CONV_CHANNELS = D_MODEL
CONV_WIDTH = 31
N_HEADS = 16
Q_LORA_RANK = 512
KV_LORA_RANK = 512
NOPE_DIM = 128
ROPE_DIM = 64
QK_DIM = NOPE_DIM + ROPE_DIM
V_DIM = 128
ROPE_THETA = 10000.0
N_EXPERTS = 32
TOP_K = 4
D_FF = D_MODEL
SWIGLU_ALPHA = 1.702
SWIGLU_LIMIT = 7.0
PLE_DIM = 256
EPS = 1e-6
NEG_INF = -1e30

N_P = BATCH * SEQ
N_S = DEC_BATCH * DEC_SEQ
N_TOK = N_P + N_S
O_U = 2 * CONV_CHANNELS
O_Q = O_U + Q_LORA_RANK
O_KV = O_Q + KV_LORA_RANK
O_KR = O_KV + ROPE_DIM
MID_W = 1152
HEAD_PAD = 256

TM = 512
CONV_T = 64
HALO = 32
MOE_BLK = 512
MOE_MAX_BLKS = (N_TOK * TOP_K) // MOE_BLK + N_EXPERTS
MOE_ROWS = MOE_MAX_BLKS * MOE_BLK
VMEM_LIMIT = 48 * 1024 * 1024


def _cparams(n_axes):
    return pltpu.CompilerParams(dimension_semantics=("arbitrary",) * n_axes,
                                vmem_limit_bytes=VMEM_LIMIT)


def _sigmoid(x):
    return 1.0 / (1.0 + jnp.exp(-x))


def _dot(a, b):
    return jnp.dot(a, b, preferred_element_type=F32)


def _stacked_rows(i, n_prompt_tiles, xp_ref, xs_ref):
    return jnp.where(i < n_prompt_tiles, xp_ref[...], xs_ref[...])


def _in_mid_kernel(xp_ref, xs_ref, g_ref, w_ref, gqa_ref, gkva_ref, h_ref, q_ref, kvp_ref, kvs_ref, kr_ref):
    i = pl.program_id(0)
    x = _stacked_rows(i, N_P // TM, xp_ref, xs_ref)
    h = x * lax.rsqrt(jnp.mean(x * x, axis=-1, keepdims=True) + EPS) * g_ref[...]
    hb = h.astype(BF16)
    h_ref[...] = hb
    z = _dot(hb, w_ref[...])
    ql = z[:, :Q_LORA_RANK]
    kvl = z[:, Q_LORA_RANK:Q_LORA_RANK + KV_LORA_RANK]
    qn = ql * lax.rsqrt(jnp.mean(ql * ql, axis=-1, keepdims=True) + EPS) * gqa_ref[...]
    q_ref[...] = qn.astype(BF16)
    kv = kvl * lax.rsqrt(jnp.mean(kvl * kvl, axis=-1, keepdims=True) + EPS) * gkva_ref[...]
    kr_ref[...] = z[:, Q_LORA_RANK + KV_LORA_RANK:]

    @pl.when(i < N_P // TM)
    def _():
        kvp_ref[...] = kv

    @pl.when(i >= N_P // TM)
    def _():
        kvs_ref[...] = kv


def _in_mid(xp, xs, g_mix, w_mid, g_qa, g_kva):
    n = N_TOK
    npt = N_P // TM
    return pl.pallas_call(
        _in_mid_kernel,
        grid=(n // TM,),
        in_specs=[
            pl.BlockSpec((TM, D_MODEL), lambda i: (jnp.minimum(i, npt - 1), 0)),
            pl.BlockSpec((TM, D_MODEL), lambda i: (jnp.maximum(i - npt, 0), 0)),
            pl.BlockSpec((1, D_MODEL), lambda i: (0, 0)),
            pl.BlockSpec((D_MODEL, MID_W), lambda i: (0, 0)),
            pl.BlockSpec((1, Q_LORA_RANK), lambda i: (0, 0)),
            pl.BlockSpec((1, KV_LORA_RANK), lambda i: (0, 0)),
        ],
        out_specs=[
            pl.BlockSpec((TM, D_MODEL), lambda i: (i, 0)),
            pl.BlockSpec((TM, Q_LORA_RANK), lambda i: (i, 0)),
            pl.BlockSpec((TM, KV_LORA_RANK), lambda i: (jnp.minimum(i, npt - 1), 0)),
            pl.BlockSpec((TM, KV_LORA_RANK), lambda i: (jnp.maximum(i - npt, 0), 0)),
            pl.BlockSpec((TM, 128), lambda i: (i, 0)),
        ],
        out_shape=[
            jax.ShapeDtypeStruct((n, D_MODEL), BF16),
            jax.ShapeDtypeStruct((n, Q_LORA_RANK), BF16),
            jax.ShapeDtypeStruct((N_P, KV_LORA_RANK), F32),
            jax.ShapeDtypeStruct((N_S, KV_LORA_RANK), F32),
            jax.ShapeDtypeStruct((n, 128), F32),
        ],
        compiler_params=_cparams(1),
        name="in_mid",
    )(xp, xs, g_mix, w_mid, g_qa, g_kva)


def _glu_kernel(h_ref, w1_ref, w2_ref, o_ref, w1b_ref, w2b_ref):
    @pl.when(pl.program_id(1) == 0)
    def _():
        w1b_ref[...] = w1_ref[...].astype(BF16)
        w2b_ref[...] = w2_ref[...].astype(BF16)

    h = h_ref[...]
    o_ref[...] = _dot(h, w1b_ref[...]) * _sigmoid(_dot(h, w2b_ref[...]))


def _in_glu(h, w_in):
    n = h.shape[0]
    tn = 1024
    nj = CONV_CHANNELS // tn
    once = pl.Buffered(1)
    return pl.pallas_call(
        _glu_kernel,
        grid=(nj, n // TM),
        in_specs=[
            pl.BlockSpec((TM, D_MODEL), lambda j, i: (i, 0)),
            pl.BlockSpec((D_MODEL, tn), lambda j, i: (0, j), pipeline_mode=once),
            pl.BlockSpec((D_MODEL, tn), lambda j, i: (0, j + nj), pipeline_mode=once),
        ],
        out_specs=pl.BlockSpec((TM, tn), lambda j, i: (i, j)),
        out_shape=jax.ShapeDtypeStruct((n, CONV_CHANNELS), F32),
        scratch_shapes=[pltpu.VMEM((D_MODEL, tn), BF16), pltpu.VMEM((D_MODEL, tn), BF16)],
        compiler_params=_cparams(2),
        name="in_glu",
    )(h, w_in, w_in)


_CONV_TILES_PER_SEQ = SEQ // CONV_T
_CONV_PROMPT_TILES = N_P // CONV_T
_CONV_LANES = 512
SUBLANES = 8
_SHIFT_ROWS = (HALO // SUBLANES - 1) * SUBLANES + CONV_T


def _conv_kernel(cur_ref, prev_ref, hist_ref, w_ref, bdw_ref, g_ref, b_ref, o_ref, win_ref, conv_ref, shift_ref):
    i = pl.program_id(0)
    first = jnp.logical_or(i >= _CONV_PROMPT_TILES, i % _CONV_TILES_PER_SEQ == 0)

    @pl.when(first)
    def _():
        win_ref[0:HALO, :] = hist_ref[0]

    @pl.when(jnp.logical_not(first))
    def _():
        win_ref[0:HALO, :] = prev_ref[...]

    win_ref[HALO:HALO + CONV_T, :] = cur_ref[...]
    for r in range(1, SUBLANES):
        shift_ref[r - 1] = win_ref[r:r + _SHIFT_ROWS, :]
    base = HALO - (CONV_WIDTH - 1)
    for c in range(0, CONV_CHANNELS, _CONV_LANES):
        acc = jnp.zeros((CONV_T, _CONV_LANES), F32)
        for k in range(CONV_WIDTH):
            q, r = divmod(base + k, SUBLANES)
            lanes = slice(c, c + _CONV_LANES)
            rows = slice(q * SUBLANES, q * SUBLANES + CONV_T)
            src = win_ref[rows, lanes] if r == 0 else shift_ref[r - 1, rows, lanes]
            acc = acc + w_ref[k:k + 1, lanes] * src
        conv_ref[:, c:c + _CONV_LANES] = acc + bdw_ref[:, c:c + _CONV_LANES]
    y = conv_ref[...]
    yc = y - jnp.mean(y, axis=-1, keepdims=True)
    var = jnp.mean(yc * yc, axis=-1, keepdims=True)
    z = yc * lax.rsqrt(var + EPS) * g_ref[...] + b_ref[...]
    o_ref[...] = (z * _sigmoid(z)).astype(BF16)


def _conv_module(glu, hist, w_dw, b_dw, g_cn, b_cn):
    n = glu.shape[0]
    n_tiles = n // CONV_T
    halo_per_tile = CONV_T // HALO

    def seq_of(i):
        return jnp.where(i < _CONV_PROMPT_TILES, i // _CONV_TILES_PER_SEQ, i - _CONV_PROMPT_TILES + BATCH)

    return pl.pallas_call(
        _conv_kernel,
        grid=(n_tiles,),
        in_specs=[
            pl.BlockSpec((CONV_T, CONV_CHANNELS), lambda i: (i, 0)),
            pl.BlockSpec((HALO, CONV_CHANNELS), lambda i: (jnp.maximum(i * halo_per_tile - 1, 0), 0)),
            pl.BlockSpec((1, HALO, CONV_CHANNELS), lambda i: (seq_of(i), 0, 0)),
            pl.BlockSpec((CONV_WIDTH, CONV_CHANNELS), lambda i: (0, 0)),
            pl.BlockSpec((1, CONV_CHANNELS), lambda i: (0, 0)),
            pl.BlockSpec((1, CONV_CHANNELS), lambda i: (0, 0)),
            pl.BlockSpec((1, CONV_CHANNELS), lambda i: (0, 0)),
        ],
        out_specs=pl.BlockSpec((CONV_T, CONV_CHANNELS), lambda i: (i, 0)),
        out_shape=jax.ShapeDtypeStruct((n, CONV_CHANNELS), BF16),
        scratch_shapes=[pltpu.VMEM((HALO + CONV_T, CONV_CHANNELS), F32),
                        pltpu.VMEM((CONV_T, CONV_CHANNELS), F32),
                        pltpu.VMEM((SUBLANES - 1, _SHIFT_ROWS, CONV_CHANNELS), F32)],
        compiler_params=_cparams(1),
        name="conv_module",
    )(glu, glu, hist, w_dw, b_dw, g_cn, b_cn)


ATT_TM = 512
_TAB_PROMPT_TILES = N_P // ATT_TM
_TAB_SEQ_TILES = SEQ // ATT_TM
_TAB_ROWS = SEQ + ATT_TM


def _tab_idx_new(i):
    return jnp.where(i < _TAB_PROMPT_TILES, i % _TAB_SEQ_TILES, _TAB_SEQ_TILES)


def _rope_pair(u, c, s):
    return u * c + pltpu.roll(u, 64, 1) * s


_Q_SCALE = math.log2(math.e) / math.sqrt(QK_DIM)


def _q_heads_kernel(ql_ref, w_ref, g_ref, c_ref, s_ref, o_ref):
    ql = ql_ref[...]
    g = g_ref[...]
    c = c_ref[...]
    s = s_ref[...]
    for h in range(N_HEADS):
        qf = _dot(ql, w_ref[:, h * HEAD_PAD:(h + 1) * HEAD_PAD])
        ssq = jnp.sum(qf * qf, axis=-1, keepdims=True)
        qn = qf * (lax.rsqrt(ssq * (1.0 / QK_DIM) + EPS) * _Q_SCALE) * g
        o_ref[h, :, :NOPE_DIM] = qn[:, :NOPE_DIM].astype(BF16)
        o_ref[h, :, NOPE_DIM:] = _rope_pair(qn[:, NOPE_DIM:], c, s).astype(BF16)


def _q_heads(q_lat, w_q, g_q, cos_t, sin_t):
    n = q_lat.shape[0]
    return pl.pallas_call(
        _q_heads_kernel,
        grid=(n // ATT_TM,),
        in_specs=[
            pl.BlockSpec((ATT_TM, Q_LORA_RANK), lambda i: (i, 0)),
            pl.BlockSpec((Q_LORA_RANK, N_HEADS * HEAD_PAD), lambda i: (0, 0)),
            pl.BlockSpec((1, HEAD_PAD), lambda i: (0, 0)),
            pl.BlockSpec((ATT_TM, 128), lambda i: (_tab_idx_new(i), 0)),
            pl.BlockSpec((ATT_TM, 128), lambda i: (_tab_idx_new(i), 0)),
        ],
        out_specs=pl.BlockSpec((N_HEADS, ATT_TM, HEAD_PAD), lambda i: (0, i, 0)),
        out_shape=jax.ShapeDtypeStruct((N_HEADS, n, HEAD_PAD), BF16),
        compiler_params=_cparams(1),
        name="q_heads",
    )(q_lat, w_q, g_q, cos_t, sin_t)


def _kv_heads_kernel(kvp_ref, kvs_ref, kr_ref, w_ref, gr_ref, c_ref, s_ref, k_ref, v_ref):
    kv = _stacked_rows(pl.program_id(0), N_P // ATT_TM, kvp_ref, kvs_ref).astype(BF16)
    u = kr_ref[...]
    ssq_r = jnp.sum(u * u, axis=-1, keepdims=True)
    krot = _rope_pair(u * gr_ref[...], c_ref[...], s_ref[...])
    for h in range(N_HEADS):
        z = _dot(kv, w_ref[:, h * HEAD_PAD:(h + 1) * HEAD_PAD])
        kn = z[:, :NOPE_DIM]
        ssq = jnp.sum(kn * kn, axis=-1, keepdims=True) + ssq_r
        scale = lax.rsqrt(ssq * (1.0 / QK_DIM) + EPS)
        k_ref[h, :, :NOPE_DIM] = (kn * scale).astype(BF16)
        k_ref[h, :, NOPE_DIM:] = (krot * scale).astype(BF16)
        v_ref[h] = z[:, NOPE_DIM:].astype(BF16)


def _kv_heads(kv_p, kv_s, kr_pad, w_kv, g_kn_rope, cos_t, sin_t, tab_idx, name):
    n = N_TOK
    npt = N_P // ATT_TM
    return pl.pallas_call(
        _kv_heads_kernel,
        grid=(n // ATT_TM,),
        in_specs=[
            pl.BlockSpec((ATT_TM, KV_LORA_RANK), lambda i: (jnp.minimum(i, npt - 1), 0)),
            pl.BlockSpec((ATT_TM, KV_LORA_RANK), lambda i: (jnp.maximum(i - npt, 0), 0)),
            pl.BlockSpec((ATT_TM, 128), lambda i: (i, 0)),
            pl.BlockSpec((KV_LORA_RANK, N_HEADS * HEAD_PAD), lambda i: (0, 0)),
            pl.BlockSpec((1, 128), lambda i: (0, 0)),
            pl.BlockSpec((ATT_TM, 128), lambda i: (tab_idx(i), 0)),
            pl.BlockSpec((ATT_TM, 128), lambda i: (tab_idx(i), 0)),
        ],
        out_specs=[
            pl.BlockSpec((N_HEADS, ATT_TM, HEAD_PAD), lambda i: (0, i, 0)),
            pl.BlockSpec((N_HEADS, ATT_TM, V_DIM), lambda i: (0, i, 0)),
        ],
        out_shape=[
            jax.ShapeDtypeStruct((N_HEADS, n, HEAD_PAD), BF16),
            jax.ShapeDtypeStruct((N_HEADS, n, V_DIM), BF16),
        ],
        compiler_params=_cparams(1),
        name=name,
    )(kv_p, kv_s, kr_pad, w_kv, g_kn_rope, cos_t, sin_t)


_TQ = 512
_TKB = 512
_HB = 4
_HBP = 4


def _flash_prompt_kernel(q_ref, k_ref, v_ref, o_ref, m_ref, l_ref, acc_ref):
    qi = pl.program_id(2)
    m_ref[...] = jnp.full(m_ref.shape, NEG_INF, F32)
    l_ref[...] = jnp.zeros(l_ref.shape, F32)
    acc_ref[...] = jnp.zeros(acc_ref.shape, F32)
    nlb = _TKB // 128

    def step(ki, masked):
        start = pl.multiple_of(ki * _TKB, _TKB)
        scores = [lax.dot_general(q_ref[hh], k_ref[hh, pl.ds(start, _TKB), :], (((1,), (1,)), ((), ())),
                                  preferred_element_type=F32) for hh in range(_HBP)]
        probs = []
        for hh in range(_HBP):
            s = scores[hh]
            if masked:
                rc = lax.broadcasted_iota(I32, (_TQ, _TKB), 0) // CHUNK
                cc = lax.broadcasted_iota(I32, (_TQ, _TKB), 1) // CHUNK
                s = jnp.where(cc <= rc, s, NEG_INF)
            sb = [s[:, c * 128:(c + 1) * 128] for c in range(nlb)]
            bm = sb[0]
            for c in range(1, nlb):
                bm = jnp.maximum(bm, sb[c])
            m_prev = m_ref[hh]
            m_new = jnp.maximum(m_prev, jnp.max(bm, axis=-1, keepdims=True))
            alpha = jnp.exp2(m_prev - m_new)
            ps = [jnp.exp2(x - m_new) for x in sb]
            psum = ps[0]
            for c in range(1, nlb):
                psum = psum + ps[c]
            l_ref[hh] = alpha * l_ref[hh] + psum
            m_ref[hh] = m_new
            probs.append((alpha, jnp.concatenate(ps, axis=1).astype(BF16)))
        for hh in range(_HBP):
            alpha, p = probs[hh]
            acc_ref[hh] = alpha * acc_ref[hh] + _dot(p, v_ref[hh, pl.ds(start, _TKB), :])

    def body(ki, carry):
        step(ki, False)
        return carry

    lax.fori_loop(0, qi, body, 0)
    step(qi, True)
    for hh in range(_HBP):
        l = jnp.sum(l_ref[hh], axis=-1, keepdims=True)
        o_ref[:, hh * V_DIM:(hh + 1) * V_DIM] = (acc_ref[hh] / l).astype(BF16)


def _flash_prompt(q, k, v):
    nq = SEQ // _TQ
    return pl.pallas_call(
        _flash_prompt_kernel,
        grid=(BATCH, N_HEADS // _HBP, nq),
        in_specs=[
            pl.BlockSpec((_HBP, _TQ, HEAD_PAD), lambda b, h, i: (h, b * nq + i, 0)),
            pl.BlockSpec((_HBP, SEQ, HEAD_PAD), lambda b, h, i: (h, b, 0)),
            pl.BlockSpec((_HBP, SEQ, V_DIM), lambda b, h, i: (h, b, 0)),
        ],
        out_specs=pl.BlockSpec((_TQ, _HBP * V_DIM), lambda b, h, i: (b * nq + i, h)),
        out_shape=jax.ShapeDtypeStruct((N_TOK, N_HEADS * V_DIM), BF16),
        scratch_shapes=[pltpu.VMEM((_HBP, _TQ, 128), F32), pltpu.VMEM((_HBP, _TQ, 128), F32),
                        pltpu.VMEM((_HBP, _TQ, V_DIM), F32)],
        compiler_params=_cparams(3),
        name="flash_prompt",
    )(q, k, v)


_KC_ROWS = 512


def _flash_sample_kernel(prev_ref, q_ref, kv_ref, kr_ref, w_ref, gr_ref, c_ref, s_ref, kn_ref, vn_ref,
                         o_ref, kvb_ref, krot_ref, ssqr_ref, k_ref, v_ref):
    del prev_ref

    @pl.when(pl.program_id(1) == 0)
    def _():
        kvb_ref[...] = kv_ref[...].astype(BF16)
        u = kr_ref[...]
        ssqr_ref[...] = jnp.broadcast_to(jnp.sum(u * u, axis=-1, keepdims=True), ssqr_ref.shape)
        krot_ref[...] = _rope_pair(u * gr_ref[...], c_ref[...], s_ref[...])

    nt = (((1,), (1,)), ((), ()))
    for hh in range(_HB):
        w = w_ref[:, hh * HEAD_PAD:(hh + 1) * HEAD_PAD]
        for r in range(0, PAST_LEN, _KC_ROWS):
            rows = slice(r, r + _KC_ROWS)
            z = _dot(kvb_ref[rows, :], w)
            kn = z[:, :NOPE_DIM]
            ssq = jnp.sum(kn * kn, axis=-1, keepdims=True) + ssqr_ref[rows, :]
            scale = lax.rsqrt(ssq * (1.0 / QK_DIM) + EPS)
            k_ref[rows, :NOPE_DIM] = (kn * scale).astype(BF16)
            k_ref[rows, NOPE_DIM:] = (krot_ref[rows, :] * scale).astype(BF16)
            v_ref[rows, :] = z[:, NOPE_DIM:].astype(BF16)
        q = q_ref[hh]
        s1 = lax.dot_general(q, k_ref[...], nt, preferred_element_type=F32)
        s2 = lax.dot_general(q, kn_ref[hh], nt, preferred_element_type=F32)
        m = jnp.maximum(jnp.max(s1, axis=-1, keepdims=True), jnp.max(s2, axis=-1, keepdims=True))
        p1 = jnp.exp2(s1 - m)
        p2 = jnp.exp2(s2 - m)
        l = jnp.sum(p1, axis=-1, keepdims=True) + jnp.sum(p2, axis=-1, keepdims=True)
        o = _dot(p1.astype(BF16), v_ref[...]) + _dot(p2.astype(BF16), vn_ref[hh])
        o_ref[:, hh * V_DIM:(hh + 1) * V_DIM] = (o / l).astype(BF16)


def _flash_sample(attn, q, cache_kv, cache_kr_pad, w_kv, g_kn_rope, cos_t, sin_t, k_new, v_new):
    assert (PAST_LEN + DEC_SEQ - 1) // CHUNK <= PAST_LEN // CHUNK
    blk0 = N_P // DEC_SEQ
    new = lambda b, h: (h, blk0 + b, 0)
    const = lambda b, h: (0, 0)
    once = pl.Buffered(1)
    return pl.pallas_call(
        _flash_sample_kernel,
        grid=(DEC_BATCH, N_HEADS // _HB),
        in_specs=[
            pl.BlockSpec(memory_space=pl.ANY),
            pl.BlockSpec((_HB, DEC_SEQ, HEAD_PAD), new),
            pl.BlockSpec((PAST_LEN, KV_LORA_RANK), lambda b, h: (b, 0)),
            pl.BlockSpec((PAST_LEN, 128), lambda b, h: (b, 0)),
            pl.BlockSpec((KV_LORA_RANK, _HB * HEAD_PAD), lambda b, h: (0, h)),
            pl.BlockSpec((1, 128), const),
            pl.BlockSpec((PAST_LEN, 128), const, pipeline_mode=once),
            pl.BlockSpec((PAST_LEN, 128), const, pipeline_mode=once),
            pl.BlockSpec((_HB, DEC_SEQ, HEAD_PAD), new),
            pl.BlockSpec((_HB, DEC_SEQ, V_DIM), new),
        ],
        out_specs=pl.BlockSpec((DEC_SEQ, _HB * V_DIM), lambda b, h: (blk0 + b, h)),
        out_shape=jax.ShapeDtypeStruct((N_TOK, N_HEADS * V_DIM), BF16),
        scratch_shapes=[pltpu.VMEM((PAST_LEN, KV_LORA_RANK), BF16),
                        pltpu.VMEM((PAST_LEN, 128), F32),
                        pltpu.VMEM((PAST_LEN, 128), F32),
                        pltpu.VMEM((PAST_LEN, HEAD_PAD), BF16),
                        pltpu.VMEM((PAST_LEN, V_DIM), BF16)],
        input_output_aliases={0: 0},
        compiler_params=_cparams(2),
        name="flash_sample",
    )(attn, q, cache_kv, cache_kr_pad, w_kv, g_kn_rope, cos_t, sin_t, k_new, v_new)


def _merge_kernel(h_ref, c_ref, a_ref, wga_ref, wgb_ref, bga_ref, bgb_ref, wc_ref, wo_ref, o_ref):
    h = h_ref[...]
    ga = _sigmoid(_dot(h, wga_ref[...]) + bga_ref[...])
    gb = _sigmoid(_dot(h, wgb_ref[...]) + bgb_ref[...])
    mix = ga * _dot(c_ref[...], wc_ref[...]) + gb * _dot(a_ref[...], wo_ref[...])
    o_ref[...] = mix.astype(BF16)


def _merge(h, c_act, attn, w_gate, b_gate, w_conv_out, w_o):
    n = h.shape[0]
    tn = 512
    nj = D_MODEL // tn
    row = lambda i, j: (i, 0)
    return pl.pallas_call(
        _merge_kernel,
        grid=(n // TM, nj),
        in_specs=[
            pl.BlockSpec((TM, D_MODEL), row),
            pl.BlockSpec((TM, CONV_CHANNELS), row),
            pl.BlockSpec((TM, N_HEADS * V_DIM), row),
            pl.BlockSpec((D_MODEL, tn), lambda i, j: (0, j)),
            pl.BlockSpec((D_MODEL, tn), lambda i, j: (0, j + nj)),
            pl.BlockSpec((1, tn), lambda i, j: (0, j)),
            pl.BlockSpec((1, tn), lambda i, j: (0, j + nj)),
            pl.BlockSpec((CONV_CHANNELS, tn), lambda i, j: (0, j)),
            pl.BlockSpec((N_HEADS * V_DIM, tn), lambda i, j: (0, j)),
        ],
        out_specs=pl.BlockSpec((TM, tn), lambda i, j: (i, j)),
        out_shape=jax.ShapeDtypeStruct((n, D_MODEL), BF16),
        compiler_params=_cparams(2),
        name="merge",
    )(h, c_act, attn, w_gate, w_gate, b_gate, b_gate, w_conv_out, w_o)


def _split_bf16(x):
    hi = x.astype(BF16)
    lo = (x - hi.astype(F32)).astype(BF16)
    return hi, lo


_HALF = D_MODEL // 2


def _pack_bf16_pair(a, b):
    ua = lax.bitcast_convert_type(a.astype(BF16).astype(F32), U32)
    ub = lax.bitcast_convert_type(b.astype(BF16).astype(F32), U32)
    return lax.bitcast_convert_type(ua | (ub >> 16), F32)


def _unpack_bf16_pair(w):
    w = lax.bitcast_convert_type(w, U32)
    a = lax.bitcast_convert_type(w & jnp.uint32(0xFFFF0000), F32).astype(BF16)
    b = lax.bitcast_convert_type(w << 16, F32).astype(BF16)
    return a, b


def _out_router_kernel(n_tiles, n_prompt_tiles, mix_ref, xp_ref, xs_ref, w_ref, g_ref, wrh_ref, wrl_ref, br_ref,
                       x1_ref, hm_ref, idx_ref, gate_ref):
    i = pl.program_id(0)

    @pl.when(i < n_tiles)
    def _():
        x = _stacked_rows(i, n_prompt_tiles, xp_ref, xs_ref)
        _out_router_tile(mix_ref, x, w_ref, g_ref, wrh_ref, wrl_ref, br_ref,
                         x1_ref, hm_ref, idx_ref, gate_ref)

    @pl.when(i >= n_tiles)
    def _():
        hm_ref[...] = jnp.zeros(hm_ref.shape, F32)


def _out_router_tile(mix_ref, x, w_ref, g_ref, wrh_ref, wrl_ref, br_ref,
                     x1_ref, hm_ref, idx_ref, gate_ref):
    x1 = x + _dot(mix_ref[...], w_ref[...])
    x1_ref[...] = x1
    hn = x1 * lax.rsqrt(jnp.mean(x1 * x1, axis=-1, keepdims=True) + EPS) * g_ref[...]
    hm_ref[...] = _pack_bf16_pair(hn[:, :_HALF], hn[:, _HALF:])
    hh, hl = _split_bf16(hn)
    logits = _dot(hh, wrh_ref[...]) + (_dot(hh, wrl_ref[...]) + _dot(hl, wrh_ref[...])) + br_ref[...]
    lane = lax.broadcasted_iota(I32, logits.shape, 1).astype(F32)
    vals = []
    idx_out = jnp.zeros(logits.shape, F32)
    for k in range(TOP_K):
        m = jnp.max(logits, axis=-1, keepdims=True)
        sel = jnp.min(jnp.where(logits == m, lane, 1e9), axis=-1, keepdims=True)
        vals.append(m)
        idx_out = jnp.where(lane == float(k), sel, idx_out)
        logits = jnp.where(lane == sel, -jnp.inf, logits)
    exps = [jnp.exp(v - vals[0]) for v in vals]
    denom = exps[0] + exps[1] + exps[2] + exps[3]
    gate_out = jnp.zeros(idx_out.shape, F32)
    for k in range(TOP_K):
        gate_out = jnp.where(lane == float(k), exps[k] / denom, gate_out)
    idx_ref[...] = idx_out.astype(I32)
    gate_ref[...] = gate_out


def _out_router(mix, xp, xs, w_out, g_ffn, wr_hi, wr_lo, b_r):
    n = N_TOK
    tm = 256
    n_tiles = n // tm
    npt = N_P // tm
    const = lambda i: (0, 0)
    row = lambda i: (jnp.minimum(i, n_tiles - 1), 0)
    return pl.pallas_call(
        functools.partial(_out_router_kernel, n_tiles, npt),
        grid=(2 * n_tiles,),
        in_specs=[
            pl.BlockSpec((tm, D_MODEL), row),
            pl.BlockSpec((tm, D_MODEL), lambda i: (jnp.minimum(i, npt - 1), 0)),
            pl.BlockSpec((tm, D_MODEL), lambda i: (jnp.clip(i - npt, 0, N_S // tm - 1), 0)),
            pl.BlockSpec((D_MODEL, D_MODEL), const),
            pl.BlockSpec((1, D_MODEL), const),
            pl.BlockSpec((D_MODEL, 128), const),
            pl.BlockSpec((D_MODEL, 128), const),
            pl.BlockSpec((1, 128), const),
        ],
        out_specs=[
            pl.BlockSpec((tm, D_MODEL), row),
            pl.BlockSpec((tm, _HALF), lambda i: (i, 0)),
            pl.BlockSpec((tm, 128), row),
            pl.BlockSpec((tm, 128), row),
        ],
        out_shape=[
            jax.ShapeDtypeStruct((n, D_MODEL), F32),
            jax.ShapeDtypeStruct((2 * n, _HALF), F32),
            jax.ShapeDtypeStruct((n, 128), I32),
            jax.ShapeDtypeStruct((n, 128), F32),
        ],
        compiler_params=_cparams(1),
        name="out_router",
    )(mix, xp, xs, w_out, g_ffn, wr_hi, wr_lo, b_r)


_F_VALID, _F_FIRST, _F_NEXT, _F_GROUP0, _F_SLOT = 1, 2, 4, 8, 16


_P_E, _P_W, _P_N, _P_B, _P_BI, _P_NE, _P_NW, _P_FL, _P_SUBS = range(9)
MOE_SUB = 128


def _stream_weights(t, plan_ref, copies, cast):
    flags = plan_ref[_P_FL, t]

    @pl.when((flags & _F_FIRST) != 0)
    def _():
        slot = (flags // _F_SLOT) & 1
        cur = copies(plan_ref[_P_E, t], plan_ref[_P_W, t], slot)

        @pl.when((flags & _F_GROUP0) != 0)
        def _():
            for c in cur:
                c.start()

        for c in cur:
            c.wait()

        @pl.when((flags & _F_NEXT) != 0)
        def _():
            for c in copies(plan_ref[_P_NE, t], plan_ref[_P_NW, t], 1 - slot):
                c.start()

        cast(slot)


def _for_used_rows(valid, subs, rows_body):
    for n_sub in range(1, MOE_BLK // MOE_SUB + 1):
        @pl.when(jnp.logical_and(valid, subs == n_sub))
        def _(m=n_sub * MOE_SUB):
            rows_body(m)


def _moe_up_kernel(plan_ref, prev_ref, x_ref, w_hbm, bg_ref, bu_ref, o_ref, wbuf_ref, wgb_ref, wub_ref, sem_ref):
    del prev_ref
    t = pl.program_id(0)

    def copies(e, w, slot):
        col = pl.multiple_of(w * _UP_TN, _UP_TN)
        return (pltpu.make_async_copy(w_hbm.at[e, :, pl.ds(col, _UP_TN)], wbuf_ref.at[slot, 0], sem_ref.at[slot, 0]),
                pltpu.make_async_copy(w_hbm.at[e, :, pl.ds(col + D_FF, _UP_TN)], wbuf_ref.at[slot, 1],
                                      sem_ref.at[slot, 1]))

    def cast(slot):
        wgb_ref[...] = wbuf_ref[slot, 0].astype(BF16)
        wub_ref[...] = wbuf_ref[slot, 1].astype(BF16)

    _stream_weights(t, plan_ref, copies, cast)
    valid = (plan_ref[_P_FL, t] & _F_VALID) != 0

    def rows_body(m):
        xa, xb = _unpack_bf16_pair(x_ref[:m, :])
        g = _dot(xa, wgb_ref[:_HALF, :]) + _dot(xb, wgb_ref[_HALF:, :]) + bg_ref[0]
        u = _dot(xa, wub_ref[:_HALF, :]) + _dot(xb, wub_ref[_HALF:, :]) + bu_ref[0]
        g = jnp.minimum(g, SWIGLU_LIMIT)
        u = jnp.clip(u, -SWIGLU_LIMIT, SWIGLU_LIMIT)
        o_ref[:m, :] = ((u + 1.0) * (g * _sigmoid(SWIGLU_ALPHA * g))).astype(BF16)
        if m < MOE_BLK:
            o_ref[m:, :] = jnp.zeros((MOE_BLK - m, o_ref.shape[1]), BF16)

    _for_used_rows(valid, plan_ref[_P_SUBS, t], rows_body)

    @pl.when(jnp.logical_not(valid))
    def _():
        o_ref[...] = jnp.zeros(o_ref.shape, BF16)


_UP_TN = 512
_UP_TILES = D_FF // _UP_TN
_DN_TN = 1024
_DN_TILES = D_MODEL // _DN_TN
MOE_CHUNKS = 4
_CHUNK_BLKS = MOE_MAX_BLKS // MOE_CHUNKS


def _moe_up(plan, act_prev, xs, w_gu, b_gu, chunk):
    steps = plan.shape[1]
    blk0 = chunk * _CHUNK_BLKS
    bspec = lambda off: pl.BlockSpec((1, 1, _UP_TN), lambda t, p: (p[_P_E, t], 0, p[_P_W, t] + off))
    aliases = {} if act_prev is None else {1: 0}
    prev = jnp.zeros((8, 128), BF16) if act_prev is None else act_prev
    return pl.pallas_call(
        _moe_up_kernel,
        grid_spec=pltpu.PrefetchScalarGridSpec(
            num_scalar_prefetch=1,
            grid=(steps,),
            in_specs=[
                pl.BlockSpec(memory_space=pl.ANY),
                pl.BlockSpec((MOE_BLK, _HALF), lambda t, p: (p[_P_BI, t], 0)),
                pl.BlockSpec(memory_space=pl.ANY),
                bspec(0), bspec(_UP_TILES),
            ],
            out_specs=pl.BlockSpec((MOE_BLK, _UP_TN),
                                   lambda t, p: (blk0 + p[_P_B, t], p[_P_N, t])),
            scratch_shapes=[pltpu.VMEM((2, 2, D_MODEL, _UP_TN), F32),
                            pltpu.VMEM((D_MODEL, _UP_TN), BF16), pltpu.VMEM((D_MODEL, _UP_TN), BF16),
                            pltpu.SemaphoreType.DMA((2, 2))],
        ),
        out_shape=jax.ShapeDtypeStruct((MOE_ROWS, D_FF), BF16),
        input_output_aliases=aliases,
        compiler_params=_cparams(1),
        name=f"moe_up_{chunk}",
    )(plan, prev, xs, w_gu, b_gu, b_gu)


_DN_HALF = _DN_TN // 2


def _moe_down_kernel(plan_ref, a_ref, w_hbm, b_ref, o_ref, wbuf_ref, wb_ref, sem_ref):
    t = pl.program_id(0)

    def copies(e, w, slot):
        col = pl.multiple_of(w * _DN_TN, _DN_TN)
        return (pltpu.make_async_copy(w_hbm.at[e, :, pl.ds(col, _DN_TN)], wbuf_ref.at[slot], sem_ref.at[slot]),)

    def cast(slot):
        wb_ref[...] = wbuf_ref[slot].astype(BF16)

    _stream_weights(t, plan_ref, copies, cast)
    valid = (plan_ref[_P_FL, t] & _F_VALID) != 0

    def rows_body(m):
        y = _dot(a_ref[:m, :], wb_ref[...]) + b_ref[0]
        o_ref[:m, :] = _pack_bf16_pair(y[:, :_DN_HALF], y[:, _DN_HALF:])
        if m < MOE_BLK:
            o_ref[m:, :] = jnp.zeros((MOE_BLK - m, o_ref.shape[1]), F32)

    _for_used_rows(valid, plan_ref[_P_SUBS, t], rows_body)

    @pl.when(jnp.logical_not(valid))
    def _():
        o_ref[...] = jnp.zeros(o_ref.shape, F32)


def _moe_down(plan, act, w_dn, b_dn):
    steps = plan.shape[1]
    return pl.pallas_call(
        _moe_down_kernel,
        grid_spec=pltpu.PrefetchScalarGridSpec(
            num_scalar_prefetch=1,
            grid=(steps,),
            in_specs=[
                pl.BlockSpec((MOE_BLK, D_FF), lambda t, p: (p[_P_BI, t], 0)),
                pl.BlockSpec(memory_space=pl.ANY),
                pl.BlockSpec((1, 1, _DN_TN), lambda t, p: (p[_P_E, t], 0, p[_P_W, t])),
            ],
            out_specs=pl.BlockSpec((MOE_BLK, _DN_HALF), lambda t, p: (p[_P_B, t], p[_P_N, t])),
            scratch_shapes=[pltpu.VMEM((2, D_FF, _DN_TN), F32), pltpu.VMEM((D_FF, _DN_TN), BF16),
                            pltpu.SemaphoreType.DMA((2,))],
        ),
        out_shape=jax.ShapeDtypeStruct((MOE_ROWS, _HALF), F32),
        compiler_params=_cparams(1),
        name="moe_down",
    )(plan, act, w_dn, b_dn)


def _moe_dispatch(top_idx):
    n_asg = N_TOK * TOP_K
    flat_e = top_idx.reshape(-1)
    onehot = (flat_e[:, None] == jnp.arange(N_EXPERTS, dtype=I32)[None, :]).astype(I32)
    csum = jnp.cumsum(onehot, axis=0)
    counts = csum[-1]
    rank = jnp.sum(csum * onehot, axis=1) - 1
    nblk = (counts + MOE_BLK - 1) // MOE_BLK
    blk_start = jnp.cumsum(nblk) - nblk
    dest = jnp.sum(onehot * blk_start[None, :], axis=1) * MOE_BLK + rank
    pad_src = jnp.arange(MOE_ROWS, dtype=I32) % N_TOK
    row_tok = pad_src.at[dest].set(jnp.arange(n_asg, dtype=I32) // TOP_K,
                                   mode="promise_in_bounds", unique_indices=True)
    return dest, row_tok, counts, nblk, blk_start


def _moe_steps(counts, nblk, blk_start, n_tiles, blk_lo, n_blks):
    t_max = n_tiles * n_blks
    lo = jnp.clip(blk_start, blk_lo, blk_lo + n_blks)
    hi = jnp.clip(blk_start + nblk, blk_lo, blk_lo + n_blks)
    nb_e = hi - lo
    per_e = nb_e * n_tiles
    s_end = jnp.cumsum(per_e)
    total = s_end[-1]
    t = jnp.arange(t_max, dtype=I32)
    tc = jnp.clip(t, 0, jnp.maximum(total - 1, 0))
    e = jnp.minimum(jnp.sum((s_end[None, :] <= tc[:, None]).astype(I32), axis=1), N_EXPERTS - 1)
    sel = (e[:, None] == jnp.arange(N_EXPERTS, dtype=I32)[None, :]).astype(I32)
    pick = lambda v: jnp.sum(sel * v[None, :], axis=1)
    local = tc - pick(s_end - per_e)
    nb = jnp.maximum(pick(nb_e), 1)
    w_tile = jnp.clip(local // nb, 0, n_tiles - 1)
    r = local % nb
    valid = t < total
    first = jnp.logical_and(valid, r == 0)
    fill = t - total
    blk = jnp.where(valid, pick(lo) - blk_lo + r, total // n_tiles + fill // n_tiles)
    rows_used = pick(counts) - (pick(lo) + r - pick(blk_start)) * MOE_BLK
    subs = jnp.clip((rows_used + MOE_SUB - 1) // MOE_SUB, 1, MOE_BLK // MOE_SUB)
    o_tile = jnp.where(valid, w_tile, fill % n_tiles)
    blk = jnp.clip(blk, 0, n_blks - 1)
    blk_in = jnp.where(valid, blk, jnp.maximum(total // n_tiles - 1, 0))
    ids = jnp.arange(N_EXPERTS, dtype=I32)
    owners = jnp.where(nb_e > 0, ids, N_EXPERTS)
    later = jnp.flip(lax.cummin(jnp.flip(owners)))
    next_owner = pick(jnp.concatenate([later[1:], jnp.full((1,), N_EXPERTS, I32)]))
    last_tile = w_tile == n_tiles - 1
    next_e = jnp.where(last_tile, next_owner, e)
    next_w = jnp.where(last_tile, 0, w_tile + 1)
    has_next = jnp.logical_and(first, next_e < N_EXPERTS)
    group = jnp.cumsum(first.astype(I32)) - 1
    flags = (valid * _F_VALID + first * _F_FIRST + has_next * _F_NEXT
             + jnp.logical_and(first, group == 0) * _F_GROUP0 + (group % 2) * _F_SLOT)
    rows = {_P_E: e, _P_W: w_tile, _P_N: o_tile, _P_B: blk, _P_BI: blk_in,
            _P_NE: jnp.minimum(next_e, N_EXPERTS - 1), _P_NW: next_w, _P_FL: flags, _P_SUBS: subs}
    return jnp.stack([rows[k].astype(I32) for k in range(len(rows))])


def _moe_plans(counts, nblk, blk_start, n_tiles, n_chunks, n_blks):
    los = jnp.arange(n_chunks, dtype=I32) * n_blks
    return jax.vmap(lambda lo: _moe_steps(counts, nblk, blk_start, n_tiles, lo, n_blks))(los)


_FIN_TM = 256
_FIN_TN = 512
FIN_CHUNKS = 4


def _unpack_expert_rows(words):
    u = lax.bitcast_convert_type(words, U32)
    hi = lax.bitcast_convert_type(u & jnp.uint32(0xFFFF0000), F32)
    lo = lax.bitcast_convert_type(u << 16, F32)
    parts = []
    for n in range(_DN_TILES):
        cols = slice(n * _DN_HALF, (n + 1) * _DN_HALF)
        parts += [hi[:, cols], lo[:, cols]]
    return jnp.concatenate(parts, axis=1)


def _final_kernel(prev_ref, x1_ref, y0_ref, y1_ref, y2_ref, y3_ref, gate_ref, g_ref, wg_ref, p_ref, wp_ref,
                  o_ref, x2_ref):
    del prev_ref
    gate = gate_ref[...]
    moe = (_unpack_expert_rows(y0_ref[0]) * gate[:, 0:1] + _unpack_expert_rows(y1_ref[0]) * gate[:, 1:2]
           + _unpack_expert_rows(y2_ref[0]) * gate[:, 2:3] + _unpack_expert_rows(y3_ref[0]) * gate[:, 3:4])
    x2 = x1_ref[...] + moe
    x2_ref[...] = x2
    hp = (x2 * lax.rsqrt(jnp.mean(x2 * x2, axis=-1, keepdims=True) + EPS) * g_ref[...]).astype(BF16)
    pb = p_ref[...].astype(BF16)
    for c in range(0, D_MODEL, _FIN_TN):
        cols = slice(c, c + _FIN_TN)
        emb = _dot(pb, wp_ref[:, cols])
        o_ref[:, cols] = x2_ref[:, cols] + _sigmoid(_dot(hp, wg_ref[:, cols])) * emb


def _final(out_prev, x1, y4, gate, g_ple, w_ple_gate, p, w_ple, tok0, out0, n, n_out, name):
    t0 = tok0 // _FIN_TM
    o0 = out0 // _FIN_TM
    pt0 = out0 // _FIN_TM
    const = lambda i: (0, 0)
    yspec = lambda k: pl.BlockSpec((1, _FIN_TM, _HALF), lambda i: (k, i, 0))
    once = pl.Buffered(1)
    aliases = {} if out_prev is None else {0: 0}
    prev = jnp.zeros((8, 128), F32) if out_prev is None else out_prev
    return pl.pallas_call(
        _final_kernel,
        grid=(n // _FIN_TM,),
        in_specs=[
            pl.BlockSpec(memory_space=pl.ANY),
            pl.BlockSpec((_FIN_TM, D_MODEL), lambda i: (t0 + i, 0)),
            yspec(0), yspec(1), yspec(2), yspec(3),
            pl.BlockSpec((_FIN_TM, 128), lambda i: (t0 + i, 0)),
            pl.BlockSpec((1, D_MODEL), const),
            pl.BlockSpec((D_MODEL, D_MODEL), const, pipeline_mode=once),
            pl.BlockSpec((_FIN_TM, PLE_DIM), lambda i: (pt0 + i, 0)),
            pl.BlockSpec((PLE_DIM, D_MODEL), const, pipeline_mode=once),
        ],
        out_specs=pl.BlockSpec((_FIN_TM, D_MODEL), lambda i: (o0 + i, 0)),
        out_shape=jax.ShapeDtypeStruct((n_out, D_MODEL), F32),
        scratch_shapes=[pltpu.VMEM((_FIN_TM, D_MODEL), F32)],
        input_output_aliases=aliases,
        compiler_params=_cparams(1),
        name=name,
    )(prev, x1, y4, y4, y4, y4, gate, g_ple, w_ple_gate, p, w_ple)


def _rope_layout(x):
    half = ROPE_DIM // 2
    z = jnp.zeros(x.shape[:-1] + (half,), x.dtype)
    return jnp.concatenate([x[..., :half], z, x[..., half:], z], axis=-1)


def _rope_tables():
    half = ROPE_DIM // 2
    inv_freq = ROPE_THETA ** (-jnp.arange(half, dtype=F32) / half)
    pos = jnp.arange(PAST_LEN + DEC_SEQ, dtype=I32)
    ang = pos.astype(F32)[:, None] * inv_freq[None, :]
    cos, sin = jnp.cos(ang), jnp.sin(ang)
    z = jnp.zeros_like(cos)
    c = jnp.concatenate([cos, z, cos, z], axis=-1)
    s = jnp.concatenate([-sin, z, sin, z], axis=-1)
    rep = ATT_TM // DEC_SEQ
    return (jnp.concatenate([c[:SEQ], jnp.tile(c[PAST_LEN:], (rep, 1))], axis=0),
            jnp.concatenate([s[:SEQ], jnp.tile(s[PAST_LEN:], (rep, 1))], axis=0))


def _layer(xp, xs, p_prompt, p_sample, cache_kv, cache_kr, state_conv,
           g_mix, w_in, b_gate, w_dw, b_dw, g_cn, b_cn, w_conv_out,
           g_qa, g_kva, w_qb, w_kb, w_vb, g_qn, g_kn, w_o, w_out,
           g_ffn, w_router, b_router, w_gu, b_gu, w_dn, b_dn,
           g_ple, w_ple_gate, w_ple):
    assert SEQ == PAST_LEN
    row = lambda v: v.reshape(1, -1)
    w_gate = w_in[:, O_KR:].astype(BF16)
    w_mid = jnp.concatenate([w_in[:, O_U:O_KV], _rope_layout(w_in[:, O_KV:O_KR])], axis=1).astype(BF16)

    h, q_lat, kv_p, kv_s, kr_pad = _in_mid(xp, xs, row(g_mix), w_mid, row(g_qa), row(g_kva))
    half = ROPE_DIM // 2
    kr_new = jnp.concatenate([kr_pad[:, :half], kr_pad[:, 2 * half:3 * half]], axis=1)
    glu = _in_glu(h, w_in)

    hist = jnp.concatenate([jnp.zeros((BATCH, HALO, CONV_CHANNELS), F32),
                            jnp.pad(state_conv, ((0, 0), (HALO - (CONV_WIDTH - 1), 0), (0, 0)))], axis=0)
    c_act = _conv_module(glu, hist, w_dw, row(b_dw), row(g_cn), row(b_cn))

    cos_t, sin_t = _rope_tables()
    w_q = jnp.concatenate([w_qb[..., :NOPE_DIM], _rope_layout(w_qb[..., NOPE_DIM:])], axis=-1)
    w_q = w_q.reshape(Q_LORA_RANK, N_HEADS * HEAD_PAD).astype(BF16)
    g_q = jnp.concatenate([g_qn[:NOPE_DIM] * g_kn[:NOPE_DIM], _rope_layout(g_qn[NOPE_DIM:])]).reshape(1, HEAD_PAD)
    q = _q_heads(q_lat, w_q, g_q, cos_t, sin_t)

    w_kv = jnp.concatenate([w_kb, w_vb], axis=-1).reshape(KV_LORA_RANK, N_HEADS * HEAD_PAD).astype(BF16)
    g_kn_rope = _rope_layout(g_kn[NOPE_DIM:]).reshape(1, 128)
    k_new, v_new = _kv_heads(kv_p, kv_s, kr_pad, w_kv, g_kn_rope, cos_t, sin_t, _tab_idx_new, "kv_heads_new")
    attn = _flash_prompt(q, k_new, v_new)
    attn = _flash_sample(attn, q, cache_kv.reshape(DEC_BATCH * PAST_LEN, KV_LORA_RANK),
                         _rope_layout(cache_kr).reshape(DEC_BATCH * PAST_LEN, 128),
                         w_kv, g_kn_rope, cos_t, sin_t, k_new, v_new)

    mix = _merge(h, c_act, attn, w_gate, row(b_gate), w_conv_out.astype(BF16), w_o.astype(BF16))

    wr = jnp.pad(w_router, ((0, 0), (0, 128 - N_EXPERTS)))
    wr_hi, wr_lo = _split_bf16(wr)
    b_r = jnp.concatenate([b_router, jnp.full((128 - N_EXPERTS,), -jnp.inf, F32)]).reshape(1, 128)
    x1, hm, idx_pad, gate_pad = _out_router(mix, xp, xs, w_out.astype(BF16), row(g_ffn), wr_hi, wr_lo, b_r)

    top_idx = idx_pad[:, :TOP_K]
    dest, row_tok, counts, nblk, blk_start = _moe_dispatch(top_idx)
    b_gu3 = b_gu.reshape(N_EXPERTS, 1, 2 * D_FF)
    chunk_rows = _CHUNK_BLKS * MOE_BLK
    up_plans = _moe_plans(counts, nblk, blk_start, _UP_TILES, MOE_CHUNKS, _CHUNK_BLKS)
    down_plan = _moe_plans(counts, nblk, blk_start, _DN_TILES, 1, MOE_MAX_BLKS)[0]
    act = None
    for c in range(MOE_CHUNKS):
        xs = hm.at[row_tok[c * chunk_rows:(c + 1) * chunk_rows]].get(mode="promise_in_bounds")
        act = _moe_up(up_plans[c], act, xs, w_gu, b_gu3, c)
    ys = _moe_down(down_plan, act, w_dn, b_dn.reshape(N_EXPERTS, 1, D_MODEL))

    dest_t = dest.reshape(N_TOK, TOP_K).T
    fin = (row(g_ple), w_ple_gate.astype(BF16))
    w_ple_b = w_ple.astype(BF16)
    n_c = N_P // FIN_CHUNKS
    out_p = None
    for c in range(FIN_CHUNKS):
        y4 = ys.at[dest_t[:, c * n_c:(c + 1) * n_c]].get(mode="promise_in_bounds")
        out_p = _final(out_p, x1, y4, gate_pad, *fin, p_prompt, w_ple_b, c * n_c, c * n_c, n_c, N_P,
                       f"final_prompt_{c}")
    y4 = ys.at[dest_t[:, N_P:]].get(mode="promise_in_bounds")
    out_s = _final(None, x1, y4, gate_pad, *fin, p_sample, w_ple_b, N_P, 0, N_S, N_S, "final_sample")
    return out_p, out_s, kv_p, kv_s, kr_new, glu


def kernel(x_prompt, x_sample, cache_kv_latent, cache_k_rope, state_conv, p_prompt, p_sample, g_mix, w_in, b_gate, w_dw, b_dw, g_cn, b_cn, w_conv_out, g_qa, g_kva, w_qb, w_kb, w_vb, g_qn, g_kn, w_o, w_out, g_ffn, w_router, b_router, w_gu, b_gu, w_dn, b_dn, g_ple, w_ple_gate, w_ple):
    assert g_mix.shape[0] == 1
    out_p, out_s, kv_p, kv_s, kr_new, glu = _layer(
        x_prompt.reshape(N_P, D_MODEL), x_sample.reshape(N_S, D_MODEL),
        p_prompt[0].reshape(N_P, PLE_DIM), p_sample[0].reshape(N_S, PLE_DIM),
        cache_kv_latent[0], cache_k_rope[0], state_conv[0],
        g_mix[0], w_in[0], b_gate[0], w_dw[0], b_dw[0], g_cn[0], b_cn[0], w_conv_out[0],
        g_qa[0], g_kva[0], w_qb[0], w_kb[0], w_vb[0], g_qn[0], g_kn[0], w_o[0], w_out[0],
        g_ffn[0], w_router[0], b_router[0], w_gu[0], b_gu[0], w_dn[0], b_dn[0],
        g_ple[0], w_ple_gate[0], w_ple[0])
    tail = CONV_WIDTH - 1
    conv_p = jnp.stack([glu[(b + 1) * SEQ - tail:(b + 1) * SEQ] for b in range(BATCH)])
    conv_s = glu[N_P:].reshape(DEC_BATCH, DEC_SEQ, CONV_CHANNELS)[:, DEC_SEQ - tail:]
    return (out_p.reshape(BATCH, SEQ, D_MODEL),
            out_s.reshape(DEC_BATCH, DEC_SEQ, D_MODEL),
            kv_p.reshape(1, BATCH, SEQ, KV_LORA_RANK),
            kr_new[:N_P].reshape(1, BATCH, SEQ, ROPE_DIM),
            conv_p[None],
            kv_s.reshape(1, DEC_BATCH, DEC_SEQ, KV_LORA_RANK),
            kr_new[N_P:].reshape(1, DEC_BATCH, DEC_SEQ, ROPE_DIM),
            conv_s[None])
```

```python
import functools
import math

import jax
import jax.numpy as jnp
from jax import lax
from jax.experimental import pallas as pl
from jax.experimental.pallas import tpu as pltpu

F32 = jnp.float32
BF16 = jnp.bfloat16
I32 = jnp.int32
U32 = jnp.uint32

D_MODEL = 2048
BATCH = 2
SEQ = 4096
DEC_BATCH = 8
DEC_SEQ = 64
PAST_LEN = 4096
CHUNK = 64
CONV_CHANNELS = D_MODEL
CONV_WIDTH = 31
N_HEADS = 16
Q_LORA_RANK = 512
KV_LORA_RANK = 512
NOPE_DIM = 128
ROPE_DIM = 64
QK_DIM = NOPE_DIM + ROPE_DIM
V_DIM = 128
ROPE_THETA = 10000.0
N_EXPERTS = 32
TOP_K = 4
D_FF = D_MODEL
SWIGLU_ALPHA = 1.702
SWIGLU_LIMIT = 7.0
PLE_DIM = 256
EPS = 1e-6
NEG_INF = -1e30

N_P = BATCH * SEQ
N_S = DEC_BATCH * DEC_SEQ
N_TOK = N_P + N_S
O_U = 2 * CONV_CHANNELS
O_Q = O_U + Q_LORA_RANK
O_KV = O_Q + KV_LORA_RANK
O_KR = O_KV + ROPE_DIM
MID_W = 1152
HEAD_PAD = 256

TM = 512
CONV_T = 64
HALO = 32
MOE_BLK = 512
MOE_MAX_BLKS = (N_TOK * TOP_K) // MOE_BLK + N_EXPERTS
MOE_ROWS = MOE_MAX_BLKS * MOE_BLK
VMEM_LIMIT = 48 * 1024 * 1024


def _cparams(n_axes):
    return pltpu.CompilerParams(dimension_semantics=("arbitrary",) * n_axes,
                                vmem_limit_bytes=VMEM_LIMIT)


def _sigmoid(x):
    return 1.0 / (1.0 + jnp.exp(-x))


def _dot(a, b):
    return jnp.dot(a, b, preferred_element_type=F32)


def _stacked_rows(i, n_prompt_tiles, xp_ref, xs_ref):
    return jnp.where(i < n_prompt_tiles, xp_ref[...], xs_ref[...])


def _in_mid_kernel(xp_ref, xs_ref, g_ref, w_ref, gqa_ref, gkva_ref, h_ref, q_ref, kvp_ref, kvs_ref, kr_ref):
    i = pl.program_id(0)
    x = _stacked_rows(i, N_P // TM, xp_ref, xs_ref)
    h = x * lax.rsqrt(jnp.mean(x * x, axis=-1, keepdims=True) + EPS) * g_ref[...]
    hb = h.astype(BF16)
    h_ref[...] = hb
    z = _dot(hb, w_ref[...])
    ql = z[:, :Q_LORA_RANK]
    kvl = z[:, Q_LORA_RANK:Q_LORA_RANK + KV_LORA_RANK]
    qn = ql * lax.rsqrt(jnp.mean(ql * ql, axis=-1, keepdims=True) + EPS) * gqa_ref[...]
    q_ref[...] = qn.astype(BF16)
    kv = kvl * lax.rsqrt(jnp.mean(kvl * kvl, axis=-1, keepdims=True) + EPS) * gkva_ref[...]
    kr_ref[...] = z[:, Q_LORA_RANK + KV_LORA_RANK:]

    @pl.when(i < N_P // TM)
    def _():
        kvp_ref[...] = kv

    @pl.when(i >= N_P // TM)
    def _():
        kvs_ref[...] = kv


def _in_mid(xp, xs, g_mix, w_mid, g_qa, g_kva):
    n = N_TOK
    npt = N_P // TM
    return pl.pallas_call(
        _in_mid_kernel,
        grid=(n // TM,),
        in_specs=[
            pl.BlockSpec((TM, D_MODEL), lambda i: (jnp.minimum(i, npt - 1), 0)),
            pl.BlockSpec((TM, D_MODEL), lambda i: (jnp.maximum(i - npt, 0), 0)),
            pl.BlockSpec((1, D_MODEL), lambda i: (0, 0)),
            pl.BlockSpec((D_MODEL, MID_W), lambda i: (0, 0)),
            pl.BlockSpec((1, Q_LORA_RANK), lambda i: (0, 0)),
            pl.BlockSpec((1, KV_LORA_RANK), lambda i: (0, 0)),
        ],
        out_specs=[
            pl.BlockSpec((TM, D_MODEL), lambda i: (i, 0)),
            pl.BlockSpec((TM, Q_LORA_RANK), lambda i: (i, 0)),
            pl.BlockSpec((TM, KV_LORA_RANK), lambda i: (jnp.minimum(i, npt - 1), 0)),
            pl.BlockSpec((TM, KV_LORA_RANK), lambda i: (jnp.maximum(i - npt, 0), 0)),
            pl.BlockSpec((TM, 128), lambda i: (i, 0)),
        ],
        out_shape=[
            jax.ShapeDtypeStruct((n, D_MODEL), BF16),
            jax.ShapeDtypeStruct((n, Q_LORA_RANK), BF16),
            jax.ShapeDtypeStruct((N_P, KV_LORA_RANK), F32),
            jax.ShapeDtypeStruct((N_S, KV_LORA_RANK), F32),
            jax.ShapeDtypeStruct((n, 128), F32),
        ],
        compiler_params=_cparams(1),
        name="in_mid",
    )(xp, xs, g_mix, w_mid, g_qa, g_kva)


def _glu_kernel(h_ref, w1_ref, w2_ref, o_ref):
    h = h_ref[...]
    o_ref[...] = _dot(h, w1_ref[...]) * _sigmoid(_dot(h, w2_ref[...]))


def _in_glu(h, w_in_b):
    n = h.shape[0]
    tn = 1024
    nj = CONV_CHANNELS // tn
    return pl.pallas_call(
        _glu_kernel,
        grid=(n // TM, nj),
        in_specs=[
            pl.BlockSpec((TM, D_MODEL), lambda i, j: (i, 0)),
            pl.BlockSpec((D_MODEL, tn), lambda i, j: (0, j)),
            pl.BlockSpec((D_MODEL, tn), lambda i, j: (0, j + nj)),
        ],
        out_specs=pl.BlockSpec((TM, tn), lambda i, j: (i, j)),
        out_shape=jax.ShapeDtypeStruct((n, CONV_CHANNELS), F32),
        compiler_params=_cparams(2),
        name="in_glu",
    )(h, w_in_b, w_in_b)


_CONV_TILES_PER_SEQ = SEQ // CONV_T
_CONV_PROMPT_TILES = N_P // CONV_T
_CONV_LANES = 512
SUBLANES = 8
_SHIFT_ROWS = (HALO // SUBLANES - 1) * SUBLANES + CONV_T


def _conv_kernel(cur_ref, prev_ref, hist_ref, w_ref, bdw_ref, g_ref, b_ref, o_ref, win_ref, conv_ref, shift_ref):
    i = pl.program_id(0)
    first = jnp.logical_or(i >= _CONV_PROMPT_TILES, i % _CONV_TILES_PER_SEQ == 0)

    @pl.when(first)
    def _():
        win_ref[0:HALO, :] = hist_ref[0]

    @pl.when(jnp.logical_not(first))
    def _():
        win_ref[0:HALO, :] = prev_ref[...]

    win_ref[HALO:HALO + CONV_T, :] = cur_ref[...]
    for r in range(1, SUBLANES):
        shift_ref[r - 1] = win_ref[r:r + _SHIFT_ROWS, :]
    base = HALO - (CONV_WIDTH - 1)
    for c in range(0, CONV_CHANNELS, _CONV_LANES):
        acc = jnp.zeros((CONV_T, _CONV_LANES), F32)
        for k in range(CONV_WIDTH):
            q, r = divmod(base + k, SUBLANES)
            lanes = slice(c, c + _CONV_LANES)
            rows = slice(q * SUBLANES, q * SUBLANES + CONV_T)
            src = win_ref[rows, lanes] if r == 0 else shift_ref[r - 1, rows, lanes]
            acc = acc + w_ref[k:k + 1, lanes] * src
        conv_ref[:, c:c + _CONV_LANES] = acc + bdw_ref[:, c:c + _CONV_LANES]
    y = conv_ref[...]
    yc = y - jnp.mean(y, axis=-1, keepdims=True)
    var = jnp.mean(yc * yc, axis=-1, keepdims=True)
    z = yc * lax.rsqrt(var + EPS) * g_ref[...] + b_ref[...]
    o_ref[...] = (z * _sigmoid(z)).astype(BF16)


def _conv_module(glu, hist, w_dw, b_dw, g_cn, b_cn):
    n = glu.shape[0]
    n_tiles = n // CONV_T
    halo_per_tile = CONV_T // HALO

    def seq_of(i):
        return jnp.where(i < _CONV_PROMPT_TILES, i // _CONV_TILES_PER_SEQ, i - _CONV_PROMPT_TILES + BATCH)

    return pl.pallas_call(
        _conv_kernel,
        grid=(n_tiles,),
        in_specs=[
            pl.BlockSpec((CONV_T, CONV_CHANNELS), lambda i: (i, 0)),
            pl.BlockSpec((HALO, CONV_CHANNELS), lambda i: (jnp.maximum(i * halo_per_tile - 1, 0), 0)),
            pl.BlockSpec((1, HALO, CONV_CHANNELS), lambda i: (seq_of(i), 0, 0)),
            pl.BlockSpec((CONV_WIDTH, CONV_CHANNELS), lambda i: (0, 0)),
            pl.BlockSpec((1, CONV_CHANNELS), lambda i: (0, 0)),
            pl.BlockSpec((1, CONV_CHANNELS), lambda i: (0, 0)),
            pl.BlockSpec((1, CONV_CHANNELS), lambda i: (0, 0)),
        ],
        out_specs=pl.BlockSpec((CONV_T, CONV_CHANNELS), lambda i: (i, 0)),
        out_shape=jax.ShapeDtypeStruct((n, CONV_CHANNELS), BF16),
        scratch_shapes=[pltpu.VMEM((HALO + CONV_T, CONV_CHANNELS), F32),
                        pltpu.VMEM((CONV_T, CONV_CHANNELS), F32),
                        pltpu.VMEM((SUBLANES - 1, _SHIFT_ROWS, CONV_CHANNELS), F32)],
        compiler_params=_cparams(1),
        name="conv_module",
    )(glu, glu, hist, w_dw, b_dw, g_cn, b_cn)


ATT_TM = 512
_TAB_PROMPT_TILES = N_P // ATT_TM
_TAB_SEQ_TILES = SEQ // ATT_TM
_TAB_ROWS = SEQ + ATT_TM


def _tab_idx_new(i):
    return jnp.where(i < _TAB_PROMPT_TILES, i % _TAB_SEQ_TILES, _TAB_SEQ_TILES)


def _rope_pair(u, c, s):
    return u * c + pltpu.roll(u, 64, 1) * s


_Q_SCALE = math.log2(math.e) / math.sqrt(QK_DIM)


def _q_heads_kernel(ql_ref, w_ref, g_ref, c_ref, s_ref, o_ref):
    ql = ql_ref[...]
    g = g_ref[...]
    c = c_ref[...]
    s = s_ref[...]
    for h in range(N_HEADS):
        qf = _dot(ql, w_ref[:, h * HEAD_PAD:(h + 1) * HEAD_PAD])
        ssq = jnp.sum(qf * qf, axis=-1, keepdims=True)
        qn = qf * (lax.rsqrt(ssq * (1.0 / QK_DIM) + EPS) * _Q_SCALE) * g
        o_ref[h, :, :NOPE_DIM] = qn[:, :NOPE_DIM].astype(BF16)
        o_ref[h, :, NOPE_DIM:] = _rope_pair(qn[:, NOPE_DIM:], c, s).astype(BF16)


def _q_heads(q_lat, w_q, g_q, cos_t, sin_t):
    n = q_lat.shape[0]
    return pl.pallas_call(
        _q_heads_kernel,
        grid=(n // ATT_TM,),
        in_specs=[
            pl.BlockSpec((ATT_TM, Q_LORA_RANK), lambda i: (i, 0)),
            pl.BlockSpec((Q_LORA_RANK, N_HEADS * HEAD_PAD), lambda i: (0, 0)),
            pl.BlockSpec((1, HEAD_PAD), lambda i: (0, 0)),
            pl.BlockSpec((ATT_TM, 128), lambda i: (_tab_idx_new(i), 0)),
            pl.BlockSpec((ATT_TM, 128), lambda i: (_tab_idx_new(i), 0)),
        ],
        out_specs=pl.BlockSpec((N_HEADS, ATT_TM, HEAD_PAD), lambda i: (0, i, 0)),
        out_shape=jax.ShapeDtypeStruct((N_HEADS, n, HEAD_PAD), BF16),
        compiler_params=_cparams(1),
        name="q_heads",
    )(q_lat, w_q, g_q, cos_t, sin_t)


def _kv_heads_kernel(kvp_ref, kvs_ref, kr_ref, w_ref, gr_ref, c_ref, s_ref, k_ref, v_ref):
    kv = _stacked_rows(pl.program_id(0), N_P // ATT_TM, kvp_ref, kvs_ref).astype(BF16)
    u = kr_ref[...]
    ssq_r = jnp.sum(u * u, axis=-1, keepdims=True)
    krot = _rope_pair(u * gr_ref[...], c_ref[...], s_ref[...])
    for h in range(N_HEADS):
        z = _dot(kv, w_ref[:, h * HEAD_PAD:(h + 1) * HEAD_PAD])
        kn = z[:, :NOPE_DIM]
        ssq = jnp.sum(kn * kn, axis=-1, keepdims=True) + ssq_r
        scale = lax.rsqrt(ssq * (1.0 / QK_DIM) + EPS)
        k_ref[h, :, :NOPE_DIM] = (kn * scale).astype(BF16)
        k_ref[h, :, NOPE_DIM:] = (krot * scale).astype(BF16)
        v_ref[h] = z[:, NOPE_DIM:].astype(BF16)


def _kv_heads(kv_p, kv_s, kr_pad, w_kv, g_kn_rope, cos_t, sin_t, tab_idx, name):
    n = N_TOK
    npt = N_P // ATT_TM
    return pl.pallas_call(
        _kv_heads_kernel,
        grid=(n // ATT_TM,),
        in_specs=[
            pl.BlockSpec((ATT_TM, KV_LORA_RANK), lambda i: (jnp.minimum(i, npt - 1), 0)),
            pl.BlockSpec((ATT_TM, KV_LORA_RANK), lambda i: (jnp.maximum(i - npt, 0), 0)),
            pl.BlockSpec((ATT_TM, 128), lambda i: (i, 0)),
            pl.BlockSpec((KV_LORA_RANK, N_HEADS * HEAD_PAD), lambda i: (0, 0)),
            pl.BlockSpec((1, 128), lambda i: (0, 0)),
            pl.BlockSpec((ATT_TM, 128), lambda i: (tab_idx(i), 0)),
            pl.BlockSpec((ATT_TM, 128), lambda i: (tab_idx(i), 0)),
        ],
        out_specs=[
            pl.BlockSpec((N_HEADS, ATT_TM, HEAD_PAD), lambda i: (0, i, 0)),
            pl.BlockSpec((N_HEADS, ATT_TM, V_DIM), lambda i: (0, i, 0)),
        ],
        out_shape=[
            jax.ShapeDtypeStruct((N_HEADS, n, HEAD_PAD), BF16),
            jax.ShapeDtypeStruct((N_HEADS, n, V_DIM), BF16),
        ],
        compiler_params=_cparams(1),
        name=name,
    )(kv_p, kv_s, kr_pad, w_kv, g_kn_rope, cos_t, sin_t)


_TQ = 512
_TKB = 512
_HB = 4
_HBP = 4


def _flash_prompt_kernel(q_ref, k_ref, v_ref, o_ref, m_ref, l_ref, acc_ref):
    qi = pl.program_id(2)
    m_ref[...] = jnp.full(m_ref.shape, NEG_INF, F32)
    l_ref[...] = jnp.zeros(l_ref.shape, F32)
    acc_ref[...] = jnp.zeros(acc_ref.shape, F32)
    nlb = _TKB // 128

    def step(ki, masked):
        start = pl.multiple_of(ki * _TKB, _TKB)
        scores = [lax.dot_general(q_ref[hh], k_ref[hh, pl.ds(start, _TKB), :], (((1,), (1,)), ((), ())),
                                  preferred_element_type=F32) for hh in range(_HBP)]
        probs = []
        for hh in range(_HBP):
            s = scores[hh]
            if masked:
                rc = lax.broadcasted_iota(I32, (_TQ, _TKB), 0) // CHUNK
                cc = lax.broadcasted_iota(I32, (_TQ, _TKB), 1) // CHUNK
                s = jnp.where(cc <= rc, s, NEG_INF)
            sb = [s[:, c * 128:(c + 1) * 128] for c in range(nlb)]
            bm = sb[0]
            for c in range(1, nlb):
                bm = jnp.maximum(bm, sb[c])
            m_prev = m_ref[hh]
            m_new = jnp.maximum(m_prev, jnp.max(bm, axis=-1, keepdims=True))
            alpha = jnp.exp2(m_prev - m_new)
            ps = [jnp.exp2(x - m_new) for x in sb]
            psum = ps[0]
            for c in range(1, nlb):
                psum = psum + ps[c]
            l_ref[hh] = alpha * l_ref[hh] + psum
            m_ref[hh] = m_new
            probs.append((alpha, jnp.concatenate(ps, axis=1).astype(BF16)))
        for hh in range(_HBP):
            alpha, p = probs[hh]
            acc_ref[hh] = alpha * acc_ref[hh] + _dot(p, v_ref[hh, pl.ds(start, _TKB), :])

    def body(ki, carry):
        step(ki, False)
        return carry

    lax.fori_loop(0, qi, body, 0)
    step(qi, True)
    for hh in range(_HBP):
        l = jnp.sum(l_ref[hh], axis=-1, keepdims=True)
        o_ref[:, hh * V_DIM:(hh + 1) * V_DIM] = (acc_ref[hh] / l).astype(BF16)


def _flash_prompt(q, k, v):
    nq = SEQ // _TQ
    return pl.pallas_call(
        _flash_prompt_kernel,
        grid=(BATCH, N_HEADS // _HBP, nq),
        in_specs=[
            pl.BlockSpec((_HBP, _TQ, HEAD_PAD), lambda b, h, i: (h, b * nq + i, 0)),
            pl.BlockSpec((_HBP, SEQ, HEAD_PAD), lambda b, h, i: (h, b, 0)),
            pl.BlockSpec((_HBP, SEQ, V_DIM), lambda b, h, i: (h, b, 0)),
        ],
        out_specs=pl.BlockSpec((_TQ, _HBP * V_DIM), lambda b, h, i: (b * nq + i, h)),
        out_shape=jax.ShapeDtypeStruct((N_TOK, N_HEADS * V_DIM), BF16),
        scratch_shapes=[pltpu.VMEM((_HBP, _TQ, 128), F32), pltpu.VMEM((_HBP, _TQ, 128), F32),
                        pltpu.VMEM((_HBP, _TQ, V_DIM), F32)],
        compiler_params=_cparams(3),
        name="flash_prompt",
    )(q, k, v)


_KC_ROWS = 512


def _flash_sample_kernel(prev_ref, q_ref, kv_ref, kr_ref, w_ref, gr_ref, c_ref, s_ref, kn_ref, vn_ref,
                         o_ref, kvb_ref, krot_ref, ssqr_ref, k_ref, v_ref):
    del prev_ref

    @pl.when(pl.program_id(1) == 0)
    def _():
        kvb_ref[...] = kv_ref[...].astype(BF16)
        u = kr_ref[...]
        ssqr_ref[...] = jnp.broadcast_to(jnp.sum(u * u, axis=-1, keepdims=True), ssqr_ref.shape)
        krot_ref[...] = _rope_pair(u * gr_ref[...], c_ref[...], s_ref[...])

    nt = (((1,), (1,)), ((), ()))
    for hh in range(_HB):
        w = w_ref[:, hh * HEAD_PAD:(hh + 1) * HEAD_PAD]
        for r in range(0, PAST_LEN, _KC_ROWS):
            rows = slice(r, r + _KC_ROWS)
            z = _dot(kvb_ref[rows, :], w)
            kn = z[:, :NOPE_DIM]
            ssq = jnp.sum(kn * kn, axis=-1, keepdims=True) + ssqr_ref[rows, :]
            scale = lax.rsqrt(ssq * (1.0 / QK_DIM) + EPS)
            k_ref[rows, :NOPE_DIM] = (kn * scale).astype(BF16)
            k_ref[rows, NOPE_DIM:] = (krot_ref[rows, :] * scale).astype(BF16)
            v_ref[rows, :] = z[:, NOPE_DIM:].astype(BF16)
        q = q_ref[hh]
        s1 = lax.dot_general(q, k_ref[...], nt, preferred_element_type=F32)
        s2 = lax.dot_general(q, kn_ref[hh], nt, preferred_element_type=F32)
        m = jnp.maximum(jnp.max(s1, axis=-1, keepdims=True), jnp.max(s2, axis=-1, keepdims=True))
        p1 = jnp.exp2(s1 - m)
        p2 = jnp.exp2(s2 - m)
        l = jnp.sum(p1, axis=-1, keepdims=True) + jnp.sum(p2, axis=-1, keepdims=True)
        o = _dot(p1.astype(BF16), v_ref[...]) + _dot(p2.astype(BF16), vn_ref[hh])
        o_ref[:, hh * V_DIM:(hh + 1) * V_DIM] = (o / l).astype(BF16)


def _flash_sample(attn, q, cache_kv, cache_kr_pad, w_kv, g_kn_rope, cos_t, sin_t, k_new, v_new):
    assert (PAST_LEN + DEC_SEQ - 1) // CHUNK <= PAST_LEN // CHUNK
    blk0 = N_P // DEC_SEQ
    new = lambda b, h: (h, blk0 + b, 0)
    const = lambda b, h: (0, 0)
    once = pl.Buffered(1)
    return pl.pallas_call(
        _flash_sample_kernel,
        grid=(DEC_BATCH, N_HEADS // _HB),
        in_specs=[
            pl.BlockSpec(memory_space=pl.ANY),
            pl.BlockSpec((_HB, DEC_SEQ, HEAD_PAD), new),
            pl.BlockSpec((PAST_LEN, KV_LORA_RANK), lambda b, h: (b, 0)),
            pl.BlockSpec((PAST_LEN, 128), lambda b, h: (b, 0)),
            pl.BlockSpec((KV_LORA_RANK, _HB * HEAD_PAD), lambda b, h: (0, h)),
            pl.BlockSpec((1, 128), const),
            pl.BlockSpec((PAST_LEN, 128), const, pipeline_mode=once),
            pl.BlockSpec((PAST_LEN, 128), const, pipeline_mode=once),
            pl.BlockSpec((_HB, DEC_SEQ, HEAD_PAD), new),
            pl.BlockSpec((_HB, DEC_SEQ, V_DIM), new),
        ],
        out_specs=pl.BlockSpec((DEC_SEQ, _HB * V_DIM), lambda b, h: (blk0 + b, h)),
        out_shape=jax.ShapeDtypeStruct((N_TOK, N_HEADS * V_DIM), BF16),
        scratch_shapes=[pltpu.VMEM((PAST_LEN, KV_LORA_RANK), BF16),
                        pltpu.VMEM((PAST_LEN, 128), F32),
                        pltpu.VMEM((PAST_LEN, 128), F32),
                        pltpu.VMEM((PAST_LEN, HEAD_PAD), BF16),
                        pltpu.VMEM((PAST_LEN, V_DIM), BF16)],
        input_output_aliases={0: 0},
        compiler_params=_cparams(2),
        name="flash_sample",
    )(attn, q, cache_kv, cache_kr_pad, w_kv, g_kn_rope, cos_t, sin_t, k_new, v_new)


def _merge_kernel(h_ref, c_ref, a_ref, wga_ref, wgb_ref, bga_ref, bgb_ref, wc_ref, wo_ref, o_ref):
    h = h_ref[...]
    ga = _sigmoid(_dot(h, wga_ref[...]) + bga_ref[...])
    gb = _sigmoid(_dot(h, wgb_ref[...]) + bgb_ref[...])
    mix = ga * _dot(c_ref[...], wc_ref[...]) + gb * _dot(a_ref[...], wo_ref[...])
    o_ref[...] = mix.astype(BF16)


def _merge(h, c_act, attn, w_gate, b_gate, w_conv_out, w_o):
    n = h.shape[0]
    tn = 512
    nj = D_MODEL // tn
    row = lambda i, j: (i, 0)
    return pl.pallas_call(
        _merge_kernel,
        grid=(n // TM, nj),
        in_specs=[
            pl.BlockSpec((TM, D_MODEL), row),
            pl.BlockSpec((TM, CONV_CHANNELS), row),
            pl.BlockSpec((TM, N_HEADS * V_DIM), row),
            pl.BlockSpec((D_MODEL, tn), lambda i, j: (0, j)),
            pl.BlockSpec((D_MODEL, tn), lambda i, j: (0, j + nj)),
            pl.BlockSpec((1, tn), lambda i, j: (0, j)),
            pl.BlockSpec((1, tn), lambda i, j: (0, j + nj)),
            pl.BlockSpec((CONV_CHANNELS, tn), lambda i, j: (0, j)),
            pl.BlockSpec((N_HEADS * V_DIM, tn), lambda i, j: (0, j)),
        ],
        out_specs=pl.BlockSpec((TM, tn), lambda i, j: (i, j)),
        out_shape=jax.ShapeDtypeStruct((n, D_MODEL), BF16),
        compiler_params=_cparams(2),
        name="merge",
    )(h, c_act, attn, w_gate, w_gate, b_gate, b_gate, w_conv_out, w_o)


def _split_bf16(x):
    hi = x.astype(BF16)
    lo = (x - hi.astype(F32)).astype(BF16)
    return hi, lo


_HALF = D_MODEL // 2


def _pack_bf16_pair(a, b):
    ua = lax.bitcast_convert_type(a.astype(BF16).astype(F32), U32)
    ub = lax.bitcast_convert_type(b.astype(BF16).astype(F32), U32)
    return lax.bitcast_convert_type(ua | (ub >> 16), F32)


def _unpack_bf16_pair(w):
    w = lax.bitcast_convert_type(w, U32)
    a = lax.bitcast_convert_type(w & jnp.uint32(0xFFFF0000), F32).astype(BF16)
    b = lax.bitcast_convert_type(w << 16, F32).astype(BF16)
    return a, b


def _out_router_kernel(n_tiles, n_prompt_tiles, mix_ref, xp_ref, xs_ref, w_ref, g_ref, wrh_ref, wrl_ref, br_ref,
                       x1_ref, hm_ref, idx_ref, gate_ref):
    i = pl.program_id(0)

    @pl.when(i < n_tiles)
    def _():
        x = _stacked_rows(i, n_prompt_tiles, xp_ref, xs_ref)
        _out_router_tile(mix_ref, x, w_ref, g_ref, wrh_ref, wrl_ref, br_ref,
                         x1_ref, hm_ref, idx_ref, gate_ref)

    @pl.when(i >= n_tiles)
    def _():
        hm_ref[...] = jnp.zeros(hm_ref.shape, F32)


def _out_router_tile(mix_ref, x, w_ref, g_ref, wrh_ref, wrl_ref, br_ref,
                     x1_ref, hm_ref, idx_ref, gate_ref):
    x1 = x + _dot(mix_ref[...], w_ref[...])
    x1_ref[...] = x1
    hn = x1 * lax.rsqrt(jnp.mean(x1 * x1, axis=-1, keepdims=True) + EPS) * g_ref[...]
    hm_ref[...] = _pack_bf16_pair(hn[:, :_HALF], hn[:, _HALF:])
    hh, hl = _split_bf16(hn)
    logits = _dot(hh, wrh_ref[...]) + (_dot(hh, wrl_ref[...]) + _dot(hl, wrh_ref[...])) + br_ref[...]
    lane = lax.broadcasted_iota(I32, logits.shape, 1).astype(F32)
    vals = []
    idx_out = jnp.zeros(logits.shape, F32)
    for k in range(TOP_K):
        m = jnp.max(logits, axis=-1, keepdims=True)
        sel = jnp.min(jnp.where(logits == m, lane, 1e9), axis=-1, keepdims=True)
        vals.append(m)
        idx_out = jnp.where(lane == float(k), sel, idx_out)
        logits = jnp.where(lane == sel, -jnp.inf, logits)
    exps = [jnp.exp(v - vals[0]) for v in vals]
    denom = exps[0] + exps[1] + exps[2] + exps[3]
    gate_out = jnp.zeros(idx_out.shape, F32)
    for k in range(TOP_K):
        gate_out = jnp.where(lane == float(k), exps[k] / denom, gate_out)
    idx_ref[...] = idx_out.astype(I32)
    gate_ref[...] = gate_out


def _out_router(mix, xp, xs, w_out, g_ffn, wr_hi, wr_lo, b_r):
    n = N_TOK
    tm = 256
    n_tiles = n // tm
    npt = N_P // tm
    const = lambda i: (0, 0)
    row = lambda i: (jnp.minimum(i, n_tiles - 1), 0)
    return pl.pallas_call(
        functools.partial(_out_router_kernel, n_tiles, npt),
        grid=(2 * n_tiles,),
        in_specs=[
            pl.BlockSpec((tm, D_MODEL), row),
            pl.BlockSpec((tm, D_MODEL), lambda i: (jnp.minimum(i, npt - 1), 0)),
            pl.BlockSpec((tm, D_MODEL), lambda i: (jnp.clip(i - npt, 0, N_S // tm - 1), 0)),
            pl.BlockSpec((D_MODEL, D_MODEL), const),
            pl.BlockSpec((1, D_MODEL), const),
            pl.BlockSpec((D_MODEL, 128), const),
            pl.BlockSpec((D_MODEL, 128), const),
            pl.BlockSpec((1, 128), const),
        ],
        out_specs=[
            pl.BlockSpec((tm, D_MODEL), row),
            pl.BlockSpec((tm, _HALF), lambda i: (i, 0)),
            pl.BlockSpec((tm, 128), row),
            pl.BlockSpec((tm, 128), row),
        ],
        out_shape=[
            jax.ShapeDtypeStruct((n, D_MODEL), F32),
            jax.ShapeDtypeStruct((2 * n, _HALF), F32),
            jax.ShapeDtypeStruct((n, 128), I32),
            jax.ShapeDtypeStruct((n, 128), F32),
        ],
        compiler_params=_cparams(1),
        name="out_router",
    )(mix, xp, xs, w_out, g_ffn, wr_hi, wr_lo, b_r)


_F_VALID, _F_FIRST, _F_NEXT, _F_GROUP0, _F_SLOT = 1, 2, 4, 8, 16


_P_E, _P_W, _P_N, _P_B, _P_BI, _P_NE, _P_NW, _P_FL, _P_SUBS = range(9)
MOE_SUB = 128


def _stream_weights(t, plan_ref, copies, cast):
    flags = plan_ref[_P_FL, t]

    @pl.when((flags & _F_FIRST) != 0)
    def _():
        slot = (flags // _F_SLOT) & 1
        cur = copies(plan_ref[_P_E, t], plan_ref[_P_W, t], slot)

        @pl.when((flags & _F_GROUP0) != 0)
        def _():
            for c in cur:
                c.start()

        for c in cur:
            c.wait()

        @pl.when((flags & _F_NEXT) != 0)
        def _():
            for c in copies(plan_ref[_P_NE, t], plan_ref[_P_NW, t], 1 - slot):
                c.start()

        cast(slot)


def _for_used_rows(valid, subs, rows_body):
    for n_sub in range(1, MOE_BLK // MOE_SUB + 1):
        @pl.when(jnp.logical_and(valid, subs == n_sub))
        def _(m=n_sub * MOE_SUB):
            rows_body(m)


def _moe_up_kernel(plan_ref, prev_ref, x_ref, w_hbm, bg_ref, bu_ref, o_ref, wbuf_ref, wgb_ref, wub_ref, sem_ref):
    del prev_ref
    t = pl.program_id(0)

    def copies(e, w, slot):
        col = pl.multiple_of(w * _UP_TN, _UP_TN)
        return (pltpu.make_async_copy(w_hbm.at[e, :, pl.ds(col, _UP_TN)], wbuf_ref.at[slot, 0], sem_ref.at[slot, 0]),
                pltpu.make_async_copy(w_hbm.at[e, :, pl.ds(col + D_FF, _UP_TN)], wbuf_ref.at[slot, 1],
                                      sem_ref.at[slot, 1]))

    def cast(slot):
        wgb_ref[...] = wbuf_ref[slot, 0].astype(BF16)
        wub_ref[...] = wbuf_ref[slot, 1].astype(BF16)

    _stream_weights(t, plan_ref, copies, cast)
    valid = (plan_ref[_P_FL, t] & _F_VALID) != 0

    def rows_body(m):
        xa, xb = _unpack_bf16_pair(x_ref[:m, :])
        g = _dot(xa, wgb_ref[:_HALF, :]) + _dot(xb, wgb_ref[_HALF:, :]) + bg_ref[0]
        u = _dot(xa, wub_ref[:_HALF, :]) + _dot(xb, wub_ref[_HALF:, :]) + bu_ref[0]
        g = jnp.minimum(g, SWIGLU_LIMIT)
        u = jnp.clip(u, -SWIGLU_LIMIT, SWIGLU_LIMIT)
        o_ref[:m, :] = ((u + 1.0) * (g * _sigmoid(SWIGLU_ALPHA * g))).astype(BF16)
        if m < MOE_BLK:
            o_ref[m:, :] = jnp.zeros((MOE_BLK - m, o_ref.shape[1]), BF16)

    _for_used_rows(valid, plan_ref[_P_SUBS, t], rows_body)

    @pl.when(jnp.logical_not(valid))
    def _():
        o_ref[...] = jnp.zeros(o_ref.shape, BF16)


_UP_TN = 512
_UP_TILES = D_FF // _UP_TN
_DN_TN = 1024
_DN_TILES = D_MODEL // _DN_TN
MOE_CHUNKS = 4
_CHUNK_BLKS = MOE_MAX_BLKS // MOE_CHUNKS


def _moe_up(plan, act_prev, xs, w_gu, b_gu, chunk):
    steps = plan.shape[1]
    blk0 = chunk * _CHUNK_BLKS
    bspec = lambda off: pl.BlockSpec((1, 1, _UP_TN), lambda t, p: (p[_P_E, t], 0, p[_P_W, t] + off))
    aliases = {} if act_prev is None else {1: 0}
    prev = jnp.zeros((8, 128), BF16) if act_prev is None else act_prev
    return pl.pallas_call(
        _moe_up_kernel,
        grid_spec=pltpu.PrefetchScalarGridSpec(
            num_scalar_prefetch=1,
            grid=(steps,),
            in_specs=[
                pl.BlockSpec(memory_space=pl.ANY),
                pl.BlockSpec((MOE_BLK, _HALF), lambda t, p: (p[_P_BI, t], 0)),
                pl.BlockSpec(memory_space=pl.ANY),
                bspec(0), bspec(_UP_TILES),
            ],
            out_specs=pl.BlockSpec((MOE_BLK, _UP_TN),
                                   lambda t, p: (blk0 + p[_P_B, t], p[_P_N, t])),
            scratch_shapes=[pltpu.VMEM((2, 2, D_MODEL, _UP_TN), F32),
                            pltpu.VMEM((D_MODEL, _UP_TN), BF16), pltpu.VMEM((D_MODEL, _UP_TN), BF16),
                            pltpu.SemaphoreType.DMA((2, 2))],
        ),
        out_shape=jax.ShapeDtypeStruct((MOE_ROWS, D_FF), BF16),
        input_output_aliases=aliases,
        compiler_params=_cparams(1),
        name=f"moe_up_{chunk}",
    )(plan, prev, xs, w_gu, b_gu, b_gu)


_DN_HALF = _DN_TN // 2


def _moe_down_kernel(plan_ref, a_ref, w_hbm, b_ref, o_ref, wbuf_ref, wb_ref, sem_ref):
    t = pl.program_id(0)

    def copies(e, w, slot):
        col = pl.multiple_of(w * _DN_TN, _DN_TN)
        return (pltpu.make_async_copy(w_hbm.at[e, :, pl.ds(col, _DN_TN)], wbuf_ref.at[slot], sem_ref.at[slot]),)

    def cast(slot):
        wb_ref[...] = wbuf_ref[slot].astype(BF16)

    _stream_weights(t, plan_ref, copies, cast)
    valid = (plan_ref[_P_FL, t] & _F_VALID) != 0

    def rows_body(m):
        y = _dot(a_ref[:m, :], wb_ref[...]) + b_ref[0]
        o_ref[:m, :] = _pack_bf16_pair(y[:, :_DN_HALF], y[:, _DN_HALF:])
        if m < MOE_BLK:
            o_ref[m:, :] = jnp.zeros((MOE_BLK - m, o_ref.shape[1]), F32)

    _for_used_rows(valid, plan_ref[_P_SUBS, t], rows_body)

    @pl.when(jnp.logical_not(valid))
    def _():
        o_ref[...] = jnp.zeros(o_ref.shape, F32)


def _moe_down(plan, act, w_dn, b_dn):
    steps = plan.shape[1]
    return pl.pallas_call(
        _moe_down_kernel,
        grid_spec=pltpu.PrefetchScalarGridSpec(
            num_scalar_prefetch=1,
            grid=(steps,),
            in_specs=[
                pl.BlockSpec((MOE_BLK, D_FF), lambda t, p: (p[_P_BI, t], 0)),
                pl.BlockSpec(memory_space=pl.ANY),
                pl.BlockSpec((1, 1, _DN_TN), lambda t, p: (p[_P_E, t], 0, p[_P_W, t])),
            ],
            out_specs=pl.BlockSpec((MOE_BLK, _DN_HALF), lambda t, p: (p[_P_B, t], p[_P_N, t])),
            scratch_shapes=[pltpu.VMEM((2, D_FF, _DN_TN), F32), pltpu.VMEM((D_FF, _DN_TN), BF16),
                            pltpu.SemaphoreType.DMA((2,))],
        ),
        out_shape=jax.ShapeDtypeStruct((MOE_ROWS, _HALF), F32),
        compiler_params=_cparams(1),
        name="moe_down",
    )(plan, act, w_dn, b_dn)


def _moe_dispatch(top_idx):
    n_asg = N_TOK * TOP_K
    flat_e = top_idx.reshape(-1)
    onehot = (flat_e[:, None] == jnp.arange(N_EXPERTS, dtype=I32)[None, :]).astype(I32)
    csum = jnp.cumsum(onehot, axis=0)
    counts = csum[-1]
    rank = jnp.sum(csum * onehot, axis=1) - 1
    nblk = (counts + MOE_BLK - 1) // MOE_BLK
    blk_start = jnp.cumsum(nblk) - nblk
    dest = jnp.sum(onehot * blk_start[None, :], axis=1) * MOE_BLK + rank
    pad_src = jnp.arange(MOE_ROWS, dtype=I32) % N_TOK
    row_tok = pad_src.at[dest].set(jnp.arange(n_asg, dtype=I32) // TOP_K,
                                   mode="promise_in_bounds", unique_indices=True)
    return dest, row_tok, counts, nblk, blk_start


def _moe_steps(counts, nblk, blk_start, n_tiles, blk_lo, n_blks):
    t_max = n_tiles * n_blks
    lo = jnp.clip(blk_start, blk_lo, blk_lo + n_blks)
    hi = jnp.clip(blk_start + nblk, blk_lo, blk_lo + n_blks)
    nb_e = hi - lo
    per_e = nb_e * n_tiles
    s_end = jnp.cumsum(per_e)
    total = s_end[-1]
    t = jnp.arange(t_max, dtype=I32)
    tc = jnp.clip(t, 0, jnp.maximum(total - 1, 0))
    e = jnp.minimum(jnp.sum((s_end[None, :] <= tc[:, None]).astype(I32), axis=1), N_EXPERTS - 1)
    sel = (e[:, None] == jnp.arange(N_EXPERTS, dtype=I32)[None, :]).astype(I32)
    pick = lambda v: jnp.sum(sel * v[None, :], axis=1)
    local = tc - pick(s_end - per_e)
    nb = jnp.maximum(pick(nb_e), 1)
    w_tile = jnp.clip(local // nb, 0, n_tiles - 1)
    r = local % nb
    valid = t < total
    first = jnp.logical_and(valid, r == 0)
    fill = t - total
    blk = jnp.where(valid, pick(lo) - blk_lo + r, total // n_tiles + fill // n_tiles)
    rows_used = pick(counts) - (pick(lo) + r - pick(blk_start)) * MOE_BLK
    subs = jnp.clip((rows_used + MOE_SUB - 1) // MOE_SUB, 1, MOE_BLK // MOE_SUB)
    o_tile = jnp.where(valid, w_tile, fill % n_tiles)
    blk = jnp.clip(blk, 0, n_blks - 1)
    blk_in = jnp.where(valid, blk, jnp.maximum(total // n_tiles - 1, 0))
    ids = jnp.arange(N_EXPERTS, dtype=I32)
    owners = jnp.where(nb_e > 0, ids, N_EXPERTS)
    later = jnp.flip(lax.cummin(jnp.flip(owners)))
    next_owner = pick(jnp.concatenate([later[1:], jnp.full((1,), N_EXPERTS, I32)]))
    last_tile = w_tile == n_tiles - 1
    next_e = jnp.where(last_tile, next_owner, e)
    next_w = jnp.where(last_tile, 0, w_tile + 1)
    has_next = jnp.logical_and(first, next_e < N_EXPERTS)
    group = jnp.cumsum(first.astype(I32)) - 1
    flags = (valid * _F_VALID + first * _F_FIRST + has_next * _F_NEXT
             + jnp.logical_and(first, group == 0) * _F_GROUP0 + (group % 2) * _F_SLOT)
    rows = {_P_E: e, _P_W: w_tile, _P_N: o_tile, _P_B: blk, _P_BI: blk_in,
            _P_NE: jnp.minimum(next_e, N_EXPERTS - 1), _P_NW: next_w, _P_FL: flags, _P_SUBS: subs}
    return jnp.stack([rows[k].astype(I32) for k in range(len(rows))])


def _moe_plans(counts, nblk, blk_start, n_tiles, n_chunks, n_blks):
    los = jnp.arange(n_chunks, dtype=I32) * n_blks
    return jax.vmap(lambda lo: _moe_steps(counts, nblk, blk_start, n_tiles, lo, n_blks))(los)


_FIN_TM = 256
_FIN_TN = 512
FIN_CHUNKS = 4


def _unpack_expert_rows(words):
    u = lax.bitcast_convert_type(words, U32)
    hi = lax.bitcast_convert_type(u & jnp.uint32(0xFFFF0000), F32)
    lo = lax.bitcast_convert_type(u << 16, F32)
    parts = []
    for n in range(_DN_TILES):
        cols = slice(n * _DN_HALF, (n + 1) * _DN_HALF)
        parts += [hi[:, cols], lo[:, cols]]
    return jnp.concatenate(parts, axis=1)


def _final_kernel(prev_ref, x1_ref, y0_ref, y1_ref, y2_ref, y3_ref, gate_ref, g_ref, wg_ref, p_ref, wp_ref,
                  o_ref, x2_ref):
    del prev_ref
    gate = gate_ref[...]
    moe = (_unpack_expert_rows(y0_ref[0]) * gate[:, 0:1] + _unpack_expert_rows(y1_ref[0]) * gate[:, 1:2]
           + _unpack_expert_rows(y2_ref[0]) * gate[:, 2:3] + _unpack_expert_rows(y3_ref[0]) * gate[:, 3:4])
    x2 = x1_ref[...] + moe
    x2_ref[...] = x2
    hp = (x2 * lax.rsqrt(jnp.mean(x2 * x2, axis=-1, keepdims=True) + EPS) * g_ref[...]).astype(BF16)
    pb = p_ref[...].astype(BF16)
    for c in range(0, D_MODEL, _FIN_TN):
        cols = slice(c, c + _FIN_TN)
        emb = _dot(pb, wp_ref[:, cols])
        o_ref[:, cols] = x2_ref[:, cols] + _sigmoid(_dot(hp, wg_ref[:, cols])) * emb


def _final(out_prev, x1, y4, gate, g_ple, w_ple_gate, p, w_ple, tok0, out0, n, n_out, name):
    t0 = tok0 // _FIN_TM
    o0 = out0 // _FIN_TM
    pt0 = out0 // _FIN_TM
    const = lambda i: (0, 0)
    yspec = lambda k: pl.BlockSpec((1, _FIN_TM, _HALF), lambda i: (k, i, 0))
    once = pl.Buffered(1)
    aliases = {} if out_prev is None else {0: 0}
    prev = jnp.zeros((8, 128), F32) if out_prev is None else out_prev
    return pl.pallas_call(
        _final_kernel,
        grid=(n // _FIN_TM,),
        in_specs=[
            pl.BlockSpec(memory_space=pl.ANY),
            pl.BlockSpec((_FIN_TM, D_MODEL), lambda i: (t0 + i, 0)),
            yspec(0), yspec(1), yspec(2), yspec(3),
            pl.BlockSpec((_FIN_TM, 128), lambda i: (t0 + i, 0)),
            pl.BlockSpec((1, D_MODEL), const),
            pl.BlockSpec((D_MODEL, D_MODEL), const, pipeline_mode=once),
            pl.BlockSpec((_FIN_TM, PLE_DIM), lambda i: (pt0 + i, 0)),
            pl.BlockSpec((PLE_DIM, D_MODEL), const, pipeline_mode=once),
        ],
        out_specs=pl.BlockSpec((_FIN_TM, D_MODEL), lambda i: (o0 + i, 0)),
        out_shape=jax.ShapeDtypeStruct((n_out, D_MODEL), F32),
        scratch_shapes=[pltpu.VMEM((_FIN_TM, D_MODEL), F32)],
        input_output_aliases=aliases,
        compiler_params=_cparams(1),
        name=name,
    )(prev, x1, y4, y4, y4, y4, gate, g_ple, w_ple_gate, p, w_ple)


def _rope_layout(x):
    half = ROPE_DIM // 2
    z = jnp.zeros(x.shape[:-1] + (half,), x.dtype)
    return jnp.concatenate([x[..., :half], z, x[..., half:], z], axis=-1)


def _rope_tables():
    half = ROPE_DIM // 2
    inv_freq = ROPE_THETA ** (-jnp.arange(half, dtype=F32) / half)
    pos = jnp.arange(PAST_LEN + DEC_SEQ, dtype=I32)
    ang = pos.astype(F32)[:, None] * inv_freq[None, :]
    cos, sin = jnp.cos(ang), jnp.sin(ang)
    z = jnp.zeros_like(cos)
    c = jnp.concatenate([cos, z, cos, z], axis=-1)
    s = jnp.concatenate([-sin, z, sin, z], axis=-1)
    rep = ATT_TM // DEC_SEQ
    return (jnp.concatenate([c[:SEQ], jnp.tile(c[PAST_LEN:], (rep, 1))], axis=0),
            jnp.concatenate([s[:SEQ], jnp.tile(s[PAST_LEN:], (rep, 1))], axis=0))


def _layer(xp, xs, p_prompt, p_sample, cache_kv, cache_kr, state_conv,
           g_mix, w_in, b_gate, w_dw, b_dw, g_cn, b_cn, w_conv_out,
           g_qa, g_kva, w_qb, w_kb, w_vb, g_qn, g_kn, w_o, w_out,
           g_ffn, w_router, b_router, w_gu, b_gu, w_dn, b_dn,
           g_ple, w_ple_gate, w_ple):
    assert SEQ == PAST_LEN
    row = lambda v: v.reshape(1, -1)
    w_in_b = w_in.astype(BF16)
    w_mid = jnp.concatenate([w_in_b[:, O_U:O_KV], _rope_layout(w_in_b[:, O_KV:O_KR])], axis=1)
    w_gate = w_in_b[:, O_KR:]

    h, q_lat, kv_p, kv_s, kr_pad = _in_mid(xp, xs, row(g_mix), w_mid, row(g_qa), row(g_kva))
    half = ROPE_DIM // 2
    kr_new = jnp.concatenate([kr_pad[:, :half], kr_pad[:, 2 * half:3 * half]], axis=1)
    glu = _in_glu(h, w_in_b)

    hist = jnp.concatenate([jnp.zeros((BATCH, HALO, CONV_CHANNELS), F32),
                            jnp.pad(state_conv, ((0, 0), (HALO - (CONV_WIDTH - 1), 0), (0, 0)))], axis=0)
    c_act = _conv_module(glu, hist, w_dw, row(b_dw), row(g_cn), row(b_cn))

    cos_t, sin_t = _rope_tables()
    w_q = jnp.concatenate([w_qb[..., :NOPE_DIM], _rope_layout(w_qb[..., NOPE_DIM:])], axis=-1)
    w_q = w_q.reshape(Q_LORA_RANK, N_HEADS * HEAD_PAD).astype(BF16)
    g_q = jnp.concatenate([g_qn[:NOPE_DIM] * g_kn[:NOPE_DIM], _rope_layout(g_qn[NOPE_DIM:])]).reshape(1, HEAD_PAD)
    q = _q_heads(q_lat, w_q, g_q, cos_t, sin_t)

    w_kv = jnp.concatenate([w_kb, w_vb], axis=-1).reshape(KV_LORA_RANK, N_HEADS * HEAD_PAD).astype(BF16)
    g_kn_rope = _rope_layout(g_kn[NOPE_DIM:]).reshape(1, 128)
    k_new, v_new = _kv_heads(kv_p, kv_s, kr_pad, w_kv, g_kn_rope, cos_t, sin_t, _tab_idx_new, "kv_heads_new")
    attn = _flash_prompt(q, k_new, v_new)
    attn = _flash_sample(attn, q, cache_kv.reshape(DEC_BATCH * PAST_LEN, KV_LORA_RANK),
                         _rope_layout(cache_kr).reshape(DEC_BATCH * PAST_LEN, 128),
                         w_kv, g_kn_rope, cos_t, sin_t, k_new, v_new)

    mix = _merge(h, c_act, attn, w_gate, row(b_gate), w_conv_out.astype(BF16), w_o.astype(BF16))

    wr = jnp.pad(w_router, ((0, 0), (0, 128 - N_EXPERTS)))
    wr_hi, wr_lo = _split_bf16(wr)
    b_r = jnp.concatenate([b_router, jnp.full((128 - N_EXPERTS,), -jnp.inf, F32)]).reshape(1, 128)
    x1, hm, idx_pad, gate_pad = _out_router(mix, xp, xs, w_out.astype(BF16), row(g_ffn), wr_hi, wr_lo, b_r)

    top_idx = idx_pad[:, :TOP_K]
    dest, row_tok, counts, nblk, blk_start = _moe_dispatch(top_idx)
    b_gu3 = b_gu.reshape(N_EXPERTS, 1, 2 * D_FF)
    chunk_rows = _CHUNK_BLKS * MOE_BLK
    up_plans = _moe_plans(counts, nblk, blk_start, _UP_TILES, MOE_CHUNKS, _CHUNK_BLKS)
    down_plan = _moe_plans(counts, nblk, blk_start, _DN_TILES, 1, MOE_MAX_BLKS)[0]
    act = None
    for c in range(MOE_CHUNKS):
        xs = hm.at[row_tok[c * chunk_rows:(c + 1) * chunk_rows]].get(mode="promise_in_bounds")
        act = _moe_up(up_plans[c], act, xs, w_gu, b_gu3, c)
    ys = _moe_down(down_plan, act, w_dn, b_dn.reshape(N_EXPERTS, 1, D_MODEL))

    dest_t = dest.reshape(N_TOK, TOP_K).T
    fin = (row(g_ple), w_ple_gate.astype(BF16))
    w_ple_b = w_ple.astype(BF16)
    n_c = N_P // FIN_CHUNKS
    out_p = None
    for c in range(FIN_CHUNKS):
        y4 = ys.at[dest_t[:, c * n_c:(c + 1) * n_c]].get(mode="promise_in_bounds")
        out_p = _final(out_p, x1, y4, gate_pad, *fin, p_prompt, w_ple_b, c * n_c, c * n_c, n_c, N_P,
                       f"final_prompt_{c}")
    y4 = ys.at[dest_t[:, N_P:]].get(mode="promise_in_bounds")
    out_s = _final(None, x1, y4, gate_pad, *fin, p_sample, w_ple_b, N_P, 0, N_S, N_S, "final_sample")
    return out_p, out_s, kv_p, kv_s, kr_new, glu


def kernel(x_prompt, x_sample, cache_kv_latent, cache_k_rope, state_conv, p_prompt, p_sample, g_mix, w_in, b_gate, w_dw, b_dw, g_cn, b_cn, w_conv_out, g_qa, g_kva, w_qb, w_kb, w_vb, g_qn, g_kn, w_o, w_out, g_ffn, w_router, b_router, w_gu, b_gu, w_dn, b_dn, g_ple, w_ple_gate, w_ple):
    assert g_mix.shape[0] == 1
    out_p, out_s, kv_p, kv_s, kr_new, glu = _layer(
        x_prompt.reshape(N_P, D_MODEL), x_sample.reshape(N_S, D_MODEL),
        p_prompt[0].reshape(N_P, PLE_DIM), p_sample[0].reshape(N_S, PLE_DIM),
        cache_kv_latent[0], cache_k_rope[0], state_conv[0],
        g_mix[0], w_in[0], b_gate[0], w_dw[0], b_dw[0], g_cn[0], b_cn[0], w_conv_out[0],
        g_qa[0], g_kva[0], w_qb[0], w_kb[0], w_vb[0], g_qn[0], g_kn[0], w_o[0], w_out[0],
        g_ffn[0], w_router[0], b_router[0], w_gu[0], b_gu[0], w_dn[0], b_dn[0],
        g_ple[0], w_ple_gate[0], w_ple[0])
    tail = CONV_WIDTH - 1
    conv_p = jnp.stack([glu[(b + 1) * SEQ - tail:(b + 1) * SEQ] for b in range(BATCH)])
    conv_s = glu[N_P:].reshape(DEC_BATCH, DEC_SEQ, CONV_CHANNELS)[:, DEC_SEQ - tail:]
    return (out_p.reshape(BATCH, SEQ, D_MODEL),
            out_s.reshape(DEC_BATCH, DEC_SEQ, D_MODEL),
            kv_p.reshape(1, BATCH, SEQ, KV_LORA_RANK),
            kr_new[:N_P].reshape(1, BATCH, SEQ, ROPE_DIM),
            conv_p[None],
            kv_s.reshape(1, DEC_BATCH, DEC_SEQ, KV_LORA_RANK),
            kr_new[N_P:].reshape(1, DEC_BATCH, DEC_SEQ, ROPE_DIM),
            conv_s[None])
```

```python
import functools
import math

import jax
import jax.numpy as jnp
from jax import lax
from jax.experimental import pallas as pl
from jax.experimental.pallas import tpu as pltpu

F32 = jnp.float32
BF16 = jnp.bfloat16
I32 = jnp.int32
U32 = jnp.uint32

D_MODEL = 2048
BATCH = 2
SEQ = 4096
DEC_BATCH = 8
DEC_SEQ = 64
PAST_LEN = 4096
CHUNK = 64
CONV_CHANNELS = D_MODEL
CONV_WIDTH = 31
N_HEADS = 16
Q_LORA_RANK = 512
KV_LORA_RANK = 512
NOPE_DIM = 128
ROPE_DIM = 64
QK_DIM = NOPE_DIM + ROPE_DIM
V_DIM = 128
ROPE_THETA = 10000.0
N_EXPERTS = 32
TOP_K = 4
D_FF = D_MODEL
SWIGLU_ALPHA = 1.702
SWIGLU_LIMIT = 7.0
PLE_DIM = 256
EPS = 1e-6
NEG_INF = -1e30

N_P = BATCH * SEQ
N_S = DEC_BATCH * DEC_SEQ
N_TOK = N_P + N_S
O_U = 2 * CONV_CHANNELS
O_Q = O_U + Q_LORA_RANK
O_KV = O_Q + KV_LORA_RANK
O_KR = O_KV + ROPE_DIM
LANES = 128
SUBLANES = 8
MID_W = Q_LORA_RANK + KV_LORA_RANK + LANES
HEAD_PAD = NOPE_DIM + LANES

TM = 512
CONV_T = 64
HALO = 32
MOE_BLK = 512
MOE_MAX_BLKS = (N_TOK * TOP_K) // MOE_BLK + N_EXPERTS
MOE_ROWS = MOE_MAX_BLKS * MOE_BLK
VMEM_LIMIT = 48 * 1024 * 1024
assert 2 * ROPE_DIM == LANES and NOPE_DIM == LANES and V_DIM == LANES and SEQ == PAST_LEN


def _cparams(n_axes):
    return pltpu.CompilerParams(dimension_semantics=("arbitrary",) * n_axes,
                                vmem_limit_bytes=VMEM_LIMIT)


def _sigmoid(x):
    return 1.0 / (1.0 + jnp.exp(-x))


def _dot(a, b):
    return jnp.dot(a, b, preferred_element_type=F32)


def _stacked_rows(i, n_prompt_tiles, xp_ref, xs_ref):
    return jnp.where(i < n_prompt_tiles, xp_ref[...], xs_ref[...])


def _in_mid_kernel(xp_ref, xs_ref, g_ref, w_ref, gqa_ref, gkva_ref, h_ref, q_ref, kvp_ref, kvs_ref, kr_ref):
    i = pl.program_id(0)
    x = _stacked_rows(i, N_P // TM, xp_ref, xs_ref)
    h = x * lax.rsqrt(jnp.mean(x * x, axis=-1, keepdims=True) + EPS) * g_ref[...]
    hb = h.astype(BF16)
    h_ref[...] = hb
    z = _dot(hb, w_ref[...])
    ql = z[:, :Q_LORA_RANK]
    kvl = z[:, Q_LORA_RANK:Q_LORA_RANK + KV_LORA_RANK]
    qn = ql * lax.rsqrt(jnp.mean(ql * ql, axis=-1, keepdims=True) + EPS) * gqa_ref[...]
    q_ref[...] = qn.astype(BF16)
    kv = kvl * lax.rsqrt(jnp.mean(kvl * kvl, axis=-1, keepdims=True) + EPS) * gkva_ref[...]
    kr_ref[...] = z[:, Q_LORA_RANK + KV_LORA_RANK:]

    @pl.when(i < N_P // TM)
    def _():
        kvp_ref[...] = kv

    @pl.when(i >= N_P // TM)
    def _():
        kvs_ref[...] = kv


def _in_mid(xp, xs, g_mix, w_mid, g_qa, g_kva):
    n = N_TOK
    npt = N_P // TM
    return pl.pallas_call(
        _in_mid_kernel,
        grid=(n // TM,),
        in_specs=[
            pl.BlockSpec((TM, D_MODEL), lambda i: (jnp.minimum(i, npt - 1), 0)),
            pl.BlockSpec((TM, D_MODEL), lambda i: (jnp.maximum(i - npt, 0), 0)),
            pl.BlockSpec((1, D_MODEL), lambda i: (0, 0)),
            pl.BlockSpec((D_MODEL, MID_W), lambda i: (0, 0)),
            pl.BlockSpec((1, Q_LORA_RANK), lambda i: (0, 0)),
            pl.BlockSpec((1, KV_LORA_RANK), lambda i: (0, 0)),
        ],
        out_specs=[
            pl.BlockSpec((TM, D_MODEL), lambda i: (i, 0)),
            pl.BlockSpec((TM, Q_LORA_RANK), lambda i: (i, 0)),
            pl.BlockSpec((TM, KV_LORA_RANK), lambda i: (jnp.minimum(i, npt - 1), 0)),
            pl.BlockSpec((TM, KV_LORA_RANK), lambda i: (jnp.maximum(i - npt, 0), 0)),
            pl.BlockSpec((TM, LANES), lambda i: (i, 0)),
        ],
        out_shape=[
            jax.ShapeDtypeStruct((n, D_MODEL), BF16),
            jax.ShapeDtypeStruct((n, Q_LORA_RANK), BF16),
            jax.ShapeDtypeStruct((N_P, KV_LORA_RANK), F32),
            jax.ShapeDtypeStruct((N_S, KV_LORA_RANK), F32),
            jax.ShapeDtypeStruct((n, LANES), F32),
        ],
        compiler_params=_cparams(1),
        name="in_mid",
    )(xp, xs, g_mix, w_mid, g_qa, g_kva)


def _glu_kernel(h_ref, w1_ref, w2_ref, o_ref):
    h = h_ref[...]
    o_ref[...] = _dot(h, w1_ref[...]) * _sigmoid(_dot(h, w2_ref[...]))


def _in_glu(h, w_in_b):
    n = h.shape[0]
    tn = 1024
    nj = CONV_CHANNELS // tn
    return pl.pallas_call(
        _glu_kernel,
        grid=(n // TM, nj),
        in_specs=[
            pl.BlockSpec((TM, D_MODEL), lambda i, j: (i, 0)),
            pl.BlockSpec((D_MODEL, tn), lambda i, j: (0, j)),
            pl.BlockSpec((D_MODEL, tn), lambda i, j: (0, j + nj)),
        ],
        out_specs=pl.BlockSpec((TM, tn), lambda i, j: (i, j)),
        out_shape=jax.ShapeDtypeStruct((n, CONV_CHANNELS), F32),
        compiler_params=_cparams(2),
        name="in_glu",
    )(h, w_in_b, w_in_b)


_CONV_TILES_PER_SEQ = SEQ // CONV_T
_CONV_PROMPT_TILES = N_P // CONV_T
_CONV_LANES = 512
_SHIFT_ROWS = (HALO // SUBLANES - 1) * SUBLANES + CONV_T


def _conv_kernel(cur_ref, prev_ref, hist_ref, w_ref, bdw_ref, g_ref, b_ref, o_ref, win_ref, conv_ref, shift_ref):
    i = pl.program_id(0)
    first = jnp.logical_or(i >= _CONV_PROMPT_TILES, i % _CONV_TILES_PER_SEQ == 0)

    @pl.when(first)
    def _():
        win_ref[0:HALO, :] = hist_ref[0]

    @pl.when(jnp.logical_not(first))
    def _():
        win_ref[0:HALO, :] = prev_ref[...]

    win_ref[HALO:HALO + CONV_T, :] = cur_ref[...]
    for r in range(1, SUBLANES):
        shift_ref[r - 1] = win_ref[r:r + _SHIFT_ROWS, :]
    base = HALO - (CONV_WIDTH - 1)
    for c in range(0, CONV_CHANNELS, _CONV_LANES):
        acc = jnp.zeros((CONV_T, _CONV_LANES), F32)
        for k in range(CONV_WIDTH):
            q, r = divmod(base + k, SUBLANES)
            lanes = slice(c, c + _CONV_LANES)
            rows = slice(q * SUBLANES, q * SUBLANES + CONV_T)
            src = win_ref[rows, lanes] if r == 0 else shift_ref[r - 1, rows, lanes]
            acc = acc + w_ref[k:k + 1, lanes] * src
        conv_ref[:, c:c + _CONV_LANES] = acc + bdw_ref[:, c:c + _CONV_LANES]
    y = conv_ref[...]
    yc = y - jnp.mean(y, axis=-1, keepdims=True)
    var = jnp.mean(yc * yc, axis=-1, keepdims=True)
    z = yc * lax.rsqrt(var + EPS) * g_ref[...] + b_ref[...]
    o_ref[...] = (z * _sigmoid(z)).astype(BF16)


def _conv_module(glu, hist, w_dw, b_dw, g_cn, b_cn):
    n = glu.shape[0]
    n_tiles = n // CONV_T
    halo_per_tile = CONV_T // HALO

    def seq_of(i):
        return jnp.where(i < _CONV_PROMPT_TILES, i // _CONV_TILES_PER_SEQ, i - _CONV_PROMPT_TILES + BATCH)

    return pl.pallas_call(
        _conv_kernel,
        grid=(n_tiles,),
        in_specs=[
            pl.BlockSpec((CONV_T, CONV_CHANNELS), lambda i: (i, 0)),
            pl.BlockSpec((HALO, CONV_CHANNELS), lambda i: (jnp.maximum(i * halo_per_tile - 1, 0), 0)),
            pl.BlockSpec((1, HALO, CONV_CHANNELS), lambda i: (seq_of(i), 0, 0)),
            pl.BlockSpec((CONV_WIDTH, CONV_CHANNELS), lambda i: (0, 0)),
            pl.BlockSpec((1, CONV_CHANNELS), lambda i: (0, 0)),
            pl.BlockSpec((1, CONV_CHANNELS), lambda i: (0, 0)),
            pl.BlockSpec((1, CONV_CHANNELS), lambda i: (0, 0)),
        ],
        out_specs=pl.BlockSpec((CONV_T, CONV_CHANNELS), lambda i: (i, 0)),
        out_shape=jax.ShapeDtypeStruct((n, CONV_CHANNELS), BF16),
        scratch_shapes=[pltpu.VMEM((HALO + CONV_T, CONV_CHANNELS), F32),
                        pltpu.VMEM((CONV_T, CONV_CHANNELS), F32),
                        pltpu.VMEM((SUBLANES - 1, _SHIFT_ROWS, CONV_CHANNELS), F32)],
        compiler_params=_cparams(1),
        name="conv_module",
    )(glu, glu, hist, w_dw, b_dw, g_cn, b_cn)


ATT_TM = 512
_TAB_PROMPT_TILES = N_P // ATT_TM
_TAB_SEQ_TILES = SEQ // ATT_TM
_TAB_ROWS = SEQ + ATT_TM


def _tab_idx_new(i):
    return jnp.where(i < _TAB_PROMPT_TILES, i % _TAB_SEQ_TILES, _TAB_SEQ_TILES)


def _rope_pair(u, c, s):
    return u * c + pltpu.roll(u, LANES // 2, 1) * s


_Q_SCALE = math.log2(math.e) / math.sqrt(QK_DIM)


def _q_heads_kernel(ql_ref, w_ref, g_ref, c_ref, s_ref, o_ref):
    ql = ql_ref[...]
    g = g_ref[...]
    c = c_ref[...]
    s = s_ref[...]
    for h in range(N_HEADS):
        qf = _dot(ql, w_ref[:, h * HEAD_PAD:(h + 1) * HEAD_PAD])
        ssq = jnp.sum(qf * qf, axis=-1, keepdims=True)
        qn = qf * (lax.rsqrt(ssq * (1.0 / QK_DIM) + EPS) * _Q_SCALE) * g
        o_ref[h, :, :NOPE_DIM] = qn[:, :NOPE_DIM].astype(BF16)
        o_ref[h, :, NOPE_DIM:] = _rope_pair(qn[:, NOPE_DIM:], c, s).astype(BF16)


def _q_heads(q_lat, w_q, g_q, cos_t, sin_t):
    n = q_lat.shape[0]
    return pl.pallas_call(
        _q_heads_kernel,
        grid=(n // ATT_TM,),
        in_specs=[
            pl.BlockSpec((ATT_TM, Q_LORA_RANK), lambda i: (i, 0)),
            pl.BlockSpec((Q_LORA_RANK, N_HEADS * HEAD_PAD), lambda i: (0, 0)),
            pl.BlockSpec((1, HEAD_PAD), lambda i: (0, 0)),
            pl.BlockSpec((ATT_TM, LANES), lambda i: (_tab_idx_new(i), 0)),
            pl.BlockSpec((ATT_TM, LANES), lambda i: (_tab_idx_new(i), 0)),
        ],
        out_specs=pl.BlockSpec((N_HEADS, ATT_TM, HEAD_PAD), lambda i: (0, i, 0)),
        out_shape=jax.ShapeDtypeStruct((N_HEADS, n, HEAD_PAD), BF16),
        compiler_params=_cparams(1),
        name="q_heads",
    )(q_lat, w_q, g_q, cos_t, sin_t)


def _kv_heads_kernel(kvp_ref, kvs_ref, kr_ref, w_ref, gr_ref, c_ref, s_ref, k_ref, v_ref):
    kv = _stacked_rows(pl.program_id(0), N_P // ATT_TM, kvp_ref, kvs_ref).astype(BF16)
    u = kr_ref[...]
    ssq_r = jnp.sum(u * u, axis=-1, keepdims=True)
    krot = _rope_pair(u * gr_ref[...], c_ref[...], s_ref[...])
    for h in range(N_HEADS):
        z = _dot(kv, w_ref[:, h * HEAD_PAD:(h + 1) * HEAD_PAD])
        kn = z[:, :NOPE_DIM]
        ssq = jnp.sum(kn * kn, axis=-1, keepdims=True) + ssq_r
        scale = lax.rsqrt(ssq * (1.0 / QK_DIM) + EPS)
        k_ref[h, :, :NOPE_DIM] = (kn * scale).astype(BF16)
        k_ref[h, :, NOPE_DIM:] = (krot * scale).astype(BF16)
        v_ref[h] = z[:, NOPE_DIM:].astype(BF16)


def _kv_heads(kv_p, kv_s, kr_pad, w_kv, g_kn_rope, cos_t, sin_t, tab_idx, name):
    n = N_TOK
    npt = N_P // ATT_TM
    return pl.pallas_call(
        _kv_heads_kernel,
        grid=(n // ATT_TM,),
        in_specs=[
            pl.BlockSpec((ATT_TM, KV_LORA_RANK), lambda i: (jnp.minimum(i, npt - 1), 0)),
            pl.BlockSpec((ATT_TM, KV_LORA_RANK), lambda i: (jnp.maximum(i - npt, 0), 0)),
            pl.BlockSpec((ATT_TM, LANES), lambda i: (i, 0)),
            pl.BlockSpec((KV_LORA_RANK, N_HEADS * HEAD_PAD), lambda i: (0, 0)),
            pl.BlockSpec((1, LANES), lambda i: (0, 0)),
            pl.BlockSpec((ATT_TM, LANES), lambda i: (tab_idx(i), 0)),
            pl.BlockSpec((ATT_TM, LANES), lambda i: (tab_idx(i), 0)),
        ],
        out_specs=[
            pl.BlockSpec((N_HEADS, ATT_TM, HEAD_PAD), lambda i: (0, i, 0)),
            pl.BlockSpec((N_HEADS, ATT_TM, V_DIM), lambda i: (0, i, 0)),
        ],
        out_shape=[
            jax.ShapeDtypeStruct((N_HEADS, n, HEAD_PAD), BF16),
            jax.ShapeDtypeStruct((N_HEADS, n, V_DIM), BF16),
        ],
        compiler_params=_cparams(1),
        name=name,
    )(kv_p, kv_s, kr_pad, w_kv, g_kn_rope, cos_t, sin_t)


_TQ = 512
_TKB = 512
_HB = 4
_HBP = 4


def _flash_prompt_kernel(q_ref, k_ref, v_ref, o_ref, m_ref, l_ref, acc_ref):
    qi = pl.program_id(2)
    m_ref[...] = jnp.full(m_ref.shape, NEG_INF, F32)
    l_ref[...] = jnp.zeros(l_ref.shape, F32)
    acc_ref[...] = jnp.zeros(acc_ref.shape, F32)
    nlb = _TKB // LANES

    def step(ki, masked):
        start = pl.multiple_of(ki * _TKB, _TKB)
        scores = [lax.dot_general(q_ref[hh], k_ref[hh, pl.ds(start, _TKB), :], (((1,), (1,)), ((), ())),
                                  preferred_element_type=F32) for hh in range(_HBP)]
        probs = []
        for hh in range(_HBP):
            s = scores[hh]
            if masked:
                rc = lax.broadcasted_iota(I32, (_TQ, _TKB), 0) // CHUNK
                cc = lax.broadcasted_iota(I32, (_TQ, _TKB), 1) // CHUNK
                s = jnp.where(cc <= rc, s, NEG_INF)
            sb = [s[:, c * LANES:(c + 1) * LANES] for c in range(nlb)]
            bm = sb[0]
            for c in range(1, nlb):
                bm = jnp.maximum(bm, sb[c])
            m_prev = m_ref[hh]
            m_new = jnp.maximum(m_prev, jnp.max(bm, axis=-1, keepdims=True))
            alpha = jnp.exp2(m_prev - m_new)
            ps = [jnp.exp2(x - m_new) for x in sb]
            psum = ps[0]
            for c in range(1, nlb):
                psum = psum + ps[c]
            l_ref[hh] = alpha * l_ref[hh] + psum
            m_ref[hh] = m_new
            probs.append((alpha, jnp.concatenate(ps, axis=1).astype(BF16)))
        for hh in range(_HBP):
            alpha, p = probs[hh]
            acc_ref[hh] = alpha * acc_ref[hh] + _dot(p, v_ref[hh, pl.ds(start, _TKB), :])

    def body(ki, carry):
        step(ki, False)
        return carry

    lax.fori_loop(0, qi, body, 0)
    step(qi, True)
    for hh in range(_HBP):
        l = jnp.sum(l_ref[hh], axis=-1, keepdims=True)
        o_ref[:, hh * V_DIM:(hh + 1) * V_DIM] = (acc_ref[hh] / l).astype(BF16)


def _flash_prompt(q, k, v):
    nq = SEQ // _TQ
    return pl.pallas_call(
        _flash_prompt_kernel,
        grid=(BATCH, N_HEADS // _HBP, nq),
        in_specs=[
            pl.BlockSpec((_HBP, _TQ, HEAD_PAD), lambda b, h, i: (h, b * nq + i, 0)),
            pl.BlockSpec((_HBP, SEQ, HEAD_PAD), lambda b, h, i: (h, b, 0)),
            pl.BlockSpec((_HBP, SEQ, V_DIM), lambda b, h, i: (h, b, 0)),
        ],
        out_specs=pl.BlockSpec((_TQ, _HBP * V_DIM), lambda b, h, i: (b * nq + i, h)),
        out_shape=jax.ShapeDtypeStruct((N_TOK, N_HEADS * V_DIM), BF16),
        scratch_shapes=[pltpu.VMEM((_HBP, _TQ, LANES), F32), pltpu.VMEM((_HBP, _TQ, LANES), F32),
                        pltpu.VMEM((_HBP, _TQ, V_DIM), F32)],
        compiler_params=_cparams(3),
        name="flash_prompt",
    )(q, k, v)


_KC_ROWS = 512


def _flash_sample_kernel(prev_ref, q_ref, kv_ref, kr_ref, w_ref, gr_ref, c_ref, s_ref, kn_ref, vn_ref,
                         o_ref, kvb_ref, krot_ref, ssqr_ref, k_ref, v_ref):
    del prev_ref

    @pl.when(pl.program_id(1) == 0)
    def _():
        kvb_ref[...] = kv_ref[...].astype(BF16)
        u = kr_ref[...]
        ssqr_ref[...] = jnp.broadcast_to(jnp.sum(u * u, axis=-1, keepdims=True), ssqr_ref.shape)
        krot_ref[...] = _rope_pair(u * gr_ref[...], c_ref[...], s_ref[...])

    nt = (((1,), (1,)), ((), ()))
    for hh in range(_HB):
        w = w_ref[:, hh * HEAD_PAD:(hh + 1) * HEAD_PAD]
        for r in range(0, PAST_LEN, _KC_ROWS):
            rows = slice(r, r + _KC_ROWS)
            z = _dot(kvb_ref[rows, :], w)
            kn = z[:, :NOPE_DIM]
            ssq = jnp.sum(kn * kn, axis=-1, keepdims=True) + ssqr_ref[rows, :]
            scale = lax.rsqrt(ssq * (1.0 / QK_DIM) + EPS)
            k_ref[rows, :NOPE_DIM] = (kn * scale).astype(BF16)
            k_ref[rows, NOPE_DIM:] = (krot_ref[rows, :] * scale).astype(BF16)
            v_ref[rows, :] = z[:, NOPE_DIM:].astype(BF16)
        q = q_ref[hh]
        s1 = lax.dot_general(q, k_ref[...], nt, preferred_element_type=F32)
        s2 = lax.dot_general(q, kn_ref[hh], nt, preferred_element_type=F32)
        m = jnp.maximum(jnp.max(s1, axis=-1, keepdims=True), jnp.max(s2, axis=-1, keepdims=True))
        p1 = jnp.exp2(s1 - m)
        p2 = jnp.exp2(s2 - m)
        l = jnp.sum(p1, axis=-1, keepdims=True) + jnp.sum(p2, axis=-1, keepdims=True)
        o = _dot(p1.astype(BF16), v_ref[...]) + _dot(p2.astype(BF16), vn_ref[hh])
        o_ref[:, hh * V_DIM:(hh + 1) * V_DIM] = (o / l).astype(BF16)


def _flash_sample(attn, q, cache_kv, cache_kr_pad, w_kv, g_kn_rope, cos_t, sin_t, k_new, v_new):
    assert (PAST_LEN + DEC_SEQ - 1) // CHUNK <= PAST_LEN // CHUNK
    blk0 = N_P // DEC_SEQ
    new = lambda b, h: (h, blk0 + b, 0)
    const = lambda b, h: (0, 0)
    once = pl.Buffered(1)
    return pl.pallas_call(
        _flash_sample_kernel,
        grid=(DEC_BATCH, N_HEADS // _HB),
        in_specs=[
            pl.BlockSpec(memory_space=pl.ANY),
            pl.BlockSpec((_HB, DEC_SEQ, HEAD_PAD), new),
            pl.BlockSpec((PAST_LEN, KV_LORA_RANK), lambda b, h: (b, 0)),
            pl.BlockSpec((PAST_LEN, LANES), lambda b, h: (b, 0)),
            pl.BlockSpec((KV_LORA_RANK, _HB * HEAD_PAD), lambda b, h: (0, h)),
            pl.BlockSpec((1, LANES), const),
            pl.BlockSpec((PAST_LEN, LANES), const, pipeline_mode=once),
            pl.BlockSpec((PAST_LEN, LANES), const, pipeline_mode=once),
            pl.BlockSpec((_HB, DEC_SEQ, HEAD_PAD), new),
            pl.BlockSpec((_HB, DEC_SEQ, V_DIM), new),
        ],
        out_specs=pl.BlockSpec((DEC_SEQ, _HB * V_DIM), lambda b, h: (blk0 + b, h)),
        out_shape=jax.ShapeDtypeStruct((N_TOK, N_HEADS * V_DIM), BF16),
        scratch_shapes=[pltpu.VMEM((PAST_LEN, KV_LORA_RANK), BF16),
                        pltpu.VMEM((PAST_LEN, LANES), F32),
                        pltpu.VMEM((PAST_LEN, LANES), F32),
                        pltpu.VMEM((PAST_LEN, HEAD_PAD), BF16),
                        pltpu.VMEM((PAST_LEN, V_DIM), BF16)],
        input_output_aliases={0: 0},
        compiler_params=_cparams(2),
        name="flash_sample",
    )(attn, q, cache_kv, cache_kr_pad, w_kv, g_kn_rope, cos_t, sin_t, k_new, v_new)


def _merge_kernel(h_ref, c_ref, a_ref, wga_ref, wgb_ref, bga_ref, bgb_ref, wc_ref, wo_ref, o_ref):
    h = h_ref[...]
    ga = _sigmoid(_dot(h, wga_ref[...]) + bga_ref[...])
    gb = _sigmoid(_dot(h, wgb_ref[...]) + bgb_ref[...])
    mix = ga * _dot(c_ref[...], wc_ref[...]) + gb * _dot(a_ref[...], wo_ref[...])
    o_ref[...] = mix.astype(BF16)


def _merge(h, c_act, attn, w_gate, b_gate, w_conv_out, w_o):
    n = h.shape[0]
    tn = 512
    nj = D_MODEL // tn
    row = lambda i, j: (i, 0)
    return pl.pallas_call(
        _merge_kernel,
        grid=(n // TM, nj),
        in_specs=[
            pl.BlockSpec((TM, D_MODEL), row),
            pl.BlockSpec((TM, CONV_CHANNELS), row),
            pl.BlockSpec((TM, N_HEADS * V_DIM), row),
            pl.BlockSpec((D_MODEL, tn), lambda i, j: (0, j)),
            pl.BlockSpec((D_MODEL, tn), lambda i, j: (0, j + nj)),
            pl.BlockSpec((1, tn), lambda i, j: (0, j)),
            pl.BlockSpec((1, tn), lambda i, j: (0, j + nj)),
            pl.BlockSpec((CONV_CHANNELS, tn), lambda i, j: (0, j)),
            pl.BlockSpec((N_HEADS * V_DIM, tn), lambda i, j: (0, j)),
        ],
        out_specs=pl.BlockSpec((TM, tn), lambda i, j: (i, j)),
        out_shape=jax.ShapeDtypeStruct((n, D_MODEL), BF16),
        compiler_params=_cparams(2),
        name="merge",
    )(h, c_act, attn, w_gate, w_gate, b_gate, b_gate, w_conv_out, w_o)


def _split_bf16(x):
    hi = x.astype(BF16)
    lo = (x - hi.astype(F32)).astype(BF16)
    return hi, lo


_HALF = D_MODEL // 2


def _pack_bf16_pair(a, b):
    ua = lax.bitcast_convert_type(a.astype(BF16).astype(F32), U32)
    ub = lax.bitcast_convert_type(b.astype(BF16).astype(F32), U32)
    return lax.bitcast_convert_type(ua | (ub >> 16), F32)


def _unpack_bf16_pair(w):
    w = lax.bitcast_convert_type(w, U32)
    a = lax.bitcast_convert_type(w & jnp.uint32(0xFFFF0000), F32).astype(BF16)
    b = lax.bitcast_convert_type(w << 16, F32).astype(BF16)
    return a, b


def _out_router_kernel(n_tiles, n_prompt_tiles, mix_ref, xp_ref, xs_ref, w_ref, g_ref, wrh_ref, wrl_ref, br_ref,
                       x1_ref, hm_ref, idx_ref, gate_ref):
    i = pl.program_id(0)

    @pl.when(i < n_tiles)
    def _():
        x = _stacked_rows(i, n_prompt_tiles, xp_ref, xs_ref)
        _out_router_tile(mix_ref, x, w_ref, g_ref, wrh_ref, wrl_ref, br_ref,
                         x1_ref, hm_ref, idx_ref, gate_ref)

    @pl.when(i >= n_tiles)
    def _():
        hm_ref[...] = jnp.zeros(hm_ref.shape, F32)


def _out_router_tile(mix_ref, x, w_ref, g_ref, wrh_ref, wrl_ref, br_ref,
                     x1_ref, hm_ref, idx_ref, gate_ref):
    x1 = x + _dot(mix_ref[...], w_ref[...])
    x1_ref[...] = x1
    hn = x1 * lax.rsqrt(jnp.mean(x1 * x1, axis=-1, keepdims=True) + EPS) * g_ref[...]
    hm_ref[...] = _pack_bf16_pair(hn[:, :_HALF], hn[:, _HALF:])
    hh, hl = _split_bf16(hn)
    logits = _dot(hh, wrh_ref[...]) + (_dot(hh, wrl_ref[...]) + _dot(hl, wrh_ref[...])) + br_ref[...]
    lane = lax.broadcasted_iota(I32, logits.shape, 1).astype(F32)
    vals = []
    idx_out = jnp.zeros(logits.shape, F32)
    for k in range(TOP_K):
        m = jnp.max(logits, axis=-1, keepdims=True)
        sel = jnp.min(jnp.where(logits == m, lane, 1e9), axis=-1, keepdims=True)
        vals.append(m)
        idx_out = jnp.where(lane == float(k), sel, idx_out)
        logits = jnp.where(lane == sel, -jnp.inf, logits)
    exps = [jnp.exp(v - vals[0]) for v in vals]
    denom = exps[0] + exps[1] + exps[2] + exps[3]
    gate_out = jnp.zeros(idx_out.shape, F32)
    for k in range(TOP_K):
        gate_out = jnp.where(lane == float(k), exps[k] / denom, gate_out)
    idx_ref[...] = idx_out.astype(I32)
    gate_ref[...] = gate_out


def _out_router(mix, xp, xs, w_out, g_ffn, wr_hi, wr_lo, b_r):
    n = N_TOK
    tm = TM
    n_tiles = n // tm
    npt = N_P // tm
    const = lambda i: (0, 0)
    row = lambda i: (jnp.minimum(i, n_tiles - 1), 0)
    once = pl.Buffered(1)
    return pl.pallas_call(
        functools.partial(_out_router_kernel, n_tiles, npt),
        grid=(2 * n_tiles,),
        in_specs=[
            pl.BlockSpec((tm, D_MODEL), row),
            pl.BlockSpec((tm, D_MODEL), lambda i: (jnp.minimum(i, npt - 1), 0)),
            pl.BlockSpec((tm, D_MODEL), lambda i: (jnp.clip(i - npt, 0, N_S // tm - 1), 0)),
            pl.BlockSpec((D_MODEL, D_MODEL), const, pipeline_mode=once),
            pl.BlockSpec((1, D_MODEL), const),
            pl.BlockSpec((D_MODEL, LANES), const, pipeline_mode=once),
            pl.BlockSpec((D_MODEL, LANES), const, pipeline_mode=once),
            pl.BlockSpec((1, LANES), const),
        ],
        out_specs=[
            pl.BlockSpec((tm, D_MODEL), row),
            pl.BlockSpec((tm, _HALF), lambda i: (i, 0)),
            pl.BlockSpec((tm, LANES), row),
            pl.BlockSpec((tm, LANES), row),
        ],
        out_shape=[
            jax.ShapeDtypeStruct((n, D_MODEL), F32),
            jax.ShapeDtypeStruct((2 * n, _HALF), F32),
            jax.ShapeDtypeStruct((n, LANES), I32),
            jax.ShapeDtypeStruct((n, LANES), F32),
        ],
        compiler_params=_cparams(1),
        name="out_router",
    )(mix, xp, xs, w_out, g_ffn, wr_hi, wr_lo, b_r)


_F_VALID, _F_FIRST, _F_NEXT, _F_GROUP0, _F_SLOT = 1, 2, 4, 8, 16


_P_E, _P_W, _P_N, _P_B, _P_BI, _P_NE, _P_NW, _P_FL, _P_SUBS = range(9)
MOE_SUB = 128


def _stream_weights(t, plan_ref, copies, cast):
    flags = plan_ref[_P_FL, t]

    @pl.when((flags & _F_FIRST) != 0)
    def _():
        slot = (flags // _F_SLOT) & 1
        cur = copies(plan_ref[_P_E, t], plan_ref[_P_W, t], slot)

        @pl.when((flags & _F_GROUP0) != 0)
        def _():
            for c in cur:
                c.start()

        for c in cur:
            c.wait()

        @pl.when((flags & _F_NEXT) != 0)
        def _():
            for c in copies(plan_ref[_P_NE, t], plan_ref[_P_NW, t], 1 - slot):
                c.start()

        cast(slot)


def _for_used_rows(valid, subs, rows_body):
    for n_sub in range(1, MOE_BLK // MOE_SUB + 1):
        @pl.when(jnp.logical_and(valid, subs == n_sub))
        def _(m=n_sub * MOE_SUB):
            rows_body(m)


def _moe_up_kernel(plan_ref, prev_ref, x_ref, w_hbm, bg_ref, bu_ref, o_ref, wbuf_ref, wgb_ref, wub_ref, sem_ref):
    del prev_ref
    t = pl.program_id(0)

    def copies(e, w, slot):
        col = pl.multiple_of(w * _UP_TN, _UP_TN)
        return (pltpu.make_async_copy(w_hbm.at[e, :, pl.ds(col, _UP_TN)], wbuf_ref.at[slot, 0], sem_ref.at[slot, 0]),
                pltpu.make_async_copy(w_hbm.at[e, :, pl.ds(col + D_FF, _UP_TN)], wbuf_ref.at[slot, 1],
                                      sem_ref.at[slot, 1]))

    def cast(slot):
        wgb_ref[...] = wbuf_ref[slot, 0].astype(BF16)
        wub_ref[...] = wbuf_ref[slot, 1].astype(BF16)

    _stream_weights(t, plan_ref, copies, cast)
    valid = (plan_ref[_P_FL, t] & _F_VALID) != 0

    def rows_body(m):
        xa, xb = _unpack_bf16_pair(x_ref[:m, :])
        g = _dot(xa, wgb_ref[:_HALF, :]) + _dot(xb, wgb_ref[_HALF:, :]) + bg_ref[0]
        u = _dot(xa, wub_ref[:_HALF, :]) + _dot(xb, wub_ref[_HALF:, :]) + bu_ref[0]
        g = jnp.minimum(g, SWIGLU_LIMIT)
        u = jnp.clip(u, -SWIGLU_LIMIT, SWIGLU_LIMIT)
        o_ref[:m, :] = ((u + 1.0) * (g * _sigmoid(SWIGLU_ALPHA * g))).astype(BF16)
        if m < MOE_BLK:
            o_ref[m:, :] = jnp.zeros((MOE_BLK - m, o_ref.shape[1]), BF16)

    _for_used_rows(valid, plan_ref[_P_SUBS, t], rows_body)

    @pl.when(jnp.logical_not(valid))
    def _():
        o_ref[...] = jnp.zeros(o_ref.shape, BF16)


_UP_TN = 512
_UP_TILES = D_FF // _UP_TN
_DN_TN = 1024
_DN_TILES = D_MODEL // _DN_TN
MOE_CHUNKS = 4
_CHUNK_BLKS = MOE_MAX_BLKS // MOE_CHUNKS


def _moe_up(plan, act_prev, xs, w_gu, b_gu, chunk):
    steps = plan.shape[1]
    blk0 = chunk * _CHUNK_BLKS
    bspec = lambda off: pl.BlockSpec((1, 1, _UP_TN), lambda t, p: (p[_P_E, t], 0, p[_P_W, t] + off))
    aliases = {} if act_prev is None else {1: 0}
    prev = jnp.zeros((SUBLANES, LANES), BF16) if act_prev is None else act_prev
    return pl.pallas_call(
        _moe_up_kernel,
        grid_spec=pltpu.PrefetchScalarGridSpec(
            num_scalar_prefetch=1,
            grid=(steps,),
            in_specs=[
                pl.BlockSpec(memory_space=pl.ANY),
                pl.BlockSpec((MOE_BLK, _HALF), lambda t, p: (p[_P_BI, t], 0)),
                pl.BlockSpec(memory_space=pl.ANY),
                bspec(0), bspec(_UP_TILES),
            ],
            out_specs=pl.BlockSpec((MOE_BLK, _UP_TN),
                                   lambda t, p: (blk0 + p[_P_B, t], p[_P_N, t])),
            scratch_shapes=[pltpu.VMEM((2, 2, D_MODEL, _UP_TN), F32),
                            pltpu.VMEM((D_MODEL, _UP_TN), BF16), pltpu.VMEM((D_MODEL, _UP_TN), BF16),
                            pltpu.SemaphoreType.DMA((2, 2))],
        ),
        out_shape=jax.ShapeDtypeStruct((MOE_ROWS, D_FF), BF16),
        input_output_aliases=aliases,
        compiler_params=_cparams(1),
        name=f"moe_up_{chunk}",
    )(plan, prev, xs, w_gu, b_gu, b_gu)


_DN_HALF = _DN_TN // 2


def _moe_down_kernel(plan_ref, a_ref, w_hbm, b_ref, o_ref, wbuf_ref, wb_ref, sem_ref):
    t = pl.program_id(0)

    def copies(e, w, slot):
        col = pl.multiple_of(w * _DN_TN, _DN_TN)
        return (pltpu.make_async_copy(w_hbm.at[e, :, pl.ds(col, _DN_TN)], wbuf_ref.at[slot], sem_ref.at[slot]),)

    def cast(slot):
        wb_ref[...] = wbuf_ref[slot].astype(BF16)

    _stream_weights(t, plan_ref, copies, cast)
    valid = (plan_ref[_P_FL, t] & _F_VALID) != 0

    def rows_body(m):
        y = _dot(a_ref[:m, :], wb_ref[...]) + b_ref[0]
        o_ref[:m, :] = _pack_bf16_pair(y[:, :_DN_HALF], y[:, _DN_HALF:])
        if m < MOE_BLK:
            o_ref[m:, :] = jnp.zeros((MOE_BLK - m, o_ref.shape[1]), F32)

    _for_used_rows(valid, plan_ref[_P_SUBS, t], rows_body)

    @pl.when(jnp.logical_not(valid))
    def _():
        o_ref[...] = jnp.zeros(o_ref.shape, F32)


def _moe_down(plan, act, w_dn, b_dn):
    steps = plan.shape[1]
    return pl.pallas_call(
        _moe_down_kernel,
        grid_spec=pltpu.PrefetchScalarGridSpec(
            num_scalar_prefetch=1,
            grid=(steps,),
            in_specs=[
                pl.BlockSpec((MOE_BLK, D_FF), lambda t, p: (p[_P_BI, t], 0)),
                pl.BlockSpec(memory_space=pl.ANY),
                pl.BlockSpec((1, 1, _DN_TN), lambda t, p: (p[_P_E, t], 0, p[_P_W, t])),
            ],
            out_specs=pl.BlockSpec((MOE_BLK, _DN_HALF), lambda t, p: (p[_P_B, t], p[_P_N, t])),
            scratch_shapes=[pltpu.VMEM((2, D_FF, _DN_TN), F32), pltpu.VMEM((D_FF, _DN_TN), BF16),
                            pltpu.SemaphoreType.DMA((2,))],
        ),
        out_shape=jax.ShapeDtypeStruct((MOE_ROWS, _HALF), F32),
        compiler_params=_cparams(1),
        name="moe_down",
    )(plan, act, w_dn, b_dn)


def _moe_dispatch(top_idx):
    n_asg = N_TOK * TOP_K
    flat_e = top_idx.reshape(-1)
    onehot = (flat_e[:, None] == jnp.arange(N_EXPERTS, dtype=I32)[None, :]).astype(I32)
    csum = jnp.cumsum(onehot, axis=0)
    counts = csum[-1]
    rank = jnp.sum(csum * onehot, axis=1) - 1
    nblk = (counts + MOE_BLK - 1) // MOE_BLK
    blk_start = jnp.cumsum(nblk) - nblk
    dest = jnp.sum(onehot * blk_start[None, :], axis=1) * MOE_BLK + rank
    pad_src = jnp.arange(MOE_ROWS, dtype=I32) % N_TOK
    row_tok = pad_src.at[dest].set(jnp.arange(n_asg, dtype=I32) // TOP_K,
                                   mode="promise_in_bounds", unique_indices=True)
    return dest, row_tok, counts, nblk, blk_start


def _moe_steps(counts, nblk, blk_start, n_tiles, blk_lo, n_blks):
    t_max = n_tiles * n_blks
    lo = jnp.clip(blk_start, blk_lo, blk_lo + n_blks)
    hi = jnp.clip(blk_start + nblk, blk_lo, blk_lo + n_blks)
    nb_e = hi - lo
    per_e = nb_e * n_tiles
    s_end = jnp.cumsum(per_e)
    total = s_end[-1]
    t = jnp.arange(t_max, dtype=I32)
    tc = jnp.clip(t, 0, jnp.maximum(total - 1, 0))
    e = jnp.minimum(jnp.sum((s_end[None, :] <= tc[:, None]).astype(I32), axis=1), N_EXPERTS - 1)
    sel = (e[:, None] == jnp.arange(N_EXPERTS, dtype=I32)[None, :]).astype(I32)
    pick = lambda v: jnp.sum(sel * v[None, :], axis=1)
    local = tc - pick(s_end - per_e)
    nb = jnp.maximum(pick(nb_e), 1)
    w_tile = jnp.clip(local // nb, 0, n_tiles - 1)
    r = local % nb
    valid = t < total
    first = jnp.logical_and(valid, r == 0)
    fill = t - total
    blk = jnp.where(valid, pick(lo) - blk_lo + r, total // n_tiles + fill // n_tiles)
    rows_used = pick(counts) - (pick(lo) + r - pick(blk_start)) * MOE_BLK
    subs = jnp.clip((rows_used + MOE_SUB - 1) // MOE_SUB, 1, MOE_BLK // MOE_SUB)
    o_tile = jnp.where(valid, w_tile, fill % n_tiles)
    blk = jnp.clip(blk, 0, n_blks - 1)
    blk_in = jnp.where(valid, blk, jnp.maximum(total // n_tiles - 1, 0))
    ids = jnp.arange(N_EXPERTS, dtype=I32)
    owners = jnp.where(nb_e > 0, ids, N_EXPERTS)
    later = jnp.flip(lax.cummin(jnp.flip(owners)))
    next_owner = pick(jnp.concatenate([later[1:], jnp.full((1,), N_EXPERTS, I32)]))
    last_tile = w_tile == n_tiles - 1
    next_e = jnp.where(last_tile, next_owner, e)
    next_w = jnp.where(last_tile, 0, w_tile + 1)
    has_next = jnp.logical_and(first, next_e < N_EXPERTS)
    group = jnp.cumsum(first.astype(I32)) - 1
    flags = (valid * _F_VALID + first * _F_FIRST + has_next * _F_NEXT
             + jnp.logical_and(first, group == 0) * _F_GROUP0 + (group % 2) * _F_SLOT)
    rows = {_P_E: e, _P_W: w_tile, _P_N: o_tile, _P_B: blk, _P_BI: blk_in,
            _P_NE: jnp.minimum(next_e, N_EXPERTS - 1), _P_NW: next_w, _P_FL: flags, _P_SUBS: subs}
    return jnp.stack([rows[k].astype(I32) for k in range(len(rows))])


def _moe_plans(counts, nblk, blk_start, n_tiles, n_chunks, n_blks):
    los = jnp.arange(n_chunks, dtype=I32) * n_blks
    return jax.vmap(lambda lo: _moe_steps(counts, nblk, blk_start, n_tiles, lo, n_blks))(los)


_FIN_TM = 256
_FIN_TN = 512
FIN_CHUNKS = 4


def _unpack_expert_rows(words):
    u = lax.bitcast_convert_type(words, U32)
    hi = lax.bitcast_convert_type(u & jnp.uint32(0xFFFF0000), F32)
    lo = lax.bitcast_convert_type(u << 16, F32)
    parts = []
    for n in range(_DN_TILES):
        cols = slice(n * _DN_HALF, (n + 1) * _DN_HALF)
        parts += [hi[:, cols], lo[:, cols]]
    return jnp.concatenate(parts, axis=1)


def _final_kernel(prev_ref, x1_ref, y0_ref, y1_ref, y2_ref, y3_ref, gate_ref, g_ref, wg_ref, p_ref, wp_ref,
                  o_ref, x2_ref):
    del prev_ref
    gate = gate_ref[...]
    moe = (_unpack_expert_rows(y0_ref[0]) * gate[:, 0:1] + _unpack_expert_rows(y1_ref[0]) * gate[:, 1:2]
           + _unpack_expert_rows(y2_ref[0]) * gate[:, 2:3] + _unpack_expert_rows(y3_ref[0]) * gate[:, 3:4])
    x2 = x1_ref[...] + moe
    x2_ref[...] = x2
    hp = (x2 * lax.rsqrt(jnp.mean(x2 * x2, axis=-1, keepdims=True) + EPS) * g_ref[...]).astype(BF16)
    pb = p_ref[...].astype(BF16)
    for c in range(0, D_MODEL, _FIN_TN):
        cols = slice(c, c + _FIN_TN)
        emb = _dot(pb, wp_ref[:, cols])
        o_ref[:, cols] = x2_ref[:, cols] + _sigmoid(_dot(hp, wg_ref[:, cols])) * emb


def _final(out_prev, x1, y4, gate, g_ple, w_ple_gate, p, w_ple, tok0, out0, n, n_out, name):
    t0 = tok0 // _FIN_TM
    o0 = out0 // _FIN_TM
    pt0 = out0 // _FIN_TM
    const = lambda i: (0, 0)
    yspec = lambda k: pl.BlockSpec((1, _FIN_TM, _HALF), lambda i: (k, i, 0))
    once = pl.Buffered(1)
    aliases = {} if out_prev is None else {0: 0}
    prev = jnp.zeros((SUBLANES, LANES), F32) if out_prev is None else out_prev
    return pl.pallas_call(
        _final_kernel,
        grid=(n // _FIN_TM,),
        in_specs=[
            pl.BlockSpec(memory_space=pl.ANY),
            pl.BlockSpec((_FIN_TM, D_MODEL), lambda i: (t0 + i, 0)),
            yspec(0), yspec(1), yspec(2), yspec(3),
            pl.BlockSpec((_FIN_TM, LANES), lambda i: (t0 + i, 0)),
            pl.BlockSpec((1, D_MODEL), const),
            pl.BlockSpec((D_MODEL, D_MODEL), const, pipeline_mode=once),
            pl.BlockSpec((_FIN_TM, PLE_DIM), lambda i: (pt0 + i, 0)),
            pl.BlockSpec((PLE_DIM, D_MODEL), const, pipeline_mode=once),
        ],
        out_specs=pl.BlockSpec((_FIN_TM, D_MODEL), lambda i: (o0 + i, 0)),
        out_shape=jax.ShapeDtypeStruct((n_out, D_MODEL), F32),
        scratch_shapes=[pltpu.VMEM((_FIN_TM, D_MODEL), F32)],
        input_output_aliases=aliases,
        compiler_params=_cparams(1),
        name=name,
    )(prev, x1, y4, y4, y4, y4, gate, g_ple, w_ple_gate, p, w_ple)


def _rope_layout(x):
    half = ROPE_DIM // 2
    z = jnp.zeros(x.shape[:-1] + (half,), x.dtype)
    return jnp.concatenate([x[..., :half], z, x[..., half:], z], axis=-1)


def _rope_tables():
    half = ROPE_DIM // 2
    inv_freq = ROPE_THETA ** (-jnp.arange(half, dtype=F32) / half)
    pos = jnp.arange(PAST_LEN + DEC_SEQ, dtype=I32)
    ang = pos.astype(F32)[:, None] * inv_freq[None, :]
    cos, sin = jnp.cos(ang), jnp.sin(ang)
    z = jnp.zeros_like(cos)
    c = jnp.concatenate([cos, z, cos, z], axis=-1)
    s = jnp.concatenate([-sin, z, sin, z], axis=-1)
    rep = ATT_TM // DEC_SEQ
    return (jnp.concatenate([c[:SEQ], jnp.tile(c[PAST_LEN:], (rep, 1))], axis=0),
            jnp.concatenate([s[:SEQ], jnp.tile(s[PAST_LEN:], (rep, 1))], axis=0))


def _layer(xp, xs, p_prompt, p_sample, cache_kv, cache_kr, state_conv,
           g_mix, w_in, b_gate, w_dw, b_dw, g_cn, b_cn, w_conv_out,
           g_qa, g_kva, w_qb, w_kb, w_vb, g_qn, g_kn, w_o, w_out,
           g_ffn, w_router, b_router, w_gu, b_gu, w_dn, b_dn,
           g_ple, w_ple_gate, w_ple):
    assert SEQ == PAST_LEN
    row = lambda v: v.reshape(1, -1)
    w_in_b = w_in.astype(BF16)
    w_mid = jnp.concatenate([w_in_b[:, O_U:O_KV], _rope_layout(w_in_b[:, O_KV:O_KR])], axis=1)
    w_gate = w_in_b[:, O_KR:]

    h, q_lat, kv_p, kv_s, kr_pad = _in_mid(xp, xs, row(g_mix), w_mid, row(g_qa), row(g_kva))
    half = ROPE_DIM // 2
    kr_new = jnp.concatenate([kr_pad[:, :half], kr_pad[:, 2 * half:3 * half]], axis=1)
    glu = _in_glu(h, w_in_b)

    hist = jnp.concatenate([jnp.zeros((BATCH, HALO, CONV_CHANNELS), F32),
                            jnp.pad(state_conv, ((0, 0), (HALO - (CONV_WIDTH - 1), 0), (0, 0)))], axis=0)
    c_act = _conv_module(glu, hist, w_dw, row(b_dw), row(g_cn), row(b_cn))

    cos_t, sin_t = _rope_tables()
    w_q = jnp.concatenate([w_qb[..., :NOPE_DIM], _rope_layout(w_qb[..., NOPE_DIM:])], axis=-1)
    w_q = w_q.reshape(Q_LORA_RANK, N_HEADS * HEAD_PAD).astype(BF16)
    g_q = jnp.concatenate([g_qn[:NOPE_DIM] * g_kn[:NOPE_DIM], _rope_layout(g_qn[NOPE_DIM:])]).reshape(1, HEAD_PAD)
    q = _q_heads(q_lat, w_q, g_q, cos_t, sin_t)

    w_kv = jnp.concatenate([w_kb, w_vb], axis=-1).reshape(KV_LORA_RANK, N_HEADS * HEAD_PAD).astype(BF16)
    g_kn_rope = _rope_layout(g_kn[NOPE_DIM:]).reshape(1, LANES)
    k_new, v_new = _kv_heads(kv_p, kv_s, kr_pad, w_kv, g_kn_rope, cos_t, sin_t, _tab_idx_new, "kv_heads_new")
    attn = _flash_prompt(q, k_new, v_new)
    attn = _flash_sample(attn, q, cache_kv.reshape(DEC_BATCH * PAST_LEN, KV_LORA_RANK),
                         _rope_layout(cache_kr).reshape(DEC_BATCH * PAST_LEN, LANES),
                         w_kv, g_kn_rope, cos_t, sin_t, k_new, v_new)

    mix = _merge(h, c_act, attn, w_gate, row(b_gate), w_conv_out.astype(BF16), w_o.astype(BF16))

    wr = jnp.pad(w_router, ((0, 0), (0, LANES - N_EXPERTS)))
    wr_hi, wr_lo = _split_bf16(wr)
    b_r = jnp.concatenate([b_router, jnp.full((LANES - N_EXPERTS,), -jnp.inf, F32)]).reshape(1, LANES)
    x1, hm, idx_pad, gate_pad = _out_router(mix, xp, xs, w_out.astype(BF16), row(g_ffn), wr_hi, wr_lo, b_r)

    top_idx = idx_pad[:, :TOP_K]
    dest, row_tok, counts, nblk, blk_start = _moe_dispatch(top_idx)
    b_gu3 = b_gu.reshape(N_EXPERTS, 1, 2 * D_FF)
    chunk_rows = _CHUNK_BLKS * MOE_BLK
    up_plans = _moe_plans(counts, nblk, blk_start, _UP_TILES, MOE_CHUNKS, _CHUNK_BLKS)
    down_plan = _moe_plans(counts, nblk, blk_start, _DN_TILES, 1, MOE_MAX_BLKS)[0]
    act = None
    for c in range(MOE_CHUNKS):
        xs = hm.at[row_tok[c * chunk_rows:(c + 1) * chunk_rows]].get(mode="promise_in_bounds")
        act = _moe_up(up_plans[c], act, xs, w_gu, b_gu3, c)
    ys = _moe_down(down_plan, act, w_dn, b_dn.reshape(N_EXPERTS, 1, D_MODEL))

    dest_t = dest.reshape(N_TOK, TOP_K).T
    fin = (row(g_ple), w_ple_gate.astype(BF16))
    w_ple_b = w_ple.astype(BF16)
    n_c = N_P // FIN_CHUNKS
    out_p = None
    for c in range(FIN_CHUNKS):
        y4 = ys.at[dest_t[:, c * n_c:(c + 1) * n_c]].get(mode="promise_in_bounds")
        out_p = _final(out_p, x1, y4, gate_pad, *fin, p_prompt, w_ple_b, c * n_c, c * n_c, n_c, N_P,
                       f"final_prompt_{c}")
    y4 = ys.at[dest_t[:, N_P:]].get(mode="promise_in_bounds")
    out_s = _final(None, x1, y4, gate_pad, *fin, p_sample, w_ple_b, N_P, 0, N_S, N_S, "final_sample")
    return out_p, out_s, kv_p, kv_s, kr_new, glu


def kernel(x_prompt, x_sample, cache_kv_latent, cache_k_rope, state_conv, p_prompt, p_sample, g_mix, w_in, b_gate, w_dw, b_dw, g_cn, b_cn, w_conv_out, g_qa, g_kva, w_qb, w_kb, w_vb, g_qn, g_kn, w_o, w_out, g_ffn, w_router, b_router, w_gu, b_gu, w_dn, b_dn, g_ple, w_ple_gate, w_ple):
    assert g_mix.shape[0] == 1
    out_p, out_s, kv_p, kv_s, kr_new, glu = _layer(
        x_prompt.reshape(N_P, D_MODEL), x_sample.reshape(N_S, D_MODEL),
        p_prompt[0].reshape(N_P, PLE_DIM), p_sample[0].reshape(N_S, PLE_DIM),
        cache_kv_latent[0], cache_k_rope[0], state_conv[0],
        g_mix[0], w_in[0], b_gate[0], w_dw[0], b_dw[0], g_cn[0], b_cn[0], w_conv_out[0],
        g_qa[0], g_kva[0], w_qb[0], w_kb[0], w_vb[0], g_qn[0], g_kn[0], w_o[0], w_out[0],
        g_ffn[0], w_router[0], b_router[0], w_gu[0], b_gu[0], w_dn[0], b_dn[0],
        g_ple[0], w_ple_gate[0], w_ple[0])
    tail = CONV_WIDTH - 1
    conv_p = jnp.stack([glu[(b + 1) * SEQ - tail:(b + 1) * SEQ] for b in range(BATCH)])
    conv_s = glu[N_P:].reshape(DEC_BATCH, DEC_SEQ, CONV_CHANNELS)[:, DEC_SEQ - tail:]
    return (out_p.reshape(BATCH, SEQ, D_MODEL),
            out_s.reshape(DEC_BATCH, DEC_SEQ, D_MODEL),
            kv_p.reshape(1, BATCH, SEQ, KV_LORA_RANK),
            kr_new[:N_P].reshape(1, BATCH, SEQ, ROPE_DIM),
            conv_p[None],
            kv_s.reshape(1, DEC_BATCH, DEC_SEQ, KV_LORA_RANK),
            kr_new[N_P:].reshape(1, DEC_BATCH, DEC_SEQ, ROPE_DIM),
            conv_s[None])
```

```python
import functools
import math

import jax
import jax.numpy as jnp
from jax import lax
from jax.experimental import pallas as pl
from jax.experimental.pallas import tpu as pltpu

F32 = jnp.float32
BF16 = jnp.bfloat16
I32 = jnp.int32
U32 = jnp.uint32

D_MODEL = 2048
BATCH = 2
SEQ = 4096
DEC_BATCH = 8
DEC_SEQ = 64
PAST_LEN = 4096
CHUNK = 64
CONV_CHANNELS = D_MODEL
CONV_WIDTH = 31
N_HEADS = 16
Q_LORA_RANK = 512
KV_LORA_RANK = 512
NOPE_DIM = 128
ROPE_DIM = 64
QK_DIM = NOPE_DIM + ROPE_DIM
V_DIM = 128
ROPE_THETA = 10000.0
N_EXPERTS = 32
TOP_K = 4
D_FF = D_MODEL
SWIGLU_ALPHA = 1.702
SWIGLU_LIMIT = 7.0
PLE_DIM = 256
EPS = 1e-6
NEG_INF = -1e30

N_P = BATCH * SEQ
N_S = DEC_BATCH * DEC_SEQ
N_TOK = N_P + N_S
O_U = 2 * CONV_CHANNELS
O_Q = O_U + Q_LORA_RANK
O_KV = O_Q + KV_LORA_RANK
O_KR = O_KV + ROPE_DIM
LANES = 128
SUBLANES = 8
MID_W = Q_LORA_RANK + KV_LORA_RANK + LANES
HEAD_PAD = NOPE_DIM + LANES

TM = 512
CONV_T = 64
HALO = 32
MOE_BLK = 512
MOE_MAX_BLKS = (N_TOK * TOP_K) // MOE_BLK + N_EXPERTS
MOE_ROWS = MOE_MAX_BLKS * MOE_BLK
VMEM_LIMIT = 48 * 1024 * 1024
assert 2 * ROPE_DIM == LANES and NOPE_DIM == LANES and V_DIM == LANES and SEQ == PAST_LEN


def _cparams(n_axes):
    return pltpu.CompilerParams(dimension_semantics=("arbitrary",) * n_axes,
                                vmem_limit_bytes=VMEM_LIMIT)


def _sigmoid(x):
    return 1.0 / (1.0 + jnp.exp(-x))


def _dot(a, b):
    return jnp.dot(a, b, preferred_element_type=F32)


def _stacked_rows(i, n_prompt_tiles, xp_ref, xs_ref):
    return jnp.where(i < n_prompt_tiles, xp_ref[...], xs_ref[...])


def _in_mid_kernel(xp_ref, xs_ref, g_ref, w_ref, gqa_ref, gkva_ref, h_ref, q_ref, kvp_ref, kvs_ref, kr_ref):
    i = pl.program_id(0)
    x = _stacked_rows(i, N_P // TM, xp_ref, xs_ref)
    h = x * lax.rsqrt(jnp.mean(x * x, axis=-1, keepdims=True) + EPS) * g_ref[...]
    hb = h.astype(BF16)
    h_ref[...] = hb
    z = _dot(hb, w_ref[...])
    ql = z[:, :Q_LORA_RANK]
    kvl = z[:, Q_LORA_RANK:Q_LORA_RANK + KV_LORA_RANK]
    qn = ql * lax.rsqrt(jnp.mean(ql * ql, axis=-1, keepdims=True) + EPS) * gqa_ref[...]
    q_ref[...] = qn.astype(BF16)
    kv = kvl * lax.rsqrt(jnp.mean(kvl * kvl, axis=-1, keepdims=True) + EPS) * gkva_ref[...]
    kr_ref[...] = z[:, Q_LORA_RANK + KV_LORA_RANK:]

    @pl.when(i < N_P // TM)
    def _():
        kvp_ref[...] = kv

    @pl.when(i >= N_P // TM)
    def _():
        kvs_ref[...] = kv


def _in_mid(xp, xs, g_mix, w_mid, g_qa, g_kva):
    n = N_TOK
    npt = N_P // TM
    return pl.pallas_call(
        _in_mid_kernel,
        grid=(n // TM,),
        in_specs=[
            pl.BlockSpec((TM, D_MODEL), lambda i: (jnp.minimum(i, npt - 1), 0)),
            pl.BlockSpec((TM, D_MODEL), lambda i: (jnp.maximum(i - npt, 0), 0)),
            pl.BlockSpec((1, D_MODEL), lambda i: (0, 0)),
            pl.BlockSpec((D_MODEL, MID_W), lambda i: (0, 0)),
            pl.BlockSpec((1, Q_LORA_RANK), lambda i: (0, 0)),
            pl.BlockSpec((1, KV_LORA_RANK), lambda i: (0, 0)),
        ],
        out_specs=[
            pl.BlockSpec((TM, D_MODEL), lambda i: (i, 0)),
            pl.BlockSpec((TM, Q_LORA_RANK), lambda i: (i, 0)),
            pl.BlockSpec((TM, KV_LORA_RANK), lambda i: (jnp.minimum(i, npt - 1), 0)),
            pl.BlockSpec((TM, KV_LORA_RANK), lambda i: (jnp.maximum(i - npt, 0), 0)),
            pl.BlockSpec((TM, LANES), lambda i: (i, 0)),
        ],
        out_shape=[
            jax.ShapeDtypeStruct((n, D_MODEL), BF16),
            jax.ShapeDtypeStruct((n, Q_LORA_RANK), BF16),
            jax.ShapeDtypeStruct((N_P, KV_LORA_RANK), F32),
            jax.ShapeDtypeStruct((N_S, KV_LORA_RANK), F32),
            jax.ShapeDtypeStruct((n, LANES), F32),
        ],
        compiler_params=_cparams(1),
        name="in_mid",
    )(xp, xs, g_mix, w_mid, g_qa, g_kva)


def _glu_kernel(h_ref, w1_ref, w2_ref, o_ref):
    h = h_ref[...]
    o_ref[...] = _dot(h, w1_ref[...]) * _sigmoid(_dot(h, w2_ref[...]))


def _in_glu(h, w_in_b):
    n = h.shape[0]
    tn = 1024
    nj = CONV_CHANNELS // tn
    return pl.pallas_call(
        _glu_kernel,
        grid=(n // TM, nj),
        in_specs=[
            pl.BlockSpec((TM, D_MODEL), lambda i, j: (i, 0)),
            pl.BlockSpec((D_MODEL, tn), lambda i, j: (0, j)),
            pl.BlockSpec((D_MODEL, tn), lambda i, j: (0, j + nj)),
        ],
        out_specs=pl.BlockSpec((TM, tn), lambda i, j: (i, j)),
        out_shape=jax.ShapeDtypeStruct((n, CONV_CHANNELS), F32),
        compiler_params=_cparams(2),
        name="in_glu",
    )(h, w_in_b, w_in_b)


_CONV_TILES_PER_SEQ = SEQ // CONV_T
_CONV_PROMPT_TILES = N_P // CONV_T
_CONV_LANES = 512
_SHIFT_ROWS = (HALO // SUBLANES - 1) * SUBLANES + CONV_T


def _conv_kernel(cur_ref, prev_ref, hist_ref, w_ref, bdw_ref, g_ref, b_ref, o_ref, win_ref, conv_ref, shift_ref):
    i = pl.program_id(0)
    first = jnp.logical_or(i >= _CONV_PROMPT_TILES, i % _CONV_TILES_PER_SEQ == 0)

    @pl.when(first)
    def _():
        win_ref[0:HALO, :] = hist_ref[0]

    @pl.when(jnp.logical_not(first))
    def _():
        win_ref[0:HALO, :] = prev_ref[...]

    win_ref[HALO:HALO + CONV_T, :] = cur_ref[...]
    for r in range(1, SUBLANES):
        shift_ref[r - 1] = win_ref[r:r + _SHIFT_ROWS, :]
    base = HALO - (CONV_WIDTH - 1)
    for c in range(0, CONV_CHANNELS, _CONV_LANES):
        acc = jnp.zeros((CONV_T, _CONV_LANES), F32)
        for k in range(CONV_WIDTH):
            q, r = divmod(base + k, SUBLANES)
            lanes = slice(c, c + _CONV_LANES)
            rows = slice(q * SUBLANES, q * SUBLANES + CONV_T)
            src = win_ref[rows, lanes] if r == 0 else shift_ref[r - 1, rows, lanes]
            acc = acc + w_ref[k:k + 1, lanes] * src
        conv_ref[:, c:c + _CONV_LANES] = acc + bdw_ref[:, c:c + _CONV_LANES]
    y = conv_ref[...]
    yc = y - jnp.mean(y, axis=-1, keepdims=True)
    var = jnp.mean(yc * yc, axis=-1, keepdims=True)
    z = yc * lax.rsqrt(var + EPS) * g_ref[...] + b_ref[...]
    o_ref[...] = (z * _sigmoid(z)).astype(BF16)


def _conv_module(glu, hist, w_dw, b_dw, g_cn, b_cn):
    n = glu.shape[0]
    n_tiles = n // CONV_T
    halo_per_tile = CONV_T // HALO

    def seq_of(i):
        return jnp.where(i < _CONV_PROMPT_TILES, i // _CONV_TILES_PER_SEQ, i - _CONV_PROMPT_TILES + BATCH)

    return pl.pallas_call(
        _conv_kernel,
        grid=(n_tiles,),
        in_specs=[
            pl.BlockSpec((CONV_T, CONV_CHANNELS), lambda i: (i, 0)),
            pl.BlockSpec((HALO, CONV_CHANNELS), lambda i: (jnp.maximum(i * halo_per_tile - 1, 0), 0)),
            pl.BlockSpec((1, HALO, CONV_CHANNELS), lambda i: (seq_of(i), 0, 0)),
            pl.BlockSpec((CONV_WIDTH, CONV_CHANNELS), lambda i: (0, 0)),
            pl.BlockSpec((1, CONV_CHANNELS), lambda i: (0, 0)),
            pl.BlockSpec((1, CONV_CHANNELS), lambda i: (0, 0)),
            pl.BlockSpec((1, CONV_CHANNELS), lambda i: (0, 0)),
        ],
        out_specs=pl.BlockSpec((CONV_T, CONV_CHANNELS), lambda i: (i, 0)),
        out_shape=jax.ShapeDtypeStruct((n, CONV_CHANNELS), BF16),
        scratch_shapes=[pltpu.VMEM((HALO + CONV_T, CONV_CHANNELS), F32),
                        pltpu.VMEM((CONV_T, CONV_CHANNELS), F32),
                        pltpu.VMEM((SUBLANES - 1, _SHIFT_ROWS, CONV_CHANNELS), F32)],
        compiler_params=_cparams(1),
        name="conv_module",
    )(glu, glu, hist, w_dw, b_dw, g_cn, b_cn)


ATT_TM = 512
_TAB_PROMPT_TILES = N_P // ATT_TM
_TAB_SEQ_TILES = SEQ // ATT_TM
_TAB_ROWS = SEQ + ATT_TM


def _tab_idx_new(i):
    return jnp.where(i < _TAB_PROMPT_TILES, i % _TAB_SEQ_TILES, _TAB_SEQ_TILES)


def _rope_pair(u, c, s):
    return u * c + pltpu.roll(u, LANES // 2, 1) * s


_Q_SCALE = math.log2(math.e) / math.sqrt(QK_DIM)


def _q_heads_kernel(ql_ref, w_ref, g_ref, c_ref, s_ref, o_ref):
    ql = ql_ref[...]
    g = g_ref[...]
    c = c_ref[...]
    s = s_ref[...]
    for h in range(N_HEADS):
        qf = _dot(ql, w_ref[:, h * HEAD_PAD:(h + 1) * HEAD_PAD])
        ssq = jnp.sum(qf * qf, axis=-1, keepdims=True)
        qn = qf * (lax.rsqrt(ssq * (1.0 / QK_DIM) + EPS) * _Q_SCALE) * g
        o_ref[h, :, :NOPE_DIM] = qn[:, :NOPE_DIM].astype(BF16)
        o_ref[h, :, NOPE_DIM:] = _rope_pair(qn[:, NOPE_DIM:], c, s).astype(BF16)


def _q_heads(q_lat, w_q, g_q, cos_t, sin_t):
    n = q_lat.shape[0]
    return pl.pallas_call(
        _q_heads_kernel,
        grid=(n // ATT_TM,),
        in_specs=[
            pl.BlockSpec((ATT_TM, Q_LORA_RANK), lambda i: (i, 0)),
            pl.BlockSpec((Q_LORA_RANK, N_HEADS * HEAD_PAD), lambda i: (0, 0)),
            pl.BlockSpec((1, HEAD_PAD), lambda i: (0, 0)),
            pl.BlockSpec((ATT_TM, LANES), lambda i: (_tab_idx_new(i), 0)),
            pl.BlockSpec((ATT_TM, LANES), lambda i: (_tab_idx_new(i), 0)),
        ],
        out_specs=pl.BlockSpec((N_HEADS, ATT_TM, HEAD_PAD), lambda i: (0, i, 0)),
        out_shape=jax.ShapeDtypeStruct((N_HEADS, n, HEAD_PAD), BF16),
        compiler_params=_cparams(1),
        name="q_heads",
    )(q_lat, w_q, g_q, cos_t, sin_t)


def _kv_heads_kernel(kvp_ref, kvs_ref, kr_ref, w_ref, gr_ref, c_ref, s_ref, k_ref, v_ref):
    kv = _stacked_rows(pl.program_id(0), N_P // ATT_TM, kvp_ref, kvs_ref).astype(BF16)
    u = kr_ref[...]
    ssq_r = jnp.sum(u * u, axis=-1, keepdims=True)
    krot = _rope_pair(u * gr_ref[...], c_ref[...], s_ref[...])
    for h in range(N_HEADS):
        z = _dot(kv, w_ref[:, h * HEAD_PAD:(h + 1) * HEAD_PAD])
        kn = z[:, :NOPE_DIM]
        ssq = jnp.sum(kn * kn, axis=-1, keepdims=True) + ssq_r
        scale = lax.rsqrt(ssq * (1.0 / QK_DIM) + EPS)
        k_ref[h, :, :NOPE_DIM] = (kn * scale).astype(BF16)
        k_ref[h, :, NOPE_DIM:] = (krot * scale).astype(BF16)
        v_ref[h] = z[:, NOPE_DIM:].astype(BF16)


def _kv_heads(kv_p, kv_s, kr_pad, w_kv, g_kn_rope, cos_t, sin_t, tab_idx, name):
    n = N_TOK
    npt = N_P // ATT_TM
    return pl.pallas_call(
        _kv_heads_kernel,
        grid=(n // ATT_TM,),
        in_specs=[
            pl.BlockSpec((ATT_TM, KV_LORA_RANK), lambda i: (jnp.minimum(i, npt - 1), 0)),
            pl.BlockSpec((ATT_TM, KV_LORA_RANK), lambda i: (jnp.maximum(i - npt, 0), 0)),
            pl.BlockSpec((ATT_TM, LANES), lambda i: (i, 0)),
            pl.BlockSpec((KV_LORA_RANK, N_HEADS * HEAD_PAD), lambda i: (0, 0)),
            pl.BlockSpec((1, LANES), lambda i: (0, 0)),
            pl.BlockSpec((ATT_TM, LANES), lambda i: (tab_idx(i), 0)),
            pl.BlockSpec((ATT_TM, LANES), lambda i: (tab_idx(i), 0)),
        ],
        out_specs=[
            pl.BlockSpec((N_HEADS, ATT_TM, HEAD_PAD), lambda i: (0, i, 0)),
            pl.BlockSpec((N_HEADS, ATT_TM, V_DIM), lambda i: (0, i, 0)),
        ],
        out_shape=[
            jax.ShapeDtypeStruct((N_HEADS, n, HEAD_PAD), BF16),
            jax.ShapeDtypeStruct((N_HEADS, n, V_DIM), BF16),
        ],
        compiler_params=_cparams(1),
        name=name,
    )(kv_p, kv_s, kr_pad, w_kv, g_kn_rope, cos_t, sin_t)


_TQ = 512
_TKB = 512
_HB = 4
_HBP = 4


def _flash_prompt_kernel(q_ref, k_ref, v_ref, o_ref, m_ref, l_ref, acc_ref):
    qi = pl.program_id(2)
    m_ref[...] = jnp.full(m_ref.shape, NEG_INF, F32)
    l_ref[...] = jnp.zeros(l_ref.shape, F32)
    acc_ref[...] = jnp.zeros(acc_ref.shape, F32)
    nlb = _TKB // LANES

    def step(ki, masked):
        start = pl.multiple_of(ki * _TKB, _TKB)
        scores = [lax.dot_general(q_ref[hh], k_ref[hh, pl.ds(start, _TKB), :], (((1,), (1,)), ((), ())),
                                  preferred_element_type=F32) for hh in range(_HBP)]
        probs = []
        for hh in range(_HBP):
            s = scores[hh]
            if masked:
                rc = lax.broadcasted_iota(I32, (_TQ, _TKB), 0) // CHUNK
                cc = lax.broadcasted_iota(I32, (_TQ, _TKB), 1) // CHUNK
                s = jnp.where(cc <= rc, s, NEG_INF)
            sb = [s[:, c * LANES:(c + 1) * LANES] for c in range(nlb)]
            bm = sb[0]
            for c in range(1, nlb):
                bm = jnp.maximum(bm, sb[c])
            m_prev = m_ref[hh]
            m_new = jnp.maximum(m_prev, jnp.max(bm, axis=-1, keepdims=True))
            alpha = jnp.exp2(m_prev - m_new)
            ps = [jnp.exp2(x - m_new) for x in sb]
            psum = ps[0]
            for c in range(1, nlb):
                psum = psum + ps[c]
            l_ref[hh] = alpha * l_ref[hh] + psum
            m_ref[hh] = m_new
            probs.append((alpha, jnp.concatenate(ps, axis=1).astype(BF16)))
        for hh in range(_HBP):
            alpha, p = probs[hh]
            acc_ref[hh] = alpha * acc_ref[hh] + _dot(p, v_ref[hh, pl.ds(start, _TKB), :])

    def body(ki, carry):
        step(ki, False)
        return carry

    lax.fori_loop(0, qi, body, 0)
    step(qi, True)
    for hh in range(_HBP):
        l = jnp.sum(l_ref[hh], axis=-1, keepdims=True)
        o_ref[:, hh * V_DIM:(hh + 1) * V_DIM] = (acc_ref[hh] / l).astype(BF16)


def _flash_prompt(q, k, v):
    nq = SEQ // _TQ
    return pl.pallas_call(
        _flash_prompt_kernel,
        grid=(BATCH, N_HEADS // _HBP, nq),
        in_specs=[
            pl.BlockSpec((_HBP, _TQ, HEAD_PAD), lambda b, h, i: (h, b * nq + i, 0)),
            pl.BlockSpec((_HBP, SEQ, HEAD_PAD), lambda b, h, i: (h, b, 0)),
            pl.BlockSpec((_HBP, SEQ, V_DIM), lambda b, h, i: (h, b, 0)),
        ],
        out_specs=pl.BlockSpec((_TQ, _HBP * V_DIM), lambda b, h, i: (b * nq + i, h)),
        out_shape=jax.ShapeDtypeStruct((N_TOK, N_HEADS * V_DIM), BF16),
        scratch_shapes=[pltpu.VMEM((_HBP, _TQ, LANES), F32), pltpu.VMEM((_HBP, _TQ, LANES), F32),
                        pltpu.VMEM((_HBP, _TQ, V_DIM), F32)],
        compiler_params=_cparams(3),
        name="flash_prompt",
    )(q, k, v)


_KC_ROWS = 512


def _flash_sample_kernel(prev_ref, q_ref, kv_ref, kr_ref, w_ref, gr_ref, c_ref, s_ref, kn_ref, vn_ref,
                         o_ref, kvb_ref, krot_ref, ssqr_ref, k_ref, v_ref):
    del prev_ref

    @pl.when(pl.program_id(1) == 0)
    def _():
        kvb_ref[...] = kv_ref[...].astype(BF16)
        u = kr_ref[...]
        ssqr_ref[...] = jnp.broadcast_to(jnp.sum(u * u, axis=-1, keepdims=True), ssqr_ref.shape)
        krot_ref[...] = _rope_pair(u * gr_ref[...], c_ref[...], s_ref[...])

    nt = (((1,), (1,)), ((), ()))
    for hh in range(_HB):
        w = w_ref[:, hh * HEAD_PAD:(hh + 1) * HEAD_PAD]
        for r in range(0, PAST_LEN, _KC_ROWS):
            rows = slice(r, r + _KC_ROWS)
            z = _dot(kvb_ref[rows, :], w)
            kn = z[:, :NOPE_DIM]
            ssq = jnp.sum(kn * kn, axis=-1, keepdims=True) + ssqr_ref[rows, :]
            scale = lax.rsqrt(ssq * (1.0 / QK_DIM) + EPS)
            k_ref[rows, :NOPE_DIM] = (kn * scale).astype(BF16)
            k_ref[rows, NOPE_DIM:] = (krot_ref[rows, :] * scale).astype(BF16)
            v_ref[rows, :] = z[:, NOPE_DIM:].astype(BF16)
        q = q_ref[hh]
        s1 = lax.dot_general(q, k_ref[...], nt, preferred_element_type=F32)
        s2 = lax.dot_general(q, kn_ref[hh], nt, preferred_element_type=F32)
        m = jnp.maximum(jnp.max(s1, axis=-1, keepdims=True), jnp.max(s2, axis=-1, keepdims=True))
        p1 = jnp.exp2(s1 - m)
        p2 = jnp.exp2(s2 - m)
        l = jnp.sum(p1, axis=-1, keepdims=True) + jnp.sum(p2, axis=-1, keepdims=True)
        o = _dot(p1.astype(BF16), v_ref[...]) + _dot(p2.astype(BF16), vn_ref[hh])
        o_ref[:, hh * V_DIM:(hh + 1) * V_DIM] = (o / l).astype(BF16)


def _flash_sample(attn, q, cache_kv, cache_kr_pad, w_kv, g_kn_rope, cos_t, sin_t, k_new, v_new):
    assert (PAST_LEN + DEC_SEQ - 1) // CHUNK <= PAST_LEN // CHUNK
    blk0 = N_P // DEC_SEQ
    new = lambda b, h: (h, blk0 + b, 0)
    const = lambda b, h: (0, 0)
    once = pl.Buffered(1)
    return pl.pallas_call(
        _flash_sample_kernel,
        grid=(DEC_BATCH, N_HEADS // _HB),
        in_specs=[
            pl.BlockSpec(memory_space=pl.ANY),
            pl.BlockSpec((_HB, DEC_SEQ, HEAD_PAD), new),
            pl.BlockSpec((PAST_LEN, KV_LORA_RANK), lambda b, h: (b, 0)),
            pl.BlockSpec((PAST_LEN, LANES), lambda b, h: (b, 0)),
            pl.BlockSpec((KV_LORA_RANK, _HB * HEAD_PAD), lambda b, h: (0, h)),
            pl.BlockSpec((1, LANES), const),
            pl.BlockSpec((PAST_LEN, LANES), const, pipeline_mode=once),
            pl.BlockSpec((PAST_LEN, LANES), const, pipeline_mode=once),
            pl.BlockSpec((_HB, DEC_SEQ, HEAD_PAD), new),
            pl.BlockSpec((_HB, DEC_SEQ, V_DIM), new),
        ],
        out_specs=pl.BlockSpec((DEC_SEQ, _HB * V_DIM), lambda b, h: (blk0 + b, h)),
        out_shape=jax.ShapeDtypeStruct((N_TOK, N_HEADS * V_DIM), BF16),
        scratch_shapes=[pltpu.VMEM((PAST_LEN, KV_LORA_RANK), BF16),
                        pltpu.VMEM((PAST_LEN, LANES), F32),
                        pltpu.VMEM((PAST_LEN, LANES), F32),
                        pltpu.VMEM((PAST_LEN, HEAD_PAD), BF16),
                        pltpu.VMEM((PAST_LEN, V_DIM), BF16)],
        input_output_aliases={0: 0},
        compiler_params=_cparams(2),
        name="flash_sample",
    )(attn, q, cache_kv, cache_kr_pad, w_kv, g_kn_rope, cos_t, sin_t, k_new, v_new)


def _merge_kernel(h_ref, c_ref, a_ref, wga_ref, wgb_ref, bga_ref, bgb_ref, wc_ref, wo_ref, o_ref):
    h = h_ref[...]
    ga = _sigmoid(_dot(h, wga_ref[...]) + bga_ref[...])
    gb = _sigmoid(_dot(h, wgb_ref[...]) + bgb_ref[...])
    mix = ga * _dot(c_ref[...], wc_ref[...]) + gb * _dot(a_ref[...], wo_ref[...])
    o_ref[...] = mix.astype(BF16)


def _merge(h, c_act, attn, w_gate, b_gate, w_conv_out, w_o):
    n = h.shape[0]
    tn = 512
    nj = D_MODEL // tn
    row = lambda i, j: (i, 0)
    return pl.pallas_call(
        _merge_kernel,
        grid=(n // TM, nj),
        in_specs=[
            pl.BlockSpec((TM, D_MODEL), row),
            pl.BlockSpec((TM, CONV_CHANNELS), row),
            pl.BlockSpec((TM, N_HEADS * V_DIM), row),
            pl.BlockSpec((D_MODEL, tn), lambda i, j: (0, j)),
            pl.BlockSpec((D_MODEL, tn), lambda i, j: (0, j + nj)),
            pl.BlockSpec((1, tn), lambda i, j: (0, j)),
            pl.BlockSpec((1, tn), lambda i, j: (0, j + nj)),
            pl.BlockSpec((CONV_CHANNELS, tn), lambda i, j: (0, j)),
            pl.BlockSpec((N_HEADS * V_DIM, tn), lambda i, j: (0, j)),
        ],
        out_specs=pl.BlockSpec((TM, tn), lambda i, j: (i, j)),
        out_shape=jax.ShapeDtypeStruct((n, D_MODEL), BF16),
        compiler_params=_cparams(2),
        name="merge",
    )(h, c_act, attn, w_gate, w_gate, b_gate, b_gate, w_conv_out, w_o)


def _split_bf16(x):
    hi = x.astype(BF16)
    lo = (x - hi.astype(F32)).astype(BF16)
    return hi, lo


_HALF = D_MODEL // 2


def _pack_bf16_pair(a, b):
    ua = lax.bitcast_convert_type(a.astype(BF16).astype(F32), U32)
    ub = lax.bitcast_convert_type(b.astype(BF16).astype(F32), U32)
    return lax.bitcast_convert_type(ua | (ub >> 16), F32)


def _unpack_bf16_pair(w):
    w = lax.bitcast_convert_type(w, U32)
    a = lax.bitcast_convert_type(w & jnp.uint32(0xFFFF0000), F32).astype(BF16)
    b = lax.bitcast_convert_type(w << 16, F32).astype(BF16)
    return a, b


def _out_router_kernel(n_tiles, n_prompt_tiles, mix_ref, xp_ref, xs_ref, w_ref, g_ref, wrh_ref, wrl_ref, br_ref,
                       x1_ref, hm_ref, idx_ref, gate_ref):
    i = pl.program_id(0)

    @pl.when(i < n_tiles)
    def _():
        x = _stacked_rows(i, n_prompt_tiles, xp_ref, xs_ref)
        _out_router_tile(mix_ref, x, w_ref, g_ref, wrh_ref, wrl_ref, br_ref,
                         x1_ref, hm_ref, idx_ref, gate_ref)

    @pl.when(i >= n_tiles)
    def _():
        hm_ref[...] = jnp.zeros(hm_ref.shape, F32)


def _out_router_tile(mix_ref, x, w_ref, g_ref, wrh_ref, wrl_ref, br_ref,
                     x1_ref, hm_ref, idx_ref, gate_ref):
    x1 = x + _dot(mix_ref[...], w_ref[...])
    x1_ref[...] = x1
    hn = x1 * lax.rsqrt(jnp.mean(x1 * x1, axis=-1, keepdims=True) + EPS) * g_ref[...]
    hm_ref[...] = _pack_bf16_pair(hn[:, :_HALF], hn[:, _HALF:])
    hh, hl = _split_bf16(hn)
    logits = _dot(hh, wrh_ref[...]) + (_dot(hh, wrl_ref[...]) + _dot(hl, wrh_ref[...])) + br_ref[...]
    lane = lax.broadcasted_iota(I32, logits.shape, 1).astype(F32)
    vals = []
    idx_out = jnp.zeros(logits.shape, F32)
    for k in range(TOP_K):
        m = jnp.max(logits, axis=-1, keepdims=True)
        sel = jnp.min(jnp.where(logits == m, lane, 1e9), axis=-1, keepdims=True)
        vals.append(m)
        idx_out = jnp.where(lane == float(k), sel, idx_out)
        logits = jnp.where(lane == sel, -jnp.inf, logits)
    exps = [jnp.exp(v - vals[0]) for v in vals]
    denom = exps[0] + exps[1] + exps[2] + exps[3]
    gate_out = jnp.zeros(idx_out.shape, F32)
    for k in range(TOP_K):
        gate_out = jnp.where(lane == float(k), exps[k] / denom, gate_out)
    idx_ref[...] = idx_out.astype(I32)
    gate_ref[...] = gate_out


def _out_router(mix, xp, xs, w_out, g_ffn, wr_hi, wr_lo, b_r):
    n = N_TOK
    tm = TM
    n_tiles = n // tm
    npt = N_P // tm
    const = lambda i: (0, 0)
    row = lambda i: (jnp.minimum(i, n_tiles - 1), 0)
    once = pl.Buffered(1)
    return pl.pallas_call(
        functools.partial(_out_router_kernel, n_tiles, npt),
        grid=(2 * n_tiles,),
        in_specs=[
            pl.BlockSpec((tm, D_MODEL), row),
            pl.BlockSpec((tm, D_MODEL), lambda i: (jnp.minimum(i, npt - 1), 0)),
            pl.BlockSpec((tm, D_MODEL), lambda i: (jnp.clip(i - npt, 0, N_S // tm - 1), 0)),
            pl.BlockSpec((D_MODEL, D_MODEL), const, pipeline_mode=once),
            pl.BlockSpec((1, D_MODEL), const),
            pl.BlockSpec((D_MODEL, LANES), const, pipeline_mode=once),
            pl.BlockSpec((D_MODEL, LANES), const, pipeline_mode=once),
            pl.BlockSpec((1, LANES), const),
        ],
        out_specs=[
            pl.BlockSpec((tm, D_MODEL), row),
            pl.BlockSpec((tm, _HALF), lambda i: (i, 0)),
            pl.BlockSpec((tm, LANES), row),
            pl.BlockSpec((tm, LANES), row),
        ],
        out_shape=[
            jax.ShapeDtypeStruct((n, D_MODEL), F32),
            jax.ShapeDtypeStruct((2 * n, _HALF), F32),
            jax.ShapeDtypeStruct((n, LANES), I32),
            jax.ShapeDtypeStruct((n, LANES), F32),
        ],
        compiler_params=_cparams(1),
        name="out_router",
    )(mix, xp, xs, w_out, g_ffn, wr_hi, wr_lo, b_r)


_F_VALID, _F_FIRST, _F_NEXT, _F_GROUP0 = 1, 2, 4, 8


_P_E, _P_W, _P_N, _P_B, _P_BI, _P_NE, _P_NW, _P_FL, _P_SUBS = range(9)
MOE_SUB = 128


def _stream_weights(t, plan_ref, copies, cast):
    flags = plan_ref[_P_FL, t]

    @pl.when((flags & _F_FIRST) != 0)
    def _():
        cur = copies(plan_ref[_P_E, t], plan_ref[_P_W, t])

        @pl.when((flags & _F_GROUP0) != 0)
        def _():
            for c in cur:
                c.start()

        for c in cur:
            c.wait()
        cast()

        @pl.when((flags & _F_NEXT) != 0)
        def _():
            for c in copies(plan_ref[_P_NE, t], plan_ref[_P_NW, t]):
                c.start()


def _for_used_rows(valid, subs, rows_body):
    for n_sub in range(1, MOE_BLK // MOE_SUB + 1):
        @pl.when(jnp.logical_and(valid, subs == n_sub))
        def _(m=n_sub * MOE_SUB):
            rows_body(m)


def _moe_up_kernel(plan_ref, prev_ref, x_ref, w_hbm, bg_ref, bu_ref, o_ref, wbuf_ref, wgb_ref, wub_ref, sem_ref):
    del prev_ref
    t = pl.program_id(0)

    def copies(e, w):
        col = pl.multiple_of(w * _UP_TN, _UP_TN)
        return (pltpu.make_async_copy(w_hbm.at[e, :, pl.ds(col, _UP_TN)], wbuf_ref.at[0], sem_ref.at[0]),
                pltpu.make_async_copy(w_hbm.at[e, :, pl.ds(col + D_FF, _UP_TN)], wbuf_ref.at[1], sem_ref.at[1]))

    def cast():
        wgb_ref[...] = wbuf_ref[0].astype(BF16)
        wub_ref[...] = wbuf_ref[1].astype(BF16)

    _stream_weights(t, plan_ref, copies, cast)
    valid = (plan_ref[_P_FL, t] & _F_VALID) != 0

    def rows_body(m):
        xa, xb = _unpack_bf16_pair(x_ref[:m, :])
        g = _dot(xa, wgb_ref[:_HALF, :]) + _dot(xb, wgb_ref[_HALF:, :]) + bg_ref[0]
        u = _dot(xa, wub_ref[:_HALF, :]) + _dot(xb, wub_ref[_HALF:, :]) + bu_ref[0]
        g = jnp.minimum(g, SWIGLU_LIMIT)
        u = jnp.clip(u, -SWIGLU_LIMIT, SWIGLU_LIMIT)
        o_ref[:m, :] = ((u + 1.0) * (g * _sigmoid(SWIGLU_ALPHA * g))).astype(BF16)
        if m < MOE_BLK:
            o_ref[m:, :] = jnp.zeros((MOE_BLK - m, o_ref.shape[1]), BF16)

    _for_used_rows(valid, plan_ref[_P_SUBS, t], rows_body)

    @pl.when(jnp.logical_not(valid))
    def _():
        o_ref[...] = jnp.zeros(o_ref.shape, BF16)


_UP_TN = 1024
_UP_TILES = D_FF // _UP_TN
_DN_TN = 2048
_DN_TILES = D_MODEL // _DN_TN
MOE_CHUNKS = 4
_CHUNK_BLKS = MOE_MAX_BLKS // MOE_CHUNKS


def _moe_up(plan, act_prev, xs, w_gu, b_gu, chunk):
    steps = plan.shape[1]
    blk0 = chunk * _CHUNK_BLKS
    bspec = lambda off: pl.BlockSpec((1, 1, _UP_TN), lambda t, p: (p[_P_E, t], 0, p[_P_W, t] + off))
    aliases = {} if act_prev is None else {1: 0}
    prev = jnp.zeros((SUBLANES, LANES), BF16) if act_prev is None else act_prev
    return pl.pallas_call(
        _moe_up_kernel,
        grid_spec=pltpu.PrefetchScalarGridSpec(
            num_scalar_prefetch=1,
            grid=(steps,),
            in_specs=[
                pl.BlockSpec(memory_space=pl.ANY),
                pl.BlockSpec((MOE_BLK, _HALF), lambda t, p: (p[_P_BI, t], 0)),
                pl.BlockSpec(memory_space=pl.ANY),
                bspec(0), bspec(_UP_TILES),
            ],
            out_specs=pl.BlockSpec((MOE_BLK, _UP_TN),
                                   lambda t, p: (blk0 + p[_P_B, t], p[_P_N, t])),
            scratch_shapes=[pltpu.VMEM((2, D_MODEL, _UP_TN), F32),
                            pltpu.VMEM((D_MODEL, _UP_TN), BF16), pltpu.VMEM((D_MODEL, _UP_TN), BF16),
                            pltpu.SemaphoreType.DMA((2,))],
        ),
        out_shape=jax.ShapeDtypeStruct((MOE_ROWS, D_FF), BF16),
        input_output_aliases=aliases,
        compiler_params=_cparams(1),
        name=f"moe_up_{chunk}",
    )(plan, prev, xs, w_gu, b_gu, b_gu)


_DN_HALF = _DN_TN // 2


def _moe_down_kernel(plan_ref, a_ref, w_hbm, b_ref, o_ref, wbuf_ref, wb_ref, sem_ref):
    t = pl.program_id(0)

    def copies(e, w):
        col = pl.multiple_of(w * _DN_TN, _DN_TN)
        return (pltpu.make_async_copy(w_hbm.at[e, :, pl.ds(col, _DN_TN)], wbuf_ref, sem_ref.at[0]),)

    def cast():
        wb_ref[...] = wbuf_ref[...].astype(BF16)

    _stream_weights(t, plan_ref, copies, cast)
    valid = (plan_ref[_P_FL, t] & _F_VALID) != 0

    def rows_body(m):
        y = _dot(a_ref[:m, :], wb_ref[...]) + b_ref[0]
        o_ref[:m, :] = _pack_bf16_pair(y[:, :_DN_HALF], y[:, _DN_HALF:])
        if m < MOE_BLK:
            o_ref[m:, :] = jnp.zeros((MOE_BLK - m, o_ref.shape[1]), F32)

    _for_used_rows(valid, plan_ref[_P_SUBS, t], rows_body)

    @pl.when(jnp.logical_not(valid))
    def _():
        o_ref[...] = jnp.zeros(o_ref.shape, F32)


def _moe_down(plan, act, w_dn, b_dn):
    steps = plan.shape[1]
    return pl.pallas_call(
        _moe_down_kernel,
        grid_spec=pltpu.PrefetchScalarGridSpec(
            num_scalar_prefetch=1,
            grid=(steps,),
            in_specs=[
                pl.BlockSpec((MOE_BLK, D_FF), lambda t, p: (p[_P_BI, t], 0)),
                pl.BlockSpec(memory_space=pl.ANY),
                pl.BlockSpec((1, 1, _DN_TN), lambda t, p: (p[_P_E, t], 0, p[_P_W, t])),
            ],
            out_specs=pl.BlockSpec((MOE_BLK, _DN_HALF), lambda t, p: (p[_P_B, t], p[_P_N, t])),
            scratch_shapes=[pltpu.VMEM((D_FF, _DN_TN), F32), pltpu.VMEM((D_FF, _DN_TN), BF16),
                            pltpu.SemaphoreType.DMA((1,))],
        ),
        out_shape=jax.ShapeDtypeStruct((MOE_ROWS, _HALF), F32),
        compiler_params=_cparams(1),
        name="moe_down",
    )(plan, act, w_dn, b_dn)


def _moe_dispatch(top_idx):
    n_asg = N_TOK * TOP_K
    flat_e = top_idx.reshape(-1)
    onehot = (flat_e[:, None] == jnp.arange(N_EXPERTS, dtype=I32)[None, :]).astype(I32)
    csum = jnp.cumsum(onehot, axis=0)
    counts = csum[-1]
    rank = jnp.sum(csum * onehot, axis=1) - 1
    nblk = (counts + MOE_BLK - 1) // MOE_BLK
    blk_start = jnp.cumsum(nblk) - nblk
    dest = jnp.sum(onehot * blk_start[None, :], axis=1) * MOE_BLK + rank
    pad_src = jnp.arange(MOE_ROWS, dtype=I32) % N_TOK
    row_tok = pad_src.at[dest].set(jnp.arange(n_asg, dtype=I32) // TOP_K,
                                   mode="promise_in_bounds", unique_indices=True)
    return dest, row_tok, counts, nblk, blk_start


def _moe_steps(counts, nblk, blk_start, n_tiles, blk_lo, n_blks):
    t_max = n_tiles * n_blks
    lo = jnp.clip(blk_start, blk_lo, blk_lo + n_blks)
    hi = jnp.clip(blk_start + nblk, blk_lo, blk_lo + n_blks)
    nb_e = hi - lo
    per_e = nb_e * n_tiles
    s_end = jnp.cumsum(per_e)
    total = s_end[-1]
    t = jnp.arange(t_max, dtype=I32)
    tc = jnp.clip(t, 0, jnp.maximum(total - 1, 0))
    e = jnp.minimum(jnp.sum((s_end[None, :] <= tc[:, None]).astype(I32), axis=1), N_EXPERTS - 1)
    sel = (e[:, None] == jnp.arange(N_EXPERTS, dtype=I32)[None, :]).astype(I32)
    pick = lambda v: jnp.sum(sel * v[None, :], axis=1)
    local = tc - pick(s_end - per_e)
    nb = jnp.maximum(pick(nb_e), 1)
    w_tile = jnp.clip(local // nb, 0, n_tiles - 1)
    r = local % nb
    valid = t < total
    first = jnp.logical_and(valid, r == 0)
    fill = t - total
    blk = jnp.where(valid, pick(lo) - blk_lo + r, total // n_tiles + fill // n_tiles)
    rows_used = pick(counts) - (pick(lo) + r - pick(blk_start)) * MOE_BLK
    subs = jnp.clip((rows_used + MOE_SUB - 1) // MOE_SUB, 1, MOE_BLK // MOE_SUB)
    o_tile = jnp.where(valid, w_tile, fill % n_tiles)
    blk = jnp.clip(blk, 0, n_blks - 1)
    blk_in = jnp.where(valid, blk, jnp.maximum(total // n_tiles - 1, 0))
    ids = jnp.arange(N_EXPERTS, dtype=I32)
    owners = jnp.where(nb_e > 0, ids, N_EXPERTS)
    later = jnp.flip(lax.cummin(jnp.flip(owners)))
    next_owner = pick(jnp.concatenate([later[1:], jnp.full((1,), N_EXPERTS, I32)]))
    last_tile = w_tile == n_tiles - 1
    next_e = jnp.where(last_tile, next_owner, e)
    next_w = jnp.where(last_tile, 0, w_tile + 1)
    has_next = jnp.logical_and(first, next_e < N_EXPERTS)
    group = jnp.cumsum(first.astype(I32)) - 1
    flags = (valid * _F_VALID + first * _F_FIRST + has_next * _F_NEXT
             + jnp.logical_and(first, group == 0) * _F_GROUP0)
    rows = {_P_E: e, _P_W: w_tile, _P_N: o_tile, _P_B: blk, _P_BI: blk_in,
            _P_NE: jnp.minimum(next_e, N_EXPERTS - 1), _P_NW: next_w, _P_FL: flags, _P_SUBS: subs}
    return jnp.stack([rows[k].astype(I32) for k in range(len(rows))])


def _moe_plans(counts, nblk, blk_start, n_tiles, n_chunks, n_blks):
    los = jnp.arange(n_chunks, dtype=I32) * n_blks
    return jax.vmap(lambda lo: _moe_steps(counts, nblk, blk_start, n_tiles, lo, n_blks))(los)


_FIN_TM = 256
_FIN_TN = 512
FIN_CHUNKS = 4


def _unpack_expert_rows(words):
    u = lax.bitcast_convert_type(words, U32)
    hi = lax.bitcast_convert_type(u & jnp.uint32(0xFFFF0000), F32)
    lo = lax.bitcast_convert_type(u << 16, F32)
    parts = []
    for n in range(_DN_TILES):
        cols = slice(n * _DN_HALF, (n + 1) * _DN_HALF)
        parts += [hi[:, cols], lo[:, cols]]
    return jnp.concatenate(parts, axis=1)


def _final_kernel(prev_ref, x1_ref, y0_ref, y1_ref, y2_ref, y3_ref, gate_ref, g_ref, wg_ref, p_ref, wp_ref,
                  o_ref, x2_ref):
    del prev_ref
    gate = gate_ref[...]
    moe = (_unpack_expert_rows(y0_ref[0]) * gate[:, 0:1] + _unpack_expert_rows(y1_ref[0]) * gate[:, 1:2]
           + _unpack_expert_rows(y2_ref[0]) * gate[:, 2:3] + _unpack_expert_rows(y3_ref[0]) * gate[:, 3:4])
    x2 = x1_ref[...] + moe
    x2_ref[...] = x2
    hp = (x2 * lax.rsqrt(jnp.mean(x2 * x2, axis=-1, keepdims=True) + EPS) * g_ref[...]).astype(BF16)
    pb = p_ref[...].astype(BF16)
    for c in range(0, D_MODEL, _FIN_TN):
        cols = slice(c, c + _FIN_TN)
        emb = _dot(pb, wp_ref[:, cols])
        o_ref[:, cols] = x2_ref[:, cols] + _sigmoid(_dot(hp, wg_ref[:, cols])) * emb


def _final(out_prev, x1, y4, gate, g_ple, w_ple_gate, p, w_ple, tok0, out0, n, n_out, name):
    t0 = tok0 // _FIN_TM
    o0 = out0 // _FIN_TM
    pt0 = out0 // _FIN_TM
    const = lambda i: (0, 0)
    yspec = lambda k: pl.BlockSpec((1, _FIN_TM, _HALF), lambda i: (k, i, 0))
    once = pl.Buffered(1)
    aliases = {} if out_prev is None else {0: 0}
    prev = jnp.zeros((SUBLANES, LANES), F32) if out_prev is None else out_prev
    return pl.pallas_call(
        _final_kernel,
        grid=(n // _FIN_TM,),
        in_specs=[
            pl.BlockSpec(memory_space=pl.ANY),
            pl.BlockSpec((_FIN_TM, D_MODEL), lambda i: (t0 + i, 0)),
            yspec(0), yspec(1), yspec(2), yspec(3),
            pl.BlockSpec((_FIN_TM, LANES), lambda i: (t0 + i, 0)),
            pl.BlockSpec((1, D_MODEL), const),
            pl.BlockSpec((D_MODEL, D_MODEL), const, pipeline_mode=once),
            pl.BlockSpec((_FIN_TM, PLE_DIM), lambda i: (pt0 + i, 0)),
            pl.BlockSpec((PLE_DIM, D_MODEL), const, pipeline_mode=once),
        ],
        out_specs=pl.BlockSpec((_FIN_TM, D_MODEL), lambda i: (o0 + i, 0)),
        out_shape=jax.ShapeDtypeStruct((n_out, D_MODEL), F32),
        scratch_shapes=[pltpu.VMEM((_FIN_TM, D_MODEL), F32)],
        input_output_aliases=aliases,
        compiler_params=_cparams(1),
        name=name,
    )(prev, x1, y4, y4, y4, y4, gate, g_ple, w_ple_gate, p, w_ple)


def _rope_layout(x):
    half = ROPE_DIM // 2
    z = jnp.zeros(x.shape[:-1] + (half,), x.dtype)
    return jnp.concatenate([x[..., :half], z, x[..., half:], z], axis=-1)


def _rope_tables():
    half = ROPE_DIM // 2
    inv_freq = ROPE_THETA ** (-jnp.arange(half, dtype=F32) / half)
    pos = jnp.arange(PAST_LEN + DEC_SEQ, dtype=I32)
    ang = pos.astype(F32)[:, None] * inv_freq[None, :]
    cos, sin = jnp.cos(ang), jnp.sin(ang)
    z = jnp.zeros_like(cos)
    c = jnp.concatenate([cos, z, cos, z], axis=-1)
    s = jnp.concatenate([-sin, z, sin, z], axis=-1)
    rep = ATT_TM // DEC_SEQ
    return (jnp.concatenate([c[:SEQ], jnp.tile(c[PAST_LEN:], (rep, 1))], axis=0),
            jnp.concatenate([s[:SEQ], jnp.tile(s[PAST_LEN:], (rep, 1))], axis=0))


def _layer(xp, xs, p_prompt, p_sample, cache_kv, cache_kr, state_conv,
           g_mix, w_in, b_gate, w_dw, b_dw, g_cn, b_cn, w_conv_out,
           g_qa, g_kva, w_qb, w_kb, w_vb, g_qn, g_kn, w_o, w_out,
           g_ffn, w_router, b_router, w_gu, b_gu, w_dn, b_dn,
           g_ple, w_ple_gate, w_ple):
    assert SEQ == PAST_LEN
    row = lambda v: v.reshape(1, -1)
    w_in_b = w_in.astype(BF16)
    w_mid = jnp.concatenate([w_in_b[:, O_U:O_KV], _rope_layout(w_in_b[:, O_KV:O_KR])], axis=1)
    w_gate = w_in_b[:, O_KR:]

    h, q_lat, kv_p, kv_s, kr_pad = _in_mid(xp, xs, row(g_mix), w_mid, row(g_qa), row(g_kva))
    half = ROPE_DIM // 2
    kr_new = jnp.concatenate([kr_pad[:, :half], kr_pad[:, 2 * half:3 * half]], axis=1)
    glu = _in_glu(h, w_in_b)

    hist = jnp.concatenate([jnp.zeros((BATCH, HALO, CONV_CHANNELS), F32),
                            jnp.pad(state_conv, ((0, 0), (HALO - (CONV_WIDTH - 1), 0), (0, 0)))], axis=0)
    c_act = _conv_module(glu, hist, w_dw, row(b_dw), row(g_cn), row(b_cn))

    cos_t, sin_t = _rope_tables()
    w_q = jnp.concatenate([w_qb[..., :NOPE_DIM], _rope_layout(w_qb[..., NOPE_DIM:])], axis=-1)
    w_q = w_q.reshape(Q_LORA_RANK, N_HEADS * HEAD_PAD).astype(BF16)
    g_q = jnp.concatenate([g_qn[:NOPE_DIM] * g_kn[:NOPE_DIM], _rope_layout(g_qn[NOPE_DIM:])]).reshape(1, HEAD_PAD)
    q = _q_heads(q_lat, w_q, g_q, cos_t, sin_t)

    w_kv = jnp.concatenate([w_kb, w_vb], axis=-1).reshape(KV_LORA_RANK, N_HEADS * HEAD_PAD).astype(BF16)
    g_kn_rope = _rope_layout(g_kn[NOPE_DIM:]).reshape(1, LANES)
    k_new, v_new = _kv_heads(kv_p, kv_s, kr_pad, w_kv, g_kn_rope, cos_t, sin_t, _tab_idx_new, "kv_heads_new")
    attn = _flash_prompt(q, k_new, v_new)
    attn = _flash_sample(attn, q, cache_kv.reshape(DEC_BATCH * PAST_LEN, KV_LORA_RANK),
                         _rope_layout(cache_kr).reshape(DEC_BATCH * PAST_LEN, LANES),
                         w_kv, g_kn_rope, cos_t, sin_t, k_new, v_new)

    mix = _merge(h, c_act, attn, w_gate, row(b_gate), w_conv_out.astype(BF16), w_o.astype(BF16))

    wr = jnp.pad(w_router, ((0, 0), (0, LANES - N_EXPERTS)))
    wr_hi, wr_lo = _split_bf16(wr)
    b_r = jnp.concatenate([b_router, jnp.full((LANES - N_EXPERTS,), -jnp.inf, F32)]).reshape(1, LANES)
    x1, hm, idx_pad, gate_pad = _out_router(mix, xp, xs, w_out.astype(BF16), row(g_ffn), wr_hi, wr_lo, b_r)

    top_idx = idx_pad[:, :TOP_K]
    dest, row_tok, counts, nblk, blk_start = _moe_dispatch(top_idx)
    b_gu3 = b_gu.reshape(N_EXPERTS, 1, 2 * D_FF)
    chunk_rows = _CHUNK_BLKS * MOE_BLK
    up_plans = _moe_plans(counts, nblk, blk_start, _UP_TILES, MOE_CHUNKS, _CHUNK_BLKS)
    down_plan = _moe_plans(counts, nblk, blk_start, _DN_TILES, 1, MOE_MAX_BLKS)[0]
    act = None
    for c in range(MOE_CHUNKS):
        xs = hm.at[row_tok[c * chunk_rows:(c + 1) * chunk_rows]].get(mode="promise_in_bounds")
        act = _moe_up(up_plans[c], act, xs, w_gu, b_gu3, c)
    ys = _moe_down(down_plan, act, w_dn, b_dn.reshape(N_EXPERTS, 1, D_MODEL))

    dest_t = dest.reshape(N_TOK, TOP_K).T
    fin = (row(g_ple), w_ple_gate.astype(BF16))
    w_ple_b = w_ple.astype(BF16)
    n_c = N_P // FIN_CHUNKS
    out_p = None
    for c in range(FIN_CHUNKS):
        y4 = ys.at[dest_t[:, c * n_c:(c + 1) * n_c]].get(mode="promise_in_bounds")
        out_p = _final(out_p, x1, y4, gate_pad, *fin, p_prompt, w_ple_b, c * n_c, c * n_c, n_c, N_P,
                       f"final_prompt_{c}")
    y4 = ys.at[dest_t[:, N_P:]].get(mode="promise_in_bounds")
    out_s = _final(None, x1, y4, gate_pad, *fin, p_sample, w_ple_b, N_P, 0, N_S, N_S, "final_sample")
    return out_p, out_s, kv_p, kv_s, kr_new, glu


def kernel(x_prompt, x_sample, cache_kv_latent, cache_k_rope, state_conv, p_prompt, p_sample, g_mix, w_in, b_gate, w_dw, b_dw, g_cn, b_cn, w_conv_out, g_qa, g_kva, w_qb, w_kb, w_vb, g_qn, g_kn, w_o, w_out, g_ffn, w_router, b_router, w_gu, b_gu, w_dn, b_dn, g_ple, w_ple_gate, w_ple):
    assert g_mix.shape[0] == 1
    out_p, out_s, kv_p, kv_s, kr_new, glu = _layer(
        x_prompt.reshape(N_P, D_MODEL), x_sample.reshape(N_S, D_MODEL),
        p_prompt[0].reshape(N_P, PLE_DIM), p_sample[0].reshape(N_S, PLE_DIM),
        cache_kv_latent[0], cache_k_rope[0], state_conv[0],
        g_mix[0], w_in[0], b_gate[0], w_dw[0], b_dw[0], g_cn[0], b_cn[0], w_conv_out[0],
        g_qa[0], g_kva[0], w_qb[0], w_kb[0], w_vb[0], g_qn[0], g_kn[0], w_o[0], w_out[0],
        g_ffn[0], w_router[0], b_router[0], w_gu[0], b_gu[0], w_dn[0], b_dn[0],
        g_ple[0], w_ple_gate[0], w_ple[0])
    tail = CONV_WIDTH - 1
    conv_p = jnp.stack([glu[(b + 1) * SEQ - tail:(b + 1) * SEQ] for b in range(BATCH)])
    conv_s = glu[N_P:].reshape(DEC_BATCH, DEC_SEQ, CONV_CHANNELS)[:, DEC_SEQ - tail:]
    return (out_p.reshape(BATCH, SEQ, D_MODEL),
            out_s.reshape(DEC_BATCH, DEC_SEQ, D_MODEL),
            kv_p.reshape(1, BATCH, SEQ, KV_LORA_RANK),
            kr_new[:N_P].reshape(1, BATCH, SEQ, ROPE_DIM),
            conv_p[None],
            kv_s.reshape(1, DEC_BATCH, DEC_SEQ, KV_LORA_RANK),
            kr_new[N_P:].reshape(1, DEC_BATCH, DEC_SEQ, ROPE_DIM),
            conv_s[None])
```

```python
import functools
import math

import jax
import jax.numpy as jnp
from jax import lax
from jax.experimental import pallas as pl
from jax.experimental.pallas import tpu as pltpu

F32 = jnp.float32
BF16 = jnp.bfloat16
I32 = jnp.int32
U32 = jnp.uint32

D_MODEL = 2048
BATCH = 2
SEQ = 4096
DEC_BATCH = 8
DEC_SEQ = 64
PAST_LEN = 4096
CHUNK = 64
CONV_CHANNELS = D_MODEL
CONV_WIDTH = 31
N_HEADS = 16
Q_LORA_RANK = 512
KV_LORA_RANK = 512
NOPE_DIM = 128
ROPE_DIM = 64
QK_DIM = NOPE_DIM + ROPE_DIM
V_DIM = 128
ROPE_THETA = 10000.0
N_EXPERTS = 32
TOP_K = 4
D_FF = D_MODEL
SWIGLU_ALPHA = 1.702
SWIGLU_LIMIT = 7.0
PLE_DIM = 256
EPS = 1e-6
NEG_INF = -1e30

N_P = BATCH * SEQ
N_S = DEC_BATCH * DEC_SEQ
N_TOK = N_P + N_S
O_U = 2 * CONV_CHANNELS
O_Q = O_U + Q_LORA_RANK
O_KV = O_Q + KV_LORA_RANK
O_KR = O_KV + ROPE_DIM
LANES = 128
SUBLANES = 8
MID_W = Q_LORA_RANK + KV_LORA_RANK + LANES
HEAD_PAD = NOPE_DIM + LANES

TM = 512
CONV_T = 64
HALO = 32
MOE_BLK = 512
MOE_MAX_BLKS = (N_TOK * TOP_K) // MOE_BLK + N_EXPERTS
MOE_ROWS = MOE_MAX_BLKS * MOE_BLK
VMEM_LIMIT = 48 * 1024 * 1024
assert 2 * ROPE_DIM == LANES and NOPE_DIM == LANES and V_DIM == LANES and SEQ == PAST_LEN


def _cparams(n_axes):
    return pltpu.CompilerParams(dimension_semantics=("arbitrary",) * n_axes,
                                vmem_limit_bytes=VMEM_LIMIT)


def _sigmoid(x):
    return 1.0 / (1.0 + jnp.exp(-x))


def _dot(a, b):
    return jnp.dot(a, b, preferred_element_type=F32)


def _stacked_rows(i, n_prompt_tiles, xp_ref, xs_ref):
    return jnp.where(i < n_prompt_tiles, xp_ref[...], xs_ref[...])


def _in_mid_kernel(xp_ref, xs_ref, g_ref, w_ref, gqa_ref, gkva_ref, h_ref, q_ref, kvp_ref, kvs_ref, kr_ref):
    i = pl.program_id(0)
    x = _stacked_rows(i, N_P // TM, xp_ref, xs_ref)
    h = x * lax.rsqrt(jnp.mean(x * x, axis=-1, keepdims=True) + EPS) * g_ref[...]
    hb = h.astype(BF16)
    h_ref[...] = hb
    z = _dot(hb, w_ref[...])
    ql = z[:, :Q_LORA_RANK]
    kvl = z[:, Q_LORA_RANK:Q_LORA_RANK + KV_LORA_RANK]
    qn = ql * lax.rsqrt(jnp.mean(ql * ql, axis=-1, keepdims=True) + EPS) * gqa_ref[...]
    q_ref[...] = qn.astype(BF16)
    kv = kvl * lax.rsqrt(jnp.mean(kvl * kvl, axis=-1, keepdims=True) + EPS) * gkva_ref[...]
    kr_ref[...] = z[:, Q_LORA_RANK + KV_LORA_RANK:]

    @pl.when(i < N_P // TM)
    def _():
        kvp_ref[...] = kv

    @pl.when(i >= N_P // TM)
    def _():
        kvs_ref[...] = kv


def _in_mid(xp, xs, g_mix, w_mid, g_qa, g_kva):
    n = N_TOK
    npt = N_P // TM
    return pl.pallas_call(
        _in_mid_kernel,
        grid=(n // TM,),
        in_specs=[
            pl.BlockSpec((TM, D_MODEL), lambda i: (jnp.minimum(i, npt - 1), 0)),
            pl.BlockSpec((TM, D_MODEL), lambda i: (jnp.maximum(i - npt, 0), 0)),
            pl.BlockSpec((1, D_MODEL), lambda i: (0, 0)),
            pl.BlockSpec((D_MODEL, MID_W), lambda i: (0, 0)),
            pl.BlockSpec((1, Q_LORA_RANK), lambda i: (0, 0)),
            pl.BlockSpec((1, KV_LORA_RANK), lambda i: (0, 0)),
        ],
        out_specs=[
            pl.BlockSpec((TM, D_MODEL), lambda i: (i, 0)),
            pl.BlockSpec((TM, Q_LORA_RANK), lambda i: (i, 0)),
            pl.BlockSpec((TM, KV_LORA_RANK), lambda i: (jnp.minimum(i, npt - 1), 0)),
            pl.BlockSpec((TM, KV_LORA_RANK), lambda i: (jnp.maximum(i - npt, 0), 0)),
            pl.BlockSpec((TM, LANES), lambda i: (i, 0)),
        ],
        out_shape=[
            jax.ShapeDtypeStruct((n, D_MODEL), BF16),
            jax.ShapeDtypeStruct((n, Q_LORA_RANK), BF16),
            jax.ShapeDtypeStruct((N_P, KV_LORA_RANK), F32),
            jax.ShapeDtypeStruct((N_S, KV_LORA_RANK), F32),
            jax.ShapeDtypeStruct((n, LANES), F32),
        ],
        compiler_params=_cparams(1),
        name="in_mid",
    )(xp, xs, g_mix, w_mid, g_qa, g_kva)


def _glu_kernel(h_ref, w1_ref, w2_ref, o_ref):
    h = h_ref[...]
    o_ref[...] = _dot(h, w1_ref[...]) * _sigmoid(_dot(h, w2_ref[...]))


def _in_glu(h, w_in_b):
    n = h.shape[0]
    tn = 1024
    nj = CONV_CHANNELS // tn
    return pl.pallas_call(
        _glu_kernel,
        grid=(n // TM, nj),
        in_specs=[
            pl.BlockSpec((TM, D_MODEL), lambda i, j: (i, 0)),
            pl.BlockSpec((D_MODEL, tn), lambda i, j: (0, j)),
            pl.BlockSpec((D_MODEL, tn), lambda i, j: (0, j + nj)),
        ],
        out_specs=pl.BlockSpec((TM, tn), lambda i, j: (i, j)),
        out_shape=jax.ShapeDtypeStruct((n, CONV_CHANNELS), F32),
        compiler_params=_cparams(2),
        name="in_glu",
    )(h, w_in_b, w_in_b)


CONV_TM = 256
_CONV_SUBS = CONV_TM // CONV_T
_CONV_SEQ_TILES = SEQ // CONV_TM
_CONV_PROMPT_TILES = N_P // CONV_TM
_CONV_LANES = 512
_SHIFT_ROWS = (HALO // SUBLANES - 1) * SUBLANES + CONV_T


def _conv_kernel(cur_ref, prev_ref, hist_ref, w_ref, bdw_ref, g_ref, b_ref, o_ref, win_ref, conv_ref, shift_ref):
    i = pl.program_id(0)
    is_sample = i >= _CONV_PROMPT_TILES
    opens = i % _CONV_SEQ_TILES == 0
    base = HALO - (CONV_WIDTH - 1)
    for j in range(_CONV_SUBS):
        r0 = j * CONV_T
        before = jnp.where(opens, 0.0, prev_ref[...]) if j == 0 else cur_ref[r0 - HALO:r0, :]
        win_ref[0:HALO, :] = jnp.where(is_sample, hist_ref[j], before)
        win_ref[HALO:HALO + CONV_T, :] = cur_ref[r0:r0 + CONV_T, :]
        for r in range(1, SUBLANES):
            shift_ref[r - 1] = win_ref[r:r + _SHIFT_ROWS, :]
        for c in range(0, CONV_CHANNELS, _CONV_LANES):
            acc = jnp.zeros((CONV_T, _CONV_LANES), F32)
            for k in range(CONV_WIDTH):
                q, r = divmod(base + k, SUBLANES)
                lanes = slice(c, c + _CONV_LANES)
                rows = slice(q * SUBLANES, q * SUBLANES + CONV_T)
                src = win_ref[rows, lanes] if r == 0 else shift_ref[r - 1, rows, lanes]
                acc = acc + w_ref[k:k + 1, lanes] * src
            conv_ref[:, c:c + _CONV_LANES] = acc + bdw_ref[:, c:c + _CONV_LANES]
        y = conv_ref[...]
        yc = y - jnp.mean(y, axis=-1, keepdims=True)
        var = jnp.mean(yc * yc, axis=-1, keepdims=True)
        z = yc * lax.rsqrt(var + EPS) * g_ref[...] + b_ref[...]
        o_ref[r0:r0 + CONV_T, :] = (z * _sigmoid(z)).astype(BF16)


def _conv_module(glu, hist_s, w_dw, b_dw, g_cn, b_cn):
    n = glu.shape[0]
    halo_per_tile = CONV_TM // HALO
    n_sample_tiles = N_S // CONV_TM
    const = lambda i: (0, 0)
    return pl.pallas_call(
        _conv_kernel,
        grid=(n // CONV_TM,),
        in_specs=[
            pl.BlockSpec((CONV_TM, CONV_CHANNELS), lambda i: (i, 0)),
            pl.BlockSpec((HALO, CONV_CHANNELS), lambda i: (jnp.maximum(i * halo_per_tile - 1, 0), 0)),
            pl.BlockSpec((_CONV_SUBS, HALO, CONV_CHANNELS),
                         lambda i: (jnp.clip(i - _CONV_PROMPT_TILES, 0, n_sample_tiles - 1), 0, 0)),
            pl.BlockSpec((CONV_WIDTH, CONV_CHANNELS), const),
            pl.BlockSpec((1, CONV_CHANNELS), const),
            pl.BlockSpec((1, CONV_CHANNELS), const),
            pl.BlockSpec((1, CONV_CHANNELS), const),
        ],
        out_specs=pl.BlockSpec((CONV_TM, CONV_CHANNELS), lambda i: (i, 0)),
        out_shape=jax.ShapeDtypeStruct((n, CONV_CHANNELS), BF16),
        scratch_shapes=[pltpu.VMEM((HALO + CONV_T, CONV_CHANNELS), F32),
                        pltpu.VMEM((CONV_T, CONV_CHANNELS), F32),
                        pltpu.VMEM((SUBLANES - 1, _SHIFT_ROWS, CONV_CHANNELS), F32)],
        compiler_params=_cparams(1),
        name="conv_module",
    )(glu, glu, hist_s, w_dw, b_dw, g_cn, b_cn)


ATT_TM = 512
_TAB_PROMPT_TILES = N_P // ATT_TM
_TAB_SEQ_TILES = SEQ // ATT_TM


def _tab_idx_new(i):
    return jnp.where(i < _TAB_PROMPT_TILES, i % _TAB_SEQ_TILES, _TAB_SEQ_TILES)


def _rope_pair(u, c, s):
    return u * c + pltpu.roll(u, LANES // 2, 1) * s


_Q_SCALE = math.log2(math.e) / math.sqrt(QK_DIM)


def _q_heads_kernel(ql_ref, w_ref, g_ref, c_ref, s_ref, o_ref):
    ql = ql_ref[...]
    g = g_ref[...]
    c = c_ref[...]
    s = s_ref[...]
    for h in range(N_HEADS):
        qf = _dot(ql, w_ref[:, h * HEAD_PAD:(h + 1) * HEAD_PAD])
        ssq = jnp.sum(qf * qf, axis=-1, keepdims=True)
        qn = qf * (lax.rsqrt(ssq * (1.0 / QK_DIM) + EPS) * _Q_SCALE) * g
        o_ref[h, :, :NOPE_DIM] = qn[:, :NOPE_DIM].astype(BF16)
        o_ref[h, :, NOPE_DIM:] = _rope_pair(qn[:, NOPE_DIM:], c, s).astype(BF16)


def _q_heads(q_lat, w_q, g_q, cos_t, sin_t):
    n = q_lat.shape[0]
    return pl.pallas_call(
        _q_heads_kernel,
        grid=(n // ATT_TM,),
        in_specs=[
            pl.BlockSpec((ATT_TM, Q_LORA_RANK), lambda i: (i, 0)),
            pl.BlockSpec((Q_LORA_RANK, N_HEADS * HEAD_PAD), lambda i: (0, 0)),
            pl.BlockSpec((1, HEAD_PAD), lambda i: (0, 0)),
            pl.BlockSpec((ATT_TM, LANES), lambda i: (_tab_idx_new(i), 0)),
            pl.BlockSpec((ATT_TM, LANES), lambda i: (_tab_idx_new(i), 0)),
        ],
        out_specs=pl.BlockSpec((N_HEADS, ATT_TM, HEAD_PAD), lambda i: (0, i, 0)),
        out_shape=jax.ShapeDtypeStruct((N_HEADS, n, HEAD_PAD), BF16),
        compiler_params=_cparams(1),
        name="q_heads",
    )(q_lat, w_q, g_q, cos_t, sin_t)


def _kv_heads_kernel(kvp_ref, kvs_ref, kr_ref, w_ref, gr_ref, c_ref, s_ref, k_ref, v_ref):
    kv = _stacked_rows(pl.program_id(0), N_P // ATT_TM, kvp_ref, kvs_ref).astype(BF16)
    u = kr_ref[...]
    ssq_r = jnp.sum(u * u, axis=-1, keepdims=True)
    krot = _rope_pair(u * gr_ref[...], c_ref[...], s_ref[...])
    for h in range(N_HEADS):
        z = _dot(kv, w_ref[:, h * HEAD_PAD:(h + 1) * HEAD_PAD])
        kn = z[:, :NOPE_DIM]
        ssq = jnp.sum(kn * kn, axis=-1, keepdims=True) + ssq_r
        scale = lax.rsqrt(ssq * (1.0 / QK_DIM) + EPS)
        k_ref[h, :, :NOPE_DIM] = (kn * scale).astype(BF16)
        k_ref[h, :, NOPE_DIM:] = (krot * scale).astype(BF16)
        v_ref[h] = z[:, NOPE_DIM:].astype(BF16)


def _kv_heads(kv_p, kv_s, kr_pad, w_kv, g_kn_rope, cos_t, sin_t, tab_idx, name):
    n = N_TOK
    npt = N_P // ATT_TM
    return pl.pallas_call(
        _kv_heads_kernel,
        grid=(n // ATT_TM,),
        in_specs=[
            pl.BlockSpec((ATT_TM, KV_LORA_RANK), lambda i: (jnp.minimum(i, npt - 1), 0)),
            pl.BlockSpec((ATT_TM, KV_LORA_RANK), lambda i: (jnp.maximum(i - npt, 0), 0)),
            pl.BlockSpec((ATT_TM, LANES), lambda i: (i, 0)),
            pl.BlockSpec((KV_LORA_RANK, N_HEADS * HEAD_PAD), lambda i: (0, 0)),
            pl.BlockSpec((1, LANES), lambda i: (0, 0)),
            pl.BlockSpec((ATT_TM, LANES), lambda i: (tab_idx(i), 0)),
            pl.BlockSpec((ATT_TM, LANES), lambda i: (tab_idx(i), 0)),
        ],
        out_specs=[
            pl.BlockSpec((N_HEADS, ATT_TM, HEAD_PAD), lambda i: (0, i, 0)),
            pl.BlockSpec((N_HEADS, ATT_TM, V_DIM), lambda i: (0, i, 0)),
        ],
        out_shape=[
            jax.ShapeDtypeStruct((N_HEADS, n, HEAD_PAD), BF16),
            jax.ShapeDtypeStruct((N_HEADS, n, V_DIM), BF16),
        ],
        compiler_params=_cparams(1),
        name=name,
    )(kv_p, kv_s, kr_pad, w_kv, g_kn_rope, cos_t, sin_t)


_TQ = 512
_TKB = 512
_HB = 4
_HBP = 4


def _flash_prompt_kernel(q_ref, k_ref, v_ref, o_ref, m_ref, l_ref, acc_ref):
    qi = pl.program_id(2)
    m_ref[...] = jnp.full(m_ref.shape, NEG_INF, F32)
    l_ref[...] = jnp.zeros(l_ref.shape, F32)
    acc_ref[...] = jnp.zeros(acc_ref.shape, F32)
    nlb = _TKB // LANES

    def step(ki, masked):
        start = pl.multiple_of(ki * _TKB, _TKB)
        scores = [lax.dot_general(q_ref[hh], k_ref[hh, pl.ds(start, _TKB), :], (((1,), (1,)), ((), ())),
                                  preferred_element_type=F32) for hh in range(_HBP)]
        probs = []
        for hh in range(_HBP):
            s = scores[hh]
            if masked:
                rc = lax.broadcasted_iota(I32, (_TQ, _TKB), 0) // CHUNK
                cc = lax.broadcasted_iota(I32, (_TQ, _TKB), 1) // CHUNK
                s = jnp.where(cc <= rc, s, NEG_INF)
            sb = [s[:, c * LANES:(c + 1) * LANES] for c in range(nlb)]
            bm = sb[0]
            for c in range(1, nlb):
                bm = jnp.maximum(bm, sb[c])
            m_prev = m_ref[hh]
            m_new = jnp.maximum(m_prev, jnp.max(bm, axis=-1, keepdims=True))
            alpha = jnp.exp2(m_prev - m_new)
            ps = [jnp.exp2(x - m_new) for x in sb]
            psum = ps[0]
            for c in range(1, nlb):
                psum = psum + ps[c]
            l_ref[hh] = alpha * l_ref[hh] + psum
            m_ref[hh] = m_new
            probs.append((alpha, jnp.concatenate(ps, axis=1).astype(BF16)))
        for hh in range(_HBP):
            alpha, p = probs[hh]
            acc_ref[hh] = alpha * acc_ref[hh] + _dot(p, v_ref[hh, pl.ds(start, _TKB), :])

    def body(ki, carry):
        step(ki, False)
        return carry

    lax.fori_loop(0, qi, body, 0)
    step(qi, True)
    for hh in range(_HBP):
        l = jnp.sum(l_ref[hh], axis=-1, keepdims=True)
        o_ref[:, hh * V_DIM:(hh + 1) * V_DIM] = (acc_ref[hh] / l).astype(BF16)


def _flash_prompt(q, k, v):
    nq = SEQ // _TQ
    return pl.pallas_call(
        _flash_prompt_kernel,
        grid=(BATCH, N_HEADS // _HBP, nq),
        in_specs=[
            pl.BlockSpec((_HBP, _TQ, HEAD_PAD), lambda b, h, i: (h, b * nq + i, 0)),
            pl.BlockSpec((_HBP, SEQ, HEAD_PAD), lambda b, h, i: (h, b, 0)),
            pl.BlockSpec((_HBP, SEQ, V_DIM), lambda b, h, i: (h, b, 0)),
        ],
        out_specs=pl.BlockSpec((_TQ, _HBP * V_DIM), lambda b, h, i: (b * nq + i, h)),
        out_shape=jax.ShapeDtypeStruct((N_TOK, N_HEADS * V_DIM), BF16),
        scratch_shapes=[pltpu.VMEM((_HBP, _TQ, LANES), F32), pltpu.VMEM((_HBP, _TQ, LANES), F32),
                        pltpu.VMEM((_HBP, _TQ, V_DIM), F32)],
        compiler_params=_cparams(3),
        name="flash_prompt",
    )(q, k, v)


_KC_ROWS = 512


def _flash_sample_kernel(prev_ref, q_ref, kv_ref, kr_ref, w_ref, gr_ref, c_ref, s_ref, kn_ref, vn_ref,
                         o_ref, kvb_ref, krot_ref, ssqr_ref, k_ref, v_ref):
    del prev_ref

    @pl.when(pl.program_id(1) == 0)
    def _():
        kvb_ref[...] = kv_ref[...].astype(BF16)
        u = kr_ref[...]
        ssqr_ref[...] = jnp.broadcast_to(jnp.sum(u * u, axis=-1, keepdims=True), ssqr_ref.shape)
        krot_ref[...] = _rope_pair(u * gr_ref[...], c_ref[...], s_ref[...])

    nt = (((1,), (1,)), ((), ()))
    for hh in range(_HB):
        w = w_ref[:, hh * HEAD_PAD:(hh + 1) * HEAD_PAD]
        for r in range(0, PAST_LEN, _KC_ROWS):
            rows = slice(r, r + _KC_ROWS)
            z = _dot(kvb_ref[rows, :], w)
            kn = z[:, :NOPE_DIM]
            ssq = jnp.sum(kn * kn, axis=-1, keepdims=True) + ssqr_ref[rows, :]
            scale = lax.rsqrt(ssq * (1.0 / QK_DIM) + EPS)
            k_ref[rows, :NOPE_DIM] = (kn * scale).astype(BF16)
            k_ref[rows, NOPE_DIM:] = (krot_ref[rows, :] * scale).astype(BF16)
            v_ref[rows, :] = z[:, NOPE_DIM:].astype(BF16)
        q = q_ref[hh]
        s1 = lax.dot_general(q, k_ref[...], nt, preferred_element_type=F32)
        s2 = lax.dot_general(q, kn_ref[hh], nt, preferred_element_type=F32)
        m = jnp.maximum(jnp.max(s1, axis=-1, keepdims=True), jnp.max(s2, axis=-1, keepdims=True))
        p1 = jnp.exp2(s1 - m)
        p2 = jnp.exp2(s2 - m)
        l = jnp.sum(p1, axis=-1, keepdims=True) + jnp.sum(p2, axis=-1, keepdims=True)
        o = _dot(p1.astype(BF16), v_ref[...]) + _dot(p2.astype(BF16), vn_ref[hh])
        o_ref[:, hh * V_DIM:(hh + 1) * V_DIM] = (o / l).astype(BF16)


def _flash_sample(attn, q, cache_kv, cache_kr_pad, w_kv, g_kn_rope, cos_t, sin_t, k_new, v_new):
    assert (PAST_LEN + DEC_SEQ - 1) // CHUNK <= PAST_LEN // CHUNK
    blk0 = N_P // DEC_SEQ
    new = lambda b, h: (h, blk0 + b, 0)
    const = lambda b, h: (0, 0)
    once = pl.Buffered(1)
    return pl.pallas_call(
        _flash_sample_kernel,
        grid=(DEC_BATCH, N_HEADS // _HB),
        in_specs=[
            pl.BlockSpec(memory_space=pl.ANY),
            pl.BlockSpec((_HB, DEC_SEQ, HEAD_PAD), new),
            pl.BlockSpec((PAST_LEN, KV_LORA_RANK), lambda b, h: (b, 0)),
            pl.BlockSpec((PAST_LEN, LANES), lambda b, h: (b, 0)),
            pl.BlockSpec((KV_LORA_RANK, _HB * HEAD_PAD), lambda b, h: (0, h)),
            pl.BlockSpec((1, LANES), const),
            pl.BlockSpec((PAST_LEN, LANES), const, pipeline_mode=once),
            pl.BlockSpec((PAST_LEN, LANES), const, pipeline_mode=once),
            pl.BlockSpec((_HB, DEC_SEQ, HEAD_PAD), new),
            pl.BlockSpec((_HB, DEC_SEQ, V_DIM), new),
        ],
        out_specs=pl.BlockSpec((DEC_SEQ, _HB * V_DIM), lambda b, h: (blk0 + b, h)),
        out_shape=jax.ShapeDtypeStruct((N_TOK, N_HEADS * V_DIM), BF16),
        scratch_shapes=[pltpu.VMEM((PAST_LEN, KV_LORA_RANK), BF16),
                        pltpu.VMEM((PAST_LEN, LANES), F32),
                        pltpu.VMEM((PAST_LEN, LANES), F32),
                        pltpu.VMEM((PAST_LEN, HEAD_PAD), BF16),
                        pltpu.VMEM((PAST_LEN, V_DIM), BF16)],
        input_output_aliases={0: 0},
        compiler_params=_cparams(2),
        name="flash_sample",
    )(attn, q, cache_kv, cache_kr_pad, w_kv, g_kn_rope, cos_t, sin_t, k_new, v_new)


def _merge_kernel(h_ref, c_ref, a_ref, wga_ref, wgb_ref, bga_ref, bgb_ref, wc_ref, wo_ref, o_ref):
    h = h_ref[...]
    ga = _sigmoid(_dot(h, wga_ref[...]) + bga_ref[...])
    gb = _sigmoid(_dot(h, wgb_ref[...]) + bgb_ref[...])
    mix = ga * _dot(c_ref[...], wc_ref[...]) + gb * _dot(a_ref[...], wo_ref[...])
    o_ref[...] = mix.astype(BF16)


def _merge(h, c_act, attn, w_gate, b_gate, w_conv_out, w_o):
    n = h.shape[0]
    tn = 512
    nj = D_MODEL // tn
    row = lambda i, j: (i, 0)
    return pl.pallas_call(
        _merge_kernel,
        grid=(n // TM, nj),
        in_specs=[
            pl.BlockSpec((TM, D_MODEL), row),
            pl.BlockSpec((TM, CONV_CHANNELS), row),
            pl.BlockSpec((TM, N_HEADS * V_DIM), row),
            pl.BlockSpec((D_MODEL, tn), lambda i, j: (0, j)),
            pl.BlockSpec((D_MODEL, tn), lambda i, j: (0, j + nj)),
            pl.BlockSpec((1, tn), lambda i, j: (0, j)),
            pl.BlockSpec((1, tn), lambda i, j: (0, j + nj)),
            pl.BlockSpec((CONV_CHANNELS, tn), lambda i, j: (0, j)),
            pl.BlockSpec((N_HEADS * V_DIM, tn), lambda i, j: (0, j)),
        ],
        out_specs=pl.BlockSpec((TM, tn), lambda i, j: (i, j)),
        out_shape=jax.ShapeDtypeStruct((n, D_MODEL), BF16),
        compiler_params=_cparams(2),
        name="merge",
    )(h, c_act, attn, w_gate, w_gate, b_gate, b_gate, w_conv_out, w_o)


def _split_bf16(x):
    hi = x.astype(BF16)
    lo = (x - hi.astype(F32)).astype(BF16)
    return hi, lo


_HALF = D_MODEL // 2


def _pack_bf16_pair(a, b):
    ua = lax.bitcast_convert_type(a.astype(BF16).astype(F32), U32)
    ub = lax.bitcast_convert_type(b.astype(BF16).astype(F32), U32)
    return lax.bitcast_convert_type(ua | (ub >> 16), F32)


def _unpack_bf16_pair(w):
    w = lax.bitcast_convert_type(w, U32)
    a = lax.bitcast_convert_type(w & jnp.uint32(0xFFFF0000), F32).astype(BF16)
    b = lax.bitcast_convert_type(w << 16, F32).astype(BF16)
    return a, b


def _out_router_kernel(n_prompt_tiles, mix_ref, xp_ref, xs_ref, w_ref, g_ref, wrh_ref, wrl_ref, br_ref,
                       x1_ref, hm_ref, idx_ref, gate_ref):
    x = _stacked_rows(pl.program_id(0), n_prompt_tiles, xp_ref, xs_ref)
    x1 = x + _dot(mix_ref[...], w_ref[...])
    x1_ref[...] = x1
    hn = x1 * lax.rsqrt(jnp.mean(x1 * x1, axis=-1, keepdims=True) + EPS) * g_ref[...]
    hm_ref[0] = _pack_bf16_pair(hn[:, :_HALF], hn[:, _HALF:])
    hm_ref[1] = jnp.zeros(hm_ref.shape[1:], F32)
    hh, hl = _split_bf16(hn)
    logits = _dot(hh, wrh_ref[...]) + (_dot(hh, wrl_ref[...]) + _dot(hl, wrh_ref[...])) + br_ref[...]
    lane = lax.broadcasted_iota(I32, logits.shape, 1).astype(F32)
    vals = []
    idx_out = jnp.zeros(logits.shape, F32)
    for k in range(TOP_K):
        m = jnp.max(logits, axis=-1, keepdims=True)
        sel = jnp.min(jnp.where(logits == m, lane, 1e9), axis=-1, keepdims=True)
        vals.append(m)
        idx_out = jnp.where(lane == float(k), sel, idx_out)
        logits = jnp.where(lane == sel, -jnp.inf, logits)
    exps = [jnp.exp(v - vals[0]) for v in vals]
    denom = exps[0] + exps[1] + exps[2] + exps[3]
    gate_out = jnp.zeros(idx_out.shape, F32)
    for k in range(TOP_K):
        gate_out = jnp.where(lane == float(k), exps[k] / denom, gate_out)
    idx_ref[...] = idx_out.astype(I32)
    gate_ref[...] = gate_out


def _out_router(mix, xp, xs, w_out, g_ffn, wr_hi, wr_lo, b_r):
    n = N_TOK
    tm = TM
    n_tiles = n // tm
    npt = N_P // tm
    const = lambda i: (0, 0)
    row = lambda i: (i, 0)
    once = pl.Buffered(1)
    return pl.pallas_call(
        functools.partial(_out_router_kernel, npt),
        grid=(n_tiles,),
        in_specs=[
            pl.BlockSpec((tm, D_MODEL), row),
            pl.BlockSpec((tm, D_MODEL), lambda i: (jnp.minimum(i, npt - 1), 0)),
            pl.BlockSpec((tm, D_MODEL), lambda i: (jnp.clip(i - npt, 0, N_S // tm - 1), 0)),
            pl.BlockSpec((D_MODEL, D_MODEL), const, pipeline_mode=once),
            pl.BlockSpec((1, D_MODEL), const),
            pl.BlockSpec((D_MODEL, LANES), const, pipeline_mode=once),
            pl.BlockSpec((D_MODEL, LANES), const, pipeline_mode=once),
            pl.BlockSpec((1, LANES), const),
        ],
        out_specs=[
            pl.BlockSpec((tm, D_MODEL), row),
            pl.BlockSpec((2, tm, _HALF), lambda i: (0, i, 0)),
            pl.BlockSpec((tm, LANES), row),
            pl.BlockSpec((tm, LANES), row),
        ],
        out_shape=[
            jax.ShapeDtypeStruct((n, D_MODEL), F32),
            jax.ShapeDtypeStruct((2, n, _HALF), F32),
            jax.ShapeDtypeStruct((n, LANES), I32),
            jax.ShapeDtypeStruct((n, LANES), F32),
        ],
        compiler_params=_cparams(1),
        name="out_router",
    )(mix, xp, xs, w_out, g_ffn, wr_hi, wr_lo, b_r)


_F_VALID, _F_FIRST, _F_NEXT, _F_GROUP0 = 1, 2, 4, 8


_P_E, _P_W, _P_N, _P_B, _P_BI, _P_NE, _P_NW, _P_FL, _P_SUBS = range(9)
MOE_SUB = 128


def _stream_weights(t, plan_ref, copies, cast):
    flags = plan_ref[_P_FL, t]

    @pl.when((flags & _F_FIRST) != 0)
    def _():
        cur = copies(plan_ref[_P_E, t], plan_ref[_P_W, t])

        @pl.when((flags & _F_GROUP0) != 0)
        def _():
            for c in cur:
                c.start()

        for c in cur:
            c.wait()
        cast()

        @pl.when((flags & _F_NEXT) != 0)
        def _():
            for c in copies(plan_ref[_P_NE, t], plan_ref[_P_NW, t]):
                c.start()


def _for_used_rows(valid, subs, rows_body):
    for n_sub in range(1, MOE_BLK // MOE_SUB + 1):
        @pl.when(jnp.logical_and(valid, subs == n_sub))
        def _(m=n_sub * MOE_SUB):
            rows_body(m)


def _moe_up_kernel(plan_ref, prev_ref, x_ref, w_hbm, bg_ref, bu_ref, o_ref, wbuf_ref, wgb_ref, wub_ref, sem_ref):
    del prev_ref
    t = pl.program_id(0)

    def copies(e, w):
        col = pl.multiple_of(w * _UP_TN, _UP_TN)
        return (pltpu.make_async_copy(w_hbm.at[e, :, pl.ds(col, _UP_TN)], wbuf_ref.at[0], sem_ref.at[0]),
                pltpu.make_async_copy(w_hbm.at[e, :, pl.ds(col + D_FF, _UP_TN)], wbuf_ref.at[1], sem_ref.at[1]))

    def cast():
        wgb_ref[...] = wbuf_ref[0].astype(BF16)
        wub_ref[...] = wbuf_ref[1].astype(BF16)

    _stream_weights(t, plan_ref, copies, cast)
    valid = (plan_ref[_P_FL, t] & _F_VALID) != 0

    def rows_body(m):
        xa, xb = _unpack_bf16_pair(x_ref[:m, :])
        g = _dot(xa, wgb_ref[:_HALF, :]) + _dot(xb, wgb_ref[_HALF:, :]) + bg_ref[0]
        u = _dot(xa, wub_ref[:_HALF, :]) + _dot(xb, wub_ref[_HALF:, :]) + bu_ref[0]
        g = jnp.minimum(g, SWIGLU_LIMIT)
        u = jnp.clip(u, -SWIGLU_LIMIT, SWIGLU_LIMIT)
        o_ref[:m, :] = ((u + 1.0) * (g * _sigmoid(SWIGLU_ALPHA * g))).astype(BF16)
        if m < MOE_BLK:
            o_ref[m:, :] = jnp.zeros((MOE_BLK - m, o_ref.shape[1]), BF16)

    _for_used_rows(valid, plan_ref[_P_SUBS, t], rows_body)

    @pl.when(jnp.logical_not(valid))
    def _():
        o_ref[...] = jnp.zeros(o_ref.shape, BF16)


_UP_TN = 1024
_UP_TILES = D_FF // _UP_TN
_DN_TN = 2048
_DN_TILES = D_MODEL // _DN_TN
MOE_CHUNKS = 4
_CHUNK_BLKS = MOE_MAX_BLKS // MOE_CHUNKS


def _moe_up(plan, act_prev, xs, w_gu, b_gu, chunk):
    steps = plan.shape[1]
    blk0 = chunk * _CHUNK_BLKS
    bspec = lambda off: pl.BlockSpec((1, 1, _UP_TN), lambda t, p: (p[_P_E, t], 0, p[_P_W, t] + off))
    aliases = {} if act_prev is None else {1: 0}
    prev = jnp.zeros((SUBLANES, LANES), BF16) if act_prev is None else act_prev
    return pl.pallas_call(
        _moe_up_kernel,
        grid_spec=pltpu.PrefetchScalarGridSpec(
            num_scalar_prefetch=1,
            grid=(steps,),
            in_specs=[
                pl.BlockSpec(memory_space=pl.ANY),
                pl.BlockSpec((MOE_BLK, _HALF), lambda t, p: (p[_P_BI, t], 0)),
                pl.BlockSpec(memory_space=pl.ANY),
                bspec(0), bspec(_UP_TILES),
            ],
            out_specs=pl.BlockSpec((MOE_BLK, _UP_TN),
                                   lambda t, p: (blk0 + p[_P_B, t], p[_P_N, t])),
            scratch_shapes=[pltpu.VMEM((2, D_MODEL, _UP_TN), F32),
                            pltpu.VMEM((D_MODEL, _UP_TN), BF16), pltpu.VMEM((D_MODEL, _UP_TN), BF16),
                            pltpu.SemaphoreType.DMA((2,))],
        ),
        out_shape=jax.ShapeDtypeStruct((MOE_ROWS, D_FF), BF16),
        input_output_aliases=aliases,
        compiler_params=_cparams(1),
        name=f"moe_up_{chunk}",
    )(plan, prev, xs, w_gu, b_gu, b_gu)


_DN_HALF = _DN_TN // 2


def _moe_down_kernel(plan_ref, a_ref, w_hbm, b_ref, o_ref, wbuf_ref, wb_ref, sem_ref):
    t = pl.program_id(0)

    def copies(e, w):
        col = pl.multiple_of(w * _DN_TN, _DN_TN)
        return (pltpu.make_async_copy(w_hbm.at[e, :, pl.ds(col, _DN_TN)], wbuf_ref, sem_ref.at[0]),)

    def cast():
        wb_ref[...] = wbuf_ref[...].astype(BF16)

    _stream_weights(t, plan_ref, copies, cast)
    valid = (plan_ref[_P_FL, t] & _F_VALID) != 0

    def rows_body(m):
        y = _dot(a_ref[:m, :], wb_ref[...]) + b_ref[0]
        o_ref[:m, :] = _pack_bf16_pair(y[:, :_DN_HALF], y[:, _DN_HALF:])
        if m < MOE_BLK:
            o_ref[m:, :] = jnp.zeros((MOE_BLK - m, o_ref.shape[1]), F32)

    _for_used_rows(valid, plan_ref[_P_SUBS, t], rows_body)

    @pl.when(jnp.logical_not(valid))
    def _():
        o_ref[...] = jnp.zeros(o_ref.shape, F32)


def _moe_down(plan, act, w_dn, b_dn):
    steps = plan.shape[1]
    return pl.pallas_call(
        _moe_down_kernel,
        grid_spec=pltpu.PrefetchScalarGridSpec(
            num_scalar_prefetch=1,
            grid=(steps,),
            in_specs=[
                pl.BlockSpec((MOE_BLK, D_FF), lambda t, p: (p[_P_BI, t], 0)),
                pl.BlockSpec(memory_space=pl.ANY),
                pl.BlockSpec((1, 1, _DN_TN), lambda t, p: (p[_P_E, t], 0, p[_P_W, t])),
            ],
            out_specs=pl.BlockSpec((MOE_BLK, _DN_HALF), lambda t, p: (p[_P_B, t], p[_P_N, t])),
            scratch_shapes=[pltpu.VMEM((D_FF, _DN_TN), F32), pltpu.VMEM((D_FF, _DN_TN), BF16),
                            pltpu.SemaphoreType.DMA((1,))],
        ),
        out_shape=jax.ShapeDtypeStruct((MOE_ROWS, _HALF), F32),
        compiler_params=_cparams(1),
        name="moe_down",
    )(plan, act, w_dn, b_dn)


def _moe_dispatch(top_idx):
    n_asg = N_TOK * TOP_K
    flat_e = top_idx.reshape(-1)
    onehot = (flat_e[:, None] == jnp.arange(N_EXPERTS, dtype=I32)[None, :]).astype(I32)
    csum = jnp.cumsum(onehot, axis=0)
    counts = csum[-1]
    rank = jnp.sum(csum * onehot, axis=1) - 1
    nblk = (counts + MOE_BLK - 1) // MOE_BLK
    blk_start = jnp.cumsum(nblk) - nblk
    dest = jnp.sum(onehot * blk_start[None, :], axis=1) * MOE_BLK + rank
    pad_src = jnp.arange(MOE_ROWS, dtype=I32) % N_TOK
    row_tok = pad_src.at[dest].set(jnp.arange(n_asg, dtype=I32) // TOP_K,
                                   mode="promise_in_bounds", unique_indices=True)
    return dest, row_tok, counts, nblk, blk_start


def _moe_steps(counts, nblk, blk_start, n_tiles, blk_lo, n_blks):
    t_max = n_tiles * n_blks
    lo = jnp.clip(blk_start, blk_lo, blk_lo + n_blks)
    hi = jnp.clip(blk_start + nblk, blk_lo, blk_lo + n_blks)
    nb_e = hi - lo
    per_e = nb_e * n_tiles
    s_end = jnp.cumsum(per_e)
    total = s_end[-1]
    t = jnp.arange(t_max, dtype=I32)
    tc = jnp.clip(t, 0, jnp.maximum(total - 1, 0))
    e = jnp.minimum(jnp.sum((s_end[None, :] <= tc[:, None]).astype(I32), axis=1), N_EXPERTS - 1)
    sel = (e[:, None] == jnp.arange(N_EXPERTS, dtype=I32)[None, :]).astype(I32)
    pick = lambda v: jnp.sum(sel * v[None, :], axis=1)
    local = tc - pick(s_end - per_e)
    nb = jnp.maximum(pick(nb_e), 1)
    w_tile = jnp.clip(local // nb, 0, n_tiles - 1)
    r = local % nb
    valid = t < total
    first = jnp.logical_and(valid, r == 0)
    fill = t - total
    blk = jnp.where(valid, pick(lo) - blk_lo + r, total // n_tiles + fill // n_tiles)
    rows_used = pick(counts) - (pick(lo) + r - pick(blk_start)) * MOE_BLK
    subs = jnp.clip((rows_used + MOE_SUB - 1) // MOE_SUB, 1, MOE_BLK // MOE_SUB)
    o_tile = jnp.where(valid, w_tile, fill % n_tiles)
    blk = jnp.clip(blk, 0, n_blks - 1)
    blk_in = jnp.where(valid, blk, jnp.maximum(total // n_tiles - 1, 0))
    ids = jnp.arange(N_EXPERTS, dtype=I32)
    owners = jnp.where(nb_e > 0, ids, N_EXPERTS)
    later = jnp.flip(lax.cummin(jnp.flip(owners)))
    next_owner = pick(jnp.concatenate([later[1:], jnp.full((1,), N_EXPERTS, I32)]))
    last_tile = w_tile == n_tiles - 1
    next_e = jnp.where(last_tile, next_owner, e)
    next_w = jnp.where(last_tile, 0, w_tile + 1)
    has_next = jnp.logical_and(first, next_e < N_EXPERTS)
    group = jnp.cumsum(first.astype(I32)) - 1
    flags = (valid * _F_VALID + first * _F_FIRST + has_next * _F_NEXT
             + jnp.logical_and(first, group == 0) * _F_GROUP0)
    rows = {_P_E: e, _P_W: w_tile, _P_N: o_tile, _P_B: blk, _P_BI: blk_in,
            _P_NE: jnp.minimum(next_e, N_EXPERTS - 1), _P_NW: next_w, _P_FL: flags, _P_SUBS: subs}
    return jnp.stack([rows[k].astype(I32) for k in range(len(rows))])


def _moe_plans(counts, nblk, blk_start, n_tiles, n_chunks, n_blks):
    los = jnp.arange(n_chunks, dtype=I32) * n_blks
    return jax.vmap(lambda lo: _moe_steps(counts, nblk, blk_start, n_tiles, lo, n_blks))(los)


_FIN_TM = 256
_FIN_TN = 512
FIN_CHUNKS = 4


def _unpack_expert_rows(words):
    u = lax.bitcast_convert_type(words, U32)
    hi = lax.bitcast_convert_type(u & jnp.uint32(0xFFFF0000), F32)
    lo = lax.bitcast_convert_type(u << 16, F32)
    parts = []
    for n in range(_DN_TILES):
        cols = slice(n * _DN_HALF, (n + 1) * _DN_HALF)
        parts += [hi[:, cols], lo[:, cols]]
    return jnp.concatenate(parts, axis=1)


def _final_kernel(prev_ref, x1_ref, y0_ref, y1_ref, y2_ref, y3_ref, gate_ref, g_ref, wg_ref, p_ref, wp_ref,
                  o_ref, x2_ref):
    del prev_ref
    gate = gate_ref[...]
    moe = (_unpack_expert_rows(y0_ref[0]) * gate[:, 0:1] + _unpack_expert_rows(y1_ref[0]) * gate[:, 1:2]
           + _unpack_expert_rows(y2_ref[0]) * gate[:, 2:3] + _unpack_expert_rows(y3_ref[0]) * gate[:, 3:4])
    x2 = x1_ref[...] + moe
    x2_ref[...] = x2
    hp = (x2 * lax.rsqrt(jnp.mean(x2 * x2, axis=-1, keepdims=True) + EPS) * g_ref[...]).astype(BF16)
    pb = p_ref[...].astype(BF16)
    for c in range(0, D_MODEL, _FIN_TN):
        cols = slice(c, c + _FIN_TN)
        emb = _dot(pb, wp_ref[:, cols])
        o_ref[:, cols] = x2_ref[:, cols] + _sigmoid(_dot(hp, wg_ref[:, cols])) * emb


def _final(out_prev, x1, y4, gate, g_ple, w_ple_gate, p, w_ple, tok0, out0, n, n_out, name):
    t0 = tok0 // _FIN_TM
    o0 = out0 // _FIN_TM
    pt0 = out0 // _FIN_TM
    const = lambda i: (0, 0)
    yspec = lambda k: pl.BlockSpec((1, _FIN_TM, _HALF), lambda i: (k, i, 0))
    once = pl.Buffered(1)
    aliases = {} if out_prev is None else {0: 0}
    prev = jnp.zeros((SUBLANES, LANES), F32) if out_prev is None else out_prev
    return pl.pallas_call(
        _final_kernel,
        grid=(n // _FIN_TM,),
        in_specs=[
            pl.BlockSpec(memory_space=pl.ANY),
            pl.BlockSpec((_FIN_TM, D_MODEL), lambda i: (t0 + i, 0)),
            yspec(0), yspec(1), yspec(2), yspec(3),
            pl.BlockSpec((_FIN_TM, LANES), lambda i: (t0 + i, 0)),
            pl.BlockSpec((1, D_MODEL), const),
            pl.BlockSpec((D_MODEL, D_MODEL), const, pipeline_mode=once),
            pl.BlockSpec((_FIN_TM, PLE_DIM), lambda i: (pt0 + i, 0)),
            pl.BlockSpec((PLE_DIM, D_MODEL), const, pipeline_mode=once),
        ],
        out_specs=pl.BlockSpec((_FIN_TM, D_MODEL), lambda i: (o0 + i, 0)),
        out_shape=jax.ShapeDtypeStruct((n_out, D_MODEL), F32),
        scratch_shapes=[pltpu.VMEM((_FIN_TM, D_MODEL), F32)],
        input_output_aliases=aliases,
        compiler_params=_cparams(1),
        name=name,
    )(prev, x1, y4, y4, y4, y4, gate, g_ple, w_ple_gate, p, w_ple)


def _rope_layout(x):
    half = ROPE_DIM // 2
    z = jnp.zeros(x.shape[:-1] + (half,), x.dtype)
    return jnp.concatenate([x[..., :half], z, x[..., half:], z], axis=-1)


def _rope_tables():
    half = ROPE_DIM // 2
    inv_freq = ROPE_THETA ** (-jnp.arange(half, dtype=F32) / half)
    pos = jnp.arange(PAST_LEN + DEC_SEQ, dtype=I32)
    ang = pos.astype(F32)[:, None] * inv_freq[None, :]
    cos, sin = jnp.cos(ang), jnp.sin(ang)
    z = jnp.zeros_like(cos)
    c = jnp.concatenate([cos, z, cos, z], axis=-1)
    s = jnp.concatenate([-sin, z, sin, z], axis=-1)
    rep = ATT_TM // DEC_SEQ
    return (jnp.concatenate([c[:SEQ], jnp.tile(c[PAST_LEN:], (rep, 1))], axis=0),
            jnp.concatenate([s[:SEQ], jnp.tile(s[PAST_LEN:], (rep, 1))], axis=0))


def _layer(xp, xs, p_prompt, p_sample, cache_kv, cache_kr, state_conv,
           g_mix, w_in, b_gate, w_dw, b_dw, g_cn, b_cn, w_conv_out,
           g_qa, g_kva, w_qb, w_kb, w_vb, g_qn, g_kn, w_o, w_out,
           g_ffn, w_router, b_router, w_gu, b_gu, w_dn, b_dn,
           g_ple, w_ple_gate, w_ple):
    assert SEQ == PAST_LEN
    row = lambda v: v.reshape(1, -1)
    w_in_b = w_in.astype(BF16)
    w_mid = jnp.concatenate([w_in_b[:, O_U:O_KV], _rope_layout(w_in_b[:, O_KV:O_KR])], axis=1)
    w_gate = w_in_b[:, O_KR:]

    h, q_lat, kv_p, kv_s, kr_pad = _in_mid(xp, xs, row(g_mix), w_mid, row(g_qa), row(g_kva))
    half = ROPE_DIM // 2
    kr_new = jnp.concatenate([kr_pad[:, :half], kr_pad[:, 2 * half:3 * half]], axis=1)
    glu = _in_glu(h, w_in_b)

    hist_s = jnp.pad(state_conv, ((0, 0), (HALO - (CONV_WIDTH - 1), 0), (0, 0)))
    c_act = _conv_module(glu, hist_s, w_dw, row(b_dw), row(g_cn), row(b_cn))

    cos_t, sin_t = _rope_tables()
    w_q = jnp.concatenate([w_qb[..., :NOPE_DIM], _rope_layout(w_qb[..., NOPE_DIM:])], axis=-1)
    w_q = w_q.reshape(Q_LORA_RANK, N_HEADS * HEAD_PAD).astype(BF16)
    g_q = jnp.concatenate([g_qn[:NOPE_DIM] * g_kn[:NOPE_DIM], _rope_layout(g_qn[NOPE_DIM:])]).reshape(1, HEAD_PAD)
    q = _q_heads(q_lat, w_q, g_q, cos_t, sin_t)

    w_kv = jnp.concatenate([w_kb, w_vb], axis=-1).reshape(KV_LORA_RANK, N_HEADS * HEAD_PAD).astype(BF16)
    g_kn_rope = _rope_layout(g_kn[NOPE_DIM:]).reshape(1, LANES)
    k_new, v_new = _kv_heads(kv_p, kv_s, kr_pad, w_kv, g_kn_rope, cos_t, sin_t, _tab_idx_new, "kv_heads_new")
    attn = _flash_prompt(q, k_new, v_new)
    attn = _flash_sample(attn, q, cache_kv.reshape(DEC_BATCH * PAST_LEN, KV_LORA_RANK),
                         _rope_layout(cache_kr).reshape(DEC_BATCH * PAST_LEN, LANES),
                         w_kv, g_kn_rope, cos_t, sin_t, k_new, v_new)

    mix = _merge(h, c_act, attn, w_gate, row(b_gate), w_conv_out.astype(BF16), w_o.astype(BF16))

    wr = jnp.pad(w_router, ((0, 0), (0, LANES - N_EXPERTS)))
    wr_hi, wr_lo = _split_bf16(wr)
    b_r = jnp.concatenate([b_router, jnp.full((LANES - N_EXPERTS,), -jnp.inf, F32)]).reshape(1, LANES)
    x1, hm, idx_pad, gate_pad = _out_router(mix, xp, xs, w_out.astype(BF16), row(g_ffn), wr_hi, wr_lo, b_r)
    hm = hm.reshape(2 * N_TOK, _HALF)

    top_idx = idx_pad[:, :TOP_K]
    dest, row_tok, counts, nblk, blk_start = _moe_dispatch(top_idx)
    b_gu3 = b_gu.reshape(N_EXPERTS, 1, 2 * D_FF)
    chunk_rows = _CHUNK_BLKS * MOE_BLK
    up_plans = _moe_plans(counts, nblk, blk_start, _UP_TILES, MOE_CHUNKS, _CHUNK_BLKS)
    down_plan = _moe_plans(counts, nblk, blk_start, _DN_TILES, 1, MOE_MAX_BLKS)[0]
    act = None
    for c in range(MOE_CHUNKS):
        xs = hm.at[row_tok[c * chunk_rows:(c + 1) * chunk_rows]].get(mode="promise_in_bounds")
        act = _moe_up(up_plans[c], act, xs, w_gu, b_gu3, c)
    ys = _moe_down(down_plan, act, w_dn, b_dn.reshape(N_EXPERTS, 1, D_MODEL))

    dest_t = dest.reshape(N_TOK, TOP_K).T
    fin = (row(g_ple), w_ple_gate.astype(BF16))
    w_ple_b = w_ple.astype(BF16)
    n_c = N_P // FIN_CHUNKS
    out_p = None
    for c in range(FIN_CHUNKS):
        y4 = ys.at[dest_t[:, c * n_c:(c + 1) * n_c]].get(mode="promise_in_bounds")
        out_p = _final(out_p, x1, y4, gate_pad, *fin, p_prompt, w_ple_b, c * n_c, c * n_c, n_c, N_P,
                       f"final_prompt_{c}")
    y4 = ys.at[dest_t[:, N_P:]].get(mode="promise_in_bounds")
    out_s = _final(None, x1, y4, gate_pad, *fin, p_sample, w_ple_b, N_P, 0, N_S, N_S, "final_sample")
    return out_p, out_s, kv_p, kv_s, kr_new, glu


def kernel(x_prompt, x_sample, cache_kv_latent, cache_k_rope, state_conv, p_prompt, p_sample, g_mix, w_in, b_gate, w_dw, b_dw, g_cn, b_cn, w_conv_out, g_qa, g_kva, w_qb, w_kb, w_vb, g_qn, g_kn, w_o, w_out, g_ffn, w_router, b_router, w_gu, b_gu, w_dn, b_dn, g_ple, w_ple_gate, w_ple):
    assert g_mix.shape[0] == 1
    out_p, out_s, kv_p, kv_s, kr_new, glu = _layer(
        x_prompt.reshape(N_P, D_MODEL), x_sample.reshape(N_S, D_MODEL),
        p_prompt[0].reshape(N_P, PLE_DIM), p_sample[0].reshape(N_S, PLE_DIM),
        cache_kv_latent[0], cache_k_rope[0], state_conv[0],
        g_mix[0], w_in[0], b_gate[0], w_dw[0], b_dw[0], g_cn[0], b_cn[0], w_conv_out[0],
        g_qa[0], g_kva[0], w_qb[0], w_kb[0], w_vb[0], g_qn[0], g_kn[0], w_o[0], w_out[0],
        g_ffn[0], w_router[0], b_router[0], w_gu[0], b_gu[0], w_dn[0], b_dn[0],
        g_ple[0], w_ple_gate[0], w_ple[0])
    tail = CONV_WIDTH - 1
    conv_p = jnp.stack([glu[(b + 1) * SEQ - tail:(b + 1) * SEQ] for b in range(BATCH)])
    conv_s = glu[N_P:].reshape(DEC_BATCH, DEC_SEQ, CONV_CHANNELS)[:, DEC_SEQ - tail:]
    return (out_p.reshape(BATCH, SEQ, D_MODEL),
            out_s.reshape(DEC_BATCH, DEC_SEQ, D_MODEL),
            kv_p.reshape(1, BATCH, SEQ, KV_LORA_RANK),
            kr_new[:N_P].reshape(1, BATCH, SEQ, ROPE_DIM),
            conv_p[None],
            kv_s.reshape(1, DEC_BATCH, DEC_SEQ, KV_LORA_RANK),
            kr_new[N_P:].reshape(1, DEC_BATCH, DEC_SEQ, ROPE_DIM),
            conv_s[None])
```

```python
import functools
import math

import jax
import jax.numpy as jnp
from jax import lax
from jax.experimental import pallas as pl
from jax.experimental.pallas import tpu as pltpu

F32 = jnp.float32
BF16 = jnp.bfloat16
I32 = jnp.int32
U32 = jnp.uint32

D_MODEL = 2048
BATCH = 2
SEQ = 4096
DEC_BATCH = 8
DEC_SEQ = 64
PAST_LEN = 4096
CHUNK = 64
CONV_CHANNELS = D_MODEL
CONV_WIDTH = 31
N_HEADS = 16
Q_LORA_RANK = 512
KV_LORA_RANK = 512
NOPE_DIM = 128
ROPE_DIM = 64
QK_DIM = NOPE_DIM + ROPE_DIM
V_DIM = 128
ROPE_THETA = 10000.0
N_EXPERTS = 32
TOP_K = 4
D_FF = D_MODEL
SWIGLU_ALPHA = 1.702
SWIGLU_LIMIT = 7.0
PLE_DIM = 256
EPS = 1e-6
NEG_INF = -1e30

N_P = BATCH * SEQ
N_S = DEC_BATCH * DEC_SEQ
N_TOK = N_P + N_S
O_U = 2 * CONV_CHANNELS
O_Q = O_U + Q_LORA_RANK
O_KV = O_Q + KV_LORA_RANK
O_KR = O_KV + ROPE_DIM
LANES = 128
SUBLANES = 8
MID_W = Q_LORA_RANK + KV_LORA_RANK + LANES
HEAD_PAD = NOPE_DIM + LANES

TM = 512
CONV_T = 64
HALO = 32
MOE_BLK = 512
MOE_MAX_BLKS = (N_TOK * TOP_K) // MOE_BLK + N_EXPERTS
MOE_ROWS = MOE_MAX_BLKS * MOE_BLK
VMEM_LIMIT = 48 * 1024 * 1024
assert 2 * ROPE_DIM == LANES and NOPE_DIM == LANES and V_DIM == LANES and SEQ == PAST_LEN


def _cparams(n_axes):
    return pltpu.CompilerParams(dimension_semantics=("arbitrary",) * n_axes,
                                vmem_limit_bytes=VMEM_LIMIT)


def _sigmoid(x):
    return 1.0 / (1.0 + jnp.exp(-x))


def _dot(a, b):
    return jnp.dot(a, b, preferred_element_type=F32)


def _stacked_rows(i, n_prompt_tiles, xp_ref, xs_ref):
    return jnp.where(i < n_prompt_tiles, xp_ref[...], xs_ref[...])


def _in_mid_kernel(xp_ref, xs_ref, g_ref, w_ref, gqa_ref, gkva_ref, h_ref, q_ref, kvp_ref, kvs_ref, kr_ref):
    i = pl.program_id(0)
    x = _stacked_rows(i, N_P // TM, xp_ref, xs_ref)
    h = x * lax.rsqrt(jnp.mean(x * x, axis=-1, keepdims=True) + EPS) * g_ref[...]
    hb = h.astype(BF16)
    h_ref[...] = hb
    z = _dot(hb, w_ref[...])
    ql = z[:, :Q_LORA_RANK]
    kvl = z[:, Q_LORA_RANK:Q_LORA_RANK + KV_LORA_RANK]
    qn = ql * lax.rsqrt(jnp.mean(ql * ql, axis=-1, keepdims=True) + EPS) * gqa_ref[...]
    q_ref[...] = qn.astype(BF16)
    kv = kvl * lax.rsqrt(jnp.mean(kvl * kvl, axis=-1, keepdims=True) + EPS) * gkva_ref[...]
    kr_ref[...] = z[:, Q_LORA_RANK + KV_LORA_RANK:]

    @pl.when(i < N_P // TM)
    def _():
        kvp_ref[...] = kv

    @pl.when(i >= N_P // TM)
    def _():
        kvs_ref[...] = kv


def _in_mid(xp, xs, g_mix, w_mid, g_qa, g_kva):
    n = N_TOK
    npt = N_P // TM
    return pl.pallas_call(
        _in_mid_kernel,
        grid=(n // TM,),
        in_specs=[
            pl.BlockSpec((TM, D_MODEL), lambda i: (jnp.minimum(i, npt - 1), 0)),
            pl.BlockSpec((TM, D_MODEL), lambda i: (jnp.maximum(i - npt, 0), 0)),
            pl.BlockSpec((1, D_MODEL), lambda i: (0, 0)),
            pl.BlockSpec((D_MODEL, MID_W), lambda i: (0, 0)),
            pl.BlockSpec((1, Q_LORA_RANK), lambda i: (0, 0)),
            pl.BlockSpec((1, KV_LORA_RANK), lambda i: (0, 0)),
        ],
        out_specs=[
            pl.BlockSpec((TM, D_MODEL), lambda i: (i, 0)),
            pl.BlockSpec((TM, Q_LORA_RANK), lambda i: (i, 0)),
            pl.BlockSpec((TM, KV_LORA_RANK), lambda i: (jnp.minimum(i, npt - 1), 0)),
            pl.BlockSpec((TM, KV_LORA_RANK), lambda i: (jnp.maximum(i - npt, 0), 0)),
            pl.BlockSpec((TM, LANES), lambda i: (i, 0)),
        ],
        out_shape=[
            jax.ShapeDtypeStruct((n, D_MODEL), BF16),
            jax.ShapeDtypeStruct((n, Q_LORA_RANK), BF16),
            jax.ShapeDtypeStruct((N_P, KV_LORA_RANK), F32),
            jax.ShapeDtypeStruct((N_S, KV_LORA_RANK), F32),
            jax.ShapeDtypeStruct((n, LANES), F32),
        ],
        compiler_params=_cparams(1),
        name="in_mid",
    )(xp, xs, g_mix, w_mid, g_qa, g_kva)


def _glu_kernel(h_ref, w1_ref, w2_ref, o_ref):
    h = h_ref[...]
    o_ref[...] = _dot(h, w1_ref[...]) * _sigmoid(_dot(h, w2_ref[...]))


def _in_glu(h, w_in_b):
    n = h.shape[0]
    tn = 1024
    nj = CONV_CHANNELS // tn
    return pl.pallas_call(
        _glu_kernel,
        grid=(n // TM, nj),
        in_specs=[
            pl.BlockSpec((TM, D_MODEL), lambda i, j: (i, 0)),
            pl.BlockSpec((D_MODEL, tn), lambda i, j: (0, j)),
            pl.BlockSpec((D_MODEL, tn), lambda i, j: (0, j + nj)),
        ],
        out_specs=pl.BlockSpec((TM, tn), lambda i, j: (i, j)),
        out_shape=jax.ShapeDtypeStruct((n, CONV_CHANNELS), F32),
        compiler_params=_cparams(2),
        name="in_glu",
    )(h, w_in_b, w_in_b)


CONV_TM = 256
_CONV_SUBS = CONV_TM // CONV_T
_CONV_SEQ_TILES = SEQ // CONV_TM
_CONV_PROMPT_TILES = N_P // CONV_TM
_CONV_LANES = 512
_SHIFT_ROWS = (HALO // SUBLANES - 1) * SUBLANES + CONV_T


def _conv_kernel(cur_ref, prev_ref, hist_ref, w_ref, bdw_ref, g_ref, b_ref, o_ref, win_ref, conv_ref, shift_ref):
    i = pl.program_id(0)
    is_sample = i >= _CONV_PROMPT_TILES
    opens = i % _CONV_SEQ_TILES == 0
    base = HALO - (CONV_WIDTH - 1)
    for j in range(_CONV_SUBS):
        r0 = j * CONV_T
        before = jnp.where(opens, 0.0, prev_ref[...]) if j == 0 else cur_ref[r0 - HALO:r0, :]
        win_ref[0:HALO, :] = jnp.where(is_sample, hist_ref[j], before)
        win_ref[HALO:HALO + CONV_T, :] = cur_ref[r0:r0 + CONV_T, :]
        for r in range(1, SUBLANES):
            shift_ref[r - 1] = win_ref[r:r + _SHIFT_ROWS, :]
        for c in range(0, CONV_CHANNELS, _CONV_LANES):
            acc = jnp.zeros((CONV_T, _CONV_LANES), F32)
            for k in range(CONV_WIDTH):
                q, r = divmod(base + k, SUBLANES)
                lanes = slice(c, c + _CONV_LANES)
                rows = slice(q * SUBLANES, q * SUBLANES + CONV_T)
                src = win_ref[rows, lanes] if r == 0 else shift_ref[r - 1, rows, lanes]
                acc = acc + w_ref[k:k + 1, lanes] * src
            conv_ref[:, c:c + _CONV_LANES] = acc + bdw_ref[:, c:c + _CONV_LANES]
        y = conv_ref[...]
        yc = y - jnp.mean(y, axis=-1, keepdims=True)
        var = jnp.mean(yc * yc, axis=-1, keepdims=True)
        z = yc * lax.rsqrt(var + EPS) * g_ref[...] + b_ref[...]
        o_ref[r0:r0 + CONV_T, :] = (z * _sigmoid(z)).astype(BF16)


def _conv_module(glu, hist_s, w_dw, b_dw, g_cn, b_cn):
    n = glu.shape[0]
    halo_per_tile = CONV_TM // HALO
    n_sample_tiles = N_S // CONV_TM
    const = lambda i: (0, 0)
    return pl.pallas_call(
        _conv_kernel,
        grid=(n // CONV_TM,),
        in_specs=[
            pl.BlockSpec((CONV_TM, CONV_CHANNELS), lambda i: (i, 0)),
            pl.BlockSpec((HALO, CONV_CHANNELS), lambda i: (jnp.maximum(i * halo_per_tile - 1, 0), 0)),
            pl.BlockSpec((_CONV_SUBS, HALO, CONV_CHANNELS),
                         lambda i: (jnp.clip(i - _CONV_PROMPT_TILES, 0, n_sample_tiles - 1), 0, 0)),
            pl.BlockSpec((CONV_WIDTH, CONV_CHANNELS), const),
            pl.BlockSpec((1, CONV_CHANNELS), const),
            pl.BlockSpec((1, CONV_CHANNELS), const),
            pl.BlockSpec((1, CONV_CHANNELS), const),
        ],
        out_specs=pl.BlockSpec((CONV_TM, CONV_CHANNELS), lambda i: (i, 0)),
        out_shape=jax.ShapeDtypeStruct((n, CONV_CHANNELS), BF16),
        scratch_shapes=[pltpu.VMEM((HALO + CONV_T, CONV_CHANNELS), F32),
                        pltpu.VMEM((CONV_T, CONV_CHANNELS), F32),
                        pltpu.VMEM((SUBLANES - 1, _SHIFT_ROWS, CONV_CHANNELS), F32)],
        compiler_params=_cparams(1),
        name="conv_module",
    )(glu, glu, hist_s, w_dw, b_dw, g_cn, b_cn)


ATT_TM = 512
_TAB_PROMPT_TILES = N_P // ATT_TM
_TAB_SEQ_TILES = SEQ // ATT_TM


def _tab_idx_new(i):
    return jnp.where(i < _TAB_PROMPT_TILES, i % _TAB_SEQ_TILES, _TAB_SEQ_TILES)


def _rope_pair(u, c, s):
    return u * c + pltpu.roll(u, LANES // 2, 1) * s


_Q_SCALE = math.log2(math.e) / math.sqrt(QK_DIM)


def _q_heads_kernel(ql_ref, w_ref, g_ref, c_ref, s_ref, o_ref):
    ql = ql_ref[...]
    g = g_ref[...]
    c = c_ref[...]
    s = s_ref[...]
    for h in range(N_HEADS):
        qf = _dot(ql, w_ref[:, h * HEAD_PAD:(h + 1) * HEAD_PAD])
        ssq = jnp.sum(qf * qf, axis=-1, keepdims=True)
        qn = qf * (lax.rsqrt(ssq * (1.0 / QK_DIM) + EPS) * _Q_SCALE) * g
        o_ref[h, :, :NOPE_DIM] = qn[:, :NOPE_DIM].astype(BF16)
        o_ref[h, :, NOPE_DIM:] = _rope_pair(qn[:, NOPE_DIM:], c, s).astype(BF16)


def _q_heads(q_lat, w_q, g_q, cos_t, sin_t):
    n = q_lat.shape[0]
    return pl.pallas_call(
        _q_heads_kernel,
        grid=(n // ATT_TM,),
        in_specs=[
            pl.BlockSpec((ATT_TM, Q_LORA_RANK), lambda i: (i, 0)),
            pl.BlockSpec((Q_LORA_RANK, N_HEADS * HEAD_PAD), lambda i: (0, 0)),
            pl.BlockSpec((1, HEAD_PAD), lambda i: (0, 0)),
            pl.BlockSpec((ATT_TM, LANES), lambda i: (_tab_idx_new(i), 0)),
            pl.BlockSpec((ATT_TM, LANES), lambda i: (_tab_idx_new(i), 0)),
        ],
        out_specs=pl.BlockSpec((N_HEADS, ATT_TM, HEAD_PAD), lambda i: (0, i, 0)),
        out_shape=jax.ShapeDtypeStruct((N_HEADS, n, HEAD_PAD), BF16),
        compiler_params=_cparams(1),
        name="q_heads",
    )(q_lat, w_q, g_q, cos_t, sin_t)


def _kv_heads_kernel(kvp_ref, kvs_ref, kr_ref, w_ref, gr_ref, c_ref, s_ref, k_ref, v_ref):
    kv = _stacked_rows(pl.program_id(0), N_P // ATT_TM, kvp_ref, kvs_ref).astype(BF16)
    u = kr_ref[...]
    ssq_r = jnp.sum(u * u, axis=-1, keepdims=True)
    krot = _rope_pair(u * gr_ref[...], c_ref[...], s_ref[...])
    for h in range(N_HEADS):
        z = _dot(kv, w_ref[:, h * HEAD_PAD:(h + 1) * HEAD_PAD])
        kn = z[:, :NOPE_DIM]
        ssq = jnp.sum(kn * kn, axis=-1, keepdims=True) + ssq_r
        scale = lax.rsqrt(ssq * (1.0 / QK_DIM) + EPS)
        k_ref[h, :, :NOPE_DIM] = (kn * scale).astype(BF16)
        k_ref[h, :, NOPE_DIM:] = (krot * scale).astype(BF16)
        v_ref[h] = z[:, NOPE_DIM:].astype(BF16)


def _kv_heads(kv_p, kv_s, kr_pad, w_kv, g_kn_rope, cos_t, sin_t, tab_idx, name):
    n = N_TOK
    npt = N_P // ATT_TM
    return pl.pallas_call(
        _kv_heads_kernel,
        grid=(n // ATT_TM,),
        in_specs=[
            pl.BlockSpec((ATT_TM, KV_LORA_RANK), lambda i: (jnp.minimum(i, npt - 1), 0)),
            pl.BlockSpec((ATT_TM, KV_LORA_RANK), lambda i: (jnp.maximum(i - npt, 0), 0)),
            pl.BlockSpec((ATT_TM, LANES), lambda i: (i, 0)),
            pl.BlockSpec((KV_LORA_RANK, N_HEADS * HEAD_PAD), lambda i: (0, 0)),
            pl.BlockSpec((1, LANES), lambda i: (0, 0)),
            pl.BlockSpec((ATT_TM, LANES), lambda i: (tab_idx(i), 0)),
            pl.BlockSpec((ATT_TM, LANES), lambda i: (tab_idx(i), 0)),
        ],
        out_specs=[
            pl.BlockSpec((N_HEADS, ATT_TM, HEAD_PAD), lambda i: (0, i, 0)),
            pl.BlockSpec((N_HEADS, ATT_TM, V_DIM), lambda i: (0, i, 0)),
        ],
        out_shape=[
            jax.ShapeDtypeStruct((N_HEADS, n, HEAD_PAD), BF16),
            jax.ShapeDtypeStruct((N_HEADS, n, V_DIM), BF16),
        ],
        compiler_params=_cparams(1),
        name=name,
    )(kv_p, kv_s, kr_pad, w_kv, g_kn_rope, cos_t, sin_t)


_TQ = 512
_TKB = 512
_HB = 4
_HBP = 4


def _flash_prompt_kernel(q_ref, k_ref, v_ref, o_ref, m_ref, l_ref, acc_ref):
    qi = pl.program_id(2)
    m_ref[...] = jnp.full(m_ref.shape, NEG_INF, F32)
    l_ref[...] = jnp.zeros(l_ref.shape, F32)
    acc_ref[...] = jnp.zeros(acc_ref.shape, F32)
    nlb = _TKB // LANES

    def step(ki, masked):
        start = pl.multiple_of(ki * _TKB, _TKB)
        scores = [lax.dot_general(q_ref[hh], k_ref[hh, pl.ds(start, _TKB), :], (((1,), (1,)), ((), ())),
                                  preferred_element_type=F32) for hh in range(_HBP)]
        probs = []
        for hh in range(_HBP):
            s = scores[hh]
            if masked:
                rc = lax.broadcasted_iota(I32, (_TQ, _TKB), 0) // CHUNK
                cc = lax.broadcasted_iota(I32, (_TQ, _TKB), 1) // CHUNK
                s = jnp.where(cc <= rc, s, NEG_INF)
            sb = [s[:, c * LANES:(c + 1) * LANES] for c in range(nlb)]
            bm = sb[0]
            for c in range(1, nlb):
                bm = jnp.maximum(bm, sb[c])
            m_prev = m_ref[hh]
            m_new = jnp.maximum(m_prev, jnp.max(bm, axis=-1, keepdims=True))
            alpha = jnp.exp2(m_prev - m_new)
            ps = [jnp.exp2(x - m_new) for x in sb]
            psum = ps[0]
            for c in range(1, nlb):
                psum = psum + ps[c]
            l_ref[hh] = alpha * l_ref[hh] + psum
            m_ref[hh] = m_new
            probs.append((alpha, jnp.concatenate(ps, axis=1).astype(BF16)))
        for hh in range(_HBP):
            alpha, p = probs[hh]
            acc_ref[hh] = alpha * acc_ref[hh] + _dot(p, v_ref[hh, pl.ds(start, _TKB), :])

    def body(kp, carry):
        step(2 * kp, False)
        step(2 * kp + 1, False)
        return carry

    lax.fori_loop(0, qi // 2, body, 0)

    @pl.when(qi % 2 == 1)
    def _():
        step(qi - 1, False)

    step(qi, True)
    for hh in range(_HBP):
        l = jnp.sum(l_ref[hh], axis=-1, keepdims=True)
        o_ref[:, hh * V_DIM:(hh + 1) * V_DIM] = (acc_ref[hh] / l).astype(BF16)


def _flash_prompt(q, k, v):
    nq = SEQ // _TQ
    return pl.pallas_call(
        _flash_prompt_kernel,
        grid=(BATCH, N_HEADS // _HBP, nq),
        in_specs=[
            pl.BlockSpec((_HBP, _TQ, HEAD_PAD), lambda b, h, i: (h, b * nq + i, 0)),
            pl.BlockSpec((_HBP, SEQ, HEAD_PAD), lambda b, h, i: (h, b, 0)),
            pl.BlockSpec((_HBP, SEQ, V_DIM), lambda b, h, i: (h, b, 0)),
        ],
        out_specs=pl.BlockSpec((_TQ, _HBP * V_DIM), lambda b, h, i: (b * nq + i, h)),
        out_shape=jax.ShapeDtypeStruct((N_TOK, N_HEADS * V_DIM), BF16),
        scratch_shapes=[pltpu.VMEM((_HBP, _TQ, LANES), F32), pltpu.VMEM((_HBP, _TQ, LANES), F32),
                        pltpu.VMEM((_HBP, _TQ, V_DIM), F32)],
        compiler_params=_cparams(3),
        name="flash_prompt",
    )(q, k, v)


_KC_ROWS = 512


def _flash_sample_kernel(prev_ref, q_ref, kv_ref, kr_ref, w_ref, gr_ref, c_ref, s_ref, kn_ref, vn_ref,
                         o_ref, kvb_ref, krot_ref, ssqr_ref, k_ref, v_ref):
    del prev_ref

    @pl.when(pl.program_id(1) == 0)
    def _():
        kvb_ref[...] = kv_ref[...].astype(BF16)
        u = kr_ref[...]
        ssqr_ref[...] = jnp.broadcast_to(jnp.sum(u * u, axis=-1, keepdims=True), ssqr_ref.shape)
        krot_ref[...] = _rope_pair(u * gr_ref[...], c_ref[...], s_ref[...])

    nt = (((1,), (1,)), ((), ()))
    for hh in range(_HB):
        w = w_ref[:, hh * HEAD_PAD:(hh + 1) * HEAD_PAD]
        for r in range(0, PAST_LEN, _KC_ROWS):
            rows = slice(r, r + _KC_ROWS)
            z = _dot(kvb_ref[rows, :], w)
            kn = z[:, :NOPE_DIM]
            ssq = jnp.sum(kn * kn, axis=-1, keepdims=True) + ssqr_ref[rows, :]
            scale = lax.rsqrt(ssq * (1.0 / QK_DIM) + EPS)
            k_ref[rows, :NOPE_DIM] = (kn * scale).astype(BF16)
            k_ref[rows, NOPE_DIM:] = (krot_ref[rows, :] * scale).astype(BF16)
            v_ref[rows, :] = z[:, NOPE_DIM:].astype(BF16)
        q = q_ref[hh]
        s1 = lax.dot_general(q, k_ref[...], nt, preferred_element_type=F32)
        s2 = lax.dot_general(q, kn_ref[hh], nt, preferred_element_type=F32)
        m = jnp.maximum(jnp.max(s1, axis=-1, keepdims=True), jnp.max(s2, axis=-1, keepdims=True))
        p1 = jnp.exp2(s1 - m)
        p2 = jnp.exp2(s2 - m)
        l = jnp.sum(p1, axis=-1, keepdims=True) + jnp.sum(p2, axis=-1, keepdims=True)
        o = _dot(p1.astype(BF16), v_ref[...]) + _dot(p2.astype(BF16), vn_ref[hh])
        o_ref[:, hh * V_DIM:(hh + 1) * V_DIM] = (o / l).astype(BF16)


def _flash_sample(attn, q, cache_kv, cache_kr_pad, w_kv, g_kn_rope, cos_t, sin_t, k_new, v_new):
    assert (PAST_LEN + DEC_SEQ - 1) // CHUNK <= PAST_LEN // CHUNK
    blk0 = N_P // DEC_SEQ
    new = lambda b, h: (h, blk0 + b, 0)
    const = lambda b, h: (0, 0)
    once = pl.Buffered(1)
    return pl.pallas_call(
        _flash_sample_kernel,
        grid=(DEC_BATCH, N_HEADS // _HB),
        in_specs=[
            pl.BlockSpec(memory_space=pl.ANY),
            pl.BlockSpec((_HB, DEC_SEQ, HEAD_PAD), new),
            pl.BlockSpec((PAST_LEN, KV_LORA_RANK), lambda b, h: (b, 0)),
            pl.BlockSpec((PAST_LEN, LANES), lambda b, h: (b, 0)),
            pl.BlockSpec((KV_LORA_RANK, _HB * HEAD_PAD), lambda b, h: (0, h)),
            pl.BlockSpec((1, LANES), const),
            pl.BlockSpec((PAST_LEN, LANES), const, pipeline_mode=once),
            pl.BlockSpec((PAST_LEN, LANES), const, pipeline_mode=once),
            pl.BlockSpec((_HB, DEC_SEQ, HEAD_PAD), new),
            pl.BlockSpec((_HB, DEC_SEQ, V_DIM), new),
        ],
        out_specs=pl.BlockSpec((DEC_SEQ, _HB * V_DIM), lambda b, h: (blk0 + b, h)),
        out_shape=jax.ShapeDtypeStruct((N_TOK, N_HEADS * V_DIM), BF16),
        scratch_shapes=[pltpu.VMEM((PAST_LEN, KV_LORA_RANK), BF16),
                        pltpu.VMEM((PAST_LEN, LANES), F32),
                        pltpu.VMEM((PAST_LEN, LANES), F32),
                        pltpu.VMEM((PAST_LEN, HEAD_PAD), BF16),
                        pltpu.VMEM((PAST_LEN, V_DIM), BF16)],
        input_output_aliases={0: 0},
        compiler_params=_cparams(2),
        name="flash_sample",
    )(attn, q, cache_kv, cache_kr_pad, w_kv, g_kn_rope, cos_t, sin_t, k_new, v_new)


def _merge_kernel(h_ref, c_ref, a_ref, wga_ref, wgb_ref, bga_ref, bgb_ref, wc_ref, wo_ref, o_ref):
    h = h_ref[...]
    ga = _sigmoid(_dot(h, wga_ref[...]) + bga_ref[...])
    gb = _sigmoid(_dot(h, wgb_ref[...]) + bgb_ref[...])
    mix = ga * _dot(c_ref[...], wc_ref[...]) + gb * _dot(a_ref[...], wo_ref[...])
    o_ref[...] = mix.astype(BF16)


def _merge(h, c_act, attn, w_gate, b_gate, w_conv_out, w_o):
    n = h.shape[0]
    tn = 512
    nj = D_MODEL // tn
    row = lambda i, j: (i, 0)
    return pl.pallas_call(
        _merge_kernel,
        grid=(n // TM, nj),
        in_specs=[
            pl.BlockSpec((TM, D_MODEL), row),
            pl.BlockSpec((TM, CONV_CHANNELS), row),
            pl.BlockSpec((TM, N_HEADS * V_DIM), row),
            pl.BlockSpec((D_MODEL, tn), lambda i, j: (0, j)),
            pl.BlockSpec((D_MODEL, tn), lambda i, j: (0, j + nj)),
            pl.BlockSpec((1, tn), lambda i, j: (0, j)),
            pl.BlockSpec((1, tn), lambda i, j: (0, j + nj)),
            pl.BlockSpec((CONV_CHANNELS, tn), lambda i, j: (0, j)),
            pl.BlockSpec((N_HEADS * V_DIM, tn), lambda i, j: (0, j)),
        ],
        out_specs=pl.BlockSpec((TM, tn), lambda i, j: (i, j)),
        out_shape=jax.ShapeDtypeStruct((n, D_MODEL), BF16),
        compiler_params=_cparams(2),
        name="merge",
    )(h, c_act, attn, w_gate, w_gate, b_gate, b_gate, w_conv_out, w_o)


def _split_bf16(x):
    hi = x.astype(BF16)
    lo = (x - hi.astype(F32)).astype(BF16)
    return hi, lo


_HALF = D_MODEL // 2


def _pack_bf16_pair(a, b):
    ua = lax.bitcast_convert_type(a.astype(BF16).astype(F32), U32)
    ub = lax.bitcast_convert_type(b.astype(BF16).astype(F32), U32)
    return lax.bitcast_convert_type(ua | (ub >> 16), F32)


def _unpack_bf16_pair(w):
    w = lax.bitcast_convert_type(w, U32)
    a = lax.bitcast_convert_type(w & jnp.uint32(0xFFFF0000), F32).astype(BF16)
    b = lax.bitcast_convert_type(w << 16, F32).astype(BF16)
    return a, b


def _out_router_kernel(n_prompt_tiles, mix_ref, xp_ref, xs_ref, w_ref, g_ref, wrh_ref, wrl_ref, br_ref,
                       x1_ref, hm_ref, idx_ref, gate_ref):
    x = _stacked_rows(pl.program_id(0), n_prompt_tiles, xp_ref, xs_ref)
    x1 = x + _dot(mix_ref[...], w_ref[...])
    x1_ref[...] = x1
    hn = x1 * lax.rsqrt(jnp.mean(x1 * x1, axis=-1, keepdims=True) + EPS) * g_ref[...]
    hm_ref[0] = _pack_bf16_pair(hn[:, :_HALF], hn[:, _HALF:])
    hm_ref[1] = jnp.zeros(hm_ref.shape[1:], F32)
    hh, hl = _split_bf16(hn)
    logits = _dot(hh, wrh_ref[...]) + (_dot(hh, wrl_ref[...]) + _dot(hl, wrh_ref[...])) + br_ref[...]
    lane = lax.broadcasted_iota(I32, logits.shape, 1).astype(F32)
    vals = []
    idx_out = jnp.zeros(logits.shape, F32)
    for k in range(TOP_K):
        m = jnp.max(logits, axis=-1, keepdims=True)
        sel = jnp.min(jnp.where(logits == m, lane, 1e9), axis=-1, keepdims=True)
        vals.append(m)
        idx_out = jnp.where(lane == float(k), sel, idx_out)
        logits = jnp.where(lane == sel, -jnp.inf, logits)
    exps = [jnp.exp(v - vals[0]) for v in vals]
    denom = exps[0] + exps[1] + exps[2] + exps[3]
    gate_out = jnp.zeros(idx_out.shape, F32)
    for k in range(TOP_K):
        gate_out = jnp.where(lane == float(k), exps[k] / denom, gate_out)
    idx_ref[...] = idx_out.astype(I32)
    gate_ref[...] = gate_out


def _out_router(mix, xp, xs, w_out, g_ffn, wr_hi, wr_lo, b_r):
    n = N_TOK
    tm = TM
    n_tiles = n // tm
    npt = N_P // tm
    const = lambda i: (0, 0)
    row = lambda i: (i, 0)
    once = pl.Buffered(1)
    return pl.pallas_call(
        functools.partial(_out_router_kernel, npt),
        grid=(n_tiles,),
        in_specs=[
            pl.BlockSpec((tm, D_MODEL), row),
            pl.BlockSpec((tm, D_MODEL), lambda i: (jnp.minimum(i, npt - 1), 0)),
            pl.BlockSpec((tm, D_MODEL), lambda i: (jnp.clip(i - npt, 0, N_S // tm - 1), 0)),
            pl.BlockSpec((D_MODEL, D_MODEL), const, pipeline_mode=once),
            pl.BlockSpec((1, D_MODEL), const),
            pl.BlockSpec((D_MODEL, LANES), const, pipeline_mode=once),
            pl.BlockSpec((D_MODEL, LANES), const, pipeline_mode=once),
            pl.BlockSpec((1, LANES), const),
        ],
        out_specs=[
            pl.BlockSpec((tm, D_MODEL), row),
            pl.BlockSpec((2, tm, _HALF), lambda i: (0, i, 0)),
            pl.BlockSpec((tm, LANES), row),
            pl.BlockSpec((tm, LANES), row),
        ],
        out_shape=[
            jax.ShapeDtypeStruct((n, D_MODEL), F32),
            jax.ShapeDtypeStruct((2, n, _HALF), F32),
            jax.ShapeDtypeStruct((n, LANES), I32),
            jax.ShapeDtypeStruct((n, LANES), F32),
        ],
        compiler_params=_cparams(1),
        name="out_router",
    )(mix, xp, xs, w_out, g_ffn, wr_hi, wr_lo, b_r)


_F_VALID, _F_FIRST, _F_NEXT, _F_GROUP0 = 1, 2, 4, 8


_P_E, _P_W, _P_N, _P_B, _P_BI, _P_NE, _P_NW, _P_FL, _P_SUBS = range(9)
MOE_SUB = 128


def _stream_weights(t, plan_ref, copies, cast):
    flags = plan_ref[_P_FL, t]

    @pl.when((flags & _F_FIRST) != 0)
    def _():
        cur = copies(plan_ref[_P_E, t], plan_ref[_P_W, t])

        @pl.when((flags & _F_GROUP0) != 0)
        def _():
            for c in cur:
                c.start()

        for c in cur:
            c.wait()
        cast()

        @pl.when((flags & _F_NEXT) != 0)
        def _():
            for c in copies(plan_ref[_P_NE, t], plan_ref[_P_NW, t]):
                c.start()


def _for_used_rows(valid, subs, rows_body):
    for n_sub in range(1, MOE_BLK // MOE_SUB + 1):
        @pl.when(jnp.logical_and(valid, subs == n_sub))
        def _(m=n_sub * MOE_SUB):
            rows_body(m)


def _moe_up_kernel(plan_ref, prev_ref, x_ref, w_hbm, bg_ref, bu_ref, o_ref, wbuf_ref, wgb_ref, wub_ref, sem_ref):
    del prev_ref
    t = pl.program_id(0)

    def copies(e, w):
        col = pl.multiple_of(w * _UP_TN, _UP_TN)
        return (pltpu.make_async_copy(w_hbm.at[e, :, pl.ds(col, _UP_TN)], wbuf_ref.at[0], sem_ref.at[0]),
                pltpu.make_async_copy(w_hbm.at[e, :, pl.ds(col + D_FF, _UP_TN)], wbuf_ref.at[1], sem_ref.at[1]))

    def cast():
        wgb_ref[...] = wbuf_ref[0].astype(BF16)
        wub_ref[...] = wbuf_ref[1].astype(BF16)

    _stream_weights(t, plan_ref, copies, cast)
    valid = (plan_ref[_P_FL, t] & _F_VALID) != 0

    def rows_body(m):
        xa, xb = _unpack_bf16_pair(x_ref[:m, :])
        g = _dot(xa, wgb_ref[:_HALF, :]) + _dot(xb, wgb_ref[_HALF:, :]) + bg_ref[0]
        u = _dot(xa, wub_ref[:_HALF, :]) + _dot(xb, wub_ref[_HALF:, :]) + bu_ref[0]
        g = jnp.minimum(g, SWIGLU_LIMIT)
        u = jnp.clip(u, -SWIGLU_LIMIT, SWIGLU_LIMIT)
        o_ref[:m, :] = ((u + 1.0) * (g * _sigmoid(SWIGLU_ALPHA * g))).astype(BF16)
        if m < MOE_BLK:
            o_ref[m:, :] = jnp.zeros((MOE_BLK - m, o_ref.shape[1]), BF16)

    _for_used_rows(valid, plan_ref[_P_SUBS, t], rows_body)

    @pl.when(jnp.logical_not(valid))
    def _():
        o_ref[...] = jnp.zeros(o_ref.shape, BF16)


_UP_TN = 1024
_UP_TILES = D_FF // _UP_TN
_DN_TN = 2048
_DN_TILES = D_MODEL // _DN_TN
MOE_CHUNKS = 4
_CHUNK_BLKS = MOE_MAX_BLKS // MOE_CHUNKS


def _moe_up(plan, act_prev, xs, w_gu, b_gu, chunk):
    steps = plan.shape[1]
    blk0 = chunk * _CHUNK_BLKS
    bspec = lambda off: pl.BlockSpec((1, 1, _UP_TN), lambda t, p: (p[_P_E, t], 0, p[_P_W, t] + off))
    aliases = {} if act_prev is None else {1: 0}
    prev = jnp.zeros((SUBLANES, LANES), BF16) if act_prev is None else act_prev
    return pl.pallas_call(
        _moe_up_kernel,
        grid_spec=pltpu.PrefetchScalarGridSpec(
            num_scalar_prefetch=1,
            grid=(steps,),
            in_specs=[
                pl.BlockSpec(memory_space=pl.ANY),
                pl.BlockSpec((MOE_BLK, _HALF), lambda t, p: (p[_P_BI, t], 0)),
                pl.BlockSpec(memory_space=pl.ANY),
                bspec(0), bspec(_UP_TILES),
            ],
            out_specs=pl.BlockSpec((MOE_BLK, _UP_TN),
                                   lambda t, p: (blk0 + p[_P_B, t], p[_P_N, t])),
            scratch_shapes=[pltpu.VMEM((2, D_MODEL, _UP_TN), F32),
                            pltpu.VMEM((D_MODEL, _UP_TN), BF16), pltpu.VMEM((D_MODEL, _UP_TN), BF16),
                            pltpu.SemaphoreType.DMA((2,))],
        ),
        out_shape=jax.ShapeDtypeStruct((MOE_ROWS, D_FF), BF16),
        input_output_aliases=aliases,
        compiler_params=_cparams(1),
        name=f"moe_up_{chunk}",
    )(plan, prev, xs, w_gu, b_gu, b_gu)


_DN_HALF = _DN_TN // 2


def _moe_down_kernel(plan_ref, a_ref, w_hbm, b_ref, o_ref, wbuf_ref, wb_ref, sem_ref):
    t = pl.program_id(0)

    def copies(e, w):
        col = pl.multiple_of(w * _DN_TN, _DN_TN)
        return (pltpu.make_async_copy(w_hbm.at[e, :, pl.ds(col, _DN_TN)], wbuf_ref, sem_ref.at[0]),)

    def cast():
        wb_ref[...] = wbuf_ref[...].astype(BF16)

    _stream_weights(t, plan_ref, copies, cast)
    valid = (plan_ref[_P_FL, t] & _F_VALID) != 0

    def rows_body(m):
        y = _dot(a_ref[:m, :], wb_ref[...]) + b_ref[0]
        o_ref[:m, :] = _pack_bf16_pair(y[:, :_DN_HALF], y[:, _DN_HALF:])
        if m < MOE_BLK:
            o_ref[m:, :] = jnp.zeros((MOE_BLK - m, o_ref.shape[1]), F32)

    _for_used_rows(valid, plan_ref[_P_SUBS, t], rows_body)

    @pl.when(jnp.logical_not(valid))
    def _():
        o_ref[...] = jnp.zeros(o_ref.shape, F32)


def _moe_down(plan, act, w_dn, b_dn):
    steps = plan.shape[1]
    return pl.pallas_call(
        _moe_down_kernel,
        grid_spec=pltpu.PrefetchScalarGridSpec(
            num_scalar_prefetch=1,
            grid=(steps,),
            in_specs=[
                pl.BlockSpec((MOE_BLK, D_FF), lambda t, p: (p[_P_BI, t], 0)),
                pl.BlockSpec(memory_space=pl.ANY),
                pl.BlockSpec((1, 1, _DN_TN), lambda t, p: (p[_P_E, t], 0, p[_P_W, t])),
            ],
            out_specs=pl.BlockSpec((MOE_BLK, _DN_HALF), lambda t, p: (p[_P_B, t], p[_P_N, t])),
            scratch_shapes=[pltpu.VMEM((D_FF, _DN_TN), F32), pltpu.VMEM((D_FF, _DN_TN), BF16),
                            pltpu.SemaphoreType.DMA((1,))],
        ),
        out_shape=jax.ShapeDtypeStruct((MOE_ROWS, _HALF), F32),
        compiler_params=_cparams(1),
        name="moe_down",
    )(plan, act, w_dn, b_dn)


def _moe_dispatch(top_idx):
    n_asg = N_TOK * TOP_K
    flat_e = top_idx.reshape(-1)
    onehot = (flat_e[:, None] == jnp.arange(N_EXPERTS, dtype=I32)[None, :]).astype(I32)
    csum = jnp.cumsum(onehot, axis=0)
    counts = csum[-1]
    rank = jnp.sum(csum * onehot, axis=1) - 1
    nblk = (counts + MOE_BLK - 1) // MOE_BLK
    blk_start = jnp.cumsum(nblk) - nblk
    dest = jnp.sum(onehot * blk_start[None, :], axis=1) * MOE_BLK + rank
    pad_src = jnp.arange(MOE_ROWS, dtype=I32) % N_TOK
    row_tok = pad_src.at[dest].set(jnp.arange(n_asg, dtype=I32) // TOP_K,
                                   mode="promise_in_bounds", unique_indices=True)
    return dest, row_tok, counts, nblk, blk_start


def _moe_steps(counts, nblk, blk_start, n_tiles, blk_lo, n_blks):
    t_max = n_tiles * n_blks
    lo = jnp.clip(blk_start, blk_lo, blk_lo + n_blks)
    hi = jnp.clip(blk_start + nblk, blk_lo, blk_lo + n_blks)
    nb_e = hi - lo
    per_e = nb_e * n_tiles
    s_end = jnp.cumsum(per_e)
    total = s_end[-1]
    t = jnp.arange(t_max, dtype=I32)
    tc = jnp.clip(t, 0, jnp.maximum(total - 1, 0))
    e = jnp.minimum(jnp.sum((s_end[None, :] <= tc[:, None]).astype(I32), axis=1), N_EXPERTS - 1)
    sel = (e[:, None] == jnp.arange(N_EXPERTS, dtype=I32)[None, :]).astype(I32)
    pick = lambda v: jnp.sum(sel * v[None, :], axis=1)
    local = tc - pick(s_end - per_e)
    nb = jnp.maximum(pick(nb_e), 1)
    w_tile = jnp.clip(local // nb, 0, n_tiles - 1)
    r = local % nb
    valid = t < total
    first = jnp.logical_and(valid, r == 0)
    fill = t - total
    blk = jnp.where(valid, pick(lo) - blk_lo + r, total // n_tiles + fill // n_tiles)
    rows_used = pick(counts) - (pick(lo) + r - pick(blk_start)) * MOE_BLK
    subs = jnp.clip((rows_used + MOE_SUB - 1) // MOE_SUB, 1, MOE_BLK // MOE_SUB)
    o_tile = jnp.where(valid, w_tile, fill % n_tiles)
    blk = jnp.clip(blk, 0, n_blks - 1)
    blk_in = jnp.where(valid, blk, jnp.maximum(total // n_tiles - 1, 0))
    ids = jnp.arange(N_EXPERTS, dtype=I32)
    owners = jnp.where(nb_e > 0, ids, N_EXPERTS)
    later = jnp.flip(lax.cummin(jnp.flip(owners)))
    next_owner = pick(jnp.concatenate([later[1:], jnp.full((1,), N_EXPERTS, I32)]))
    last_tile = w_tile == n_tiles - 1
    next_e = jnp.where(last_tile, next_owner, e)
    next_w = jnp.where(last_tile, 0, w_tile + 1)
    has_next = jnp.logical_and(first, next_e < N_EXPERTS)
    group = jnp.cumsum(first.astype(I32)) - 1
    flags = (valid * _F_VALID + first * _F_FIRST + has_next * _F_NEXT
             + jnp.logical_and(first, group == 0) * _F_GROUP0)
    rows = {_P_E: e, _P_W: w_tile, _P_N: o_tile, _P_B: blk, _P_BI: blk_in,
            _P_NE: jnp.minimum(next_e, N_EXPERTS - 1), _P_NW: next_w, _P_FL: flags, _P_SUBS: subs}
    return jnp.stack([rows[k].astype(I32) for k in range(len(rows))])


def _moe_plans(counts, nblk, blk_start, n_tiles, n_chunks, n_blks):
    los = jnp.arange(n_chunks, dtype=I32) * n_blks
    return jax.vmap(lambda lo: _moe_steps(counts, nblk, blk_start, n_tiles, lo, n_blks))(los)


_FIN_TM = 256
_FIN_TN = 512
FIN_CHUNKS = 4


def _unpack_expert_rows(words):
    u = lax.bitcast_convert_type(words, U32)
    hi = lax.bitcast_convert_type(u & jnp.uint32(0xFFFF0000), F32)
    lo = lax.bitcast_convert_type(u << 16, F32)
    parts = []
    for n in range(_DN_TILES):
        cols = slice(n * _DN_HALF, (n + 1) * _DN_HALF)
        parts += [hi[:, cols], lo[:, cols]]
    return jnp.concatenate(parts, axis=1)


def _final_kernel(prev_ref, x1_ref, y0_ref, y1_ref, y2_ref, y3_ref, gate_ref, g_ref, wg_ref, p_ref, wp_ref,
                  o_ref, x2_ref):
    del prev_ref
    gate = gate_ref[...]
    moe = (_unpack_expert_rows(y0_ref[0]) * gate[:, 0:1] + _unpack_expert_rows(y1_ref[0]) * gate[:, 1:2]
           + _unpack_expert_rows(y2_ref[0]) * gate[:, 2:3] + _unpack_expert_rows(y3_ref[0]) * gate[:, 3:4])
    x2 = x1_ref[...] + moe
    x2_ref[...] = x2
    hp = (x2 * lax.rsqrt(jnp.mean(x2 * x2, axis=-1, keepdims=True) + EPS) * g_ref[...]).astype(BF16)
    pb = p_ref[...].astype(BF16)
    for c in range(0, D_MODEL, _FIN_TN):
        cols = slice(c, c + _FIN_TN)
        emb = _dot(pb, wp_ref[:, cols])
        o_ref[:, cols] = x2_ref[:, cols] + _sigmoid(_dot(hp, wg_ref[:, cols])) * emb


def _final(out_prev, x1, y4, gate, g_ple, w_ple_gate, p, w_ple, tok0, out0, n, n_out, name):
    t0 = tok0 // _FIN_TM
    o0 = out0 // _FIN_TM
    pt0 = out0 // _FIN_TM
    const = lambda i: (0, 0)
    yspec = lambda k: pl.BlockSpec((1, _FIN_TM, _HALF), lambda i: (k, i, 0))
    once = pl.Buffered(1)
    aliases = {} if out_prev is None else {0: 0}
    prev = jnp.zeros((SUBLANES, LANES), F32) if out_prev is None else out_prev
    return pl.pallas_call(
        _final_kernel,
        grid=(n // _FIN_TM,),
        in_specs=[
            pl.BlockSpec(memory_space=pl.ANY),
            pl.BlockSpec((_FIN_TM, D_MODEL), lambda i: (t0 + i, 0)),
            yspec(0), yspec(1), yspec(2), yspec(3),
            pl.BlockSpec((_FIN_TM, LANES), lambda i: (t0 + i, 0)),
            pl.BlockSpec((1, D_MODEL), const),
            pl.BlockSpec((D_MODEL, D_MODEL), const, pipeline_mode=once),
            pl.BlockSpec((_FIN_TM, PLE_DIM), lambda i: (pt0 + i, 0)),
            pl.BlockSpec((PLE_DIM, D_MODEL), const, pipeline_mode=once),
        ],
        out_specs=pl.BlockSpec((_FIN_TM, D_MODEL), lambda i: (o0 + i, 0)),
        out_shape=jax.ShapeDtypeStruct((n_out, D_MODEL), F32),
        scratch_shapes=[pltpu.VMEM((_FIN_TM, D_MODEL), F32)],
        input_output_aliases=aliases,
        compiler_params=_cparams(1),
        name=name,
    )(prev, x1, y4, y4, y4, y4, gate, g_ple, w_ple_gate, p, w_ple)


def _rope_layout(x):
    half = ROPE_DIM // 2
    z = jnp.zeros(x.shape[:-1] + (half,), x.dtype)
    return jnp.concatenate([x[..., :half], z, x[..., half:], z], axis=-1)


def _rope_tables():
    half = ROPE_DIM // 2
    inv_freq = ROPE_THETA ** (-jnp.arange(half, dtype=F32) / half)
    pos = jnp.arange(PAST_LEN + DEC_SEQ, dtype=I32)
    ang = pos.astype(F32)[:, None] * inv_freq[None, :]
    cos, sin = jnp.cos(ang), jnp.sin(ang)
    z = jnp.zeros_like(cos)
    c = jnp.concatenate([cos, z, cos, z], axis=-1)
    s = jnp.concatenate([-sin, z, sin, z], axis=-1)
    rep = ATT_TM // DEC_SEQ
    return (jnp.concatenate([c[:SEQ], jnp.tile(c[PAST_LEN:], (rep, 1))], axis=0),
            jnp.concatenate([s[:SEQ], jnp.tile(s[PAST_LEN:], (rep, 1))], axis=0))


def _layer(xp, xs, p_prompt, p_sample, cache_kv, cache_kr, state_conv,
           g_mix, w_in, b_gate, w_dw, b_dw, g_cn, b_cn, w_conv_out,
           g_qa, g_kva, w_qb, w_kb, w_vb, g_qn, g_kn, w_o, w_out,
           g_ffn, w_router, b_router, w_gu, b_gu, w_dn, b_dn,
           g_ple, w_ple_gate, w_ple):
    assert SEQ == PAST_LEN
    row = lambda v: v.reshape(1, -1)
    w_in_b = w_in.astype(BF16)
    w_mid = jnp.concatenate([w_in_b[:, O_U:O_KV], _rope_layout(w_in_b[:, O_KV:O_KR])], axis=1)
    w_gate = w_in_b[:, O_KR:]

    h, q_lat, kv_p, kv_s, kr_pad = _in_mid(xp, xs, row(g_mix), w_mid, row(g_qa), row(g_kva))
    half = ROPE_DIM // 2
    kr_new = jnp.concatenate([kr_pad[:, :half], kr_pad[:, 2 * half:3 * half]], axis=1)
    glu = _in_glu(h, w_in_b)

    hist_s = jnp.pad(state_conv, ((0, 0), (HALO - (CONV_WIDTH - 1), 0), (0, 0)))
    c_act = _conv_module(glu, hist_s, w_dw, row(b_dw), row(g_cn), row(b_cn))

    cos_t, sin_t = _rope_tables()
    w_q = jnp.concatenate([w_qb[..., :NOPE_DIM], _rope_layout(w_qb[..., NOPE_DIM:])], axis=-1)
    w_q = w_q.reshape(Q_LORA_RANK, N_HEADS * HEAD_PAD).astype(BF16)
    g_q = jnp.concatenate([g_qn[:NOPE_DIM] * g_kn[:NOPE_DIM], _rope_layout(g_qn[NOPE_DIM:])]).reshape(1, HEAD_PAD)
    q = _q_heads(q_lat, w_q, g_q, cos_t, sin_t)

    w_kv = jnp.concatenate([w_kb, w_vb], axis=-1).reshape(KV_LORA_RANK, N_HEADS * HEAD_PAD).astype(BF16)
    g_kn_rope = _rope_layout(g_kn[NOPE_DIM:]).reshape(1, LANES)
    k_new, v_new = _kv_heads(kv_p, kv_s, kr_pad, w_kv, g_kn_rope, cos_t, sin_t, _tab_idx_new, "kv_heads_new")
    attn = _flash_prompt(q, k_new, v_new)
    attn = _flash_sample(attn, q, cache_kv.reshape(DEC_BATCH * PAST_LEN, KV_LORA_RANK),
                         _rope_layout(cache_kr).reshape(DEC_BATCH * PAST_LEN, LANES),
                         w_kv, g_kn_rope, cos_t, sin_t, k_new, v_new)

    mix = _merge(h, c_act, attn, w_gate, row(b_gate), w_conv_out.astype(BF16), w_o.astype(BF16))

    wr = jnp.pad(w_router, ((0, 0), (0, LANES - N_EXPERTS)))
    wr_hi, wr_lo = _split_bf16(wr)
    b_r = jnp.concatenate([b_router, jnp.full((LANES - N_EXPERTS,), -jnp.inf, F32)]).reshape(1, LANES)
    x1, hm, idx_pad, gate_pad = _out_router(mix, xp, xs, w_out.astype(BF16), row(g_ffn), wr_hi, wr_lo, b_r)
    hm = hm.reshape(2 * N_TOK, _HALF)

    top_idx = idx_pad[:, :TOP_K]
    dest, row_tok, counts, nblk, blk_start = _moe_dispatch(top_idx)
    b_gu3 = b_gu.reshape(N_EXPERTS, 1, 2 * D_FF)
    chunk_rows = _CHUNK_BLKS * MOE_BLK
    up_plans = _moe_plans(counts, nblk, blk_start, _UP_TILES, MOE_CHUNKS, _CHUNK_BLKS)
    down_plan = _moe_plans(counts, nblk, blk_start, _DN_TILES, 1, MOE_MAX_BLKS)[0]
    act = None
    for c in range(MOE_CHUNKS):
        xs = hm.at[row_tok[c * chunk_rows:(c + 1) * chunk_rows]].get(mode="promise_in_bounds")
        act = _moe_up(up_plans[c], act, xs, w_gu, b_gu3, c)
    ys = _moe_down(down_plan, act, w_dn, b_dn.reshape(N_EXPERTS, 1, D_MODEL))

    dest_t = dest.reshape(N_TOK, TOP_K).T
    fin = (row(g_ple), w_ple_gate.astype(BF16))
    w_ple_b = w_ple.astype(BF16)
    n_c = N_P // FIN_CHUNKS
    out_p = None
    for c in range(FIN_CHUNKS):
        y4 = ys.at[dest_t[:, c * n_c:(c + 1) * n_c]].get(mode="promise_in_bounds")
        out_p = _final(out_p, x1, y4, gate_pad, *fin, p_prompt, w_ple_b, c * n_c, c * n_c, n_c, N_P,
                       f"final_prompt_{c}")
    y4 = ys.at[dest_t[:, N_P:]].get(mode="promise_in_bounds")
    out_s = _final(None, x1, y4, gate_pad, *fin, p_sample, w_ple_b, N_P, 0, N_S, N_S, "final_sample")
    return out_p, out_s, kv_p, kv_s, kr_new, glu


def kernel(x_prompt, x_sample, cache_kv_latent, cache_k_rope, state_conv, p_prompt, p_sample, g_mix, w_in, b_gate, w_dw, b_dw, g_cn, b_cn, w_conv_out, g_qa, g_kva, w_qb, w_kb, w_vb, g_qn, g_kn, w_o, w_out, g_ffn, w_router, b_router, w_gu, b_gu, w_dn, b_dn, g_ple, w_ple_gate, w_ple):
    assert g_mix.shape[0] == 1
    out_p, out_s, kv_p, kv_s, kr_new, glu = _layer(
        x_prompt.reshape(N_P, D_MODEL), x_sample.reshape(N_S, D_MODEL),
        p_prompt[0].reshape(N_P, PLE_DIM), p_sample[0].reshape(N_S, PLE_DIM),
        cache_kv_latent[0], cache_k_rope[0], state_conv[0],
        g_mix[0], w_in[0], b_gate[0], w_dw[0], b_dw[0], g_cn[0], b_cn[0], w_conv_out[0],
        g_qa[0], g_kva[0], w_qb[0], w_kb[0], w_vb[0], g_qn[0], g_kn[0], w_o[0], w_out[0],
        g_ffn[0], w_router[0], b_router[0], w_gu[0], b_gu[0], w_dn[0], b_dn[0],
        g_ple[0], w_ple_gate[0], w_ple[0])
    tail = CONV_WIDTH - 1
    conv_p = jnp.stack([glu[(b + 1) * SEQ - tail:(b + 1) * SEQ] for b in range(BATCH)])
    conv_s = glu[N_P:].reshape(DEC_BATCH, DEC_SEQ, CONV_CHANNELS)[:, DEC_SEQ - tail:]
    return (out_p.reshape(BATCH, SEQ, D_MODEL),
            out_s.reshape(DEC_BATCH, DEC_SEQ, D_MODEL),
            kv_p.reshape(1, BATCH, SEQ, KV_LORA_RANK),
            kr_new[:N_P].reshape(1, BATCH, SEQ, ROPE_DIM),
            conv_p[None],
            kv_s.reshape(1, DEC_BATCH, DEC_SEQ, KV_LORA_RANK),
            kr_new[N_P:].reshape(1, DEC_BATCH, DEC_SEQ, ROPE_DIM),
            conv_s[None])
```

```python
import functools
import math

import jax
import jax.numpy as jnp
from jax import lax
from jax.experimental import pallas as pl
from jax.experimental.pallas import tpu as pltpu

F32 = jnp.float32
BF16 = jnp.bfloat16
I32 = jnp.int32
U32 = jnp.uint32

D_MODEL = 2048
BATCH = 2
SEQ = 4096
DEC_BATCH = 8
DEC_SEQ = 64
PAST_LEN = 4096
CHUNK = 64
CONV_CHANNELS = D_MODEL
CONV_WIDTH = 31
N_HEADS = 16
Q_LORA_RANK = 512
KV_LORA_RANK = 512
NOPE_DIM = 128
ROPE_DIM = 64
QK_DIM = NOPE_DIM + ROPE_DIM
V_DIM = 128
ROPE_THETA = 10000.0
N_EXPERTS = 32
TOP_K = 4
D_FF = D_MODEL
SWIGLU_ALPHA = 1.702
SWIGLU_LIMIT = 7.0
PLE_DIM = 256
EPS = 1e-6
NEG_INF = -1e30

N_P = BATCH * SEQ
N_S = DEC_BATCH * DEC_SEQ
N_TOK = N_P + N_S
O_U = 2 * CONV_CHANNELS
O_Q = O_U + Q_LORA_RANK
O_KV = O_Q + KV_LORA_RANK
O_KR = O_KV + ROPE_DIM
LANES = 128
SUBLANES = 8
MID_W = Q_LORA_RANK + KV_LORA_RANK + LANES
HEAD_PAD = NOPE_DIM + LANES

TM = 512
CONV_T = 64
HALO = 32
MOE_BLK = 512
MOE_MAX_BLKS = (N_TOK * TOP_K) // MOE_BLK + N_EXPERTS
MOE_ROWS = MOE_MAX_BLKS * MOE_BLK
VMEM_LIMIT = 48 * 1024 * 1024
assert 2 * ROPE_DIM == LANES and NOPE_DIM == LANES and V_DIM == LANES and SEQ == PAST_LEN


def _cparams(n_axes):
    return pltpu.CompilerParams(dimension_semantics=("arbitrary",) * n_axes,
                                vmem_limit_bytes=VMEM_LIMIT)


def _sigmoid(x):
    return 1.0 / (1.0 + jnp.exp(-x))


def _dot(a, b):
    return jnp.dot(a, b, preferred_element_type=F32)


def _stacked_rows(i, n_prompt_tiles, xp_ref, xs_ref):
    return jnp.where(i < n_prompt_tiles, xp_ref[...], xs_ref[...])


def _in_mid_kernel(xp_ref, xs_ref, g_ref, w_ref, gqa_ref, gkva_ref, h_ref, q_ref, kvp_ref, kvs_ref, kr_ref):
    i = pl.program_id(0)
    x = _stacked_rows(i, N_P // TM, xp_ref, xs_ref)
    h = x * lax.rsqrt(jnp.mean(x * x, axis=-1, keepdims=True) + EPS) * g_ref[...]
    hb = h.astype(BF16)
    h_ref[...] = hb
    z = _dot(hb, w_ref[...])
    ql = z[:, :Q_LORA_RANK]
    kvl = z[:, Q_LORA_RANK:Q_LORA_RANK + KV_LORA_RANK]
    qn = ql * lax.rsqrt(jnp.mean(ql * ql, axis=-1, keepdims=True) + EPS) * gqa_ref[...]
    q_ref[...] = qn.astype(BF16)
    kv = kvl * lax.rsqrt(jnp.mean(kvl * kvl, axis=-1, keepdims=True) + EPS) * gkva_ref[...]
    kr_ref[...] = z[:, Q_LORA_RANK + KV_LORA_RANK:]

    @pl.when(i < N_P // TM)
    def _():
        kvp_ref[...] = kv

    @pl.when(i >= N_P // TM)
    def _():
        kvs_ref[...] = kv


def _in_mid(xp, xs, g_mix, w_mid, g_qa, g_kva):
    n = N_TOK
    npt = N_P // TM
    return pl.pallas_call(
        _in_mid_kernel,
        grid=(n // TM,),
        in_specs=[
            pl.BlockSpec((TM, D_MODEL), lambda i: (jnp.minimum(i, npt - 1), 0)),
            pl.BlockSpec((TM, D_MODEL), lambda i: (jnp.maximum(i - npt, 0), 0)),
            pl.BlockSpec((1, D_MODEL), lambda i: (0, 0)),
            pl.BlockSpec((D_MODEL, MID_W), lambda i: (0, 0)),
            pl.BlockSpec((1, Q_LORA_RANK), lambda i: (0, 0)),
            pl.BlockSpec((1, KV_LORA_RANK), lambda i: (0, 0)),
        ],
        out_specs=[
            pl.BlockSpec((TM, D_MODEL), lambda i: (i, 0)),
            pl.BlockSpec((TM, Q_LORA_RANK), lambda i: (i, 0)),
            pl.BlockSpec((TM, KV_LORA_RANK), lambda i: (jnp.minimum(i, npt - 1), 0)),
            pl.BlockSpec((TM, KV_LORA_RANK), lambda i: (jnp.maximum(i - npt, 0), 0)),
            pl.BlockSpec((TM, LANES), lambda i: (i, 0)),
        ],
        out_shape=[
            jax.ShapeDtypeStruct((n, D_MODEL), BF16),
            jax.ShapeDtypeStruct((n, Q_LORA_RANK), BF16),
            jax.ShapeDtypeStruct((N_P, KV_LORA_RANK), F32),
            jax.ShapeDtypeStruct((N_S, KV_LORA_RANK), F32),
            jax.ShapeDtypeStruct((n, LANES), F32),
        ],
        compiler_params=_cparams(1),
        name="in_mid",
    )(xp, xs, g_mix, w_mid, g_qa, g_kva)


def _glu_kernel(h_ref, w1_ref, w2_ref, o_ref):
    h = h_ref[...]
    o_ref[...] = _dot(h, w1_ref[...]) * _sigmoid(_dot(h, w2_ref[...]))


def _in_glu(h, w_in_b):
    n = h.shape[0]
    tn = 1024
    nj = CONV_CHANNELS // tn
    return pl.pallas_call(
        _glu_kernel,
        grid=(n // TM, nj),
        in_specs=[
            pl.BlockSpec((TM, D_MODEL), lambda i, j: (i, 0)),
            pl.BlockSpec((D_MODEL, tn), lambda i, j: (0, j)),
            pl.BlockSpec((D_MODEL, tn), lambda i, j: (0, j + nj)),
        ],
        out_specs=pl.BlockSpec((TM, tn), lambda i, j: (i, j)),
        out_shape=jax.ShapeDtypeStruct((n, CONV_CHANNELS), F32),
        compiler_params=_cparams(2),
        name="in_glu",
    )(h, w_in_b, w_in_b)


CONV_TM = 256
_CONV_SUBS = CONV_TM // CONV_T
_CONV_SEQ_TILES = SEQ // CONV_TM
_CONV_PROMPT_TILES = N_P // CONV_TM
_CONV_LANES = 512
_SHIFT_ROWS = (HALO // SUBLANES - 1) * SUBLANES + CONV_T


def _conv_kernel(cur_ref, prev_ref, hist_ref, w_ref, bdw_ref, g_ref, b_ref, o_ref, win_ref, conv_ref, shift_ref):
    i = pl.program_id(0)
    is_sample = i >= _CONV_PROMPT_TILES
    opens = i % _CONV_SEQ_TILES == 0
    base = HALO - (CONV_WIDTH - 1)
    for j in range(_CONV_SUBS):
        r0 = j * CONV_T
        before = jnp.where(opens, 0.0, prev_ref[...]) if j == 0 else cur_ref[r0 - HALO:r0, :]
        win_ref[0:HALO, :] = jnp.where(is_sample, hist_ref[j], before)
        win_ref[HALO:HALO + CONV_T, :] = cur_ref[r0:r0 + CONV_T, :]
        for r in range(1, SUBLANES):
            shift_ref[r - 1] = win_ref[r:r + _SHIFT_ROWS, :]
        for c in range(0, CONV_CHANNELS, _CONV_LANES):
            acc = jnp.zeros((CONV_T, _CONV_LANES), F32)
            for k in range(CONV_WIDTH):
                q, r = divmod(base + k, SUBLANES)
                lanes = slice(c, c + _CONV_LANES)
                rows = slice(q * SUBLANES, q * SUBLANES + CONV_T)
                src = win_ref[rows, lanes] if r == 0 else shift_ref[r - 1, rows, lanes]
                acc = acc + w_ref[k:k + 1, lanes] * src
            conv_ref[:, c:c + _CONV_LANES] = acc + bdw_ref[:, c:c + _CONV_LANES]
        y = conv_ref[...]
        yc = y - jnp.mean(y, axis=-1, keepdims=True)
        var = jnp.mean(yc * yc, axis=-1, keepdims=True)
        z = yc * lax.rsqrt(var + EPS) * g_ref[...] + b_ref[...]
        o_ref[r0:r0 + CONV_T, :] = (z * _sigmoid(z)).astype(BF16)


def _conv_module(glu, hist_s, w_dw, b_dw, g_cn, b_cn):
    n = glu.shape[0]
    halo_per_tile = CONV_TM // HALO
    n_sample_tiles = N_S // CONV_TM
    const = lambda i: (0, 0)
    return pl.pallas_call(
        _conv_kernel,
        grid=(n // CONV_TM,),
        in_specs=[
            pl.BlockSpec((CONV_TM, CONV_CHANNELS), lambda i: (i, 0)),
            pl.BlockSpec((HALO, CONV_CHANNELS), lambda i: (jnp.maximum(i * halo_per_tile - 1, 0), 0)),
            pl.BlockSpec((_CONV_SUBS, HALO, CONV_CHANNELS),
                         lambda i: (jnp.clip(i - _CONV_PROMPT_TILES, 0, n_sample_tiles - 1), 0, 0)),
            pl.BlockSpec((CONV_WIDTH, CONV_CHANNELS), const),
            pl.BlockSpec((1, CONV_CHANNELS), const),
            pl.BlockSpec((1, CONV_CHANNELS), const),
            pl.BlockSpec((1, CONV_CHANNELS), const),
        ],
        out_specs=pl.BlockSpec((CONV_TM, CONV_CHANNELS), lambda i: (i, 0)),
        out_shape=jax.ShapeDtypeStruct((n, CONV_CHANNELS), BF16),
        scratch_shapes=[pltpu.VMEM((HALO + CONV_T, CONV_CHANNELS), F32),
                        pltpu.VMEM((CONV_T, CONV_CHANNELS), F32),
                        pltpu.VMEM((SUBLANES - 1, _SHIFT_ROWS, CONV_CHANNELS), F32)],
        compiler_params=_cparams(1),
        name="conv_module",
    )(glu, glu, hist_s, w_dw, b_dw, g_cn, b_cn)


ATT_TM = 512
_TAB_PROMPT_TILES = N_P // ATT_TM
_TAB_SEQ_TILES = SEQ // ATT_TM


def _tab_idx_new(i):
    return jnp.where(i < _TAB_PROMPT_TILES, i % _TAB_SEQ_TILES, _TAB_SEQ_TILES)


def _rope_pair(u, c, s):
    return u * c + pltpu.roll(u, LANES // 2, 1) * s


_Q_SCALE = math.log2(math.e) / math.sqrt(QK_DIM)


def _q_heads_kernel(ql_ref, w_ref, g_ref, c_ref, s_ref, o_ref):
    ql = ql_ref[...]
    g = g_ref[...]
    c = c_ref[...]
    s = s_ref[...]
    for h in range(N_HEADS):
        qf = _dot(ql, w_ref[:, h * HEAD_PAD:(h + 1) * HEAD_PAD])
        ssq = jnp.sum(qf * qf, axis=-1, keepdims=True)
        qn = qf * (lax.rsqrt(ssq * (1.0 / QK_DIM) + EPS) * _Q_SCALE) * g
        o_ref[h, :, :NOPE_DIM] = qn[:, :NOPE_DIM].astype(BF16)
        o_ref[h, :, NOPE_DIM:] = _rope_pair(qn[:, NOPE_DIM:], c, s).astype(BF16)


def _q_heads(q_lat, w_q, g_q, cos_t, sin_t):
    n = q_lat.shape[0]
    return pl.pallas_call(
        _q_heads_kernel,
        grid=(n // ATT_TM,),
        in_specs=[
            pl.BlockSpec((ATT_TM, Q_LORA_RANK), lambda i: (i, 0)),
            pl.BlockSpec((Q_LORA_RANK, N_HEADS * HEAD_PAD), lambda i: (0, 0)),
            pl.BlockSpec((1, HEAD_PAD), lambda i: (0, 0)),
            pl.BlockSpec((ATT_TM, LANES), lambda i: (_tab_idx_new(i), 0)),
            pl.BlockSpec((ATT_TM, LANES), lambda i: (_tab_idx_new(i), 0)),
        ],
        out_specs=pl.BlockSpec((N_HEADS, ATT_TM, HEAD_PAD), lambda i: (0, i, 0)),
        out_shape=jax.ShapeDtypeStruct((N_HEADS, n, HEAD_PAD), BF16),
        compiler_params=_cparams(1),
        name="q_heads",
    )(q_lat, w_q, g_q, cos_t, sin_t)


def _kv_heads_kernel(kvp_ref, kvs_ref, kr_ref, w_ref, gr_ref, c_ref, s_ref, k_ref, v_ref):
    kv = _stacked_rows(pl.program_id(0), N_P // ATT_TM, kvp_ref, kvs_ref).astype(BF16)
    u = kr_ref[...]
    ssq_r = jnp.sum(u * u, axis=-1, keepdims=True)
    krot = _rope_pair(u * gr_ref[...], c_ref[...], s_ref[...])
    for h in range(N_HEADS):
        z = _dot(kv, w_ref[:, h * HEAD_PAD:(h + 1) * HEAD_PAD])
        kn = z[:, :NOPE_DIM]
        ssq = jnp.sum(kn * kn, axis=-1, keepdims=True) + ssq_r
        scale = lax.rsqrt(ssq * (1.0 / QK_DIM) + EPS)
        k_ref[h, :, :NOPE_DIM] = (kn * scale).astype(BF16)
        k_ref[h, :, NOPE_DIM:] = (krot * scale).astype(BF16)
        v_ref[h] = z[:, NOPE_DIM:].astype(BF16)


def _kv_heads(kv_p, kv_s, kr_pad, w_kv, g_kn_rope, cos_t, sin_t, tab_idx, name):
    n = N_TOK
    npt = N_P // ATT_TM
    return pl.pallas_call(
        _kv_heads_kernel,
        grid=(n // ATT_TM,),
        in_specs=[
            pl.BlockSpec((ATT_TM, KV_LORA_RANK), lambda i: (jnp.minimum(i, npt - 1), 0)),
            pl.BlockSpec((ATT_TM, KV_LORA_RANK), lambda i: (jnp.maximum(i - npt, 0), 0)),
            pl.BlockSpec((ATT_TM, LANES), lambda i: (i, 0)),
            pl.BlockSpec((KV_LORA_RANK, N_HEADS * HEAD_PAD), lambda i: (0, 0)),
            pl.BlockSpec((1, LANES), lambda i: (0, 0)),
            pl.BlockSpec((ATT_TM, LANES), lambda i: (tab_idx(i), 0)),
            pl.BlockSpec((ATT_TM, LANES), lambda i: (tab_idx(i), 0)),
        ],
        out_specs=[
            pl.BlockSpec((N_HEADS, ATT_TM, HEAD_PAD), lambda i: (0, i, 0)),
            pl.BlockSpec((N_HEADS, ATT_TM, V_DIM), lambda i: (0, i, 0)),
        ],
        out_shape=[
            jax.ShapeDtypeStruct((N_HEADS, n, HEAD_PAD), BF16),
            jax.ShapeDtypeStruct((N_HEADS, n, V_DIM), BF16),
        ],
        compiler_params=_cparams(1),
        name=name,
    )(kv_p, kv_s, kr_pad, w_kv, g_kn_rope, cos_t, sin_t)


_TQ = 512
_TKB = 512
_HB = 4
_HBP = 4


def _flash_prompt_kernel(q_ref, k_ref, v_ref, o_ref, m_ref, l_ref, acc_ref):
    qi = pl.program_id(2)
    m_ref[...] = jnp.full(m_ref.shape, NEG_INF, F32)
    l_ref[...] = jnp.zeros(l_ref.shape, F32)
    acc_ref[...] = jnp.zeros(acc_ref.shape, F32)
    nlb = _TKB // LANES

    def step(ki, masked):
        start = pl.multiple_of(ki * _TKB, _TKB)
        scores = [lax.dot_general(q_ref[hh], k_ref[hh, pl.ds(start, _TKB), :], (((1,), (1,)), ((), ())),
                                  preferred_element_type=F32) for hh in range(_HBP)]
        probs = []
        for hh in range(_HBP):
            s = scores[hh]
            if masked:
                rc = lax.broadcasted_iota(I32, (_TQ, _TKB), 0) // CHUNK
                cc = lax.broadcasted_iota(I32, (_TQ, _TKB), 1) // CHUNK
                s = jnp.where(cc <= rc, s, NEG_INF)
            sb = [s[:, c * LANES:(c + 1) * LANES] for c in range(nlb)]
            bm = sb[0]
            for c in range(1, nlb):
                bm = jnp.maximum(bm, sb[c])
            m_prev = m_ref[hh]
            m_new = jnp.maximum(m_prev, jnp.max(bm, axis=-1, keepdims=True))
            alpha = jnp.exp2(m_prev - m_new)
            ps = [jnp.exp2(x - m_new) for x in sb]
            psum = ps[0]
            for c in range(1, nlb):
                psum = psum + ps[c]
            l_ref[hh] = alpha * l_ref[hh] + psum
            m_ref[hh] = m_new
            probs.append((alpha, jnp.concatenate(ps, axis=1).astype(BF16)))
        for hh in range(_HBP):
            alpha, p = probs[hh]
            acc_ref[hh] = alpha * acc_ref[hh] + _dot(p, v_ref[hh, pl.ds(start, _TKB), :])

    def body(kp, carry):
        step(2 * kp, False)
        step(2 * kp + 1, False)
        return carry

    lax.fori_loop(0, qi // 2, body, 0)

    @pl.when(qi % 2 == 1)
    def _():
        step(qi - 1, False)

    step(qi, True)
    for hh in range(_HBP):
        l = jnp.sum(l_ref[hh], axis=-1, keepdims=True)
        o_ref[:, hh * V_DIM:(hh + 1) * V_DIM] = (acc_ref[hh] / l).astype(BF16)


def _flash_prompt(q, k, v):
    nq = SEQ // _TQ
    return pl.pallas_call(
        _flash_prompt_kernel,
        grid=(BATCH, N_HEADS // _HBP, nq),
        in_specs=[
            pl.BlockSpec((_HBP, _TQ, HEAD_PAD), lambda b, h, i: (h, b * nq + i, 0)),
            pl.BlockSpec((_HBP, SEQ, HEAD_PAD), lambda b, h, i: (h, b, 0)),
            pl.BlockSpec((_HBP, SEQ, V_DIM), lambda b, h, i: (h, b, 0)),
        ],
        out_specs=pl.BlockSpec((_TQ, _HBP * V_DIM), lambda b, h, i: (b * nq + i, h)),
        out_shape=jax.ShapeDtypeStruct((N_TOK, N_HEADS * V_DIM), BF16),
        scratch_shapes=[pltpu.VMEM((_HBP, _TQ, LANES), F32), pltpu.VMEM((_HBP, _TQ, LANES), F32),
                        pltpu.VMEM((_HBP, _TQ, V_DIM), F32)],
        compiler_params=_cparams(3),
        name="flash_prompt",
    )(q, k, v)


_KC_ROWS = 512


def _flash_sample_kernel(prev_ref, q_ref, kv_ref, kr_ref, w_ref, gr_ref, c_ref, s_ref, kn_ref, vn_ref,
                         o_ref, kvb_ref, krot_ref, ssqr_ref, k_ref, v_ref):
    del prev_ref

    @pl.when(pl.program_id(1) == 0)
    def _():
        kvb_ref[...] = kv_ref[...].astype(BF16)
        u = kr_ref[...]
        ssqr_ref[...] = jnp.broadcast_to(jnp.sum(u * u, axis=-1, keepdims=True), ssqr_ref.shape)
        krot_ref[...] = _rope_pair(u * gr_ref[...], c_ref[...], s_ref[...])

    nt = (((1,), (1,)), ((), ()))
    for hh in range(_HB):
        w = w_ref[:, hh * HEAD_PAD:(hh + 1) * HEAD_PAD]
        for r in range(0, PAST_LEN, _KC_ROWS):
            rows = slice(r, r + _KC_ROWS)
            z = _dot(kvb_ref[rows, :], w)
            kn = z[:, :NOPE_DIM]
            ssq = jnp.sum(kn * kn, axis=-1, keepdims=True) + ssqr_ref[rows, :]
            scale = lax.rsqrt(ssq * (1.0 / QK_DIM) + EPS)
            k_ref[rows, :NOPE_DIM] = (kn * scale).astype(BF16)
            k_ref[rows, NOPE_DIM:] = (krot_ref[rows, :] * scale).astype(BF16)
            v_ref[rows, :] = z[:, NOPE_DIM:].astype(BF16)
        q = q_ref[hh]
        s1 = lax.dot_general(q, k_ref[...], nt, preferred_element_type=F32)
        s2 = lax.dot_general(q, kn_ref[hh], nt, preferred_element_type=F32)
        m = jnp.maximum(jnp.max(s1, axis=-1, keepdims=True), jnp.max(s2, axis=-1, keepdims=True))
        p1 = jnp.exp2(s1 - m)
        p2 = jnp.exp2(s2 - m)
        l = jnp.sum(p1, axis=-1, keepdims=True) + jnp.sum(p2, axis=-1, keepdims=True)
        o = _dot(p1.astype(BF16), v_ref[...]) + _dot(p2.astype(BF16), vn_ref[hh])
        o_ref[:, hh * V_DIM:(hh + 1) * V_DIM] = (o / l).astype(BF16)


def _flash_sample(attn, q, cache_kv, cache_kr_pad, w_kv, g_kn_rope, cos_t, sin_t, k_new, v_new):
    assert (PAST_LEN + DEC_SEQ - 1) // CHUNK <= PAST_LEN // CHUNK
    blk0 = N_P // DEC_SEQ
    new = lambda b, h: (h, blk0 + b, 0)
    const = lambda b, h: (0, 0)
    once = pl.Buffered(1)
    return pl.pallas_call(
        _flash_sample_kernel,
        grid=(DEC_BATCH, N_HEADS // _HB),
        in_specs=[
            pl.BlockSpec(memory_space=pl.ANY),
            pl.BlockSpec((_HB, DEC_SEQ, HEAD_PAD), new),
            pl.BlockSpec((PAST_LEN, KV_LORA_RANK), lambda b, h: (b, 0)),
            pl.BlockSpec((PAST_LEN, LANES), lambda b, h: (b, 0)),
            pl.BlockSpec((KV_LORA_RANK, _HB * HEAD_PAD), lambda b, h: (0, h)),
            pl.BlockSpec((1, LANES), const),
            pl.BlockSpec((PAST_LEN, LANES), const, pipeline_mode=once),
            pl.BlockSpec((PAST_LEN, LANES), const, pipeline_mode=once),
            pl.BlockSpec((_HB, DEC_SEQ, HEAD_PAD), new),
            pl.BlockSpec((_HB, DEC_SEQ, V_DIM), new),
        ],
        out_specs=pl.BlockSpec((DEC_SEQ, _HB * V_DIM), lambda b, h: (blk0 + b, h)),
        out_shape=jax.ShapeDtypeStruct((N_TOK, N_HEADS * V_DIM), BF16),
        scratch_shapes=[pltpu.VMEM((PAST_LEN, KV_LORA_RANK), BF16),
                        pltpu.VMEM((PAST_LEN, LANES), F32),
                        pltpu.VMEM((PAST_LEN, LANES), F32),
                        pltpu.VMEM((PAST_LEN, HEAD_PAD), BF16),
                        pltpu.VMEM((PAST_LEN, V_DIM), BF16)],
        input_output_aliases={0: 0},
        compiler_params=_cparams(2),
        name="flash_sample",
    )(attn, q, cache_kv, cache_kr_pad, w_kv, g_kn_rope, cos_t, sin_t, k_new, v_new)


def _merge_kernel(h_ref, c_ref, a_ref, wga_ref, wgb_ref, bga_ref, bgb_ref, wc_ref, wo_ref, o_ref):
    h = h_ref[...]
    ga = _sigmoid(_dot(h, wga_ref[...]) + bga_ref[...])
    gb = _sigmoid(_dot(h, wgb_ref[...]) + bgb_ref[...])
    mix = ga * _dot(c_ref[...], wc_ref[...]) + gb * _dot(a_ref[...], wo_ref[...])
    o_ref[...] = mix.astype(BF16)


def _merge(h, c_act, attn, w_gate, b_gate, w_conv_out, w_o):
    n = h.shape[0]
    tn = 512
    nj = D_MODEL // tn
    row = lambda i, j: (i, 0)
    return pl.pallas_call(
        _merge_kernel,
        grid=(n // TM, nj),
        in_specs=[
            pl.BlockSpec((TM, D_MODEL), row),
            pl.BlockSpec((TM, CONV_CHANNELS), row),
            pl.BlockSpec((TM, N_HEADS * V_DIM), row),
            pl.BlockSpec((D_MODEL, tn), lambda i, j: (0, j)),
            pl.BlockSpec((D_MODEL, tn), lambda i, j: (0, j + nj)),
            pl.BlockSpec((1, tn), lambda i, j: (0, j)),
            pl.BlockSpec((1, tn), lambda i, j: (0, j + nj)),
            pl.BlockSpec((CONV_CHANNELS, tn), lambda i, j: (0, j)),
            pl.BlockSpec((N_HEADS * V_DIM, tn), lambda i, j: (0, j)),
        ],
        out_specs=pl.BlockSpec((TM, tn), lambda i, j: (i, j)),
        out_shape=jax.ShapeDtypeStruct((n, D_MODEL), BF16),
        compiler_params=_cparams(2),
        name="merge",
    )(h, c_act, attn, w_gate, w_gate, b_gate, b_gate, w_conv_out, w_o)


def _split_bf16(x):
    hi = x.astype(BF16)
    lo = (x - hi.astype(F32)).astype(BF16)
    return hi, lo


_HALF = D_MODEL // 2


def _pack_bf16_pair(a, b):
    ua = lax.bitcast_convert_type(a.astype(BF16).astype(F32), U32)
    ub = lax.bitcast_convert_type(b.astype(BF16).astype(F32), U32)
    return lax.bitcast_convert_type(ua | (ub >> 16), F32)


def _unpack_bf16_pair(w):
    w = lax.bitcast_convert_type(w, U32)
    a = lax.bitcast_convert_type(w & jnp.uint32(0xFFFF0000), F32).astype(BF16)
    b = lax.bitcast_convert_type(w << 16, F32).astype(BF16)
    return a, b


def _out_router_kernel(n_prompt_tiles, mix_ref, xp_ref, xs_ref, w_ref, g_ref, wrh_ref, wrl_ref, br_ref,
                       x1_ref, hm_ref, idx_ref, gate_ref):
    x = _stacked_rows(pl.program_id(0), n_prompt_tiles, xp_ref, xs_ref)
    x1 = x + _dot(mix_ref[...], w_ref[...])
    x1_ref[...] = x1
    hn = x1 * lax.rsqrt(jnp.mean(x1 * x1, axis=-1, keepdims=True) + EPS) * g_ref[...]
    hm_ref[0] = _pack_bf16_pair(hn[:, :_HALF], hn[:, _HALF:])
    hm_ref[1] = jnp.zeros(hm_ref.shape[1:], F32)
    hh, hl = _split_bf16(hn)
    logits = _dot(hh, wrh_ref[...]) + (_dot(hh, wrl_ref[...]) + _dot(hl, wrh_ref[...])) + br_ref[...]
    lane = lax.broadcasted_iota(I32, logits.shape, 1).astype(F32)
    vals = []
    idx_out = jnp.zeros(logits.shape, F32)
    for k in range(TOP_K):
        m = jnp.max(logits, axis=-1, keepdims=True)
        sel = jnp.min(jnp.where(logits == m, lane, 1e9), axis=-1, keepdims=True)
        vals.append(m)
        idx_out = jnp.where(lane == float(k), sel, idx_out)
        logits = jnp.where(lane == sel, -jnp.inf, logits)
    exps = [jnp.exp(v - vals[0]) for v in vals]
    denom = exps[0] + exps[1] + exps[2] + exps[3]
    gate_out = jnp.zeros(idx_out.shape, F32)
    for k in range(TOP_K):
        gate_out = jnp.where(lane == float(k), exps[k] / denom, gate_out)
    idx_ref[...] = idx_out.astype(I32)
    gate_ref[...] = gate_out


def _out_router(mix, xp, xs, w_out, g_ffn, wr_hi, wr_lo, b_r):
    n = N_TOK
    tm = TM
    n_tiles = n // tm
    npt = N_P // tm
    const = lambda i: (0, 0)
    row = lambda i: (i, 0)
    once = pl.Buffered(1)
    return pl.pallas_call(
        functools.partial(_out_router_kernel, npt),
        grid=(n_tiles,),
        in_specs=[
            pl.BlockSpec((tm, D_MODEL), row),
            pl.BlockSpec((tm, D_MODEL), lambda i: (jnp.minimum(i, npt - 1), 0)),
            pl.BlockSpec((tm, D_MODEL), lambda i: (jnp.clip(i - npt, 0, N_S // tm - 1), 0)),
            pl.BlockSpec((D_MODEL, D_MODEL), const, pipeline_mode=once),
            pl.BlockSpec((1, D_MODEL), const),
            pl.BlockSpec((D_MODEL, LANES), const, pipeline_mode=once),
            pl.BlockSpec((D_MODEL, LANES), const, pipeline_mode=once),
            pl.BlockSpec((1, LANES), const),
        ],
        out_specs=[
            pl.BlockSpec((tm, D_MODEL), row),
            pl.BlockSpec((2, tm, _HALF), lambda i: (0, i, 0)),
            pl.BlockSpec((tm, LANES), row),
            pl.BlockSpec((tm, LANES), row),
        ],
        out_shape=[
            jax.ShapeDtypeStruct((n, D_MODEL), F32),
            jax.ShapeDtypeStruct((2, n, _HALF), F32),
            jax.ShapeDtypeStruct((n, LANES), I32),
            jax.ShapeDtypeStruct((n, LANES), F32),
        ],
        compiler_params=_cparams(1),
        name="out_router",
    )(mix, xp, xs, w_out, g_ffn, wr_hi, wr_lo, b_r)


_F_VALID, _F_FIRST, _F_NEXT, _F_GROUP0 = 1, 2, 4, 8


_P_E, _P_W, _P_N, _P_B, _P_BI, _P_NE, _P_NW, _P_FL, _P_SUBS = range(9)
MOE_SUB = 128


def _stream_weights(t, plan_ref, copies, cast):
    flags = plan_ref[_P_FL, t]

    @pl.when((flags & _F_FIRST) != 0)
    def _():
        cur = copies(plan_ref[_P_E, t], plan_ref[_P_W, t])

        @pl.when((flags & _F_GROUP0) != 0)
        def _():
            for c in cur:
                c.start()

        for c in cur:
            c.wait()
        cast()

        @pl.when((flags & _F_NEXT) != 0)
        def _():
            for c in copies(plan_ref[_P_NE, t], plan_ref[_P_NW, t]):
                c.start()


def _for_used_rows(valid, subs, rows_body):
    for n_sub in range(1, MOE_BLK // MOE_SUB + 1):
        @pl.when(jnp.logical_and(valid, subs == n_sub))
        def _(m=n_sub * MOE_SUB):
            rows_body(m)


def _moe_up_kernel(plan_ref, prev_ref, x_ref, w_hbm, bg_ref, bu_ref, o_ref, wbuf_ref, wgb_ref, wub_ref, sem_ref):
    del prev_ref
    t = pl.program_id(0)

    def copies(e, w):
        col = pl.multiple_of(w * _UP_TN, _UP_TN)
        return (pltpu.make_async_copy(w_hbm.at[e, :, pl.ds(col, _UP_TN)], wbuf_ref.at[0], sem_ref.at[0]),
                pltpu.make_async_copy(w_hbm.at[e, :, pl.ds(col + D_FF, _UP_TN)], wbuf_ref.at[1], sem_ref.at[1]))

    def cast():
        wgb_ref[...] = wbuf_ref[0].astype(BF16)
        wub_ref[...] = wbuf_ref[1].astype(BF16)

    _stream_weights(t, plan_ref, copies, cast)
    valid = (plan_ref[_P_FL, t] & _F_VALID) != 0

    def rows_body(m):
        xa, xb = _unpack_bf16_pair(x_ref[:m, :])
        g = _dot(xa, wgb_ref[:_HALF, :]) + _dot(xb, wgb_ref[_HALF:, :]) + bg_ref[0]
        u = _dot(xa, wub_ref[:_HALF, :]) + _dot(xb, wub_ref[_HALF:, :]) + bu_ref[0]
        g = jnp.minimum(g, SWIGLU_LIMIT)
        u = jnp.clip(u, -SWIGLU_LIMIT, SWIGLU_LIMIT)
        o_ref[:m, :] = ((u + 1.0) * (g * _sigmoid(SWIGLU_ALPHA * g))).astype(BF16)
        if m < MOE_BLK:
            o_ref[m:, :] = jnp.zeros((MOE_BLK - m, o_ref.shape[1]), BF16)

    _for_used_rows(valid, plan_ref[_P_SUBS, t], rows_body)

    @pl.when(jnp.logical_not(valid))
    def _():
        o_ref[...] = jnp.zeros(o_ref.shape, BF16)


_UP_TN = 1024
_UP_TILES = D_FF // _UP_TN
_DN_TN = 2048
_DN_TILES = D_MODEL // _DN_TN
MOE_CHUNKS = 4
_CHUNK_BLKS = MOE_MAX_BLKS // MOE_CHUNKS


def _moe_up(plan, act_prev, xs, w_gu, b_gu, chunk, early=()):
    steps = plan.shape[1]
    n_early = len(early)

    def body(*refs):
        return _moe_up_kernel(*refs[:6], *refs[6 + n_early:])

    blk0 = chunk * _CHUNK_BLKS
    bspec = lambda off: pl.BlockSpec((1, 1, _UP_TN), lambda t, p: (p[_P_E, t], 0, p[_P_W, t] + off))
    aliases = {} if act_prev is None else {1: 0}
    prev = jnp.zeros((SUBLANES, LANES), BF16) if act_prev is None else act_prev
    return pl.pallas_call(
        body,
        grid_spec=pltpu.PrefetchScalarGridSpec(
            num_scalar_prefetch=1,
            grid=(steps,),
            in_specs=[
                pl.BlockSpec(memory_space=pl.ANY),
                pl.BlockSpec((MOE_BLK, _HALF), lambda t, p: (p[_P_BI, t], 0)),
                pl.BlockSpec(memory_space=pl.ANY),
                bspec(0), bspec(_UP_TILES),
            ] + [pl.BlockSpec(memory_space=pl.ANY)] * n_early,
            out_specs=pl.BlockSpec((MOE_BLK, _UP_TN),
                                   lambda t, p: (blk0 + p[_P_B, t], p[_P_N, t])),
            scratch_shapes=[pltpu.VMEM((2, D_MODEL, _UP_TN), F32),
                            pltpu.VMEM((D_MODEL, _UP_TN), BF16), pltpu.VMEM((D_MODEL, _UP_TN), BF16),
                            pltpu.SemaphoreType.DMA((2,))],
        ),
        out_shape=jax.ShapeDtypeStruct((MOE_ROWS, D_FF), BF16),
        input_output_aliases=aliases,
        compiler_params=_cparams(1),
        name=f"moe_up_{chunk}",
    )(plan, prev, xs, w_gu, b_gu, b_gu, *early)


_DN_HALF = _DN_TN // 2


def _moe_down_kernel(plan_ref, a_ref, w_hbm, b_ref, o_ref, wbuf_ref, wb_ref, sem_ref):
    t = pl.program_id(0)

    def copies(e, w):
        col = pl.multiple_of(w * _DN_TN, _DN_TN)
        return (pltpu.make_async_copy(w_hbm.at[e, :, pl.ds(col, _DN_TN)], wbuf_ref, sem_ref.at[0]),)

    def cast():
        wb_ref[...] = wbuf_ref[...].astype(BF16)

    _stream_weights(t, plan_ref, copies, cast)
    valid = (plan_ref[_P_FL, t] & _F_VALID) != 0

    def rows_body(m):
        y = _dot(a_ref[:m, :], wb_ref[...]) + b_ref[0]
        o_ref[:m, :] = _pack_bf16_pair(y[:, :_DN_HALF], y[:, _DN_HALF:])
        if m < MOE_BLK:
            o_ref[m:, :] = jnp.zeros((MOE_BLK - m, o_ref.shape[1]), F32)

    _for_used_rows(valid, plan_ref[_P_SUBS, t], rows_body)

    @pl.when(jnp.logical_not(valid))
    def _():
        o_ref[...] = jnp.zeros(o_ref.shape, F32)


def _moe_down(plan, act, w_dn, b_dn):
    steps = plan.shape[1]
    return pl.pallas_call(
        _moe_down_kernel,
        grid_spec=pltpu.PrefetchScalarGridSpec(
            num_scalar_prefetch=1,
            grid=(steps,),
            in_specs=[
                pl.BlockSpec((MOE_BLK, D_FF), lambda t, p: (p[_P_BI, t], 0)),
                pl.BlockSpec(memory_space=pl.ANY),
                pl.BlockSpec((1, 1, _DN_TN), lambda t, p: (p[_P_E, t], 0, p[_P_W, t])),
            ],
            out_specs=pl.BlockSpec((MOE_BLK, _DN_HALF), lambda t, p: (p[_P_B, t], p[_P_N, t])),
            scratch_shapes=[pltpu.VMEM((D_FF, _DN_TN), F32), pltpu.VMEM((D_FF, _DN_TN), BF16),
                            pltpu.SemaphoreType.DMA((1,))],
        ),
        out_shape=jax.ShapeDtypeStruct((MOE_ROWS, _HALF), F32),
        compiler_params=_cparams(1),
        name="moe_down",
    )(plan, act, w_dn, b_dn)


def _moe_dispatch(top_idx):
    n_asg = N_TOK * TOP_K
    flat_e = top_idx.reshape(-1)
    onehot = (flat_e[:, None] == jnp.arange(N_EXPERTS, dtype=I32)[None, :]).astype(I32)
    csum = jnp.cumsum(onehot, axis=0)
    counts = csum[-1]
    rank = jnp.sum(csum * onehot, axis=1) - 1
    nblk = (counts + MOE_BLK - 1) // MOE_BLK
    blk_start = jnp.cumsum(nblk) - nblk
    dest = jnp.sum(onehot * blk_start[None, :], axis=1) * MOE_BLK + rank
    pad_src = jnp.arange(MOE_ROWS, dtype=I32) % N_TOK
    row_tok = pad_src.at[dest].set(jnp.arange(n_asg, dtype=I32) // TOP_K,
                                   mode="promise_in_bounds", unique_indices=True)
    return dest, row_tok, counts, nblk, blk_start


def _moe_steps(counts, nblk, blk_start, n_tiles, blk_lo, n_blks):
    t_max = n_tiles * n_blks
    lo = jnp.clip(blk_start, blk_lo, blk_lo + n_blks)
    hi = jnp.clip(blk_start + nblk, blk_lo, blk_lo + n_blks)
    nb_e = hi - lo
    per_e = nb_e * n_tiles
    s_end = jnp.cumsum(per_e)
    total = s_end[-1]
    t = jnp.arange(t_max, dtype=I32)
    tc = jnp.clip(t, 0, jnp.maximum(total - 1, 0))
    e = jnp.minimum(jnp.sum((s_end[None, :] <= tc[:, None]).astype(I32), axis=1), N_EXPERTS - 1)
    sel = (e[:, None] == jnp.arange(N_EXPERTS, dtype=I32)[None, :]).astype(I32)
    pick = lambda v: jnp.sum(sel * v[None, :], axis=1)
    local = tc - pick(s_end - per_e)
    nb = jnp.maximum(pick(nb_e), 1)
    w_tile = jnp.clip(local // nb, 0, n_tiles - 1)
    r = local % nb
    valid = t < total
    first = jnp.logical_and(valid, r == 0)
    fill = t - total
    blk = jnp.where(valid, pick(lo) - blk_lo + r, total // n_tiles + fill // n_tiles)
    rows_used = pick(counts) - (pick(lo) + r - pick(blk_start)) * MOE_BLK
    subs = jnp.clip((rows_used + MOE_SUB - 1) // MOE_SUB, 1, MOE_BLK // MOE_SUB)
    o_tile = jnp.where(valid, w_tile, fill % n_tiles)
    blk = jnp.clip(blk, 0, n_blks - 1)
    blk_in = jnp.where(valid, blk, jnp.maximum(total // n_tiles - 1, 0))
    ids = jnp.arange(N_EXPERTS, dtype=I32)
    owners = jnp.where(nb_e > 0, ids, N_EXPERTS)
    later = jnp.flip(lax.cummin(jnp.flip(owners)))
    next_owner = pick(jnp.concatenate([later[1:], jnp.full((1,), N_EXPERTS, I32)]))
    last_tile = w_tile == n_tiles - 1
    next_e = jnp.where(last_tile, next_owner, e)
    next_w = jnp.where(last_tile, 0, w_tile + 1)
    has_next = jnp.logical_and(first, next_e < N_EXPERTS)
    group = jnp.cumsum(first.astype(I32)) - 1
    flags = (valid * _F_VALID + first * _F_FIRST + has_next * _F_NEXT
             + jnp.logical_and(first, group == 0) * _F_GROUP0)
    rows = {_P_E: e, _P_W: w_tile, _P_N: o_tile, _P_B: blk, _P_BI: blk_in,
            _P_NE: jnp.minimum(next_e, N_EXPERTS - 1), _P_NW: next_w, _P_FL: flags, _P_SUBS: subs}
    return jnp.stack([rows[k].astype(I32) for k in range(len(rows))])


def _moe_plans(counts, nblk, blk_start, n_tiles, n_chunks, n_blks):
    los = jnp.arange(n_chunks, dtype=I32) * n_blks
    return jax.vmap(lambda lo: _moe_steps(counts, nblk, blk_start, n_tiles, lo, n_blks))(los)


_FIN_TM = 256
_FIN_TN = 512
FIN_CHUNKS = 4


def _unpack_expert_rows(words):
    u = lax.bitcast_convert_type(words, U32)
    hi = lax.bitcast_convert_type(u & jnp.uint32(0xFFFF0000), F32)
    lo = lax.bitcast_convert_type(u << 16, F32)
    parts = []
    for n in range(_DN_TILES):
        cols = slice(n * _DN_HALF, (n + 1) * _DN_HALF)
        parts += [hi[:, cols], lo[:, cols]]
    return jnp.concatenate(parts, axis=1)


def _final_kernel(prev_ref, x1_ref, y0_ref, y1_ref, y2_ref, y3_ref, gate_ref, g_ref, wg_ref, p_ref, wp_ref,
                  o_ref, x2_ref):
    del prev_ref
    gate = gate_ref[...]
    moe = (_unpack_expert_rows(y0_ref[0]) * gate[:, 0:1] + _unpack_expert_rows(y1_ref[0]) * gate[:, 1:2]
           + _unpack_expert_rows(y2_ref[0]) * gate[:, 2:3] + _unpack_expert_rows(y3_ref[0]) * gate[:, 3:4])
    x2 = x1_ref[...] + moe
    x2_ref[...] = x2
    hp = (x2 * lax.rsqrt(jnp.mean(x2 * x2, axis=-1, keepdims=True) + EPS) * g_ref[...]).astype(BF16)
    pb = p_ref[...].astype(BF16)
    for c in range(0, D_MODEL, _FIN_TN):
        cols = slice(c, c + _FIN_TN)
        emb = _dot(pb, wp_ref[:, cols])
        o_ref[:, cols] = x2_ref[:, cols] + _sigmoid(_dot(hp, wg_ref[:, cols])) * emb


def _final(out_prev, x1, y4, gate, g_ple, w_ple_gate, p, w_ple, tok0, out0, n, n_out, name):
    t0 = tok0 // _FIN_TM
    o0 = out0 // _FIN_TM
    pt0 = out0 // _FIN_TM
    const = lambda i: (0, 0)
    yspec = lambda k: pl.BlockSpec((1, _FIN_TM, _HALF), lambda i: (k, i, 0))
    once = pl.Buffered(1)
    aliases = {} if out_prev is None else {0: 0}
    prev = jnp.zeros((SUBLANES, LANES), F32) if out_prev is None else out_prev
    return pl.pallas_call(
        _final_kernel,
        grid=(n // _FIN_TM,),
        in_specs=[
            pl.BlockSpec(memory_space=pl.ANY),
            pl.BlockSpec((_FIN_TM, D_MODEL), lambda i: (t0 + i, 0)),
            yspec(0), yspec(1), yspec(2), yspec(3),
            pl.BlockSpec((_FIN_TM, LANES), lambda i: (t0 + i, 0)),
            pl.BlockSpec((1, D_MODEL), const),
            pl.BlockSpec((D_MODEL, D_MODEL), const, pipeline_mode=once),
            pl.BlockSpec((_FIN_TM, PLE_DIM), lambda i: (pt0 + i, 0)),
            pl.BlockSpec((PLE_DIM, D_MODEL), const, pipeline_mode=once),
        ],
        out_specs=pl.BlockSpec((_FIN_TM, D_MODEL), lambda i: (o0 + i, 0)),
        out_shape=jax.ShapeDtypeStruct((n_out, D_MODEL), F32),
        scratch_shapes=[pltpu.VMEM((_FIN_TM, D_MODEL), F32)],
        input_output_aliases=aliases,
        compiler_params=_cparams(1),
        name=name,
    )(prev, x1, y4, y4, y4, y4, gate, g_ple, w_ple_gate, p, w_ple)


def _rope_layout(x):
    half = ROPE_DIM // 2
    z = jnp.zeros(x.shape[:-1] + (half,), x.dtype)
    return jnp.concatenate([x[..., :half], z, x[..., half:], z], axis=-1)


def _rope_tables():
    half = ROPE_DIM // 2
    inv_freq = ROPE_THETA ** (-jnp.arange(half, dtype=F32) / half)
    pos = jnp.arange(PAST_LEN + DEC_SEQ, dtype=I32)
    ang = pos.astype(F32)[:, None] * inv_freq[None, :]
    cos, sin = jnp.cos(ang), jnp.sin(ang)
    z = jnp.zeros_like(cos)
    c = jnp.concatenate([cos, z, cos, z], axis=-1)
    s = jnp.concatenate([-sin, z, sin, z], axis=-1)
    rep = ATT_TM // DEC_SEQ
    return (jnp.concatenate([c[:SEQ], jnp.tile(c[PAST_LEN:], (rep, 1))], axis=0),
            jnp.concatenate([s[:SEQ], jnp.tile(s[PAST_LEN:], (rep, 1))], axis=0))


def _layer(xp, xs, p_prompt, p_sample, cache_kv, cache_kr, state_conv,
           g_mix, w_in, b_gate, w_dw, b_dw, g_cn, b_cn, w_conv_out,
           g_qa, g_kva, w_qb, w_kb, w_vb, g_qn, g_kn, w_o, w_out,
           g_ffn, w_router, b_router, w_gu, b_gu, w_dn, b_dn,
           g_ple, w_ple_gate, w_ple):
    assert SEQ == PAST_LEN
    row = lambda v: v.reshape(1, -1)
    w_in_b = w_in.astype(BF16)
    w_mid = jnp.concatenate([w_in_b[:, O_U:O_KV], _rope_layout(w_in_b[:, O_KV:O_KR])], axis=1)
    w_gate = w_in_b[:, O_KR:]

    h, q_lat, kv_p, kv_s, kr_pad = _in_mid(xp, xs, row(g_mix), w_mid, row(g_qa), row(g_kva))
    half = ROPE_DIM // 2
    kr_new = jnp.concatenate([kr_pad[:, :half], kr_pad[:, 2 * half:3 * half]], axis=1)
    glu = _in_glu(h, w_in_b)

    hist_s = jnp.pad(state_conv, ((0, 0), (HALO - (CONV_WIDTH - 1), 0), (0, 0)))
    c_act = _conv_module(glu, hist_s, w_dw, row(b_dw), row(g_cn), row(b_cn))

    cos_t, sin_t = _rope_tables()
    w_q = jnp.concatenate([w_qb[..., :NOPE_DIM], _rope_layout(w_qb[..., NOPE_DIM:])], axis=-1)
    w_q = w_q.reshape(Q_LORA_RANK, N_HEADS * HEAD_PAD).astype(BF16)
    g_q = jnp.concatenate([g_qn[:NOPE_DIM] * g_kn[:NOPE_DIM], _rope_layout(g_qn[NOPE_DIM:])]).reshape(1, HEAD_PAD)
    q = _q_heads(q_lat, w_q, g_q, cos_t, sin_t)

    w_kv = jnp.concatenate([w_kb, w_vb], axis=-1).reshape(KV_LORA_RANK, N_HEADS * HEAD_PAD).astype(BF16)
    g_kn_rope = _rope_layout(g_kn[NOPE_DIM:]).reshape(1, LANES)
    k_new, v_new = _kv_heads(kv_p, kv_s, kr_pad, w_kv, g_kn_rope, cos_t, sin_t, _tab_idx_new, "kv_heads_new")
    attn = _flash_prompt(q, k_new, v_new)
    attn = _flash_sample(attn, q, cache_kv.reshape(DEC_BATCH * PAST_LEN, KV_LORA_RANK),
                         _rope_layout(cache_kr).reshape(DEC_BATCH * PAST_LEN, LANES),
                         w_kv, g_kn_rope, cos_t, sin_t, k_new, v_new)

    mix = _merge(h, c_act, attn, w_gate, row(b_gate), w_conv_out.astype(BF16), w_o.astype(BF16))

    wr = jnp.pad(w_router, ((0, 0), (0, LANES - N_EXPERTS)))
    wr_hi, wr_lo = _split_bf16(wr)
    b_r = jnp.concatenate([b_router, jnp.full((LANES - N_EXPERTS,), -jnp.inf, F32)]).reshape(1, LANES)
    x1, hm, idx_pad, gate_pad = _out_router(mix, xp, xs, w_out.astype(BF16), row(g_ffn), wr_hi, wr_lo, b_r)
    hm = hm.reshape(2 * N_TOK, _HALF)

    top_idx = idx_pad[:, :TOP_K]
    dest, row_tok, counts, nblk, blk_start = _moe_dispatch(top_idx)
    b_gu3 = b_gu.reshape(N_EXPERTS, 1, 2 * D_FF)
    chunk_rows = _CHUNK_BLKS * MOE_BLK
    up_plans = _moe_plans(counts, nblk, blk_start, _UP_TILES, MOE_CHUNKS, _CHUNK_BLKS)
    down_plan = _moe_plans(counts, nblk, blk_start, _DN_TILES, 1, MOE_MAX_BLKS)[0]
    fin = (row(g_ple), w_ple_gate.astype(BF16))
    w_ple_b = w_ple.astype(BF16)
    tail = CONV_WIDTH - 1
    conv_p = jnp.stack([glu[(b + 1) * SEQ - tail:(b + 1) * SEQ] for b in range(BATCH)])
    conv_s = glu[N_P:].reshape(DEC_BATCH, DEC_SEQ, CONV_CHANNELS)[:, DEC_SEQ - tail:]
    early = (fin[1], w_ple_b, kr_new, conv_p, conv_s)
    act = None
    for c in range(MOE_CHUNKS):
        xs = hm.at[row_tok[c * chunk_rows:(c + 1) * chunk_rows]].get(mode="promise_in_bounds")
        act = _moe_up(up_plans[c], act, xs, w_gu, b_gu3, c, early if c == 0 else ())
    ys = _moe_down(down_plan, act, w_dn, b_dn.reshape(N_EXPERTS, 1, D_MODEL))

    dest_t = dest.reshape(N_TOK, TOP_K).T
    n_c = N_P // FIN_CHUNKS
    out_p = None
    for c in range(FIN_CHUNKS):
        y4 = ys.at[dest_t[:, c * n_c:(c + 1) * n_c]].get(mode="promise_in_bounds")
        out_p = _final(out_p, x1, y4, gate_pad, *fin, p_prompt, w_ple_b, c * n_c, c * n_c, n_c, N_P,
                       f"final_prompt_{c}")
    y4 = ys.at[dest_t[:, N_P:]].get(mode="promise_in_bounds")
    out_s = _final(None, x1, y4, gate_pad, *fin, p_sample, w_ple_b, N_P, 0, N_S, N_S, "final_sample")
    return out_p, out_s, kv_p, kv_s, kr_new, conv_p, conv_s


def kernel(x_prompt, x_sample, cache_kv_latent, cache_k_rope, state_conv, p_prompt, p_sample, g_mix, w_in, b_gate, w_dw, b_dw, g_cn, b_cn, w_conv_out, g_qa, g_kva, w_qb, w_kb, w_vb, g_qn, g_kn, w_o, w_out, g_ffn, w_router, b_router, w_gu, b_gu, w_dn, b_dn, g_ple, w_ple_gate, w_ple):
    assert g_mix.shape[0] == 1
    out_p, out_s, kv_p, kv_s, kr_new, conv_p, conv_s = _layer(
        x_prompt.reshape(N_P, D_MODEL), x_sample.reshape(N_S, D_MODEL),
        p_prompt[0].reshape(N_P, PLE_DIM), p_sample[0].reshape(N_S, PLE_DIM),
        cache_kv_latent[0], cache_k_rope[0], state_conv[0],
        g_mix[0], w_in[0], b_gate[0], w_dw[0], b_dw[0], g_cn[0], b_cn[0], w_conv_out[0],
        g_qa[0], g_kva[0], w_qb[0], w_kb[0], w_vb[0], g_qn[0], g_kn[0], w_o[0], w_out[0],
        g_ffn[0], w_router[0], b_router[0], w_gu[0], b_gu[0], w_dn[0], b_dn[0],
        g_ple[0], w_ple_gate[0], w_ple[0])
    return (out_p.reshape(BATCH, SEQ, D_MODEL),
            out_s.reshape(DEC_BATCH, DEC_SEQ, D_MODEL),
            kv_p.reshape(1, BATCH, SEQ, KV_LORA_RANK),
            kr_new[:N_P].reshape(1, BATCH, SEQ, ROPE_DIM),
            conv_p[None],
            kv_s.reshape(1, DEC_BATCH, DEC_SEQ, KV_LORA_RANK),
            kr_new[N_P:].reshape(1, DEC_BATCH, DEC_SEQ, ROPE_DIM),
            conv_s[None])
```

```python
import functools
import math

import jax
import jax.numpy as jnp
from jax import lax
from jax.experimental import pallas as pl
from jax.experimental.pallas import tpu as pltpu

F32 = jnp.float32
BF16 = jnp.bfloat16
I32 = jnp.int32
U32 = jnp.uint32

D_MODEL = 2048
BATCH = 2
SEQ = 4096
DEC_BATCH = 8
DEC_SEQ = 64
PAST_LEN = 4096
CHUNK = 64
CONV_CHANNELS = D_MODEL
CONV_WIDTH = 31
N_HEADS = 16
Q_LORA_RANK = 512
KV_LORA_RANK = 512
NOPE_DIM = 128
ROPE_DIM = 64
QK_DIM = NOPE_DIM + ROPE_DIM
V_DIM = 128
ROPE_THETA = 10000.0
N_EXPERTS = 32
TOP_K = 4
D_FF = D_MODEL
SWIGLU_ALPHA = 1.702
SWIGLU_LIMIT = 7.0
PLE_DIM = 256
EPS = 1e-6
NEG_INF = -1e30

N_P = BATCH * SEQ
N_S = DEC_BATCH * DEC_SEQ
N_TOK = N_P + N_S
O_U = 2 * CONV_CHANNELS
O_Q = O_U + Q_LORA_RANK
O_KV = O_Q + KV_LORA_RANK
O_KR = O_KV + ROPE_DIM
LANES = 128
SUBLANES = 8
MID_W = Q_LORA_RANK + KV_LORA_RANK + LANES
HEAD_PAD = NOPE_DIM + LANES

TM = 512
CONV_T = 64
HALO = 32
MOE_BLK = 512
MOE_MAX_BLKS = (N_TOK * TOP_K) // MOE_BLK + N_EXPERTS
MOE_ROWS = MOE_MAX_BLKS * MOE_BLK
VMEM_LIMIT = 48 * 1024 * 1024
assert 2 * ROPE_DIM == LANES and NOPE_DIM == LANES and V_DIM == LANES and SEQ == PAST_LEN


def _cparams(n_axes):
    return pltpu.CompilerParams(dimension_semantics=("arbitrary",) * n_axes,
                                vmem_limit_bytes=VMEM_LIMIT)


def _sigmoid(x):
    return 1.0 / (1.0 + jnp.exp(-x))


def _dot(a, b):
    return jnp.dot(a, b, preferred_element_type=F32)


def _stacked_rows(i, n_prompt_tiles, xp_ref, xs_ref):
    return jnp.where(i < n_prompt_tiles, xp_ref[...], xs_ref[...])


def _in_mid_kernel(xp_ref, xs_ref, g_ref, w_ref, gqa_ref, gkva_ref, h_ref, q_ref, kvp_ref, kvs_ref, kr_ref):
    i = pl.program_id(0)
    x = _stacked_rows(i, N_P // TM, xp_ref, xs_ref)
    h = x * lax.rsqrt(jnp.mean(x * x, axis=-1, keepdims=True) + EPS) * g_ref[...]
    hb = h.astype(BF16)
    h_ref[...] = hb
    z = _dot(hb, w_ref[...])
    ql = z[:, :Q_LORA_RANK]
    kvl = z[:, Q_LORA_RANK:Q_LORA_RANK + KV_LORA_RANK]
    qn = ql * lax.rsqrt(jnp.mean(ql * ql, axis=-1, keepdims=True) + EPS) * gqa_ref[...]
    q_ref[...] = qn.astype(BF16)
    kv = kvl * lax.rsqrt(jnp.mean(kvl * kvl, axis=-1, keepdims=True) + EPS) * gkva_ref[...]
    kr_ref[...] = z[:, Q_LORA_RANK + KV_LORA_RANK:]

    @pl.when(i < N_P // TM)
    def _():
        kvp_ref[...] = kv

    @pl.when(i >= N_P // TM)
    def _():
        kvs_ref[...] = kv


def _in_mid(xp, xs, g_mix, w_mid, g_qa, g_kva):
    n = N_TOK
    npt = N_P // TM
    return pl.pallas_call(
        _in_mid_kernel,
        grid=(n // TM,),
        in_specs=[
            pl.BlockSpec((TM, D_MODEL), lambda i: (jnp.minimum(i, npt - 1), 0)),
            pl.BlockSpec((TM, D_MODEL), lambda i: (jnp.maximum(i - npt, 0), 0)),
            pl.BlockSpec((1, D_MODEL), lambda i: (0, 0)),
            pl.BlockSpec((D_MODEL, MID_W), lambda i: (0, 0)),
            pl.BlockSpec((1, Q_LORA_RANK), lambda i: (0, 0)),
            pl.BlockSpec((1, KV_LORA_RANK), lambda i: (0, 0)),
        ],
        out_specs=[
            pl.BlockSpec((TM, D_MODEL), lambda i: (i, 0)),
            pl.BlockSpec((TM, Q_LORA_RANK), lambda i: (i, 0)),
            pl.BlockSpec((TM, KV_LORA_RANK), lambda i: (jnp.minimum(i, npt - 1), 0)),
            pl.BlockSpec((TM, KV_LORA_RANK), lambda i: (jnp.maximum(i - npt, 0), 0)),
            pl.BlockSpec((TM, LANES), lambda i: (i, 0)),
        ],
        out_shape=[
            jax.ShapeDtypeStruct((n, D_MODEL), BF16),
            jax.ShapeDtypeStruct((n, Q_LORA_RANK), BF16),
            jax.ShapeDtypeStruct((N_P, KV_LORA_RANK), F32),
            jax.ShapeDtypeStruct((N_S, KV_LORA_RANK), F32),
            jax.ShapeDtypeStruct((n, LANES), F32),
        ],
        compiler_params=_cparams(1),
        name="in_mid",
    )(xp, xs, g_mix, w_mid, g_qa, g_kva)


def _glu_kernel(h_ref, w1_ref, w2_ref, o_ref):
    h = h_ref[...]
    o_ref[...] = _dot(h, w1_ref[...]) * _sigmoid(_dot(h, w2_ref[...]))


def _in_glu(h, w_in_b):
    n = h.shape[0]
    tn = 1024
    nj = CONV_CHANNELS // tn
    return pl.pallas_call(
        _glu_kernel,
        grid=(n // TM, nj),
        in_specs=[
            pl.BlockSpec((TM, D_MODEL), lambda i, j: (i, 0)),
            pl.BlockSpec((D_MODEL, tn), lambda i, j: (0, j)),
            pl.BlockSpec((D_MODEL, tn), lambda i, j: (0, j + nj)),
        ],
        out_specs=pl.BlockSpec((TM, tn), lambda i, j: (i, j)),
        out_shape=jax.ShapeDtypeStruct((n, CONV_CHANNELS), F32),
        compiler_params=_cparams(2),
        name="in_glu",
    )(h, w_in_b, w_in_b)


CONV_TM = 256
_CONV_SUBS = CONV_TM // CONV_T
_CONV_SEQ_TILES = SEQ // CONV_TM
_CONV_PROMPT_TILES = N_P // CONV_TM
_CONV_LANES = 512
_SHIFT_ROWS = (HALO // SUBLANES - 1) * SUBLANES + CONV_T


def _conv_kernel(cur_ref, prev_ref, hist_ref, w_ref, bdw_ref, g_ref, b_ref, o_ref, win_ref, conv_ref, shift_ref):
    i = pl.program_id(0)
    is_sample = i >= _CONV_PROMPT_TILES
    opens = i % _CONV_SEQ_TILES == 0
    base = HALO - (CONV_WIDTH - 1)
    for j in range(_CONV_SUBS):
        r0 = j * CONV_T
        before = jnp.where(opens, 0.0, prev_ref[...]) if j == 0 else cur_ref[r0 - HALO:r0, :]
        win_ref[0:HALO, :] = jnp.where(is_sample, hist_ref[j], before)
        win_ref[HALO:HALO + CONV_T, :] = cur_ref[r0:r0 + CONV_T, :]
        for r in range(1, SUBLANES):
            shift_ref[r - 1] = win_ref[r:r + _SHIFT_ROWS, :]
        for c in range(0, CONV_CHANNELS, _CONV_LANES):
            acc = jnp.zeros((CONV_T, _CONV_LANES), F32)
            for k in range(CONV_WIDTH):
                q, r = divmod(base + k, SUBLANES)
                lanes = slice(c, c + _CONV_LANES)
                rows = slice(q * SUBLANES, q * SUBLANES + CONV_T)
                src = win_ref[rows, lanes] if r == 0 else shift_ref[r - 1, rows, lanes]
                acc = acc + w_ref[k:k + 1, lanes] * src
            conv_ref[:, c:c + _CONV_LANES] = acc + bdw_ref[:, c:c + _CONV_LANES]
        y = conv_ref[...]
        yc = y - jnp.mean(y, axis=-1, keepdims=True)
        var = jnp.mean(yc * yc, axis=-1, keepdims=True)
        z = yc * lax.rsqrt(var + EPS) * g_ref[...] + b_ref[...]
        o_ref[r0:r0 + CONV_T, :] = (z * _sigmoid(z)).astype(BF16)


def _conv_module(glu, hist_s, w_dw, b_dw, g_cn, b_cn):
    n = glu.shape[0]
    halo_per_tile = CONV_TM // HALO
    n_sample_tiles = N_S // CONV_TM
    const = lambda i: (0, 0)
    return pl.pallas_call(
        _conv_kernel,
        grid=(n // CONV_TM,),
        in_specs=[
            pl.BlockSpec((CONV_TM, CONV_CHANNELS), lambda i: (i, 0)),
            pl.BlockSpec((HALO, CONV_CHANNELS), lambda i: (jnp.maximum(i * halo_per_tile - 1, 0), 0)),
            pl.BlockSpec((_CONV_SUBS, HALO, CONV_CHANNELS),
                         lambda i: (jnp.clip(i - _CONV_PROMPT_TILES, 0, n_sample_tiles - 1), 0, 0)),
            pl.BlockSpec((CONV_WIDTH, CONV_CHANNELS), const),
            pl.BlockSpec((1, CONV_CHANNELS), const),
            pl.BlockSpec((1, CONV_CHANNELS), const),
            pl.BlockSpec((1, CONV_CHANNELS), const),
        ],
        out_specs=pl.BlockSpec((CONV_TM, CONV_CHANNELS), lambda i: (i, 0)),
        out_shape=jax.ShapeDtypeStruct((n, CONV_CHANNELS), BF16),
        scratch_shapes=[pltpu.VMEM((HALO + CONV_T, CONV_CHANNELS), F32),
                        pltpu.VMEM((CONV_T, CONV_CHANNELS), F32),
                        pltpu.VMEM((SUBLANES - 1, _SHIFT_ROWS, CONV_CHANNELS), F32)],
        compiler_params=_cparams(1),
        name="conv_module",
    )(glu, glu, hist_s, w_dw, b_dw, g_cn, b_cn)


ATT_TM = 512
_TAB_PROMPT_TILES = N_P // ATT_TM
_TAB_SEQ_TILES = SEQ // ATT_TM


def _tab_idx_new(i):
    return jnp.where(i < _TAB_PROMPT_TILES, i % _TAB_SEQ_TILES, _TAB_SEQ_TILES)


def _rope_pair(u, c, s):
    return u * c + pltpu.roll(u, LANES // 2, 1) * s


_Q_SCALE = math.log2(math.e) / math.sqrt(QK_DIM)


def _q_heads_kernel(ql_ref, w_ref, g_ref, c_ref, s_ref, o_ref):
    ql = ql_ref[...]
    g = g_ref[...]
    c = c_ref[...]
    s = s_ref[...]
    for h in range(N_HEADS):
        qf = _dot(ql, w_ref[:, h * HEAD_PAD:(h + 1) * HEAD_PAD])
        ssq = jnp.sum(qf * qf, axis=-1, keepdims=True)
        qn = qf * (lax.rsqrt(ssq * (1.0 / QK_DIM) + EPS) * _Q_SCALE) * g
        o_ref[h, :, :NOPE_DIM] = qn[:, :NOPE_DIM].astype(BF16)
        o_ref[h, :, NOPE_DIM:] = _rope_pair(qn[:, NOPE_DIM:], c, s).astype(BF16)


def _q_heads(q_lat, w_q, g_q, cos_t, sin_t):
    n = q_lat.shape[0]
    return pl.pallas_call(
        _q_heads_kernel,
        grid=(n // ATT_TM,),
        in_specs=[
            pl.BlockSpec((ATT_TM, Q_LORA_RANK), lambda i: (i, 0)),
            pl.BlockSpec((Q_LORA_RANK, N_HEADS * HEAD_PAD), lambda i: (0, 0)),
            pl.BlockSpec((1, HEAD_PAD), lambda i: (0, 0)),
            pl.BlockSpec((ATT_TM, LANES), lambda i: (_tab_idx_new(i), 0)),
            pl.BlockSpec((ATT_TM, LANES), lambda i: (_tab_idx_new(i), 0)),
        ],
        out_specs=pl.BlockSpec((N_HEADS, ATT_TM, HEAD_PAD), lambda i: (0, i, 0)),
        out_shape=jax.ShapeDtypeStruct((N_HEADS, n, HEAD_PAD), BF16),
        compiler_params=_cparams(1),
        name="q_heads",
    )(q_lat, w_q, g_q, cos_t, sin_t)


def _kv_heads_kernel(kvp_ref, kvs_ref, kr_ref, w_ref, gr_ref, c_ref, s_ref, k_ref, v_ref):
    kv = _stacked_rows(pl.program_id(0), N_P // ATT_TM, kvp_ref, kvs_ref).astype(BF16)
    u = kr_ref[...]
    ssq_r = jnp.sum(u * u, axis=-1, keepdims=True)
    krot = _rope_pair(u * gr_ref[...], c_ref[...], s_ref[...])
    for h in range(N_HEADS):
        z = _dot(kv, w_ref[:, h * HEAD_PAD:(h + 1) * HEAD_PAD])
        kn = z[:, :NOPE_DIM]
        ssq = jnp.sum(kn * kn, axis=-1, keepdims=True) + ssq_r
        scale = lax.rsqrt(ssq * (1.0 / QK_DIM) + EPS)
        k_ref[h, :, :NOPE_DIM] = (kn * scale).astype(BF16)
        k_ref[h, :, NOPE_DIM:] = (krot * scale).astype(BF16)
        v_ref[h] = z[:, NOPE_DIM:].astype(BF16)


def _kv_heads(kv_p, kv_s, kr_pad, w_kv, g_kn_rope, cos_t, sin_t, tab_idx, name):
    n = N_TOK
    npt = N_P // ATT_TM
    return pl.pallas_call(
        _kv_heads_kernel,
        grid=(n // ATT_TM,),
        in_specs=[
            pl.BlockSpec((ATT_TM, KV_LORA_RANK), lambda i: (jnp.minimum(i, npt - 1), 0)),
            pl.BlockSpec((ATT_TM, KV_LORA_RANK), lambda i: (jnp.maximum(i - npt, 0), 0)),
            pl.BlockSpec((ATT_TM, LANES), lambda i: (i, 0)),
            pl.BlockSpec((KV_LORA_RANK, N_HEADS * HEAD_PAD), lambda i: (0, 0)),
            pl.BlockSpec((1, LANES), lambda i: (0, 0)),
            pl.BlockSpec((ATT_TM, LANES), lambda i: (tab_idx(i), 0)),
            pl.BlockSpec((ATT_TM, LANES), lambda i: (tab_idx(i), 0)),
        ],
        out_specs=[
            pl.BlockSpec((N_HEADS, ATT_TM, HEAD_PAD), lambda i: (0, i, 0)),
            pl.BlockSpec((N_HEADS, ATT_TM, V_DIM), lambda i: (0, i, 0)),
        ],
        out_shape=[
            jax.ShapeDtypeStruct((N_HEADS, n, HEAD_PAD), BF16),
            jax.ShapeDtypeStruct((N_HEADS, n, V_DIM), BF16),
        ],
        compiler_params=_cparams(1),
        name=name,
    )(kv_p, kv_s, kr_pad, w_kv, g_kn_rope, cos_t, sin_t)


_TQ = 512
_TKB = 512
_HB = 4
_HBP = 4


def _flash_prompt_kernel(q_ref, k_ref, v_ref, o_ref, m_ref, l_ref, acc_ref):
    qi = pl.program_id(2)
    m_ref[...] = jnp.full(m_ref.shape, NEG_INF, F32)
    l_ref[...] = jnp.zeros(l_ref.shape, F32)
    acc_ref[...] = jnp.zeros(acc_ref.shape, F32)
    nlb = _TKB // LANES

    def step(ki, masked):
        start = pl.multiple_of(ki * _TKB, _TKB)
        scores = [lax.dot_general(q_ref[hh], k_ref[hh, pl.ds(start, _TKB), :], (((1,), (1,)), ((), ())),
                                  preferred_element_type=F32) for hh in range(_HBP)]
        probs = []
        for hh in range(_HBP):
            s = scores[hh]
            if masked:
                rc = lax.broadcasted_iota(I32, (_TQ, _TKB), 0) // CHUNK
                cc = lax.broadcasted_iota(I32, (_TQ, _TKB), 1) // CHUNK
                s = jnp.where(cc <= rc, s, NEG_INF)
            sb = [s[:, c * LANES:(c + 1) * LANES] for c in range(nlb)]
            bm = sb[0]
            for c in range(1, nlb):
                bm = jnp.maximum(bm, sb[c])
            m_prev = m_ref[hh]
            m_new = jnp.maximum(m_prev, jnp.max(bm, axis=-1, keepdims=True))
            alpha = jnp.exp2(m_prev - m_new)
            ps = [jnp.exp2(x - m_new) for x in sb]
            psum = ps[0]
            for c in range(1, nlb):
                psum = psum + ps[c]
            l_ref[hh] = alpha * l_ref[hh] + psum
            m_ref[hh] = m_new
            probs.append((alpha, jnp.concatenate(ps, axis=1).astype(BF16)))
        for hh in range(_HBP):
            alpha, p = probs[hh]
            acc_ref[hh] = alpha * acc_ref[hh] + _dot(p, v_ref[hh, pl.ds(start, _TKB), :])

    def body(kp, carry):
        step(2 * kp, False)
        step(2 * kp + 1, False)
        return carry

    lax.fori_loop(0, qi // 2, body, 0)

    @pl.when(qi % 2 == 1)
    def _():
        step(qi - 1, False)

    step(qi, True)
    for hh in range(_HBP):
        l = jnp.sum(l_ref[hh], axis=-1, keepdims=True)
        o_ref[:, hh * V_DIM:(hh + 1) * V_DIM] = (acc_ref[hh] / l).astype(BF16)


def _flash_prompt(q, k, v):
    nq = SEQ // _TQ
    return pl.pallas_call(
        _flash_prompt_kernel,
        grid=(BATCH, N_HEADS // _HBP, nq),
        in_specs=[
            pl.BlockSpec((_HBP, _TQ, HEAD_PAD), lambda b, h, i: (h, b * nq + i, 0)),
            pl.BlockSpec((_HBP, SEQ, HEAD_PAD), lambda b, h, i: (h, b, 0)),
            pl.BlockSpec((_HBP, SEQ, V_DIM), lambda b, h, i: (h, b, 0)),
        ],
        out_specs=pl.BlockSpec((_TQ, _HBP * V_DIM), lambda b, h, i: (b * nq + i, h)),
        out_shape=jax.ShapeDtypeStruct((N_TOK, N_HEADS * V_DIM), BF16),
        scratch_shapes=[pltpu.VMEM((_HBP, _TQ, LANES), F32), pltpu.VMEM((_HBP, _TQ, LANES), F32),
                        pltpu.VMEM((_HBP, _TQ, V_DIM), F32)],
        compiler_params=_cparams(3),
        name="flash_prompt",
    )(q, k, v)


_KC_ROWS = 512


def _flash_sample_kernel(prev_ref, q_ref, kv_ref, kr_ref, w_ref, gr_ref, c_ref, s_ref, kn_ref, vn_ref,
                         o_ref, kvb_ref, krot_ref, ssqr_ref, k_ref, v_ref):
    del prev_ref

    @pl.when(pl.program_id(1) == 0)
    def _():
        kvb_ref[...] = kv_ref[...].astype(BF16)
        u = kr_ref[...]
        ssqr_ref[...] = jnp.broadcast_to(jnp.sum(u * u, axis=-1, keepdims=True), ssqr_ref.shape)
        krot_ref[...] = _rope_pair(u * gr_ref[...], c_ref[...], s_ref[...])

    nt = (((1,), (1,)), ((), ()))
    for hh in range(_HB):
        w = w_ref[:, hh * HEAD_PAD:(hh + 1) * HEAD_PAD]
        for r in range(0, PAST_LEN, _KC_ROWS):
            rows = slice(r, r + _KC_ROWS)
            z = _dot(kvb_ref[rows, :], w)
            kn = z[:, :NOPE_DIM]
            ssq = jnp.sum(kn * kn, axis=-1, keepdims=True) + ssqr_ref[rows, :]
            scale = lax.rsqrt(ssq * (1.0 / QK_DIM) + EPS)
            k_ref[rows, :NOPE_DIM] = (kn * scale).astype(BF16)
            k_ref[rows, NOPE_DIM:] = (krot_ref[rows, :] * scale).astype(BF16)
            v_ref[rows, :] = z[:, NOPE_DIM:].astype(BF16)
        q = q_ref[hh]
        s1 = lax.dot_general(q, k_ref[...], nt, preferred_element_type=F32)
        s2 = lax.dot_general(q, kn_ref[hh], nt, preferred_element_type=F32)
        m = jnp.maximum(jnp.max(s1, axis=-1, keepdims=True), jnp.max(s2, axis=-1, keepdims=True))
        p1 = jnp.exp2(s1 - m)
        p2 = jnp.exp2(s2 - m)
        l = jnp.sum(p1, axis=-1, keepdims=True) + jnp.sum(p2, axis=-1, keepdims=True)
        o = _dot(p1.astype(BF16), v_ref[...]) + _dot(p2.astype(BF16), vn_ref[hh])
        o_ref[:, hh * V_DIM:(hh + 1) * V_DIM] = (o / l).astype(BF16)


def _flash_sample(attn, q, cache_kv, cache_kr_pad, w_kv, g_kn_rope, cos_t, sin_t, k_new, v_new):
    assert (PAST_LEN + DEC_SEQ - 1) // CHUNK <= PAST_LEN // CHUNK
    blk0 = N_P // DEC_SEQ
    new = lambda b, h: (h, blk0 + b, 0)
    const = lambda b, h: (0, 0)
    once = pl.Buffered(1)
    return pl.pallas_call(
        _flash_sample_kernel,
        grid=(DEC_BATCH, N_HEADS // _HB),
        in_specs=[
            pl.BlockSpec(memory_space=pl.ANY),
            pl.BlockSpec((_HB, DEC_SEQ, HEAD_PAD), new),
            pl.BlockSpec((PAST_LEN, KV_LORA_RANK), lambda b, h: (b, 0)),
            pl.BlockSpec((PAST_LEN, LANES), lambda b, h: (b, 0)),
            pl.BlockSpec((KV_LORA_RANK, _HB * HEAD_PAD), lambda b, h: (0, h)),
            pl.BlockSpec((1, LANES), const),
            pl.BlockSpec((PAST_LEN, LANES), const, pipeline_mode=once),
            pl.BlockSpec((PAST_LEN, LANES), const, pipeline_mode=once),
            pl.BlockSpec((_HB, DEC_SEQ, HEAD_PAD), new),
            pl.BlockSpec((_HB, DEC_SEQ, V_DIM), new),
        ],
        out_specs=pl.BlockSpec((DEC_SEQ, _HB * V_DIM), lambda b, h: (blk0 + b, h)),
        out_shape=jax.ShapeDtypeStruct((N_TOK, N_HEADS * V_DIM), BF16),
        scratch_shapes=[pltpu.VMEM((PAST_LEN, KV_LORA_RANK), BF16),
                        pltpu.VMEM((PAST_LEN, LANES), F32),
                        pltpu.VMEM((PAST_LEN, LANES), F32),
                        pltpu.VMEM((PAST_LEN, HEAD_PAD), BF16),
                        pltpu.VMEM((PAST_LEN, V_DIM), BF16)],
        input_output_aliases={0: 0},
        compiler_params=_cparams(2),
        name="flash_sample",
    )(attn, q, cache_kv, cache_kr_pad, w_kv, g_kn_rope, cos_t, sin_t, k_new, v_new)


def _merge_kernel(h_ref, c_ref, a_ref, wga_ref, wgb_ref, bga_ref, bgb_ref, wc_ref, wo_ref, o_ref):
    h = h_ref[...]
    ga = _sigmoid(_dot(h, wga_ref[...]) + bga_ref[...])
    gb = _sigmoid(_dot(h, wgb_ref[...]) + bgb_ref[...])
    mix = ga * _dot(c_ref[...], wc_ref[...]) + gb * _dot(a_ref[...], wo_ref[...])
    o_ref[...] = mix.astype(BF16)


def _merge(h, c_act, attn, w_gate, b_gate, w_conv_out, w_o):
    n = h.shape[0]
    tn = 512
    nj = D_MODEL // tn
    row = lambda i, j: (i, 0)
    return pl.pallas_call(
        _merge_kernel,
        grid=(n // TM, nj),
        in_specs=[
            pl.BlockSpec((TM, D_MODEL), row),
            pl.BlockSpec((TM, CONV_CHANNELS), row),
            pl.BlockSpec((TM, N_HEADS * V_DIM), row),
            pl.BlockSpec((D_MODEL, tn), lambda i, j: (0, j)),
            pl.BlockSpec((D_MODEL, tn), lambda i, j: (0, j + nj)),
            pl.BlockSpec((1, tn), lambda i, j: (0, j)),
            pl.BlockSpec((1, tn), lambda i, j: (0, j + nj)),
            pl.BlockSpec((CONV_CHANNELS, tn), lambda i, j: (0, j)),
            pl.BlockSpec((N_HEADS * V_DIM, tn), lambda i, j: (0, j)),
        ],
        out_specs=pl.BlockSpec((TM, tn), lambda i, j: (i, j)),
        out_shape=jax.ShapeDtypeStruct((n, D_MODEL), BF16),
        compiler_params=_cparams(2),
        name="merge",
    )(h, c_act, attn, w_gate, w_gate, b_gate, b_gate, w_conv_out, w_o)


def _split_bf16(x):
    hi = x.astype(BF16)
    lo = (x - hi.astype(F32)).astype(BF16)
    return hi, lo


_HALF = D_MODEL // 2


def _pack_bf16_pair(a, b):
    ua = lax.bitcast_convert_type(a.astype(BF16).astype(F32), U32)
    ub = lax.bitcast_convert_type(b.astype(BF16).astype(F32), U32)
    return lax.bitcast_convert_type(ua | (ub >> 16), F32)


def _unpack_bf16_pair(w):
    w = lax.bitcast_convert_type(w, U32)
    a = lax.bitcast_convert_type(w & jnp.uint32(0xFFFF0000), F32).astype(BF16)
    b = lax.bitcast_convert_type(w << 16, F32).astype(BF16)
    return a, b


def _out_router_kernel(n_prompt_tiles, mix_ref, xp_ref, xs_ref, w_ref, g_ref, wrh_ref, wrl_ref, br_ref,
                       x1_ref, hm_ref, idx_ref, gate_ref):
    x = _stacked_rows(pl.program_id(0), n_prompt_tiles, xp_ref, xs_ref)
    x1 = x + _dot(mix_ref[...], w_ref[...])
    x1_ref[...] = x1
    hn = x1 * lax.rsqrt(jnp.mean(x1 * x1, axis=-1, keepdims=True) + EPS) * g_ref[...]
    hm_ref[0] = _pack_bf16_pair(hn[:, :_HALF], hn[:, _HALF:])
    hm_ref[1] = jnp.zeros(hm_ref.shape[1:], F32)
    hh, hl = _split_bf16(hn)
    logits = _dot(hh, wrh_ref[...]) + (_dot(hh, wrl_ref[...]) + _dot(hl, wrh_ref[...])) + br_ref[...]
    lane = lax.broadcasted_iota(I32, logits.shape, 1).astype(F32)
    vals = []
    idx_out = jnp.zeros(logits.shape, F32)
    for k in range(TOP_K):
        m = jnp.max(logits, axis=-1, keepdims=True)
        sel = jnp.min(jnp.where(logits == m, lane, 1e9), axis=-1, keepdims=True)
        vals.append(m)
        idx_out = jnp.where(lane == float(k), sel, idx_out)
        logits = jnp.where(lane == sel, -jnp.inf, logits)
    exps = [jnp.exp(v - vals[0]) for v in vals]
    denom = exps[0] + exps[1] + exps[2] + exps[3]
    gate_out = jnp.zeros(idx_out.shape, F32)
    for k in range(TOP_K):
        gate_out = jnp.where(lane == float(k), exps[k] / denom, gate_out)
    idx_ref[...] = idx_out.astype(I32)
    gate_ref[...] = gate_out


def _out_router(mix, xp, xs, w_out, g_ffn, wr_hi, wr_lo, b_r):
    n = N_TOK
    tm = TM
    n_tiles = n // tm
    npt = N_P // tm
    const = lambda i: (0, 0)
    row = lambda i: (i, 0)
    once = pl.Buffered(1)
    return pl.pallas_call(
        functools.partial(_out_router_kernel, npt),
        grid=(n_tiles,),
        in_specs=[
            pl.BlockSpec((tm, D_MODEL), row),
            pl.BlockSpec((tm, D_MODEL), lambda i: (jnp.minimum(i, npt - 1), 0)),
            pl.BlockSpec((tm, D_MODEL), lambda i: (jnp.clip(i - npt, 0, N_S // tm - 1), 0)),
            pl.BlockSpec((D_MODEL, D_MODEL), const, pipeline_mode=once),
            pl.BlockSpec((1, D_MODEL), const),
            pl.BlockSpec((D_MODEL, LANES), const, pipeline_mode=once),
            pl.BlockSpec((D_MODEL, LANES), const, pipeline_mode=once),
            pl.BlockSpec((1, LANES), const),
        ],
        out_specs=[
            pl.BlockSpec((tm, D_MODEL), row),
            pl.BlockSpec((2, tm, _HALF), lambda i: (0, i, 0)),
            pl.BlockSpec((tm, LANES), row),
            pl.BlockSpec((tm, LANES), row),
        ],
        out_shape=[
            jax.ShapeDtypeStruct((n, D_MODEL), F32),
            jax.ShapeDtypeStruct((2, n, _HALF), F32),
            jax.ShapeDtypeStruct((n, LANES), I32),
            jax.ShapeDtypeStruct((n, LANES), F32),
        ],
        compiler_params=_cparams(1),
        name="out_router",
    )(mix, xp, xs, w_out, g_ffn, wr_hi, wr_lo, b_r)


_F_VALID, _F_FIRST, _F_NEXT, _F_GROUP0 = 1, 2, 4, 8


_P_E, _P_W, _P_N, _P_B, _P_BI, _P_NE, _P_NW, _P_FL, _P_SUBS = range(9)
MOE_SUB = 64


def _stream_weights(t, plan_ref, copies, cast):
    flags = plan_ref[_P_FL, t]

    @pl.when((flags & _F_FIRST) != 0)
    def _():
        cur = copies(plan_ref[_P_E, t], plan_ref[_P_W, t])

        @pl.when((flags & _F_GROUP0) != 0)
        def _():
            for c in cur:
                c.start()

        for c in cur:
            c.wait()
        cast()

        @pl.when((flags & _F_NEXT) != 0)
        def _():
            for c in copies(plan_ref[_P_NE, t], plan_ref[_P_NW, t]):
                c.start()


def _for_used_rows(valid, subs, rows_body):
    for n_sub in range(1, MOE_BLK // MOE_SUB + 1):
        @pl.when(jnp.logical_and(valid, subs == n_sub))
        def _(m=n_sub * MOE_SUB):
            rows_body(m)


def _moe_up_kernel(plan_ref, prev_ref, x_ref, w_hbm, bg_ref, bu_ref, o_ref, wbuf_ref, wgb_ref, wub_ref, sem_ref):
    del prev_ref
    t = pl.program_id(0)

    def copies(e, w):
        col = pl.multiple_of(w * _UP_TN, _UP_TN)
        return (pltpu.make_async_copy(w_hbm.at[e, :, pl.ds(col, _UP_TN)], wbuf_ref.at[0], sem_ref.at[0]),
                pltpu.make_async_copy(w_hbm.at[e, :, pl.ds(col + D_FF, _UP_TN)], wbuf_ref.at[1], sem_ref.at[1]))

    def cast():
        wgb_ref[...] = wbuf_ref[0].astype(BF16)
        wub_ref[...] = wbuf_ref[1].astype(BF16)

    _stream_weights(t, plan_ref, copies, cast)
    valid = (plan_ref[_P_FL, t] & _F_VALID) != 0

    def rows_body(m):
        xa, xb = _unpack_bf16_pair(x_ref[:m, :])
        g = _dot(xa, wgb_ref[:_HALF, :]) + _dot(xb, wgb_ref[_HALF:, :]) + bg_ref[0]
        u = _dot(xa, wub_ref[:_HALF, :]) + _dot(xb, wub_ref[_HALF:, :]) + bu_ref[0]
        g = jnp.minimum(g, SWIGLU_LIMIT)
        u = jnp.clip(u, -SWIGLU_LIMIT, SWIGLU_LIMIT)
        o_ref[:m, :] = ((u + 1.0) * (g * _sigmoid(SWIGLU_ALPHA * g))).astype(BF16)
        if m < MOE_BLK:
            o_ref[m:, :] = jnp.zeros((MOE_BLK - m, o_ref.shape[1]), BF16)

    _for_used_rows(valid, plan_ref[_P_SUBS, t], rows_body)

    @pl.when(jnp.logical_not(valid))
    def _():
        o_ref[...] = jnp.zeros(o_ref.shape, BF16)


_UP_TN = 1024
_UP_TILES = D_FF // _UP_TN
_DN_TN = 2048
_DN_TILES = D_MODEL // _DN_TN
MOE_CHUNKS = 4
_CHUNK_BLKS = MOE_MAX_BLKS // MOE_CHUNKS


def _moe_up(plan, act_prev, xs, w_gu, b_gu, chunk):
    steps = plan.shape[1]
    blk0 = chunk * _CHUNK_BLKS
    bspec = lambda off: pl.BlockSpec((1, 1, _UP_TN), lambda t, p: (p[_P_E, t], 0, p[_P_W, t] + off))
    aliases = {} if act_prev is None else {1: 0}
    prev = jnp.zeros((SUBLANES, LANES), BF16) if act_prev is None else act_prev
    return pl.pallas_call(
        _moe_up_kernel,
        grid_spec=pltpu.PrefetchScalarGridSpec(
            num_scalar_prefetch=1,
            grid=(steps,),
            in_specs=[
                pl.BlockSpec(memory_space=pl.ANY),
                pl.BlockSpec((MOE_BLK, _HALF), lambda t, p: (p[_P_BI, t], 0)),
                pl.BlockSpec(memory_space=pl.ANY),
                bspec(0), bspec(_UP_TILES),
            ],
            out_specs=pl.BlockSpec((MOE_BLK, _UP_TN),
                                   lambda t, p: (blk0 + p[_P_B, t], p[_P_N, t])),
            scratch_shapes=[pltpu.VMEM((2, D_MODEL, _UP_TN), F32),
                            pltpu.VMEM((D_MODEL, _UP_TN), BF16), pltpu.VMEM((D_MODEL, _UP_TN), BF16),
                            pltpu.SemaphoreType.DMA((2,))],
        ),
        out_shape=jax.ShapeDtypeStruct((MOE_ROWS, D_FF), BF16),
        input_output_aliases=aliases,
        compiler_params=_cparams(1),
        name=f"moe_up_{chunk}",
    )(plan, prev, xs, w_gu, b_gu, b_gu)


_DN_HALF = _DN_TN // 2


def _moe_down_kernel(plan_ref, a_ref, w_hbm, b_ref, o_ref, wbuf_ref, wb_ref, sem_ref):
    t = pl.program_id(0)

    def copies(e, w):
        col = pl.multiple_of(w * _DN_TN, _DN_TN)
        return (pltpu.make_async_copy(w_hbm.at[e, :, pl.ds(col, _DN_TN)], wbuf_ref, sem_ref.at[0]),)

    def cast():
        wb_ref[...] = wbuf_ref[...].astype(BF16)

    _stream_weights(t, plan_ref, copies, cast)
    valid = (plan_ref[_P_FL, t] & _F_VALID) != 0

    def rows_body(m):
        y = _dot(a_ref[:m, :], wb_ref[...]) + b_ref[0]
        o_ref[:m, :] = _pack_bf16_pair(y[:, :_DN_HALF], y[:, _DN_HALF:])
        if m < MOE_BLK:
            o_ref[m:, :] = jnp.zeros((MOE_BLK - m, o_ref.shape[1]), F32)

    _for_used_rows(valid, plan_ref[_P_SUBS, t], rows_body)

    @pl.when(jnp.logical_not(valid))
    def _():
        o_ref[...] = jnp.zeros(o_ref.shape, F32)


def _moe_down(plan, act, w_dn, b_dn):
    steps = plan.shape[1]
    return pl.pallas_call(
        _moe_down_kernel,
        grid_spec=pltpu.PrefetchScalarGridSpec(
            num_scalar_prefetch=1,
            grid=(steps,),
            in_specs=[
                pl.BlockSpec((MOE_BLK, D_FF), lambda t, p: (p[_P_BI, t], 0)),
                pl.BlockSpec(memory_space=pl.ANY),
                pl.BlockSpec((1, 1, _DN_TN), lambda t, p: (p[_P_E, t], 0, p[_P_W, t])),
            ],
            out_specs=pl.BlockSpec((MOE_BLK, _DN_HALF), lambda t, p: (p[_P_B, t], p[_P_N, t])),
            scratch_shapes=[pltpu.VMEM((D_FF, _DN_TN), F32), pltpu.VMEM((D_FF, _DN_TN), BF16),
                            pltpu.SemaphoreType.DMA((1,))],
        ),
        out_shape=jax.ShapeDtypeStruct((MOE_ROWS, _HALF), F32),
        compiler_params=_cparams(1),
        name="moe_down",
    )(plan, act, w_dn, b_dn)


def _moe_dispatch(top_idx):
    n_asg = N_TOK * TOP_K
    flat_e = top_idx.reshape(-1)
    onehot = (flat_e[:, None] == jnp.arange(N_EXPERTS, dtype=I32)[None, :]).astype(I32)
    csum = jnp.cumsum(onehot, axis=0)
    counts = csum[-1]
    rank = jnp.sum(csum * onehot, axis=1) - 1
    nblk = (counts + MOE_BLK - 1) // MOE_BLK
    blk_start = jnp.cumsum(nblk) - nblk
    dest = jnp.sum(onehot * blk_start[None, :], axis=1) * MOE_BLK + rank
    pad_src = jnp.arange(MOE_ROWS, dtype=I32) % N_TOK
    row_tok = pad_src.at[dest].set(jnp.arange(n_asg, dtype=I32) // TOP_K,
                                   mode="promise_in_bounds", unique_indices=True)
    return dest, row_tok, counts, nblk, blk_start


def _moe_steps(counts, nblk, blk_start, n_tiles, blk_lo, n_blks):
    t_max = n_tiles * n_blks
    lo = jnp.clip(blk_start, blk_lo, blk_lo + n_blks)
    hi = jnp.clip(blk_start + nblk, blk_lo, blk_lo + n_blks)
    nb_e = hi - lo
    per_e = nb_e * n_tiles
    s_end = jnp.cumsum(per_e)
    total = s_end[-1]
    t = jnp.arange(t_max, dtype=I32)
    tc = jnp.clip(t, 0, jnp.maximum(total - 1, 0))
    e = jnp.minimum(jnp.sum((s_end[None, :] <= tc[:, None]).astype(I32), axis=1), N_EXPERTS - 1)
    sel = (e[:, None] == jnp.arange(N_EXPERTS, dtype=I32)[None, :]).astype(I32)
    pick = lambda v: jnp.sum(sel * v[None, :], axis=1)
    local = tc - pick(s_end - per_e)
    nb = jnp.maximum(pick(nb_e), 1)
    w_tile = jnp.clip(local // nb, 0, n_tiles - 1)
    r = local % nb
    valid = t < total
    first = jnp.logical_and(valid, r == 0)
    fill = t - total
    blk = jnp.where(valid, pick(lo) - blk_lo + r, total // n_tiles + fill // n_tiles)
    rows_used = pick(counts) - (pick(lo) + r - pick(blk_start)) * MOE_BLK
    subs = jnp.clip((rows_used + MOE_SUB - 1) // MOE_SUB, 1, MOE_BLK // MOE_SUB)
    o_tile = jnp.where(valid, w_tile, fill % n_tiles)
    blk = jnp.clip(blk, 0, n_blks - 1)
    blk_in = jnp.where(valid, blk, jnp.maximum(total // n_tiles - 1, 0))
    ids = jnp.arange(N_EXPERTS, dtype=I32)
    owners = jnp.where(nb_e > 0, ids, N_EXPERTS)
    later = jnp.flip(lax.cummin(jnp.flip(owners)))
    next_owner = pick(jnp.concatenate([later[1:], jnp.full((1,), N_EXPERTS, I32)]))
    last_tile = w_tile == n_tiles - 1
    next_e = jnp.where(last_tile, next_owner, e)
    next_w = jnp.where(last_tile, 0, w_tile + 1)
    has_next = jnp.logical_and(first, next_e < N_EXPERTS)
    group = jnp.cumsum(first.astype(I32)) - 1
    flags = (valid * _F_VALID + first * _F_FIRST + has_next * _F_NEXT
             + jnp.logical_and(first, group == 0) * _F_GROUP0)
    rows = {_P_E: e, _P_W: w_tile, _P_N: o_tile, _P_B: blk, _P_BI: blk_in,
            _P_NE: jnp.minimum(next_e, N_EXPERTS - 1), _P_NW: next_w, _P_FL: flags, _P_SUBS: subs}
    return jnp.stack([rows[k].astype(I32) for k in range(len(rows))])


def _moe_plans(counts, nblk, blk_start, n_tiles, n_chunks, n_blks):
    los = jnp.arange(n_chunks, dtype=I32) * n_blks
    return jax.vmap(lambda lo: _moe_steps(counts, nblk, blk_start, n_tiles, lo, n_blks))(los)


_FIN_TM = 256
_FIN_TN = 512
FIN_CHUNKS = 4


def _unpack_expert_rows(words):
    u = lax.bitcast_convert_type(words, U32)
    hi = lax.bitcast_convert_type(u & jnp.uint32(0xFFFF0000), F32)
    lo = lax.bitcast_convert_type(u << 16, F32)
    parts = []
    for n in range(_DN_TILES):
        cols = slice(n * _DN_HALF, (n + 1) * _DN_HALF)
        parts += [hi[:, cols], lo[:, cols]]
    return jnp.concatenate(parts, axis=1)


def _final_kernel(prev_ref, x1_ref, y0_ref, y1_ref, y2_ref, y3_ref, gate_ref, g_ref, wg_ref, p_ref, wp_ref,
                  o_ref, x2_ref):
    del prev_ref
    gate = gate_ref[...]
    moe = (_unpack_expert_rows(y0_ref[0]) * gate[:, 0:1] + _unpack_expert_rows(y1_ref[0]) * gate[:, 1:2]
           + _unpack_expert_rows(y2_ref[0]) * gate[:, 2:3] + _unpack_expert_rows(y3_ref[0]) * gate[:, 3:4])
    x2 = x1_ref[...] + moe
    x2_ref[...] = x2
    hp = (x2 * lax.rsqrt(jnp.mean(x2 * x2, axis=-1, keepdims=True) + EPS) * g_ref[...]).astype(BF16)
    pb = p_ref[...].astype(BF16)
    for c in range(0, D_MODEL, _FIN_TN):
        cols = slice(c, c + _FIN_TN)
        emb = _dot(pb, wp_ref[:, cols])
        o_ref[:, cols] = x2_ref[:, cols] + _sigmoid(_dot(hp, wg_ref[:, cols])) * emb


def _final(out_prev, x1, y4, gate, g_ple, w_ple_gate, p, w_ple, tok0, out0, n, n_out, name):
    t0 = tok0 // _FIN_TM
    o0 = out0 // _FIN_TM
    pt0 = out0 // _FIN_TM
    const = lambda i: (0, 0)
    yspec = lambda k: pl.BlockSpec((1, _FIN_TM, _HALF), lambda i: (k, i, 0))
    once = pl.Buffered(1)
    aliases = {} if out_prev is None else {0: 0}
    prev = jnp.zeros((SUBLANES, LANES), F32) if out_prev is None else out_prev
    return pl.pallas_call(
        _final_kernel,
        grid=(n // _FIN_TM,),
        in_specs=[
            pl.BlockSpec(memory_space=pl.ANY),
            pl.BlockSpec((_FIN_TM, D_MODEL), lambda i: (t0 + i, 0)),
            yspec(0), yspec(1), yspec(2), yspec(3),
            pl.BlockSpec((_FIN_TM, LANES), lambda i: (t0 + i, 0)),
            pl.BlockSpec((1, D_MODEL), const),
            pl.BlockSpec((D_MODEL, D_MODEL), const, pipeline_mode=once),
            pl.BlockSpec((_FIN_TM, PLE_DIM), lambda i: (pt0 + i, 0)),
            pl.BlockSpec((PLE_DIM, D_MODEL), const, pipeline_mode=once),
        ],
        out_specs=pl.BlockSpec((_FIN_TM, D_MODEL), lambda i: (o0 + i, 0)),
        out_shape=jax.ShapeDtypeStruct((n_out, D_MODEL), F32),
        scratch_shapes=[pltpu.VMEM((_FIN_TM, D_MODEL), F32)],
        input_output_aliases=aliases,
        compiler_params=_cparams(1),
        name=name,
    )(prev, x1, y4, y4, y4, y4, gate, g_ple, w_ple_gate, p, w_ple)


def _rope_layout(x):
    half = ROPE_DIM // 2
    z = jnp.zeros(x.shape[:-1] + (half,), x.dtype)
    return jnp.concatenate([x[..., :half], z, x[..., half:], z], axis=-1)


def _rope_tables():
    half = ROPE_DIM // 2
    inv_freq = ROPE_THETA ** (-jnp.arange(half, dtype=F32) / half)
    pos = jnp.arange(PAST_LEN + DEC_SEQ, dtype=I32)
    ang = pos.astype(F32)[:, None] * inv_freq[None, :]
    cos, sin = jnp.cos(ang), jnp.sin(ang)
    z = jnp.zeros_like(cos)
    c = jnp.concatenate([cos, z, cos, z], axis=-1)
    s = jnp.concatenate([-sin, z, sin, z], axis=-1)
    rep = ATT_TM // DEC_SEQ
    return (jnp.concatenate([c[:SEQ], jnp.tile(c[PAST_LEN:], (rep, 1))], axis=0),
            jnp.concatenate([s[:SEQ], jnp.tile(s[PAST_LEN:], (rep, 1))], axis=0))


def _layer(xp, xs, p_prompt, p_sample, cache_kv, cache_kr, state_conv,
           g_mix, w_in, b_gate, w_dw, b_dw, g_cn, b_cn, w_conv_out,
           g_qa, g_kva, w_qb, w_kb, w_vb, g_qn, g_kn, w_o, w_out,
           g_ffn, w_router, b_router, w_gu, b_gu, w_dn, b_dn,
           g_ple, w_ple_gate, w_ple):
    assert SEQ == PAST_LEN
    row = lambda v: v.reshape(1, -1)
    w_in_b = w_in.astype(BF16)
    w_mid = jnp.concatenate([w_in_b[:, O_U:O_KV], _rope_layout(w_in_b[:, O_KV:O_KR])], axis=1)
    w_gate = w_in_b[:, O_KR:]

    h, q_lat, kv_p, kv_s, kr_pad = _in_mid(xp, xs, row(g_mix), w_mid, row(g_qa), row(g_kva))
    half = ROPE_DIM // 2
    kr_new = jnp.concatenate([kr_pad[:, :half], kr_pad[:, 2 * half:3 * half]], axis=1)
    glu = _in_glu(h, w_in_b)

    hist_s = jnp.pad(state_conv, ((0, 0), (HALO - (CONV_WIDTH - 1), 0), (0, 0)))
    c_act = _conv_module(glu, hist_s, w_dw, row(b_dw), row(g_cn), row(b_cn))

    cos_t, sin_t = _rope_tables()
    w_q = jnp.concatenate([w_qb[..., :NOPE_DIM], _rope_layout(w_qb[..., NOPE_DIM:])], axis=-1)
    w_q = w_q.reshape(Q_LORA_RANK, N_HEADS * HEAD_PAD).astype(BF16)
    g_q = jnp.concatenate([g_qn[:NOPE_DIM] * g_kn[:NOPE_DIM], _rope_layout(g_qn[NOPE_DIM:])]).reshape(1, HEAD_PAD)
    q = _q_heads(q_lat, w_q, g_q, cos_t, sin_t)

    w_kv = jnp.concatenate([w_kb, w_vb], axis=-1).reshape(KV_LORA_RANK, N_HEADS * HEAD_PAD).astype(BF16)
    g_kn_rope = _rope_layout(g_kn[NOPE_DIM:]).reshape(1, LANES)
    k_new, v_new = _kv_heads(kv_p, kv_s, kr_pad, w_kv, g_kn_rope, cos_t, sin_t, _tab_idx_new, "kv_heads_new")
    attn = _flash_prompt(q, k_new, v_new)
    attn = _flash_sample(attn, q, cache_kv.reshape(DEC_BATCH * PAST_LEN, KV_LORA_RANK),
                         _rope_layout(cache_kr).reshape(DEC_BATCH * PAST_LEN, LANES),
                         w_kv, g_kn_rope, cos_t, sin_t, k_new, v_new)

    mix = _merge(h, c_act, attn, w_gate, row(b_gate), w_conv_out.astype(BF16), w_o.astype(BF16))

    wr = jnp.pad(w_router, ((0, 0), (0, LANES - N_EXPERTS)))
    wr_hi, wr_lo = _split_bf16(wr)
    b_r = jnp.concatenate([b_router, jnp.full((LANES - N_EXPERTS,), -jnp.inf, F32)]).reshape(1, LANES)
    x1, hm, idx_pad, gate_pad = _out_router(mix, xp, xs, w_out.astype(BF16), row(g_ffn), wr_hi, wr_lo, b_r)
    hm = hm.reshape(2 * N_TOK, _HALF)

    top_idx = idx_pad[:, :TOP_K]
    dest, row_tok, counts, nblk, blk_start = _moe_dispatch(top_idx)
    b_gu3 = b_gu.reshape(N_EXPERTS, 1, 2 * D_FF)
    chunk_rows = _CHUNK_BLKS * MOE_BLK
    up_plans = _moe_plans(counts, nblk, blk_start, _UP_TILES, MOE_CHUNKS, _CHUNK_BLKS)
    down_plan = _moe_plans(counts, nblk, blk_start, _DN_TILES, 1, MOE_MAX_BLKS)[0]
    act = None
    for c in range(MOE_CHUNKS):
        xs = hm.at[row_tok[c * chunk_rows:(c + 1) * chunk_rows]].get(mode="promise_in_bounds")
        act = _moe_up(up_plans[c], act, xs, w_gu, b_gu3, c)
    ys = _moe_down(down_plan, act, w_dn, b_dn.reshape(N_EXPERTS, 1, D_MODEL))

    dest_t = dest.reshape(N_TOK, TOP_K).T
    fin = (row(g_ple), w_ple_gate.astype(BF16))
    w_ple_b = w_ple.astype(BF16)
    n_c = N_P // FIN_CHUNKS
    out_p = None
    for c in range(FIN_CHUNKS):
        y4 = ys.at[dest_t[:, c * n_c:(c + 1) * n_c]].get(mode="promise_in_bounds")
        out_p = _final(out_p, x1, y4, gate_pad, *fin, p_prompt, w_ple_b, c * n_c, c * n_c, n_c, N_P,
                       f"final_prompt_{c}")
    y4 = ys.at[dest_t[:, N_P:]].get(mode="promise_in_bounds")
    out_s = _final(None, x1, y4, gate_pad, *fin, p_sample, w_ple_b, N_P, 0, N_S, N_S, "final_sample")
    return out_p, out_s, kv_p, kv_s, kr_new, glu


def kernel(x_prompt, x_sample, cache_kv_latent, cache_k_rope, state_conv, p_prompt, p_sample, g_mix, w_in, b_gate, w_dw, b_dw, g_cn, b_cn, w_conv_out, g_qa, g_kva, w_qb, w_kb, w_vb, g_qn, g_kn, w_o, w_out, g_ffn, w_router, b_router, w_gu, b_gu, w_dn, b_dn, g_ple, w_ple_gate, w_ple):
    assert g_mix.shape[0] == 1
    out_p, out_s, kv_p, kv_s, kr_new, glu = _layer(
        x_prompt.reshape(N_P, D_MODEL), x_sample.reshape(N_S, D_MODEL),
        p_prompt[0].reshape(N_P, PLE_DIM), p_sample[0].reshape(N_S, PLE_DIM),
        cache_kv_latent[0], cache_k_rope[0], state_conv[0],
        g_mix[0], w_in[0], b_gate[0], w_dw[0], b_dw[0], g_cn[0], b_cn[0], w_conv_out[0],
        g_qa[0], g_kva[0], w_qb[0], w_kb[0], w_vb[0], g_qn[0], g_kn[0], w_o[0], w_out[0],
        g_ffn[0], w_router[0], b_router[0], w_gu[0], b_gu[0], w_dn[0], b_dn[0],
        g_ple[0], w_ple_gate[0], w_ple[0])
    tail = CONV_WIDTH - 1
    conv_p = jnp.stack([glu[(b + 1) * SEQ - tail:(b + 1) * SEQ] for b in range(BATCH)])
    conv_s = glu[N_P:].reshape(DEC_BATCH, DEC_SEQ, CONV_CHANNELS)[:, DEC_SEQ - tail:]
    return (out_p.reshape(BATCH, SEQ, D_MODEL),
            out_s.reshape(DEC_BATCH, DEC_SEQ, D_MODEL),
            kv_p.reshape(1, BATCH, SEQ, KV_LORA_RANK),
            kr_new[:N_P].reshape(1, BATCH, SEQ, ROPE_DIM),
            conv_p[None],
            kv_s.reshape(1, DEC_BATCH, DEC_SEQ, KV_LORA_RANK),
            kr_new[N_P:].reshape(1, DEC_BATCH, DEC_SEQ, ROPE_DIM),
            conv_s[None])
```

```python
import functools
import math

import jax
import jax.numpy as jnp
from jax import lax
from jax.experimental import pallas as pl
from jax.experimental.pallas import tpu as pltpu

F32 = jnp.float32
BF16 = jnp.bfloat16
I32 = jnp.int32
U32 = jnp.uint32

D_MODEL = 2048
BATCH = 2
SEQ = 4096
DEC_BATCH = 8
DEC_SEQ = 64
PAST_LEN = 4096
CHUNK = 64
CONV_CHANNELS = D_MODEL
CONV_WIDTH = 31
N_HEADS = 16
Q_LORA_RANK = 512
KV_LORA_RANK = 512
NOPE_DIM = 128
ROPE_DIM = 64
QK_DIM = NOPE_DIM + ROPE_DIM
V_DIM = 128
ROPE_THETA = 10000.0
N_EXPERTS = 32
TOP_K = 4
D_FF = D_MODEL
SWIGLU_ALPHA = 1.702
SWIGLU_LIMIT = 7.0
PLE_DIM = 256
EPS = 1e-6
NEG_INF = -1e30

N_P = BATCH * SEQ
N_S = DEC_BATCH * DEC_SEQ
N_TOK = N_P + N_S
O_U = 2 * CONV_CHANNELS
O_Q = O_U + Q_LORA_RANK
O_KV = O_Q + KV_LORA_RANK
O_KR = O_KV + ROPE_DIM
LANES = 128
SUBLANES = 8
MID_W = Q_LORA_RANK + KV_LORA_RANK + LANES
HEAD_PAD = NOPE_DIM + LANES

TM = 512
CONV_T = 64
HALO = 32
MOE_BLK = 512
MOE_MAX_BLKS = (N_TOK * TOP_K) // MOE_BLK + N_EXPERTS
MOE_ROWS = MOE_MAX_BLKS * MOE_BLK
VMEM_LIMIT = 48 * 1024 * 1024
assert 2 * ROPE_DIM == LANES and NOPE_DIM == LANES and V_DIM == LANES and SEQ == PAST_LEN


def _cparams(n_axes):
    return pltpu.CompilerParams(dimension_semantics=("arbitrary",) * n_axes,
                                vmem_limit_bytes=VMEM_LIMIT)


def _sigmoid(x):
    return 1.0 / (1.0 + jnp.exp(-x))


def _dot(a, b):
    return jnp.dot(a, b, preferred_element_type=F32)


def _stacked_rows(i, n_prompt_tiles, xp_ref, xs_ref):
    return jnp.where(i < n_prompt_tiles, xp_ref[...], xs_ref[...])


def _in_mid_kernel(xp_ref, xs_ref, g_ref, w_ref, gqa_ref, gkva_ref, h_ref, q_ref, kvp_ref, kvs_ref, kr_ref):
    i = pl.program_id(0)
    x = _stacked_rows(i, N_P // TM, xp_ref, xs_ref)
    h = x * lax.rsqrt(jnp.mean(x * x, axis=-1, keepdims=True) + EPS) * g_ref[...]
    hb = h.astype(BF16)
    h_ref[...] = hb
    z = _dot(hb, w_ref[...])
    ql = z[:, :Q_LORA_RANK]
    kvl = z[:, Q_LORA_RANK:Q_LORA_RANK + KV_LORA_RANK]
    qn = ql * lax.rsqrt(jnp.mean(ql * ql, axis=-1, keepdims=True) + EPS) * gqa_ref[...]
    q_ref[...] = qn.astype(BF16)
    kv = kvl * lax.rsqrt(jnp.mean(kvl * kvl, axis=-1, keepdims=True) + EPS) * gkva_ref[...]
    kr_ref[...] = z[:, Q_LORA_RANK + KV_LORA_RANK:]

    @pl.when(i < N_P // TM)
    def _():
        kvp_ref[...] = kv

    @pl.when(i >= N_P // TM)
    def _():
        kvs_ref[...] = kv


def _in_mid(xp, xs, g_mix, w_mid, g_qa, g_kva):
    n = N_TOK
    npt = N_P // TM
    return pl.pallas_call(
        _in_mid_kernel,
        grid=(n // TM,),
        in_specs=[
            pl.BlockSpec((TM, D_MODEL), lambda i: (jnp.minimum(i, npt - 1), 0)),
            pl.BlockSpec((TM, D_MODEL), lambda i: (jnp.maximum(i - npt, 0), 0)),
            pl.BlockSpec((1, D_MODEL), lambda i: (0, 0)),
            pl.BlockSpec((D_MODEL, MID_W), lambda i: (0, 0)),
            pl.BlockSpec((1, Q_LORA_RANK), lambda i: (0, 0)),
            pl.BlockSpec((1, KV_LORA_RANK), lambda i: (0, 0)),
        ],
        out_specs=[
            pl.BlockSpec((TM, D_MODEL), lambda i: (i, 0)),
            pl.BlockSpec((TM, Q_LORA_RANK), lambda i: (i, 0)),
            pl.BlockSpec((TM, KV_LORA_RANK), lambda i: (jnp.minimum(i, npt - 1), 0)),
            pl.BlockSpec((TM, KV_LORA_RANK), lambda i: (jnp.maximum(i - npt, 0), 0)),
            pl.BlockSpec((TM, LANES), lambda i: (i, 0)),
        ],
        out_shape=[
            jax.ShapeDtypeStruct((n, D_MODEL), BF16),
            jax.ShapeDtypeStruct((n, Q_LORA_RANK), BF16),
            jax.ShapeDtypeStruct((N_P, KV_LORA_RANK), F32),
            jax.ShapeDtypeStruct((N_S, KV_LORA_RANK), F32),
            jax.ShapeDtypeStruct((n, LANES), F32),
        ],
        compiler_params=_cparams(1),
        name="in_mid",
    )(xp, xs, g_mix, w_mid, g_qa, g_kva)


def _glu_kernel(h_ref, w1_ref, w2_ref, o_ref):
    h = h_ref[...]
    o_ref[...] = _dot(h, w1_ref[...]) * _sigmoid(_dot(h, w2_ref[...]))


def _in_glu(h, w_in_b):
    n = h.shape[0]
    tn = 1024
    nj = CONV_CHANNELS // tn
    return pl.pallas_call(
        _glu_kernel,
        grid=(n // TM, nj),
        in_specs=[
            pl.BlockSpec((TM, D_MODEL), lambda i, j: (i, 0)),
            pl.BlockSpec((D_MODEL, tn), lambda i, j: (0, j)),
            pl.BlockSpec((D_MODEL, tn), lambda i, j: (0, j + nj)),
        ],
        out_specs=pl.BlockSpec((TM, tn), lambda i, j: (i, j)),
        out_shape=jax.ShapeDtypeStruct((n, CONV_CHANNELS), F32),
        compiler_params=_cparams(2),
        name="in_glu",
    )(h, w_in_b, w_in_b)


CONV_TM = 256
_CONV_SUBS = CONV_TM // CONV_T
_CONV_SEQ_TILES = SEQ // CONV_TM
_CONV_PROMPT_TILES = N_P // CONV_TM
_CONV_LANES = 512
_SHIFT_ROWS = (HALO // SUBLANES - 1) * SUBLANES + CONV_T


def _conv_kernel(cur_ref, prev_ref, hist_ref, w_ref, bdw_ref, g_ref, b_ref, o_ref, win_ref, conv_ref, shift_ref):
    i = pl.program_id(0)
    is_sample = i >= _CONV_PROMPT_TILES
    opens = i % _CONV_SEQ_TILES == 0
    base = HALO - (CONV_WIDTH - 1)
    for j in range(_CONV_SUBS):
        r0 = j * CONV_T
        before = jnp.where(opens, 0.0, prev_ref[...]) if j == 0 else cur_ref[r0 - HALO:r0, :]
        win_ref[0:HALO, :] = jnp.where(is_sample, hist_ref[j], before)
        win_ref[HALO:HALO + CONV_T, :] = cur_ref[r0:r0 + CONV_T, :]
        for r in range(1, SUBLANES):
            shift_ref[r - 1] = win_ref[r:r + _SHIFT_ROWS, :]
        for c in range(0, CONV_CHANNELS, _CONV_LANES):
            acc = jnp.zeros((CONV_T, _CONV_LANES), F32)
            for k in range(CONV_WIDTH):
                q, r = divmod(base + k, SUBLANES)
                lanes = slice(c, c + _CONV_LANES)
                rows = slice(q * SUBLANES, q * SUBLANES + CONV_T)
                src = win_ref[rows, lanes] if r == 0 else shift_ref[r - 1, rows, lanes]
                acc = acc + w_ref[k:k + 1, lanes] * src
            conv_ref[:, c:c + _CONV_LANES] = acc + bdw_ref[:, c:c + _CONV_LANES]
        y = conv_ref[...]
        yc = y - jnp.mean(y, axis=-1, keepdims=True)
        var = jnp.mean(yc * yc, axis=-1, keepdims=True)
        z = yc * lax.rsqrt(var + EPS) * g_ref[...] + b_ref[...]
        o_ref[r0:r0 + CONV_T, :] = (z * _sigmoid(z)).astype(BF16)


def _conv_module(glu, hist_s, w_dw, b_dw, g_cn, b_cn):
    n = glu.shape[0]
    halo_per_tile = CONV_TM // HALO
    n_sample_tiles = N_S // CONV_TM
    const = lambda i: (0, 0)
    return pl.pallas_call(
        _conv_kernel,
        grid=(n // CONV_TM,),
        in_specs=[
            pl.BlockSpec((CONV_TM, CONV_CHANNELS), lambda i: (i, 0)),
            pl.BlockSpec((HALO, CONV_CHANNELS), lambda i: (jnp.maximum(i * halo_per_tile - 1, 0), 0)),
            pl.BlockSpec((_CONV_SUBS, HALO, CONV_CHANNELS),
                         lambda i: (jnp.clip(i - _CONV_PROMPT_TILES, 0, n_sample_tiles - 1), 0, 0)),
            pl.BlockSpec((CONV_WIDTH, CONV_CHANNELS), const),
            pl.BlockSpec((1, CONV_CHANNELS), const),
            pl.BlockSpec((1, CONV_CHANNELS), const),
            pl.BlockSpec((1, CONV_CHANNELS), const),
        ],
        out_specs=pl.BlockSpec((CONV_TM, CONV_CHANNELS), lambda i: (i, 0)),
        out_shape=jax.ShapeDtypeStruct((n, CONV_CHANNELS), BF16),
        scratch_shapes=[pltpu.VMEM((HALO + CONV_T, CONV_CHANNELS), F32),
                        pltpu.VMEM((CONV_T, CONV_CHANNELS), F32),
                        pltpu.VMEM((SUBLANES - 1, _SHIFT_ROWS, CONV_CHANNELS), F32)],
        compiler_params=_cparams(1),
        name="conv_module",
    )(glu, glu, hist_s, w_dw, b_dw, g_cn, b_cn)


ATT_TM = 512
_TAB_PROMPT_TILES = N_P // ATT_TM
_TAB_SEQ_TILES = SEQ // ATT_TM


def _tab_idx_new(i):
    return jnp.where(i < _TAB_PROMPT_TILES, i % _TAB_SEQ_TILES, _TAB_SEQ_TILES)


def _rope_pair(u, c, s):
    return u * c + pltpu.roll(u, LANES // 2, 1) * s


_Q_SCALE = math.log2(math.e) / math.sqrt(QK_DIM)


def _q_heads_kernel(ql_ref, w_ref, g_ref, c_ref, s_ref, o_ref):
    ql = ql_ref[...]
    g = g_ref[...]
    c = c_ref[...]
    s = s_ref[...]
    for h in range(N_HEADS):
        qf = _dot(ql, w_ref[:, h * HEAD_PAD:(h + 1) * HEAD_PAD])
        ssq = jnp.sum(qf * qf, axis=-1, keepdims=True)
        qn = qf * (lax.rsqrt(ssq * (1.0 / QK_DIM) + EPS) * _Q_SCALE) * g
        o_ref[h, :, :NOPE_DIM] = qn[:, :NOPE_DIM].astype(BF16)
        o_ref[h, :, NOPE_DIM:] = _rope_pair(qn[:, NOPE_DIM:], c, s).astype(BF16)


def _q_heads(q_lat, w_q, g_q, cos_t, sin_t):
    n = q_lat.shape[0]
    return pl.pallas_call(
        _q_heads_kernel,
        grid=(n // ATT_TM,),
        in_specs=[
            pl.BlockSpec((ATT_TM, Q_LORA_RANK), lambda i: (i, 0)),
            pl.BlockSpec((Q_LORA_RANK, N_HEADS * HEAD_PAD), lambda i: (0, 0)),
            pl.BlockSpec((1, HEAD_PAD), lambda i: (0, 0)),
            pl.BlockSpec((ATT_TM, LANES), lambda i: (_tab_idx_new(i), 0)),
            pl.BlockSpec((ATT_TM, LANES), lambda i: (_tab_idx_new(i), 0)),
        ],
        out_specs=pl.BlockSpec((N_HEADS, ATT_TM, HEAD_PAD), lambda i: (0, i, 0)),
        out_shape=jax.ShapeDtypeStruct((N_HEADS, n, HEAD_PAD), BF16),
        compiler_params=_cparams(1),
        name="q_heads",
    )(q_lat, w_q, g_q, cos_t, sin_t)


def _kv_heads_kernel(kvp_ref, kvs_ref, kr_ref, w_ref, gr_ref, c_ref, s_ref, k_ref, v_ref):
    kv = _stacked_rows(pl.program_id(0), N_P // ATT_TM, kvp_ref, kvs_ref).astype(BF16)
    u = kr_ref[...]
    ssq_r = jnp.sum(u * u, axis=-1, keepdims=True)
    krot = _rope_pair(u * gr_ref[...], c_ref[...], s_ref[...])
    for h in range(N_HEADS):
        z = _dot(kv, w_ref[:, h * HEAD_PAD:(h + 1) * HEAD_PAD])
        kn = z[:, :NOPE_DIM]
        ssq = jnp.sum(kn * kn, axis=-1, keepdims=True) + ssq_r
        scale = lax.rsqrt(ssq * (1.0 / QK_DIM) + EPS)
        k_ref[h, :, :NOPE_DIM] = (kn * scale).astype(BF16)
        k_ref[h, :, NOPE_DIM:] = (krot * scale).astype(BF16)
        v_ref[h] = z[:, NOPE_DIM:].astype(BF16)


def _kv_heads(kv_p, kv_s, kr_pad, w_kv, g_kn_rope, cos_t, sin_t, tab_idx, name):
    n = N_TOK
    npt = N_P // ATT_TM
    return pl.pallas_call(
        _kv_heads_kernel,
        grid=(n // ATT_TM,),
        in_specs=[
            pl.BlockSpec((ATT_TM, KV_LORA_RANK), lambda i: (jnp.minimum(i, npt - 1), 0)),
            pl.BlockSpec((ATT_TM, KV_LORA_RANK), lambda i: (jnp.maximum(i - npt, 0), 0)),
            pl.BlockSpec((ATT_TM, LANES), lambda i: (i, 0)),
            pl.BlockSpec((KV_LORA_RANK, N_HEADS * HEAD_PAD), lambda i: (0, 0)),
            pl.BlockSpec((1, LANES), lambda i: (0, 0)),
            pl.BlockSpec((ATT_TM, LANES), lambda i: (tab_idx(i), 0)),
            pl.BlockSpec((ATT_TM, LANES), lambda i: (tab_idx(i), 0)),
        ],
        out_specs=[
            pl.BlockSpec((N_HEADS, ATT_TM, HEAD_PAD), lambda i: (0, i, 0)),
            pl.BlockSpec((N_HEADS, ATT_TM, V_DIM), lambda i: (0, i, 0)),
        ],
        out_shape=[
            jax.ShapeDtypeStruct((N_HEADS, n, HEAD_PAD), BF16),
            jax.ShapeDtypeStruct((N_HEADS, n, V_DIM), BF16),
        ],
        compiler_params=_cparams(1),
        name=name,
    )(kv_p, kv_s, kr_pad, w_kv, g_kn_rope, cos_t, sin_t)


_TQ = 512
_TKB = 512
_HB = 4
_HBP = 4


def _flash_prompt_kernel(q_ref, k_ref, v_ref, o_ref, m_ref, l_ref, acc_ref):
    qi = pl.program_id(2)
    m_ref[...] = jnp.full(m_ref.shape, NEG_INF, F32)
    l_ref[...] = jnp.zeros(l_ref.shape, F32)
    acc_ref[...] = jnp.zeros(acc_ref.shape, F32)
    nlb = _TKB // LANES

    def step(ki, masked):
        start = pl.multiple_of(ki * _TKB, _TKB)
        scores = [lax.dot_general(q_ref[hh], k_ref[hh, pl.ds(start, _TKB), :], (((1,), (1,)), ((), ())),
                                  preferred_element_type=F32) for hh in range(_HBP)]
        probs = []
        for hh in range(_HBP):
            s = scores[hh]
            if masked:
                rc = lax.broadcasted_iota(I32, (_TQ, _TKB), 0) // CHUNK
                cc = lax.broadcasted_iota(I32, (_TQ, _TKB), 1) // CHUNK
                s = jnp.where(cc <= rc, s, NEG_INF)
            sb = [s[:, c * LANES:(c + 1) * LANES] for c in range(nlb)]
            bm = sb[0]
            for c in range(1, nlb):
                bm = jnp.maximum(bm, sb[c])
            m_prev = m_ref[hh]
            m_new = jnp.maximum(m_prev, jnp.max(bm, axis=-1, keepdims=True))
            alpha = jnp.exp2(m_prev - m_new)
            ps = [jnp.exp2(x - m_new) for x in sb]
            psum = ps[0]
            for c in range(1, nlb):
                psum = psum + ps[c]
            l_ref[hh] = alpha * l_ref[hh] + psum
            m_ref[hh] = m_new
            probs.append((alpha, jnp.concatenate(ps, axis=1).astype(BF16)))
        for hh in range(_HBP):
            alpha, p = probs[hh]
            acc_ref[hh] = alpha * acc_ref[hh] + _dot(p, v_ref[hh, pl.ds(start, _TKB), :])

    def body(kp, carry):
        step(2 * kp, False)
        step(2 * kp + 1, False)
        return carry

    lax.fori_loop(0, qi // 2, body, 0)

    @pl.when(qi % 2 == 1)
    def _():
        step(qi - 1, False)

    step(qi, True)
    for hh in range(_HBP):
        l = jnp.sum(l_ref[hh], axis=-1, keepdims=True)
        o_ref[:, hh * V_DIM:(hh + 1) * V_DIM] = (acc_ref[hh] / l).astype(BF16)


def _flash_prompt(q, k, v):
    nq = SEQ // _TQ
    return pl.pallas_call(
        _flash_prompt_kernel,
        grid=(BATCH, N_HEADS // _HBP, nq),
        in_specs=[
            pl.BlockSpec((_HBP, _TQ, HEAD_PAD), lambda b, h, i: (h, b * nq + i, 0)),
            pl.BlockSpec((_HBP, SEQ, HEAD_PAD), lambda b, h, i: (h, b, 0)),
            pl.BlockSpec((_HBP, SEQ, V_DIM), lambda b, h, i: (h, b, 0)),
        ],
        out_specs=pl.BlockSpec((_TQ, _HBP * V_DIM), lambda b, h, i: (b * nq + i, h)),
        out_shape=jax.ShapeDtypeStruct((N_TOK, N_HEADS * V_DIM), BF16),
        scratch_shapes=[pltpu.VMEM((_HBP, _TQ, LANES), F32), pltpu.VMEM((_HBP, _TQ, LANES), F32),
                        pltpu.VMEM((_HBP, _TQ, V_DIM), F32)],
        compiler_params=_cparams(3),
        name="flash_prompt",
    )(q, k, v)


_KC_ROWS = 512


def _flash_sample_kernel(prev_ref, q_ref, kv_ref, kr_ref, w_ref, gr_ref, c_ref, s_ref, kn_ref, vn_ref,
                         o_ref, kvb_ref, krot_ref, ssqr_ref, k_ref, v_ref):
    del prev_ref

    @pl.when(pl.program_id(1) == 0)
    def _():
        kvb_ref[...] = kv_ref[...].astype(BF16)
        u = kr_ref[...]
        ssqr_ref[...] = jnp.broadcast_to(jnp.sum(u * u, axis=-1, keepdims=True), ssqr_ref.shape)
        krot_ref[...] = _rope_pair(u * gr_ref[...], c_ref[...], s_ref[...])

    nt = (((1,), (1,)), ((), ()))
    for hh in range(_HB):
        w = w_ref[:, hh * HEAD_PAD:(hh + 1) * HEAD_PAD]
        for r in range(0, PAST_LEN, _KC_ROWS):
            rows = slice(r, r + _KC_ROWS)
            z = _dot(kvb_ref[rows, :], w)
            kn = z[:, :NOPE_DIM]
            ssq = jnp.sum(kn * kn, axis=-1, keepdims=True) + ssqr_ref[rows, :]
            scale = lax.rsqrt(ssq * (1.0 / QK_DIM) + EPS)
            k_ref[rows, :NOPE_DIM] = (kn * scale).astype(BF16)
            k_ref[rows, NOPE_DIM:] = (krot_ref[rows, :] * scale).astype(BF16)
            v_ref[rows, :] = z[:, NOPE_DIM:].astype(BF16)
        q = q_ref[hh]
        s1 = lax.dot_general(q, k_ref[...], nt, preferred_element_type=F32)
        s2 = lax.dot_general(q, kn_ref[hh], nt, preferred_element_type=F32)
        m = jnp.maximum(jnp.max(s1, axis=-1, keepdims=True), jnp.max(s2, axis=-1, keepdims=True))
        p1 = jnp.exp2(s1 - m)
        p2 = jnp.exp2(s2 - m)
        l = jnp.sum(p1, axis=-1, keepdims=True) + jnp.sum(p2, axis=-1, keepdims=True)
        o = _dot(p1.astype(BF16), v_ref[...]) + _dot(p2.astype(BF16), vn_ref[hh])
        o_ref[:, hh * V_DIM:(hh + 1) * V_DIM] = (o / l).astype(BF16)


def _flash_sample(attn, q, cache_kv, cache_kr_pad, w_kv, g_kn_rope, cos_t, sin_t, k_new, v_new):
    assert (PAST_LEN + DEC_SEQ - 1) // CHUNK <= PAST_LEN // CHUNK
    blk0 = N_P // DEC_SEQ
    new = lambda b, h: (h, blk0 + b, 0)
    const = lambda b, h: (0, 0)
    once = pl.Buffered(1)
    return pl.pallas_call(
        _flash_sample_kernel,
        grid=(DEC_BATCH, N_HEADS // _HB),
        in_specs=[
            pl.BlockSpec(memory_space=pl.ANY),
            pl.BlockSpec((_HB, DEC_SEQ, HEAD_PAD), new),
            pl.BlockSpec((PAST_LEN, KV_LORA_RANK), lambda b, h: (b, 0)),
            pl.BlockSpec((PAST_LEN, LANES), lambda b, h: (b, 0)),
            pl.BlockSpec((KV_LORA_RANK, _HB * HEAD_PAD), lambda b, h: (0, h)),
            pl.BlockSpec((1, LANES), const),
            pl.BlockSpec((PAST_LEN, LANES), const, pipeline_mode=once),
            pl.BlockSpec((PAST_LEN, LANES), const, pipeline_mode=once),
            pl.BlockSpec((_HB, DEC_SEQ, HEAD_PAD), new),
            pl.BlockSpec((_HB, DEC_SEQ, V_DIM), new),
        ],
        out_specs=pl.BlockSpec((DEC_SEQ, _HB * V_DIM), lambda b, h: (blk0 + b, h)),
        out_shape=jax.ShapeDtypeStruct((N_TOK, N_HEADS * V_DIM), BF16),
        scratch_shapes=[pltpu.VMEM((PAST_LEN, KV_LORA_RANK), BF16),
                        pltpu.VMEM((PAST_LEN, LANES), F32),
                        pltpu.VMEM((PAST_LEN, LANES), F32),
                        pltpu.VMEM((PAST_LEN, HEAD_PAD), BF16),
                        pltpu.VMEM((PAST_LEN, V_DIM), BF16)],
        input_output_aliases={0: 0},
        compiler_params=_cparams(2),
        name="flash_sample",
    )(attn, q, cache_kv, cache_kr_pad, w_kv, g_kn_rope, cos_t, sin_t, k_new, v_new)


def _merge_kernel(h_ref, c_ref, a_ref, wga_ref, wgb_ref, bga_ref, bgb_ref, wc_ref, wo_ref, o_ref):
    h = h_ref[...]
    ga = _sigmoid(_dot(h, wga_ref[...]) + bga_ref[...])
    gb = _sigmoid(_dot(h, wgb_ref[...]) + bgb_ref[...])
    mix = ga * _dot(c_ref[...], wc_ref[...]) + gb * _dot(a_ref[...], wo_ref[...])
    o_ref[...] = mix.astype(BF16)


def _merge(h, c_act, attn, w_gate, b_gate, w_conv_out, w_o):
    n = h.shape[0]
    tn = 512
    nj = D_MODEL // tn
    row = lambda i, j: (i, 0)
    return pl.pallas_call(
        _merge_kernel,
        grid=(n // TM, nj),
        in_specs=[
            pl.BlockSpec((TM, D_MODEL), row),
            pl.BlockSpec((TM, CONV_CHANNELS), row),
            pl.BlockSpec((TM, N_HEADS * V_DIM), row),
            pl.BlockSpec((D_MODEL, tn), lambda i, j: (0, j)),
            pl.BlockSpec((D_MODEL, tn), lambda i, j: (0, j + nj)),
            pl.BlockSpec((1, tn), lambda i, j: (0, j)),
            pl.BlockSpec((1, tn), lambda i, j: (0, j + nj)),
            pl.BlockSpec((CONV_CHANNELS, tn), lambda i, j: (0, j)),
            pl.BlockSpec((N_HEADS * V_DIM, tn), lambda i, j: (0, j)),
        ],
        out_specs=pl.BlockSpec((TM, tn), lambda i, j: (i, j)),
        out_shape=jax.ShapeDtypeStruct((n, D_MODEL), BF16),
        compiler_params=_cparams(2),
        name="merge",
    )(h, c_act, attn, w_gate, w_gate, b_gate, b_gate, w_conv_out, w_o)


def _split_bf16(x):
    hi = x.astype(BF16)
    lo = (x - hi.astype(F32)).astype(BF16)
    return hi, lo


_HALF = D_MODEL // 2


def _pack_bf16_pair(a, b):
    ua = lax.bitcast_convert_type(a.astype(BF16).astype(F32), U32)
    ub = lax.bitcast_convert_type(b.astype(BF16).astype(F32), U32)
    return lax.bitcast_convert_type(ua | (ub >> 16), F32)


def _unpack_bf16_pair(w):
    w = lax.bitcast_convert_type(w, U32)
    a = lax.bitcast_convert_type(w & jnp.uint32(0xFFFF0000), F32).astype(BF16)
    b = lax.bitcast_convert_type(w << 16, F32).astype(BF16)
    return a, b


def _out_router_kernel(n_prompt_tiles, mix_ref, xp_ref, xs_ref, w_ref, g_ref, wrh_ref, wrl_ref, br_ref,
                       x1_ref, hm_ref, idx_ref, gate_ref):
    x = _stacked_rows(pl.program_id(0), n_prompt_tiles, xp_ref, xs_ref)
    x1 = x + _dot(mix_ref[...], w_ref[...])
    x1_ref[...] = x1
    hn = x1 * lax.rsqrt(jnp.mean(x1 * x1, axis=-1, keepdims=True) + EPS) * g_ref[...]
    hm_ref[0] = _pack_bf16_pair(hn[:, :_HALF], hn[:, _HALF:])
    hm_ref[1] = jnp.zeros(hm_ref.shape[1:], F32)
    hh, hl = _split_bf16(hn)
    logits = _dot(hh, wrh_ref[...]) + (_dot(hh, wrl_ref[...]) + _dot(hl, wrh_ref[...])) + br_ref[...]
    lane = lax.broadcasted_iota(I32, logits.shape, 1).astype(F32)
    vals = []
    idx_out = jnp.zeros(logits.shape, F32)
    for k in range(TOP_K):
        m = jnp.max(logits, axis=-1, keepdims=True)
        sel = jnp.min(jnp.where(logits == m, lane, 1e9), axis=-1, keepdims=True)
        vals.append(m)
        idx_out = jnp.where(lane == float(k), sel, idx_out)
        logits = jnp.where(lane == sel, -jnp.inf, logits)
    exps = [jnp.exp(v - vals[0]) for v in vals]
    denom = exps[0] + exps[1] + exps[2] + exps[3]
    gate_out = jnp.zeros(idx_out.shape, F32)
    for k in range(TOP_K):
        gate_out = jnp.where(lane == float(k), exps[k] / denom, gate_out)
    idx_ref[...] = idx_out.astype(I32)
    gate_ref[...] = gate_out


def _out_router(mix, xp, xs, w_out, g_ffn, wr_hi, wr_lo, b_r):
    n = N_TOK
    tm = TM
    n_tiles = n // tm
    npt = N_P // tm
    const = lambda i: (0, 0)
    row = lambda i: (i, 0)
    once = pl.Buffered(1)
    return pl.pallas_call(
        functools.partial(_out_router_kernel, npt),
        grid=(n_tiles,),
        in_specs=[
            pl.BlockSpec((tm, D_MODEL), row),
            pl.BlockSpec((tm, D_MODEL), lambda i: (jnp.minimum(i, npt - 1), 0)),
            pl.BlockSpec((tm, D_MODEL), lambda i: (jnp.clip(i - npt, 0, N_S // tm - 1), 0)),
            pl.BlockSpec((D_MODEL, D_MODEL), const, pipeline_mode=once),
            pl.BlockSpec((1, D_MODEL), const),
            pl.BlockSpec((D_MODEL, LANES), const, pipeline_mode=once),
            pl.BlockSpec((D_MODEL, LANES), const, pipeline_mode=once),
            pl.BlockSpec((1, LANES), const),
        ],
        out_specs=[
            pl.BlockSpec((tm, D_MODEL), row),
            pl.BlockSpec((2, tm, _HALF), lambda i: (0, i, 0)),
            pl.BlockSpec((tm, LANES), row),
            pl.BlockSpec((tm, LANES), row),
        ],
        out_shape=[
            jax.ShapeDtypeStruct((n, D_MODEL), F32),
            jax.ShapeDtypeStruct((2, n, _HALF), F32),
            jax.ShapeDtypeStruct((n, LANES), I32),
            jax.ShapeDtypeStruct((n, LANES), F32),
        ],
        compiler_params=_cparams(1),
        name="out_router",
    )(mix, xp, xs, w_out, g_ffn, wr_hi, wr_lo, b_r)


_F_VALID, _F_FIRST, _F_NEXT, _F_GROUP0 = 1, 2, 4, 8


_P_E, _P_W, _P_N, _P_B, _P_BI, _P_NE, _P_NW, _P_FL, _P_SUBS = range(9)
MOE_SUB = 256


def _stream_weights(t, plan_ref, copies, cast):
    flags = plan_ref[_P_FL, t]

    @pl.when((flags & _F_FIRST) != 0)
    def _():
        cur = copies(plan_ref[_P_E, t], plan_ref[_P_W, t])

        @pl.when((flags & _F_GROUP0) != 0)
        def _():
            for c in cur:
                c.start()

        for c in cur:
            c.wait()
        cast()

        @pl.when((flags & _F_NEXT) != 0)
        def _():
            for c in copies(plan_ref[_P_NE, t], plan_ref[_P_NW, t]):
                c.start()


def _for_used_rows(valid, subs, rows_body):
    for n_sub in range(1, MOE_BLK // MOE_SUB + 1):
        @pl.when(jnp.logical_and(valid, subs == n_sub))
        def _(m=n_sub * MOE_SUB):
            rows_body(m)


def _moe_up_kernel(plan_ref, prev_ref, x_ref, w_hbm, bg_ref, bu_ref, o_ref, wbuf_ref, wgb_ref, wub_ref, sem_ref):
    del prev_ref
    t = pl.program_id(0)

    def copies(e, w):
        col = pl.multiple_of(w * _UP_TN, _UP_TN)
        return (pltpu.make_async_copy(w_hbm.at[e, :, pl.ds(col, _UP_TN)], wbuf_ref.at[0], sem_ref.at[0]),
                pltpu.make_async_copy(w_hbm.at[e, :, pl.ds(col + D_FF, _UP_TN)], wbuf_ref.at[1], sem_ref.at[1]))

    def cast():
        wgb_ref[...] = wbuf_ref[0].astype(BF16)
        wub_ref[...] = wbuf_ref[1].astype(BF16)

    _stream_weights(t, plan_ref, copies, cast)
    valid = (plan_ref[_P_FL, t] & _F_VALID) != 0

    def rows_body(m):
        xa, xb = _unpack_bf16_pair(x_ref[:m, :])
        g = _dot(xa, wgb_ref[:_HALF, :]) + _dot(xb, wgb_ref[_HALF:, :]) + bg_ref[0]
        u = _dot(xa, wub_ref[:_HALF, :]) + _dot(xb, wub_ref[_HALF:, :]) + bu_ref[0]
        g = jnp.minimum(g, SWIGLU_LIMIT)
        u = jnp.clip(u, -SWIGLU_LIMIT, SWIGLU_LIMIT)
        o_ref[:m, :] = ((u + 1.0) * (g * _sigmoid(SWIGLU_ALPHA * g))).astype(BF16)
        if m < MOE_BLK:
            o_ref[m:, :] = jnp.zeros((MOE_BLK - m, o_ref.shape[1]), BF16)

    _for_used_rows(valid, plan_ref[_P_SUBS, t], rows_body)

    @pl.when(jnp.logical_not(valid))
    def _():
        o_ref[...] = jnp.zeros(o_ref.shape, BF16)


_UP_TN = 1024
_UP_TILES = D_FF // _UP_TN
_DN_TN = 2048
_DN_TILES = D_MODEL // _DN_TN
MOE_CHUNKS = 4
_CHUNK_BLKS = MOE_MAX_BLKS // MOE_CHUNKS


def _moe_up(plan, act_prev, xs, w_gu, b_gu, chunk):
    steps = plan.shape[1]
    blk0 = chunk * _CHUNK_BLKS
    bspec = lambda off: pl.BlockSpec((1, 1, _UP_TN), lambda t, p: (p[_P_E, t], 0, p[_P_W, t] + off))
    aliases = {} if act_prev is None else {1: 0}
    prev = jnp.zeros((SUBLANES, LANES), BF16) if act_prev is None else act_prev
    return pl.pallas_call(
        _moe_up_kernel,
        grid_spec=pltpu.PrefetchScalarGridSpec(
            num_scalar_prefetch=1,
            grid=(steps,),
            in_specs=[
                pl.BlockSpec(memory_space=pl.ANY),
                pl.BlockSpec((MOE_BLK, _HALF), lambda t, p: (p[_P_BI, t], 0)),
                pl.BlockSpec(memory_space=pl.ANY),
                bspec(0), bspec(_UP_TILES),
            ],
            out_specs=pl.BlockSpec((MOE_BLK, _UP_TN),
                                   lambda t, p: (blk0 + p[_P_B, t], p[_P_N, t])),
            scratch_shapes=[pltpu.VMEM((2, D_MODEL, _UP_TN), F32),
                            pltpu.VMEM((D_MODEL, _UP_TN), BF16), pltpu.VMEM((D_MODEL, _UP_TN), BF16),
                            pltpu.SemaphoreType.DMA((2,))],
        ),
        out_shape=jax.ShapeDtypeStruct((MOE_ROWS, D_FF), BF16),
        input_output_aliases=aliases,
        compiler_params=_cparams(1),
        name=f"moe_up_{chunk}",
    )(plan, prev, xs, w_gu, b_gu, b_gu)


_DN_HALF = _DN_TN // 2


def _moe_down_kernel(plan_ref, a_ref, w_hbm, b_ref, o_ref, wbuf_ref, wb_ref, sem_ref):
    t = pl.program_id(0)

    def copies(e, w):
        col = pl.multiple_of(w * _DN_TN, _DN_TN)
        return (pltpu.make_async_copy(w_hbm.at[e, :, pl.ds(col, _DN_TN)], wbuf_ref, sem_ref.at[0]),)

    def cast():
        wb_ref[...] = wbuf_ref[...].astype(BF16)

    _stream_weights(t, plan_ref, copies, cast)
    valid = (plan_ref[_P_FL, t] & _F_VALID) != 0

    def rows_body(m):
        y = _dot(a_ref[:m, :], wb_ref[...]) + b_ref[0]
        o_ref[:m, :] = _pack_bf16_pair(y[:, :_DN_HALF], y[:, _DN_HALF:])
        if m < MOE_BLK:
            o_ref[m:, :] = jnp.zeros((MOE_BLK - m, o_ref.shape[1]), F32)

    _for_used_rows(valid, plan_ref[_P_SUBS, t], rows_body)

    @pl.when(jnp.logical_not(valid))
    def _():
        o_ref[...] = jnp.zeros(o_ref.shape, F32)


def _moe_down(plan, act, w_dn, b_dn):
    steps = plan.shape[1]
    return pl.pallas_call(
        _moe_down_kernel,
        grid_spec=pltpu.PrefetchScalarGridSpec(
            num_scalar_prefetch=1,
            grid=(steps,),
            in_specs=[
                pl.BlockSpec((MOE_BLK, D_FF), lambda t, p: (p[_P_BI, t], 0)),
                pl.BlockSpec(memory_space=pl.ANY),
                pl.BlockSpec((1, 1, _DN_TN), lambda t, p: (p[_P_E, t], 0, p[_P_W, t])),
            ],
            out_specs=pl.BlockSpec((MOE_BLK, _DN_HALF), lambda t, p: (p[_P_B, t], p[_P_N, t])),
            scratch_shapes=[pltpu.VMEM((D_FF, _DN_TN), F32), pltpu.VMEM((D_FF, _DN_TN), BF16),
                            pltpu.SemaphoreType.DMA((1,))],
        ),
        out_shape=jax.ShapeDtypeStruct((MOE_ROWS, _HALF), F32),
        compiler_params=_cparams(1),
        name="moe_down",
    )(plan, act, w_dn, b_dn)


def _moe_dispatch(top_idx):
    n_asg = N_TOK * TOP_K
    flat_e = top_idx.reshape(-1)
    onehot = (flat_e[:, None] == jnp.arange(N_EXPERTS, dtype=I32)[None, :]).astype(I32)
    csum = jnp.cumsum(onehot, axis=0)
    counts = csum[-1]
    rank = jnp.sum(csum * onehot, axis=1) - 1
    nblk = (counts + MOE_BLK - 1) // MOE_BLK
    blk_start = jnp.cumsum(nblk) - nblk
    dest = jnp.sum(onehot * blk_start[None, :], axis=1) * MOE_BLK + rank
    pad_src = jnp.arange(MOE_ROWS, dtype=I32) % N_TOK
    row_tok = pad_src.at[dest].set(jnp.arange(n_asg, dtype=I32) // TOP_K,
                                   mode="promise_in_bounds", unique_indices=True)
    return dest, row_tok, counts, nblk, blk_start


def _moe_steps(counts, nblk, blk_start, n_tiles, blk_lo, n_blks):
    t_max = n_tiles * n_blks
    lo = jnp.clip(blk_start, blk_lo, blk_lo + n_blks)
    hi = jnp.clip(blk_start + nblk, blk_lo, blk_lo + n_blks)
    nb_e = hi - lo
    per_e = nb_e * n_tiles
    s_end = jnp.cumsum(per_e)
    total = s_end[-1]
    t = jnp.arange(t_max, dtype=I32)
    tc = jnp.clip(t, 0, jnp.maximum(total - 1, 0))
    e = jnp.minimum(jnp.sum((s_end[None, :] <= tc[:, None]).astype(I32), axis=1), N_EXPERTS - 1)
    sel = (e[:, None] == jnp.arange(N_EXPERTS, dtype=I32)[None, :]).astype(I32)
    pick = lambda v: jnp.sum(sel * v[None, :], axis=1)
    local = tc - pick(s_end - per_e)
    nb = jnp.maximum(pick(nb_e), 1)
    w_tile = jnp.clip(local // nb, 0, n_tiles - 1)
    r = local % nb
    valid = t < total
    first = jnp.logical_and(valid, r == 0)
    fill = t - total
    blk = jnp.where(valid, pick(lo) - blk_lo + r, total // n_tiles + fill // n_tiles)
    rows_used = pick(counts) - (pick(lo) + r - pick(blk_start)) * MOE_BLK
    subs = jnp.clip((rows_used + MOE_SUB - 1) // MOE_SUB, 1, MOE_BLK // MOE_SUB)
    o_tile = jnp.where(valid, w_tile, fill % n_tiles)
    blk = jnp.clip(blk, 0, n_blks - 1)
    blk_in = jnp.where(valid, blk, jnp.maximum(total // n_tiles - 1, 0))
    ids = jnp.arange(N_EXPERTS, dtype=I32)
    owners = jnp.where(nb_e > 0, ids, N_EXPERTS)
    later = jnp.flip(lax.cummin(jnp.flip(owners)))
    next_owner = pick(jnp.concatenate([later[1:], jnp.full((1,), N_EXPERTS, I32)]))
    last_tile = w_tile == n_tiles - 1
    next_e = jnp.where(last_tile, next_owner, e)
    next_w = jnp.where(last_tile, 0, w_tile + 1)
    has_next = jnp.logical_and(first, next_e < N_EXPERTS)
    group = jnp.cumsum(first.astype(I32)) - 1
    flags = (valid * _F_VALID + first * _F_FIRST + has_next * _F_NEXT
             + jnp.logical_and(first, group == 0) * _F_GROUP0)
    rows = {_P_E: e, _P_W: w_tile, _P_N: o_tile, _P_B: blk, _P_BI: blk_in,
            _P_NE: jnp.minimum(next_e, N_EXPERTS - 1), _P_NW: next_w, _P_FL: flags, _P_SUBS: subs}
    return jnp.stack([rows[k].astype(I32) for k in range(len(rows))])


def _moe_plans(counts, nblk, blk_start, n_tiles, n_chunks, n_blks):
    los = jnp.arange(n_chunks, dtype=I32) * n_blks
    return jax.vmap(lambda lo: _moe_steps(counts, nblk, blk_start, n_tiles, lo, n_blks))(los)


_FIN_TM = 256
_FIN_TN = 512
FIN_CHUNKS = 4


def _unpack_expert_rows(words):
    u = lax.bitcast_convert_type(words, U32)
    hi = lax.bitcast_convert_type(u & jnp.uint32(0xFFFF0000), F32)
    lo = lax.bitcast_convert_type(u << 16, F32)
    parts = []
    for n in range(_DN_TILES):
        cols = slice(n * _DN_HALF, (n + 1) * _DN_HALF)
        parts += [hi[:, cols], lo[:, cols]]
    return jnp.concatenate(parts, axis=1)


def _final_kernel(prev_ref, x1_ref, y0_ref, y1_ref, y2_ref, y3_ref, gate_ref, g_ref, wg_ref, p_ref, wp_ref,
                  o_ref, x2_ref):
    del prev_ref
    gate = gate_ref[...]
    moe = (_unpack_expert_rows(y0_ref[0]) * gate[:, 0:1] + _unpack_expert_rows(y1_ref[0]) * gate[:, 1:2]
           + _unpack_expert_rows(y2_ref[0]) * gate[:, 2:3] + _unpack_expert_rows(y3_ref[0]) * gate[:, 3:4])
    x2 = x1_ref[...] + moe
    x2_ref[...] = x2
    hp = (x2 * lax.rsqrt(jnp.mean(x2 * x2, axis=-1, keepdims=True) + EPS) * g_ref[...]).astype(BF16)
    pb = p_ref[...].astype(BF16)
    for c in range(0, D_MODEL, _FIN_TN):
        cols = slice(c, c + _FIN_TN)
        emb = _dot(pb, wp_ref[:, cols])
        o_ref[:, cols] = x2_ref[:, cols] + _sigmoid(_dot(hp, wg_ref[:, cols])) * emb


def _final(out_prev, x1, y4, gate, g_ple, w_ple_gate, p, w_ple, tok0, out0, n, n_out, name):
    t0 = tok0 // _FIN_TM
    o0 = out0 // _FIN_TM
    pt0 = out0 // _FIN_TM
    const = lambda i: (0, 0)
    yspec = lambda k: pl.BlockSpec((1, _FIN_TM, _HALF), lambda i: (k, i, 0))
    once = pl.Buffered(1)
    aliases = {} if out_prev is None else {0: 0}
    prev = jnp.zeros((SUBLANES, LANES), F32) if out_prev is None else out_prev
    return pl.pallas_call(
        _final_kernel,
        grid=(n // _FIN_TM,),
        in_specs=[
            pl.BlockSpec(memory_space=pl.ANY),
            pl.BlockSpec((_FIN_TM, D_MODEL), lambda i: (t0 + i, 0)),
            yspec(0), yspec(1), yspec(2), yspec(3),
            pl.BlockSpec((_FIN_TM, LANES), lambda i: (t0 + i, 0)),
            pl.BlockSpec((1, D_MODEL), const),
            pl.BlockSpec((D_MODEL, D_MODEL), const, pipeline_mode=once),
            pl.BlockSpec((_FIN_TM, PLE_DIM), lambda i: (pt0 + i, 0)),
            pl.BlockSpec((PLE_DIM, D_MODEL), const, pipeline_mode=once),
        ],
        out_specs=pl.BlockSpec((_FIN_TM, D_MODEL), lambda i: (o0 + i, 0)),
        out_shape=jax.ShapeDtypeStruct((n_out, D_MODEL), F32),
        scratch_shapes=[pltpu.VMEM((_FIN_TM, D_MODEL), F32)],
        input_output_aliases=aliases,
        compiler_params=_cparams(1),
        name=name,
    )(prev, x1, y4, y4, y4, y4, gate, g_ple, w_ple_gate, p, w_ple)


def _rope_layout(x):
    half = ROPE_DIM // 2
    z = jnp.zeros(x.shape[:-1] + (half,), x.dtype)
    return jnp.concatenate([x[..., :half], z, x[..., half:], z], axis=-1)


def _rope_tables():
    half = ROPE_DIM // 2
    inv_freq = ROPE_THETA ** (-jnp.arange(half, dtype=F32) / half)
    pos = jnp.arange(PAST_LEN + DEC_SEQ, dtype=I32)
    ang = pos.astype(F32)[:, None] * inv_freq[None, :]
    cos, sin = jnp.cos(ang), jnp.sin(ang)
    z = jnp.zeros_like(cos)
    c = jnp.concatenate([cos, z, cos, z], axis=-1)
    s = jnp.concatenate([-sin, z, sin, z], axis=-1)
    rep = ATT_TM // DEC_SEQ
    return (jnp.concatenate([c[:SEQ], jnp.tile(c[PAST_LEN:], (rep, 1))], axis=0),
            jnp.concatenate([s[:SEQ], jnp.tile(s[PAST_LEN:], (rep, 1))], axis=0))


def _layer(xp, xs, p_prompt, p_sample, cache_kv, cache_kr, state_conv,
           g_mix, w_in, b_gate, w_dw, b_dw, g_cn, b_cn, w_conv_out,
           g_qa, g_kva, w_qb, w_kb, w_vb, g_qn, g_kn, w_o, w_out,
           g_ffn, w_router, b_router, w_gu, b_gu, w_dn, b_dn,
           g_ple, w_ple_gate, w_ple):
    assert SEQ == PAST_LEN
    row = lambda v: v.reshape(1, -1)
    w_in_b = w_in.astype(BF16)
    w_mid = jnp.concatenate([w_in_b[:, O_U:O_KV], _rope_layout(w_in_b[:, O_KV:O_KR])], axis=1)
    w_gate = w_in_b[:, O_KR:]

    h, q_lat, kv_p, kv_s, kr_pad = _in_mid(xp, xs, row(g_mix), w_mid, row(g_qa), row(g_kva))
    half = ROPE_DIM // 2
    kr_new = jnp.concatenate([kr_pad[:, :half], kr_pad[:, 2 * half:3 * half]], axis=1)
    glu = _in_glu(h, w_in_b)

    hist_s = jnp.pad(state_conv, ((0, 0), (HALO - (CONV_WIDTH - 1), 0), (0, 0)))
    c_act = _conv_module(glu, hist_s, w_dw, row(b_dw), row(g_cn), row(b_cn))

    cos_t, sin_t = _rope_tables()
    w_q = jnp.concatenate([w_qb[..., :NOPE_DIM], _rope_layout(w_qb[..., NOPE_DIM:])], axis=-1)
    w_q = w_q.reshape(Q_LORA_RANK, N_HEADS * HEAD_PAD).astype(BF16)
    g_q = jnp.concatenate([g_qn[:NOPE_DIM] * g_kn[:NOPE_DIM], _rope_layout(g_qn[NOPE_DIM:])]).reshape(1, HEAD_PAD)
    q = _q_heads(q_lat, w_q, g_q, cos_t, sin_t)

    w_kv = jnp.concatenate([w_kb, w_vb], axis=-1).reshape(KV_LORA_RANK, N_HEADS * HEAD_PAD).astype(BF16)
    g_kn_rope = _rope_layout(g_kn[NOPE_DIM:]).reshape(1, LANES)
    k_new, v_new = _kv_heads(kv_p, kv_s, kr_pad, w_kv, g_kn_rope, cos_t, sin_t, _tab_idx_new, "kv_heads_new")
    attn = _flash_prompt(q, k_new, v_new)
    attn = _flash_sample(attn, q, cache_kv.reshape(DEC_BATCH * PAST_LEN, KV_LORA_RANK),
                         _rope_layout(cache_kr).reshape(DEC_BATCH * PAST_LEN, LANES),
                         w_kv, g_kn_rope, cos_t, sin_t, k_new, v_new)

    mix = _merge(h, c_act, attn, w_gate, row(b_gate), w_conv_out.astype(BF16), w_o.astype(BF16))

    wr = jnp.pad(w_router, ((0, 0), (0, LANES - N_EXPERTS)))
    wr_hi, wr_lo = _split_bf16(wr)
    b_r = jnp.concatenate([b_router, jnp.full((LANES - N_EXPERTS,), -jnp.inf, F32)]).reshape(1, LANES)
    x1, hm, idx_pad, gate_pad = _out_router(mix, xp, xs, w_out.astype(BF16), row(g_ffn), wr_hi, wr_lo, b_r)
    hm = hm.reshape(2 * N_TOK, _HALF)

    top_idx = idx_pad[:, :TOP_K]
    dest, row_tok, counts, nblk, blk_start = _moe_dispatch(top_idx)
    b_gu3 = b_gu.reshape(N_EXPERTS, 1, 2 * D_FF)
    chunk_rows = _CHUNK_BLKS * MOE_BLK
    up_plans = _moe_plans(counts, nblk, blk_start, _UP_TILES, MOE_CHUNKS, _CHUNK_BLKS)
    down_plan = _moe_plans(counts, nblk, blk_start, _DN_TILES, 1, MOE_MAX_BLKS)[0]
    act = None
    for c in range(MOE_CHUNKS):
        xs = hm.at[row_tok[c * chunk_rows:(c + 1) * chunk_rows]].get(mode="promise_in_bounds")
        act = _moe_up(up_plans[c], act, xs, w_gu, b_gu3, c)
    ys = _moe_down(down_plan, act, w_dn, b_dn.reshape(N_EXPERTS, 1, D_MODEL))

    dest_t = dest.reshape(N_TOK, TOP_K).T
    fin = (row(g_ple), w_ple_gate.astype(BF16))
    w_ple_b = w_ple.astype(BF16)
    n_c = N_P // FIN_CHUNKS
    out_p = None
    for c in range(FIN_CHUNKS):
        y4 = ys.at[dest_t[:, c * n_c:(c + 1) * n_c]].get(mode="promise_in_bounds")
        out_p = _final(out_p, x1, y4, gate_pad, *fin, p_prompt, w_ple_b, c * n_c, c * n_c, n_c, N_P,
                       f"final_prompt_{c}")
    y4 = ys.at[dest_t[:, N_P:]].get(mode="promise_in_bounds")
    out_s = _final(None, x1, y4, gate_pad, *fin, p_sample, w_ple_b, N_P, 0, N_S, N_S, "final_sample")
    return out_p, out_s, kv_p, kv_s, kr_new, glu


def kernel(x_prompt, x_sample, cache_kv_latent, cache_k_rope, state_conv, p_prompt, p_sample, g_mix, w_in, b_gate, w_dw, b_dw, g_cn, b_cn, w_conv_out, g_qa, g_kva, w_qb, w_kb, w_vb, g_qn, g_kn, w_o, w_out, g_ffn, w_router, b_router, w_gu, b_gu, w_dn, b_dn, g_ple, w_ple_gate, w_ple):
    assert g_mix.shape[0] == 1
    out_p, out_s, kv_p, kv_s, kr_new, glu = _layer(
        x_prompt.reshape(N_P, D_MODEL), x_sample.reshape(N_S, D_MODEL),
        p_prompt[0].reshape(N_P, PLE_DIM), p_sample[0].reshape(N_S, PLE_DIM),
        cache_kv_latent[0], cache_k_rope[0], state_conv[0],
        g_mix[0], w_in[0], b_gate[0], w_dw[0], b_dw[0], g_cn[0], b_cn[0], w_conv_out[0],
        g_qa[0], g_kva[0], w_qb[0], w_kb[0], w_vb[0], g_qn[0], g_kn[0], w_o[0], w_out[0],
        g_ffn[0], w_router[0], b_router[0], w_gu[0], b_gu[0], w_dn[0], b_dn[0],
        g_ple[0], w_ple_gate[0], w_ple[0])
    tail = CONV_WIDTH - 1
    conv_p = jnp.stack([glu[(b + 1) * SEQ - tail:(b + 1) * SEQ] for b in range(BATCH)])
    conv_s = glu[N_P:].reshape(DEC_BATCH, DEC_SEQ, CONV_CHANNELS)[:, DEC_SEQ - tail:]
    return (out_p.reshape(BATCH, SEQ, D_MODEL),
            out_s.reshape(DEC_BATCH, DEC_SEQ, D_MODEL),
            kv_p.reshape(1, BATCH, SEQ, KV_LORA_RANK),
            kr_new[:N_P].reshape(1, BATCH, SEQ, ROPE_DIM),
            conv_p[None],
            kv_s.reshape(1, DEC_BATCH, DEC_SEQ, KV_LORA_RANK),
            kr_new[N_P:].reshape(1, DEC_BATCH, DEC_SEQ, ROPE_DIM),
            conv_s[None])
```

```python
import functools
import math

import jax
import jax.numpy as jnp
from jax import lax
from jax.experimental import pallas as pl
from jax.experimental.pallas import tpu as pltpu

F32 = jnp.float32
BF16 = jnp.bfloat16
I32 = jnp.int32
U32 = jnp.uint32

D_MODEL = 2048
BATCH = 2
SEQ = 4096
DEC_BATCH = 8
DEC_SEQ = 64
PAST_LEN = 4096
CHUNK = 64
CONV_CHANNELS = D_MODEL
CONV_WIDTH = 31
N_HEADS = 16
Q_LORA_RANK = 512
KV_LORA_RANK = 512
NOPE_DIM = 128
ROPE_DIM = 64
QK_DIM = NOPE_DIM + ROPE_DIM
V_DIM = 128
ROPE_THETA = 10000.0
N_EXPERTS = 32
TOP_K = 4
D_FF = D_MODEL
SWIGLU_ALPHA = 1.702
SWIGLU_LIMIT = 7.0
PLE_DIM = 256
EPS = 1e-6
NEG_INF = -1e30

N_P = BATCH * SEQ
N_S = DEC_BATCH * DEC_SEQ
N_TOK = N_P + N_S
O_U = 2 * CONV_CHANNELS
O_Q = O_U + Q_LORA_RANK
O_KV = O_Q + KV_LORA_RANK
O_KR = O_KV + ROPE_DIM
LANES = 128
SUBLANES = 8
MID_W = Q_LORA_RANK + KV_LORA_RANK + LANES
HEAD_PAD = NOPE_DIM + LANES

TM = 512
CONV_T = 64
HALO = 32
MOE_BLK = 512
MOE_MAX_BLKS = (N_TOK * TOP_K) // MOE_BLK + N_EXPERTS
MOE_ROWS = MOE_MAX_BLKS * MOE_BLK
VMEM_LIMIT = 48 * 1024 * 1024
assert 2 * ROPE_DIM == LANES and NOPE_DIM == LANES and V_DIM == LANES and SEQ == PAST_LEN


def _cparams(n_axes):
    return pltpu.CompilerParams(dimension_semantics=("arbitrary",) * n_axes,
                                vmem_limit_bytes=VMEM_LIMIT)


def _sigmoid(x):
    return 1.0 / (1.0 + jnp.exp(-x))


def _dot(a, b):
    return jnp.dot(a, b, preferred_element_type=F32)


def _stacked_rows(i, n_prompt_tiles, xp_ref, xs_ref):
    return jnp.where(i < n_prompt_tiles, xp_ref[...], xs_ref[...])


def _in_mid_kernel(xp_ref, xs_ref, g_ref, w_ref, gqa_ref, gkva_ref, h_ref, q_ref, kvp_ref, kvs_ref, kr_ref):
    i = pl.program_id(0)
    x = _stacked_rows(i, N_P // TM, xp_ref, xs_ref)
    h = x * lax.rsqrt(jnp.mean(x * x, axis=-1, keepdims=True) + EPS) * g_ref[...]
    hb = h.astype(BF16)
    h_ref[...] = hb
    z = _dot(hb, w_ref[...])
    ql = z[:, :Q_LORA_RANK]
    kvl = z[:, Q_LORA_RANK:Q_LORA_RANK + KV_LORA_RANK]
    qn = ql * lax.rsqrt(jnp.mean(ql * ql, axis=-1, keepdims=True) + EPS) * gqa_ref[...]
    q_ref[...] = qn.astype(BF16)
    kv = kvl * lax.rsqrt(jnp.mean(kvl * kvl, axis=-1, keepdims=True) + EPS) * gkva_ref[...]
    kr_ref[...] = z[:, Q_LORA_RANK + KV_LORA_RANK:]

    @pl.when(i < N_P // TM)
    def _():
        kvp_ref[...] = kv

    @pl.when(i >= N_P // TM)
    def _():
        kvs_ref[...] = kv


def _in_mid(xp, xs, g_mix, w_mid, g_qa, g_kva):
    n = N_TOK
    npt = N_P // TM
    return pl.pallas_call(
        _in_mid_kernel,
        grid=(n // TM,),
        in_specs=[
            pl.BlockSpec((TM, D_MODEL), lambda i: (jnp.minimum(i, npt - 1), 0)),
            pl.BlockSpec((TM, D_MODEL), lambda i: (jnp.maximum(i - npt, 0), 0)),
            pl.BlockSpec((1, D_MODEL), lambda i: (0, 0)),
            pl.BlockSpec((D_MODEL, MID_W), lambda i: (0, 0)),
            pl.BlockSpec((1, Q_LORA_RANK), lambda i: (0, 0)),
            pl.BlockSpec((1, KV_LORA_RANK), lambda i: (0, 0)),
        ],
        out_specs=[
            pl.BlockSpec((TM, D_MODEL), lambda i: (i, 0)),
            pl.BlockSpec((TM, Q_LORA_RANK), lambda i: (i, 0)),
            pl.BlockSpec((TM, KV_LORA_RANK), lambda i: (jnp.minimum(i, npt - 1), 0)),
            pl.BlockSpec((TM, KV_LORA_RANK), lambda i: (jnp.maximum(i - npt, 0), 0)),
            pl.BlockSpec((TM, LANES), lambda i: (i, 0)),
        ],
        out_shape=[
            jax.ShapeDtypeStruct((n, D_MODEL), BF16),
            jax.ShapeDtypeStruct((n, Q_LORA_RANK), BF16),
            jax.ShapeDtypeStruct((N_P, KV_LORA_RANK), F32),
            jax.ShapeDtypeStruct((N_S, KV_LORA_RANK), F32),
            jax.ShapeDtypeStruct((n, LANES), F32),
        ],
        compiler_params=_cparams(1),
        name="in_mid",
    )(xp, xs, g_mix, w_mid, g_qa, g_kva)


def _glu_kernel(h_ref, w1_ref, w2_ref, o_ref):
    h = h_ref[...]
    o_ref[...] = _dot(h, w1_ref[...]) * _sigmoid(_dot(h, w2_ref[...]))


def _in_glu(h, w_in_b):
    n = h.shape[0]
    tn = CONV_CHANNELS
    nj = CONV_CHANNELS // tn
    once = pl.Buffered(1)
    return pl.pallas_call(
        _glu_kernel,
        grid=(n // TM, nj),
        in_specs=[
            pl.BlockSpec((TM, D_MODEL), lambda i, j: (i, 0)),
            pl.BlockSpec((D_MODEL, tn), lambda i, j: (0, j), pipeline_mode=once),
            pl.BlockSpec((D_MODEL, tn), lambda i, j: (0, j + nj), pipeline_mode=once),
        ],
        out_specs=pl.BlockSpec((TM, tn), lambda i, j: (i, j)),
        out_shape=jax.ShapeDtypeStruct((n, CONV_CHANNELS), F32),
        compiler_params=_cparams(2),
        name="in_glu",
    )(h, w_in_b, w_in_b)


CONV_TM = 256
_CONV_SUBS = CONV_TM // CONV_T
_CONV_SEQ_TILES = SEQ // CONV_TM
_CONV_PROMPT_TILES = N_P // CONV_TM
_CONV_LANES = 512
_SHIFT_ROWS = (HALO // SUBLANES - 1) * SUBLANES + CONV_T


def _conv_kernel(cur_ref, prev_ref, hist_ref, w_ref, bdw_ref, g_ref, b_ref, o_ref, win_ref, conv_ref, shift_ref):
    i = pl.program_id(0)
    is_sample = i >= _CONV_PROMPT_TILES
    opens = i % _CONV_SEQ_TILES == 0
    base = HALO - (CONV_WIDTH - 1)
    for j in range(_CONV_SUBS):
        r0 = j * CONV_T
        before = jnp.where(opens, 0.0, prev_ref[...]) if j == 0 else cur_ref[r0 - HALO:r0, :]
        win_ref[0:HALO, :] = jnp.where(is_sample, hist_ref[j], before)
        win_ref[HALO:HALO + CONV_T, :] = cur_ref[r0:r0 + CONV_T, :]
        for r in range(1, SUBLANES):
            shift_ref[r - 1] = win_ref[r:r + _SHIFT_ROWS, :]
        for c in range(0, CONV_CHANNELS, _CONV_LANES):
            acc = jnp.zeros((CONV_T, _CONV_LANES), F32)
            for k in range(CONV_WIDTH):
                q, r = divmod(base + k, SUBLANES)
                lanes = slice(c, c + _CONV_LANES)
                rows = slice(q * SUBLANES, q * SUBLANES + CONV_T)
                src = win_ref[rows, lanes] if r == 0 else shift_ref[r - 1, rows, lanes]
                acc = acc + w_ref[k:k + 1, lanes] * src
            conv_ref[:, c:c + _CONV_LANES] = acc + bdw_ref[:, c:c + _CONV_LANES]
        y = conv_ref[...]
        yc = y - jnp.mean(y, axis=-1, keepdims=True)
        var = jnp.mean(yc * yc, axis=-1, keepdims=True)
        z = yc * lax.rsqrt(var + EPS) * g_ref[...] + b_ref[...]
        o_ref[r0:r0 + CONV_T, :] = (z * _sigmoid(z)).astype(BF16)


def _conv_module(glu, hist_s, w_dw, b_dw, g_cn, b_cn):
    n = glu.shape[0]
    halo_per_tile = CONV_TM // HALO
    n_sample_tiles = N_S // CONV_TM
    const = lambda i: (0, 0)
    return pl.pallas_call(
        _conv_kernel,
        grid=(n // CONV_TM,),
        in_specs=[
            pl.BlockSpec((CONV_TM, CONV_CHANNELS), lambda i: (i, 0)),
            pl.BlockSpec((HALO, CONV_CHANNELS), lambda i: (jnp.maximum(i * halo_per_tile - 1, 0), 0)),
            pl.BlockSpec((_CONV_SUBS, HALO, CONV_CHANNELS),
                         lambda i: (jnp.clip(i - _CONV_PROMPT_TILES, 0, n_sample_tiles - 1), 0, 0)),
            pl.BlockSpec((CONV_WIDTH, CONV_CHANNELS), const),
            pl.BlockSpec((1, CONV_CHANNELS), const),
            pl.BlockSpec((1, CONV_CHANNELS), const),
            pl.BlockSpec((1, CONV_CHANNELS), const),
        ],
        out_specs=pl.BlockSpec((CONV_TM, CONV_CHANNELS), lambda i: (i, 0)),
        out_shape=jax.ShapeDtypeStruct((n, CONV_CHANNELS), BF16),
        scratch_shapes=[pltpu.VMEM((HALO + CONV_T, CONV_CHANNELS), F32),
                        pltpu.VMEM((CONV_T, CONV_CHANNELS), F32),
                        pltpu.VMEM((SUBLANES - 1, _SHIFT_ROWS, CONV_CHANNELS), F32)],
        compiler_params=_cparams(1),
        name="conv_module",
    )(glu, glu, hist_s, w_dw, b_dw, g_cn, b_cn)


ATT_TM = 512
_TAB_PROMPT_TILES = N_P // ATT_TM
_TAB_SEQ_TILES = SEQ // ATT_TM


def _tab_idx_new(i):
    return jnp.where(i < _TAB_PROMPT_TILES, i % _TAB_SEQ_TILES, _TAB_SEQ_TILES)


def _rope_pair(u, c, s):
    return u * c + pltpu.roll(u, LANES // 2, 1) * s


_Q_SCALE = math.log2(math.e) / math.sqrt(QK_DIM)


def _q_heads_kernel(ql_ref, w_ref, g_ref, c_ref, s_ref, o_ref):
    ql = ql_ref[...]
    g = g_ref[...]
    c = c_ref[...]
    s = s_ref[...]
    for h in range(N_HEADS):
        qf = _dot(ql, w_ref[:, h * HEAD_PAD:(h + 1) * HEAD_PAD])
        ssq = jnp.sum(qf * qf, axis=-1, keepdims=True)
        qn = qf * (lax.rsqrt(ssq * (1.0 / QK_DIM) + EPS) * _Q_SCALE) * g
        o_ref[h, :, :NOPE_DIM] = qn[:, :NOPE_DIM].astype(BF16)
        o_ref[h, :, NOPE_DIM:] = _rope_pair(qn[:, NOPE_DIM:], c, s).astype(BF16)


def _q_heads(q_lat, w_q, g_q, cos_t, sin_t):
    n = q_lat.shape[0]
    return pl.pallas_call(
        _q_heads_kernel,
        grid=(n // ATT_TM,),
        in_specs=[
            pl.BlockSpec((ATT_TM, Q_LORA_RANK), lambda i: (i, 0)),
            pl.BlockSpec((Q_LORA_RANK, N_HEADS * HEAD_PAD), lambda i: (0, 0)),
            pl.BlockSpec((1, HEAD_PAD), lambda i: (0, 0)),
            pl.BlockSpec((ATT_TM, LANES), lambda i: (_tab_idx_new(i), 0)),
            pl.BlockSpec((ATT_TM, LANES), lambda i: (_tab_idx_new(i), 0)),
        ],
        out_specs=pl.BlockSpec((N_HEADS, ATT_TM, HEAD_PAD), lambda i: (0, i, 0)),
        out_shape=jax.ShapeDtypeStruct((N_HEADS, n, HEAD_PAD), BF16),
        compiler_params=_cparams(1),
        name="q_heads",
    )(q_lat, w_q, g_q, cos_t, sin_t)


def _kv_heads_kernel(kvp_ref, kvs_ref, kr_ref, w_ref, gr_ref, c_ref, s_ref, k_ref, v_ref):
    kv = _stacked_rows(pl.program_id(0), N_P // ATT_TM, kvp_ref, kvs_ref).astype(BF16)
    u = kr_ref[...]
    ssq_r = jnp.sum(u * u, axis=-1, keepdims=True)
    krot = _rope_pair(u * gr_ref[...], c_ref[...], s_ref[...])
    for h in range(N_HEADS):
        z = _dot(kv, w_ref[:, h * HEAD_PAD:(h + 1) * HEAD_PAD])
        kn = z[:, :NOPE_DIM]
        ssq = jnp.sum(kn * kn, axis=-1, keepdims=True) + ssq_r
        scale = lax.rsqrt(ssq * (1.0 / QK_DIM) + EPS)
        k_ref[h, :, :NOPE_DIM] = (kn * scale).astype(BF16)
        k_ref[h, :, NOPE_DIM:] = (krot * scale).astype(BF16)
        v_ref[h] = z[:, NOPE_DIM:].astype(BF16)


def _kv_heads(kv_p, kv_s, kr_pad, w_kv, g_kn_rope, cos_t, sin_t, tab_idx, name):
    n = N_TOK
    npt = N_P // ATT_TM
    return pl.pallas_call(
        _kv_heads_kernel,
        grid=(n // ATT_TM,),
        in_specs=[
            pl.BlockSpec((ATT_TM, KV_LORA_RANK), lambda i: (jnp.minimum(i, npt - 1), 0)),
            pl.BlockSpec((ATT_TM, KV_LORA_RANK), lambda i: (jnp.maximum(i - npt, 0), 0)),
            pl.BlockSpec((ATT_TM, LANES), lambda i: (i, 0)),
            pl.BlockSpec((KV_LORA_RANK, N_HEADS * HEAD_PAD), lambda i: (0, 0)),
            pl.BlockSpec((1, LANES), lambda i: (0, 0)),
            pl.BlockSpec((ATT_TM, LANES), lambda i: (tab_idx(i), 0)),
            pl.BlockSpec((ATT_TM, LANES), lambda i: (tab_idx(i), 0)),
        ],
        out_specs=[
            pl.BlockSpec((N_HEADS, ATT_TM, HEAD_PAD), lambda i: (0, i, 0)),
            pl.BlockSpec((N_HEADS, ATT_TM, V_DIM), lambda i: (0, i, 0)),
        ],
        out_shape=[
            jax.ShapeDtypeStruct((N_HEADS, n, HEAD_PAD), BF16),
            jax.ShapeDtypeStruct((N_HEADS, n, V_DIM), BF16),
        ],
        compiler_params=_cparams(1),
        name=name,
    )(kv_p, kv_s, kr_pad, w_kv, g_kn_rope, cos_t, sin_t)


_TQ = 512
_TKB = 512
_HB = 4
_HBP = 4


def _flash_prompt_kernel(q_ref, k_ref, v_ref, o_ref, m_ref, l_ref, acc_ref):
    qi = pl.program_id(2)
    m_ref[...] = jnp.full(m_ref.shape, NEG_INF, F32)
    l_ref[...] = jnp.zeros(l_ref.shape, F32)
    acc_ref[...] = jnp.zeros(acc_ref.shape, F32)
    nlb = _TKB // LANES

    def step(ki, masked):
        start = pl.multiple_of(ki * _TKB, _TKB)
        scores = [lax.dot_general(q_ref[hh], k_ref[hh, pl.ds(start, _TKB), :], (((1,), (1,)), ((), ())),
                                  preferred_element_type=F32) for hh in range(_HBP)]
        probs = []
        for hh in range(_HBP):
            s = scores[hh]
            if masked:
                rc = lax.broadcasted_iota(I32, (_TQ, _TKB), 0) // CHUNK
                cc = lax.broadcasted_iota(I32, (_TQ, _TKB), 1) // CHUNK
                s = jnp.where(cc <= rc, s, NEG_INF)
            sb = [s[:, c * LANES:(c + 1) * LANES] for c in range(nlb)]
            bm = sb[0]
            for c in range(1, nlb):
                bm = jnp.maximum(bm, sb[c])
            m_prev = m_ref[hh]
            m_new = jnp.maximum(m_prev, jnp.max(bm, axis=-1, keepdims=True))
            alpha = jnp.exp2(m_prev - m_new)
            ps = [jnp.exp2(x - m_new) for x in sb]
            psum = ps[0]
            for c in range(1, nlb):
                psum = psum + ps[c]
            l_ref[hh] = alpha * l_ref[hh] + psum
            m_ref[hh] = m_new
            probs.append((alpha, jnp.concatenate(ps, axis=1).astype(BF16)))
        for hh in range(_HBP):
            alpha, p = probs[hh]
            acc_ref[hh] = alpha * acc_ref[hh] + _dot(p, v_ref[hh, pl.ds(start, _TKB), :])

    def body(kp, carry):
        step(2 * kp, False)
        step(2 * kp + 1, False)
        return carry

    lax.fori_loop(0, qi // 2, body, 0)

    @pl.when(qi % 2 == 1)
    def _():
        step(qi - 1, False)

    step(qi, True)
    for hh in range(_HBP):
        l = jnp.sum(l_ref[hh], axis=-1, keepdims=True)
        o_ref[:, hh * V_DIM:(hh + 1) * V_DIM] = (acc_ref[hh] / l).astype(BF16)


def _flash_prompt(q, k, v):
    nq = SEQ // _TQ
    return pl.pallas_call(
        _flash_prompt_kernel,
        grid=(BATCH, N_HEADS // _HBP, nq),
        in_specs=[
            pl.BlockSpec((_HBP, _TQ, HEAD_PAD), lambda b, h, i: (h, b * nq + i, 0)),
            pl.BlockSpec((_HBP, SEQ, HEAD_PAD), lambda b, h, i: (h, b, 0)),
            pl.BlockSpec((_HBP, SEQ, V_DIM), lambda b, h, i: (h, b, 0)),
        ],
        out_specs=pl.BlockSpec((_TQ, _HBP * V_DIM), lambda b, h, i: (b * nq + i, h)),
        out_shape=jax.ShapeDtypeStruct((N_TOK, N_HEADS * V_DIM), BF16),
        scratch_shapes=[pltpu.VMEM((_HBP, _TQ, LANES), F32), pltpu.VMEM((_HBP, _TQ, LANES), F32),
                        pltpu.VMEM((_HBP, _TQ, V_DIM), F32)],
        compiler_params=_cparams(3),
        name="flash_prompt",
    )(q, k, v)


_KC_ROWS = 512


def _flash_sample_kernel(prev_ref, q_ref, kv_ref, kr_ref, w_ref, gr_ref, c_ref, s_ref, kn_ref, vn_ref,
                         o_ref, kvb_ref, krot_ref, ssqr_ref, k_ref, v_ref):
    del prev_ref

    @pl.when(pl.program_id(1) == 0)
    def _():
        kvb_ref[...] = kv_ref[...].astype(BF16)
        u = kr_ref[...]
        ssqr_ref[...] = jnp.broadcast_to(jnp.sum(u * u, axis=-1, keepdims=True), ssqr_ref.shape)
        krot_ref[...] = _rope_pair(u * gr_ref[...], c_ref[...], s_ref[...])

    nt = (((1,), (1,)), ((), ()))
    for hh in range(_HB):
        w = w_ref[:, hh * HEAD_PAD:(hh + 1) * HEAD_PAD]
        for r in range(0, PAST_LEN, _KC_ROWS):
            rows = slice(r, r + _KC_ROWS)
            z = _dot(kvb_ref[rows, :], w)
            kn = z[:, :NOPE_DIM]
            ssq = jnp.sum(kn * kn, axis=-1, keepdims=True) + ssqr_ref[rows, :]
            scale = lax.rsqrt(ssq * (1.0 / QK_DIM) + EPS)
            k_ref[rows, :NOPE_DIM] = (kn * scale).astype(BF16)
            k_ref[rows, NOPE_DIM:] = (krot_ref[rows, :] * scale).astype(BF16)
            v_ref[rows, :] = z[:, NOPE_DIM:].astype(BF16)
        q = q_ref[hh]
        s1 = lax.dot_general(q, k_ref[...], nt, preferred_element_type=F32)
        s2 = lax.dot_general(q, kn_ref[hh], nt, preferred_element_type=F32)
        m = jnp.maximum(jnp.max(s1, axis=-1, keepdims=True), jnp.max(s2, axis=-1, keepdims=True))
        p1 = jnp.exp2(s1 - m)
        p2 = jnp.exp2(s2 - m)
        l = jnp.sum(p1, axis=-1, keepdims=True) + jnp.sum(p2, axis=-1, keepdims=True)
        o = _dot(p1.astype(BF16), v_ref[...]) + _dot(p2.astype(BF16), vn_ref[hh])
        o_ref[:, hh * V_DIM:(hh + 1) * V_DIM] = (o / l).astype(BF16)


def _flash_sample(attn, q, cache_kv, cache_kr_pad, w_kv, g_kn_rope, cos_t, sin_t, k_new, v_new):
    assert (PAST_LEN + DEC_SEQ - 1) // CHUNK <= PAST_LEN // CHUNK
    blk0 = N_P // DEC_SEQ
    new = lambda b, h: (h, blk0 + b, 0)
    const = lambda b, h: (0, 0)
    once = pl.Buffered(1)
    return pl.pallas_call(
        _flash_sample_kernel,
        grid=(DEC_BATCH, N_HEADS // _HB),
        in_specs=[
            pl.BlockSpec(memory_space=pl.ANY),
            pl.BlockSpec((_HB, DEC_SEQ, HEAD_PAD), new),
            pl.BlockSpec((PAST_LEN, KV_LORA_RANK), lambda b, h: (b, 0)),
            pl.BlockSpec((PAST_LEN, LANES), lambda b, h: (b, 0)),
            pl.BlockSpec((KV_LORA_RANK, _HB * HEAD_PAD), lambda b, h: (0, h)),
            pl.BlockSpec((1, LANES), const),
            pl.BlockSpec((PAST_LEN, LANES), const, pipeline_mode=once),
            pl.BlockSpec((PAST_LEN, LANES), const, pipeline_mode=once),
            pl.BlockSpec((_HB, DEC_SEQ, HEAD_PAD), new),
            pl.BlockSpec((_HB, DEC_SEQ, V_DIM), new),
        ],
        out_specs=pl.BlockSpec((DEC_SEQ, _HB * V_DIM), lambda b, h: (blk0 + b, h)),
        out_shape=jax.ShapeDtypeStruct((N_TOK, N_HEADS * V_DIM), BF16),
        scratch_shapes=[pltpu.VMEM((PAST_LEN, KV_LORA_RANK), BF16),
                        pltpu.VMEM((PAST_LEN, LANES), F32),
                        pltpu.VMEM((PAST_LEN, LANES), F32),
                        pltpu.VMEM((PAST_LEN, HEAD_PAD), BF16),
                        pltpu.VMEM((PAST_LEN, V_DIM), BF16)],
        input_output_aliases={0: 0},
        compiler_params=_cparams(2),
        name="flash_sample",
    )(attn, q, cache_kv, cache_kr_pad, w_kv, g_kn_rope, cos_t, sin_t, k_new, v_new)


def _merge_kernel(h_ref, c_ref, a_ref, wga_ref, wgb_ref, bga_ref, bgb_ref, wc_ref, wo_ref, o_ref):
    h = h_ref[...]
    ga = _sigmoid(_dot(h, wga_ref[...]) + bga_ref[...])
    gb = _sigmoid(_dot(h, wgb_ref[...]) + bgb_ref[...])
    mix = ga * _dot(c_ref[...], wc_ref[...]) + gb * _dot(a_ref[...], wo_ref[...])
    o_ref[...] = mix.astype(BF16)


def _merge(h, c_act, attn, w_gate, b_gate, w_conv_out, w_o):
    n = h.shape[0]
    tn = 512
    nj = D_MODEL // tn
    row = lambda i, j: (i, 0)
    return pl.pallas_call(
        _merge_kernel,
        grid=(n // TM, nj),
        in_specs=[
            pl.BlockSpec((TM, D_MODEL), row),
            pl.BlockSpec((TM, CONV_CHANNELS), row),
            pl.BlockSpec((TM, N_HEADS * V_DIM), row),
            pl.BlockSpec((D_MODEL, tn), lambda i, j: (0, j)),
            pl.BlockSpec((D_MODEL, tn), lambda i, j: (0, j + nj)),
            pl.BlockSpec((1, tn), lambda i, j: (0, j)),
            pl.BlockSpec((1, tn), lambda i, j: (0, j + nj)),
            pl.BlockSpec((CONV_CHANNELS, tn), lambda i, j: (0, j)),
            pl.BlockSpec((N_HEADS * V_DIM, tn), lambda i, j: (0, j)),
        ],
        out_specs=pl.BlockSpec((TM, tn), lambda i, j: (i, j)),
        out_shape=jax.ShapeDtypeStruct((n, D_MODEL), BF16),
        compiler_params=_cparams(2),
        name="merge",
    )(h, c_act, attn, w_gate, w_gate, b_gate, b_gate, w_conv_out, w_o)


def _split_bf16(x):
    hi = x.astype(BF16)
    lo = (x - hi.astype(F32)).astype(BF16)
    return hi, lo


_HALF = D_MODEL // 2


def _pack_bf16_pair(a, b):
    ua = lax.bitcast_convert_type(a.astype(BF16).astype(F32), U32)
    ub = lax.bitcast_convert_type(b.astype(BF16).astype(F32), U32)
    return lax.bitcast_convert_type(ua | (ub >> 16), F32)


def _unpack_bf16_pair(w):
    w = lax.bitcast_convert_type(w, U32)
    a = lax.bitcast_convert_type(w & jnp.uint32(0xFFFF0000), F32).astype(BF16)
    b = lax.bitcast_convert_type(w << 16, F32).astype(BF16)
    return a, b


def _out_router_kernel(n_prompt_tiles, mix_ref, xp_ref, xs_ref, w_ref, g_ref, wrh_ref, wrl_ref, br_ref,
                       x1_ref, hm_ref, idx_ref, gate_ref):
    x = _stacked_rows(pl.program_id(0), n_prompt_tiles, xp_ref, xs_ref)
    x1 = x + _dot(mix_ref[...], w_ref[...])
    x1_ref[...] = x1
    hn = x1 * lax.rsqrt(jnp.mean(x1 * x1, axis=-1, keepdims=True) + EPS) * g_ref[...]
    hm_ref[0] = _pack_bf16_pair(hn[:, :_HALF], hn[:, _HALF:])
    hm_ref[1] = jnp.zeros(hm_ref.shape[1:], F32)
    hh, hl = _split_bf16(hn)
    logits = _dot(hh, wrh_ref[...]) + (_dot(hh, wrl_ref[...]) + _dot(hl, wrh_ref[...])) + br_ref[...]
    lane = lax.broadcasted_iota(I32, logits.shape, 1).astype(F32)
    vals = []
    idx_out = jnp.zeros(logits.shape, F32)
    for k in range(TOP_K):
        m = jnp.max(logits, axis=-1, keepdims=True)
        sel = jnp.min(jnp.where(logits == m, lane, 1e9), axis=-1, keepdims=True)
        vals.append(m)
        idx_out = jnp.where(lane == float(k), sel, idx_out)
        logits = jnp.where(lane == sel, -jnp.inf, logits)
    exps = [jnp.exp(v - vals[0]) for v in vals]
    denom = exps[0] + exps[1] + exps[2] + exps[3]
    gate_out = jnp.zeros(idx_out.shape, F32)
    for k in range(TOP_K):
        gate_out = jnp.where(lane == float(k), exps[k] / denom, gate_out)
    idx_ref[...] = idx_out.astype(I32)
    gate_ref[...] = gate_out


def _out_router(mix, xp, xs, w_out, g_ffn, wr_hi, wr_lo, b_r):
    n = N_TOK
    tm = TM
    n_tiles = n // tm
    npt = N_P // tm
    const = lambda i: (0, 0)
    row = lambda i: (i, 0)
    once = pl.Buffered(1)
    return pl.pallas_call(
        functools.partial(_out_router_kernel, npt),
        grid=(n_tiles,),
        in_specs=[
            pl.BlockSpec((tm, D_MODEL), row),
            pl.BlockSpec((tm, D_MODEL), lambda i: (jnp.minimum(i, npt - 1), 0)),
            pl.BlockSpec((tm, D_MODEL), lambda i: (jnp.clip(i - npt, 0, N_S // tm - 1), 0)),
            pl.BlockSpec((D_MODEL, D_MODEL), const, pipeline_mode=once),
            pl.BlockSpec((1, D_MODEL), const),
            pl.BlockSpec((D_MODEL, LANES), const, pipeline_mode=once),
            pl.BlockSpec((D_MODEL, LANES), const, pipeline_mode=once),
            pl.BlockSpec((1, LANES), const),
        ],
        out_specs=[
            pl.BlockSpec((tm, D_MODEL), row),
            pl.BlockSpec((2, tm, _HALF), lambda i: (0, i, 0)),
            pl.BlockSpec((tm, LANES), row),
            pl.BlockSpec((tm, LANES), row),
        ],
        out_shape=[
            jax.ShapeDtypeStruct((n, D_MODEL), F32),
            jax.ShapeDtypeStruct((2, n, _HALF), F32),
            jax.ShapeDtypeStruct((n, LANES), I32),
            jax.ShapeDtypeStruct((n, LANES), F32),
        ],
        compiler_params=_cparams(1),
        name="out_router",
    )(mix, xp, xs, w_out, g_ffn, wr_hi, wr_lo, b_r)


_F_VALID, _F_FIRST, _F_NEXT, _F_GROUP0 = 1, 2, 4, 8


_P_E, _P_W, _P_N, _P_B, _P_BI, _P_NE, _P_NW, _P_FL, _P_SUBS = range(9)
MOE_SUB = 128


def _stream_weights(t, plan_ref, copies, cast):
    flags = plan_ref[_P_FL, t]

    @pl.when((flags & _F_FIRST) != 0)
    def _():
        cur = copies(plan_ref[_P_E, t], plan_ref[_P_W, t])

        @pl.when((flags & _F_GROUP0) != 0)
        def _():
            for c in cur:
                c.start()

        for c in cur:
            c.wait()
        cast()

        @pl.when((flags & _F_NEXT) != 0)
        def _():
            for c in copies(plan_ref[_P_NE, t], plan_ref[_P_NW, t]):
                c.start()


def _for_used_rows(valid, subs, rows_body):
    for n_sub in range(1, MOE_BLK // MOE_SUB + 1):
        @pl.when(jnp.logical_and(valid, subs == n_sub))
        def _(m=n_sub * MOE_SUB):
            rows_body(m)


def _moe_up_kernel(plan_ref, prev_ref, x_ref, w_hbm, bg_ref, bu_ref, o_ref, wbuf_ref, wgb_ref, wub_ref, sem_ref):
    del prev_ref
    t = pl.program_id(0)

    def copies(e, w):
        col = pl.multiple_of(w * _UP_TN, _UP_TN)
        return (pltpu.make_async_copy(w_hbm.at[e, :, pl.ds(col, _UP_TN)], wbuf_ref.at[0], sem_ref.at[0]),
                pltpu.make_async_copy(w_hbm.at[e, :, pl.ds(col + D_FF, _UP_TN)], wbuf_ref.at[1], sem_ref.at[1]))

    def cast():
        wgb_ref[...] = wbuf_ref[0].astype(BF16)
        wub_ref[...] = wbuf_ref[1].astype(BF16)

    _stream_weights(t, plan_ref, copies, cast)
    valid = (plan_ref[_P_FL, t] & _F_VALID) != 0

    def rows_body(m):
        xa, xb = _unpack_bf16_pair(x_ref[:m, :])
        g = _dot(xa, wgb_ref[:_HALF, :]) + _dot(xb, wgb_ref[_HALF:, :]) + bg_ref[0]
        u = _dot(xa, wub_ref[:_HALF, :]) + _dot(xb, wub_ref[_HALF:, :]) + bu_ref[0]
        g = jnp.minimum(g, SWIGLU_LIMIT)
        u = jnp.clip(u, -SWIGLU_LIMIT, SWIGLU_LIMIT)
        o_ref[:m, :] = ((u + 1.0) * (g * _sigmoid(SWIGLU_ALPHA * g))).astype(BF16)
        if m < MOE_BLK:
            o_ref[m:, :] = jnp.zeros((MOE_BLK - m, o_ref.shape[1]), BF16)

    _for_used_rows(valid, plan_ref[_P_SUBS, t], rows_body)

    @pl.when(jnp.logical_not(valid))
    def _():
        o_ref[...] = jnp.zeros(o_ref.shape, BF16)


_UP_TN = 1024
_UP_TILES = D_FF // _UP_TN
_DN_TN = 2048
_DN_TILES = D_MODEL // _DN_TN
MOE_CHUNKS = 4
_CHUNK_BLKS = MOE_MAX_BLKS // MOE_CHUNKS


def _moe_up(plan, act_prev, xs, w_gu, b_gu, chunk):
    steps = plan.shape[1]
    blk0 = chunk * _CHUNK_BLKS
    bspec = lambda off: pl.BlockSpec((1, 1, _UP_TN), lambda t, p: (p[_P_E, t], 0, p[_P_W, t] + off))
    aliases = {} if act_prev is None else {1: 0}
    prev = jnp.zeros((SUBLANES, LANES), BF16) if act_prev is None else act_prev
    return pl.pallas_call(
        _moe_up_kernel,
        grid_spec=pltpu.PrefetchScalarGridSpec(
            num_scalar_prefetch=1,
            grid=(steps,),
            in_specs=[
                pl.BlockSpec(memory_space=pl.ANY),
                pl.BlockSpec((MOE_BLK, _HALF), lambda t, p: (p[_P_BI, t], 0)),
                pl.BlockSpec(memory_space=pl.ANY),
                bspec(0), bspec(_UP_TILES),
            ],
            out_specs=pl.BlockSpec((MOE_BLK, _UP_TN),
                                   lambda t, p: (blk0 + p[_P_B, t], p[_P_N, t])),
            scratch_shapes=[pltpu.VMEM((2, D_MODEL, _UP_TN), F32),
                            pltpu.VMEM((D_MODEL, _UP_TN), BF16), pltpu.VMEM((D_MODEL, _UP_TN), BF16),
                            pltpu.SemaphoreType.DMA((2,))],
        ),
        out_shape=jax.ShapeDtypeStruct((MOE_ROWS, D_FF), BF16),
        input_output_aliases=aliases,
        compiler_params=_cparams(1),
        name=f"moe_up_{chunk}",
    )(plan, prev, xs, w_gu, b_gu, b_gu)


_DN_HALF = _DN_TN // 2


def _moe_down_kernel(plan_ref, a_ref, w_hbm, b_ref, o_ref, wbuf_ref, wb_ref, sem_ref):
    t = pl.program_id(0)

    def copies(e, w):
        col = pl.multiple_of(w * _DN_TN, _DN_TN)
        return (pltpu.make_async_copy(w_hbm.at[e, :, pl.ds(col, _DN_TN)], wbuf_ref, sem_ref.at[0]),)

    def cast():
        wb_ref[...] = wbuf_ref[...].astype(BF16)

    _stream_weights(t, plan_ref, copies, cast)
    valid = (plan_ref[_P_FL, t] & _F_VALID) != 0

    def rows_body(m):
        y = _dot(a_ref[:m, :], wb_ref[...]) + b_ref[0]
        o_ref[:m, :] = _pack_bf16_pair(y[:, :_DN_HALF], y[:, _DN_HALF:])
        if m < MOE_BLK:
            o_ref[m:, :] = jnp.zeros((MOE_BLK - m, o_ref.shape[1]), F32)

    _for_used_rows(valid, plan_ref[_P_SUBS, t], rows_body)

    @pl.when(jnp.logical_not(valid))
    def _():
        o_ref[...] = jnp.zeros(o_ref.shape, F32)


def _moe_down(plan, act, w_dn, b_dn):
    steps = plan.shape[1]
    return pl.pallas_call(
        _moe_down_kernel,
        grid_spec=pltpu.PrefetchScalarGridSpec(
            num_scalar_prefetch=1,
            grid=(steps,),
            in_specs=[
                pl.BlockSpec((MOE_BLK, D_FF), lambda t, p: (p[_P_BI, t], 0)),
                pl.BlockSpec(memory_space=pl.ANY),
                pl.BlockSpec((1, 1, _DN_TN), lambda t, p: (p[_P_E, t], 0, p[_P_W, t])),
            ],
            out_specs=pl.BlockSpec((MOE_BLK, _DN_HALF), lambda t, p: (p[_P_B, t], p[_P_N, t])),
            scratch_shapes=[pltpu.VMEM((D_FF, _DN_TN), F32), pltpu.VMEM((D_FF, _DN_TN), BF16),
                            pltpu.SemaphoreType.DMA((1,))],
        ),
        out_shape=jax.ShapeDtypeStruct((MOE_ROWS, _HALF), F32),
        compiler_params=_cparams(1),
        name="moe_down",
    )(plan, act, w_dn, b_dn)


def _moe_dispatch(top_idx):
    n_asg = N_TOK * TOP_K
    flat_e = top_idx.reshape(-1)
    onehot = (flat_e[:, None] == jnp.arange(N_EXPERTS, dtype=I32)[None, :]).astype(I32)
    csum = jnp.cumsum(onehot, axis=0)
    counts = csum[-1]
    rank = jnp.sum(csum * onehot, axis=1) - 1
    nblk = (counts + MOE_BLK - 1) // MOE_BLK
    blk_start = jnp.cumsum(nblk) - nblk
    dest = jnp.sum(onehot * blk_start[None, :], axis=1) * MOE_BLK + rank
    pad_src = jnp.arange(MOE_ROWS, dtype=I32) % N_TOK
    row_tok = pad_src.at[dest].set(jnp.arange(n_asg, dtype=I32) // TOP_K,
                                   mode="promise_in_bounds", unique_indices=True)
    return dest, row_tok, counts, nblk, blk_start


def _moe_steps(counts, nblk, blk_start, n_tiles, blk_lo, n_blks):
    t_max = n_tiles * n_blks
    lo = jnp.clip(blk_start, blk_lo, blk_lo + n_blks)
    hi = jnp.clip(blk_start + nblk, blk_lo, blk_lo + n_blks)
    nb_e = hi - lo
    per_e = nb_e * n_tiles
    s_end = jnp.cumsum(per_e)
    total = s_end[-1]
    t = jnp.arange(t_max, dtype=I32)
    tc = jnp.clip(t, 0, jnp.maximum(total - 1, 0))
    e = jnp.minimum(jnp.sum((s_end[None, :] <= tc[:, None]).astype(I32), axis=1), N_EXPERTS - 1)
    sel = (e[:, None] == jnp.arange(N_EXPERTS, dtype=I32)[None, :]).astype(I32)
    pick = lambda v: jnp.sum(sel * v[None, :], axis=1)
    local = tc - pick(s_end - per_e)
    nb = jnp.maximum(pick(nb_e), 1)
    w_tile = jnp.clip(local // nb, 0, n_tiles - 1)
    r = local % nb
    valid = t < total
    first = jnp.logical_and(valid, r == 0)
    fill = t - total
    blk = jnp.where(valid, pick(lo) - blk_lo + r, total // n_tiles + fill // n_tiles)
    rows_used = pick(counts) - (pick(lo) + r - pick(blk_start)) * MOE_BLK
    subs = jnp.clip((rows_used + MOE_SUB - 1) // MOE_SUB, 1, MOE_BLK // MOE_SUB)
    o_tile = jnp.where(valid, w_tile, fill % n_tiles)
    blk = jnp.clip(blk, 0, n_blks - 1)
    blk_in = jnp.where(valid, blk, jnp.maximum(total // n_tiles - 1, 0))
    ids = jnp.arange(N_EXPERTS, dtype=I32)
    owners = jnp.where(nb_e > 0, ids, N_EXPERTS)
    later = jnp.flip(lax.cummin(jnp.flip(owners)))
    next_owner = pick(jnp.concatenate([later[1:], jnp.full((1,), N_EXPERTS, I32)]))
    last_tile = w_tile == n_tiles - 1
    next_e = jnp.where(last_tile, next_owner, e)
    next_w = jnp.where(last_tile, 0, w_tile + 1)
    has_next = jnp.logical_and(first, next_e < N_EXPERTS)
    group = jnp.cumsum(first.astype(I32)) - 1
    flags = (valid * _F_VALID + first * _F_FIRST + has_next * _F_NEXT
             + jnp.logical_and(first, group == 0) * _F_GROUP0)
    rows = {_P_E: e, _P_W: w_tile, _P_N: o_tile, _P_B: blk, _P_BI: blk_in,
            _P_NE: jnp.minimum(next_e, N_EXPERTS - 1), _P_NW: next_w, _P_FL: flags, _P_SUBS: subs}
    return jnp.stack([rows[k].astype(I32) for k in range(len(rows))])


def _moe_plans(counts, nblk, blk_start, n_tiles, n_chunks, n_blks):
    los = jnp.arange(n_chunks, dtype=I32) * n_blks
    return jax.vmap(lambda lo: _moe_steps(counts, nblk, blk_start, n_tiles, lo, n_blks))(los)


_FIN_TM = 256
_FIN_TN = 512
FIN_CHUNKS = 4


def _unpack_expert_rows(words):
    u = lax.bitcast_convert_type(words, U32)
    hi = lax.bitcast_convert_type(u & jnp.uint32(0xFFFF0000), F32)
    lo = lax.bitcast_convert_type(u << 16, F32)
    parts = []
    for n in range(_DN_TILES):
        cols = slice(n * _DN_HALF, (n + 1) * _DN_HALF)
        parts += [hi[:, cols], lo[:, cols]]
    return jnp.concatenate(parts, axis=1)


def _final_kernel(prev_ref, x1_ref, y0_ref, y1_ref, y2_ref, y3_ref, gate_ref, g_ref, wg_ref, p_ref, wp_ref,
                  o_ref, x2_ref):
    del prev_ref
    gate = gate_ref[...]
    moe = (_unpack_expert_rows(y0_ref[0]) * gate[:, 0:1] + _unpack_expert_rows(y1_ref[0]) * gate[:, 1:2]
           + _unpack_expert_rows(y2_ref[0]) * gate[:, 2:3] + _unpack_expert_rows(y3_ref[0]) * gate[:, 3:4])
    x2 = x1_ref[...] + moe
    x2_ref[...] = x2
    hp = (x2 * lax.rsqrt(jnp.mean(x2 * x2, axis=-1, keepdims=True) + EPS) * g_ref[...]).astype(BF16)
    pb = p_ref[...].astype(BF16)
    for c in range(0, D_MODEL, _FIN_TN):
        cols = slice(c, c + _FIN_TN)
        emb = _dot(pb, wp_ref[:, cols])
        o_ref[:, cols] = x2_ref[:, cols] + _sigmoid(_dot(hp, wg_ref[:, cols])) * emb


def _final(out_prev, x1, y4, gate, g_ple, w_ple_gate, p, w_ple, tok0, out0, n, n_out, name):
    t0 = tok0 // _FIN_TM
    o0 = out0 // _FIN_TM
    pt0 = out0 // _FIN_TM
    const = lambda i: (0, 0)
    yspec = lambda k: pl.BlockSpec((1, _FIN_TM, _HALF), lambda i: (k, i, 0))
    once = pl.Buffered(1)
    aliases = {} if out_prev is None else {0: 0}
    prev = jnp.zeros((SUBLANES, LANES), F32) if out_prev is None else out_prev
    return pl.pallas_call(
        _final_kernel,
        grid=(n // _FIN_TM,),
        in_specs=[
            pl.BlockSpec(memory_space=pl.ANY),
            pl.BlockSpec((_FIN_TM, D_MODEL), lambda i: (t0 + i, 0)),
            yspec(0), yspec(1), yspec(2), yspec(3),
            pl.BlockSpec((_FIN_TM, LANES), lambda i: (t0 + i, 0)),
            pl.BlockSpec((1, D_MODEL), const),
            pl.BlockSpec((D_MODEL, D_MODEL), const, pipeline_mode=once),
            pl.BlockSpec((_FIN_TM, PLE_DIM), lambda i: (pt0 + i, 0)),
            pl.BlockSpec((PLE_DIM, D_MODEL), const, pipeline_mode=once),
        ],
        out_specs=pl.BlockSpec((_FIN_TM, D_MODEL), lambda i: (o0 + i, 0)),
        out_shape=jax.ShapeDtypeStruct((n_out, D_MODEL), F32),
        scratch_shapes=[pltpu.VMEM((_FIN_TM, D_MODEL), F32)],
        input_output_aliases=aliases,
        compiler_params=_cparams(1),
        name=name,
    )(prev, x1, y4, y4, y4, y4, gate, g_ple, w_ple_gate, p, w_ple)


def _rope_layout(x):
    half = ROPE_DIM // 2
    z = jnp.zeros(x.shape[:-1] + (half,), x.dtype)
    return jnp.concatenate([x[..., :half], z, x[..., half:], z], axis=-1)


def _rope_tables():
    half = ROPE_DIM // 2
    inv_freq = ROPE_THETA ** (-jnp.arange(half, dtype=F32) / half)
    pos = jnp.arange(PAST_LEN + DEC_SEQ, dtype=I32)
    ang = pos.astype(F32)[:, None] * inv_freq[None, :]
    cos, sin = jnp.cos(ang), jnp.sin(ang)
    z = jnp.zeros_like(cos)
    c = jnp.concatenate([cos, z, cos, z], axis=-1)
    s = jnp.concatenate([-sin, z, sin, z], axis=-1)
    rep = ATT_TM // DEC_SEQ
    return (jnp.concatenate([c[:SEQ], jnp.tile(c[PAST_LEN:], (rep, 1))], axis=0),
            jnp.concatenate([s[:SEQ], jnp.tile(s[PAST_LEN:], (rep, 1))], axis=0))


def _layer(xp, xs, p_prompt, p_sample, cache_kv, cache_kr, state_conv,
           g_mix, w_in, b_gate, w_dw, b_dw, g_cn, b_cn, w_conv_out,
           g_qa, g_kva, w_qb, w_kb, w_vb, g_qn, g_kn, w_o, w_out,
           g_ffn, w_router, b_router, w_gu, b_gu, w_dn, b_dn,
           g_ple, w_ple_gate, w_ple):
    assert SEQ == PAST_LEN
    row = lambda v: v.reshape(1, -1)
    w_in_b = w_in.astype(BF16)
    w_mid = jnp.concatenate([w_in_b[:, O_U:O_KV], _rope_layout(w_in_b[:, O_KV:O_KR])], axis=1)
    w_gate = w_in_b[:, O_KR:]

    h, q_lat, kv_p, kv_s, kr_pad = _in_mid(xp, xs, row(g_mix), w_mid, row(g_qa), row(g_kva))
    half = ROPE_DIM // 2
    kr_new = jnp.concatenate([kr_pad[:, :half], kr_pad[:, 2 * half:3 * half]], axis=1)
    glu = _in_glu(h, w_in_b)

    hist_s = jnp.pad(state_conv, ((0, 0), (HALO - (CONV_WIDTH - 1), 0), (0, 0)))
    c_act = _conv_module(glu, hist_s, w_dw, row(b_dw), row(g_cn), row(b_cn))

    cos_t, sin_t = _rope_tables()
    w_q = jnp.concatenate([w_qb[..., :NOPE_DIM], _rope_layout(w_qb[..., NOPE_DIM:])], axis=-1)
    w_q = w_q.reshape(Q_LORA_RANK, N_HEADS * HEAD_PAD).astype(BF16)
    g_q = jnp.concatenate([g_qn[:NOPE_DIM] * g_kn[:NOPE_DIM], _rope_layout(g_qn[NOPE_DIM:])]).reshape(1, HEAD_PAD)
    q = _q_heads(q_lat, w_q, g_q, cos_t, sin_t)

    w_kv = jnp.concatenate([w_kb, w_vb], axis=-1).reshape(KV_LORA_RANK, N_HEADS * HEAD_PAD).astype(BF16)
    g_kn_rope = _rope_layout(g_kn[NOPE_DIM:]).reshape(1, LANES)
    k_new, v_new = _kv_heads(kv_p, kv_s, kr_pad, w_kv, g_kn_rope, cos_t, sin_t, _tab_idx_new, "kv_heads_new")
    attn = _flash_prompt(q, k_new, v_new)
    attn = _flash_sample(attn, q, cache_kv.reshape(DEC_BATCH * PAST_LEN, KV_LORA_RANK),
                         _rope_layout(cache_kr).reshape(DEC_BATCH * PAST_LEN, LANES),
                         w_kv, g_kn_rope, cos_t, sin_t, k_new, v_new)

    mix = _merge(h, c_act, attn, w_gate, row(b_gate), w_conv_out.astype(BF16), w_o.astype(BF16))

    wr = jnp.pad(w_router, ((0, 0), (0, LANES - N_EXPERTS)))
    wr_hi, wr_lo = _split_bf16(wr)
    b_r = jnp.concatenate([b_router, jnp.full((LANES - N_EXPERTS,), -jnp.inf, F32)]).reshape(1, LANES)
    x1, hm, idx_pad, gate_pad = _out_router(mix, xp, xs, w_out.astype(BF16), row(g_ffn), wr_hi, wr_lo, b_r)
    hm = hm.reshape(2 * N_TOK, _HALF)

    top_idx = idx_pad[:, :TOP_K]
    dest, row_tok, counts, nblk, blk_start = _moe_dispatch(top_idx)
    b_gu3 = b_gu.reshape(N_EXPERTS, 1, 2 * D_FF)
    chunk_rows = _CHUNK_BLKS * MOE_BLK
    up_plans = _moe_plans(counts, nblk, blk_start, _UP_TILES, MOE_CHUNKS, _CHUNK_BLKS)
    down_plan = _moe_plans(counts, nblk, blk_start, _DN_TILES, 1, MOE_MAX_BLKS)[0]
    act = None
    for c in range(MOE_CHUNKS):
        xs = hm.at[row_tok[c * chunk_rows:(c + 1) * chunk_rows]].get(mode="promise_in_bounds")
        act = _moe_up(up_plans[c], act, xs, w_gu, b_gu3, c)
    ys = _moe_down(down_plan, act, w_dn, b_dn.reshape(N_EXPERTS, 1, D_MODEL))

    dest_t = dest.reshape(N_TOK, TOP_K).T
    fin = (row(g_ple), w_ple_gate.astype(BF16))
    w_ple_b = w_ple.astype(BF16)
    n_c = N_P // FIN_CHUNKS
    out_p = None
    for c in range(FIN_CHUNKS):
        y4 = ys.at[dest_t[:, c * n_c:(c + 1) * n_c]].get(mode="promise_in_bounds")
        out_p = _final(out_p, x1, y4, gate_pad, *fin, p_prompt, w_ple_b, c * n_c, c * n_c, n_c, N_P,
                       f"final_prompt_{c}")
    y4 = ys.at[dest_t[:, N_P:]].get(mode="promise_in_bounds")
    out_s = _final(None, x1, y4, gate_pad, *fin, p_sample, w_ple_b, N_P, 0, N_S, N_S, "final_sample")
    return out_p, out_s, kv_p, kv_s, kr_new, glu


def kernel(x_prompt, x_sample, cache_kv_latent, cache_k_rope, state_conv, p_prompt, p_sample, g_mix, w_in, b_gate, w_dw, b_dw, g_cn, b_cn, w_conv_out, g_qa, g_kva, w_qb, w_kb, w_vb, g_qn, g_kn, w_o, w_out, g_ffn, w_router, b_router, w_gu, b_gu, w_dn, b_dn, g_ple, w_ple_gate, w_ple):
    assert g_mix.shape[0] == 1
    out_p, out_s, kv_p, kv_s, kr_new, glu = _layer(
        x_prompt.reshape(N_P, D_MODEL), x_sample.reshape(N_S, D_MODEL),
        p_prompt[0].reshape(N_P, PLE_DIM), p_sample[0].reshape(N_S, PLE_DIM),
        cache_kv_latent[0], cache_k_rope[0], state_conv[0],
        g_mix[0], w_in[0], b_gate[0], w_dw[0], b_dw[0], g_cn[0], b_cn[0], w_conv_out[0],
        g_qa[0], g_kva[0], w_qb[0], w_kb[0], w_vb[0], g_qn[0], g_kn[0], w_o[0], w_out[0],
        g_ffn[0], w_router[0], b_router[0], w_gu[0], b_gu[0], w_dn[0], b_dn[0],
        g_ple[0], w_ple_gate[0], w_ple[0])
    tail = CONV_WIDTH - 1
    conv_p = jnp.stack([glu[(b + 1) * SEQ - tail:(b + 1) * SEQ] for b in range(BATCH)])
    conv_s = glu[N_P:].reshape(DEC_BATCH, DEC_SEQ, CONV_CHANNELS)[:, DEC_SEQ - tail:]
    return (out_p.reshape(BATCH, SEQ, D_MODEL),
            out_s.reshape(DEC_BATCH, DEC_SEQ, D_MODEL),
            kv_p.reshape(1, BATCH, SEQ, KV_LORA_RANK),
            kr_new[:N_P].reshape(1, BATCH, SEQ, ROPE_DIM),
            conv_p[None],
            kv_s.reshape(1, DEC_BATCH, DEC_SEQ, KV_LORA_RANK),
            kr_new[N_P:].reshape(1, DEC_BATCH, DEC_SEQ, ROPE_DIM),
            conv_s[None])
```

```python
import functools
import math

import jax
import jax.numpy as jnp
from jax import lax
from jax.experimental import pallas as pl
from jax.experimental.pallas import tpu as pltpu

F32 = jnp.float32
BF16 = jnp.bfloat16
I32 = jnp.int32
U32 = jnp.uint32

D_MODEL = 2048
BATCH = 2
SEQ = 4096
DEC_BATCH = 8
DEC_SEQ = 64
PAST_LEN = 4096
CHUNK = 64
CONV_CHANNELS = D_MODEL
CONV_WIDTH = 31
N_HEADS = 16
Q_LORA_RANK = 512
KV_LORA_RANK = 512
NOPE_DIM = 128
ROPE_DIM = 64
QK_DIM = NOPE_DIM + ROPE_DIM
V_DIM = 128
ROPE_THETA = 10000.0
N_EXPERTS = 32
TOP_K = 4
D_FF = D_MODEL
SWIGLU_ALPHA = 1.702
SWIGLU_LIMIT = 7.0
PLE_DIM = 256
EPS = 1e-6
NEG_INF = -1e30

N_P = BATCH * SEQ
N_S = DEC_BATCH * DEC_SEQ
N_TOK = N_P + N_S
O_U = 2 * CONV_CHANNELS
O_Q = O_U + Q_LORA_RANK
O_KV = O_Q + KV_LORA_RANK
O_KR = O_KV + ROPE_DIM
LANES = 128
SUBLANES = 8
MID_W = Q_LORA_RANK + KV_LORA_RANK + LANES
HEAD_PAD = NOPE_DIM + LANES

TM = 512
CONV_T = 64
HALO = 32
MOE_BLK = 512
MOE_MAX_BLKS = (N_TOK * TOP_K) // MOE_BLK + N_EXPERTS
MOE_ROWS = MOE_MAX_BLKS * MOE_BLK
VMEM_LIMIT = 48 * 1024 * 1024
assert 2 * ROPE_DIM == LANES and NOPE_DIM == LANES and V_DIM == LANES and SEQ == PAST_LEN


def _cparams(n_axes):
    return pltpu.CompilerParams(dimension_semantics=("arbitrary",) * n_axes,
                                vmem_limit_bytes=VMEM_LIMIT)


def _sigmoid(x):
    return 1.0 / (1.0 + jnp.exp(-x))


def _dot(a, b):
    return jnp.dot(a, b, preferred_element_type=F32)


def _stacked_rows(i, n_prompt_tiles, xp_ref, xs_ref):
    return jnp.where(i < n_prompt_tiles, xp_ref[...], xs_ref[...])


def _in_mid_kernel(xp_ref, xs_ref, g_ref, w_ref, gqa_ref, gkva_ref, h_ref, q_ref, kvp_ref, kvs_ref, kr_ref):
    i = pl.program_id(0)
    x = _stacked_rows(i, N_P // TM, xp_ref, xs_ref)
    h = x * lax.rsqrt(jnp.mean(x * x, axis=-1, keepdims=True) + EPS) * g_ref[...]
    hb = h.astype(BF16)
    h_ref[...] = hb
    z = _dot(hb, w_ref[...])
    ql = z[:, :Q_LORA_RANK]
    kvl = z[:, Q_LORA_RANK:Q_LORA_RANK + KV_LORA_RANK]
    qn = ql * lax.rsqrt(jnp.mean(ql * ql, axis=-1, keepdims=True) + EPS) * gqa_ref[...]
    q_ref[...] = qn.astype(BF16)
    kv = kvl * lax.rsqrt(jnp.mean(kvl * kvl, axis=-1, keepdims=True) + EPS) * gkva_ref[...]
    kr_ref[...] = z[:, Q_LORA_RANK + KV_LORA_RANK:]

    @pl.when(i < N_P // TM)
    def _():
        kvp_ref[...] = kv

    @pl.when(i >= N_P // TM)
    def _():
        kvs_ref[...] = kv


def _in_mid(xp, xs, g_mix, w_mid, g_qa, g_kva):
    n = N_TOK
    npt = N_P // TM
    return pl.pallas_call(
        _in_mid_kernel,
        grid=(n // TM,),
        in_specs=[
            pl.BlockSpec((TM, D_MODEL), lambda i: (jnp.minimum(i, npt - 1), 0)),
            pl.BlockSpec((TM, D_MODEL), lambda i: (jnp.maximum(i - npt, 0), 0)),
            pl.BlockSpec((1, D_MODEL), lambda i: (0, 0)),
            pl.BlockSpec((D_MODEL, MID_W), lambda i: (0, 0)),
            pl.BlockSpec((1, Q_LORA_RANK), lambda i: (0, 0)),
            pl.BlockSpec((1, KV_LORA_RANK), lambda i: (0, 0)),
        ],
        out_specs=[
            pl.BlockSpec((TM, D_MODEL), lambda i: (i, 0)),
            pl.BlockSpec((TM, Q_LORA_RANK), lambda i: (i, 0)),
            pl.BlockSpec((TM, KV_LORA_RANK), lambda i: (jnp.minimum(i, npt - 1), 0)),
            pl.BlockSpec((TM, KV_LORA_RANK), lambda i: (jnp.maximum(i - npt, 0), 0)),
            pl.BlockSpec((TM, LANES), lambda i: (i, 0)),
        ],
        out_shape=[
            jax.ShapeDtypeStruct((n, D_MODEL), BF16),
            jax.ShapeDtypeStruct((n, Q_LORA_RANK), BF16),
            jax.ShapeDtypeStruct((N_P, KV_LORA_RANK), F32),
            jax.ShapeDtypeStruct((N_S, KV_LORA_RANK), F32),
            jax.ShapeDtypeStruct((n, LANES), F32),
        ],
        compiler_params=_cparams(1),
        name="in_mid",
    )(xp, xs, g_mix, w_mid, g_qa, g_kva)


def _glu_kernel(h_ref, w1_ref, w2_ref, o_ref):
    h = h_ref[...]
    o_ref[...] = _dot(h, w1_ref[...]) * _sigmoid(_dot(h, w2_ref[...]))


def _in_glu(h, w_in_b):
    n = h.shape[0]
    tn = 1024
    nj = CONV_CHANNELS // tn
    return pl.pallas_call(
        _glu_kernel,
        grid=(n // TM, nj),
        in_specs=[
            pl.BlockSpec((TM, D_MODEL), lambda i, j: (i, 0)),
            pl.BlockSpec((D_MODEL, tn), lambda i, j: (0, j)),
            pl.BlockSpec((D_MODEL, tn), lambda i, j: (0, j + nj)),
        ],
        out_specs=pl.BlockSpec((TM, tn), lambda i, j: (i, j)),
        out_shape=jax.ShapeDtypeStruct((n, CONV_CHANNELS), F32),
        compiler_params=_cparams(2),
        name="in_glu",
    )(h, w_in_b, w_in_b)


CONV_TM = 256
_CONV_SUBS = CONV_TM // CONV_T
_CONV_SEQ_TILES = SEQ // CONV_TM
_CONV_PROMPT_TILES = N_P // CONV_TM
_CONV_LANES = 512
_SHIFT_ROWS = (HALO // SUBLANES - 1) * SUBLANES + CONV_T


def _conv_kernel(cur_ref, prev_ref, hist_ref, w_ref, bdw_ref, g_ref, b_ref, o_ref, win_ref, conv_ref, shift_ref):
    i = pl.program_id(0)
    is_sample = i >= _CONV_PROMPT_TILES
    opens = i % _CONV_SEQ_TILES == 0
    base = HALO - (CONV_WIDTH - 1)
    for j in range(_CONV_SUBS):
        r0 = j * CONV_T
        before = jnp.where(opens, 0.0, prev_ref[...]) if j == 0 else cur_ref[r0 - HALO:r0, :]
        win_ref[0:HALO, :] = jnp.where(is_sample, hist_ref[j], before)
        win_ref[HALO:HALO + CONV_T, :] = cur_ref[r0:r0 + CONV_T, :]
        for r in range(1, SUBLANES):
            shift_ref[r - 1] = win_ref[r:r + _SHIFT_ROWS, :]
        for c in range(0, CONV_CHANNELS, _CONV_LANES):
            acc = jnp.zeros((CONV_T, _CONV_LANES), F32)
            for k in range(CONV_WIDTH):
                q, r = divmod(base + k, SUBLANES)
                lanes = slice(c, c + _CONV_LANES)
                rows = slice(q * SUBLANES, q * SUBLANES + CONV_T)
                src = win_ref[rows, lanes] if r == 0 else shift_ref[r - 1, rows, lanes]
                acc = acc + w_ref[k:k + 1, lanes] * src
            conv_ref[:, c:c + _CONV_LANES] = acc + bdw_ref[:, c:c + _CONV_LANES]
        y = conv_ref[...]
        yc = y - jnp.mean(y, axis=-1, keepdims=True)
        var = jnp.mean(yc * yc, axis=-1, keepdims=True)
        z = yc * lax.rsqrt(var + EPS) * g_ref[...] + b_ref[...]
        o_ref[r0:r0 + CONV_T, :] = (z * _sigmoid(z)).astype(BF16)


def _conv_module(glu, hist_s, w_dw, b_dw, g_cn, b_cn):
    n = glu.shape[0]
    halo_per_tile = CONV_TM // HALO
    n_sample_tiles = N_S // CONV_TM
    const = lambda i: (0, 0)
    return pl.pallas_call(
        _conv_kernel,
        grid=(n // CONV_TM,),
        in_specs=[
            pl.BlockSpec((CONV_TM, CONV_CHANNELS), lambda i: (i, 0)),
            pl.BlockSpec((HALO, CONV_CHANNELS), lambda i: (jnp.maximum(i * halo_per_tile - 1, 0), 0)),
            pl.BlockSpec((_CONV_SUBS, HALO, CONV_CHANNELS),
                         lambda i: (jnp.clip(i - _CONV_PROMPT_TILES, 0, n_sample_tiles - 1), 0, 0)),
            pl.BlockSpec((CONV_WIDTH, CONV_CHANNELS), const),
            pl.BlockSpec((1, CONV_CHANNELS), const),
            pl.BlockSpec((1, CONV_CHANNELS), const),
            pl.BlockSpec((1, CONV_CHANNELS), const),
        ],
        out_specs=pl.BlockSpec((CONV_TM, CONV_CHANNELS), lambda i: (i, 0)),
        out_shape=jax.ShapeDtypeStruct((n, CONV_CHANNELS), BF16),
        scratch_shapes=[pltpu.VMEM((HALO + CONV_T, CONV_CHANNELS), F32),
                        pltpu.VMEM((CONV_T, CONV_CHANNELS), F32),
                        pltpu.VMEM((SUBLANES - 1, _SHIFT_ROWS, CONV_CHANNELS), F32)],
        compiler_params=_cparams(1),
        name="conv_module",
    )(glu, glu, hist_s, w_dw, b_dw, g_cn, b_cn)


ATT_TM = 512
_TAB_PROMPT_TILES = N_P // ATT_TM
_TAB_SEQ_TILES = SEQ // ATT_TM


def _tab_idx_new(i):
    return jnp.where(i < _TAB_PROMPT_TILES, i % _TAB_SEQ_TILES, _TAB_SEQ_TILES)


def _rope_pair(u, c, s):
    return u * c + pltpu.roll(u, LANES // 2, 1) * s


_Q_SCALE = math.log2(math.e) / math.sqrt(QK_DIM)


def _q_heads_kernel(ql_ref, w_ref, g_ref, c_ref, s_ref, o_ref):
    ql = ql_ref[...]
    g = g_ref[...]
    c = c_ref[...]
    s = s_ref[...]
    for h in range(N_HEADS):
        qf = _dot(ql, w_ref[:, h * HEAD_PAD:(h + 1) * HEAD_PAD])
        ssq = jnp.sum(qf * qf, axis=-1, keepdims=True)
        qn = qf * (lax.rsqrt(ssq * (1.0 / QK_DIM) + EPS) * _Q_SCALE) * g
        o_ref[h, :, :NOPE_DIM] = qn[:, :NOPE_DIM].astype(BF16)
        o_ref[h, :, NOPE_DIM:] = _rope_pair(qn[:, NOPE_DIM:], c, s).astype(BF16)


def _q_heads(q_lat, w_q, g_q, cos_t, sin_t):
    n = q_lat.shape[0]
    return pl.pallas_call(
        _q_heads_kernel,
        grid=(n // ATT_TM,),
        in_specs=[
            pl.BlockSpec((ATT_TM, Q_LORA_RANK), lambda i: (i, 0)),
            pl.BlockSpec((Q_LORA_RANK, N_HEADS * HEAD_PAD), lambda i: (0, 0)),
            pl.BlockSpec((1, HEAD_PAD), lambda i: (0, 0)),
            pl.BlockSpec((ATT_TM, LANES), lambda i: (_tab_idx_new(i), 0)),
            pl.BlockSpec((ATT_TM, LANES), lambda i: (_tab_idx_new(i), 0)),
        ],
        out_specs=pl.BlockSpec((N_HEADS, ATT_TM, HEAD_PAD), lambda i: (0, i, 0)),
        out_shape=jax.ShapeDtypeStruct((N_HEADS, n, HEAD_PAD), BF16),
        compiler_params=_cparams(1),
        name="q_heads",
    )(q_lat, w_q, g_q, cos_t, sin_t)


def _kv_heads_kernel(kvp_ref, kvs_ref, kr_ref, w_ref, gr_ref, c_ref, s_ref, k_ref, v_ref):
    kv = _stacked_rows(pl.program_id(0), N_P // ATT_TM, kvp_ref, kvs_ref).astype(BF16)
    u = kr_ref[...]
    ssq_r = jnp.sum(u * u, axis=-1, keepdims=True)
    krot = _rope_pair(u * gr_ref[...], c_ref[...], s_ref[...])
    for h in range(N_HEADS):
        z = _dot(kv, w_ref[:, h * HEAD_PAD:(h + 1) * HEAD_PAD])
        kn = z[:, :NOPE_DIM]
        ssq = jnp.sum(kn * kn, axis=-1, keepdims=True) + ssq_r
        scale = lax.rsqrt(ssq * (1.0 / QK_DIM) + EPS)
        k_ref[h, :, :NOPE_DIM] = (kn * scale).astype(BF16)
        k_ref[h, :, NOPE_DIM:] = (krot * scale).astype(BF16)
        v_ref[h] = z[:, NOPE_DIM:].astype(BF16)


def _kv_heads(kv_p, kv_s, kr_pad, w_kv, g_kn_rope, cos_t, sin_t, tab_idx, name):
    n = N_TOK
    npt = N_P // ATT_TM
    return pl.pallas_call(
        _kv_heads_kernel,
        grid=(n // ATT_TM,),
        in_specs=[
            pl.BlockSpec((ATT_TM, KV_LORA_RANK), lambda i: (jnp.minimum(i, npt - 1), 0)),
            pl.BlockSpec((ATT_TM, KV_LORA_RANK), lambda i: (jnp.maximum(i - npt, 0), 0)),
            pl.BlockSpec((ATT_TM, LANES), lambda i: (i, 0)),
            pl.BlockSpec((KV_LORA_RANK, N_HEADS * HEAD_PAD), lambda i: (0, 0)),
            pl.BlockSpec((1, LANES), lambda i: (0, 0)),
            pl.BlockSpec((ATT_TM, LANES), lambda i: (tab_idx(i), 0)),
            pl.BlockSpec((ATT_TM, LANES), lambda i: (tab_idx(i), 0)),
        ],
        out_specs=[
            pl.BlockSpec((N_HEADS, ATT_TM, HEAD_PAD), lambda i: (0, i, 0)),
            pl.BlockSpec((N_HEADS, ATT_TM, V_DIM), lambda i: (0, i, 0)),
        ],
        out_shape=[
            jax.ShapeDtypeStruct((N_HEADS, n, HEAD_PAD), BF16),
            jax.ShapeDtypeStruct((N_HEADS, n, V_DIM), BF16),
        ],
        compiler_params=_cparams(1),
        name=name,
    )(kv_p, kv_s, kr_pad, w_kv, g_kn_rope, cos_t, sin_t)


_TQ = 512
_TKB = 512
_HB = 4
_HBP = 4


def _flash_prompt_kernel(q_ref, k_ref, v_ref, o_ref, m_ref, l_ref, acc_ref):
    qi = pl.program_id(2)
    m_ref[...] = jnp.full(m_ref.shape, NEG_INF, F32)
    l_ref[...] = jnp.zeros(l_ref.shape, F32)
    acc_ref[...] = jnp.zeros(acc_ref.shape, F32)
    nlb = _TKB // LANES

    def step(ki, masked):
        start = pl.multiple_of(ki * _TKB, _TKB)
        scores = [lax.dot_general(q_ref[hh], k_ref[hh, pl.ds(start, _TKB), :], (((1,), (1,)), ((), ())),
                                  preferred_element_type=F32) for hh in range(_HBP)]
        probs = []
        for hh in range(_HBP):
            s = scores[hh]
            if masked:
                rc = lax.broadcasted_iota(I32, (_TQ, _TKB), 0) // CHUNK
                cc = lax.broadcasted_iota(I32, (_TQ, _TKB), 1) // CHUNK
                s = jnp.where(cc <= rc, s, NEG_INF)
            sb = [s[:, c * LANES:(c + 1) * LANES] for c in range(nlb)]
            bm = sb[0]
            for c in range(1, nlb):
                bm = jnp.maximum(bm, sb[c])
            m_prev = m_ref[hh]
            m_new = jnp.maximum(m_prev, jnp.max(bm, axis=-1, keepdims=True))
            alpha = jnp.exp2(m_prev - m_new)
            ps = [jnp.exp2(x - m_new) for x in sb]
            psum = ps[0]
            for c in range(1, nlb):
                psum = psum + ps[c]
            l_ref[hh] = alpha * l_ref[hh] + psum
            m_ref[hh] = m_new
            probs.append((alpha, jnp.concatenate(ps, axis=1).astype(BF16)))
        for hh in range(_HBP):
            alpha, p = probs[hh]
            acc_ref[hh] = alpha * acc_ref[hh] + _dot(p, v_ref[hh, pl.ds(start, _TKB), :])

    def body(kp, carry):
        step(2 * kp, False)
        step(2 * kp + 1, False)
        return carry

    lax.fori_loop(0, qi // 2, body, 0)

    @pl.when(qi % 2 == 1)
    def _():
        step(qi - 1, False)

    step(qi, True)
    for hh in range(_HBP):
        l = jnp.sum(l_ref[hh], axis=-1, keepdims=True)
        o_ref[:, hh * V_DIM:(hh + 1) * V_DIM] = (acc_ref[hh] / l).astype(BF16)


def _flash_prompt(q, k, v):
    nq = SEQ // _TQ
    return pl.pallas_call(
        _flash_prompt_kernel,
        grid=(BATCH, N_HEADS // _HBP, nq),
        in_specs=[
            pl.BlockSpec((_HBP, _TQ, HEAD_PAD), lambda b, h, i: (h, b * nq + i, 0)),
            pl.BlockSpec((_HBP, SEQ, HEAD_PAD), lambda b, h, i: (h, b, 0)),
            pl.BlockSpec((_HBP, SEQ, V_DIM), lambda b, h, i: (h, b, 0)),
        ],
        out_specs=pl.BlockSpec((_TQ, _HBP * V_DIM), lambda b, h, i: (b * nq + i, h)),
        out_shape=jax.ShapeDtypeStruct((N_TOK, N_HEADS * V_DIM), BF16),
        scratch_shapes=[pltpu.VMEM((_HBP, _TQ, LANES), F32), pltpu.VMEM((_HBP, _TQ, LANES), F32),
                        pltpu.VMEM((_HBP, _TQ, V_DIM), F32)],
        compiler_params=_cparams(3),
        name="flash_prompt",
    )(q, k, v)


_KC_ROWS = 512


def _flash_sample_kernel(prev_ref, q_ref, kv_ref, kr_ref, w_ref, gr_ref, c_ref, s_ref, kn_ref, vn_ref,
                         o_ref, kvb_ref, krot_ref, ssqr_ref, k_ref, v_ref):
    del prev_ref

    @pl.when(pl.program_id(1) == 0)
    def _():
        kvb_ref[...] = kv_ref[...].astype(BF16)
        u = kr_ref[...]
        ssqr_ref[...] = jnp.broadcast_to(jnp.sum(u * u, axis=-1, keepdims=True), ssqr_ref.shape)
        krot_ref[...] = _rope_pair(u * gr_ref[...], c_ref[...], s_ref[...])

    nt = (((1,), (1,)), ((), ()))
    for hh in range(_HB):
        w = w_ref[:, hh * HEAD_PAD:(hh + 1) * HEAD_PAD]
        for r in range(0, PAST_LEN, _KC_ROWS):
            rows = slice(r, r + _KC_ROWS)
            z = _dot(kvb_ref[rows, :], w)
            kn = z[:, :NOPE_DIM]
            ssq = jnp.sum(kn * kn, axis=-1, keepdims=True) + ssqr_ref[rows, :]
            scale = lax.rsqrt(ssq * (1.0 / QK_DIM) + EPS)
            k_ref[rows, :NOPE_DIM] = (kn * scale).astype(BF16)
            k_ref[rows, NOPE_DIM:] = (krot_ref[rows, :] * scale).astype(BF16)
            v_ref[rows, :] = z[:, NOPE_DIM:].astype(BF16)
        q = q_ref[hh]
        s1 = lax.dot_general(q, k_ref[...], nt, preferred_element_type=F32)
        s2 = lax.dot_general(q, kn_ref[hh], nt, preferred_element_type=F32)
        m = jnp.maximum(jnp.max(s1, axis=-1, keepdims=True), jnp.max(s2, axis=-1, keepdims=True))
        p1 = jnp.exp2(s1 - m)
        p2 = jnp.exp2(s2 - m)
        l = jnp.sum(p1, axis=-1, keepdims=True) + jnp.sum(p2, axis=-1, keepdims=True)
        o = _dot(p1.astype(BF16), v_ref[...]) + _dot(p2.astype(BF16), vn_ref[hh])
        o_ref[:, hh * V_DIM:(hh + 1) * V_DIM] = (o / l).astype(BF16)


def _flash_sample(attn, q, cache_kv, cache_kr_pad, w_kv, g_kn_rope, cos_t, sin_t, k_new, v_new):
    assert (PAST_LEN + DEC_SEQ - 1) // CHUNK <= PAST_LEN // CHUNK
    blk0 = N_P // DEC_SEQ
    new = lambda b, h: (h, blk0 + b, 0)
    const = lambda b, h: (0, 0)
    once = pl.Buffered(1)
    return pl.pallas_call(
        _flash_sample_kernel,
        grid=(DEC_BATCH, N_HEADS // _HB),
        in_specs=[
            pl.BlockSpec(memory_space=pl.ANY),
            pl.BlockSpec((_HB, DEC_SEQ, HEAD_PAD), new),
            pl.BlockSpec((PAST_LEN, KV_LORA_RANK), lambda b, h: (b, 0)),
            pl.BlockSpec((PAST_LEN, LANES), lambda b, h: (b, 0)),
            pl.BlockSpec((KV_LORA_RANK, _HB * HEAD_PAD), lambda b, h: (0, h)),
            pl.BlockSpec((1, LANES), const),
            pl.BlockSpec((PAST_LEN, LANES), const, pipeline_mode=once),
            pl.BlockSpec((PAST_LEN, LANES), const, pipeline_mode=once),
            pl.BlockSpec((_HB, DEC_SEQ, HEAD_PAD), new),
            pl.BlockSpec((_HB, DEC_SEQ, V_DIM), new),
        ],
        out_specs=pl.BlockSpec((DEC_SEQ, _HB * V_DIM), lambda b, h: (blk0 + b, h)),
        out_shape=jax.ShapeDtypeStruct((N_TOK, N_HEADS * V_DIM), BF16),
        scratch_shapes=[pltpu.VMEM((PAST_LEN, KV_LORA_RANK), BF16),
                        pltpu.VMEM((PAST_LEN, LANES), F32),
                        pltpu.VMEM((PAST_LEN, LANES), F32),
                        pltpu.VMEM((PAST_LEN, HEAD_PAD), BF16),
                        pltpu.VMEM((PAST_LEN, V_DIM), BF16)],
        input_output_aliases={0: 0},
        compiler_params=_cparams(2),
        name="flash_sample",
    )(attn, q, cache_kv, cache_kr_pad, w_kv, g_kn_rope, cos_t, sin_t, k_new, v_new)


def _merge_kernel(h_ref, c_ref, a_ref, wga_ref, wgb_ref, bga_ref, bgb_ref, wc_ref, wo_ref, o_ref):
    h = h_ref[...]
    ga = _sigmoid(_dot(h, wga_ref[...]) + bga_ref[...])
    gb = _sigmoid(_dot(h, wgb_ref[...]) + bgb_ref[...])
    mix = ga * _dot(c_ref[...], wc_ref[...]) + gb * _dot(a_ref[...], wo_ref[...])
    o_ref[...] = mix.astype(BF16)


def _merge(h, c_act, attn, w_gate, b_gate, w_conv_out, w_o):
    n = h.shape[0]
    tn = 512
    nj = D_MODEL // tn
    row = lambda i, j: (i, 0)
    return pl.pallas_call(
        _merge_kernel,
        grid=(n // TM, nj),
        in_specs=[
            pl.BlockSpec((TM, D_MODEL), row),
            pl.BlockSpec((TM, CONV_CHANNELS), row),
            pl.BlockSpec((TM, N_HEADS * V_DIM), row),
            pl.BlockSpec((D_MODEL, tn), lambda i, j: (0, j)),
            pl.BlockSpec((D_MODEL, tn), lambda i, j: (0, j + nj)),
            pl.BlockSpec((1, tn), lambda i, j: (0, j)),
            pl.BlockSpec((1, tn), lambda i, j: (0, j + nj)),
            pl.BlockSpec((CONV_CHANNELS, tn), lambda i, j: (0, j)),
            pl.BlockSpec((N_HEADS * V_DIM, tn), lambda i, j: (0, j)),
        ],
        out_specs=pl.BlockSpec((TM, tn), lambda i, j: (i, j)),
        out_shape=jax.ShapeDtypeStruct((n, D_MODEL), BF16),
        compiler_params=_cparams(2),
        name="merge",
    )(h, c_act, attn, w_gate, w_gate, b_gate, b_gate, w_conv_out, w_o)


def _split_bf16(x):
    hi = x.astype(BF16)
    lo = (x - hi.astype(F32)).astype(BF16)
    return hi, lo


_HALF = D_MODEL // 2


def _pack_bf16_pair(a, b):
    ua = lax.bitcast_convert_type(a.astype(BF16).astype(F32), U32)
    ub = lax.bitcast_convert_type(b.astype(BF16).astype(F32), U32)
    return lax.bitcast_convert_type(ua | (ub >> 16), F32)


def _unpack_bf16_pair(w):
    w = lax.bitcast_convert_type(w, U32)
    a = lax.bitcast_convert_type(w & jnp.uint32(0xFFFF0000), F32).astype(BF16)
    b = lax.bitcast_convert_type(w << 16, F32).astype(BF16)
    return a, b


def _out_router_kernel(n_prompt_tiles, mix_ref, xp_ref, xs_ref, w_ref, g_ref, wrh_ref, wrl_ref, br_ref,
                       x1_ref, hm_ref, idx_ref, gate_ref):
    x = _stacked_rows(pl.program_id(0), n_prompt_tiles, xp_ref, xs_ref)
    x1 = x + _dot(mix_ref[...], w_ref[...])
    x1_ref[...] = x1
    hn = x1 * lax.rsqrt(jnp.mean(x1 * x1, axis=-1, keepdims=True) + EPS) * g_ref[...]
    hm_ref[0] = _pack_bf16_pair(hn[:, :_HALF], hn[:, _HALF:])
    hm_ref[1] = jnp.zeros(hm_ref.shape[1:], F32)
    hh, hl = _split_bf16(hn)
    logits = _dot(hh, wrh_ref[...]) + (_dot(hh, wrl_ref[...]) + _dot(hl, wrh_ref[...])) + br_ref[...]
    lane = lax.broadcasted_iota(I32, logits.shape, 1).astype(F32)
    vals = []
    idx_out = jnp.zeros(logits.shape, F32)
    for k in range(TOP_K):
        m = jnp.max(logits, axis=-1, keepdims=True)
        sel = jnp.min(jnp.where(logits == m, lane, 1e9), axis=-1, keepdims=True)
        vals.append(m)
        idx_out = jnp.where(lane == float(k), sel, idx_out)
        logits = jnp.where(lane == sel, -jnp.inf, logits)
    exps = [jnp.exp(v - vals[0]) for v in vals]
    denom = exps[0] + exps[1] + exps[2] + exps[3]
    gate_out = jnp.zeros(idx_out.shape, F32)
    for k in range(TOP_K):
        gate_out = jnp.where(lane == float(k), exps[k] / denom, gate_out)
    idx_ref[...] = idx_out.astype(I32)
    gate_ref[...] = gate_out


def _out_router(mix, xp, xs, w_out, g_ffn, wr_hi, wr_lo, b_r):
    n = N_TOK
    tm = TM
    n_tiles = n // tm
    npt = N_P // tm
    const = lambda i: (0, 0)
    row = lambda i: (i, 0)
    once = pl.Buffered(1)
    return pl.pallas_call(
        functools.partial(_out_router_kernel, npt),
        grid=(n_tiles,),
        in_specs=[
            pl.BlockSpec((tm, D_MODEL), row),
            pl.BlockSpec((tm, D_MODEL), lambda i: (jnp.minimum(i, npt - 1), 0)),
            pl.BlockSpec((tm, D_MODEL), lambda i: (jnp.clip(i - npt, 0, N_S // tm - 1), 0)),
            pl.BlockSpec((D_MODEL, D_MODEL), const, pipeline_mode=once),
            pl.BlockSpec((1, D_MODEL), const),
            pl.BlockSpec((D_MODEL, LANES), const, pipeline_mode=once),
            pl.BlockSpec((D_MODEL, LANES), const, pipeline_mode=once),
            pl.BlockSpec((1, LANES), const),
        ],
        out_specs=[
            pl.BlockSpec((tm, D_MODEL), row),
            pl.BlockSpec((2, tm, _HALF), lambda i: (0, i, 0)),
            pl.BlockSpec((tm, LANES), row),
            pl.BlockSpec((tm, LANES), row),
        ],
        out_shape=[
            jax.ShapeDtypeStruct((n, D_MODEL), F32),
            jax.ShapeDtypeStruct((2, n, _HALF), F32),
            jax.ShapeDtypeStruct((n, LANES), I32),
            jax.ShapeDtypeStruct((n, LANES), F32),
        ],
        compiler_params=_cparams(1),
        name="out_router",
    )(mix, xp, xs, w_out, g_ffn, wr_hi, wr_lo, b_r)


_F_VALID, _F_FIRST, _F_NEXT, _F_GROUP0 = 1, 2, 4, 8


_P_E, _P_W, _P_N, _P_B, _P_BI, _P_NE, _P_NW, _P_FL, _P_SUBS = range(9)
MOE_SUB = 128
_WEIGHT_DMA_PRIORITY = 1


def _stream_weights(t, plan_ref, copies, cast):
    flags = plan_ref[_P_FL, t]

    @pl.when((flags & _F_FIRST) != 0)
    def _():
        cur = copies(plan_ref[_P_E, t], plan_ref[_P_W, t])

        @pl.when((flags & _F_GROUP0) != 0)
        def _():
            for c in cur:
                c.start(priority=_WEIGHT_DMA_PRIORITY)

        for c in cur:
            c.wait()
        cast()

        @pl.when((flags & _F_NEXT) != 0)
        def _():
            for c in copies(plan_ref[_P_NE, t], plan_ref[_P_NW, t]):
                c.start(priority=_WEIGHT_DMA_PRIORITY)


def _for_used_rows(valid, subs, rows_body):
    for n_sub in range(1, MOE_BLK // MOE_SUB + 1):
        @pl.when(jnp.logical_and(valid, subs == n_sub))
        def _(m=n_sub * MOE_SUB):
            rows_body(m)


def _moe_up_kernel(plan_ref, prev_ref, x_ref, w_hbm, bg_ref, bu_ref, o_ref, wbuf_ref, wgb_ref, wub_ref, sem_ref):
    del prev_ref
    t = pl.program_id(0)

    def copies(e, w):
        col = pl.multiple_of(w * _UP_TN, _UP_TN)
        return (pltpu.make_async_copy(w_hbm.at[e, :, pl.ds(col, _UP_TN)], wbuf_ref.at[0], sem_ref.at[0]),
                pltpu.make_async_copy(w_hbm.at[e, :, pl.ds(col + D_FF, _UP_TN)], wbuf_ref.at[1], sem_ref.at[1]))

    def cast():
        wgb_ref[...] = wbuf_ref[0].astype(BF16)
        wub_ref[...] = wbuf_ref[1].astype(BF16)

    _stream_weights(t, plan_ref, copies, cast)
    valid = (plan_ref[_P_FL, t] & _F_VALID) != 0

    def rows_body(m):
        xa, xb = _unpack_bf16_pair(x_ref[:m, :])
        g = _dot(xa, wgb_ref[:_HALF, :]) + _dot(xb, wgb_ref[_HALF:, :]) + bg_ref[0]
        u = _dot(xa, wub_ref[:_HALF, :]) + _dot(xb, wub_ref[_HALF:, :]) + bu_ref[0]
        g = jnp.minimum(g, SWIGLU_LIMIT)
        u = jnp.clip(u, -SWIGLU_LIMIT, SWIGLU_LIMIT)
        o_ref[:m, :] = ((u + 1.0) * (g * _sigmoid(SWIGLU_ALPHA * g))).astype(BF16)
        if m < MOE_BLK:
            o_ref[m:, :] = jnp.zeros((MOE_BLK - m, o_ref.shape[1]), BF16)

    _for_used_rows(valid, plan_ref[_P_SUBS, t], rows_body)

    @pl.when(jnp.logical_not(valid))
    def _():
        o_ref[...] = jnp.zeros(o_ref.shape, BF16)


_UP_TN = 1024
_UP_TILES = D_FF // _UP_TN
_DN_TN = 2048
_DN_TILES = D_MODEL // _DN_TN
MOE_CHUNKS = 4
_CHUNK_BLKS = MOE_MAX_BLKS // MOE_CHUNKS


def _moe_up(plan, act_prev, xs, w_gu, b_gu, chunk):
    steps = plan.shape[1]
    blk0 = chunk * _CHUNK_BLKS
    bspec = lambda off: pl.BlockSpec((1, 1, _UP_TN), lambda t, p: (p[_P_E, t], 0, p[_P_W, t] + off))
    aliases = {} if act_prev is None else {1: 0}
    prev = jnp.zeros((SUBLANES, LANES), BF16) if act_prev is None else act_prev
    return pl.pallas_call(
        _moe_up_kernel,
        grid_spec=pltpu.PrefetchScalarGridSpec(
            num_scalar_prefetch=1,
            grid=(steps,),
            in_specs=[
                pl.BlockSpec(memory_space=pl.ANY),
                pl.BlockSpec((MOE_BLK, _HALF), lambda t, p: (p[_P_BI, t], 0)),
                pl.BlockSpec(memory_space=pl.ANY),
                bspec(0), bspec(_UP_TILES),
            ],
            out_specs=pl.BlockSpec((MOE_BLK, _UP_TN),
                                   lambda t, p: (blk0 + p[_P_B, t], p[_P_N, t])),
            scratch_shapes=[pltpu.VMEM((2, D_MODEL, _UP_TN), F32),
                            pltpu.VMEM((D_MODEL, _UP_TN), BF16), pltpu.VMEM((D_MODEL, _UP_TN), BF16),
                            pltpu.SemaphoreType.DMA((2,))],
        ),
        out_shape=jax.ShapeDtypeStruct((MOE_ROWS, D_FF), BF16),
        input_output_aliases=aliases,
        compiler_params=_cparams(1),
        name=f"moe_up_{chunk}",
    )(plan, prev, xs, w_gu, b_gu, b_gu)


_DN_HALF = _DN_TN // 2


def _moe_down_kernel(plan_ref, a_ref, w_hbm, b_ref, o_ref, wbuf_ref, wb_ref, sem_ref):
    t = pl.program_id(0)

    def copies(e, w):
        col = pl.multiple_of(w * _DN_TN, _DN_TN)
        return (pltpu.make_async_copy(w_hbm.at[e, :, pl.ds(col, _DN_TN)], wbuf_ref, sem_ref.at[0]),)

    def cast():
        wb_ref[...] = wbuf_ref[...].astype(BF16)

    _stream_weights(t, plan_ref, copies, cast)
    valid = (plan_ref[_P_FL, t] & _F_VALID) != 0

    def rows_body(m):
        y = _dot(a_ref[:m, :], wb_ref[...]) + b_ref[0]
        o_ref[:m, :] = _pack_bf16_pair(y[:, :_DN_HALF], y[:, _DN_HALF:])
        if m < MOE_BLK:
            o_ref[m:, :] = jnp.zeros((MOE_BLK - m, o_ref.shape[1]), F32)

    _for_used_rows(valid, plan_ref[_P_SUBS, t], rows_body)

    @pl.when(jnp.logical_not(valid))
    def _():
        o_ref[...] = jnp.zeros(o_ref.shape, F32)


def _moe_down(plan, act, w_dn, b_dn):
    steps = plan.shape[1]
    return pl.pallas_call(
        _moe_down_kernel,
        grid_spec=pltpu.PrefetchScalarGridSpec(
            num_scalar_prefetch=1,
            grid=(steps,),
            in_specs=[
                pl.BlockSpec((MOE_BLK, D_FF), lambda t, p: (p[_P_BI, t], 0)),
                pl.BlockSpec(memory_space=pl.ANY),
                pl.BlockSpec((1, 1, _DN_TN), lambda t, p: (p[_P_E, t], 0, p[_P_W, t])),
            ],
            out_specs=pl.BlockSpec((MOE_BLK, _DN_HALF), lambda t, p: (p[_P_B, t], p[_P_N, t])),
            scratch_shapes=[pltpu.VMEM((D_FF, _DN_TN), F32), pltpu.VMEM((D_FF, _DN_TN), BF16),
                            pltpu.SemaphoreType.DMA((1,))],
        ),
        out_shape=jax.ShapeDtypeStruct((MOE_ROWS, _HALF), F32),
        compiler_params=_cparams(1),
        name="moe_down",
    )(plan, act, w_dn, b_dn)


def _moe_dispatch(top_idx):
    n_asg = N_TOK * TOP_K
    flat_e = top_idx.reshape(-1)
    onehot = (flat_e[:, None] == jnp.arange(N_EXPERTS, dtype=I32)[None, :]).astype(I32)
    csum = jnp.cumsum(onehot, axis=0)
    counts = csum[-1]
    rank = jnp.sum(csum * onehot, axis=1) - 1
    nblk = (counts + MOE_BLK - 1) // MOE_BLK
    blk_start = jnp.cumsum(nblk) - nblk
    dest = jnp.sum(onehot * blk_start[None, :], axis=1) * MOE_BLK + rank
    pad_src = jnp.arange(MOE_ROWS, dtype=I32) % N_TOK
    row_tok = pad_src.at[dest].set(jnp.arange(n_asg, dtype=I32) // TOP_K,
                                   mode="promise_in_bounds", unique_indices=True)
    return dest, row_tok, counts, nblk, blk_start


def _moe_steps(counts, nblk, blk_start, n_tiles, blk_lo, n_blks):
    t_max = n_tiles * n_blks
    lo = jnp.clip(blk_start, blk_lo, blk_lo + n_blks)
    hi = jnp.clip(blk_start + nblk, blk_lo, blk_lo + n_blks)
    nb_e = hi - lo
    per_e = nb_e * n_tiles
    s_end = jnp.cumsum(per_e)
    total = s_end[-1]
    t = jnp.arange(t_max, dtype=I32)
    tc = jnp.clip(t, 0, jnp.maximum(total - 1, 0))
    e = jnp.minimum(jnp.sum((s_end[None, :] <= tc[:, None]).astype(I32), axis=1), N_EXPERTS - 1)
    sel = (e[:, None] == jnp.arange(N_EXPERTS, dtype=I32)[None, :]).astype(I32)
    pick = lambda v: jnp.sum(sel * v[None, :], axis=1)
    local = tc - pick(s_end - per_e)
    nb = jnp.maximum(pick(nb_e), 1)
    w_tile = jnp.clip(local // nb, 0, n_tiles - 1)
    r = local % nb
    valid = t < total
    first = jnp.logical_and(valid, r == 0)
    fill = t - total
    blk = jnp.where(valid, pick(lo) - blk_lo + r, total // n_tiles + fill // n_tiles)
    rows_used = pick(counts) - (pick(lo) + r - pick(blk_start)) * MOE_BLK
    subs = jnp.clip((rows_used + MOE_SUB - 1) // MOE_SUB, 1, MOE_BLK // MOE_SUB)
    o_tile = jnp.where(valid, w_tile, fill % n_tiles)
    blk = jnp.clip(blk, 0, n_blks - 1)
    blk_in = jnp.where(valid, blk, jnp.maximum(total // n_tiles - 1, 0))
    ids = jnp.arange(N_EXPERTS, dtype=I32)
    owners = jnp.where(nb_e > 0, ids, N_EXPERTS)
    later = jnp.flip(lax.cummin(jnp.flip(owners)))
    next_owner = pick(jnp.concatenate([later[1:], jnp.full((1,), N_EXPERTS, I32)]))
    last_tile = w_tile == n_tiles - 1
    next_e = jnp.where(last_tile, next_owner, e)
    next_w = jnp.where(last_tile, 0, w_tile + 1)
    has_next = jnp.logical_and(first, next_e < N_EXPERTS)
    group = jnp.cumsum(first.astype(I32)) - 1
    flags = (valid * _F_VALID + first * _F_FIRST + has_next * _F_NEXT
             + jnp.logical_and(first, group == 0) * _F_GROUP0)
    rows = {_P_E: e, _P_W: w_tile, _P_N: o_tile, _P_B: blk, _P_BI: blk_in,
            _P_NE: jnp.minimum(next_e, N_EXPERTS - 1), _P_NW: next_w, _P_FL: flags, _P_SUBS: subs}
    return jnp.stack([rows[k].astype(I32) for k in range(len(rows))])


def _moe_plans(counts, nblk, blk_start, n_tiles, n_chunks, n_blks):
    los = jnp.arange(n_chunks, dtype=I32) * n_blks
    return jax.vmap(lambda lo: _moe_steps(counts, nblk, blk_start, n_tiles, lo, n_blks))(los)


_FIN_TM = 256
_FIN_TN = 512
FIN_CHUNKS = 4


def _unpack_expert_rows(words):
    u = lax.bitcast_convert_type(words, U32)
    hi = lax.bitcast_convert_type(u & jnp.uint32(0xFFFF0000), F32)
    lo = lax.bitcast_convert_type(u << 16, F32)
    parts = []
    for n in range(_DN_TILES):
        cols = slice(n * _DN_HALF, (n + 1) * _DN_HALF)
        parts += [hi[:, cols], lo[:, cols]]
    return jnp.concatenate(parts, axis=1)


def _final_kernel(prev_ref, x1_ref, y0_ref, y1_ref, y2_ref, y3_ref, gate_ref, g_ref, wg_ref, p_ref, wp_ref,
                  o_ref, x2_ref):
    del prev_ref
    gate = gate_ref[...]
    moe = (_unpack_expert_rows(y0_ref[0]) * gate[:, 0:1] + _unpack_expert_rows(y1_ref[0]) * gate[:, 1:2]
           + _unpack_expert_rows(y2_ref[0]) * gate[:, 2:3] + _unpack_expert_rows(y3_ref[0]) * gate[:, 3:4])
    x2 = x1_ref[...] + moe
    x2_ref[...] = x2
    hp = (x2 * lax.rsqrt(jnp.mean(x2 * x2, axis=-1, keepdims=True) + EPS) * g_ref[...]).astype(BF16)
    pb = p_ref[...].astype(BF16)
    for c in range(0, D_MODEL, _FIN_TN):
        cols = slice(c, c + _FIN_TN)
        emb = _dot(pb, wp_ref[:, cols])
        o_ref[:, cols] = x2_ref[:, cols] + _sigmoid(_dot(hp, wg_ref[:, cols])) * emb


def _final(out_prev, x1, y4, gate, g_ple, w_ple_gate, p, w_ple, tok0, out0, n, n_out, name):
    t0 = tok0 // _FIN_TM
    o0 = out0 // _FIN_TM
    pt0 = out0 // _FIN_TM
    const = lambda i: (0, 0)
    yspec = lambda k: pl.BlockSpec((1, _FIN_TM, _HALF), lambda i: (k, i, 0))
    once = pl.Buffered(1)
    aliases = {} if out_prev is None else {0: 0}
    prev = jnp.zeros((SUBLANES, LANES), F32) if out_prev is None else out_prev
    return pl.pallas_call(
        _final_kernel,
        grid=(n // _FIN_TM,),
        in_specs=[
            pl.BlockSpec(memory_space=pl.ANY),
            pl.BlockSpec((_FIN_TM, D_MODEL), lambda i: (t0 + i, 0)),
            yspec(0), yspec(1), yspec(2), yspec(3),
            pl.BlockSpec((_FIN_TM, LANES), lambda i: (t0 + i, 0)),
            pl.BlockSpec((1, D_MODEL), const),
            pl.BlockSpec((D_MODEL, D_MODEL), const, pipeline_mode=once),
            pl.BlockSpec((_FIN_TM, PLE_DIM), lambda i: (pt0 + i, 0)),
            pl.BlockSpec((PLE_DIM, D_MODEL), const, pipeline_mode=once),
        ],
        out_specs=pl.BlockSpec((_FIN_TM, D_MODEL), lambda i: (o0 + i, 0)),
        out_shape=jax.ShapeDtypeStruct((n_out, D_MODEL), F32),
        scratch_shapes=[pltpu.VMEM((_FIN_TM, D_MODEL), F32)],
        input_output_aliases=aliases,
        compiler_params=_cparams(1),
        name=name,
    )(prev, x1, y4, y4, y4, y4, gate, g_ple, w_ple_gate, p, w_ple)


def _rope_layout(x):
    half = ROPE_DIM // 2
    z = jnp.zeros(x.shape[:-1] + (half,), x.dtype)
    return jnp.concatenate([x[..., :half], z, x[..., half:], z], axis=-1)


def _rope_tables():
    half = ROPE_DIM // 2
    inv_freq = ROPE_THETA ** (-jnp.arange(half, dtype=F32) / half)
    pos = jnp.arange(PAST_LEN + DEC_SEQ, dtype=I32)
    ang = pos.astype(F32)[:, None] * inv_freq[None, :]
    cos, sin = jnp.cos(ang), jnp.sin(ang)
    z = jnp.zeros_like(cos)
    c = jnp.concatenate([cos, z, cos, z], axis=-1)
    s = jnp.concatenate([-sin, z, sin, z], axis=-1)
    rep = ATT_TM // DEC_SEQ
    return (jnp.concatenate([c[:SEQ], jnp.tile(c[PAST_LEN:], (rep, 1))], axis=0),
            jnp.concatenate([s[:SEQ], jnp.tile(s[PAST_LEN:], (rep, 1))], axis=0))


def _layer(xp, xs, p_prompt, p_sample, cache_kv, cache_kr, state_conv,
           g_mix, w_in, b_gate, w_dw, b_dw, g_cn, b_cn, w_conv_out,
           g_qa, g_kva, w_qb, w_kb, w_vb, g_qn, g_kn, w_o, w_out,
           g_ffn, w_router, b_router, w_gu, b_gu, w_dn, b_dn,
           g_ple, w_ple_gate, w_ple):
    assert SEQ == PAST_LEN
    row = lambda v: v.reshape(1, -1)
    w_in_b = w_in.astype(BF16)
    w_mid = jnp.concatenate([w_in_b[:, O_U:O_KV], _rope_layout(w_in_b[:, O_KV:O_KR])], axis=1)
    w_gate = w_in_b[:, O_KR:]

    h, q_lat, kv_p, kv_s, kr_pad = _in_mid(xp, xs, row(g_mix), w_mid, row(g_qa), row(g_kva))
    half = ROPE_DIM // 2
    kr_new = jnp.concatenate([kr_pad[:, :half], kr_pad[:, 2 * half:3 * half]], axis=1)
    glu = _in_glu(h, w_in_b)

    hist_s = jnp.pad(state_conv, ((0, 0), (HALO - (CONV_WIDTH - 1), 0), (0, 0)))
    c_act = _conv_module(glu, hist_s, w_dw, row(b_dw), row(g_cn), row(b_cn))

    cos_t, sin_t = _rope_tables()
    w_q = jnp.concatenate([w_qb[..., :NOPE_DIM], _rope_layout(w_qb[..., NOPE_DIM:])], axis=-1)
    w_q = w_q.reshape(Q_LORA_RANK, N_HEADS * HEAD_PAD).astype(BF16)
    g_q = jnp.concatenate([g_qn[:NOPE_DIM] * g_kn[:NOPE_DIM], _rope_layout(g_qn[NOPE_DIM:])]).reshape(1, HEAD_PAD)
    q = _q_heads(q_lat, w_q, g_q, cos_t, sin_t)

    w_kv = jnp.concatenate([w_kb, w_vb], axis=-1).reshape(KV_LORA_RANK, N_HEADS * HEAD_PAD).astype(BF16)
    g_kn_rope = _rope_layout(g_kn[NOPE_DIM:]).reshape(1, LANES)
    k_new, v_new = _kv_heads(kv_p, kv_s, kr_pad, w_kv, g_kn_rope, cos_t, sin_t, _tab_idx_new, "kv_heads_new")
    attn = _flash_prompt(q, k_new, v_new)
    attn = _flash_sample(attn, q, cache_kv.reshape(DEC_BATCH * PAST_LEN, KV_LORA_RANK),
                         _rope_layout(cache_kr).reshape(DEC_BATCH * PAST_LEN, LANES),
                         w_kv, g_kn_rope, cos_t, sin_t, k_new, v_new)

    mix = _merge(h, c_act, attn, w_gate, row(b_gate), w_conv_out.astype(BF16), w_o.astype(BF16))

    wr = jnp.pad(w_router, ((0, 0), (0, LANES - N_EXPERTS)))
    wr_hi, wr_lo = _split_bf16(wr)
    b_r = jnp.concatenate([b_router, jnp.full((LANES - N_EXPERTS,), -jnp.inf, F32)]).reshape(1, LANES)
    x1, hm, idx_pad, gate_pad = _out_router(mix, xp, xs, w_out.astype(BF16), row(g_ffn), wr_hi, wr_lo, b_r)
    hm = hm.reshape(2 * N_TOK, _HALF)

    top_idx = idx_pad[:, :TOP_K]
    dest, row_tok, counts, nblk, blk_start = _moe_dispatch(top_idx)
    b_gu3 = b_gu.reshape(N_EXPERTS, 1, 2 * D_FF)
    chunk_rows = _CHUNK_BLKS * MOE_BLK
    up_plans = _moe_plans(counts, nblk, blk_start, _UP_TILES, MOE_CHUNKS, _CHUNK_BLKS)
    down_plan = _moe_plans(counts, nblk, blk_start, _DN_TILES, 1, MOE_MAX_BLKS)[0]
    act = None
    for c in range(MOE_CHUNKS):
        xs = hm.at[row_tok[c * chunk_rows:(c + 1) * chunk_rows]].get(mode="promise_in_bounds")
        act = _moe_up(up_plans[c], act, xs, w_gu, b_gu3, c)
    ys = _moe_down(down_plan, act, w_dn, b_dn.reshape(N_EXPERTS, 1, D_MODEL))

    dest_t = dest.reshape(N_TOK, TOP_K).T
    fin = (row(g_ple), w_ple_gate.astype(BF16))
    w_ple_b = w_ple.astype(BF16)
    n_c = N_P // FIN_CHUNKS
    out_p = None
    for c in range(FIN_CHUNKS):
        y4 = ys.at[dest_t[:, c * n_c:(c + 1) * n_c]].get(mode="promise_in_bounds")
        out_p = _final(out_p, x1, y4, gate_pad, *fin, p_prompt, w_ple_b, c * n_c, c * n_c, n_c, N_P,
                       f"final_prompt_{c}")
    y4 = ys.at[dest_t[:, N_P:]].get(mode="promise_in_bounds")
    out_s = _final(None, x1, y4, gate_pad, *fin, p_sample, w_ple_b, N_P, 0, N_S, N_S, "final_sample")
    return out_p, out_s, kv_p, kv_s, kr_new, glu


def kernel(x_prompt, x_sample, cache_kv_latent, cache_k_rope, state_conv, p_prompt, p_sample, g_mix, w_in, b_gate, w_dw, b_dw, g_cn, b_cn, w_conv_out, g_qa, g_kva, w_qb, w_kb, w_vb, g_qn, g_kn, w_o, w_out, g_ffn, w_router, b_router, w_gu, b_gu, w_dn, b_dn, g_ple, w_ple_gate, w_ple):
    assert g_mix.shape[0] == 1
    out_p, out_s, kv_p, kv_s, kr_new, glu = _layer(
        x_prompt.reshape(N_P, D_MODEL), x_sample.reshape(N_S, D_MODEL),
        p_prompt[0].reshape(N_P, PLE_DIM), p_sample[0].reshape(N_S, PLE_DIM),
        cache_kv_latent[0], cache_k_rope[0], state_conv[0],
        g_mix[0], w_in[0], b_gate[0], w_dw[0], b_dw[0], g_cn[0], b_cn[0], w_conv_out[0],
        g_qa[0], g_kva[0], w_qb[0], w_kb[0], w_vb[0], g_qn[0], g_kn[0], w_o[0], w_out[0],
        g_ffn[0], w_router[0], b_router[0], w_gu[0], b_gu[0], w_dn[0], b_dn[0],
        g_ple[0], w_ple_gate[0], w_ple[0])
    tail = CONV_WIDTH - 1
    conv_p = jnp.stack([glu[(b + 1) * SEQ - tail:(b + 1) * SEQ] for b in range(BATCH)])
    conv_s = glu[N_P:].reshape(DEC_BATCH, DEC_SEQ, CONV_CHANNELS)[:, DEC_SEQ - tail:]
    return (out_p.reshape(BATCH, SEQ, D_MODEL),
            out_s.reshape(DEC_BATCH, DEC_SEQ, D_MODEL),
            kv_p.reshape(1, BATCH, SEQ, KV_LORA_RANK),
            kr_new[:N_P].reshape(1, BATCH, SEQ, ROPE_DIM),
            conv_p[None],
            kv_s.reshape(1, DEC_BATCH, DEC_SEQ, KV_LORA_RANK),
            kr_new[N_P:].reshape(1, DEC_BATCH, DEC_SEQ, ROPE_DIM),
            conv_s[None])
```
